```python
import jax, jax.numpy as jnp
from jax import lax
import numpy as np

D_MODEL = 1024
BATCH = 8
SEQ = 4096
DEPTH = 1
DEC_BATCH = 128
DEC_SEQ = 4
PAST_LEN = 16384
PAGE_SIZE = 128

HEAD_DIM = 64
N_HEADS = D_MODEL // HEAD_DIM
ATT_HEADS = N_HEADS // 2
KV_HEADS = 2
Q_PER_KV = ATT_HEADS // KV_HEADS
GM_HEADS = N_HEADS - ATT_HEADS
ATT_WIDTH = ATT_HEADS * HEAD_DIM
KV_WIDTH = KV_HEADS * HEAD_DIM
GM_WIDTH = GM_HEADS * HEAD_DIM
IN_WIDTH = ATT_WIDTH + 2 * KV_WIDTH + 2 * GM_WIDTH
MIX_WIDTH = ATT_WIDTH + GM_WIDTH
WINDOW = 128
CHUNK = 128
ROPE_THETA = 10000.0
ATT_SCALE = HEAD_DIM ** -0.5
N_EXPERTS = 256
TOP_K = 8
N_GROUPS = 8
TOPK_GROUPS = 4
F_EXPERT = 256
F_SHARED = 256
ROUTED_SCALE = 2.5
EXPERT_BLOCK = 128
LN_EPS = 1e-5
ALPHA = (2.0 * DEPTH) ** 0.25
BETA = (8.0 * DEPTH) ** -0.25

kernel_name = 'hymba_gmlp_swa_sink_moe_deepnorm_step'


def layer_norm(x, g, b):
    xf = x.astype(jnp.float32)
    mu = jnp.mean(xf, axis=-1, keepdims=True)
    var = jnp.mean(jnp.square(xf - mu), axis=-1, keepdims=True)
    return ((xf - mu) * lax.rsqrt(var + LN_EPS) * g.astype(jnp.float32) + b.astype(jnp.float32)).astype(x.dtype)


def rope(x, pos):
    half = HEAD_DIM // 2
    inv = ROPE_THETA ** (-jnp.arange(half, dtype=jnp.float32) * 2.0 / HEAD_DIM)
    ang = pos.astype(jnp.float32)[:, None] * inv[None, :]
    cos = jnp.cos(ang)[:, None, :]
    sin = jnp.sin(ang)[:, None, :]
    xf = x.astype(jnp.float32)
    x1, x2 = xf[..., :half], xf[..., half:]
    return jnp.concatenate([x1 * cos - x2 * sin, x2 * cos + x1 * sin], axis=-1).astype(x.dtype)


def project(x, w_in, gm_ln_g, gm_ln_b, pos):
    z = jnp.einsum('bsd,de->bse', x, w_in)
    q, k, v, gz = jnp.split(z, [ATT_WIDTH, ATT_WIDTH + KV_WIDTH, ATT_WIDTH + 2 * KV_WIDTH], axis=-1)
    gu, gv = jnp.split(jax.nn.gelu(gz, approximate=False), 2, axis=-1)
    gv = layer_norm(gv, gm_ln_g, gm_ln_b)
    b, s = x.shape[:2]
    q = rope(q.reshape(b, s, ATT_HEADS, HEAD_DIM), pos)
    k = rope(k.reshape(b, s, KV_HEADS, HEAD_DIM), pos)
    v = v.reshape(b, s, KV_HEADS, HEAD_DIM)
    return q, k, v, gu, gv


def band_mask(n_q, n_k, offset):
    dist = offset + jnp.arange(n_q)[:, None] - jnp.arange(n_k)[None, :]
    return (dist >= 0) & (dist <= WINDOW)


def sink_softmax(scores, sink):
    s = sink.astype(jnp.float32).reshape(KV_HEADS, Q_PER_KV)[:, :, None, None]
    col = jnp.broadcast_to(s, scores.shape[:-1] + (1,))
    return jax.nn.softmax(jnp.concatenate([scores, col], axis=-1), axis=-1)[..., :-1]


def attn_prompt(q, k, v, sink):
    b, s = q.shape[:2]
    nb = s // WINDOW
    qb = q.reshape(b, nb, WINDOW, KV_HEADS, Q_PER_KV, HEAD_DIM)
    kb = k.reshape(b, nb, WINDOW, KV_HEADS, HEAD_DIM)
    vb = v.reshape(b, nb, WINDOW, KV_HEADS, HEAD_DIM)
    pad = jnp.zeros_like(kb[:, :1])
    kk = jnp.concatenate([jnp.concatenate([pad, kb[:, :-1]], axis=1), kb], axis=2)
    vv = jnp.concatenate([jnp.concatenate([pad, vb[:, :-1]], axis=1), vb], axis=2)
    scores = jnp.einsum('bnqkgd,bnskd->bnkgqs', qb, kk, preferred_element_type=jnp.float32) * ATT_SCALE
    valid_prev = (jnp.arange(nb)[:, None, None] > 0) | (jnp.arange(2 * WINDOW) >= WINDOW)[None, None, :]
    mask = band_mask(WINDOW, 2 * WINDOW, WINDOW)[None] & valid_prev
    scores = jnp.where(mask[None, :, None, None], scores, -jnp.inf)
    p = sink_softmax(scores, sink)
    out = jnp.einsum('bnkgqs,bnskd->bnqkgd', p.astype(v.dtype), vv)
    return out.reshape(b, s, ATT_WIDTH)


def attn_sample(q, k, v, cache_k, cache_v, sink):
    b, t = q.shape[:2]
    r = cache_k.shape[1]
    qg = q.reshape(b, t, KV_HEADS, Q_PER_KV, HEAD_DIM)
    kk = jnp.concatenate([cache_k.astype(k.dtype), k], axis=1)
    vv = jnp.concatenate([cache_v.astype(v.dtype), v], axis=1)
    scores = jnp.einsum('bqkgd,bskd->bkgqs', qg, kk, preferred_element_type=jnp.float32) * ATT_SCALE
    scores = jnp.where(band_mask(t, r + t, r), scores, -jnp.inf)
    p = sink_softmax(scores, sink)
    out = jnp.einsum('bkgqs,bskd->bqkgd', p.astype(v.dtype), vv)
    return out.reshape(b, t, ATT_WIDTH), kk[:, t:], vv[:, t:]


def spatial_gate(gu, gv, w_s, b_s):
    n = gu.shape[2]
    w = jnp.tril(w_s[:, :n, :n])
    sv = jnp.einsum('hij,bnjhc->bnihc', w, gv) + b_s[:, :n].T[:, :, None]
    return gu * sv


def route(h2, router_w, router_bias):
    scores = jax.nn.sigmoid(jnp.einsum('nd,de->ne', h2, router_w, preferred_element_type=jnp.float32))
    biased = scores + router_bias.astype(jnp.float32)
    n = h2.shape[0]
    grp = biased.reshape(n, N_GROUPS, N_EXPERTS // N_GROUPS)
    grp_score = lax.top_k(grp, 2)[0].sum(-1)
    _, gidx = lax.top_k(grp_score, TOPK_GROUPS)
    gmask = jnp.any(gidx[:, :, None] == jnp.arange(N_GROUPS)[None, None, :], axis=1)
    emask = jnp.repeat(gmask, N_EXPERTS // N_GROUPS, axis=-1)
    _, eidx = lax.top_k(jnp.where(emask, biased, -jnp.inf), TOP_K)
    w = jnp.take_along_axis(scores, eidx, axis=-1)
    w = w / jnp.sum(w, axis=-1, keepdims=True) * ROUTED_SCALE
    return eidx, w


def routed_experts(x2, eidx, w, w_gate_e, w_up_e, w_down_e):
    n = x2.shape[0]
    a = n * TOP_K
    n_blocks = -(-(a + N_EXPERTS * (EXPERT_BLOCK - 1)) // EXPERT_BLOCK)
    e_flat = eidx.reshape(-1).astype(jnp.int32)
    tok_flat = jnp.arange(a, dtype=jnp.int32) // TOP_K
    w_flat = w.reshape(-1)
    order = jnp.argsort(e_flat)
    e_sorted = e_flat[order]
    counts = jnp.bincount(e_flat, length=N_EXPERTS).astype(jnp.int32)
    padded = (counts + EXPERT_BLOCK - 1) // EXPERT_BLOCK * EXPERT_BLOCK
    pad_end = jnp.cumsum(padded)
    pad_start = pad_end - padded
    sort_start = jnp.cumsum(counts) - counts
    dest = pad_start[e_sorted] + jnp.arange(a, dtype=jnp.int32) - sort_start[e_sorted]
    rows = n_blocks * EXPERT_BLOCK
    row_tok = jnp.full((rows,), n, jnp.int32).at[dest].set(tok_flat[order])
    row_w = jnp.zeros((rows,), w.dtype).at[dest].set(w_flat[order])
    block_e = jnp.minimum(jnp.searchsorted(pad_end, jnp.arange(n_blocks, dtype=jnp.int32) * EXPERT_BLOCK, side='right'), N_EXPERTS - 1)
    x_ext = jnp.concatenate([x2, jnp.zeros((1, D_MODEL), x2.dtype)], axis=0)

    def block_fn(args):
        tok, e, wr = args
        xb = x_ext[tok]
        hb = jax.nn.silu(xb @ w_gate_e[e]) * (xb @ w_up_e[e])
        return (hb @ w_down_e[e]) * wr[:, None]

    out = lax.map(block_fn, (row_tok.reshape(n_blocks, EXPERT_BLOCK), block_e, row_w.reshape(n_blocks, EXPERT_BLOCK)))
    y = jnp.zeros((n + 1, D_MODEL), out.dtype).at[row_tok].add(out.reshape(rows, D_MODEL))
    return y[:n]


def swiglu(x, wg, wu, wd):
    return (jax.nn.silu(x @ wg) * (x @ wu)) @ wd


def post_layer(x, mix, ln1_g, ln1_b, router_w, router_bias, w_gate_e, w_up_e, w_down_e, w_gate_s, w_up_s, w_down_s, ln2_g, ln2_b):
    h = layer_norm(ALPHA * x + mix, ln1_g, ln1_b)
    lead = h.shape[:-1]
    h2 = h.reshape(-1, D_MODEL)
    eidx, w = route(h2, router_w, router_bias)
    ffn = routed_experts(h2, eidx, w.astype(h2.dtype), w_gate_e, w_up_e, w_down_e) + swiglu(h2, w_gate_s, w_up_s, w_down_s)
    return layer_norm(ALPHA * h + ffn.reshape(lead + (D_MODEL,)).astype(h.dtype), ln2_g, ln2_b)


def setup_inputs(seed: int = 0) -> dict:
    key = jax.random.key(seed)
    ks = jax.random.split(key, 24)
    r = min(WINDOW, PAST_LEN)
    nrm = lambda k, shape, scale: jax.random.normal(k, shape, jnp.float32) * scale
    return {
        'x_prompt': nrm(ks[0], (BATCH, SEQ, D_MODEL), 1.0),
        'x_sample': nrm(ks[1], (DEC_BATCH, DEC_SEQ, D_MODEL), 1.0),
        'cache_k': nrm(ks[2], (DEPTH, DEC_BATCH, r, KV_HEADS, HEAD_DIM), 1.0),
        'cache_v': nrm(ks[3], (DEPTH, DEC_BATCH, r, KV_HEADS, HEAD_DIM), 1.0),
        'w_in': nrm(ks[4], (DEPTH, D_MODEL, IN_WIDTH), D_MODEL ** -0.5),
        'sink': nrm(ks[5], (DEPTH, ATT_HEADS), 0.5),
        'gm_ln_g': 1.0 + nrm(ks[6], (DEPTH, GM_WIDTH), 0.02),
        'gm_ln_b': nrm(ks[7], (DEPTH, GM_WIDTH), 0.02),
        'gm_w_s': nrm(ks[8], (DEPTH, GM_HEADS, CHUNK, CHUNK), CHUNK ** -0.5),
        'gm_b_s': 1.0 + nrm(ks[9], (DEPTH, GM_HEADS, CHUNK), 0.1),
        'w_out': nrm(ks[10], (DEPTH, MIX_WIDTH, D_MODEL), MIX_WIDTH ** -0.5 * BETA),
        'ln1_g': 1.0 + nrm(ks[11], (DEPTH, D_MODEL), 0.02),
        'ln1_b': nrm(ks[12], (DEPTH, D_MODEL), 0.02),
        'router_w': nrm(ks[13], (DEPTH, D_MODEL, N_EXPERTS), D_MODEL ** -0.5),
        'router_bias': nrm(ks[14], (DEPTH, N_EXPERTS), 0.01),
        'w_gate_e': nrm(ks[15], (DEPTH, N_EXPERTS, D_MODEL, F_EXPERT), D_MODEL ** -0.5),
        'w_up_e': nrm(ks[16], (DEPTH, N_EXPERTS, D_MODEL, F_EXPERT), D_MODEL ** -0.5),
        'w_down_e': nrm(ks[17], (DEPTH, N_EXPERTS, F_EXPERT, D_MODEL), F_EXPERT ** -0.5 * BETA),
        'w_gate_s': nrm(ks[18], (DEPTH, D_MODEL, F_SHARED), D_MODEL ** -0.5),
        'w_up_s': nrm(ks[19], (DEPTH, D_MODEL, F_SHARED), D_MODEL ** -0.5),
        'w_down_s': nrm(ks[20], (DEPTH, F_SHARED, D_MODEL), F_SHARED ** -0.5 * BETA),
        'ln2_g': 1.0 + nrm(ks[21], (DEPTH, D_MODEL), 0.02),
        'ln2_b': nrm(ks[22], (DEPTH, D_MODEL), 0.02),
    }


def reference(x_prompt, x_sample, cache_k, cache_v, w_in, sink, gm_ln_g, gm_ln_b, gm_w_s, gm_b_s, w_out, ln1_g, ln1_b, router_w, router_bias, w_gate_e, w_up_e, w_down_e, w_gate_s, w_up_s, w_down_s, ln2_g, ln2_b):
    bp, sp = x_prompt.shape[:2]
    bs, ts = x_sample.shape[:2]
    r = cache_k.shape[2]
    pos_p = jnp.arange(sp, dtype=jnp.int32)
    pos_s = PAST_LEN + jnp.arange(ts, dtype=jnp.int32)
    hp, hs = x_prompt, x_sample
    new_kp, new_vp, new_ks, new_vs, new_gs = [], [], [], [], []
    for l in range(DEPTH):
        ffn = (ln1_g[l], ln1_b[l], router_w[l], router_bias[l], w_gate_e[l], w_up_e[l], w_down_e[l], w_gate_s[l], w_up_s[l], w_down_s[l], ln2_g[l], ln2_b[l])
        q, k, v, gu, gv = project(hp, w_in[l], gm_ln_g[l], gm_ln_b[l], pos_p)
        att = attn_prompt(q, k, v, sink[l])
        gshape = (bp, sp // CHUNK, CHUNK, GM_HEADS, HEAD_DIM)
        gm = spatial_gate(gu.reshape(gshape), gv.reshape(gshape), gm_w_s[l], gm_b_s[l]).reshape(bp, sp, GM_WIDTH)
        mix = jnp.concatenate([att, gm], axis=-1) @ w_out[l]
        new_kp.append(k[:, sp - r:])
        new_vp.append(v[:, sp - r:])
        hp = post_layer(hp, mix, *ffn)
        q, k, v, gu, gv = project(hs, w_in[l], gm_ln_g[l], gm_ln_b[l], pos_s)
        att, k_buf, v_buf = attn_sample(q, k, v, cache_k[l], cache_v[l], sink[l])
        gshape = (bs, 1, ts, GM_HEADS, HEAD_DIM)
        gm = spatial_gate(gu.reshape(gshape), gv.reshape(gshape), gm_w_s[l], gm_b_s[l]).reshape(bs, ts, GM_WIDTH)
        mix = jnp.concatenate([att, gm], axis=-1) @ w_out[l]
        new_ks.append(k_buf)
        new_vs.append(v_buf)
        new_gs.append(gv)
        hs = post_layer(hs, mix, *ffn)
    return (hp, hs, jnp.stack(new_kp), jnp.stack(new_vp), jnp.stack(new_ks), jnp.stack(new_vs), jnp.stack(new_gs))
```

```python
import functools

import jax
import jax.numpy as jnp
import numpy as np
from jax import lax
from jax.experimental import pallas as pl
from jax.experimental.pallas import tpu as pltpu

D_MODEL = 1024
HEAD_DIM = 64
ATT_HEADS = 8
KV_HEADS = 2
Q_PER_KV = ATT_HEADS // KV_HEADS
GM_HEADS = 8
ATT_WIDTH = ATT_HEADS * HEAD_DIM
KV_WIDTH = KV_HEADS * HEAD_DIM
GM_WIDTH = GM_HEADS * HEAD_DIM
ROPE_WIDTH = ATT_WIDTH + KV_WIDTH
IN_WIDTH = ATT_WIDTH + 2 * KV_WIDTH + 2 * GM_WIDTH
WINDOW = 128
CHUNK = 128
PAST_LEN = 16384
ROPE_THETA = 10000.0
ATT_SCALE = HEAD_DIM ** -0.5
N_EXPERTS = 256
TOP_K = 8
N_GROUPS = 8
TOPK_GROUPS = 4
F_EXPERT = 256
ROUTED_SCALE = 2.5
LN_EPS = 1e-5
DEPTH = 1
ALPHA = (2.0 * DEPTH) ** 0.25

LANES = 128
ROW_TILE = 512
EXPERT_ROWS = 256
COMBINE_ROWS = 256
VMEM_LIMIT = 56 * 1024 * 1024

F32 = jnp.float32
BF16 = jnp.bfloat16


def _params(n_axes):
    return pltpu.CompilerParams(dimension_semantics=("arbitrary",) * n_axes, vmem_limit_bytes=VMEM_LIMIT)


def _layer_norm(x, g, b):
    mu = jnp.mean(x, axis=-1, keepdims=True)
    xc = x - mu
    var = jnp.mean(xc * xc, axis=-1, keepdims=True)
    return xc * lax.rsqrt(var + LN_EPS) * g + b


def _gelu(x):
    return 0.5 * x * (1.0 + lax.erf(x * np.float32(np.sqrt(0.5))))


def _in_proj_kernel(x_ref, w_ref, cos_ref, sa_ref, sb_ref, g_ref, b_ref, q_ref, k_ref, v_ref, gu_ref, gv_ref):
    x = x_ref[...].astype(BF16)
    zr = jnp.dot(x, w_ref[:, :ROPE_WIDTH], preferred_element_type=F32)
    pieces = []
    for c in range(ROPE_WIDTH // LANES):
        sl = slice(c * LANES, (c + 1) * LANES)
        zc = zr[:, sl]
        pieces.append(zc * cos_ref[:, sl]
                      + pltpu.roll(zc, LANES - HEAD_DIM // 2, 1) * sa_ref[:, sl]
                      + pltpu.roll(zc, HEAD_DIM // 2, 1) * sb_ref[:, sl])
    for c in range(ATT_WIDTH // LANES):
        q_ref[:, c * LANES:(c + 1) * LANES] = pieces[c].astype(q_ref.dtype)
    k_ref[...] = pieces[ATT_WIDTH // LANES]
    v_ref[...] = jnp.dot(x, w_ref[:, ROPE_WIDTH:ROPE_WIDTH + KV_WIDTH], preferred_element_type=F32)
    g0 = ROPE_WIDTH + KV_WIDTH
    zu = jnp.dot(x, w_ref[:, g0:g0 + GM_WIDTH], preferred_element_type=F32)
    gu_ref[...] = _gelu(zu).astype(gu_ref.dtype)
    zv = jnp.dot(x, w_ref[:, g0 + GM_WIDTH:g0 + 2 * GM_WIDTH], preferred_element_type=F32)
    gv = _layer_norm(_gelu(zv), g_ref[...], b_ref[...])
    gv_ref[...] = gv.astype(gv_ref.dtype)


def _in_proj(x2, w_in_b, tabs, ln_g, ln_b, *, tm, gv_dtype, name):
    n = x2.shape[0]
    cos_t, sa_t, sb_t = tabs
    period = cos_t.shape[0] // tm
    row = lambda i: (i, 0)
    tab = lambda i: (i % period, 0)
    fixed = lambda i: (0, 0)
    return pl.pallas_call(
        _in_proj_kernel,
        grid=(n // tm,),
        in_specs=[
            pl.BlockSpec((tm, D_MODEL), row),
            pl.BlockSpec((D_MODEL, IN_WIDTH), fixed),
            pl.BlockSpec((tm, ROPE_WIDTH), tab),
            pl.BlockSpec((tm, ROPE_WIDTH), tab),
            pl.BlockSpec((tm, ROPE_WIDTH), tab),
            pl.BlockSpec((1, GM_WIDTH), fixed),
            pl.BlockSpec((1, GM_WIDTH), fixed),
        ],
        out_specs=[
            pl.BlockSpec((tm, ATT_WIDTH), row),
            pl.BlockSpec((tm, KV_WIDTH), row),
            pl.BlockSpec((tm, KV_WIDTH), row),
            pl.BlockSpec((tm, GM_WIDTH), row),
            pl.BlockSpec((tm, GM_WIDTH), row),
        ],
        out_shape=[
            jax.ShapeDtypeStruct((n, ATT_WIDTH), BF16),
            jax.ShapeDtypeStruct((n, KV_WIDTH), F32),
            jax.ShapeDtypeStruct((n, KV_WIDTH), F32),
            jax.ShapeDtypeStruct((n, GM_WIDTH), BF16),
            jax.ShapeDtypeStruct((n, GM_WIDTH), gv_dtype),
        ],
        compiler_params=_params(1),
        name=name,
    )(x2, w_in_b, cos_t, sa_t, sb_t, ln_g, ln_b)


def _rope_tables(pos):
    half = HEAD_DIM // 2
    inv = ROPE_THETA ** (-jnp.arange(half, dtype=F32) * 2.0 / HEAD_DIM)
    ang = pos.astype(F32)[:, None] * inv[None, :]
    cos, sin = jnp.cos(ang), jnp.sin(ang)
    zero = jnp.zeros_like(sin)
    heads = ROPE_WIDTH // HEAD_DIM
    scale = jnp.where(jnp.arange(ROPE_WIDTH) < ATT_WIDTH, ATT_SCALE, 1.0).astype(F32)[None, :]
    cos_t = jnp.tile(jnp.concatenate([cos, cos], axis=1), (1, heads)) * scale
    sa_t = jnp.tile(jnp.concatenate([-sin, zero], axis=1), (1, heads)) * scale
    sb_t = jnp.tile(jnp.concatenate([zero, sin], axis=1), (1, heads)) * scale
    return cos_t, sa_t, sb_t


def _attn_kernel(sink_ref, q_ref, kc_ref, vc_ref, kp_ref, vp_ref, o_ref, *, tq, first_block_has_no_prev):
    kk = jnp.concatenate([kp_ref[...], kc_ref[...]], axis=0).astype(BF16)
    vv = jnp.concatenate([vp_ref[...], vc_ref[...]], axis=0).astype(BF16)
    nk = WINDOW + tq
    qi = lax.broadcasted_iota(jnp.int32, (tq, nk), 0)
    ks = lax.broadcasted_iota(jnp.int32, (tq, nk), 1)
    mask = (ks >= qi) & (ks <= qi + WINDOW)
    if first_block_has_no_prev:
        mask = mask & ((pl.program_id(1) > 0) | (ks >= WINDOW))
    q = q_ref[...]
    outs = []
    for g in range(KV_HEADS):
        kg = kk[:, g * HEAD_DIM:(g + 1) * HEAD_DIM]
        vg = vv[:, g * HEAD_DIM:(g + 1) * HEAD_DIM]
        for j in range(Q_PER_KV):
            h = g * Q_PER_KV + j
            qh = q[:, h * HEAD_DIM:(h + 1) * HEAD_DIM]
            s = lax.dot_general(qh, kg, (((1,), (1,)), ((), ())), preferred_element_type=F32)
            s = jnp.where(mask, s, -jnp.inf)
            sink = sink_ref[h]
            m = jnp.maximum(jnp.max(s, axis=-1, keepdims=True), sink)
            p = jnp.exp(s - m)
            denom = jnp.sum(p, axis=-1, keepdims=True) + jnp.exp(sink - m)
            p = (p / denom).astype(BF16)
            outs.append(jnp.dot(p, vg, preferred_element_type=F32))
    o_ref[...] = jnp.concatenate(outs, axis=1).astype(o_ref.dtype)


def _attention(sink, q2, k2, v2, kprev2, vprev2, *, batch, nb, tq, prev_blocks, first_block_has_no_prev, name):
    cur = lambda b, n, s: (b * nb + n, 0)
    prev = lambda b, n, s: (b * prev_blocks + jnp.maximum(n - 1, 0), 0)
    kern = functools.partial(_attn_kernel, tq=tq, first_block_has_no_prev=first_block_has_no_prev)
    return pl.pallas_call(
        kern,
        grid_spec=pltpu.PrefetchScalarGridSpec(
            num_scalar_prefetch=1,
            grid=(batch, nb),
            in_specs=[
                pl.BlockSpec((tq, ATT_WIDTH), cur),
                pl.BlockSpec((tq, KV_WIDTH), cur),
                pl.BlockSpec((tq, KV_WIDTH), cur),
                pl.BlockSpec((WINDOW, KV_WIDTH), prev),
                pl.BlockSpec((WINDOW, KV_WIDTH), prev),
            ],
            out_specs=pl.BlockSpec((tq, ATT_WIDTH), cur),
        ),
        out_shape=jax.ShapeDtypeStruct(q2.shape, BF16),
        compiler_params=_params(2),
        name=name,
    )(sink, q2, k2, v2, kprev2, vprev2)


def _gate_kernel(gu_ref, gv_ref, w_ref, b_ref, o_ref, *, chunk, n_chunks):
    ri = lax.broadcasted_iota(jnp.int32, (chunk, chunk), 0)
    ci = lax.broadcasted_iota(jnp.int32, (chunk, chunk), 1)
    ws = [jnp.where(ci <= ri, w_ref[h], 0.0).astype(BF16) for h in range(GM_HEADS)]
    for c in range(n_chunks):
        rows = slice(c * chunk, (c + 1) * chunk)
        gv = gv_ref[rows, :].astype(BF16)
        sv = jnp.concatenate(
            [jnp.dot(ws[h], gv[:, h * HEAD_DIM:(h + 1) * HEAD_DIM], preferred_element_type=F32)
             for h in range(GM_HEADS)], axis=1)
        o_ref[rows, :] = (gu_ref[rows, :].astype(F32) * (sv + b_ref[...])).astype(o_ref.dtype)


def _gate(gu2, gv2, w_s, b_tab, *, chunk, n_chunks, name):
    n = gu2.shape[0]
    tm = chunk * n_chunks
    row = lambda i: (i, 0)
    kern = functools.partial(_gate_kernel, chunk=chunk, n_chunks=n_chunks)
    return pl.pallas_call(
        kern,
        grid=(n // tm,),
        in_specs=[
            pl.BlockSpec((tm, GM_WIDTH), row),
            pl.BlockSpec((tm, GM_WIDTH), row),
            pl.BlockSpec((GM_HEADS, chunk, chunk), lambda i: (0, 0, 0)),
            pl.BlockSpec((chunk, GM_WIDTH), lambda i: (0, 0)),
        ],
        out_specs=pl.BlockSpec((tm, GM_WIDTH), row),
        out_shape=jax.ShapeDtypeStruct((n, GM_WIDTH), BF16),
        compiler_params=_params(1),
        name=name,
    )(gu2, gv2, w_s, b_tab)


def _out_proj_kernel(att_ref, gm_ref, x_ref, wo_ref, g_ref, b_ref, rh_ref, rl_ref, *rest):
    h_ref, hb_ref, lg_ref = rest[-3:]
    mix = jnp.dot(att_ref[...], wo_ref[:ATT_WIDTH, :], preferred_element_type=F32)
    mix = mix + jnp.dot(gm_ref[...], wo_ref[ATT_WIDTH:, :], preferred_element_type=F32)
    h = _layer_norm(ALPHA * x_ref[...] + mix, g_ref[...], b_ref[...])
    h_ref[...] = h
    h_hi = h.astype(BF16)
    hb_ref[...] = h_hi
    h_lo = (h - h_hi.astype(F32)).astype(BF16)
    lg = jnp.dot(h_hi, rh_ref[...], preferred_element_type=F32)
    lg = lg + jnp.dot(h_lo, rh_ref[...], preferred_element_type=F32)
    lg = lg + jnp.dot(h_hi, rl_ref[...], preferred_element_type=F32)
    lg_ref[...] = lg


def _out_proj(att2, gm2, x2, w_out_b, ln_g, ln_b, r_hi, r_lo, *, n_total, row_offset, prior, name):
    n = x2.shape[0]
    tm = ROW_TILE
    off = row_offset // tm
    row = lambda i: (i, 0)
    orow = lambda i: (i + off, 0)
    fixed = lambda i: (0, 0)
    in_specs = [
        pl.BlockSpec((tm, ATT_WIDTH), row),
        pl.BlockSpec((tm, GM_WIDTH), row),
        pl.BlockSpec((tm, D_MODEL), row),
        pl.BlockSpec((D_MODEL, D_MODEL), fixed),
        pl.BlockSpec((1, D_MODEL), fixed),
        pl.BlockSpec((1, D_MODEL), fixed),
        pl.BlockSpec((D_MODEL, N_EXPERTS), fixed),
        pl.BlockSpec((D_MODEL, N_EXPERTS), fixed),
    ]
    args = [att2, gm2, x2, w_out_b, ln_g, ln_b, r_hi, r_lo]
    aliases = {}
    if prior is not None:
        for j, p in enumerate(prior):
            in_specs.append(pl.BlockSpec(memory_space=pl.ANY))
            aliases[len(args)] = j
            args.append(p)
    return pl.pallas_call(
        _out_proj_kernel,
        grid=(n // tm,),
        in_specs=in_specs,
        out_specs=[
            pl.BlockSpec((tm, D_MODEL), orow),
            pl.BlockSpec((tm, D_MODEL), orow),
            pl.BlockSpec((tm, N_EXPERTS), orow),
        ],
        out_shape=[
            jax.ShapeDtypeStruct((n_total, D_MODEL), F32),
            jax.ShapeDtypeStruct((n_total, D_MODEL), BF16),
            jax.ShapeDtypeStruct((n_total, N_EXPERTS), F32),
        ],
        input_output_aliases=aliases,
        compiler_params=_params(1),
        name=name,
    )(*args)


def _experts_kernel(be_ref, nu_ref, x_ref, wg_ref, wu_ref, wd_ref, o_ref, wg_s, wu_s, wd_s):
    i = pl.program_id(0)

    @pl.when(i < nu_ref[0])
    def _():
        prev = be_ref[jnp.maximum(i - 1, 0)]

        @pl.when((i == 0) | (be_ref[i] != prev))
        def _():
            wg_s[...] = wg_ref[0].astype(BF16)
            wu_s[...] = wu_ref[0].astype(BF16)
            wd_s[...] = wd_ref[0].astype(BF16)

        x = x_ref[...]
        a = jnp.dot(x, wg_s[...], preferred_element_type=F32)
        u = jnp.dot(x, wu_s[...], preferred_element_type=F32)
        hb = (a * jax.nn.sigmoid(a) * u).astype(BF16)
        o_ref[...] = jnp.dot(hb, wd_s[...], preferred_element_type=F32).astype(o_ref.dtype)


def _experts(block_e, n_used, x_sorted, w_gate_e, w_up_e, w_down_e):
    rows = x_sorted.shape[0]
    n_blocks = rows // EXPERT_ROWS
    xrow = lambda i, be, nu: (jnp.minimum(i, nu[0] - 1), 0)
    wmap = lambda i, be, nu: (be[i], 0, 0)
    return pl.pallas_call(
        _experts_kernel,
        grid_spec=pltpu.PrefetchScalarGridSpec(
            num_scalar_prefetch=2,
            grid=(n_blocks,),
            in_specs=[
                pl.BlockSpec((EXPERT_ROWS, D_MODEL), xrow),
                pl.BlockSpec((1, D_MODEL, F_EXPERT), wmap),
                pl.BlockSpec((1, D_MODEL, F_EXPERT), wmap),
                pl.BlockSpec((1, F_EXPERT, D_MODEL), wmap),
            ],
            out_specs=pl.BlockSpec((EXPERT_ROWS, D_MODEL), xrow),
            scratch_shapes=[
                pltpu.VMEM((D_MODEL, F_EXPERT), BF16),
                pltpu.VMEM((D_MODEL, F_EXPERT), BF16),
                pltpu.VMEM((F_EXPERT, D_MODEL), BF16),
            ],
        ),
        out_shape=jax.ShapeDtypeStruct((rows, D_MODEL), BF16),
        compiler_params=_params(1),
        name="experts",
    )(block_e, n_used, x_sorted, w_gate_e, w_up_e, w_down_e)


def _combine_kernel(g_ref, w_ref, h_ref, hb_ref, wg_ref, wu_ref, wd_ref, ln_g_ref, ln_b_ref, y_ref):
    w = w_ref[...]
    hb = hb_ref[...]
    a = jnp.dot(hb, wg_ref[...], preferred_element_type=F32)
    u = jnp.dot(hb, wu_ref[...], preferred_element_type=F32)
    ffn = jnp.dot((a * jax.nn.sigmoid(a) * u).astype(BF16), wd_ref[...], preferred_element_type=F32)
    for k in range(TOP_K):
        ffn = ffn + w[:, k:k + 1] * g_ref[:, k * D_MODEL:(k + 1) * D_MODEL].astype(F32)
    y_ref[...] = _layer_norm(ALPHA * h_ref[...] + ffn, ln_g_ref[...], ln_b_ref[...])


def _combine(g2, w2, h_all, hb_all, wg_b, wu_b, wd_b, ln_g, ln_b, *, n, row_offset, name):
    tm = COMBINE_ROWS
    off = row_offset // tm
    row = lambda i: (i + off, 0)
    fixed = lambda i: (0, 0)
    return pl.pallas_call(
        _combine_kernel,
        grid=(n // tm,),
        in_specs=[
            pl.BlockSpec((tm, TOP_K * D_MODEL), row),
            pl.BlockSpec((tm, TOP_K), row),
            pl.BlockSpec((tm, D_MODEL), row),
            pl.BlockSpec((tm, D_MODEL), row),
            pl.BlockSpec((D_MODEL, F_EXPERT), fixed),
            pl.BlockSpec((D_MODEL, F_EXPERT), fixed),
            pl.BlockSpec((F_EXPERT, D_MODEL), fixed),
            pl.BlockSpec((1, D_MODEL), fixed),
            pl.BlockSpec((1, D_MODEL), fixed),
        ],
        out_specs=pl.BlockSpec((tm, D_MODEL), lambda i: (i, 0)),
        out_shape=jax.ShapeDtypeStruct((n, D_MODEL), F32),
        compiler_params=_params(1),
        name=name,
    )(g2, w2, h_all, hb_all, wg_b, wu_b, wd_b, ln_g, ln_b)


def _route(logits, router_bias):
    scores = jax.nn.sigmoid(logits)
    biased = scores + router_bias.astype(F32)
    n = logits.shape[0]
    grp = biased.reshape(n, N_GROUPS, N_EXPERTS // N_GROUPS)
    grp_score = lax.top_k(grp, 2)[0].sum(-1)
    _, gidx = lax.top_k(grp_score, TOPK_GROUPS)
    gmask = jnp.any(gidx[:, :, None] == jnp.arange(N_GROUPS)[None, None, :], axis=1)
    emask = jnp.repeat(gmask, N_EXPERTS // N_GROUPS, axis=-1)
    _, eidx = lax.top_k(jnp.where(emask, biased, -jnp.inf), TOP_K)
    w = jnp.take_along_axis(scores, eidx, axis=-1)
    w = w / jnp.sum(w, axis=-1, keepdims=True) * ROUTED_SCALE
    return eidx.astype(jnp.int32), w


def _dispatch_plan(eidx, n_blocks):
    a = eidx.size
    e_flat = eidx.reshape(-1)
    flat = jnp.arange(a, dtype=jnp.int32)
    e_sorted, order = lax.sort((e_flat, flat), num_keys=1)
    experts = jnp.arange(N_EXPERTS, dtype=jnp.int32)
    sort_end = jnp.searchsorted(e_sorted, experts, side="right").astype(jnp.int32)
    sort_start = jnp.searchsorted(e_sorted, experts, side="left").astype(jnp.int32)
    counts = sort_end - sort_start
    padded = (counts + EXPERT_ROWS - 1) // EXPERT_ROWS * EXPERT_ROWS
    pad_end = jnp.cumsum(padded).astype(jnp.int32)
    pad_start = pad_end - padded
    n_used = pad_end[-1] // EXPERT_ROWS
    dest_sorted = pad_start[e_sorted] + flat - sort_start[e_sorted]
    _, dest_flat = lax.sort((order, dest_sorted), num_keys=1)
    blk = jnp.arange(n_blocks, dtype=jnp.int32)
    block_e = jnp.searchsorted(pad_end, blk * EXPERT_ROWS, side="right").astype(jnp.int32)
    last_e = block_e[jnp.maximum(n_used - 1, 0)]
    block_e = jnp.where(blk < n_used, jnp.minimum(block_e, N_EXPERTS - 1), last_e)
    r = jnp.arange(n_blocks * EXPERT_ROWS, dtype=jnp.int32)
    re = block_e[r // EXPERT_ROWS]
    j = r - pad_start[re]
    valid = (j >= 0) & (j < counts[re])
    pos = jnp.clip(sort_start[re] + j, 0, a - 1)
    row_tok = jnp.where(valid, order[pos] // TOP_K, 0)
    return row_tok, dest_flat, block_e, n_used.reshape(1).astype(jnp.int32)


def kernel(x_prompt, x_sample, cache_k, cache_v, w_in, sink, gm_ln_g, gm_ln_b, gm_w_s, gm_b_s, w_out, ln1_g, ln1_b,
           router_w, router_bias, w_gate_e, w_up_e, w_down_e, w_gate_s, w_up_s, w_down_s, ln2_g, ln2_b):
    bp, sp = x_prompt.shape[:2]
    bs, ts = x_sample.shape[:2]
    r = cache_k.shape[2]
    assert r == WINDOW and sp % ROW_TILE == 0 and (bs * ts) % ROW_TILE == 0
    n_p, n_s = bp * sp, bs * ts
    n_total = n_p + n_s
    l = 0

    w_in_b = w_in[l].astype(BF16)
    w_out_b = w_out[l].astype(BF16)
    r_hi = router_w[l].astype(BF16)
    r_lo = (router_w[l] - r_hi.astype(F32)).astype(BF16)
    row_vec = lambda v: v.reshape(1, -1)
    gm_g, gm_b = row_vec(gm_ln_g[l]), row_vec(gm_ln_b[l])
    sink_l = sink[l].astype(F32)

    xp2 = x_prompt.reshape(n_p, D_MODEL)
    tabs_p = _rope_tables(jnp.arange(sp, dtype=jnp.int32))
    q, k, v, gu, gv = _in_proj(xp2, w_in_b, tabs_p, gm_g, gm_b, tm=ROW_TILE, gv_dtype=BF16, name="in_proj_prompt")
    nb = sp // WINDOW
    att = _attention(sink_l, q, k, v, k, v, batch=bp, nb=nb, tq=WINDOW, prev_blocks=nb,
                     first_block_has_no_prev=True, name="attn_prompt")
    b_tab_p = jnp.repeat(gm_b_s[l].T, HEAD_DIM, axis=1)
    gm = _gate(gu, gv, gm_w_s[l], b_tab_p, chunk=CHUNK, n_chunks=ROW_TILE // CHUNK, name="gate_prompt")
    new_kp = k.reshape(bp, sp, KV_HEADS, HEAD_DIM)[:, sp - r:][None]
    new_vp = v.reshape(bp, sp, KV_HEADS, HEAD_DIM)[:, sp - r:][None]
    outs_p = _out_proj(att, gm, xp2, w_out_b, row_vec(ln1_g[l]), row_vec(ln1_b[l]), r_hi, r_lo,
                       n_total=n_total, row_offset=0, prior=None, name="out_proj_prompt")

    xs2 = x_sample.reshape(n_s, D_MODEL)
    pos_s = PAST_LEN + jnp.arange(ts, dtype=jnp.int32)
    tabs_s = tuple(jnp.tile(t, (bs, 1)) for t in _rope_tables(pos_s))
    q, k, v, gu, gv = _in_proj(xs2, w_in_b, tabs_s, gm_g, gm_b, tm=n_s, gv_dtype=F32, name="in_proj_sample")
    tq = 8
    pad_rows = lambda t: jnp.pad(t.reshape(bs, ts, -1), ((0, 0), (0, tq - ts), (0, 0))).reshape(bs * tq, -1)
    ck2 = cache_k[l].reshape(bs * r, KV_WIDTH)
    cv2 = cache_v[l].reshape(bs * r, KV_WIDTH)
    att = _attention(sink_l, pad_rows(q), pad_rows(k), pad_rows(v), ck2, cv2, batch=bs, nb=1, tq=tq, prev_blocks=1,
                     first_block_has_no_prev=False, name="attn_sample")
    att = att.reshape(bs, tq, ATT_WIDTH)[:, :ts].reshape(n_s, ATT_WIDTH)
    w_small = gm_w_s[l][:, :ts, :ts]
    w_big = jnp.einsum("ab,hij->haibj", jnp.eye(bs, dtype=F32), w_small).reshape(GM_HEADS, n_s, n_s)
    b_tab_s = jnp.tile(jnp.repeat(gm_b_s[l][:, :ts].T, HEAD_DIM, axis=1), (bs, 1))
    gm = _gate(gu, gv, w_big, b_tab_s, chunk=n_s, n_chunks=1, name="gate_sample")
    new_ks = jnp.concatenate([cache_k[l], k.reshape(bs, ts, KV_HEADS, HEAD_DIM)], axis=1)[:, ts:][None]
    new_vs = jnp.concatenate([cache_v[l], v.reshape(bs, ts, KV_HEADS, HEAD_DIM)], axis=1)[:, ts:][None]
    new_gs = gv.reshape(bs, ts, GM_WIDTH)[None]
    h_all, hb_all, logits = _out_proj(att, gm, xs2, w_out_b, row_vec(ln1_g[l]), row_vec(ln1_b[l]), r_hi, r_lo,
                                      n_total=n_total, row_offset=n_p, prior=outs_p, name="out_proj_sample")

    eidx, wts = _route(logits, router_bias[l])
    a = n_total * TOP_K
    n_blocks = -(-(a + N_EXPERTS * (EXPERT_ROWS - 1)) // EXPERT_ROWS)
    row_tok, dest_flat, block_e, n_used = _dispatch_plan(eidx, n_blocks)
    x_sorted = jnp.take(hb_all, row_tok, axis=0)
    out_sorted = _experts(block_e, n_used, x_sorted, w_gate_e[l], w_up_e[l], w_down_e[l])
    g2 = jnp.take(out_sorted, dest_flat, axis=0).reshape(n_total, TOP_K * D_MODEL)
    shared = (w_gate_s[l].astype(BF16), w_up_s[l].astype(BF16), w_down_s[l].astype(BF16))
    ln2 = (row_vec(ln2_g[l]), row_vec(ln2_b[l]))
    y_p = _combine(g2, wts, h_all, hb_all, *shared, *ln2, n=n_p, row_offset=0, name="combine_prompt")
    y_s = _combine(g2, wts, h_all, hb_all, *shared, *ln2, n=n_s, row_offset=n_p, name="combine_sample")
    return (y_p.reshape(bp, sp, D_MODEL), y_s.reshape(bs, ts, D_MODEL), new_kp, new_vp, new_ks, new_vs, new_gs)
```

```python
import functools

import jax
import jax.numpy as jnp
import numpy as np
from jax import lax
from jax.experimental import pallas as pl
from jax.experimental.pallas import tpu as pltpu

D_MODEL = 1024
HEAD_DIM = 64
ATT_HEADS = 8
KV_HEADS = 2
Q_PER_KV = ATT_HEADS // KV_HEADS
GM_HEADS = 8
ATT_WIDTH = ATT_HEADS * HEAD_DIM
KV_WIDTH = KV_HEADS * HEAD_DIM
GM_WIDTH = GM_HEADS * HEAD_DIM
ROPE_WIDTH = ATT_WIDTH + KV_WIDTH
IN_WIDTH = ATT_WIDTH + 2 * KV_WIDTH + 2 * GM_WIDTH
WINDOW = 128
CHUNK = 128
PAST_LEN = 16384
ROPE_THETA = 10000.0
ATT_SCALE = HEAD_DIM ** -0.5
N_EXPERTS = 256
TOP_K = 8
N_GROUPS = 8
TOPK_GROUPS = 4
F_EXPERT = 256
ROUTED_SCALE = 2.5
LN_EPS = 1e-5
DEPTH = 1
ALPHA = (2.0 * DEPTH) ** 0.25

LANES = 128
ROW_TILE = 512
EXPERT_ROWS = 256
COMBINE_ROWS = 128
DISPATCH_ROWS = 256
ROUTE_TILE = 256
DMA_UNROLL = 4
VMEM_LIMIT = 56 * 1024 * 1024

F32 = jnp.float32
BF16 = jnp.bfloat16


def _params(n_axes):
    return pltpu.CompilerParams(dimension_semantics=("arbitrary",) * n_axes, vmem_limit_bytes=VMEM_LIMIT)


def _layer_norm(x, g, b):
    mu = jnp.mean(x, axis=-1, keepdims=True)
    xc = x - mu
    var = jnp.mean(xc * xc, axis=-1, keepdims=True)
    return xc * lax.rsqrt(var + LN_EPS) * g + b


def _gelu(x):
    return 0.5 * x * (1.0 + lax.erf(x * np.float32(np.sqrt(0.5))))


def _in_proj_kernel(x_ref, w_ref, cos_ref, sa_ref, sb_ref, g_ref, b_ref, q_ref, k_ref, v_ref, gu_ref, gv_ref):
    x = x_ref[...].astype(BF16)
    zr = jnp.dot(x, w_ref[:, :ROPE_WIDTH], preferred_element_type=F32)
    pieces = []
    for c in range(ROPE_WIDTH // LANES):
        sl = slice(c * LANES, (c + 1) * LANES)
        zc = zr[:, sl]
        pieces.append(zc * cos_ref[:, sl]
                      + pltpu.roll(zc, LANES - HEAD_DIM // 2, 1) * sa_ref[:, sl]
                      + pltpu.roll(zc, HEAD_DIM // 2, 1) * sb_ref[:, sl])
    for c in range(ATT_WIDTH // LANES):
        q_ref[:, c * LANES:(c + 1) * LANES] = pieces[c].astype(q_ref.dtype)
    k_ref[...] = pieces[ATT_WIDTH // LANES]
    v_ref[...] = jnp.dot(x, w_ref[:, ROPE_WIDTH:ROPE_WIDTH + KV_WIDTH], preferred_element_type=F32)
    g0 = ROPE_WIDTH + KV_WIDTH
    zu = jnp.dot(x, w_ref[:, g0:g0 + GM_WIDTH], preferred_element_type=F32)
    gu_ref[...] = _gelu(zu).astype(gu_ref.dtype)
    zv = jnp.dot(x, w_ref[:, g0 + GM_WIDTH:g0 + 2 * GM_WIDTH], preferred_element_type=F32)
    gv = _layer_norm(_gelu(zv), g_ref[...], b_ref[...])
    gv_ref[...] = gv.astype(gv_ref.dtype)


def _in_proj(x2, w_in_b, tabs, ln_g, ln_b, *, tm, gv_dtype, name):
    n = x2.shape[0]
    cos_t, sa_t, sb_t = tabs
    period = cos_t.shape[0] // tm
    row = lambda i: (i, 0)
    tab = lambda i: (i % period, 0)
    fixed = lambda i: (0, 0)
    return pl.pallas_call(
        _in_proj_kernel,
        grid=(n // tm,),
        in_specs=[
            pl.BlockSpec((tm, D_MODEL), row),
            pl.BlockSpec((D_MODEL, IN_WIDTH), fixed),
            pl.BlockSpec((tm, ROPE_WIDTH), tab),
            pl.BlockSpec((tm, ROPE_WIDTH), tab),
            pl.BlockSpec((tm, ROPE_WIDTH), tab),
            pl.BlockSpec((1, GM_WIDTH), fixed),
            pl.BlockSpec((1, GM_WIDTH), fixed),
        ],
        out_specs=[
            pl.BlockSpec((tm, ATT_WIDTH), row),
            pl.BlockSpec((tm, KV_WIDTH), row),
            pl.BlockSpec((tm, KV_WIDTH), row),
            pl.BlockSpec((tm, GM_WIDTH), row),
            pl.BlockSpec((tm, GM_WIDTH), row),
        ],
        out_shape=[
            jax.ShapeDtypeStruct((n, ATT_WIDTH), BF16),
            jax.ShapeDtypeStruct((n, KV_WIDTH), F32),
            jax.ShapeDtypeStruct((n, KV_WIDTH), F32),
            jax.ShapeDtypeStruct((n, GM_WIDTH), BF16),
            jax.ShapeDtypeStruct((n, GM_WIDTH), gv_dtype),
        ],
        compiler_params=_params(1),
        name=name,
    )(x2, w_in_b, cos_t, sa_t, sb_t, ln_g, ln_b)


def _rope_tables(pos):
    half = HEAD_DIM // 2
    inv = ROPE_THETA ** (-jnp.arange(half, dtype=F32) * 2.0 / HEAD_DIM)
    ang = pos.astype(F32)[:, None] * inv[None, :]
    cos, sin = jnp.cos(ang), jnp.sin(ang)
    zero = jnp.zeros_like(sin)
    heads = ROPE_WIDTH // HEAD_DIM
    scale = jnp.where(jnp.arange(ROPE_WIDTH) < ATT_WIDTH, ATT_SCALE, 1.0).astype(F32)[None, :]
    cos_t = jnp.tile(jnp.concatenate([cos, cos], axis=1), (1, heads)) * scale
    sa_t = jnp.tile(jnp.concatenate([-sin, zero], axis=1), (1, heads)) * scale
    sb_t = jnp.tile(jnp.concatenate([zero, sin], axis=1), (1, heads)) * scale
    return cos_t, sa_t, sb_t


def _attn_kernel(sink_ref, q_ref, kc_ref, vc_ref, kp_ref, vp_ref, o_ref, *, tq, first_block_has_no_prev):
    kk = jnp.concatenate([kp_ref[...], kc_ref[...]], axis=0).astype(BF16)
    vv = jnp.concatenate([vp_ref[...], vc_ref[...]], axis=0).astype(BF16)
    nk = WINDOW + tq
    qi = lax.broadcasted_iota(jnp.int32, (tq, nk), 0)
    ks = lax.broadcasted_iota(jnp.int32, (tq, nk), 1)
    mask = (ks >= qi) & (ks <= qi + WINDOW)
    if first_block_has_no_prev:
        mask = mask & ((pl.program_id(1) > 0) | (ks >= WINDOW))
    q = q_ref[...]
    outs = []
    for g in range(KV_HEADS):
        kg = kk[:, g * HEAD_DIM:(g + 1) * HEAD_DIM]
        vg = vv[:, g * HEAD_DIM:(g + 1) * HEAD_DIM]
        for j in range(Q_PER_KV):
            h = g * Q_PER_KV + j
            qh = q[:, h * HEAD_DIM:(h + 1) * HEAD_DIM]
            s = lax.dot_general(qh, kg, (((1,), (1,)), ((), ())), preferred_element_type=F32)
            s = jnp.where(mask, s, -jnp.inf)
            sink = sink_ref[h]
            m = jnp.maximum(jnp.max(s, axis=-1, keepdims=True), sink)
            p = jnp.exp(s - m)
            denom = jnp.sum(p, axis=-1, keepdims=True) + jnp.exp(sink - m)
            p = (p / denom).astype(BF16)
            outs.append(jnp.dot(p, vg, preferred_element_type=F32))
    o_ref[...] = jnp.concatenate(outs, axis=1).astype(o_ref.dtype)


def _attention(sink, q2, k2, v2, kprev2, vprev2, *, batch, nb, tq, prev_blocks, first_block_has_no_prev, name):
    cur = lambda b, n, s: (b * nb + n, 0)
    prev = lambda b, n, s: (b * prev_blocks + jnp.maximum(n - 1, 0), 0)
    kern = functools.partial(_attn_kernel, tq=tq, first_block_has_no_prev=first_block_has_no_prev)
    return pl.pallas_call(
        kern,
        grid_spec=pltpu.PrefetchScalarGridSpec(
            num_scalar_prefetch=1,
            grid=(batch, nb),
            in_specs=[
                pl.BlockSpec((tq, ATT_WIDTH), cur),
                pl.BlockSpec((tq, KV_WIDTH), cur),
                pl.BlockSpec((tq, KV_WIDTH), cur),
                pl.BlockSpec((WINDOW, KV_WIDTH), prev),
                pl.BlockSpec((WINDOW, KV_WIDTH), prev),
            ],
            out_specs=pl.BlockSpec((tq, ATT_WIDTH), cur),
        ),
        out_shape=jax.ShapeDtypeStruct(q2.shape, BF16),
        compiler_params=_params(2),
        name=name,
    )(sink, q2, k2, v2, kprev2, vprev2)


def _gate_kernel(gu_ref, gv_ref, w_ref, b_ref, o_ref, *, chunk, n_chunks):
    ri = lax.broadcasted_iota(jnp.int32, (chunk, chunk), 0)
    ci = lax.broadcasted_iota(jnp.int32, (chunk, chunk), 1)
    ws = [jnp.where(ci <= ri, w_ref[h], 0.0).astype(BF16) for h in range(GM_HEADS)]
    for c in range(n_chunks):
        rows = slice(c * chunk, (c + 1) * chunk)
        gv = gv_ref[rows, :].astype(BF16)
        sv = jnp.concatenate(
            [jnp.dot(ws[h], gv[:, h * HEAD_DIM:(h + 1) * HEAD_DIM], preferred_element_type=F32)
             for h in range(GM_HEADS)], axis=1)
        o_ref[rows, :] = (gu_ref[rows, :].astype(F32) * (sv + b_ref[...])).astype(o_ref.dtype)


def _gate(gu2, gv2, w_s, b_tab, *, chunk, n_chunks, name):
    n = gu2.shape[0]
    tm = chunk * n_chunks
    row = lambda i: (i, 0)
    kern = functools.partial(_gate_kernel, chunk=chunk, n_chunks=n_chunks)
    return pl.pallas_call(
        kern,
        grid=(n // tm,),
        in_specs=[
            pl.BlockSpec((tm, GM_WIDTH), row),
            pl.BlockSpec((tm, GM_WIDTH), row),
            pl.BlockSpec((GM_HEADS, chunk, chunk), lambda i: (0, 0, 0)),
            pl.BlockSpec((chunk, GM_WIDTH), lambda i: (0, 0)),
        ],
        out_specs=pl.BlockSpec((tm, GM_WIDTH), row),
        out_shape=jax.ShapeDtypeStruct((n, GM_WIDTH), BF16),
        compiler_params=_params(1),
        name=name,
    )(gu2, gv2, w_s, b_tab)


def _out_proj_kernel(att_ref, gm_ref, x_ref, wo_ref, g_ref, b_ref, rh_ref, rl_ref, h_ref, lg_ref):
    mix = jnp.dot(att_ref[...], wo_ref[:ATT_WIDTH, :], preferred_element_type=F32)
    mix = mix + jnp.dot(gm_ref[...], wo_ref[ATT_WIDTH:, :], preferred_element_type=F32)
    h = _layer_norm(ALPHA * x_ref[...] + mix, g_ref[...], b_ref[...])
    h_ref[...] = h
    h_hi = h.astype(BF16)
    h_lo = (h - h_hi.astype(F32)).astype(BF16)
    nt = (((1,), (1,)), ((), ()))
    lg = lax.dot_general(rh_ref[...], h_hi, nt, preferred_element_type=F32)
    lg = lg + lax.dot_general(rh_ref[...], h_lo, nt, preferred_element_type=F32)
    lg = lg + lax.dot_general(rl_ref[...], h_hi, nt, preferred_element_type=F32)
    lg_ref[...] = lg


def _out_proj(att2, gm2, x2, w_out_b, ln_g, ln_b, r_hi, r_lo, *, name):
    n = x2.shape[0]
    tm = ROW_TILE
    row = lambda i: (i, 0)
    fixed = lambda i: (0, 0)
    return pl.pallas_call(
        _out_proj_kernel,
        grid=(n // tm,),
        in_specs=[
            pl.BlockSpec((tm, ATT_WIDTH), row),
            pl.BlockSpec((tm, GM_WIDTH), row),
            pl.BlockSpec((tm, D_MODEL), row),
            pl.BlockSpec((D_MODEL, D_MODEL), fixed),
            pl.BlockSpec((1, D_MODEL), fixed),
            pl.BlockSpec((1, D_MODEL), fixed),
            pl.BlockSpec((N_EXPERTS, D_MODEL), fixed),
            pl.BlockSpec((N_EXPERTS, D_MODEL), fixed),
        ],
        out_specs=[
            pl.BlockSpec((tm, D_MODEL), row),
            pl.BlockSpec((N_EXPERTS, tm), lambda i: (0, i)),
        ],
        out_shape=[
            jax.ShapeDtypeStruct((n, D_MODEL), F32),
            jax.ShapeDtypeStruct((N_EXPERTS, n), F32),
        ],
        compiler_params=_params(1),
        name=name,
    )(att2, gm2, x2, w_out_b, ln_g, ln_b, r_hi, r_lo)


def _experts_kernel(be_ref, nu_ref, x_ref, wg_ref, wu_ref, wd_ref, o_ref, wg_s, wu_s, wd_s):
    i = pl.program_id(0)

    @pl.when(i < nu_ref[0])
    def _():
        prev = be_ref[jnp.maximum(i - 1, 0)]

        @pl.when((i == 0) | (be_ref[i] != prev))
        def _():
            wg_s[...] = wg_ref[0].astype(BF16)
            wu_s[...] = wu_ref[0].astype(BF16)
            wd_s[...] = wd_ref[0].astype(BF16)

        x = x_ref[...].astype(BF16)
        a = jnp.dot(x, wg_s[...], preferred_element_type=F32)
        u = jnp.dot(x, wu_s[...], preferred_element_type=F32)
        hb = (a * jax.nn.sigmoid(a) * u).astype(BF16)
        o_ref[...] = jnp.dot(hb, wd_s[...], preferred_element_type=F32).astype(o_ref.dtype)


def _experts(block_e, n_used, x_sorted, w_gate_e, w_up_e, w_down_e):
    rows = x_sorted.shape[0]
    n_blocks = rows // EXPERT_ROWS
    xrow = lambda i, be, nu: (jnp.minimum(i, nu[0] - 1), 0)
    wmap = lambda i, be, nu: (be[i], 0, 0)
    return pl.pallas_call(
        _experts_kernel,
        grid_spec=pltpu.PrefetchScalarGridSpec(
            num_scalar_prefetch=2,
            grid=(n_blocks,),
            in_specs=[
                pl.BlockSpec((EXPERT_ROWS, D_MODEL), xrow),
                pl.BlockSpec((1, D_MODEL, F_EXPERT), wmap),
                pl.BlockSpec((1, D_MODEL, F_EXPERT), wmap),
                pl.BlockSpec((1, F_EXPERT, D_MODEL), wmap),
            ],
            out_specs=pl.BlockSpec((EXPERT_ROWS, D_MODEL), xrow),
            scratch_shapes=[
                pltpu.VMEM((D_MODEL, F_EXPERT), BF16),
                pltpu.VMEM((D_MODEL, F_EXPERT), BF16),
                pltpu.VMEM((F_EXPERT, D_MODEL), BF16),
            ],
        ),
        out_shape=jax.ShapeDtypeStruct((rows, D_MODEL), F32),
        compiler_params=_params(1),
        name="experts",
    )(block_e, n_used, x_sorted, w_gate_e, w_up_e, w_down_e)


def _combine_kernel(dest_ref, w_ref, h_ref, wg_ref, wu_ref, wd_ref, ln_g_ref, ln_b_ref, os_ref, y_ref, gbuf, sem):
    t = h_ref.shape[0]

    def row_copy(j, k):
        return pltpu.make_async_copy(os_ref.at[pl.ds(dest_ref[k, j], 1)], gbuf.at[k, pl.ds(j, 1)], sem)

    def issue(j, c):
        for k in range(TOP_K):
            row_copy(j, k).start()
        return c

    def drain(j, c):
        for k in range(TOP_K):
            row_copy(j, k).wait()
        return c

    lax.fori_loop(0, t, issue, 0, unroll=DMA_UNROLL)
    h = h_ref[...]
    hb = h.astype(BF16)
    a = jnp.dot(hb, wg_ref[...], preferred_element_type=F32)
    u = jnp.dot(hb, wu_ref[...], preferred_element_type=F32)
    ffn = jnp.dot((a * jax.nn.sigmoid(a) * u).astype(BF16), wd_ref[...], preferred_element_type=F32)
    lax.fori_loop(0, t, drain, 0, unroll=DMA_UNROLL)
    w = w_ref[...]
    for k in range(TOP_K):
        ffn = ffn + w[:, k:k + 1] * gbuf[k]
    y_ref[...] = _layer_norm(ALPHA * h + ffn, ln_g_ref[...], ln_b_ref[...])


def _combine(dest_t, w2, h2, wg_b, wu_b, wd_b, ln_g, ln_b, out_sorted, *, row_offset, name):
    n = h2.shape[0]
    tm = COMBINE_ROWS
    off = row_offset // tm
    row = lambda i: (i, 0)
    fixed = lambda i: (0, 0)
    return pl.pallas_call(
        _combine_kernel,
        grid=(n // tm,),
        in_specs=[
            pl.BlockSpec((TOP_K, tm), lambda i: (0, i + off), memory_space=pltpu.SMEM),
            pl.BlockSpec((tm, TOP_K), lambda i: (i + off, 0)),
            pl.BlockSpec((tm, D_MODEL), row),
            pl.BlockSpec((D_MODEL, F_EXPERT), fixed),
            pl.BlockSpec((D_MODEL, F_EXPERT), fixed),
            pl.BlockSpec((F_EXPERT, D_MODEL), fixed),
            pl.BlockSpec((1, D_MODEL), fixed),
            pl.BlockSpec((1, D_MODEL), fixed),
            pl.BlockSpec(memory_space=pl.ANY),
        ],
        out_specs=pl.BlockSpec((tm, D_MODEL), row),
        out_shape=jax.ShapeDtypeStruct((n, D_MODEL), F32),
        scratch_shapes=[pltpu.VMEM((TOP_K, tm, D_MODEL), F32), pltpu.SemaphoreType.DMA],
        compiler_params=_params(1),
        name=name,
    )(dest_t, w2, h2, wg_b, wu_b, wd_b, ln_g, ln_b, out_sorted)


def _route_kernel(lg_ref, bias_ref, eidx_ref, w_ref, rank_ref, cnt_ref, carry_ref):
    @pl.when(pl.program_id(0) == 0)
    def _():
        carry_ref[...] = jnp.zeros_like(carry_ref)

    t = lg_ref.shape[1]
    gsz = N_EXPERTS // N_GROUPS
    neg = -jnp.inf
    s = jax.nn.sigmoid(lg_ref[...])
    biased = s + bias_ref[...]
    io_g = lax.broadcasted_iota(jnp.int32, (gsz, t), 0)
    grp_rows = []
    for g in range(N_GROUPS):
        blk = biased[g * gsz:(g + 1) * gsz, :]
        m1 = jnp.max(blk, axis=0, keepdims=True)
        i1 = jnp.min(jnp.where(blk == m1, io_g, gsz), axis=0, keepdims=True)
        m2 = jnp.max(jnp.where(io_g == i1, neg, blk), axis=0, keepdims=True)
        grp_rows.append(m1 + m2)
    gs = jnp.concatenate(grp_rows, axis=0)
    io8 = lax.broadcasted_iota(jnp.int32, (N_GROUPS, t), 0)
    gsel = jnp.zeros((N_GROUPS, t), jnp.int32)
    for _ in range(TOPK_GROUPS):
        m = jnp.max(gs, axis=0, keepdims=True)
        gi = jnp.min(jnp.where(gs == m, io8, N_GROUPS), axis=0, keepdims=True)
        hit = io8 == gi
        gsel = jnp.where(hit, 1, gsel)
        gs = jnp.where(hit, neg, gs)
    masked = jnp.concatenate(
        [jnp.where(gsel[g:g + 1, :] > 0, biased[g * gsz:(g + 1) * gsz, :], neg) for g in range(N_GROUPS)], axis=0)

    eio = lax.broadcasted_iota(jnp.int32, (N_EXPERTS, t), 0)
    cur = masked
    idx_rows, w_rows = [], []
    for _ in range(TOP_K):
        m = jnp.max(cur, axis=0, keepdims=True)
        idx = jnp.min(jnp.where(cur == m, eio, N_EXPERTS), axis=0, keepdims=True)
        hit = eio == idx
        w_rows.append(jnp.sum(jnp.where(hit, s, 0.0), axis=0, keepdims=True))
        cur = jnp.where(hit, neg, cur)
        idx_rows.append(idx)
    sel = jnp.where(cur != masked, 1.0, 0.0)

    tri = jnp.where(lax.broadcasted_iota(jnp.int32, (t, t), 0) < lax.broadcasted_iota(jnp.int32, (t, t), 1), 1.0, 0.0)
    pref = jnp.dot(sel.astype(BF16), tri.astype(BF16), preferred_element_type=F32) + carry_ref[...]
    rank_rows = [jnp.sum(jnp.where(eio == idx_rows[k], pref, 0.0), axis=0, keepdims=True) for k in range(TOP_K)]
    carry_ref[...] += jnp.sum(sel, axis=1, keepdims=True)

    wk = jnp.concatenate(w_rows, axis=0)
    eidx_ref[...] = jnp.concatenate(idx_rows, axis=0)
    w_ref[...] = wk / jnp.sum(wk, axis=0, keepdims=True) * ROUTED_SCALE
    rank_ref[...] = jnp.concatenate(rank_rows, axis=0).astype(jnp.int32)
    cnt_ref[...] = carry_ref[...].astype(jnp.int32)


def _route(logits_t, bias_col):
    n = logits_t.shape[1]
    t = ROUTE_TILE
    col = lambda i: (0, i)
    fixed = lambda i: (0, 0)
    return pl.pallas_call(
        _route_kernel,
        grid=(n // t,),
        in_specs=[pl.BlockSpec((N_EXPERTS, t), col), pl.BlockSpec((N_EXPERTS, 1), fixed)],
        out_specs=[
            pl.BlockSpec((TOP_K, t), col),
            pl.BlockSpec((TOP_K, t), col),
            pl.BlockSpec((TOP_K, t), col),
            pl.BlockSpec((N_EXPERTS, 1), fixed),
        ],
        out_shape=[
            jax.ShapeDtypeStruct((TOP_K, n), jnp.int32),
            jax.ShapeDtypeStruct((TOP_K, n), F32),
            jax.ShapeDtypeStruct((TOP_K, n), jnp.int32),
            jax.ShapeDtypeStruct((N_EXPERTS, 1), jnp.int32),
        ],
        scratch_shapes=[pltpu.VMEM((N_EXPERTS, 1), F32)],
        compiler_params=_params(1),
        name="route",
    )(logits_t, bias_col)


def _dest_kernel(eidx_ref, rank_ref, start_ref, dest_ref):
    t = eidx_ref.shape[1]
    eio = lax.broadcasted_iota(jnp.int32, (N_EXPERTS, t), 0)
    start = start_ref[...]
    rows = [jnp.sum(jnp.where(eio == eidx_ref[k:k + 1, :], start, 0.0), axis=0, keepdims=True) for k in range(TOP_K)]
    dest_ref[...] = jnp.concatenate(rows, axis=0).astype(jnp.int32) + rank_ref[...]


def _dest(eidx_t, rank_t, pad_start_col):
    n = eidx_t.shape[1]
    t = ROW_TILE
    col = lambda i: (0, i)
    return pl.pallas_call(
        _dest_kernel,
        grid=(n // t,),
        in_specs=[pl.BlockSpec((TOP_K, t), col), pl.BlockSpec((TOP_K, t), col),
                  pl.BlockSpec((N_EXPERTS, 1), lambda i: (0, 0))],
        out_specs=pl.BlockSpec((TOP_K, t), col),
        out_shape=jax.ShapeDtypeStruct((TOP_K, n), jnp.int32),
        compiler_params=_params(1),
        name="dest",
    )(eidx_t, rank_t, pad_start_col)


def _dispatch_kernel(dest_ref, h_ref, xs_in_ref, xs_ref, sem):
    del xs_in_ref
    t = h_ref.shape[0]

    def row_copy(j, k):
        return pltpu.make_async_copy(h_ref.at[pl.ds(j, 1)], xs_ref.at[pl.ds(dest_ref[k, j], 1)], sem)

    def issue(j, c):
        for k in range(TOP_K):
            row_copy(j, k).start()
        return c

    def drain(j, c):
        for k in range(TOP_K):
            row_copy(j, k).wait()
        return c

    lax.fori_loop(0, t, issue, 0, unroll=DMA_UNROLL)
    lax.fori_loop(0, t, drain, 0, unroll=DMA_UNROLL)


def _dispatch(dest_t, h2, xs_init, *, col_offset, name):
    n = h2.shape[0]
    t = DISPATCH_ROWS
    off = col_offset // t
    return pl.pallas_call(
        _dispatch_kernel,
        grid=(n // t,),
        in_specs=[
            pl.BlockSpec((TOP_K, t), lambda i: (0, i + off), memory_space=pltpu.SMEM),
            pl.BlockSpec((t, D_MODEL), lambda i: (i, 0)),
            pl.BlockSpec(memory_space=pl.ANY),
        ],
        out_specs=pl.BlockSpec(memory_space=pl.ANY),
        out_shape=jax.ShapeDtypeStruct(xs_init.shape, xs_init.dtype),
        input_output_aliases={2: 0},
        scratch_shapes=[pltpu.SemaphoreType.DMA],
        compiler_params=_params(1),
        name=name,
    )(dest_t, h2, xs_init)


def _block_plan(counts, n_blocks):
    padded = (counts + EXPERT_ROWS - 1) // EXPERT_ROWS * EXPERT_ROWS
    pad_end = jnp.cumsum(padded).astype(jnp.int32)
    pad_start = pad_end - padded
    n_used = pad_end[-1] // EXPERT_ROWS
    blk = jnp.arange(n_blocks, dtype=jnp.int32)
    block_e = jnp.sum((pad_end[None, :] <= blk[:, None] * EXPERT_ROWS).astype(jnp.int32), axis=1)
    block_e = jnp.minimum(block_e, N_EXPERTS - 1)
    last_e = jnp.max(jnp.where(blk < n_used, block_e, 0))
    block_e = jnp.where(blk < n_used, block_e, last_e)
    return pad_start, block_e, n_used.reshape(1).astype(jnp.int32)


def kernel(x_prompt, x_sample, cache_k, cache_v, w_in, sink, gm_ln_g, gm_ln_b, gm_w_s, gm_b_s, w_out, ln1_g, ln1_b,
           router_w, router_bias, w_gate_e, w_up_e, w_down_e, w_gate_s, w_up_s, w_down_s, ln2_g, ln2_b):
    bp, sp = x_prompt.shape[:2]
    bs, ts = x_sample.shape[:2]
    r = cache_k.shape[2]
    assert r == WINDOW and sp % ROW_TILE == 0 and (bs * ts) % ROW_TILE == 0
    n_p, n_s = bp * sp, bs * ts
    n_total = n_p + n_s
    l = 0

    w_in_b = w_in[l].astype(BF16)
    w_out_b = w_out[l].astype(BF16)
    router_t = router_w[l].T
    r_hi = router_t.astype(BF16)
    r_lo = (router_t - r_hi.astype(F32)).astype(BF16)
    row_vec = lambda v: v.reshape(1, -1)
    gm_g, gm_b = row_vec(gm_ln_g[l]), row_vec(gm_ln_b[l])
    sink_l = sink[l].astype(F32)

    xp2 = x_prompt.reshape(n_p, D_MODEL)
    tabs_p = _rope_tables(jnp.arange(sp, dtype=jnp.int32))
    q, k, v, gu, gv = _in_proj(xp2, w_in_b, tabs_p, gm_g, gm_b, tm=ROW_TILE, gv_dtype=BF16, name="in_proj_prompt")
    nb = sp // WINDOW
    att = _attention(sink_l, q, k, v, k, v, batch=bp, nb=nb, tq=WINDOW, prev_blocks=nb,
                     first_block_has_no_prev=True, name="attn_prompt")
    b_tab_p = jnp.repeat(gm_b_s[l].T, HEAD_DIM, axis=1)
    gm = _gate(gu, gv, gm_w_s[l], b_tab_p, chunk=CHUNK, n_chunks=ROW_TILE // CHUNK, name="gate_prompt")
    new_kp = k.reshape(bp, sp, KV_HEADS, HEAD_DIM)[:, sp - r:][None]
    new_vp = v.reshape(bp, sp, KV_HEADS, HEAD_DIM)[:, sp - r:][None]
    h_p, lt_p = _out_proj(att, gm, xp2, w_out_b, row_vec(ln1_g[l]), row_vec(ln1_b[l]), r_hi, r_lo,
                          name="out_proj_prompt")

    xs2 = x_sample.reshape(n_s, D_MODEL)
    pos_s = PAST_LEN + jnp.arange(ts, dtype=jnp.int32)
    tabs_s = tuple(jnp.tile(t, (bs, 1)) for t in _rope_tables(pos_s))
    q, k, v, gu, gv = _in_proj(xs2, w_in_b, tabs_s, gm_g, gm_b, tm=n_s, gv_dtype=F32, name="in_proj_sample")
    tq = 8
    pad_rows = lambda t: jnp.pad(t.reshape(bs, ts, -1), ((0, 0), (0, tq - ts), (0, 0))).reshape(bs * tq, -1)
    ck2 = cache_k[l].reshape(bs * r, KV_WIDTH)
    cv2 = cache_v[l].reshape(bs * r, KV_WIDTH)
    att = _attention(sink_l, pad_rows(q), pad_rows(k), pad_rows(v), ck2, cv2, batch=bs, nb=1, tq=tq, prev_blocks=1,
                     first_block_has_no_prev=False, name="attn_sample")
    att = att.reshape(bs, tq, ATT_WIDTH)[:, :ts].reshape(n_s, ATT_WIDTH)
    w_small = gm_w_s[l][:, :ts, :ts]
    w_big = jnp.einsum("ab,hij->haibj", jnp.eye(bs, dtype=F32), w_small).reshape(GM_HEADS, n_s, n_s)
    b_tab_s = jnp.tile(jnp.repeat(gm_b_s[l][:, :ts].T, HEAD_DIM, axis=1), (bs, 1))
    gm = _gate(gu, gv, w_big, b_tab_s, chunk=n_s, n_chunks=1, name="gate_sample")
    new_ks = jnp.concatenate([cache_k[l], k.reshape(bs, ts, KV_HEADS, HEAD_DIM)], axis=1)[:, ts:][None]
    new_vs = jnp.concatenate([cache_v[l], v.reshape(bs, ts, KV_HEADS, HEAD_DIM)], axis=1)[:, ts:][None]
    new_gs = gv.reshape(bs, ts, GM_WIDTH)[None]
    h_s, lt_s = _out_proj(att, gm, xs2, w_out_b, row_vec(ln1_g[l]), row_vec(ln1_b[l]), r_hi, r_lo,
                          name="out_proj_sample")

    logits_t = jnp.concatenate([lt_p, lt_s], axis=1)
    eidx_t, w_t, rank_t, counts = _route(logits_t, router_bias[l].astype(F32).reshape(N_EXPERTS, 1))
    a = n_total * TOP_K
    n_blocks = -(-(a + N_EXPERTS * (EXPERT_ROWS - 1)) // EXPERT_ROWS)
    pad_start, block_e, n_used = _block_plan(counts.reshape(N_EXPERTS), n_blocks)
    dest_t = _dest(eidx_t, rank_t, pad_start.astype(F32).reshape(N_EXPERTS, 1))
    x_sorted = jnp.zeros((n_blocks * EXPERT_ROWS, D_MODEL), F32)
    x_sorted = _dispatch(dest_t, h_p, x_sorted, col_offset=0, name="dispatch_prompt")
    x_sorted = _dispatch(dest_t, h_s, x_sorted, col_offset=n_p, name="dispatch_sample")
    out_sorted = _experts(block_e, n_used, x_sorted, w_gate_e[l], w_up_e[l], w_down_e[l])
    shared = (w_gate_s[l].astype(BF16), w_up_s[l].astype(BF16), w_down_s[l].astype(BF16))
    ln2 = (row_vec(ln2_g[l]), row_vec(ln2_b[l]))
    wts = w_t.T
    y_p = _combine(dest_t, wts, h_p, *shared, *ln2, out_sorted, row_offset=0, name="combine_prompt")
    y_s = _combine(dest_t, wts, h_s, *shared, *ln2, out_sorted, row_offset=n_p, name="combine_sample")
    return (y_p.reshape(bp, sp, D_MODEL), y_s.reshape(bs, ts, D_MODEL), new_kp, new_vp, new_ks, new_vs, new_gs)
```

```python
import functools

import jax
import jax.numpy as jnp
import numpy as np
from jax import lax
from jax.experimental import pallas as pl
from jax.experimental.pallas import tpu as pltpu

D_MODEL = 1024
HEAD_DIM = 64
ATT_HEADS = 8
KV_HEADS = 2
Q_PER_KV = ATT_HEADS // KV_HEADS
GM_HEADS = 8
ATT_WIDTH = ATT_HEADS * HEAD_DIM
KV_WIDTH = KV_HEADS * HEAD_DIM
GM_WIDTH = GM_HEADS * HEAD_DIM
ROPE_WIDTH = ATT_WIDTH + KV_WIDTH
IN_WIDTH = ATT_WIDTH + 2 * KV_WIDTH + 2 * GM_WIDTH
WINDOW = 128
CHUNK = 128
PAST_LEN = 16384
ROPE_THETA = 10000.0
ATT_SCALE = HEAD_DIM ** -0.5
N_EXPERTS = 256
TOP_K = 8
N_GROUPS = 8
TOPK_GROUPS = 4
F_EXPERT = 256
ROUTED_SCALE = 2.5
LN_EPS = 1e-5
DEPTH = 1
ALPHA = (2.0 * DEPTH) ** 0.25

PACKED = D_MODEL // 2
LANES = 128
ROW_TILE = 512
EXPERT_ROWS = 256
COMBINE_ROWS = 256
DISPATCH_ROWS = 256
ROUTE_TILE = 256
DMA_UNROLL = 4
DMA_PRIORITIES = 2
VMEM_LIMIT = 56 * 1024 * 1024

F32 = jnp.float32
BF16 = jnp.bfloat16


def _params(n_axes):
    return pltpu.CompilerParams(dimension_semantics=("arbitrary",) * n_axes, vmem_limit_bytes=VMEM_LIMIT)


def _layer_norm(x, g, b):
    mu = jnp.mean(x, axis=-1, keepdims=True)
    xc = x - mu
    var = jnp.mean(xc * xc, axis=-1, keepdims=True)
    return xc * lax.rsqrt(var + LN_EPS) * g + b


def _pack_halves(x):
    xb = x.astype(BF16)
    lo = lax.bitcast_convert_type(xb[:, :PACKED].astype(F32), jnp.uint32)
    hi = lax.bitcast_convert_type(xb[:, PACKED:].astype(F32), jnp.uint32)
    return (lo >> 16) | hi


def _unpack_halves(w):
    lo = lax.bitcast_convert_type(w << 16, F32)
    hi = lax.bitcast_convert_type(w & jnp.uint32(0xFFFF0000), F32)
    return lo, hi


def _gelu(x):
    return 0.5 * x * (1.0 + lax.erf(x * np.float32(np.sqrt(0.5))))


def _in_proj_kernel(x_ref, w_ref, cos_ref, sa_ref, sb_ref, g_ref, b_ref, q_ref, k_ref, v_ref, gu_ref, gv_ref):
    x = x_ref[...].astype(BF16)
    zr = jnp.dot(x, w_ref[:, :ROPE_WIDTH], preferred_element_type=F32)
    pieces = []
    for c in range(ROPE_WIDTH // LANES):
        sl = slice(c * LANES, (c + 1) * LANES)
        zc = zr[:, sl]
        pieces.append(zc * cos_ref[:, sl]
                      + pltpu.roll(zc, LANES - HEAD_DIM // 2, 1) * sa_ref[:, sl]
                      + pltpu.roll(zc, HEAD_DIM // 2, 1) * sb_ref[:, sl])
    for c in range(ATT_WIDTH // LANES):
        q_ref[:, c * LANES:(c + 1) * LANES] = pieces[c].astype(q_ref.dtype)
    k_ref[...] = pieces[ATT_WIDTH // LANES]
    v_ref[...] = jnp.dot(x, w_ref[:, ROPE_WIDTH:ROPE_WIDTH + KV_WIDTH], preferred_element_type=F32)
    g0 = ROPE_WIDTH + KV_WIDTH
    zu = jnp.dot(x, w_ref[:, g0:g0 + GM_WIDTH], preferred_element_type=F32)
    gu_ref[...] = _gelu(zu).astype(gu_ref.dtype)
    zv = jnp.dot(x, w_ref[:, g0 + GM_WIDTH:g0 + 2 * GM_WIDTH], preferred_element_type=F32)
    gv = _layer_norm(_gelu(zv), g_ref[...], b_ref[...])
    gv_ref[...] = gv.astype(gv_ref.dtype)


def _in_proj(x2, w_in_b, tabs, ln_g, ln_b, *, tm, gv_dtype, name):
    n = x2.shape[0]
    cos_t, sa_t, sb_t = tabs
    period = cos_t.shape[0] // tm
    row = lambda i: (i, 0)
    tab = lambda i: (i % period, 0)
    fixed = lambda i: (0, 0)
    return pl.pallas_call(
        _in_proj_kernel,
        grid=(n // tm,),
        in_specs=[
            pl.BlockSpec((tm, D_MODEL), row),
            pl.BlockSpec((D_MODEL, IN_WIDTH), fixed),
            pl.BlockSpec((tm, ROPE_WIDTH), tab),
            pl.BlockSpec((tm, ROPE_WIDTH), tab),
            pl.BlockSpec((tm, ROPE_WIDTH), tab),
            pl.BlockSpec((1, GM_WIDTH), fixed),
            pl.BlockSpec((1, GM_WIDTH), fixed),
        ],
        out_specs=[
            pl.BlockSpec((tm, ATT_WIDTH), row),
            pl.BlockSpec((tm, KV_WIDTH), row),
            pl.BlockSpec((tm, KV_WIDTH), row),
            pl.BlockSpec((tm, GM_WIDTH), row),
            pl.BlockSpec((tm, GM_WIDTH), row),
        ],
        out_shape=[
            jax.ShapeDtypeStruct((n, ATT_WIDTH), BF16),
            jax.ShapeDtypeStruct((n, KV_WIDTH), F32),
            jax.ShapeDtypeStruct((n, KV_WIDTH), F32),
            jax.ShapeDtypeStruct((n, GM_WIDTH), BF16),
            jax.ShapeDtypeStruct((n, GM_WIDTH), gv_dtype),
        ],
        compiler_params=_params(1),
        name=name,
    )(x2, w_in_b, cos_t, sa_t, sb_t, ln_g, ln_b)


def _rope_tables(pos):
    half = HEAD_DIM // 2
    inv = ROPE_THETA ** (-jnp.arange(half, dtype=F32) * 2.0 / HEAD_DIM)
    ang = pos.astype(F32)[:, None] * inv[None, :]
    cos, sin = jnp.cos(ang), jnp.sin(ang)
    zero = jnp.zeros_like(sin)
    heads = ROPE_WIDTH // HEAD_DIM
    scale = jnp.where(jnp.arange(ROPE_WIDTH) < ATT_WIDTH, ATT_SCALE, 1.0).astype(F32)[None, :]
    cos_t = jnp.tile(jnp.concatenate([cos, cos], axis=1), (1, heads)) * scale
    sa_t = jnp.tile(jnp.concatenate([-sin, zero], axis=1), (1, heads)) * scale
    sb_t = jnp.tile(jnp.concatenate([zero, sin], axis=1), (1, heads)) * scale
    return cos_t, sa_t, sb_t


def _attn_kernel(sink_ref, q_ref, kc_ref, vc_ref, kp_ref, vp_ref, o_ref, *, tq, first_block_has_no_prev):
    kk = jnp.concatenate([kp_ref[...], kc_ref[...]], axis=0).astype(BF16)
    vv = jnp.concatenate([vp_ref[...], vc_ref[...]], axis=0).astype(BF16)
    nk = WINDOW + tq
    qi = lax.broadcasted_iota(jnp.int32, (tq, nk), 0)
    ks = lax.broadcasted_iota(jnp.int32, (tq, nk), 1)
    mask = (ks >= qi) & (ks <= qi + WINDOW)
    if first_block_has_no_prev:
        mask = mask & ((pl.program_id(1) > 0) | (ks >= WINDOW))
    q = q_ref[...]
    outs = []
    for g in range(KV_HEADS):
        kg = kk[:, g * HEAD_DIM:(g + 1) * HEAD_DIM]
        vg = vv[:, g * HEAD_DIM:(g + 1) * HEAD_DIM]
        for j in range(Q_PER_KV):
            h = g * Q_PER_KV + j
            qh = q[:, h * HEAD_DIM:(h + 1) * HEAD_DIM]
            s = lax.dot_general(qh, kg, (((1,), (1,)), ((), ())), preferred_element_type=F32)
            s = jnp.where(mask, s, -jnp.inf)
            sink = sink_ref[h]
            m = jnp.maximum(jnp.max(s, axis=-1, keepdims=True), sink)
            p = jnp.exp(s - m)
            denom = jnp.sum(p, axis=-1, keepdims=True) + jnp.exp(sink - m)
            p = (p / denom).astype(BF16)
            outs.append(jnp.dot(p, vg, preferred_element_type=F32))
    o_ref[...] = jnp.concatenate(outs, axis=1).astype(o_ref.dtype)


def _attention(sink, q2, k2, v2, kprev2, vprev2, *, batch, nb, tq, prev_blocks, first_block_has_no_prev, name):
    cur = lambda b, n, s: (b * nb + n, 0)
    prev = lambda b, n, s: (b * prev_blocks + jnp.maximum(n - 1, 0), 0)
    kern = functools.partial(_attn_kernel, tq=tq, first_block_has_no_prev=first_block_has_no_prev)
    return pl.pallas_call(
        kern,
        grid_spec=pltpu.PrefetchScalarGridSpec(
            num_scalar_prefetch=1,
            grid=(batch, nb),
            in_specs=[
                pl.BlockSpec((tq, ATT_WIDTH), cur),
                pl.BlockSpec((tq, KV_WIDTH), cur),
                pl.BlockSpec((tq, KV_WIDTH), cur),
                pl.BlockSpec((WINDOW, KV_WIDTH), prev),
                pl.BlockSpec((WINDOW, KV_WIDTH), prev),
            ],
            out_specs=pl.BlockSpec((tq, ATT_WIDTH), cur),
        ),
        out_shape=jax.ShapeDtypeStruct(q2.shape, BF16),
        compiler_params=_params(2),
        name=name,
    )(sink, q2, k2, v2, kprev2, vprev2)


def _gate_kernel(gu_ref, gv_ref, w_ref, b_ref, o_ref, *, chunk, n_chunks):
    ri = lax.broadcasted_iota(jnp.int32, (chunk, chunk), 0)
    ci = lax.broadcasted_iota(jnp.int32, (chunk, chunk), 1)
    ws = [jnp.where(ci <= ri, w_ref[h], 0.0).astype(BF16) for h in range(GM_HEADS)]
    for c in range(n_chunks):
        rows = slice(c * chunk, (c + 1) * chunk)
        gv = gv_ref[rows, :].astype(BF16)
        sv = jnp.concatenate(
            [jnp.dot(ws[h], gv[:, h * HEAD_DIM:(h + 1) * HEAD_DIM], preferred_element_type=F32)
             for h in range(GM_HEADS)], axis=1)
        o_ref[rows, :] = (gu_ref[rows, :].astype(F32) * (sv + b_ref[...])).astype(o_ref.dtype)


def _gate(gu2, gv2, w_s, b_tab, *, chunk, n_chunks, name):
    n = gu2.shape[0]
    tm = chunk * n_chunks
    row = lambda i: (i, 0)
    kern = functools.partial(_gate_kernel, chunk=chunk, n_chunks=n_chunks)
    return pl.pallas_call(
        kern,
        grid=(n // tm,),
        in_specs=[
            pl.BlockSpec((tm, GM_WIDTH), row),
            pl.BlockSpec((tm, GM_WIDTH), row),
            pl.BlockSpec((GM_HEADS, chunk, chunk), lambda i: (0, 0, 0)),
            pl.BlockSpec((chunk, GM_WIDTH), lambda i: (0, 0)),
        ],
        out_specs=pl.BlockSpec((tm, GM_WIDTH), row),
        out_shape=jax.ShapeDtypeStruct((n, GM_WIDTH), BF16),
        compiler_params=_params(1),
        name=name,
    )(gu2, gv2, w_s, b_tab)


def _out_proj_kernel(att_ref, gm_ref, x_ref, wo_ref, g_ref, b_ref, rh_ref, rl_ref, h_ref, hp_ref, lg_ref):
    mix = jnp.dot(att_ref[...], wo_ref[:ATT_WIDTH, :], preferred_element_type=F32)
    mix = mix + jnp.dot(gm_ref[...], wo_ref[ATT_WIDTH:, :], preferred_element_type=F32)
    h = _layer_norm(ALPHA * x_ref[...] + mix, g_ref[...], b_ref[...])
    h_ref[...] = h
    hp_ref[...] = _pack_halves(h)
    h_hi = h.astype(BF16)
    h_lo = (h - h_hi.astype(F32)).astype(BF16)
    nt = (((1,), (1,)), ((), ()))
    lg = lax.dot_general(rh_ref[...], h_hi, nt, preferred_element_type=F32)
    lg = lg + lax.dot_general(rh_ref[...], h_lo, nt, preferred_element_type=F32)
    lg = lg + lax.dot_general(rl_ref[...], h_hi, nt, preferred_element_type=F32)
    lg_ref[...] = lg


def _out_proj(att2, gm2, x2, w_out_b, ln_g, ln_b, r_hi, r_lo, *, name):
    n = x2.shape[0]
    tm = ROW_TILE
    row = lambda i: (i, 0)
    fixed = lambda i: (0, 0)
    return pl.pallas_call(
        _out_proj_kernel,
        grid=(n // tm,),
        in_specs=[
            pl.BlockSpec((tm, ATT_WIDTH), row),
            pl.BlockSpec((tm, GM_WIDTH), row),
            pl.BlockSpec((tm, D_MODEL), row),
            pl.BlockSpec((D_MODEL, D_MODEL), fixed),
            pl.BlockSpec((1, D_MODEL), fixed),
            pl.BlockSpec((1, D_MODEL), fixed),
            pl.BlockSpec((N_EXPERTS, D_MODEL), fixed),
            pl.BlockSpec((N_EXPERTS, D_MODEL), fixed),
        ],
        out_specs=[
            pl.BlockSpec((tm, D_MODEL), row),
            pl.BlockSpec((tm, PACKED), row),
            pl.BlockSpec((N_EXPERTS, tm), lambda i: (0, i)),
        ],
        out_shape=[
            jax.ShapeDtypeStruct((n, D_MODEL), F32),
            jax.ShapeDtypeStruct((n, PACKED), jnp.uint32),
            jax.ShapeDtypeStruct((N_EXPERTS, n), F32),
        ],
        compiler_params=_params(1),
        name=name,
    )(att2, gm2, x2, w_out_b, ln_g, ln_b, r_hi, r_lo)


def _experts_kernel(be_ref, nu_ref, x_ref, wg_ref, wu_ref, wd_ref, o_ref, wg_s, wu_s, wd_s):
    i = pl.program_id(0)

    @pl.when(i < nu_ref[0])
    def _():
        prev = be_ref[jnp.maximum(i - 1, 0)]

        @pl.when((i == 0) | (be_ref[i] != prev))
        def _():
            wg_s[...] = wg_ref[0].astype(BF16)
            wu_s[...] = wu_ref[0].astype(BF16)
            wd_s[...] = wd_ref[0].astype(BF16)

        xl, xh = _unpack_halves(x_ref[...])
        xl, xh = xl.astype(BF16), xh.astype(BF16)
        a = (jnp.dot(xl, wg_s[:PACKED, :], preferred_element_type=F32)
             + jnp.dot(xh, wg_s[PACKED:, :], preferred_element_type=F32))
        u = (jnp.dot(xl, wu_s[:PACKED, :], preferred_element_type=F32)
             + jnp.dot(xh, wu_s[PACKED:, :], preferred_element_type=F32))
        hb = (a * jax.nn.sigmoid(a) * u).astype(BF16)
        o_ref[...] = _pack_halves(jnp.dot(hb, wd_s[...], preferred_element_type=F32))

    @pl.when(i >= nu_ref[0])
    def _():
        o_ref[...] = jnp.zeros_like(o_ref)


def _experts(block_e, n_used, x_sorted, w_gate_e, w_up_e, w_down_e):
    n_blocks = block_e.shape[0]
    rows = n_blocks * EXPERT_ROWS
    xrow = lambda i, be, nu: (jnp.minimum(i, nu[0] - 1), 0)
    orow = lambda i, be, nu: (i, 0)
    wmap = lambda i, be, nu: (be[i], 0, 0)
    return pl.pallas_call(
        _experts_kernel,
        grid_spec=pltpu.PrefetchScalarGridSpec(
            num_scalar_prefetch=2,
            grid=(n_blocks,),
            in_specs=[
                pl.BlockSpec((EXPERT_ROWS, PACKED), xrow),
                pl.BlockSpec((1, D_MODEL, F_EXPERT), wmap),
                pl.BlockSpec((1, D_MODEL, F_EXPERT), wmap),
                pl.BlockSpec((1, F_EXPERT, D_MODEL), wmap),
            ],
            out_specs=pl.BlockSpec((EXPERT_ROWS, PACKED), orow),
            scratch_shapes=[
                pltpu.VMEM((D_MODEL, F_EXPERT), BF16),
                pltpu.VMEM((D_MODEL, F_EXPERT), BF16),
                pltpu.VMEM((F_EXPERT, D_MODEL), BF16),
            ],
        ),
        out_shape=jax.ShapeDtypeStruct((rows, PACKED), jnp.uint32),
        compiler_params=_params(1),
        name="experts",
    )(block_e, n_used, x_sorted, w_gate_e, w_up_e, w_down_e)


def _combine_kernel(dest_ref, w_ref, h_ref, wg_ref, wu_ref, wd_ref, ln_g_ref, ln_b_ref, os_ref, y_ref, gbuf, sem):
    t = h_ref.shape[0]

    def row_copy(j, k):
        return pltpu.make_async_copy(os_ref.at[pl.ds(dest_ref[k, j], 1)], gbuf.at[k, pl.ds(j, 1)], sem)

    def issue(j, c):
        for k in range(TOP_K):
            row_copy(j, k).start(priority=k % DMA_PRIORITIES)
        return c

    def drain(j, c):
        for k in range(TOP_K):
            row_copy(j, k).wait()
        return c

    lax.fori_loop(0, t, issue, 0, unroll=DMA_UNROLL)
    h = h_ref[...]
    hb = h.astype(BF16)
    a = jnp.dot(hb, wg_ref[...], preferred_element_type=F32)
    u = jnp.dot(hb, wu_ref[...], preferred_element_type=F32)
    ffn = jnp.dot((a * jax.nn.sigmoid(a) * u).astype(BF16), wd_ref[...], preferred_element_type=F32)
    lax.fori_loop(0, t, drain, 0, unroll=DMA_UNROLL)
    w = w_ref[...]
    lo_acc = jnp.zeros((t, PACKED), F32)
    hi_acc = jnp.zeros((t, PACKED), F32)
    for k in range(TOP_K):
        lo, hi = _unpack_halves(gbuf[k])
        lo_acc = lo_acc + w[:, k:k + 1] * lo
        hi_acc = hi_acc + w[:, k:k + 1] * hi
    ffn = ffn + jnp.concatenate([lo_acc, hi_acc], axis=1)
    y_ref[...] = _layer_norm(ALPHA * h + ffn, ln_g_ref[...], ln_b_ref[...])


def _combine(dest_t, w2, h2, wg_b, wu_b, wd_b, ln_g, ln_b, out_sorted, *, row_offset, name):
    n = h2.shape[0]
    tm = COMBINE_ROWS
    off = row_offset // tm
    row = lambda i: (i, 0)
    fixed = lambda i: (0, 0)
    return pl.pallas_call(
        _combine_kernel,
        grid=(n // tm,),
        in_specs=[
            pl.BlockSpec((TOP_K, tm), lambda i: (0, i + off), memory_space=pltpu.SMEM),
            pl.BlockSpec((tm, TOP_K), lambda i: (i + off, 0)),
            pl.BlockSpec((tm, D_MODEL), row),
            pl.BlockSpec((D_MODEL, F_EXPERT), fixed),
            pl.BlockSpec((D_MODEL, F_EXPERT), fixed),
            pl.BlockSpec((F_EXPERT, D_MODEL), fixed),
            pl.BlockSpec((1, D_MODEL), fixed),
            pl.BlockSpec((1, D_MODEL), fixed),
            pl.BlockSpec(memory_space=pl.ANY),
        ],
        out_specs=pl.BlockSpec((tm, D_MODEL), row),
        out_shape=jax.ShapeDtypeStruct((n, D_MODEL), F32),
        scratch_shapes=[pltpu.VMEM((TOP_K, tm, PACKED), jnp.uint32), pltpu.SemaphoreType.DMA],
        compiler_params=_params(1),
        name=name,
    )(dest_t, w2, h2, wg_b, wu_b, wd_b, ln_g, ln_b, out_sorted)


def _route_kernel(lg_ref, bias_ref, eidx_ref, w_ref, rank_ref, cnt_ref, carry_ref):
    @pl.when(pl.program_id(0) == 0)
    def _():
        carry_ref[...] = jnp.zeros_like(carry_ref)

    t = lg_ref.shape[1]
    gsz = N_EXPERTS // N_GROUPS
    neg = -jnp.inf
    s = jax.nn.sigmoid(lg_ref[...])
    biased = s + bias_ref[...]
    io_g = lax.broadcasted_iota(jnp.int32, (gsz, t), 0)
    grp_rows = []
    for g in range(N_GROUPS):
        blk = biased[g * gsz:(g + 1) * gsz, :]
        m1 = jnp.max(blk, axis=0, keepdims=True)
        i1 = jnp.min(jnp.where(blk == m1, io_g, gsz), axis=0, keepdims=True)
        m2 = jnp.max(jnp.where(io_g == i1, neg, blk), axis=0, keepdims=True)
        grp_rows.append(m1 + m2)
    gs = jnp.concatenate(grp_rows, axis=0)
    io8 = lax.broadcasted_iota(jnp.int32, (N_GROUPS, t), 0)
    gsel = jnp.zeros((N_GROUPS, t), jnp.int32)
    for _ in range(TOPK_GROUPS):
        m = jnp.max(gs, axis=0, keepdims=True)
        gi = jnp.min(jnp.where(gs == m, io8, N_GROUPS), axis=0, keepdims=True)
        hit = io8 == gi
        gsel = jnp.where(hit, 1, gsel)
        gs = jnp.where(hit, neg, gs)
    masked = jnp.concatenate(
        [jnp.where(gsel[g:g + 1, :] > 0, biased[g * gsz:(g + 1) * gsz, :], neg) for g in range(N_GROUPS)], axis=0)

    eio = lax.broadcasted_iota(jnp.int32, (N_EXPERTS, t), 0)
    cur = masked
    idx_rows, w_rows = [], []
    for _ in range(TOP_K):
        m = jnp.max(cur, axis=0, keepdims=True)
        idx = jnp.min(jnp.where(cur == m, eio, N_EXPERTS), axis=0, keepdims=True)
        hit = eio == idx
        w_rows.append(jnp.sum(jnp.where(hit, s, 0.0), axis=0, keepdims=True))
        cur = jnp.where(hit, neg, cur)
        idx_rows.append(idx)
    sel = jnp.where(cur != masked, 1.0, 0.0)

    tri = jnp.where(lax.broadcasted_iota(jnp.int32, (t, t), 0) < lax.broadcasted_iota(jnp.int32, (t, t), 1), 1.0, 0.0)
    pref = jnp.dot(sel.astype(BF16), tri.astype(BF16), preferred_element_type=F32) + carry_ref[...]
    rank_rows = [jnp.sum(jnp.where(eio == idx_rows[k], pref, 0.0), axis=0, keepdims=True) for k in range(TOP_K)]
    carry_ref[...] += jnp.sum(sel, axis=1, keepdims=True)

    wk = jnp.concatenate(w_rows, axis=0)
    eidx_ref[...] = jnp.concatenate(idx_rows, axis=0)
    w_ref[...] = wk / jnp.sum(wk, axis=0, keepdims=True) * ROUTED_SCALE
    rank_ref[...] = jnp.concatenate(rank_rows, axis=0).astype(jnp.int32)
    cnt_ref[...] = carry_ref[...].astype(jnp.int32)


def _route(logits_t, bias_col):
    n = logits_t.shape[1]
    t = ROUTE_TILE
    col = lambda i: (0, i)
    fixed = lambda i: (0, 0)
    return pl.pallas_call(
        _route_kernel,
        grid=(n // t,),
        in_specs=[pl.BlockSpec((N_EXPERTS, t), col), pl.BlockSpec((N_EXPERTS, 1), fixed)],
        out_specs=[
            pl.BlockSpec((TOP_K, t), col),
            pl.BlockSpec((TOP_K, t), col),
            pl.BlockSpec((TOP_K, t), col),
            pl.BlockSpec((N_EXPERTS, 1), fixed),
        ],
        out_shape=[
            jax.ShapeDtypeStruct((TOP_K, n), jnp.int32),
            jax.ShapeDtypeStruct((TOP_K, n), F32),
            jax.ShapeDtypeStruct((TOP_K, n), jnp.int32),
            jax.ShapeDtypeStruct((N_EXPERTS, 1), jnp.int32),
        ],
        scratch_shapes=[pltpu.VMEM((N_EXPERTS, 1), F32)],
        compiler_params=_params(1),
        name="route",
    )(logits_t, bias_col)


def _dest_kernel(eidx_ref, rank_ref, start_ref, dest_ref):
    t = eidx_ref.shape[1]
    eio = lax.broadcasted_iota(jnp.int32, (N_EXPERTS, t), 0)
    start = start_ref[...]
    rows = [jnp.sum(jnp.where(eio == eidx_ref[k:k + 1, :], start, 0.0), axis=0, keepdims=True) for k in range(TOP_K)]
    dest_ref[...] = jnp.concatenate(rows, axis=0).astype(jnp.int32) + rank_ref[...]


def _dest(eidx_t, rank_t, pad_start_col):
    n = eidx_t.shape[1]
    t = ROW_TILE
    col = lambda i: (0, i)
    return pl.pallas_call(
        _dest_kernel,
        grid=(n // t,),
        in_specs=[pl.BlockSpec((TOP_K, t), col), pl.BlockSpec((TOP_K, t), col),
                  pl.BlockSpec((N_EXPERTS, 1), lambda i: (0, 0))],
        out_specs=pl.BlockSpec((TOP_K, t), col),
        out_shape=jax.ShapeDtypeStruct((TOP_K, n), jnp.int32),
        compiler_params=_params(1),
        name="dest",
    )(eidx_t, rank_t, pad_start_col)


def _dispatch_kernel(fill_ref, len_ref, nu_ref, dest_ref, hp_ref, hs_ref, xs_ref, zbuf, sem, fill_sem, *,
                     prompt_tiles, n_blocks):
    i = pl.program_id(0)

    @pl.when(i == 0)
    def _():
        zbuf[...] = jnp.zeros_like(zbuf)

        def fill_copy(row0, size):
            return pltpu.make_async_copy(zbuf.at[pl.ds(0, size)], xs_ref.at[pl.ds(pl.multiple_of(row0, 8), size)],
                                         fill_sem)

        def on_padding(fn):
            def body(e, c):
                base, length = fill_ref[e], len_ref[e]
                size = EXPERT_ROWS
                while size >= 8:
                    piece = fill_copy(base + (length & ~(2 * size - 1)), size)
                    pl.when((length & size) != 0)(functools.partial(fn, piece))
                    size //= 2
                return c
            lax.fori_loop(0, N_EXPERTS, body, 0)

        def on_unused_blocks(fn):
            lax.fori_loop(nu_ref[0], n_blocks, lambda b, c: (fn(fill_copy(b * EXPERT_ROWS, EXPERT_ROWS)), c)[1], 0)

        on_padding(lambda cp: cp.start())
        on_unused_blocks(lambda cp: cp.start())
        on_padding(lambda cp: cp.wait())
        on_unused_blocks(lambda cp: cp.wait())

    def scatter_rows(src_ref):
        t = src_ref.shape[0]

        def row_copy(j, k):
            return pltpu.make_async_copy(src_ref.at[pl.ds(j, 1)], xs_ref.at[pl.ds(dest_ref[k, j], 1)], sem)

        def issue(j, c):
            for k in range(TOP_K):
                row_copy(j, k).start(priority=k % DMA_PRIORITIES)
            return c

        def drain(j, c):
            for k in range(TOP_K):
                row_copy(j, k).wait()
            return c

        lax.fori_loop(0, t, issue, 0, unroll=DMA_UNROLL)
        lax.fori_loop(0, t, drain, 0, unroll=DMA_UNROLL)

    @pl.when(i < prompt_tiles)
    def _():
        scatter_rows(hp_ref)

    @pl.when(i >= prompt_tiles)
    def _():
        scatter_rows(hs_ref)


def _dispatch(fill_start, fill_len, n_used, dest_t, hp_p, hp_s, *, n_blocks):
    t = DISPATCH_ROWS
    prompt_tiles = hp_p.shape[0] // t
    sample_tiles = hp_s.shape[0] // t
    kern = functools.partial(_dispatch_kernel, prompt_tiles=prompt_tiles, n_blocks=n_blocks)
    return pl.pallas_call(
        kern,
        grid_spec=pltpu.PrefetchScalarGridSpec(
            num_scalar_prefetch=3,
            grid=(prompt_tiles + sample_tiles,),
            in_specs=[
                pl.BlockSpec((TOP_K, t), lambda i, *_: (0, i), memory_space=pltpu.SMEM),
                pl.BlockSpec((t, PACKED), lambda i, *_: (jnp.minimum(i, prompt_tiles - 1), 0)),
                pl.BlockSpec((t, PACKED), lambda i, *_: (jnp.maximum(i - prompt_tiles, 0), 0)),
            ],
            out_specs=pl.BlockSpec(memory_space=pl.ANY),
            scratch_shapes=[pltpu.VMEM((EXPERT_ROWS, PACKED), jnp.uint32), pltpu.SemaphoreType.DMA,
                            pltpu.SemaphoreType.DMA],
        ),
        out_shape=jax.ShapeDtypeStruct((n_blocks * EXPERT_ROWS, PACKED), jnp.uint32),
        compiler_params=_params(1),
        name="dispatch",
    )(fill_start, fill_len, n_used, dest_t, hp_p, hp_s)


def _block_plan(counts, n_blocks):
    padded = (counts + EXPERT_ROWS - 1) // EXPERT_ROWS * EXPERT_ROWS
    pad_end = jnp.cumsum(padded).astype(jnp.int32)
    pad_start = pad_end - padded
    n_used = pad_end[-1] // EXPERT_ROWS
    blk = jnp.arange(n_blocks, dtype=jnp.int32)
    block_e = jnp.sum((pad_end[None, :] <= blk[:, None] * EXPERT_ROWS).astype(jnp.int32), axis=1)
    block_e = jnp.minimum(block_e, N_EXPERTS - 1)
    last_e = jnp.max(jnp.where(blk < n_used, block_e, 0))
    block_e = jnp.where(blk < n_used, block_e, last_e)
    fill_start = ((pad_start + counts) // 8 * 8).astype(jnp.int32)
    fill_len = pad_end - fill_start
    return pad_start, fill_start, fill_len, block_e, n_used.reshape(1).astype(jnp.int32)


def kernel(x_prompt, x_sample, cache_k, cache_v, w_in, sink, gm_ln_g, gm_ln_b, gm_w_s, gm_b_s, w_out, ln1_g, ln1_b,
           router_w, router_bias, w_gate_e, w_up_e, w_down_e, w_gate_s, w_up_s, w_down_s, ln2_g, ln2_b):
    bp, sp = x_prompt.shape[:2]
    bs, ts = x_sample.shape[:2]
    r = cache_k.shape[2]
    assert r == WINDOW and sp % ROW_TILE == 0 and (bs * ts) % ROW_TILE == 0
    n_p, n_s = bp * sp, bs * ts
    n_total = n_p + n_s
    l = 0

    w_in_b = w_in[l].astype(BF16)
    w_out_b = w_out[l].astype(BF16)
    router_t = router_w[l].T
    r_hi = router_t.astype(BF16)
    r_lo = (router_t - r_hi.astype(F32)).astype(BF16)
    row_vec = lambda v: v.reshape(1, -1)
    gm_g, gm_b = row_vec(gm_ln_g[l]), row_vec(gm_ln_b[l])
    sink_l = sink[l].astype(F32)

    xp2 = x_prompt.reshape(n_p, D_MODEL)
    tabs_p = _rope_tables(jnp.arange(sp, dtype=jnp.int32))
    q, k, v, gu, gv = _in_proj(xp2, w_in_b, tabs_p, gm_g, gm_b, tm=ROW_TILE, gv_dtype=BF16, name="in_proj_prompt")
    nb = sp // WINDOW
    att = _attention(sink_l, q, k, v, k, v, batch=bp, nb=nb, tq=WINDOW, prev_blocks=nb,
                     first_block_has_no_prev=True, name="attn_prompt")
    b_tab_p = jnp.repeat(gm_b_s[l].T, HEAD_DIM, axis=1)
    gm = _gate(gu, gv, gm_w_s[l], b_tab_p, chunk=CHUNK, n_chunks=ROW_TILE // CHUNK, name="gate_prompt")
    new_kp = k.reshape(bp, sp, KV_HEADS, HEAD_DIM)[:, sp - r:][None]
    new_vp = v.reshape(bp, sp, KV_HEADS, HEAD_DIM)[:, sp - r:][None]
    h_p, hp_p, lt_p = _out_proj(att, gm, xp2, w_out_b, row_vec(ln1_g[l]), row_vec(ln1_b[l]), r_hi, r_lo,
                          name="out_proj_prompt")

    xs2 = x_sample.reshape(n_s, D_MODEL)
    pos_s = PAST_LEN + jnp.arange(ts, dtype=jnp.int32)
    tabs_s = tuple(jnp.tile(t, (bs, 1)) for t in _rope_tables(pos_s))
    q, k, v, gu, gv = _in_proj(xs2, w_in_b, tabs_s, gm_g, gm_b, tm=n_s, gv_dtype=F32, name="in_proj_sample")
    tq = 8
    pad_rows = lambda t: jnp.pad(t.reshape(bs, ts, -1), ((0, 0), (0, tq - ts), (0, 0))).reshape(bs * tq, -1)
    ck2 = cache_k[l].reshape(bs * r, KV_WIDTH)
    cv2 = cache_v[l].reshape(bs * r, KV_WIDTH)
    att = _attention(sink_l, pad_rows(q), pad_rows(k), pad_rows(v), ck2, cv2, batch=bs, nb=1, tq=tq, prev_blocks=1,
                     first_block_has_no_prev=False, name="attn_sample")
    att = att.reshape(bs, tq, ATT_WIDTH)[:, :ts].reshape(n_s, ATT_WIDTH)
    w_small = gm_w_s[l][:, :ts, :ts]
    w_big = jnp.einsum("ab,hij->haibj", jnp.eye(bs, dtype=F32), w_small).reshape(GM_HEADS, n_s, n_s)
    b_tab_s = jnp.tile(jnp.repeat(gm_b_s[l][:, :ts].T, HEAD_DIM, axis=1), (bs, 1))
    gm = _gate(gu, gv, w_big, b_tab_s, chunk=n_s, n_chunks=1, name="gate_sample")
    new_ks = jnp.concatenate([cache_k[l], k.reshape(bs, ts, KV_HEADS, HEAD_DIM)], axis=1)[:, ts:][None]
    new_vs = jnp.concatenate([cache_v[l], v.reshape(bs, ts, KV_HEADS, HEAD_DIM)], axis=1)[:, ts:][None]
    new_gs = gv.reshape(bs, ts, GM_WIDTH)[None]
    h_s, hp_s, lt_s = _out_proj(att, gm, xs2, w_out_b, row_vec(ln1_g[l]), row_vec(ln1_b[l]), r_hi, r_lo,
                          name="out_proj_sample")

    logits_t = jnp.concatenate([lt_p, lt_s], axis=1)
    eidx_t, w_t, rank_t, counts = _route(logits_t, router_bias[l].astype(F32).reshape(N_EXPERTS, 1))
    a = n_total * TOP_K
    n_blocks = -(-(a + N_EXPERTS * (EXPERT_ROWS - 1)) // EXPERT_ROWS)
    pad_start, fill_start, fill_len, block_e, n_used = _block_plan(counts.reshape(N_EXPERTS), n_blocks)
    dest_t = _dest(eidx_t, rank_t, pad_start.astype(F32).reshape(N_EXPERTS, 1))
    x_sorted = _dispatch(fill_start, fill_len, n_used, dest_t, hp_p, hp_s, n_blocks=n_blocks)
    out_sorted = _experts(block_e, n_used, x_sorted, w_gate_e[l], w_up_e[l], w_down_e[l])
    shared = (w_gate_s[l].astype(BF16), w_up_s[l].astype(BF16), w_down_s[l].astype(BF16))
    ln2 = (row_vec(ln2_g[l]), row_vec(ln2_b[l]))
    wts = w_t.T
    y_p = _combine(dest_t, wts, h_p, *shared, *ln2, out_sorted, row_offset=0, name="combine_prompt")
    y_s = _combine(dest_t, wts, h_s, *shared, *ln2, out_sorted, row_offset=n_p, name="combine_sample")
    return (y_p.reshape(bp, sp, D_MODEL), y_s.reshape(bs, ts, D_MODEL), new_kp, new_vp, new_ks, new_vs, new_gs)
```

```python
import functools

import jax
import jax.numpy as jnp
import numpy as np
from jax import lax
from jax.experimental import pallas as pl
from jax.experimental.pallas import tpu as pltpu

D_MODEL = 1024
HEAD_DIM = 64
ATT_HEADS = 8
KV_HEADS = 2
Q_PER_KV = ATT_HEADS // KV_HEADS
GM_HEADS = 8
ATT_WIDTH = ATT_HEADS * HEAD_DIM
KV_WIDTH = KV_HEADS * HEAD_DIM
GM_WIDTH = GM_HEADS * HEAD_DIM
ROPE_WIDTH = ATT_WIDTH + KV_WIDTH
IN_WIDTH = ATT_WIDTH + 2 * KV_WIDTH + 2 * GM_WIDTH
WINDOW = 128
CHUNK = 128
PAST_LEN = 16384
ROPE_THETA = 10000.0
ATT_SCALE = HEAD_DIM ** -0.5
N_EXPERTS = 256
TOP_K = 8
N_GROUPS = 8
TOPK_GROUPS = 4
F_EXPERT = 256
ROUTED_SCALE = 2.5
LN_EPS = 1e-5
DEPTH = 1
ALPHA = (2.0 * DEPTH) ** 0.25

PACKED = D_MODEL // 2
LANES = 128
ROW_TILE = 512
EXPERT_ROWS = 256
COMBINE_ROWS = 256
DISPATCH_ROWS = 256
ROUTE_TILE = 256
SAMPLE_SEQS_PER_STEP = 16
DMA_UNROLL = 4
DMA_PRIORITIES = 2
VMEM_LIMIT = 56 * 1024 * 1024

F32 = jnp.float32
BF16 = jnp.bfloat16


def _params(n_axes):
    return pltpu.CompilerParams(dimension_semantics=("arbitrary",) * n_axes, vmem_limit_bytes=VMEM_LIMIT)


def _layer_norm(x, g, b):
    mu = jnp.mean(x, axis=-1, keepdims=True)
    xc = x - mu
    var = jnp.mean(xc * xc, axis=-1, keepdims=True)
    return xc * lax.rsqrt(var + LN_EPS) * g + b


def _pack_halves(x):
    xb = x.astype(BF16)
    lo = lax.bitcast_convert_type(xb[:, :PACKED].astype(F32), jnp.uint32)
    hi = lax.bitcast_convert_type(xb[:, PACKED:].astype(F32), jnp.uint32)
    return (lo >> 16) | hi


def _unpack_halves(w):
    lo = lax.bitcast_convert_type(w << 16, F32)
    hi = lax.bitcast_convert_type(w & jnp.uint32(0xFFFF0000), F32)
    return lo, hi


def _gelu(x):
    return 0.5 * x * (1.0 + lax.erf(x * np.float32(np.sqrt(0.5))))


def _in_proj_kernel(x_ref, w_ref, cos_ref, sa_ref, sb_ref, g_ref, b_ref, q_ref, k_ref, v_ref, gu_ref, gv_ref):
    x = x_ref[...].astype(BF16)
    zr = jnp.dot(x, w_ref[:, :ROPE_WIDTH], preferred_element_type=F32)
    pieces = []
    for c in range(ROPE_WIDTH // LANES):
        sl = slice(c * LANES, (c + 1) * LANES)
        zc = zr[:, sl]
        pieces.append(zc * cos_ref[:, sl]
                      + pltpu.roll(zc, LANES - HEAD_DIM // 2, 1) * sa_ref[:, sl]
                      + pltpu.roll(zc, HEAD_DIM // 2, 1) * sb_ref[:, sl])
    for c in range(ATT_WIDTH // LANES):
        q_ref[:, c * LANES:(c + 1) * LANES] = pieces[c].astype(q_ref.dtype)
    k_ref[...] = pieces[ATT_WIDTH // LANES]
    v_ref[...] = jnp.dot(x, w_ref[:, ROPE_WIDTH:ROPE_WIDTH + KV_WIDTH], preferred_element_type=F32)
    g0 = ROPE_WIDTH + KV_WIDTH
    zu = jnp.dot(x, w_ref[:, g0:g0 + GM_WIDTH], preferred_element_type=F32)
    gu_ref[...] = _gelu(zu).astype(gu_ref.dtype)
    zv = jnp.dot(x, w_ref[:, g0 + GM_WIDTH:g0 + 2 * GM_WIDTH], preferred_element_type=F32)
    gv = _layer_norm(_gelu(zv), g_ref[...], b_ref[...])
    gv_ref[...] = gv.astype(gv_ref.dtype)


def _in_proj(x2, w_in_b, tabs, ln_g, ln_b, *, tm, gv_dtype, name):
    n = x2.shape[0]
    cos_t, sa_t, sb_t = tabs
    period = cos_t.shape[0] // tm
    row = lambda i: (i, 0)
    tab = lambda i: (i % period, 0)
    fixed = lambda i: (0, 0)
    return pl.pallas_call(
        _in_proj_kernel,
        grid=(n // tm,),
        in_specs=[
            pl.BlockSpec((tm, D_MODEL), row),
            pl.BlockSpec((D_MODEL, IN_WIDTH), fixed),
            pl.BlockSpec((tm, ROPE_WIDTH), tab),
            pl.BlockSpec((tm, ROPE_WIDTH), tab),
            pl.BlockSpec((tm, ROPE_WIDTH), tab),
            pl.BlockSpec((1, GM_WIDTH), fixed),
            pl.BlockSpec((1, GM_WIDTH), fixed),
        ],
        out_specs=[
            pl.BlockSpec((tm, ATT_WIDTH), row),
            pl.BlockSpec((tm, KV_WIDTH), row),
            pl.BlockSpec((tm, KV_WIDTH), row),
            pl.BlockSpec((tm, GM_WIDTH), row),
            pl.BlockSpec((tm, GM_WIDTH), row),
        ],
        out_shape=[
            jax.ShapeDtypeStruct((n, ATT_WIDTH), BF16),
            jax.ShapeDtypeStruct((n, KV_WIDTH), F32),
            jax.ShapeDtypeStruct((n, KV_WIDTH), F32),
            jax.ShapeDtypeStruct((n, GM_WIDTH), BF16),
            jax.ShapeDtypeStruct((n, GM_WIDTH), gv_dtype),
        ],
        compiler_params=_params(1),
        name=name,
    )(x2, w_in_b, cos_t, sa_t, sb_t, ln_g, ln_b)


def _rope_tables(pos):
    half = HEAD_DIM // 2
    inv = ROPE_THETA ** (-jnp.arange(half, dtype=F32) * 2.0 / HEAD_DIM)
    ang = pos.astype(F32)[:, None] * inv[None, :]
    cos, sin = jnp.cos(ang), jnp.sin(ang)
    zero = jnp.zeros_like(sin)
    heads = ROPE_WIDTH // HEAD_DIM
    scale = jnp.where(jnp.arange(ROPE_WIDTH) < ATT_WIDTH, ATT_SCALE, 1.0).astype(F32)[None, :]
    cos_t = jnp.tile(jnp.concatenate([cos, cos], axis=1), (1, heads)) * scale
    sa_t = jnp.tile(jnp.concatenate([-sin, zero], axis=1), (1, heads)) * scale
    sb_t = jnp.tile(jnp.concatenate([zero, sin], axis=1), (1, heads)) * scale
    return cos_t, sa_t, sb_t


def _attn_kernel(sink_ref, q_ref, kc_ref, vc_ref, kp_ref, vp_ref, o_ref, *, tq, seqs, stack, first_block_has_no_prev):
    nk = WINDOW + tq
    rows = stack * tq
    qi = lax.broadcasted_iota(jnp.int32, (rows, nk), 0) & (tq - 1)
    ks = lax.broadcasted_iota(jnp.int32, (rows, nk), 1)
    mask = (ks >= qi) & (ks <= qi + WINDOW)
    if first_block_has_no_prev:
        mask = mask & ((pl.program_id(1) > 0) | (ks >= WINDOW))
    sinks = [jnp.concatenate([jnp.full((tq, 1), sink_ref[h0 + j], F32) for j in range(stack)], axis=0)
             for h0 in range(0, ATT_HEADS, stack)]
    for b in range(seqs):
        qrows = slice(b * tq, (b + 1) * tq)
        prows = slice(b * WINDOW, (b + 1) * WINDOW)
        q = q_ref[qrows, :]
        kk = jnp.concatenate([kp_ref[prows, :], kc_ref[qrows, :]], axis=0).astype(BF16)
        vv = jnp.concatenate([vp_ref[prows, :], vc_ref[qrows, :]], axis=0).astype(BF16)
        outs = []
        for i, h0 in enumerate(range(0, ATT_HEADS, stack)):
            g = h0 // Q_PER_KV
            kg = kk[:, g * HEAD_DIM:(g + 1) * HEAD_DIM]
            vg = vv[:, g * HEAD_DIM:(g + 1) * HEAD_DIM]
            qg = jnp.concatenate([q[:, (h0 + j) * HEAD_DIM:(h0 + j + 1) * HEAD_DIM] for j in range(stack)], axis=0)
            s = lax.dot_general(qg, kg, (((1,), (1,)), ((), ())), preferred_element_type=F32)
            s = jnp.where(mask, s, -jnp.inf)
            m = jnp.maximum(jnp.max(s, axis=-1, keepdims=True), sinks[i])
            p = jnp.exp(s - m)
            denom = jnp.sum(p, axis=-1, keepdims=True) + jnp.exp(sinks[i] - m)
            o = jnp.dot((p / denom).astype(BF16), vg, preferred_element_type=F32)
            outs.extend(o[j * tq:(j + 1) * tq, :] for j in range(stack))
        o_ref[qrows, :] = jnp.concatenate(outs, axis=1).astype(o_ref.dtype)


def _attention(sink, q2, k2, v2, kprev2, vprev2, *, batch, nb, tq, seqs, prev_blocks, first_block_has_no_prev, name):
    assert tq & (tq - 1) == 0 and batch % seqs == 0 and (seqs == 1 or nb == prev_blocks == 1)
    cur = lambda b, n, s: (b * nb + n, 0)
    prev = lambda b, n, s: (b * prev_blocks + jnp.maximum(n - 1, 0), 0)
    stack = Q_PER_KV if Q_PER_KV * tq <= WINDOW else 1
    kern = functools.partial(_attn_kernel, tq=tq, seqs=seqs, stack=stack,
                             first_block_has_no_prev=first_block_has_no_prev)
    return pl.pallas_call(
        kern,
        grid_spec=pltpu.PrefetchScalarGridSpec(
            num_scalar_prefetch=1,
            grid=(batch // seqs, nb),
            in_specs=[
                pl.BlockSpec((seqs * tq, ATT_WIDTH), cur),
                pl.BlockSpec((seqs * tq, KV_WIDTH), cur),
                pl.BlockSpec((seqs * tq, KV_WIDTH), cur),
                pl.BlockSpec((seqs * WINDOW, KV_WIDTH), prev),
                pl.BlockSpec((seqs * WINDOW, KV_WIDTH), prev),
            ],
            out_specs=pl.BlockSpec((seqs * tq, ATT_WIDTH), cur),
        ),
        out_shape=jax.ShapeDtypeStruct(q2.shape, BF16),
        compiler_params=_params(2),
        name=name,
    )(sink, q2, k2, v2, kprev2, vprev2)


def _gate_kernel(gu_ref, gv_ref, w_ref, b_ref, o_ref, *, chunk, n_chunks):
    ri = lax.broadcasted_iota(jnp.int32, (chunk, chunk), 0)
    ci = lax.broadcasted_iota(jnp.int32, (chunk, chunk), 1)
    ws = [jnp.where(ci <= ri, w_ref[h], 0.0).astype(BF16) for h in range(GM_HEADS)]
    for c in range(n_chunks):
        rows = slice(c * chunk, (c + 1) * chunk)
        gv = gv_ref[rows, :].astype(BF16)
        sv = jnp.concatenate(
            [jnp.dot(ws[h], gv[:, h * HEAD_DIM:(h + 1) * HEAD_DIM], preferred_element_type=F32)
             for h in range(GM_HEADS)], axis=1)
        o_ref[rows, :] = (gu_ref[rows, :].astype(F32) * (sv + b_ref[...])).astype(o_ref.dtype)


def _gate(gu2, gv2, w_s, b_tab, *, chunk, n_chunks, name):
    n = gu2.shape[0]
    tm = chunk * n_chunks
    row = lambda i: (i, 0)
    kern = functools.partial(_gate_kernel, chunk=chunk, n_chunks=n_chunks)
    return pl.pallas_call(
        kern,
        grid=(n // tm,),
        in_specs=[
            pl.BlockSpec((tm, GM_WIDTH), row),
            pl.BlockSpec((tm, GM_WIDTH), row),
            pl.BlockSpec((GM_HEADS, chunk, chunk), lambda i: (0, 0, 0)),
            pl.BlockSpec((chunk, GM_WIDTH), lambda i: (0, 0)),
        ],
        out_specs=pl.BlockSpec((tm, GM_WIDTH), row),
        out_shape=jax.ShapeDtypeStruct((n, GM_WIDTH), BF16),
        compiler_params=_params(1),
        name=name,
    )(gu2, gv2, w_s, b_tab)


def _out_proj_kernel(att_ref, gm_ref, x_ref, wo_ref, g_ref, b_ref, rh_ref, rl_ref, h_ref, hp_ref, lg_ref):
    mix = jnp.dot(att_ref[...], wo_ref[:ATT_WIDTH, :], preferred_element_type=F32)
    mix = mix + jnp.dot(gm_ref[...], wo_ref[ATT_WIDTH:, :], preferred_element_type=F32)
    h = _layer_norm(ALPHA * x_ref[...] + mix, g_ref[...], b_ref[...])
    h_ref[...] = h
    hp_ref[...] = _pack_halves(h)
    h_hi = h.astype(BF16)
    h_lo = (h - h_hi.astype(F32)).astype(BF16)
    nt = (((1,), (1,)), ((), ()))
    lg = lax.dot_general(rh_ref[...], h_hi, nt, preferred_element_type=F32)
    lg = lg + lax.dot_general(rh_ref[...], h_lo, nt, preferred_element_type=F32)
    lg = lg + lax.dot_general(rl_ref[...], h_hi, nt, preferred_element_type=F32)
    lg_ref[...] = lg


def _out_proj(att2, gm2, x2, w_out_b, ln_g, ln_b, r_hi, r_lo, *, name):
    n = x2.shape[0]
    tm = ROW_TILE
    row = lambda i: (i, 0)
    fixed = lambda i: (0, 0)
    return pl.pallas_call(
        _out_proj_kernel,
        grid=(n // tm,),
        in_specs=[
            pl.BlockSpec((tm, ATT_WIDTH), row),
            pl.BlockSpec((tm, GM_WIDTH), row),
            pl.BlockSpec((tm, D_MODEL), row),
            pl.BlockSpec((D_MODEL, D_MODEL), fixed),
            pl.BlockSpec((1, D_MODEL), fixed),
            pl.BlockSpec((1, D_MODEL), fixed),
            pl.BlockSpec((N_EXPERTS, D_MODEL), fixed),
            pl.BlockSpec((N_EXPERTS, D_MODEL), fixed),
        ],
        out_specs=[
            pl.BlockSpec((tm, D_MODEL), row),
            pl.BlockSpec((tm, PACKED), row),
            pl.BlockSpec((N_EXPERTS, tm), lambda i: (0, i)),
        ],
        out_shape=[
            jax.ShapeDtypeStruct((n, D_MODEL), F32),
            jax.ShapeDtypeStruct((n, PACKED), jnp.uint32),
            jax.ShapeDtypeStruct((N_EXPERTS, n), F32),
        ],
        compiler_params=_params(1),
        name=name,
    )(att2, gm2, x2, w_out_b, ln_g, ln_b, r_hi, r_lo)


def _experts_kernel(first_ref, nblk_ref, nu_ref, x_hbm, wg_ref, wu_ref, wd_ref, o_hbm, xbuf, obuf, wg_s, wu_s, wd_s,
                    in_sem, out_sem, fill_sem, *, n_blocks):
    e = pl.program_id(0)
    nb = nblk_ref[e]
    b0 = first_ref[e]

    def rows_of(block):
        return pl.ds(pl.multiple_of(block * EXPERT_ROWS, EXPERT_ROWS), EXPERT_ROWS)

    def in_copy(c, slot):
        return pltpu.make_async_copy(x_hbm.at[rows_of(b0 + c)], xbuf.at[slot], in_sem.at[slot])

    def out_copy(c, slot):
        return pltpu.make_async_copy(obuf.at[slot], o_hbm.at[rows_of(b0 + c)], out_sem.at[slot])

    @pl.when(nb > 0)
    def _():
        in_copy(0, 0).start()
        wg_s[...] = wg_ref[0].astype(BF16)
        wu_s[...] = wu_ref[0].astype(BF16)
        wd_s[...] = wd_ref[0].astype(BF16)

        def body(c, carry):
            slot = c & 1
            in_copy(c, slot).wait()

            @pl.when(c + 1 < nb)
            def _():
                in_copy(c + 1, 1 - slot).start()

            @pl.when(c >= 2)
            def _():
                out_copy(c - 2, slot).wait()

            xl, xh = _unpack_halves(xbuf[slot])
            xl, xh = xl.astype(BF16), xh.astype(BF16)
            a = (jnp.dot(xl, wg_s[:PACKED, :], preferred_element_type=F32)
                 + jnp.dot(xh, wg_s[PACKED:, :], preferred_element_type=F32))
            u = (jnp.dot(xl, wu_s[:PACKED, :], preferred_element_type=F32)
                 + jnp.dot(xh, wu_s[PACKED:, :], preferred_element_type=F32))
            hb = (a * jax.nn.sigmoid(a) * u).astype(BF16)
            obuf[slot] = _pack_halves(jnp.dot(hb, wd_s[...], preferred_element_type=F32))
            out_copy(c, slot).start()
            return carry

        lax.fori_loop(0, nb, body, 0)

        @pl.when(nb >= 2)
        def _():
            out_copy(nb - 2, nb & 1).wait()

        out_copy(nb - 1, (nb - 1) & 1).wait()

    @pl.when(e == N_EXPERTS - 1)
    def _():
        obuf[0] = jnp.zeros((EXPERT_ROWS, PACKED), jnp.uint32)

        def on_unused_blocks(fn):
            def body(b, c):
                fn(pltpu.make_async_copy(obuf.at[0], o_hbm.at[rows_of(b)], fill_sem))
                return c
            lax.fori_loop(nu_ref[0], n_blocks, body, 0)

        on_unused_blocks(lambda cp: cp.start())
        on_unused_blocks(lambda cp: cp.wait())


def _experts(first_block, n_expert_blocks, n_used, x_sorted, w_gate_e, w_up_e, w_down_e):
    rows = x_sorted.shape[0]
    wmap = lambda e, *_: (e, 0, 0)
    kern = functools.partial(_experts_kernel, n_blocks=rows // EXPERT_ROWS)
    return pl.pallas_call(
        kern,
        grid_spec=pltpu.PrefetchScalarGridSpec(
            num_scalar_prefetch=3,
            grid=(N_EXPERTS,),
            in_specs=[
                pl.BlockSpec(memory_space=pl.ANY),
                pl.BlockSpec((1, D_MODEL, F_EXPERT), wmap),
                pl.BlockSpec((1, D_MODEL, F_EXPERT), wmap),
                pl.BlockSpec((1, F_EXPERT, D_MODEL), wmap),
            ],
            out_specs=pl.BlockSpec(memory_space=pl.ANY),
            scratch_shapes=[
                pltpu.VMEM((2, EXPERT_ROWS, PACKED), jnp.uint32),
                pltpu.VMEM((2, EXPERT_ROWS, PACKED), jnp.uint32),
                pltpu.VMEM((D_MODEL, F_EXPERT), BF16),
                pltpu.VMEM((D_MODEL, F_EXPERT), BF16),
                pltpu.VMEM((F_EXPERT, D_MODEL), BF16),
                pltpu.SemaphoreType.DMA((2,)),
                pltpu.SemaphoreType.DMA((2,)),
                pltpu.SemaphoreType.DMA,
            ],
        ),
        out_shape=jax.ShapeDtypeStruct((rows, PACKED), jnp.uint32),
        compiler_params=_params(1),
        name="experts",
    )(first_block, n_expert_blocks, n_used, x_sorted, w_gate_e, w_up_e, w_down_e)


def _combine_kernel(dest_ref, w_ref, h_ref, wg_ref, wu_ref, wd_ref, ln_g_ref, ln_b_ref, os_ref, y_ref, gbuf, sem):
    t = h_ref.shape[0]

    def row_copy(j, k):
        return pltpu.make_async_copy(os_ref.at[pl.ds(dest_ref[k, j], 1)], gbuf.at[k, pl.ds(j, 1)], sem)

    def issue(j, c):
        for k in range(TOP_K):
            row_copy(j, k).start(priority=k % DMA_PRIORITIES)
        return c

    def drain(j, c):
        for k in range(TOP_K):
            row_copy(j, k).wait()
        return c

    lax.fori_loop(0, t, issue, 0, unroll=DMA_UNROLL)
    h = h_ref[...]
    hb = h.astype(BF16)
    a = jnp.dot(hb, wg_ref[...], preferred_element_type=F32)
    u = jnp.dot(hb, wu_ref[...], preferred_element_type=F32)
    ffn = jnp.dot((a * jax.nn.sigmoid(a) * u).astype(BF16), wd_ref[...], preferred_element_type=F32)
    lax.fori_loop(0, t, drain, 0, unroll=DMA_UNROLL)
    w = w_ref[...]
    lo_acc = jnp.zeros((t, PACKED), F32)
    hi_acc = jnp.zeros((t, PACKED), F32)
    for k in range(TOP_K):
        lo, hi = _unpack_halves(gbuf[k])
        lo_acc = lo_acc + w[:, k:k + 1] * lo
        hi_acc = hi_acc + w[:, k:k + 1] * hi
    ffn = ffn + jnp.concatenate([lo_acc, hi_acc], axis=1)
    y_ref[...] = _layer_norm(ALPHA * h + ffn, ln_g_ref[...], ln_b_ref[...])


def _combine(dest_t, w2, h2, wg_b, wu_b, wd_b, ln_g, ln_b, out_sorted, *, row_offset, name):
    n = h2.shape[0]
    tm = COMBINE_ROWS
    off = row_offset // tm
    row = lambda i: (i, 0)
    fixed = lambda i: (0, 0)
    return pl.pallas_call(
        _combine_kernel,
        grid=(n // tm,),
        in_specs=[
            pl.BlockSpec((TOP_K, tm), lambda i: (0, i + off), memory_space=pltpu.SMEM),
            pl.BlockSpec((tm, TOP_K), lambda i: (i + off, 0)),
            pl.BlockSpec((tm, D_MODEL), row),
            pl.BlockSpec((D_MODEL, F_EXPERT), fixed),
            pl.BlockSpec((D_MODEL, F_EXPERT), fixed),
            pl.BlockSpec((F_EXPERT, D_MODEL), fixed),
            pl.BlockSpec((1, D_MODEL), fixed),
            pl.BlockSpec((1, D_MODEL), fixed),
            pl.BlockSpec(memory_space=pl.ANY),
        ],
        out_specs=pl.BlockSpec((tm, D_MODEL), row),
        out_shape=jax.ShapeDtypeStruct((n, D_MODEL), F32),
        scratch_shapes=[pltpu.VMEM((TOP_K, tm, PACKED), jnp.uint32), pltpu.SemaphoreType.DMA],
        compiler_params=_params(1),
        name=name,
    )(dest_t, w2, h2, wg_b, wu_b, wd_b, ln_g, ln_b, out_sorted)


def _route_kernel(lgp_ref, lgs_ref, bias_ref, eidx_ref, w_ref, rank_ref, cnt_ref, carry_ref, *, prompt_tiles):
    @pl.when(pl.program_id(0) == 0)
    def _():
        carry_ref[...] = jnp.zeros_like(carry_ref)

    t = lgp_ref.shape[1]
    gsz = N_EXPERTS // N_GROUPS
    neg = -jnp.inf
    s = jax.nn.sigmoid(jnp.where(pl.program_id(0) < prompt_tiles, lgp_ref[...], lgs_ref[...]))
    biased = s + bias_ref[...]
    io_g = lax.broadcasted_iota(jnp.int32, (gsz, t), 0)
    grp_rows = []
    for g in range(N_GROUPS):
        blk = biased[g * gsz:(g + 1) * gsz, :]
        m1 = jnp.max(blk, axis=0, keepdims=True)
        i1 = jnp.min(jnp.where(blk == m1, io_g, gsz), axis=0, keepdims=True)
        m2 = jnp.max(jnp.where(io_g == i1, neg, blk), axis=0, keepdims=True)
        grp_rows.append(m1 + m2)
    gs = jnp.concatenate(grp_rows, axis=0)
    io8 = lax.broadcasted_iota(jnp.int32, (N_GROUPS, t), 0)
    gsel = jnp.zeros((N_GROUPS, t), jnp.int32)
    for _ in range(TOPK_GROUPS):
        m = jnp.max(gs, axis=0, keepdims=True)
        gi = jnp.min(jnp.where(gs == m, io8, N_GROUPS), axis=0, keepdims=True)
        hit = io8 == gi
        gsel = jnp.where(hit, 1, gsel)
        gs = jnp.where(hit, neg, gs)
    masked = jnp.concatenate(
        [jnp.where(gsel[g:g + 1, :] > 0, biased[g * gsz:(g + 1) * gsz, :], neg) for g in range(N_GROUPS)], axis=0)

    eio = lax.broadcasted_iota(jnp.int32, (N_EXPERTS, t), 0)
    cur = masked
    idx_rows, w_rows = [], []
    for _ in range(TOP_K):
        m = jnp.max(cur, axis=0, keepdims=True)
        idx = jnp.min(jnp.where(cur == m, eio, N_EXPERTS), axis=0, keepdims=True)
        hit = eio == idx
        w_rows.append(jnp.sum(jnp.where(hit, s, 0.0), axis=0, keepdims=True))
        cur = jnp.where(hit, neg, cur)
        idx_rows.append(idx)
    sel = jnp.where(cur != masked, 1.0, 0.0)

    tri = jnp.where(lax.broadcasted_iota(jnp.int32, (t, t), 0) < lax.broadcasted_iota(jnp.int32, (t, t), 1), 1.0, 0.0)
    pref = jnp.dot(sel.astype(BF16), tri.astype(BF16), preferred_element_type=F32) + carry_ref[...]
    rank_rows = [jnp.sum(jnp.where(eio == idx_rows[k], pref, 0.0), axis=0, keepdims=True) for k in range(TOP_K)]
    carry_ref[...] += jnp.sum(sel, axis=1, keepdims=True)

    wk = jnp.concatenate(w_rows, axis=0)
    eidx_ref[...] = jnp.concatenate(idx_rows, axis=0)
    w_ref[...] = wk / jnp.sum(wk, axis=0, keepdims=True) * ROUTED_SCALE
    rank_ref[...] = jnp.concatenate(rank_rows, axis=0).astype(jnp.int32)
    cnt_ref[...] = carry_ref[...].astype(jnp.int32)


def _route(logits_p, logits_s, bias_col):
    t = ROUTE_TILE
    prompt_tiles = logits_p.shape[1] // t
    n = logits_p.shape[1] + logits_s.shape[1]
    col = lambda i: (0, i)
    fixed = lambda i: (0, 0)
    kern = functools.partial(_route_kernel, prompt_tiles=prompt_tiles)
    return pl.pallas_call(
        kern,
        grid=(n // t,),
        in_specs=[pl.BlockSpec((N_EXPERTS, t), lambda i: (0, jnp.minimum(i, prompt_tiles - 1))),
                  pl.BlockSpec((N_EXPERTS, t), lambda i: (0, jnp.maximum(i - prompt_tiles, 0))),
                  pl.BlockSpec((N_EXPERTS, 1), fixed)],
        out_specs=[
            pl.BlockSpec((TOP_K, t), col),
            pl.BlockSpec((TOP_K, t), col),
            pl.BlockSpec((TOP_K, t), col),
            pl.BlockSpec((N_EXPERTS, 1), fixed),
        ],
        out_shape=[
            jax.ShapeDtypeStruct((TOP_K, n), jnp.int32),
            jax.ShapeDtypeStruct((TOP_K, n), F32),
            jax.ShapeDtypeStruct((TOP_K, n), jnp.int32),
            jax.ShapeDtypeStruct((N_EXPERTS, 1), jnp.int32),
        ],
        scratch_shapes=[pltpu.VMEM((N_EXPERTS, 1), F32)],
        compiler_params=_params(1),
        name="route",
    )(logits_p, logits_s, bias_col)


def _dest_kernel(eidx_ref, rank_ref, start_ref, dest_ref):
    t = eidx_ref.shape[1]
    eio = lax.broadcasted_iota(jnp.int32, (N_EXPERTS, t), 0)
    start = start_ref[...]
    rows = [jnp.sum(jnp.where(eio == eidx_ref[k:k + 1, :], start, 0.0), axis=0, keepdims=True) for k in range(TOP_K)]
    dest_ref[...] = jnp.concatenate(rows, axis=0).astype(jnp.int32) + rank_ref[...]


def _dest(eidx_t, rank_t, pad_start_col):
    n = eidx_t.shape[1]
    t = ROW_TILE
    col = lambda i: (0, i)
    return pl.pallas_call(
        _dest_kernel,
        grid=(n // t,),
        in_specs=[pl.BlockSpec((TOP_K, t), col), pl.BlockSpec((TOP_K, t), col),
                  pl.BlockSpec((N_EXPERTS, 1), lambda i: (0, 0))],
        out_specs=pl.BlockSpec((TOP_K, t), col),
        out_shape=jax.ShapeDtypeStruct((TOP_K, n), jnp.int32),
        compiler_params=_params(1),
        name="dest",
    )(eidx_t, rank_t, pad_start_col)


def _dispatch_kernel(fill_ref, len_ref, nu_ref, dest_ref, hp_ref, hs_ref, xs_ref, zbuf, sem, fill_sem, *,
                     prompt_tiles, n_blocks):
    i = pl.program_id(0)

    @pl.when(i == 0)
    def _():
        zbuf[...] = jnp.zeros_like(zbuf)

        def fill_copy(row0, size):
            return pltpu.make_async_copy(zbuf.at[pl.ds(0, size)], xs_ref.at[pl.ds(pl.multiple_of(row0, 8), size)],
                                         fill_sem)

        def on_padding(fn):
            def body(e, c):
                base, length = fill_ref[e], len_ref[e]
                size = EXPERT_ROWS
                while size >= 8:
                    piece = fill_copy(base + (length & ~(2 * size - 1)), size)
                    pl.when((length & size) != 0)(functools.partial(fn, piece))
                    size //= 2
                return c
            lax.fori_loop(0, N_EXPERTS, body, 0)

        def on_unused_blocks(fn):
            lax.fori_loop(nu_ref[0], n_blocks, lambda b, c: (fn(fill_copy(b * EXPERT_ROWS, EXPERT_ROWS)), c)[1], 0)

        on_padding(lambda cp: cp.start())
        on_unused_blocks(lambda cp: cp.start())
        on_padding(lambda cp: cp.wait())
        on_unused_blocks(lambda cp: cp.wait())

    def scatter_rows(src_ref):
        t = src_ref.shape[0]

        def row_copy(j, k):
            return pltpu.make_async_copy(src_ref.at[pl.ds(j, 1)], xs_ref.at[pl.ds(dest_ref[k, j], 1)], sem)

        def issue(j, c):
            for k in range(TOP_K):
                row_copy(j, k).start(priority=k % DMA_PRIORITIES)
            return c

        def drain(j, c):
            for k in range(TOP_K):
                row_copy(j, k).wait()
            return c

        lax.fori_loop(0, t, issue, 0, unroll=DMA_UNROLL)
        lax.fori_loop(0, t, drain, 0, unroll=DMA_UNROLL)

    @pl.when(i < prompt_tiles)
    def _():
        scatter_rows(hp_ref)

    @pl.when(i >= prompt_tiles)
    def _():
        scatter_rows(hs_ref)


def _dispatch(fill_start, fill_len, n_used, dest_t, hp_p, hp_s, *, n_blocks):
    t = DISPATCH_ROWS
    prompt_tiles = hp_p.shape[0] // t
    sample_tiles = hp_s.shape[0] // t
    kern = functools.partial(_dispatch_kernel, prompt_tiles=prompt_tiles, n_blocks=n_blocks)
    return pl.pallas_call(
        kern,
        grid_spec=pltpu.PrefetchScalarGridSpec(
            num_scalar_prefetch=3,
            grid=(prompt_tiles + sample_tiles,),
            in_specs=[
                pl.BlockSpec((TOP_K, t), lambda i, *_: (0, i), memory_space=pltpu.SMEM),
                pl.BlockSpec((t, PACKED), lambda i, *_: (jnp.minimum(i, prompt_tiles - 1), 0)),
                pl.BlockSpec((t, PACKED), lambda i, *_: (jnp.maximum(i - prompt_tiles, 0), 0)),
            ],
            out_specs=pl.BlockSpec(memory_space=pl.ANY),
            scratch_shapes=[pltpu.VMEM((EXPERT_ROWS, PACKED), jnp.uint32), pltpu.SemaphoreType.DMA,
                            pltpu.SemaphoreType.DMA],
        ),
        out_shape=jax.ShapeDtypeStruct((n_blocks * EXPERT_ROWS, PACKED), jnp.uint32),
        compiler_params=_params(1),
        name="dispatch",
    )(fill_start, fill_len, n_used, dest_t, hp_p, hp_s)


def _block_plan(counts):
    padded = (counts + EXPERT_ROWS - 1) // EXPERT_ROWS * EXPERT_ROWS
    pad_end = jnp.cumsum(padded).astype(jnp.int32)
    pad_start = pad_end - padded
    n_used = pad_end[-1] // EXPERT_ROWS
    fill_start = ((pad_start + counts) // 8 * 8).astype(jnp.int32)
    fill_len = pad_end - fill_start
    first_block = pad_start // EXPERT_ROWS
    n_expert_blocks = (padded // EXPERT_ROWS).astype(jnp.int32)
    return pad_start, fill_start, fill_len, first_block, n_expert_blocks, n_used.reshape(1).astype(jnp.int32)


def kernel(x_prompt, x_sample, cache_k, cache_v, w_in, sink, gm_ln_g, gm_ln_b, gm_w_s, gm_b_s, w_out, ln1_g, ln1_b,
           router_w, router_bias, w_gate_e, w_up_e, w_down_e, w_gate_s, w_up_s, w_down_s, ln2_g, ln2_b):
    bp, sp = x_prompt.shape[:2]
    bs, ts = x_sample.shape[:2]
    r = cache_k.shape[2]
    assert r == WINDOW and sp % ROW_TILE == 0 and (bs * ts) % ROW_TILE == 0
    n_p, n_s = bp * sp, bs * ts
    n_total = n_p + n_s
    l = 0

    w_in_b = w_in[l].astype(BF16)
    w_out_b = w_out[l].astype(BF16)
    router_t = router_w[l].T
    r_hi = router_t.astype(BF16)
    r_lo = (router_t - r_hi.astype(F32)).astype(BF16)
    row_vec = lambda v: v.reshape(1, -1)
    gm_g, gm_b = row_vec(gm_ln_g[l]), row_vec(gm_ln_b[l])
    sink_l = sink[l].astype(F32)

    xp2 = x_prompt.reshape(n_p, D_MODEL)
    tabs_p = _rope_tables(jnp.arange(sp, dtype=jnp.int32))
    q, k, v, gu, gv = _in_proj(xp2, w_in_b, tabs_p, gm_g, gm_b, tm=ROW_TILE, gv_dtype=BF16, name="in_proj_prompt")
    nb = sp // WINDOW
    att = _attention(sink_l, q, k, v, k, v, batch=bp, nb=nb, tq=WINDOW, seqs=1, prev_blocks=nb,
                     first_block_has_no_prev=True, name="attn_prompt")
    b_tab_p = jnp.repeat(gm_b_s[l].T, HEAD_DIM, axis=1)
    gm = _gate(gu, gv, gm_w_s[l], b_tab_p, chunk=CHUNK, n_chunks=ROW_TILE // CHUNK, name="gate_prompt")
    new_kp = k.reshape(bp, sp, KV_HEADS, HEAD_DIM)[:, sp - r:][None]
    new_vp = v.reshape(bp, sp, KV_HEADS, HEAD_DIM)[:, sp - r:][None]
    h_p, hp_p, lt_p = _out_proj(att, gm, xp2, w_out_b, row_vec(ln1_g[l]), row_vec(ln1_b[l]), r_hi, r_lo,
                          name="out_proj_prompt")

    xs2 = x_sample.reshape(n_s, D_MODEL)
    pos_s = PAST_LEN + jnp.arange(ts, dtype=jnp.int32)
    tabs_s = tuple(jnp.tile(t, (bs, 1)) for t in _rope_tables(pos_s))
    q, k, v, gu, gv = _in_proj(xs2, w_in_b, tabs_s, gm_g, gm_b, tm=n_s, gv_dtype=F32, name="in_proj_sample")
    tq = 8
    pad_rows = lambda t: jnp.pad(t.reshape(bs, ts, -1), ((0, 0), (0, tq - ts), (0, 0))).reshape(bs * tq, -1)
    ck2 = cache_k[l].reshape(bs * r, KV_WIDTH)
    cv2 = cache_v[l].reshape(bs * r, KV_WIDTH)
    att = _attention(sink_l, pad_rows(q), pad_rows(k), pad_rows(v), ck2, cv2, batch=bs, nb=1, tq=tq,
                     seqs=SAMPLE_SEQS_PER_STEP, prev_blocks=1, first_block_has_no_prev=False, name="attn_sample")
    att = att.reshape(bs, tq, ATT_WIDTH)[:, :ts].reshape(n_s, ATT_WIDTH)
    w_small = gm_w_s[l][:, :ts, :ts]
    w_big = jnp.einsum("ab,hij->haibj", jnp.eye(bs, dtype=F32), w_small).reshape(GM_HEADS, n_s, n_s)
    b_tab_s = jnp.tile(jnp.repeat(gm_b_s[l][:, :ts].T, HEAD_DIM, axis=1), (bs, 1))
    gm = _gate(gu, gv, w_big, b_tab_s, chunk=n_s, n_chunks=1, name="gate_sample")
    new_ks = jnp.concatenate([cache_k[l], k.reshape(bs, ts, KV_HEADS, HEAD_DIM)], axis=1)[:, ts:][None]
    new_vs = jnp.concatenate([cache_v[l], v.reshape(bs, ts, KV_HEADS, HEAD_DIM)], axis=1)[:, ts:][None]
    new_gs = gv.reshape(bs, ts, GM_WIDTH)[None]
    h_s, hp_s, lt_s = _out_proj(att, gm, xs2, w_out_b, row_vec(ln1_g[l]), row_vec(ln1_b[l]), r_hi, r_lo,
                          name="out_proj_sample")

    eidx_t, w_t, rank_t, counts = _route(lt_p, lt_s, router_bias[l].astype(F32).reshape(N_EXPERTS, 1))
    a = n_total * TOP_K
    n_blocks = -(-(a + N_EXPERTS * (EXPERT_ROWS - 1)) // EXPERT_ROWS)
    pad_start, fill_start, fill_len, first_block, n_expert_blocks, n_used = _block_plan(counts.reshape(N_EXPERTS))
    dest_t = _dest(eidx_t, rank_t, pad_start.astype(F32).reshape(N_EXPERTS, 1))
    x_sorted = _dispatch(fill_start, fill_len, n_used, dest_t, hp_p, hp_s, n_blocks=n_blocks)
    out_sorted = _experts(first_block, n_expert_blocks, n_used, x_sorted, w_gate_e[l], w_up_e[l], w_down_e[l])
    shared = (w_gate_s[l].astype(BF16), w_up_s[l].astype(BF16), w_down_s[l].astype(BF16))
    ln2 = (row_vec(ln2_g[l]), row_vec(ln2_b[l]))
    wts = w_t.T
    y_p = _combine(dest_t, wts, h_p, *shared, *ln2, out_sorted, row_offset=0, name="combine_prompt")
    y_s = _combine(dest_t, wts, h_s, *shared, *ln2, out_sorted, row_offset=n_p, name="combine_sample")
    return (y_p.reshape(bp, sp, D_MODEL), y_s.reshape(bs, ts, D_MODEL), new_kp, new_vp, new_ks, new_vs, new_gs)
```

```python
import functools

import jax
import jax.numpy as jnp
import numpy as np
from jax import lax
from jax.experimental import pallas as pl
from jax.experimental.pallas import tpu as pltpu

D_MODEL = 1024
HEAD_DIM = 64
ATT_HEADS = 8
KV_HEADS = 2
Q_PER_KV = ATT_HEADS // KV_HEADS
GM_HEADS = 8
ATT_WIDTH = ATT_HEADS * HEAD_DIM
KV_WIDTH = KV_HEADS * HEAD_DIM
GM_WIDTH = GM_HEADS * HEAD_DIM
ROPE_WIDTH = ATT_WIDTH + KV_WIDTH
IN_WIDTH = ATT_WIDTH + 2 * KV_WIDTH + 2 * GM_WIDTH
WINDOW = 128
CHUNK = 128
PAST_LEN = 16384
ROPE_THETA = 10000.0
ATT_SCALE = HEAD_DIM ** -0.5
N_EXPERTS = 256
TOP_K = 8
N_GROUPS = 8
TOPK_GROUPS = 4
F_EXPERT = 256
ROUTED_SCALE = 2.5
LN_EPS = 1e-5
DEPTH = 1
ALPHA = (2.0 * DEPTH) ** 0.25

PACKED = D_MODEL // 2
LANES = 128
ROW_TILE = 512
EXPERT_ROWS = 256
EXPERT_RING = 8
COMBINE_ROWS = 256
DISPATCH_ROWS = 256
ROUTE_TILE = 256
SAMPLE_SEQS_PER_STEP = 16
DMA_UNROLL = 4
DMA_PRIORITIES = 2
VMEM_LIMIT = 56 * 1024 * 1024

F32 = jnp.float32
BF16 = jnp.bfloat16


def _params(n_axes):
    return pltpu.CompilerParams(dimension_semantics=("arbitrary",) * n_axes, vmem_limit_bytes=VMEM_LIMIT)


def _layer_norm(x, g, b):
    mu = jnp.mean(x, axis=-1, keepdims=True)
    xc = x - mu
    var = jnp.mean(xc * xc, axis=-1, keepdims=True)
    return xc * lax.rsqrt(var + LN_EPS) * g + b


def _pack_halves(x):
    xb = x.astype(BF16)
    lo = lax.bitcast_convert_type(xb[:, :PACKED].astype(F32), jnp.uint32)
    hi = lax.bitcast_convert_type(xb[:, PACKED:].astype(F32), jnp.uint32)
    return (lo >> 16) | hi


def _unpack_halves(w):
    lo = lax.bitcast_convert_type(w << 16, F32)
    hi = lax.bitcast_convert_type(w & jnp.uint32(0xFFFF0000), F32)
    return lo, hi


def _gelu(x):
    return 0.5 * x * (1.0 + lax.erf(x * np.float32(np.sqrt(0.5))))


def _in_proj_kernel(x_ref, w_ref, cos_ref, sa_ref, sb_ref, g_ref, b_ref, q_ref, k_ref, v_ref, gu_ref, gv_ref):
    x = x_ref[...].astype(BF16)
    zr = jnp.dot(x, w_ref[:, :ROPE_WIDTH], preferred_element_type=F32)
    pieces = []
    for c in range(ROPE_WIDTH // LANES):
        sl = slice(c * LANES, (c + 1) * LANES)
        zc = zr[:, sl]
        pieces.append(zc * cos_ref[:, sl]
                      + pltpu.roll(zc, LANES - HEAD_DIM // 2, 1) * sa_ref[:, sl]
                      + pltpu.roll(zc, HEAD_DIM // 2, 1) * sb_ref[:, sl])
    for c in range(ATT_WIDTH // LANES):
        q_ref[:, c * LANES:(c + 1) * LANES] = pieces[c].astype(q_ref.dtype)
    k_ref[...] = pieces[ATT_WIDTH // LANES]
    v_ref[...] = jnp.dot(x, w_ref[:, ROPE_WIDTH:ROPE_WIDTH + KV_WIDTH], preferred_element_type=F32)
    g0 = ROPE_WIDTH + KV_WIDTH
    zu = jnp.dot(x, w_ref[:, g0:g0 + GM_WIDTH], preferred_element_type=F32)
    gu_ref[...] = _gelu(zu).astype(gu_ref.dtype)
    zv = jnp.dot(x, w_ref[:, g0 + GM_WIDTH:g0 + 2 * GM_WIDTH], preferred_element_type=F32)
    gv = _layer_norm(_gelu(zv), g_ref[...], b_ref[...])
    gv_ref[...] = gv.astype(gv_ref.dtype)


def _in_proj(x2, w_in_b, tabs, ln_g, ln_b, *, tm, gv_dtype, name):
    n = x2.shape[0]
    cos_t, sa_t, sb_t = tabs
    period = cos_t.shape[0] // tm
    row = lambda i: (i, 0)
    tab = lambda i: (i % period, 0)
    fixed = lambda i: (0, 0)
    return pl.pallas_call(
        _in_proj_kernel,
        grid=(n // tm,),
        in_specs=[
            pl.BlockSpec((tm, D_MODEL), row),
            pl.BlockSpec((D_MODEL, IN_WIDTH), fixed),
            pl.BlockSpec((tm, ROPE_WIDTH), tab),
            pl.BlockSpec((tm, ROPE_WIDTH), tab),
            pl.BlockSpec((tm, ROPE_WIDTH), tab),
            pl.BlockSpec((1, GM_WIDTH), fixed),
            pl.BlockSpec((1, GM_WIDTH), fixed),
        ],
        out_specs=[
            pl.BlockSpec((tm, ATT_WIDTH), row),
            pl.BlockSpec((tm, KV_WIDTH), row),
            pl.BlockSpec((tm, KV_WIDTH), row),
            pl.BlockSpec((tm, GM_WIDTH), row),
            pl.BlockSpec((tm, GM_WIDTH), row),
        ],
        out_shape=[
            jax.ShapeDtypeStruct((n, ATT_WIDTH), BF16),
            jax.ShapeDtypeStruct((n, KV_WIDTH), F32),
            jax.ShapeDtypeStruct((n, KV_WIDTH), F32),
            jax.ShapeDtypeStruct((n, GM_WIDTH), BF16),
            jax.ShapeDtypeStruct((n, GM_WIDTH), gv_dtype),
        ],
        compiler_params=_params(1),
        name=name,
    )(x2, w_in_b, cos_t, sa_t, sb_t, ln_g, ln_b)


def _rope_tables(pos):
    half = HEAD_DIM // 2
    inv = ROPE_THETA ** (-jnp.arange(half, dtype=F32) * 2.0 / HEAD_DIM)
    ang = pos.astype(F32)[:, None] * inv[None, :]
    cos, sin = jnp.cos(ang), jnp.sin(ang)
    zero = jnp.zeros_like(sin)
    heads = ROPE_WIDTH // HEAD_DIM
    scale = jnp.where(jnp.arange(ROPE_WIDTH) < ATT_WIDTH, ATT_SCALE, 1.0).astype(F32)[None, :]
    cos_t = jnp.tile(jnp.concatenate([cos, cos], axis=1), (1, heads)) * scale
    sa_t = jnp.tile(jnp.concatenate([-sin, zero], axis=1), (1, heads)) * scale
    sb_t = jnp.tile(jnp.concatenate([zero, sin], axis=1), (1, heads)) * scale
    return cos_t, sa_t, sb_t


def _attn_kernel(sink_ref, q_ref, kc_ref, vc_ref, kp_ref, vp_ref, o_ref, *, tq, seqs, stack, first_block_has_no_prev):
    nk = WINDOW + tq
    rows = stack * tq
    qi = lax.broadcasted_iota(jnp.int32, (rows, nk), 0) & (tq - 1)
    ks = lax.broadcasted_iota(jnp.int32, (rows, nk), 1)
    mask = (ks >= qi) & (ks <= qi + WINDOW)
    if first_block_has_no_prev:
        mask = mask & ((pl.program_id(1) > 0) | (ks >= WINDOW))
    sinks = [jnp.concatenate([jnp.full((tq, 1), sink_ref[h0 + j], F32) for j in range(stack)], axis=0)
             for h0 in range(0, ATT_HEADS, stack)]
    for b in range(seqs):
        qrows = slice(b * tq, (b + 1) * tq)
        prows = slice(b * WINDOW, (b + 1) * WINDOW)
        q = q_ref[qrows, :]
        kk = jnp.concatenate([kp_ref[prows, :], kc_ref[qrows, :]], axis=0).astype(BF16)
        vv = jnp.concatenate([vp_ref[prows, :], vc_ref[qrows, :]], axis=0).astype(BF16)
        outs = []
        for i, h0 in enumerate(range(0, ATT_HEADS, stack)):
            g = h0 // Q_PER_KV
            kg = kk[:, g * HEAD_DIM:(g + 1) * HEAD_DIM]
            vg = vv[:, g * HEAD_DIM:(g + 1) * HEAD_DIM]
            qg = jnp.concatenate([q[:, (h0 + j) * HEAD_DIM:(h0 + j + 1) * HEAD_DIM] for j in range(stack)], axis=0)
            s = lax.dot_general(qg, kg, (((1,), (1,)), ((), ())), preferred_element_type=F32)
            s = jnp.where(mask, s, -jnp.inf)
            m = jnp.maximum(jnp.max(s, axis=-1, keepdims=True), sinks[i])
            p = jnp.exp(s - m)
            denom = jnp.sum(p, axis=-1, keepdims=True) + jnp.exp(sinks[i] - m)
            o = jnp.dot((p / denom).astype(BF16), vg, preferred_element_type=F32)
            outs.extend(o[j * tq:(j + 1) * tq, :] for j in range(stack))
        o_ref[qrows, :] = jnp.concatenate(outs, axis=1).astype(o_ref.dtype)


def _attention(sink, q2, k2, v2, kprev2, vprev2, *, batch, nb, tq, seqs, prev_blocks, first_block_has_no_prev, name):
    assert tq & (tq - 1) == 0 and batch % seqs == 0 and (seqs == 1 or nb == prev_blocks == 1)
    cur = lambda b, n, s: (b * nb + n, 0)
    prev = lambda b, n, s: (b * prev_blocks + jnp.maximum(n - 1, 0), 0)
    stack = Q_PER_KV if Q_PER_KV * tq <= WINDOW else 1
    kern = functools.partial(_attn_kernel, tq=tq, seqs=seqs, stack=stack,
                             first_block_has_no_prev=first_block_has_no_prev)
    return pl.pallas_call(
        kern,
        grid_spec=pltpu.PrefetchScalarGridSpec(
            num_scalar_prefetch=1,
            grid=(batch // seqs, nb),
            in_specs=[
                pl.BlockSpec((seqs * tq, ATT_WIDTH), cur),
                pl.BlockSpec((seqs * tq, KV_WIDTH), cur),
                pl.BlockSpec((seqs * tq, KV_WIDTH), cur),
                pl.BlockSpec((seqs * WINDOW, KV_WIDTH), prev),
                pl.BlockSpec((seqs * WINDOW, KV_WIDTH), prev),
            ],
            out_specs=pl.BlockSpec((seqs * tq, ATT_WIDTH), cur),
        ),
        out_shape=jax.ShapeDtypeStruct(q2.shape, BF16),
        compiler_params=_params(2),
        name=name,
    )(sink, q2, k2, v2, kprev2, vprev2)


def _gate_kernel(gu_ref, gv_ref, w_ref, b_ref, o_ref, *, chunk, n_chunks):
    ri = lax.broadcasted_iota(jnp.int32, (chunk, chunk), 0)
    ci = lax.broadcasted_iota(jnp.int32, (chunk, chunk), 1)
    ws = [jnp.where(ci <= ri, w_ref[h], 0.0).astype(BF16) for h in range(GM_HEADS)]
    for c in range(n_chunks):
        rows = slice(c * chunk, (c + 1) * chunk)
        gv = gv_ref[rows, :].astype(BF16)
        sv = jnp.concatenate(
            [jnp.dot(ws[h], gv[:, h * HEAD_DIM:(h + 1) * HEAD_DIM], preferred_element_type=F32)
             for h in range(GM_HEADS)], axis=1)
        o_ref[rows, :] = (gu_ref[rows, :].astype(F32) * (sv + b_ref[...])).astype(o_ref.dtype)


def _gate(gu2, gv2, w_s, b_tab, *, chunk, n_chunks, name):
    n = gu2.shape[0]
    tm = chunk * n_chunks
    row = lambda i: (i, 0)
    kern = functools.partial(_gate_kernel, chunk=chunk, n_chunks=n_chunks)
    return pl.pallas_call(
        kern,
        grid=(n // tm,),
        in_specs=[
            pl.BlockSpec((tm, GM_WIDTH), row),
            pl.BlockSpec((tm, GM_WIDTH), row),
            pl.BlockSpec((GM_HEADS, chunk, chunk), lambda i: (0, 0, 0)),
            pl.BlockSpec((chunk, GM_WIDTH), lambda i: (0, 0)),
        ],
        out_specs=pl.BlockSpec((tm, GM_WIDTH), row),
        out_shape=jax.ShapeDtypeStruct((n, GM_WIDTH), BF16),
        compiler_params=_params(1),
        name=name,
    )(gu2, gv2, w_s, b_tab)


def _out_proj_kernel(att_ref, gm_ref, x_ref, wo_ref, g_ref, b_ref, rh_ref, rl_ref, h_ref, hp_ref, lg_ref):
    mix = jnp.dot(att_ref[...], wo_ref[:ATT_WIDTH, :], preferred_element_type=F32)
    mix = mix + jnp.dot(gm_ref[...], wo_ref[ATT_WIDTH:, :], preferred_element_type=F32)
    h = _layer_norm(ALPHA * x_ref[...] + mix, g_ref[...], b_ref[...])
    h_ref[...] = h
    hp_ref[...] = _pack_halves(h)
    h_hi = h.astype(BF16)
    h_lo = (h - h_hi.astype(F32)).astype(BF16)
    nt = (((1,), (1,)), ((), ()))
    lg = lax.dot_general(rh_ref[...], h_hi, nt, preferred_element_type=F32)
    lg = lg + lax.dot_general(rh_ref[...], h_lo, nt, preferred_element_type=F32)
    lg = lg + lax.dot_general(rl_ref[...], h_hi, nt, preferred_element_type=F32)
    lg_ref[...] = lg


def _out_proj(att2, gm2, x2, w_out_b, ln_g, ln_b, r_hi, r_lo, *, name):
    n = x2.shape[0]
    tm = ROW_TILE
    row = lambda i: (i, 0)
    fixed = lambda i: (0, 0)
    return pl.pallas_call(
        _out_proj_kernel,
        grid=(n // tm,),
        in_specs=[
            pl.BlockSpec((tm, ATT_WIDTH), row),
            pl.BlockSpec((tm, GM_WIDTH), row),
            pl.BlockSpec((tm, D_MODEL), row),
            pl.BlockSpec((D_MODEL, D_MODEL), fixed),
            pl.BlockSpec((1, D_MODEL), fixed),
            pl.BlockSpec((1, D_MODEL), fixed),
            pl.BlockSpec((N_EXPERTS, D_MODEL), fixed),
            pl.BlockSpec((N_EXPERTS, D_MODEL), fixed),
        ],
        out_specs=[
            pl.BlockSpec((tm, D_MODEL), row),
            pl.BlockSpec((tm, PACKED), row),
            pl.BlockSpec((N_EXPERTS, tm), lambda i: (0, i)),
        ],
        out_shape=[
            jax.ShapeDtypeStruct((n, D_MODEL), F32),
            jax.ShapeDtypeStruct((n, PACKED), jnp.uint32),
            jax.ShapeDtypeStruct((N_EXPERTS, n), F32),
        ],
        compiler_params=_params(1),
        name=name,
    )(att2, gm2, x2, w_out_b, ln_g, ln_b, r_hi, r_lo)


def _experts_kernel(first_ref, nblk_ref, nu_ref, x_hbm, wg_ref, wu_ref, wd_ref, o_hbm, xbuf, obuf, wg_s, wu_s, wd_s,
                    in_sem, out_sem, fill_sem, *, n_blocks):
    e = pl.program_id(0)
    nb = nblk_ref[e]
    b0 = first_ref[e]
    n_used = nu_ref[0]
    ahead = EXPERT_RING // 2

    def rows_of(block):
        return pl.ds(pl.multiple_of(block * EXPERT_ROWS, EXPERT_ROWS), EXPERT_ROWS)

    def slot_of(block):
        return block & (EXPERT_RING - 1)

    def in_copy(block):
        return pltpu.make_async_copy(x_hbm.at[rows_of(block)], xbuf.at[slot_of(block)], in_sem.at[slot_of(block)])

    def out_copy(block):
        return pltpu.make_async_copy(obuf.at[slot_of(block)], o_hbm.at[rows_of(block)], out_sem.at[slot_of(block)])

    @pl.when(e == 0)
    def _():
        for j in range(ahead):
            pl.when(j < n_used)(lambda j=j: in_copy(j).start())

    def process(block, count):
        blocks = [block + j for j in range(count)]
        for blk in blocks:
            in_copy(blk).wait()
        x = [xbuf[slot_of(blk)] for blk in blocks]
        xl, xh = _unpack_halves(x[0] if count == 1 else jnp.concatenate(x, axis=0))
        xl, xh = xl.astype(BF16), xh.astype(BF16)
        for blk in blocks:
            pl.when(blk + ahead < n_used)(lambda blk=blk: in_copy(blk + ahead).start())
            pl.when(blk >= ahead)(lambda blk=blk: out_copy(blk - ahead).wait())
        a = (jnp.dot(xl, wg_s[:PACKED, :], preferred_element_type=F32)
             + jnp.dot(xh, wg_s[PACKED:, :], preferred_element_type=F32))
        u = (jnp.dot(xl, wu_s[:PACKED, :], preferred_element_type=F32)
             + jnp.dot(xh, wu_s[PACKED:, :], preferred_element_type=F32))
        hb = (a * jax.nn.sigmoid(a) * u).astype(BF16)
        o = _pack_halves(jnp.dot(hb, wd_s[...], preferred_element_type=F32))
        for j, blk in enumerate(blocks):
            obuf[slot_of(blk)] = o[j * EXPERT_ROWS:(j + 1) * EXPERT_ROWS, :]
            out_copy(blk).start()

    @pl.when(nb > 0)
    def _():
        wg_s[...] = wg_ref[0].astype(BF16)
        wu_s[...] = wu_ref[0].astype(BF16)
        wd_s[...] = wd_ref[0].astype(BF16)

        def pair(i, carry):
            process(b0 + 2 * i, 2)
            return carry

        lax.fori_loop(0, nb // 2, pair, 0)
        pl.when(nb % 2 == 1)(lambda: process(b0 + nb - 1, 1))

    @pl.when(e == N_EXPERTS - 1)
    def _():
        for j in range(ahead):
            pl.when(n_used - 1 - j >= 0)(lambda j=j: out_copy(n_used - 1 - j).wait())
        obuf[0] = jnp.zeros((EXPERT_ROWS, PACKED), jnp.uint32)

        def on_unused_blocks(fn):
            def body(b, c):
                fn(pltpu.make_async_copy(obuf.at[0], o_hbm.at[rows_of(b)], fill_sem))
                return c
            lax.fori_loop(n_used, n_blocks, body, 0)

        on_unused_blocks(lambda cp: cp.start())
        on_unused_blocks(lambda cp: cp.wait())


def _experts(first_block, n_expert_blocks, n_used, x_sorted, w_gate_e, w_up_e, w_down_e):
    rows = x_sorted.shape[0]
    wmap = lambda e, *_: (e, 0, 0)
    kern = functools.partial(_experts_kernel, n_blocks=rows // EXPERT_ROWS)
    return pl.pallas_call(
        kern,
        grid_spec=pltpu.PrefetchScalarGridSpec(
            num_scalar_prefetch=3,
            grid=(N_EXPERTS,),
            in_specs=[
                pl.BlockSpec(memory_space=pl.ANY),
                pl.BlockSpec((1, D_MODEL, F_EXPERT), wmap),
                pl.BlockSpec((1, D_MODEL, F_EXPERT), wmap),
                pl.BlockSpec((1, F_EXPERT, D_MODEL), wmap),
            ],
            out_specs=pl.BlockSpec(memory_space=pl.ANY),
            scratch_shapes=[
                pltpu.VMEM((EXPERT_RING, EXPERT_ROWS, PACKED), jnp.uint32),
                pltpu.VMEM((EXPERT_RING, EXPERT_ROWS, PACKED), jnp.uint32),
                pltpu.VMEM((D_MODEL, F_EXPERT), BF16),
                pltpu.VMEM((D_MODEL, F_EXPERT), BF16),
                pltpu.VMEM((F_EXPERT, D_MODEL), BF16),
                pltpu.SemaphoreType.DMA((EXPERT_RING,)),
                pltpu.SemaphoreType.DMA((EXPERT_RING,)),
                pltpu.SemaphoreType.DMA,
            ],
        ),
        out_shape=jax.ShapeDtypeStruct((rows, PACKED), jnp.uint32),
        compiler_params=_params(1),
        name="experts",
    )(first_block, n_expert_blocks, n_used, x_sorted, w_gate_e, w_up_e, w_down_e)


def _combine_kernel(dest_ref, w_ref, h_ref, wg_ref, wu_ref, wd_ref, ln_g_ref, ln_b_ref, os_ref, y_ref, gbuf, sem):
    t = h_ref.shape[0]

    def row_copy(j, k):
        return pltpu.make_async_copy(os_ref.at[pl.ds(dest_ref[k, j], 1)], gbuf.at[k, pl.ds(j, 1)], sem)

    def issue(j, c):
        for k in range(TOP_K):
            row_copy(j, k).start(priority=k % DMA_PRIORITIES)
        return c

    def drain(j, c):
        for k in range(TOP_K):
            row_copy(j, k).wait()
        return c

    lax.fori_loop(0, t, issue, 0, unroll=DMA_UNROLL)
    h = h_ref[...]
    hb = h.astype(BF16)
    a = jnp.dot(hb, wg_ref[...], preferred_element_type=F32)
    u = jnp.dot(hb, wu_ref[...], preferred_element_type=F32)
    ffn = jnp.dot((a * jax.nn.sigmoid(a) * u).astype(BF16), wd_ref[...], preferred_element_type=F32)
    lax.fori_loop(0, t, drain, 0, unroll=DMA_UNROLL)
    w = w_ref[...]
    lo_acc = jnp.zeros((t, PACKED), F32)
    hi_acc = jnp.zeros((t, PACKED), F32)
    for k in range(TOP_K):
        lo, hi = _unpack_halves(gbuf[k])
        lo_acc = lo_acc + w[:, k:k + 1] * lo
        hi_acc = hi_acc + w[:, k:k + 1] * hi
    ffn = ffn + jnp.concatenate([lo_acc, hi_acc], axis=1)
    y_ref[...] = _layer_norm(ALPHA * h + ffn, ln_g_ref[...], ln_b_ref[...])


def _combine(dest_t, w2, h2, wg_b, wu_b, wd_b, ln_g, ln_b, out_sorted, *, row_offset, name):
    n = h2.shape[0]
    tm = COMBINE_ROWS
    off = row_offset // tm
    row = lambda i: (i, 0)
    fixed = lambda i: (0, 0)
    return pl.pallas_call(
        _combine_kernel,
        grid=(n // tm,),
        in_specs=[
            pl.BlockSpec((TOP_K, tm), lambda i: (0, i + off), memory_space=pltpu.SMEM),
            pl.BlockSpec((tm, TOP_K), lambda i: (i + off, 0)),
            pl.BlockSpec((tm, D_MODEL), row),
            pl.BlockSpec((D_MODEL, F_EXPERT), fixed),
            pl.BlockSpec((D_MODEL, F_EXPERT), fixed),
            pl.BlockSpec((F_EXPERT, D_MODEL), fixed),
            pl.BlockSpec((1, D_MODEL), fixed),
            pl.BlockSpec((1, D_MODEL), fixed),
            pl.BlockSpec(memory_space=pl.ANY),
        ],
        out_specs=pl.BlockSpec((tm, D_MODEL), row),
        out_shape=jax.ShapeDtypeStruct((n, D_MODEL), F32),
        scratch_shapes=[pltpu.VMEM((TOP_K, tm, PACKED), jnp.uint32), pltpu.SemaphoreType.DMA],
        compiler_params=_params(1),
        name=name,
    )(dest_t, w2, h2, wg_b, wu_b, wd_b, ln_g, ln_b, out_sorted)


def _route_kernel(lgp_ref, lgs_ref, bias_ref, eidx_ref, w_ref, rank_ref, cnt_ref, carry_ref, *, prompt_tiles):
    @pl.when(pl.program_id(0) == 0)
    def _():
        carry_ref[...] = jnp.zeros_like(carry_ref)

    t = lgp_ref.shape[1]
    gsz = N_EXPERTS // N_GROUPS
    neg = -jnp.inf
    s = jax.nn.sigmoid(jnp.where(pl.program_id(0) < prompt_tiles, lgp_ref[...], lgs_ref[...]))
    biased = s + bias_ref[...]
    io_g = lax.broadcasted_iota(jnp.int32, (gsz, t), 0)
    grp_rows = []
    for g in range(N_GROUPS):
        blk = biased[g * gsz:(g + 1) * gsz, :]
        m1 = jnp.max(blk, axis=0, keepdims=True)
        i1 = jnp.min(jnp.where(blk == m1, io_g, gsz), axis=0, keepdims=True)
        m2 = jnp.max(jnp.where(io_g == i1, neg, blk), axis=0, keepdims=True)
        grp_rows.append(m1 + m2)
    gs = jnp.concatenate(grp_rows, axis=0)
    io8 = lax.broadcasted_iota(jnp.int32, (N_GROUPS, t), 0)
    gsel = jnp.zeros((N_GROUPS, t), jnp.int32)
    for _ in range(TOPK_GROUPS):
        m = jnp.max(gs, axis=0, keepdims=True)
        gi = jnp.min(jnp.where(gs == m, io8, N_GROUPS), axis=0, keepdims=True)
        hit = io8 == gi
        gsel = jnp.where(hit, 1, gsel)
        gs = jnp.where(hit, neg, gs)
    masked = jnp.concatenate(
        [jnp.where(gsel[g:g + 1, :] > 0, biased[g * gsz:(g + 1) * gsz, :], neg) for g in range(N_GROUPS)], axis=0)

    eio = lax.broadcasted_iota(jnp.int32, (N_EXPERTS, t), 0)
    cur = masked
    idx_rows, w_rows = [], []
    for _ in range(TOP_K):
        m = jnp.max(cur, axis=0, keepdims=True)
        idx = jnp.min(jnp.where(cur == m, eio, N_EXPERTS), axis=0, keepdims=True)
        hit = eio == idx
        w_rows.append(jnp.sum(jnp.where(hit, s, 0.0), axis=0, keepdims=True))
        cur = jnp.where(hit, neg, cur)
        idx_rows.append(idx)
    sel = jnp.where(cur != masked, 1.0, 0.0)

    tri = jnp.where(lax.broadcasted_iota(jnp.int32, (t, t), 0) < lax.broadcasted_iota(jnp.int32, (t, t), 1), 1.0, 0.0)
    pref = jnp.dot(sel.astype(BF16), tri.astype(BF16), preferred_element_type=F32) + carry_ref[...]
    rank_rows = [jnp.sum(jnp.where(eio == idx_rows[k], pref, 0.0), axis=0, keepdims=True) for k in range(TOP_K)]
    carry_ref[...] += jnp.sum(sel, axis=1, keepdims=True)

    wk = jnp.concatenate(w_rows, axis=0)
    eidx_ref[...] = jnp.concatenate(idx_rows, axis=0)
    w_ref[...] = wk / jnp.sum(wk, axis=0, keepdims=True) * ROUTED_SCALE
    rank_ref[...] = jnp.concatenate(rank_rows, axis=0).astype(jnp.int32)
    cnt_ref[...] = carry_ref[...].astype(jnp.int32)


def _route(logits_p, logits_s, bias_col):
    t = ROUTE_TILE
    prompt_tiles = logits_p.shape[1] // t
    n = logits_p.shape[1] + logits_s.shape[1]
    col = lambda i: (0, i)
    fixed = lambda i: (0, 0)
    kern = functools.partial(_route_kernel, prompt_tiles=prompt_tiles)
    return pl.pallas_call(
        kern,
        grid=(n // t,),
        in_specs=[pl.BlockSpec((N_EXPERTS, t), lambda i: (0, jnp.minimum(i, prompt_tiles - 1))),
                  pl.BlockSpec((N_EXPERTS, t), lambda i: (0, jnp.maximum(i - prompt_tiles, 0))),
                  pl.BlockSpec((N_EXPERTS, 1), fixed)],
        out_specs=[
            pl.BlockSpec((TOP_K, t), col),
            pl.BlockSpec((TOP_K, t), col),
            pl.BlockSpec((TOP_K, t), col),
            pl.BlockSpec((N_EXPERTS, 1), fixed),
        ],
        out_shape=[
            jax.ShapeDtypeStruct((TOP_K, n), jnp.int32),
            jax.ShapeDtypeStruct((TOP_K, n), F32),
            jax.ShapeDtypeStruct((TOP_K, n), jnp.int32),
            jax.ShapeDtypeStruct((N_EXPERTS, 1), jnp.int32),
        ],
        scratch_shapes=[pltpu.VMEM((N_EXPERTS, 1), F32)],
        compiler_params=_params(1),
        name="route",
    )(logits_p, logits_s, bias_col)


def _dest_kernel(eidx_ref, rank_ref, start_ref, dest_ref):
    t = eidx_ref.shape[1]
    eio = lax.broadcasted_iota(jnp.int32, (N_EXPERTS, t), 0)
    start = start_ref[...]
    rows = [jnp.sum(jnp.where(eio == eidx_ref[k:k + 1, :], start, 0.0), axis=0, keepdims=True) for k in range(TOP_K)]
    dest_ref[...] = jnp.concatenate(rows, axis=0).astype(jnp.int32) + rank_ref[...]


def _dest(eidx_t, rank_t, pad_start_col):
    n = eidx_t.shape[1]
    t = ROW_TILE
    col = lambda i: (0, i)
    return pl.pallas_call(
        _dest_kernel,
        grid=(n // t,),
        in_specs=[pl.BlockSpec((TOP_K, t), col), pl.BlockSpec((TOP_K, t), col),
                  pl.BlockSpec((N_EXPERTS, 1), lambda i: (0, 0))],
        out_specs=pl.BlockSpec((TOP_K, t), col),
        out_shape=jax.ShapeDtypeStruct((TOP_K, n), jnp.int32),
        compiler_params=_params(1),
        name="dest",
    )(eidx_t, rank_t, pad_start_col)


def _dispatch_kernel(fill_ref, len_ref, nu_ref, dest_ref, hp_ref, hs_ref, xs_ref, zbuf, sem, fill_sem, *,
                     prompt_tiles, n_blocks):
    i = pl.program_id(0)

    @pl.when(i == 0)
    def _():
        zbuf[...] = jnp.zeros_like(zbuf)

        def fill_copy(row0, size):
            return pltpu.make_async_copy(zbuf.at[pl.ds(0, size)], xs_ref.at[pl.ds(pl.multiple_of(row0, 8), size)],
                                         fill_sem)

        def on_padding(fn):
            def body(e, c):
                base, length = fill_ref[e], len_ref[e]
                size = EXPERT_ROWS
                while size >= 8:
                    piece = fill_copy(base + (length & ~(2 * size - 1)), size)
                    pl.when((length & size) != 0)(functools.partial(fn, piece))
                    size //= 2
                return c
            lax.fori_loop(0, N_EXPERTS, body, 0)

        def on_unused_blocks(fn):
            lax.fori_loop(nu_ref[0], n_blocks, lambda b, c: (fn(fill_copy(b * EXPERT_ROWS, EXPERT_ROWS)), c)[1], 0)

        on_padding(lambda cp: cp.start())
        on_unused_blocks(lambda cp: cp.start())
        on_padding(lambda cp: cp.wait())
        on_unused_blocks(lambda cp: cp.wait())

    def scatter_rows(src_ref):
        t = src_ref.shape[0]

        def row_copy(j, k):
            return pltpu.make_async_copy(src_ref.at[pl.ds(j, 1)], xs_ref.at[pl.ds(dest_ref[k, j], 1)], sem)

        def issue(j, c):
            for k in range(TOP_K):
                row_copy(j, k).start(priority=k % DMA_PRIORITIES)
            return c

        def drain(j, c):
            for k in range(TOP_K):
                row_copy(j, k).wait()
            return c

        lax.fori_loop(0, t, issue, 0, unroll=DMA_UNROLL)
        lax.fori_loop(0, t, drain, 0, unroll=DMA_UNROLL)

    @pl.when(i < prompt_tiles)
    def _():
        scatter_rows(hp_ref)

    @pl.when(i >= prompt_tiles)
    def _():
        scatter_rows(hs_ref)


def _dispatch(fill_start, fill_len, n_used, dest_t, hp_p, hp_s, *, n_blocks):
    t = DISPATCH_ROWS
    prompt_tiles = hp_p.shape[0] // t
    sample_tiles = hp_s.shape[0] // t
    kern = functools.partial(_dispatch_kernel, prompt_tiles=prompt_tiles, n_blocks=n_blocks)
    return pl.pallas_call(
        kern,
        grid_spec=pltpu.PrefetchScalarGridSpec(
            num_scalar_prefetch=3,
            grid=(prompt_tiles + sample_tiles,),
            in_specs=[
                pl.BlockSpec((TOP_K, t), lambda i, *_: (0, i), memory_space=pltpu.SMEM),
                pl.BlockSpec((t, PACKED), lambda i, *_: (jnp.minimum(i, prompt_tiles - 1), 0)),
                pl.BlockSpec((t, PACKED), lambda i, *_: (jnp.maximum(i - prompt_tiles, 0), 0)),
            ],
            out_specs=pl.BlockSpec(memory_space=pl.ANY),
            scratch_shapes=[pltpu.VMEM((EXPERT_ROWS, PACKED), jnp.uint32), pltpu.SemaphoreType.DMA,
                            pltpu.SemaphoreType.DMA],
        ),
        out_shape=jax.ShapeDtypeStruct((n_blocks * EXPERT_ROWS, PACKED), jnp.uint32),
        compiler_params=_params(1),
        name="dispatch",
    )(fill_start, fill_len, n_used, dest_t, hp_p, hp_s)


def _block_plan(counts):
    padded = (counts + EXPERT_ROWS - 1) // EXPERT_ROWS * EXPERT_ROWS
    pad_end = jnp.cumsum(padded).astype(jnp.int32)
    pad_start = pad_end - padded
    n_used = pad_end[-1] // EXPERT_ROWS
    fill_start = ((pad_start + counts) // 8 * 8).astype(jnp.int32)
    fill_len = pad_end - fill_start
    first_block = pad_start // EXPERT_ROWS
    n_expert_blocks = (padded // EXPERT_ROWS).astype(jnp.int32)
    return pad_start, fill_start, fill_len, first_block, n_expert_blocks, n_used.reshape(1).astype(jnp.int32)


def kernel(x_prompt, x_sample, cache_k, cache_v, w_in, sink, gm_ln_g, gm_ln_b, gm_w_s, gm_b_s, w_out, ln1_g, ln1_b,
           router_w, router_bias, w_gate_e, w_up_e, w_down_e, w_gate_s, w_up_s, w_down_s, ln2_g, ln2_b):
    bp, sp = x_prompt.shape[:2]
    bs, ts = x_sample.shape[:2]
    r = cache_k.shape[2]
    assert r == WINDOW and sp % ROW_TILE == 0 and (bs * ts) % ROW_TILE == 0
    n_p, n_s = bp * sp, bs * ts
    n_total = n_p + n_s
    l = 0

    w_in_b = w_in[l].astype(BF16)
    w_out_b = w_out[l].astype(BF16)
    router_t = router_w[l].T
    r_hi = router_t.astype(BF16)
    r_lo = (router_t - r_hi.astype(F32)).astype(BF16)
    row_vec = lambda v: v.reshape(1, -1)
    gm_g, gm_b = row_vec(gm_ln_g[l]), row_vec(gm_ln_b[l])
    sink_l = sink[l].astype(F32)

    xp2 = x_prompt.reshape(n_p, D_MODEL)
    tabs_p = _rope_tables(jnp.arange(sp, dtype=jnp.int32))
    q, k, v, gu, gv = _in_proj(xp2, w_in_b, tabs_p, gm_g, gm_b, tm=ROW_TILE, gv_dtype=BF16, name="in_proj_prompt")
    nb = sp // WINDOW
    att = _attention(sink_l, q, k, v, k, v, batch=bp, nb=nb, tq=WINDOW, seqs=1, prev_blocks=nb,
                     first_block_has_no_prev=True, name="attn_prompt")
    b_tab_p = jnp.repeat(gm_b_s[l].T, HEAD_DIM, axis=1)
    gm = _gate(gu, gv, gm_w_s[l], b_tab_p, chunk=CHUNK, n_chunks=ROW_TILE // CHUNK, name="gate_prompt")
    new_kp = k.reshape(bp, sp, KV_HEADS, HEAD_DIM)[:, sp - r:][None]
    new_vp = v.reshape(bp, sp, KV_HEADS, HEAD_DIM)[:, sp - r:][None]
    h_p, hp_p, lt_p = _out_proj(att, gm, xp2, w_out_b, row_vec(ln1_g[l]), row_vec(ln1_b[l]), r_hi, r_lo,
                          name="out_proj_prompt")

    xs2 = x_sample.reshape(n_s, D_MODEL)
    pos_s = PAST_LEN + jnp.arange(ts, dtype=jnp.int32)
    tabs_s = tuple(jnp.tile(t, (bs, 1)) for t in _rope_tables(pos_s))
    q, k, v, gu, gv = _in_proj(xs2, w_in_b, tabs_s, gm_g, gm_b, tm=n_s, gv_dtype=F32, name="in_proj_sample")
    tq = 8
    pad_rows = lambda t: jnp.pad(t.reshape(bs, ts, -1), ((0, 0), (0, tq - ts), (0, 0))).reshape(bs * tq, -1)
    ck2 = cache_k[l].reshape(bs * r, KV_WIDTH)
    cv2 = cache_v[l].reshape(bs * r, KV_WIDTH)
    att = _attention(sink_l, pad_rows(q), pad_rows(k), pad_rows(v), ck2, cv2, batch=bs, nb=1, tq=tq,
                     seqs=SAMPLE_SEQS_PER_STEP, prev_blocks=1, first_block_has_no_prev=False, name="attn_sample")
    att = att.reshape(bs, tq, ATT_WIDTH)[:, :ts].reshape(n_s, ATT_WIDTH)
    w_small = gm_w_s[l][:, :ts, :ts]
    w_big = jnp.einsum("ab,hij->haibj", jnp.eye(bs, dtype=F32), w_small).reshape(GM_HEADS, n_s, n_s)
    b_tab_s = jnp.tile(jnp.repeat(gm_b_s[l][:, :ts].T, HEAD_DIM, axis=1), (bs, 1))
    gm = _gate(gu, gv, w_big, b_tab_s, chunk=n_s, n_chunks=1, name="gate_sample")
    new_ks = jnp.concatenate([cache_k[l], k.reshape(bs, ts, KV_HEADS, HEAD_DIM)], axis=1)[:, ts:][None]
    new_vs = jnp.concatenate([cache_v[l], v.reshape(bs, ts, KV_HEADS, HEAD_DIM)], axis=1)[:, ts:][None]
    new_gs = gv.reshape(bs, ts, GM_WIDTH)[None]
    h_s, hp_s, lt_s = _out_proj(att, gm, xs2, w_out_b, row_vec(ln1_g[l]), row_vec(ln1_b[l]), r_hi, r_lo,
                          name="out_proj_sample")

    eidx_t, w_t, rank_t, counts = _route(lt_p, lt_s, router_bias[l].astype(F32).reshape(N_EXPERTS, 1))
    a = n_total * TOP_K
    n_blocks = -(-(a + N_EXPERTS * (EXPERT_ROWS - 1)) // EXPERT_ROWS)
    pad_start, fill_start, fill_len, first_block, n_expert_blocks, n_used = _block_plan(counts.reshape(N_EXPERTS))
    dest_t = _dest(eidx_t, rank_t, pad_start.astype(F32).reshape(N_EXPERTS, 1))
    x_sorted = _dispatch(fill_start, fill_len, n_used, dest_t, hp_p, hp_s, n_blocks=n_blocks)
    out_sorted = _experts(first_block, n_expert_blocks, n_used, x_sorted, w_gate_e[l], w_up_e[l], w_down_e[l])
    shared = (w_gate_s[l].astype(BF16), w_up_s[l].astype(BF16), w_down_s[l].astype(BF16))
    ln2 = (row_vec(ln2_g[l]), row_vec(ln2_b[l]))
    wts = w_t.T
    y_p = _combine(dest_t, wts, h_p, *shared, *ln2, out_sorted, row_offset=0, name="combine_prompt")
    y_s = _combine(dest_t, wts, h_s, *shared, *ln2, out_sorted, row_offset=n_p, name="combine_sample")
    return (y_p.reshape(bp, sp, D_MODEL), y_s.reshape(bs, ts, D_MODEL), new_kp, new_vp, new_ks, new_vs, new_gs)
```

```python
import functools

import jax
import jax.numpy as jnp
import numpy as np
from jax import lax
from jax.experimental import pallas as pl
from jax.experimental.pallas import tpu as pltpu

D_MODEL = 1024
HEAD_DIM = 64
ATT_HEADS = 8
KV_HEADS = 2
Q_PER_KV = ATT_HEADS // KV_HEADS
GM_HEADS = 8
ATT_WIDTH = ATT_HEADS * HEAD_DIM
KV_WIDTH = KV_HEADS * HEAD_DIM
GM_WIDTH = GM_HEADS * HEAD_DIM
ROPE_WIDTH = ATT_WIDTH + KV_WIDTH
IN_WIDTH = ATT_WIDTH + 2 * KV_WIDTH + 2 * GM_WIDTH
WINDOW = 128
CHUNK = 128
PAST_LEN = 16384
ROPE_THETA = 10000.0
ATT_SCALE = HEAD_DIM ** -0.5
N_EXPERTS = 256
TOP_K = 8
N_GROUPS = 8
TOPK_GROUPS = 4
F_EXPERT = 256
ROUTED_SCALE = 2.5
LN_EPS = 1e-5
DEPTH = 1
ALPHA = (2.0 * DEPTH) ** 0.25

LANES = 128
SLABS = D_MODEL // LANES
ROW_TILE = 512
EXPERT_ROWS = 256
EXPERT_RING = 8
COMBINE_ROWS = 256
DISPATCH_ROWS = 256
ROUTE_TILE = 256
SAMPLE_SEQS_PER_STEP = 16
DMA_UNROLL = 4
DMA_PRIORITIES = 2
VMEM_LIMIT = 56 * 1024 * 1024

F32 = jnp.float32
BF16 = jnp.bfloat16


def _params(n_axes):
    return pltpu.CompilerParams(dimension_semantics=("arbitrary",) * n_axes, vmem_limit_bytes=VMEM_LIMIT)


def _layer_norm(x, g, b):
    mu = jnp.mean(x, axis=-1, keepdims=True)
    xc = x - mu
    var = jnp.mean(xc * xc, axis=-1, keepdims=True)
    return xc * lax.rsqrt(var + LN_EPS) * g + b


def _store_row_tiles(ref2d, x):
    m = x.shape[0]
    for s in range(SLABS):
        ref2d[pl.ds(s, m, stride=SLABS), :] = x[:, s * LANES:(s + 1) * LANES]


def _load_row_tiles(ref2d, m, first_row=0):
    return jnp.concatenate([ref2d[pl.ds(first_row * SLABS + s, m, stride=SLABS), :] for s in range(SLABS)], axis=1)


def _gelu(x):
    return 0.5 * x * (1.0 + lax.erf(x * np.float32(np.sqrt(0.5))))


def _in_proj_kernel(x_ref, w_ref, cos_ref, sa_ref, sb_ref, g_ref, b_ref, q_ref, k_ref, v_ref, gu_ref, gv_ref):
    x = x_ref[...].astype(BF16)
    zr = jnp.dot(x, w_ref[:, :ROPE_WIDTH], preferred_element_type=F32)
    pieces = []
    for c in range(ROPE_WIDTH // LANES):
        sl = slice(c * LANES, (c + 1) * LANES)
        zc = zr[:, sl]
        pieces.append(zc * cos_ref[:, sl]
                      + pltpu.roll(zc, LANES - HEAD_DIM // 2, 1) * sa_ref[:, sl]
                      + pltpu.roll(zc, HEAD_DIM // 2, 1) * sb_ref[:, sl])
    for c in range(ATT_WIDTH // LANES):
        q_ref[:, c * LANES:(c + 1) * LANES] = pieces[c].astype(q_ref.dtype)
    k_ref[...] = pieces[ATT_WIDTH // LANES]
    v_ref[...] = jnp.dot(x, w_ref[:, ROPE_WIDTH:ROPE_WIDTH + KV_WIDTH], preferred_element_type=F32)
    g0 = ROPE_WIDTH + KV_WIDTH
    zu = jnp.dot(x, w_ref[:, g0:g0 + GM_WIDTH], preferred_element_type=F32)
    gu_ref[...] = _gelu(zu).astype(gu_ref.dtype)
    zv = jnp.dot(x, w_ref[:, g0 + GM_WIDTH:g0 + 2 * GM_WIDTH], preferred_element_type=F32)
    gv = _layer_norm(_gelu(zv), g_ref[...], b_ref[...])
    gv_ref[...] = gv.astype(gv_ref.dtype)


def _in_proj(x2, w_in_b, tabs, ln_g, ln_b, *, tm, gv_dtype, name):
    n = x2.shape[0]
    cos_t, sa_t, sb_t = tabs
    period = cos_t.shape[0] // tm
    row = lambda i: (i, 0)
    tab = lambda i: (i % period, 0)
    fixed = lambda i: (0, 0)
    return pl.pallas_call(
        _in_proj_kernel,
        grid=(n // tm,),
        in_specs=[
            pl.BlockSpec((tm, D_MODEL), row),
            pl.BlockSpec((D_MODEL, IN_WIDTH), fixed),
            pl.BlockSpec((tm, ROPE_WIDTH), tab),
            pl.BlockSpec((tm, ROPE_WIDTH), tab),
            pl.BlockSpec((tm, ROPE_WIDTH), tab),
            pl.BlockSpec((1, GM_WIDTH), fixed),
            pl.BlockSpec((1, GM_WIDTH), fixed),
        ],
        out_specs=[
            pl.BlockSpec((tm, ATT_WIDTH), row),
            pl.BlockSpec((tm, KV_WIDTH), row),
            pl.BlockSpec((tm, KV_WIDTH), row),
            pl.BlockSpec((tm, GM_WIDTH), row),
            pl.BlockSpec((tm, GM_WIDTH), row),
        ],
        out_shape=[
            jax.ShapeDtypeStruct((n, ATT_WIDTH), BF16),
            jax.ShapeDtypeStruct((n, KV_WIDTH), F32),
            jax.ShapeDtypeStruct((n, KV_WIDTH), F32),
            jax.ShapeDtypeStruct((n, GM_WIDTH), BF16),
            jax.ShapeDtypeStruct((n, GM_WIDTH), gv_dtype),
        ],
        compiler_params=_params(1),
        name=name,
    )(x2, w_in_b, cos_t, sa_t, sb_t, ln_g, ln_b)


def _rope_tables(pos):
    half = HEAD_DIM // 2
    inv = ROPE_THETA ** (-jnp.arange(half, dtype=F32) * 2.0 / HEAD_DIM)
    ang = pos.astype(F32)[:, None] * inv[None, :]
    cos, sin = jnp.cos(ang), jnp.sin(ang)
    zero = jnp.zeros_like(sin)
    heads = ROPE_WIDTH // HEAD_DIM
    scale = jnp.where(jnp.arange(ROPE_WIDTH) < ATT_WIDTH, ATT_SCALE, 1.0).astype(F32)[None, :]
    cos_t = jnp.tile(jnp.concatenate([cos, cos], axis=1), (1, heads)) * scale
    sa_t = jnp.tile(jnp.concatenate([-sin, zero], axis=1), (1, heads)) * scale
    sb_t = jnp.tile(jnp.concatenate([zero, sin], axis=1), (1, heads)) * scale
    return cos_t, sa_t, sb_t


def _attn_kernel(sink_ref, q_ref, kc_ref, vc_ref, kp_ref, vp_ref, o_ref, *, tq, seqs, stack, first_block_has_no_prev):
    nk = WINDOW + tq
    rows = stack * tq
    qi = lax.broadcasted_iota(jnp.int32, (rows, nk), 0) & (tq - 1)
    ks = lax.broadcasted_iota(jnp.int32, (rows, nk), 1)
    mask = (ks >= qi) & (ks <= qi + WINDOW)
    if first_block_has_no_prev:
        mask = mask & ((pl.program_id(1) > 0) | (ks >= WINDOW))
    sinks = [jnp.concatenate([jnp.full((tq, 1), sink_ref[h0 + j], F32) for j in range(stack)], axis=0)
             for h0 in range(0, ATT_HEADS, stack)]
    for b in range(seqs):
        qrows = slice(b * tq, (b + 1) * tq)
        prows = slice(b * WINDOW, (b + 1) * WINDOW)
        q = q_ref[qrows, :]
        kk = jnp.concatenate([kp_ref[prows, :], kc_ref[qrows, :]], axis=0).astype(BF16)
        vv = jnp.concatenate([vp_ref[prows, :], vc_ref[qrows, :]], axis=0).astype(BF16)
        outs = []
        for i, h0 in enumerate(range(0, ATT_HEADS, stack)):
            g = h0 // Q_PER_KV
            kg = kk[:, g * HEAD_DIM:(g + 1) * HEAD_DIM]
            vg = vv[:, g * HEAD_DIM:(g + 1) * HEAD_DIM]
            qg = jnp.concatenate([q[:, (h0 + j) * HEAD_DIM:(h0 + j + 1) * HEAD_DIM] for j in range(stack)], axis=0)
            s = lax.dot_general(qg, kg, (((1,), (1,)), ((), ())), preferred_element_type=F32)
            s = jnp.where(mask, s, -jnp.inf)
            m = jnp.maximum(jnp.max(s, axis=-1, keepdims=True), sinks[i])
            p = jnp.exp(s - m)
            denom = jnp.sum(p, axis=-1, keepdims=True) + jnp.exp(sinks[i] - m)
            o = jnp.dot((p / denom).astype(BF16), vg, preferred_element_type=F32)
            outs.extend(o[j * tq:(j + 1) * tq, :] for j in range(stack))
        o_ref[qrows, :] = jnp.concatenate(outs, axis=1).astype(o_ref.dtype)


def _attention(sink, q2, k2, v2, kprev2, vprev2, *, batch, nb, tq, seqs, prev_blocks, first_block_has_no_prev, name):
    assert tq & (tq - 1) == 0 and batch % seqs == 0 and (seqs == 1 or nb == prev_blocks == 1)
    cur = lambda b, n, s: (b * nb + n, 0)
    prev = lambda b, n, s: (b * prev_blocks + jnp.maximum(n - 1, 0), 0)
    stack = Q_PER_KV if Q_PER_KV * tq <= WINDOW else 1
    kern = functools.partial(_attn_kernel, tq=tq, seqs=seqs, stack=stack,
                             first_block_has_no_prev=first_block_has_no_prev)
    return pl.pallas_call(
        kern,
        grid_spec=pltpu.PrefetchScalarGridSpec(
            num_scalar_prefetch=1,
            grid=(batch // seqs, nb),
            in_specs=[
                pl.BlockSpec((seqs * tq, ATT_WIDTH), cur),
                pl.BlockSpec((seqs * tq, KV_WIDTH), cur),
                pl.BlockSpec((seqs * tq, KV_WIDTH), cur),
                pl.BlockSpec((seqs * WINDOW, KV_WIDTH), prev),
                pl.BlockSpec((seqs * WINDOW, KV_WIDTH), prev),
            ],
            out_specs=pl.BlockSpec((seqs * tq, ATT_WIDTH), cur),
        ),
        out_shape=jax.ShapeDtypeStruct(q2.shape, BF16),
        compiler_params=_params(2),
        name=name,
    )(sink, q2, k2, v2, kprev2, vprev2)


def _gate_kernel(gu_ref, gv_ref, w_ref, b_ref, o_ref, *, chunk, n_chunks):
    ri = lax.broadcasted_iota(jnp.int32, (chunk, chunk), 0)
    ci = lax.broadcasted_iota(jnp.int32, (chunk, chunk), 1)
    ws = [jnp.where(ci <= ri, w_ref[h], 0.0).astype(BF16) for h in range(GM_HEADS)]
    for c in range(n_chunks):
        rows = slice(c * chunk, (c + 1) * chunk)
        gv = gv_ref[rows, :].astype(BF16)
        sv = jnp.concatenate(
            [jnp.dot(ws[h], gv[:, h * HEAD_DIM:(h + 1) * HEAD_DIM], preferred_element_type=F32)
             for h in range(GM_HEADS)], axis=1)
        o_ref[rows, :] = (gu_ref[rows, :].astype(F32) * (sv + b_ref[...])).astype(o_ref.dtype)


def _gate(gu2, gv2, w_s, b_tab, *, chunk, n_chunks, name):
    n = gu2.shape[0]
    tm = chunk * n_chunks
    row = lambda i: (i, 0)
    kern = functools.partial(_gate_kernel, chunk=chunk, n_chunks=n_chunks)
    return pl.pallas_call(
        kern,
        grid=(n // tm,),
        in_specs=[
            pl.BlockSpec((tm, GM_WIDTH), row),
            pl.BlockSpec((tm, GM_WIDTH), row),
            pl.BlockSpec((GM_HEADS, chunk, chunk), lambda i: (0, 0, 0)),
            pl.BlockSpec((chunk, GM_WIDTH), lambda i: (0, 0)),
        ],
        out_specs=pl.BlockSpec((tm, GM_WIDTH), row),
        out_shape=jax.ShapeDtypeStruct((n, GM_WIDTH), BF16),
        compiler_params=_params(1),
        name=name,
    )(gu2, gv2, w_s, b_tab)


def _out_proj_kernel(att_ref, gm_ref, x_ref, wo_ref, g_ref, b_ref, rh_ref, rl_ref, h_ref, lg_ref):
    mix = jnp.dot(att_ref[...], wo_ref[:ATT_WIDTH, :], preferred_element_type=F32)
    mix = mix + jnp.dot(gm_ref[...], wo_ref[ATT_WIDTH:, :], preferred_element_type=F32)
    h = _layer_norm(ALPHA * x_ref[...] + mix, g_ref[...], b_ref[...])
    _store_row_tiles(h_ref, h)
    h_hi = h.astype(BF16)
    h_lo = (h - h_hi.astype(F32)).astype(BF16)
    nt = (((1,), (1,)), ((), ()))
    lg = lax.dot_general(rh_ref[...], h_hi, nt, preferred_element_type=F32)
    lg = lg + lax.dot_general(rh_ref[...], h_lo, nt, preferred_element_type=F32)
    lg = lg + lax.dot_general(rl_ref[...], h_hi, nt, preferred_element_type=F32)
    lg_ref[...] = lg


def _out_proj(att2, gm2, x2, w_out_b, ln_g, ln_b, r_hi, r_lo, *, name):
    n = x2.shape[0]
    tm = ROW_TILE
    row = lambda i: (i, 0)
    fixed = lambda i: (0, 0)
    return pl.pallas_call(
        _out_proj_kernel,
        grid=(n // tm,),
        in_specs=[
            pl.BlockSpec((tm, ATT_WIDTH), row),
            pl.BlockSpec((tm, GM_WIDTH), row),
            pl.BlockSpec((tm, D_MODEL), row),
            pl.BlockSpec((D_MODEL, D_MODEL), fixed),
            pl.BlockSpec((1, D_MODEL), fixed),
            pl.BlockSpec((1, D_MODEL), fixed),
            pl.BlockSpec((N_EXPERTS, D_MODEL), fixed),
            pl.BlockSpec((N_EXPERTS, D_MODEL), fixed),
        ],
        out_specs=[
            pl.BlockSpec((tm * SLABS, LANES), row),
            pl.BlockSpec((N_EXPERTS, tm), lambda i: (0, i)),
        ],
        out_shape=[
            jax.ShapeDtypeStruct((n * SLABS, LANES), F32),
            jax.ShapeDtypeStruct((N_EXPERTS, n), F32),
        ],
        compiler_params=_params(1),
        name=name,
    )(att2, gm2, x2, w_out_b, ln_g, ln_b, r_hi, r_lo)


def _experts_kernel(first_ref, nblk_ref, nu_ref, x_hbm, wg_ref, wu_ref, wd_ref, o_hbm, xbuf, obuf, wg_s, wu_s, wd_s,
                    in_sem, out_sem, fill_sem, *, n_blocks):
    e = pl.program_id(0)
    nb = nblk_ref[e]
    b0 = first_ref[e]
    n_used = nu_ref[0]
    ahead = EXPERT_RING // 2

    def rows_of(block):
        return pl.ds(pl.multiple_of(block * EXPERT_ROWS, EXPERT_ROWS), EXPERT_ROWS)

    def slot_of(block):
        return block & (EXPERT_RING - 1)

    def slab(s):
        return pl.ds(s * LANES, LANES)

    def in_copies(block):
        slot = slot_of(block)
        return [pltpu.make_async_copy(x_hbm.at[rows_of(block), s, :], xbuf.at[slot, :, slab(s)], in_sem.at[slot])
                for s in range(SLABS)]

    def out_copies(block, slot=None, sem=None):
        slot = slot_of(block) if slot is None else slot
        sem = out_sem.at[slot] if sem is None else sem
        return [pltpu.make_async_copy(obuf.at[slot, :, slab(s)], o_hbm.at[rows_of(block), s, :], sem)
                for s in range(SLABS)]

    def start(copies):
        for cp in copies:
            cp.start()

    def wait(copies):
        for cp in copies:
            cp.wait()

    @pl.when(e == 0)
    def _():
        for j in range(ahead):
            pl.when(j < n_used)(lambda j=j: start(in_copies(j)))

    def process(block, count):
        blocks = [block + j for j in range(count)]
        for blk in blocks:
            wait(in_copies(blk))
        x = [xbuf[slot_of(blk)].astype(BF16) for blk in blocks]
        x = x[0] if count == 1 else jnp.concatenate(x, axis=0)
        for blk in blocks:
            pl.when(blk + ahead < n_used)(lambda blk=blk: start(in_copies(blk + ahead)))
            pl.when(blk >= ahead)(lambda blk=blk: wait(out_copies(blk - ahead)))
        a = jnp.dot(x, wg_s[...], preferred_element_type=F32)
        u = jnp.dot(x, wu_s[...], preferred_element_type=F32)
        hb = (a * jax.nn.sigmoid(a) * u).astype(BF16)
        o = jnp.dot(hb, wd_s[...], preferred_element_type=F32)
        for j, blk in enumerate(blocks):
            obuf[slot_of(blk)] = o[j * EXPERT_ROWS:(j + 1) * EXPERT_ROWS, :]
            start(out_copies(blk))

    @pl.when(nb > 0)
    def _():
        wg_s[...] = wg_ref[0].astype(BF16)
        wu_s[...] = wu_ref[0].astype(BF16)
        wd_s[...] = wd_ref[0].astype(BF16)

        def pair(i, carry):
            process(b0 + 2 * i, 2)
            return carry

        lax.fori_loop(0, nb // 2, pair, 0)
        pl.when(nb % 2 == 1)(lambda: process(b0 + nb - 1, 1))

    @pl.when(e == N_EXPERTS - 1)
    def _():
        for j in range(ahead):
            pl.when(n_used - 1 - j >= 0)(lambda j=j: wait(out_copies(n_used - 1 - j)))
        obuf[0] = jnp.zeros((EXPERT_ROWS, D_MODEL), F32)

        def on_unused_blocks(fn):
            def body(b, c):
                fn(out_copies(b, slot=0, sem=fill_sem))
                return c
            lax.fori_loop(n_used, n_blocks, body, 0)

        on_unused_blocks(start)
        on_unused_blocks(wait)


def _experts(first_block, n_expert_blocks, n_used, x_sorted, w_gate_e, w_up_e, w_down_e):
    rows = x_sorted.shape[0]
    wmap = lambda e, *_: (e, 0, 0)
    kern = functools.partial(_experts_kernel, n_blocks=rows // EXPERT_ROWS)
    return pl.pallas_call(
        kern,
        grid_spec=pltpu.PrefetchScalarGridSpec(
            num_scalar_prefetch=3,
            grid=(N_EXPERTS,),
            in_specs=[
                pl.BlockSpec(memory_space=pl.ANY),
                pl.BlockSpec((1, D_MODEL, F_EXPERT), wmap),
                pl.BlockSpec((1, D_MODEL, F_EXPERT), wmap),
                pl.BlockSpec((1, F_EXPERT, D_MODEL), wmap),
            ],
            out_specs=pl.BlockSpec(memory_space=pl.ANY),
            scratch_shapes=[
                pltpu.VMEM((EXPERT_RING, EXPERT_ROWS, D_MODEL), F32),
                pltpu.VMEM((EXPERT_RING, EXPERT_ROWS, D_MODEL), F32),
                pltpu.VMEM((D_MODEL, F_EXPERT), BF16),
                pltpu.VMEM((D_MODEL, F_EXPERT), BF16),
                pltpu.VMEM((F_EXPERT, D_MODEL), BF16),
                pltpu.SemaphoreType.DMA((EXPERT_RING,)),
                pltpu.SemaphoreType.DMA((EXPERT_RING,)),
                pltpu.SemaphoreType.DMA,
            ],
        ),
        out_shape=jax.ShapeDtypeStruct(x_sorted.shape, F32),
        compiler_params=_params(1),
        name="experts",
    )(first_block, n_expert_blocks, n_used, x_sorted, w_gate_e, w_up_e, w_down_e)


def _combine_kernel(dest_ref, w_ref, h_ref, wg_ref, wu_ref, wd_ref, ln_g_ref, ln_b_ref, os_ref, y_ref, gbuf, sem):
    t = y_ref.shape[0]
    gtiles = gbuf.reshape(TOP_K * t, SLABS, LANES)

    def row_copy(j, k):
        return pltpu.make_async_copy(os_ref.at[pl.ds(dest_ref[k, j], 1)], gtiles.at[pl.ds(k * t + j, 1)], sem)

    def issue(j, c):
        for k in range(TOP_K):
            row_copy(j, k).start(priority=k % DMA_PRIORITIES)
        return c

    def drain(j, c):
        for k in range(TOP_K):
            row_copy(j, k).wait()
        return c

    lax.fori_loop(0, t, issue, 0, unroll=DMA_UNROLL)
    h = _load_row_tiles(h_ref, t)
    hb = h.astype(BF16)
    a = jnp.dot(hb, wg_ref[...], preferred_element_type=F32)
    u = jnp.dot(hb, wu_ref[...], preferred_element_type=F32)
    ffn = jnp.dot((a * jax.nn.sigmoid(a) * u).astype(BF16), wd_ref[...], preferred_element_type=F32)
    lax.fori_loop(0, t, drain, 0, unroll=DMA_UNROLL)
    w = w_ref[...]
    for k in range(TOP_K):
        ffn = ffn + w[:, k:k + 1] * _load_row_tiles(gbuf, t, first_row=k * t)
    y_ref[...] = _layer_norm(ALPHA * h + ffn, ln_g_ref[...], ln_b_ref[...])


def _combine(dest_t, w2, h_tiled, wg_b, wu_b, wd_b, ln_g, ln_b, out_sorted, *, row_offset, name):
    n = h_tiled.shape[0] // SLABS
    tm = COMBINE_ROWS
    off = row_offset // tm
    row = lambda i: (i, 0)
    fixed = lambda i: (0, 0)
    return pl.pallas_call(
        _combine_kernel,
        grid=(n // tm,),
        in_specs=[
            pl.BlockSpec((TOP_K, tm), lambda i: (0, i + off), memory_space=pltpu.SMEM),
            pl.BlockSpec((tm, TOP_K), lambda i: (i + off, 0)),
            pl.BlockSpec((tm * SLABS, LANES), row),
            pl.BlockSpec((D_MODEL, F_EXPERT), fixed),
            pl.BlockSpec((D_MODEL, F_EXPERT), fixed),
            pl.BlockSpec((F_EXPERT, D_MODEL), fixed),
            pl.BlockSpec((1, D_MODEL), fixed),
            pl.BlockSpec((1, D_MODEL), fixed),
            pl.BlockSpec(memory_space=pl.ANY),
        ],
        out_specs=pl.BlockSpec((tm, D_MODEL), row),
        out_shape=jax.ShapeDtypeStruct((n, D_MODEL), F32),
        scratch_shapes=[pltpu.VMEM((TOP_K * tm * SLABS, LANES), F32), pltpu.SemaphoreType.DMA],
        compiler_params=_params(1),
        name=name,
    )(dest_t, w2, h_tiled, wg_b, wu_b, wd_b, ln_g, ln_b, out_sorted)


def _route_kernel(lgp_ref, lgs_ref, bias_ref, eidx_ref, w_ref, rank_ref, cnt_ref, carry_ref, *, prompt_tiles):
    @pl.when(pl.program_id(0) == 0)
    def _():
        carry_ref[...] = jnp.zeros_like(carry_ref)

    t = lgp_ref.shape[1]
    gsz = N_EXPERTS // N_GROUPS
    neg = -jnp.inf
    s = jax.nn.sigmoid(jnp.where(pl.program_id(0) < prompt_tiles, lgp_ref[...], lgs_ref[...]))
    biased = s + bias_ref[...]
    io_g = lax.broadcasted_iota(jnp.int32, (gsz, t), 0)
    grp_rows = []
    for g in range(N_GROUPS):
        blk = biased[g * gsz:(g + 1) * gsz, :]
        m1 = jnp.max(blk, axis=0, keepdims=True)
        i1 = jnp.min(jnp.where(blk == m1, io_g, gsz), axis=0, keepdims=True)
        m2 = jnp.max(jnp.where(io_g == i1, neg, blk), axis=0, keepdims=True)
        grp_rows.append(m1 + m2)
    gs = jnp.concatenate(grp_rows, axis=0)
    io8 = lax.broadcasted_iota(jnp.int32, (N_GROUPS, t), 0)
    gsel = jnp.zeros((N_GROUPS, t), jnp.int32)
    for _ in range(TOPK_GROUPS):
        m = jnp.max(gs, axis=0, keepdims=True)
        gi = jnp.min(jnp.where(gs == m, io8, N_GROUPS), axis=0, keepdims=True)
        hit = io8 == gi
        gsel = jnp.where(hit, 1, gsel)
        gs = jnp.where(hit, neg, gs)
    masked = jnp.concatenate(
        [jnp.where(gsel[g:g + 1, :] > 0, biased[g * gsz:(g + 1) * gsz, :], neg) for g in range(N_GROUPS)], axis=0)

    eio = lax.broadcasted_iota(jnp.int32, (N_EXPERTS, t), 0)
    cur = masked
    idx_rows, w_rows = [], []
    for _ in range(TOP_K):
        m = jnp.max(cur, axis=0, keepdims=True)
        idx = jnp.min(jnp.where(cur == m, eio, N_EXPERTS), axis=0, keepdims=True)
        hit = eio == idx
        w_rows.append(jnp.sum(jnp.where(hit, s, 0.0), axis=0, keepdims=True))
        cur = jnp.where(hit, neg, cur)
        idx_rows.append(idx)
    sel = jnp.where(cur != masked, 1.0, 0.0)

    tri = jnp.where(lax.broadcasted_iota(jnp.int32, (t, t), 0) < lax.broadcasted_iota(jnp.int32, (t, t), 1), 1.0, 0.0)
    pref = jnp.dot(sel.astype(BF16), tri.astype(BF16), preferred_element_type=F32) + carry_ref[...]
    rank_rows = [jnp.sum(jnp.where(eio == idx_rows[k], pref, 0.0), axis=0, keepdims=True) for k in range(TOP_K)]
    carry_ref[...] += jnp.sum(sel, axis=1, keepdims=True)

    wk = jnp.concatenate(w_rows, axis=0)
    eidx_ref[...] = jnp.concatenate(idx_rows, axis=0)
    w_ref[...] = wk / jnp.sum(wk, axis=0, keepdims=True) * ROUTED_SCALE
    rank_ref[...] = jnp.concatenate(rank_rows, axis=0).astype(jnp.int32)
    cnt_ref[...] = carry_ref[...].astype(jnp.int32)


def _route(logits_p, logits_s, bias_col):
    t = ROUTE_TILE
    prompt_tiles = logits_p.shape[1] // t
    n = logits_p.shape[1] + logits_s.shape[1]
    col = lambda i: (0, i)
    fixed = lambda i: (0, 0)
    kern = functools.partial(_route_kernel, prompt_tiles=prompt_tiles)
    return pl.pallas_call(
        kern,
        grid=(n // t,),
        in_specs=[pl.BlockSpec((N_EXPERTS, t), lambda i: (0, jnp.minimum(i, prompt_tiles - 1))),
                  pl.BlockSpec((N_EXPERTS, t), lambda i: (0, jnp.maximum(i - prompt_tiles, 0))),
                  pl.BlockSpec((N_EXPERTS, 1), fixed)],
        out_specs=[
            pl.BlockSpec((TOP_K, t), col),
            pl.BlockSpec((TOP_K, t), col),
            pl.BlockSpec((TOP_K, t), col),
            pl.BlockSpec((N_EXPERTS, 1), fixed),
        ],
        out_shape=[
            jax.ShapeDtypeStruct((TOP_K, n), jnp.int32),
            jax.ShapeDtypeStruct((TOP_K, n), F32),
            jax.ShapeDtypeStruct((TOP_K, n), jnp.int32),
            jax.ShapeDtypeStruct((N_EXPERTS, 1), jnp.int32),
        ],
        scratch_shapes=[pltpu.VMEM((N_EXPERTS, 1), F32)],
        compiler_params=_params(1),
        name="route",
    )(logits_p, logits_s, bias_col)


def _dest_kernel(eidx_ref, rank_ref, start_ref, dest_ref):
    t = eidx_ref.shape[1]
    eio = lax.broadcasted_iota(jnp.int32, (N_EXPERTS, t), 0)
    start = start_ref[...]
    rows = [jnp.sum(jnp.where(eio == eidx_ref[k:k + 1, :], start, 0.0), axis=0, keepdims=True) for k in range(TOP_K)]
    dest_ref[...] = jnp.concatenate(rows, axis=0).astype(jnp.int32) + rank_ref[...]


def _dest(eidx_t, rank_t, pad_start_col):
    n = eidx_t.shape[1]
    t = ROW_TILE
    col = lambda i: (0, i)
    return pl.pallas_call(
        _dest_kernel,
        grid=(n // t,),
        in_specs=[pl.BlockSpec((TOP_K, t), col), pl.BlockSpec((TOP_K, t), col),
                  pl.BlockSpec((N_EXPERTS, 1), lambda i: (0, 0))],
        out_specs=pl.BlockSpec((TOP_K, t), col),
        out_shape=jax.ShapeDtypeStruct((TOP_K, n), jnp.int32),
        compiler_params=_params(1),
        name="dest",
    )(eidx_t, rank_t, pad_start_col)


def _dispatch_kernel(fill_ref, len_ref, nu_ref, dest_ref, hp_ref, hs_ref, xs_ref, zbuf, sem, fill_sem, *,
                     prompt_tiles, n_blocks):
    i = pl.program_id(0)

    @pl.when(i == 0)
    def _():
        zbuf[...] = jnp.zeros_like(zbuf)

        def fill_copy(row0, size):
            return pltpu.make_async_copy(zbuf.at[pl.ds(0, size)], xs_ref.at[pl.ds(row0, size)], fill_sem)

        def on_padding(fn):
            def body(e, c):
                base, length = fill_ref[e], len_ref[e]
                size = EXPERT_ROWS // 2
                while size >= 1:
                    piece = fill_copy(base + (length & ~(2 * size - 1)), size)
                    pl.when((length & size) != 0)(functools.partial(fn, piece))
                    size //= 2
                return c
            lax.fori_loop(0, N_EXPERTS, body, 0)

        def on_unused_blocks(fn):
            lax.fori_loop(nu_ref[0], n_blocks, lambda b, c: (fn(fill_copy(b * EXPERT_ROWS, EXPERT_ROWS)), c)[1], 0)

        on_padding(lambda cp: cp.start())
        on_unused_blocks(lambda cp: cp.start())
        on_padding(lambda cp: cp.wait())
        on_unused_blocks(lambda cp: cp.wait())

    def scatter_rows(src_ref):
        t = src_ref.shape[0]

        def row_copy(j, k):
            return pltpu.make_async_copy(src_ref.at[pl.ds(j, 1)], xs_ref.at[pl.ds(dest_ref[k, j], 1)], sem)

        def issue(j, c):
            for k in range(TOP_K):
                row_copy(j, k).start(priority=k % DMA_PRIORITIES)
            return c

        def drain(j, c):
            for k in range(TOP_K):
                row_copy(j, k).wait()
            return c

        lax.fori_loop(0, t, issue, 0, unroll=DMA_UNROLL)
        lax.fori_loop(0, t, drain, 0, unroll=DMA_UNROLL)

    @pl.when(i < prompt_tiles)
    def _():
        scatter_rows(hp_ref)

    @pl.when(i >= prompt_tiles)
    def _():
        scatter_rows(hs_ref)


def _dispatch(fill_start, fill_len, n_used, dest_t, hp_p, hp_s, *, n_blocks):
    t = DISPATCH_ROWS
    prompt_tiles = hp_p.shape[0] // t
    sample_tiles = hp_s.shape[0] // t
    tile = (t, SLABS, LANES)
    kern = functools.partial(_dispatch_kernel, prompt_tiles=prompt_tiles, n_blocks=n_blocks)
    return pl.pallas_call(
        kern,
        grid_spec=pltpu.PrefetchScalarGridSpec(
            num_scalar_prefetch=3,
            grid=(prompt_tiles + sample_tiles,),
            in_specs=[
                pl.BlockSpec((TOP_K, t), lambda i, *_: (0, i), memory_space=pltpu.SMEM),
                pl.BlockSpec(tile, lambda i, *_: (jnp.minimum(i, prompt_tiles - 1), 0, 0)),
                pl.BlockSpec(tile, lambda i, *_: (jnp.maximum(i - prompt_tiles, 0), 0, 0)),
            ],
            out_specs=pl.BlockSpec(memory_space=pl.ANY),
            scratch_shapes=[pltpu.VMEM((EXPERT_ROWS, SLABS, LANES), F32), pltpu.SemaphoreType.DMA,
                            pltpu.SemaphoreType.DMA],
        ),
        out_shape=jax.ShapeDtypeStruct((n_blocks * EXPERT_ROWS, SLABS, LANES), F32),
        compiler_params=_params(1),
        name="dispatch",
    )(fill_start, fill_len, n_used, dest_t, hp_p, hp_s)


def _block_plan(counts):
    padded = (counts + EXPERT_ROWS - 1) // EXPERT_ROWS * EXPERT_ROWS
    pad_end = jnp.cumsum(padded).astype(jnp.int32)
    pad_start = pad_end - padded
    n_used = pad_end[-1] // EXPERT_ROWS
    fill_start = pad_start + counts
    fill_len = pad_end - fill_start
    first_block = pad_start // EXPERT_ROWS
    n_expert_blocks = (padded // EXPERT_ROWS).astype(jnp.int32)
    return pad_start, fill_start, fill_len, first_block, n_expert_blocks, n_used.reshape(1).astype(jnp.int32)


def kernel(x_prompt, x_sample, cache_k, cache_v, w_in, sink, gm_ln_g, gm_ln_b, gm_w_s, gm_b_s, w_out, ln1_g, ln1_b,
           router_w, router_bias, w_gate_e, w_up_e, w_down_e, w_gate_s, w_up_s, w_down_s, ln2_g, ln2_b):
    bp, sp = x_prompt.shape[:2]
    bs, ts = x_sample.shape[:2]
    r = cache_k.shape[2]
    assert r == WINDOW and sp % ROW_TILE == 0 and (bs * ts) % ROW_TILE == 0
    n_p, n_s = bp * sp, bs * ts
    n_total = n_p + n_s
    l = 0

    w_in_b = w_in[l].astype(BF16)
    w_out_b = w_out[l].astype(BF16)
    router_t = router_w[l].T
    r_hi = router_t.astype(BF16)
    r_lo = (router_t - r_hi.astype(F32)).astype(BF16)
    row_vec = lambda v: v.reshape(1, -1)
    gm_g, gm_b = row_vec(gm_ln_g[l]), row_vec(gm_ln_b[l])
    sink_l = sink[l].astype(F32)

    xp2 = x_prompt.reshape(n_p, D_MODEL)
    tabs_p = _rope_tables(jnp.arange(sp, dtype=jnp.int32))
    q, k, v, gu, gv = _in_proj(xp2, w_in_b, tabs_p, gm_g, gm_b, tm=ROW_TILE, gv_dtype=BF16, name="in_proj_prompt")
    nb = sp // WINDOW
    att = _attention(sink_l, q, k, v, k, v, batch=bp, nb=nb, tq=WINDOW, seqs=1, prev_blocks=nb,
                     first_block_has_no_prev=True, name="attn_prompt")
    b_tab_p = jnp.repeat(gm_b_s[l].T, HEAD_DIM, axis=1)
    gm = _gate(gu, gv, gm_w_s[l], b_tab_p, chunk=CHUNK, n_chunks=ROW_TILE // CHUNK, name="gate_prompt")
    new_kp = k.reshape(bp, sp, KV_HEADS, HEAD_DIM)[:, sp - r:][None]
    new_vp = v.reshape(bp, sp, KV_HEADS, HEAD_DIM)[:, sp - r:][None]
    h_p, lt_p = _out_proj(att, gm, xp2, w_out_b, row_vec(ln1_g[l]), row_vec(ln1_b[l]), r_hi, r_lo,
                          name="out_proj_prompt")

    xs2 = x_sample.reshape(n_s, D_MODEL)
    pos_s = PAST_LEN + jnp.arange(ts, dtype=jnp.int32)
    tabs_s = tuple(jnp.tile(t, (bs, 1)) for t in _rope_tables(pos_s))
    q, k, v, gu, gv = _in_proj(xs2, w_in_b, tabs_s, gm_g, gm_b, tm=n_s, gv_dtype=F32, name="in_proj_sample")
    tq = 8
    pad_rows = lambda t: jnp.pad(t.reshape(bs, ts, -1), ((0, 0), (0, tq - ts), (0, 0))).reshape(bs * tq, -1)
    ck2 = cache_k[l].reshape(bs * r, KV_WIDTH)
    cv2 = cache_v[l].reshape(bs * r, KV_WIDTH)
    att = _attention(sink_l, pad_rows(q), pad_rows(k), pad_rows(v), ck2, cv2, batch=bs, nb=1, tq=tq,
                     seqs=SAMPLE_SEQS_PER_STEP, prev_blocks=1, first_block_has_no_prev=False, name="attn_sample")
    att = att.reshape(bs, tq, ATT_WIDTH)[:, :ts].reshape(n_s, ATT_WIDTH)
    w_small = gm_w_s[l][:, :ts, :ts]
    w_big = jnp.einsum("ab,hij->haibj", jnp.eye(bs, dtype=F32), w_small).reshape(GM_HEADS, n_s, n_s)
    b_tab_s = jnp.tile(jnp.repeat(gm_b_s[l][:, :ts].T, HEAD_DIM, axis=1), (bs, 1))
    gm = _gate(gu, gv, w_big, b_tab_s, chunk=n_s, n_chunks=1, name="gate_sample")
    new_ks = jnp.concatenate([cache_k[l], k.reshape(bs, ts, KV_HEADS, HEAD_DIM)], axis=1)[:, ts:][None]
    new_vs = jnp.concatenate([cache_v[l], v.reshape(bs, ts, KV_HEADS, HEAD_DIM)], axis=1)[:, ts:][None]
    new_gs = gv.reshape(bs, ts, GM_WIDTH)[None]
    h_s, lt_s = _out_proj(att, gm, xs2, w_out_b, row_vec(ln1_g[l]), row_vec(ln1_b[l]), r_hi, r_lo,
                          name="out_proj_sample")

    eidx_t, w_t, rank_t, counts = _route(lt_p, lt_s, router_bias[l].astype(F32).reshape(N_EXPERTS, 1))
    a = n_total * TOP_K
    n_blocks = -(-(a + N_EXPERTS * (EXPERT_ROWS - 1)) // EXPERT_ROWS)
    pad_start, fill_start, fill_len, first_block, n_expert_blocks, n_used = _block_plan(counts.reshape(N_EXPERTS))
    dest_t = _dest(eidx_t, rank_t, pad_start.astype(F32).reshape(N_EXPERTS, 1))
    tiles = lambda a: a.reshape(-1, SLABS, LANES)
    x_sorted = _dispatch(fill_start, fill_len, n_used, dest_t, tiles(h_p), tiles(h_s), n_blocks=n_blocks)
    out_sorted = _experts(first_block, n_expert_blocks, n_used, x_sorted, w_gate_e[l], w_up_e[l], w_down_e[l])
    shared = (w_gate_s[l].astype(BF16), w_up_s[l].astype(BF16), w_down_s[l].astype(BF16))
    ln2 = (row_vec(ln2_g[l]), row_vec(ln2_b[l]))
    wts = w_t.T
    y_p = _combine(dest_t, wts, h_p, *shared, *ln2, out_sorted, row_offset=0, name="combine_prompt")
    y_s = _combine(dest_t, wts, h_s, *shared, *ln2, out_sorted, row_offset=n_p, name="combine_sample")
    return (y_p.reshape(bp, sp, D_MODEL), y_s.reshape(bs, ts, D_MODEL), new_kp, new_vp, new_ks, new_vs, new_gs)
```

```python
import functools

import jax
import jax.numpy as jnp
import numpy as np
from jax import lax
from jax.experimental import pallas as pl
from jax.experimental.pallas import tpu as pltpu

D_MODEL = 1024
HEAD_DIM = 64
ATT_HEADS = 8
KV_HEADS = 2
Q_PER_KV = ATT_HEADS // KV_HEADS
GM_HEADS = 8
ATT_WIDTH = ATT_HEADS * HEAD_DIM
KV_WIDTH = KV_HEADS * HEAD_DIM
GM_WIDTH = GM_HEADS * HEAD_DIM
ROPE_WIDTH = ATT_WIDTH + KV_WIDTH
IN_WIDTH = ATT_WIDTH + 2 * KV_WIDTH + 2 * GM_WIDTH
WINDOW = 128
CHUNK = 128
PAST_LEN = 16384
ROPE_THETA = 10000.0
ATT_SCALE = HEAD_DIM ** -0.5
N_EXPERTS = 256
TOP_K = 8
N_GROUPS = 8
TOPK_GROUPS = 4
F_EXPERT = 256
ROUTED_SCALE = 2.5
LN_EPS = 1e-5
DEPTH = 1
ALPHA = (2.0 * DEPTH) ** 0.25

LANES = 128
SLABS = D_MODEL // LANES
ROW_TILE = 512
EXPERT_ROWS = 256
EXPERT_RING = 8
COMBINE_ROWS = 256
DISPATCH_ROWS = 256
ROUTE_TILE = 256
SAMPLE_SEQS_PER_STEP = 16
DMA_UNROLL = 4
DMA_PRIORITIES = 2
VMEM_LIMIT = 56 * 1024 * 1024

F32 = jnp.float32
BF16 = jnp.bfloat16


def _params(n_axes):
    return pltpu.CompilerParams(dimension_semantics=("arbitrary",) * n_axes, vmem_limit_bytes=VMEM_LIMIT)


def _layer_norm(x, g, b):
    mu = jnp.mean(x, axis=-1, keepdims=True)
    xc = x - mu
    var = jnp.mean(xc * xc, axis=-1, keepdims=True)
    return xc * lax.rsqrt(var + LN_EPS) * g + b


def _store_row_tiles(ref2d, x):
    m = x.shape[0]
    for s in range(SLABS):
        ref2d[pl.ds(s, m, stride=SLABS), :] = x[:, s * LANES:(s + 1) * LANES]


def _load_row_tiles(ref2d, m, first_row=0):
    return jnp.concatenate([ref2d[pl.ds(first_row * SLABS + s, m, stride=SLABS), :] for s in range(SLABS)], axis=1)


def _gelu(x):
    return 0.5 * x * (1.0 + lax.erf(x * np.float32(np.sqrt(0.5))))


def _in_proj_kernel(x_ref, w_ref, cos_ref, sa_ref, sb_ref, g_ref, b_ref, q_ref, k_ref, v_ref, gu_ref, gv_ref):
    x = x_ref[...].astype(BF16)
    zr = jnp.dot(x, w_ref[:, :ROPE_WIDTH], preferred_element_type=F32)
    pieces = []
    for c in range(ROPE_WIDTH // LANES):
        zc = zr[:, c * LANES:(c + 1) * LANES]
        tl = slice(0, LANES) if c < ATT_WIDTH // LANES else slice(LANES, 2 * LANES)
        pieces.append(zc * cos_ref[:, tl]
                      + pltpu.roll(zc, LANES - HEAD_DIM // 2, 1) * sa_ref[:, tl]
                      + pltpu.roll(zc, HEAD_DIM // 2, 1) * sb_ref[:, tl])
    for c in range(ATT_WIDTH // LANES):
        q_ref[:, c * LANES:(c + 1) * LANES] = pieces[c].astype(q_ref.dtype)
    k_ref[...] = pieces[ATT_WIDTH // LANES]
    v_ref[...] = jnp.dot(x, w_ref[:, ROPE_WIDTH:ROPE_WIDTH + KV_WIDTH], preferred_element_type=F32)
    g0 = ROPE_WIDTH + KV_WIDTH
    zu = jnp.dot(x, w_ref[:, g0:g0 + GM_WIDTH], preferred_element_type=F32)
    gu_ref[...] = _gelu(zu).astype(gu_ref.dtype)
    zv = jnp.dot(x, w_ref[:, g0 + GM_WIDTH:g0 + 2 * GM_WIDTH], preferred_element_type=F32)
    gv = _layer_norm(_gelu(zv), g_ref[...], b_ref[...])
    gv_ref[...] = gv.astype(gv_ref.dtype)


def _in_proj(x2, w_in_b, tabs, ln_g, ln_b, *, tm, gv_dtype, name):
    n = x2.shape[0]
    cos_t, sa_t, sb_t = tabs
    period = cos_t.shape[0] // tm
    row = lambda i: (i, 0)
    tab = lambda i: (i % period, 0)
    fixed = lambda i: (0, 0)
    return pl.pallas_call(
        _in_proj_kernel,
        grid=(n // tm,),
        in_specs=[
            pl.BlockSpec((tm, D_MODEL), row),
            pl.BlockSpec((D_MODEL, IN_WIDTH), fixed),
            pl.BlockSpec((tm, 2 * LANES), tab),
            pl.BlockSpec((tm, 2 * LANES), tab),
            pl.BlockSpec((tm, 2 * LANES), tab),
            pl.BlockSpec((1, GM_WIDTH), fixed),
            pl.BlockSpec((1, GM_WIDTH), fixed),
        ],
        out_specs=[
            pl.BlockSpec((tm, ATT_WIDTH), row),
            pl.BlockSpec((tm, KV_WIDTH), row),
            pl.BlockSpec((tm, KV_WIDTH), row),
            pl.BlockSpec((tm, GM_WIDTH), row),
            pl.BlockSpec((tm, GM_WIDTH), row),
        ],
        out_shape=[
            jax.ShapeDtypeStruct((n, ATT_WIDTH), BF16),
            jax.ShapeDtypeStruct((n, KV_WIDTH), F32),
            jax.ShapeDtypeStruct((n, KV_WIDTH), F32),
            jax.ShapeDtypeStruct((n, GM_WIDTH), BF16),
            jax.ShapeDtypeStruct((n, GM_WIDTH), gv_dtype),
        ],
        compiler_params=_params(1),
        name=name,
    )(x2, w_in_b, cos_t, sa_t, sb_t, ln_g, ln_b)


def _rope_tables(pos):
    half = HEAD_DIM // 2
    lane = jnp.arange(2 * LANES, dtype=jnp.int32)
    inv = ROPE_THETA ** (-(lane % half).astype(F32) * 2.0 / HEAD_DIM)
    ang = pos.astype(F32)[:, None] * inv[None, :]
    scale = jnp.where(lane < LANES, ATT_SCALE, 1.0).astype(F32)[None, :]
    first_half = ((lane % HEAD_DIM) < half)[None, :]
    cos, sin = jnp.cos(ang) * scale, jnp.sin(ang) * scale
    return cos, jnp.where(first_half, -sin, 0.0), jnp.where(first_half, 0.0, sin)


def _attn_kernel(sink_ref, q_ref, kc_ref, vc_ref, kp_ref, vp_ref, o_ref, *, tq, seqs, stack, first_block_has_no_prev):
    nk = WINDOW + tq
    rows = stack * tq
    qi = lax.broadcasted_iota(jnp.int32, (rows, nk), 0) & (tq - 1)
    ks = lax.broadcasted_iota(jnp.int32, (rows, nk), 1)
    mask = (ks >= qi) & (ks <= qi + WINDOW)
    if first_block_has_no_prev:
        mask = mask & ((pl.program_id(1) > 0) | (ks >= WINDOW))
    sinks = [jnp.concatenate([jnp.full((tq, 1), sink_ref[h0 + j], F32) for j in range(stack)], axis=0)
             for h0 in range(0, ATT_HEADS, stack)]
    for b in range(seqs):
        qrows = slice(b * tq, (b + 1) * tq)
        prows = slice(b * WINDOW, (b + 1) * WINDOW)
        q = q_ref[qrows, :]
        kk = jnp.concatenate([kp_ref[prows, :], kc_ref[qrows, :]], axis=0).astype(BF16)
        vv = jnp.concatenate([vp_ref[prows, :], vc_ref[qrows, :]], axis=0).astype(BF16)
        outs = []
        for i, h0 in enumerate(range(0, ATT_HEADS, stack)):
            g = h0 // Q_PER_KV
            kg = kk[:, g * HEAD_DIM:(g + 1) * HEAD_DIM]
            vg = vv[:, g * HEAD_DIM:(g + 1) * HEAD_DIM]
            qg = jnp.concatenate([q[:, (h0 + j) * HEAD_DIM:(h0 + j + 1) * HEAD_DIM] for j in range(stack)], axis=0)
            s = lax.dot_general(qg, kg, (((1,), (1,)), ((), ())), preferred_element_type=F32)
            s = jnp.where(mask, s, -jnp.inf)
            m = jnp.maximum(jnp.max(s, axis=-1, keepdims=True), sinks[i])
            p = jnp.exp(s - m)
            denom = jnp.sum(p, axis=-1, keepdims=True) + jnp.exp(sinks[i] - m)
            o = jnp.dot((p / denom).astype(BF16), vg, preferred_element_type=F32)
            outs.extend(o[j * tq:(j + 1) * tq, :] for j in range(stack))
        o_ref[qrows, :] = jnp.concatenate(outs, axis=1).astype(o_ref.dtype)


def _attention(sink, q2, k2, v2, kprev2, vprev2, *, batch, nb, tq, seqs, prev_blocks, first_block_has_no_prev, name):
    assert tq & (tq - 1) == 0 and batch % seqs == 0 and (seqs == 1 or nb == prev_blocks == 1)
    cur = lambda b, n, s: (b * nb + n, 0)
    prev = lambda b, n, s: (b * prev_blocks + jnp.maximum(n - 1, 0), 0)
    stack = Q_PER_KV if Q_PER_KV * tq <= WINDOW else 1
    kern = functools.partial(_attn_kernel, tq=tq, seqs=seqs, stack=stack,
                             first_block_has_no_prev=first_block_has_no_prev)
    return pl.pallas_call(
        kern,
        grid_spec=pltpu.PrefetchScalarGridSpec(
            num_scalar_prefetch=1,
            grid=(batch // seqs, nb),
            in_specs=[
                pl.BlockSpec((seqs * tq, ATT_WIDTH), cur),
                pl.BlockSpec((seqs * tq, KV_WIDTH), cur),
                pl.BlockSpec((seqs * tq, KV_WIDTH), cur),
                pl.BlockSpec((seqs * WINDOW, KV_WIDTH), prev),
                pl.BlockSpec((seqs * WINDOW, KV_WIDTH), prev),
            ],
            out_specs=pl.BlockSpec((seqs * tq, ATT_WIDTH), cur),
        ),
        out_shape=jax.ShapeDtypeStruct(q2.shape, BF16),
        compiler_params=_params(2),
        name=name,
    )(sink, q2, k2, v2, kprev2, vprev2)


def _gate_kernel(gu_ref, gv_ref, w_ref, b_ref, o_ref, *, chunk, n_chunks):
    ri = lax.broadcasted_iota(jnp.int32, (chunk, chunk), 0)
    ci = lax.broadcasted_iota(jnp.int32, (chunk, chunk), 1)
    ws = [jnp.where(ci <= ri, w_ref[h], 0.0).astype(BF16) for h in range(GM_HEADS)]
    for c in range(n_chunks):
        rows = slice(c * chunk, (c + 1) * chunk)
        gv = gv_ref[rows, :].astype(BF16)
        sv = jnp.concatenate(
            [jnp.dot(ws[h], gv[:, h * HEAD_DIM:(h + 1) * HEAD_DIM], preferred_element_type=F32)
             for h in range(GM_HEADS)], axis=1)
        o_ref[rows, :] = (gu_ref[rows, :].astype(F32) * (sv + b_ref[...])).astype(o_ref.dtype)


def _gate(gu2, gv2, w_s, b_tab, *, chunk, n_chunks, name):
    n = gu2.shape[0]
    tm = chunk * n_chunks
    row = lambda i: (i, 0)
    kern = functools.partial(_gate_kernel, chunk=chunk, n_chunks=n_chunks)
    return pl.pallas_call(
        kern,
        grid=(n // tm,),
        in_specs=[
            pl.BlockSpec((tm, GM_WIDTH), row),
            pl.BlockSpec((tm, GM_WIDTH), row),
            pl.BlockSpec((GM_HEADS, chunk, chunk), lambda i: (0, 0, 0)),
            pl.BlockSpec((chunk, GM_WIDTH), lambda i: (0, 0)),
        ],
        out_specs=pl.BlockSpec((tm, GM_WIDTH), row),
        out_shape=jax.ShapeDtypeStruct((n, GM_WIDTH), BF16),
        compiler_params=_params(1),
        name=name,
    )(gu2, gv2, w_s, b_tab)


def _out_proj_kernel(att_ref, gm_ref, x_ref, wo_ref, g_ref, b_ref, rh_ref, rl_ref, h_ref, lg_ref):
    mix = jnp.dot(att_ref[...], wo_ref[:ATT_WIDTH, :], preferred_element_type=F32)
    mix = mix + jnp.dot(gm_ref[...], wo_ref[ATT_WIDTH:, :], preferred_element_type=F32)
    h = _layer_norm(ALPHA * x_ref[...] + mix, g_ref[...], b_ref[...])
    _store_row_tiles(h_ref, h)
    h_hi = h.astype(BF16)
    h_lo = (h - h_hi.astype(F32)).astype(BF16)
    nt = (((1,), (1,)), ((), ()))
    lg = lax.dot_general(rh_ref[...], h_hi, nt, preferred_element_type=F32)
    lg = lg + lax.dot_general(rh_ref[...], h_lo, nt, preferred_element_type=F32)
    lg = lg + lax.dot_general(rl_ref[...], h_hi, nt, preferred_element_type=F32)
    lg_ref[...] = lg


def _out_proj(att2, gm2, x2, w_out_b, ln_g, ln_b, r_hi, r_lo, *, name):
    n = x2.shape[0]
    tm = ROW_TILE
    row = lambda i: (i, 0)
    fixed = lambda i: (0, 0)
    return pl.pallas_call(
        _out_proj_kernel,
        grid=(n // tm,),
        in_specs=[
            pl.BlockSpec((tm, ATT_WIDTH), row),
            pl.BlockSpec((tm, GM_WIDTH), row),
            pl.BlockSpec((tm, D_MODEL), row),
            pl.BlockSpec((D_MODEL, D_MODEL), fixed),
            pl.BlockSpec((1, D_MODEL), fixed),
            pl.BlockSpec((1, D_MODEL), fixed),
            pl.BlockSpec((N_EXPERTS, D_MODEL), fixed),
            pl.BlockSpec((N_EXPERTS, D_MODEL), fixed),
        ],
        out_specs=[
            pl.BlockSpec((tm * SLABS, LANES), row),
            pl.BlockSpec((N_EXPERTS, tm), lambda i: (0, i)),
        ],
        out_shape=[
            jax.ShapeDtypeStruct((n * SLABS, LANES), F32),
            jax.ShapeDtypeStruct((N_EXPERTS, n), F32),
        ],
        compiler_params=_params(1),
        name=name,
    )(att2, gm2, x2, w_out_b, ln_g, ln_b, r_hi, r_lo)


def _experts_kernel(first_ref, nblk_ref, nu_ref, x_hbm, wg_ref, wu_ref, wd_ref, o_hbm, xbuf, obuf, wg_s, wu_s, wd_s,
                    in_sem, out_sem, fill_sem, *, n_blocks):
    e = pl.program_id(0)
    nb = nblk_ref[e]
    b0 = first_ref[e]
    n_used = nu_ref[0]
    ahead = EXPERT_RING // 2

    def rows_of(block):
        return pl.ds(pl.multiple_of(block * EXPERT_ROWS, EXPERT_ROWS), EXPERT_ROWS)

    def slot_of(block):
        return block & (EXPERT_RING - 1)

    def slab(s):
        return pl.ds(s * LANES, LANES)

    def in_copies(block):
        slot = slot_of(block)
        return [pltpu.make_async_copy(x_hbm.at[rows_of(block), s, :], xbuf.at[slot, :, slab(s)], in_sem.at[slot])
                for s in range(SLABS)]

    def out_copies(block, slot=None, sem=None):
        slot = slot_of(block) if slot is None else slot
        sem = out_sem.at[slot] if sem is None else sem
        return [pltpu.make_async_copy(obuf.at[slot, :, slab(s)], o_hbm.at[rows_of(block), s, :], sem)
                for s in range(SLABS)]

    def start(copies):
        for cp in copies:
            cp.start()

    def wait(copies):
        for cp in copies:
            cp.wait()

    @pl.when(e == 0)
    def _():
        for j in range(ahead):
            pl.when(j < n_used)(lambda j=j: start(in_copies(j)))

    def process(block, count):
        blocks = [block + j for j in range(count)]
        for blk in blocks:
            wait(in_copies(blk))
        x = [xbuf[slot_of(blk)].astype(BF16) for blk in blocks]
        x = x[0] if count == 1 else jnp.concatenate(x, axis=0)
        for blk in blocks:
            pl.when(blk + ahead < n_used)(lambda blk=blk: start(in_copies(blk + ahead)))
            pl.when(blk >= ahead)(lambda blk=blk: wait(out_copies(blk - ahead)))
        a = jnp.dot(x, wg_s[...], preferred_element_type=F32)
        u = jnp.dot(x, wu_s[...], preferred_element_type=F32)
        hb = (a * jax.nn.sigmoid(a) * u).astype(BF16)
        o = jnp.dot(hb, wd_s[...], preferred_element_type=F32)
        for j, blk in enumerate(blocks):
            obuf[slot_of(blk)] = o[j * EXPERT_ROWS:(j + 1) * EXPERT_ROWS, :]
            start(out_copies(blk))

    @pl.when(nb > 0)
    def _():
        wg_s[...] = wg_ref[0].astype(BF16)
        wu_s[...] = wu_ref[0].astype(BF16)
        wd_s[...] = wd_ref[0].astype(BF16)

        def pair(i, carry):
            process(b0 + 2 * i, 2)
            return carry

        lax.fori_loop(0, nb // 2, pair, 0)
        pl.when(nb % 2 == 1)(lambda: process(b0 + nb - 1, 1))

    @pl.when(e == N_EXPERTS - 1)
    def _():
        for j in range(ahead):
            pl.when(n_used - 1 - j >= 0)(lambda j=j: wait(out_copies(n_used - 1 - j)))
        obuf[0] = jnp.zeros((EXPERT_ROWS, D_MODEL), F32)

        def on_unused_blocks(fn):
            def body(b, c):
                fn(out_copies(b, slot=0, sem=fill_sem))
                return c
            lax.fori_loop(n_used, n_blocks, body, 0)

        on_unused_blocks(start)
        on_unused_blocks(wait)


def _experts(first_block, n_expert_blocks, n_used, x_sorted, w_gate_e, w_up_e, w_down_e):
    rows = x_sorted.shape[0]
    wmap = lambda e, *_: (e, 0, 0)
    kern = functools.partial(_experts_kernel, n_blocks=rows // EXPERT_ROWS)
    return pl.pallas_call(
        kern,
        grid_spec=pltpu.PrefetchScalarGridSpec(
            num_scalar_prefetch=3,
            grid=(N_EXPERTS,),
            in_specs=[
                pl.BlockSpec(memory_space=pl.ANY),
                pl.BlockSpec((1, D_MODEL, F_EXPERT), wmap),
                pl.BlockSpec((1, D_MODEL, F_EXPERT), wmap),
                pl.BlockSpec((1, F_EXPERT, D_MODEL), wmap),
            ],
            out_specs=pl.BlockSpec(memory_space=pl.ANY),
            scratch_shapes=[
                pltpu.VMEM((EXPERT_RING, EXPERT_ROWS, D_MODEL), F32),
                pltpu.VMEM((EXPERT_RING, EXPERT_ROWS, D_MODEL), F32),
                pltpu.VMEM((D_MODEL, F_EXPERT), BF16),
                pltpu.VMEM((D_MODEL, F_EXPERT), BF16),
                pltpu.VMEM((F_EXPERT, D_MODEL), BF16),
                pltpu.SemaphoreType.DMA((EXPERT_RING,)),
                pltpu.SemaphoreType.DMA((EXPERT_RING,)),
                pltpu.SemaphoreType.DMA,
            ],
        ),
        out_shape=jax.ShapeDtypeStruct(x_sorted.shape, F32),
        compiler_params=_params(1),
        name="experts",
    )(first_block, n_expert_blocks, n_used, x_sorted, w_gate_e, w_up_e, w_down_e)


def _combine_kernel(dcur_ref, dnext_ref, w_ref, h_ref, wg_ref, wu_ref, wd_ref, ln_g_ref, ln_b_ref, os_ref, y_ref,
                    gbuf, sem):
    i = pl.program_id(0)
    t = y_ref.shape[0]
    gtiles = gbuf.reshape(2 * TOP_K * t, SLABS, LANES)

    def row_copy(dref, half, j, k):
        return pltpu.make_async_copy(os_ref.at[pl.ds(dref[k, j], 1)],
                                     gtiles.at[pl.ds((half * TOP_K + k) * t + j, 1)], sem.at[half])

    def issue(dref, half):
        def body(j, c):
            for k in range(TOP_K):
                row_copy(dref, half, j, k).start(priority=k % DMA_PRIORITIES)
            return c
        lax.fori_loop(0, t, body, 0, unroll=DMA_UNROLL)

    def drain(dref, half):
        def body(j, c):
            for k in range(TOP_K):
                row_copy(dref, half, j, k).wait()
            return c
        lax.fori_loop(0, t, body, 0, unroll=DMA_UNROLL)

    def step(half):
        pl.when(i == 0)(lambda: issue(dcur_ref, half))
        pl.when(i + 1 < pl.num_programs(0))(lambda: issue(dnext_ref, 1 - half))
        h = _load_row_tiles(h_ref, t)
        hb = h.astype(BF16)
        a = jnp.dot(hb, wg_ref[...], preferred_element_type=F32)
        u = jnp.dot(hb, wu_ref[...], preferred_element_type=F32)
        ffn = jnp.dot((a * jax.nn.sigmoid(a) * u).astype(BF16), wd_ref[...], preferred_element_type=F32)
        drain(dcur_ref, half)
        w = w_ref[...]
        for k in range(TOP_K):
            ffn = ffn + w[:, k:k + 1] * _load_row_tiles(gbuf, t, first_row=(half * TOP_K + k) * t)
        y_ref[...] = _layer_norm(ALPHA * h + ffn, ln_g_ref[...], ln_b_ref[...])

    pl.when(i % 2 == 0)(lambda: step(0))
    pl.when(i % 2 == 1)(lambda: step(1))


def _combine(dest_t, w2, h_tiled, wg_b, wu_b, wd_b, ln_g, ln_b, out_sorted, *, row_offset, name):
    n = h_tiled.shape[0] // SLABS
    tm = COMBINE_ROWS
    off = row_offset // tm
    last = n // tm - 1
    row = lambda i: (i, 0)
    fixed = lambda i: (0, 0)
    return pl.pallas_call(
        _combine_kernel,
        grid=(n // tm,),
        in_specs=[
            pl.BlockSpec((TOP_K, tm), lambda i: (0, i + off), memory_space=pltpu.SMEM),
            pl.BlockSpec((TOP_K, tm), lambda i: (0, jnp.minimum(i + 1, last) + off), memory_space=pltpu.SMEM),
            pl.BlockSpec((tm, TOP_K), lambda i: (i + off, 0)),
            pl.BlockSpec((tm * SLABS, LANES), row),
            pl.BlockSpec((D_MODEL, F_EXPERT), fixed),
            pl.BlockSpec((D_MODEL, F_EXPERT), fixed),
            pl.BlockSpec((F_EXPERT, D_MODEL), fixed),
            pl.BlockSpec((1, D_MODEL), fixed),
            pl.BlockSpec((1, D_MODEL), fixed),
            pl.BlockSpec(memory_space=pl.ANY),
        ],
        out_specs=pl.BlockSpec((tm, D_MODEL), row),
        out_shape=jax.ShapeDtypeStruct((n, D_MODEL), F32),
        scratch_shapes=[pltpu.VMEM((2 * TOP_K * tm * SLABS, LANES), F32), pltpu.SemaphoreType.DMA((2,))],
        compiler_params=_params(1),
        name=name,
    )(dest_t, dest_t, w2, h_tiled, wg_b, wu_b, wd_b, ln_g, ln_b, out_sorted)


def _route_kernel(lgp_ref, lgs_ref, bias_ref, eidx_ref, w_ref, rank_ref, cnt_ref, carry_ref, *, prompt_tiles):
    @pl.when(pl.program_id(0) == 0)
    def _():
        carry_ref[...] = jnp.zeros_like(carry_ref)

    t = lgp_ref.shape[1]
    gsz = N_EXPERTS // N_GROUPS
    neg = -jnp.inf
    s = jax.nn.sigmoid(jnp.where(pl.program_id(0) < prompt_tiles, lgp_ref[...], lgs_ref[...]))
    biased = s + bias_ref[...]
    io_g = lax.broadcasted_iota(jnp.int32, (gsz, t), 0)
    grp_rows = []
    for g in range(N_GROUPS):
        blk = biased[g * gsz:(g + 1) * gsz, :]
        m1 = jnp.max(blk, axis=0, keepdims=True)
        i1 = jnp.min(jnp.where(blk == m1, io_g, gsz), axis=0, keepdims=True)
        m2 = jnp.max(jnp.where(io_g == i1, neg, blk), axis=0, keepdims=True)
        grp_rows.append(m1 + m2)
    gs = jnp.concatenate(grp_rows, axis=0)
    io8 = lax.broadcasted_iota(jnp.int32, (N_GROUPS, t), 0)
    gsel = jnp.zeros((N_GROUPS, t), jnp.int32)
    for _ in range(TOPK_GROUPS):
        m = jnp.max(gs, axis=0, keepdims=True)
        gi = jnp.min(jnp.where(gs == m, io8, N_GROUPS), axis=0, keepdims=True)
        hit = io8 == gi
        gsel = jnp.where(hit, 1, gsel)
        gs = jnp.where(hit, neg, gs)
    masked = jnp.concatenate(
        [jnp.where(gsel[g:g + 1, :] > 0, biased[g * gsz:(g + 1) * gsz, :], neg) for g in range(N_GROUPS)], axis=0)

    eio = lax.broadcasted_iota(jnp.int32, (N_EXPERTS, t), 0)
    cur = masked
    idx_rows, w_rows = [], []
    for _ in range(TOP_K):
        m = jnp.max(cur, axis=0, keepdims=True)
        idx = jnp.min(jnp.where(cur == m, eio, N_EXPERTS), axis=0, keepdims=True)
        hit = eio == idx
        w_rows.append(jnp.sum(jnp.where(hit, s, 0.0), axis=0, keepdims=True))
        cur = jnp.where(hit, neg, cur)
        idx_rows.append(idx)
    sel = jnp.where(cur != masked, 1.0, 0.0)

    tri = jnp.where(lax.broadcasted_iota(jnp.int32, (t, t), 0) < lax.broadcasted_iota(jnp.int32, (t, t), 1), 1.0, 0.0)
    pref = jnp.dot(sel.astype(BF16), tri.astype(BF16), preferred_element_type=F32) + carry_ref[...]
    rank_rows = [jnp.sum(jnp.where(eio == idx_rows[k], pref, 0.0), axis=0, keepdims=True) for k in range(TOP_K)]
    carry_ref[...] += jnp.sum(sel, axis=1, keepdims=True)

    wk = jnp.concatenate(w_rows, axis=0)
    eidx_ref[...] = jnp.concatenate(idx_rows, axis=0)
    w_ref[...] = wk / jnp.sum(wk, axis=0, keepdims=True) * ROUTED_SCALE
    rank_ref[...] = jnp.concatenate(rank_rows, axis=0).astype(jnp.int32)
    cnt_ref[...] = carry_ref[...].astype(jnp.int32)


def _route(logits_p, logits_s, bias_col):
    t = ROUTE_TILE
    prompt_tiles = logits_p.shape[1] // t
    n = logits_p.shape[1] + logits_s.shape[1]
    col = lambda i: (0, i)
    fixed = lambda i: (0, 0)
    kern = functools.partial(_route_kernel, prompt_tiles=prompt_tiles)
    return pl.pallas_call(
        kern,
        grid=(n // t,),
        in_specs=[pl.BlockSpec((N_EXPERTS, t), lambda i: (0, jnp.minimum(i, prompt_tiles - 1))),
                  pl.BlockSpec((N_EXPERTS, t), lambda i: (0, jnp.maximum(i - prompt_tiles, 0))),
                  pl.BlockSpec((N_EXPERTS, 1), fixed)],
        out_specs=[
            pl.BlockSpec((TOP_K, t), col),
            pl.BlockSpec((TOP_K, t), col),
            pl.BlockSpec((TOP_K, t), col),
            pl.BlockSpec((N_EXPERTS, 1), fixed),
        ],
        out_shape=[
            jax.ShapeDtypeStruct((TOP_K, n), jnp.int32),
            jax.ShapeDtypeStruct((TOP_K, n), F32),
            jax.ShapeDtypeStruct((TOP_K, n), jnp.int32),
            jax.ShapeDtypeStruct((N_EXPERTS, 1), jnp.int32),
        ],
        scratch_shapes=[pltpu.VMEM((N_EXPERTS, 1), F32)],
        compiler_params=_params(1),
        name="route",
    )(logits_p, logits_s, bias_col)


def _dest_kernel(eidx_ref, rank_ref, start_ref, dest_ref):
    t = eidx_ref.shape[1]
    eio = lax.broadcasted_iota(jnp.int32, (N_EXPERTS, t), 0)
    start = start_ref[...]
    rows = [jnp.sum(jnp.where(eio == eidx_ref[k:k + 1, :], start, 0.0), axis=0, keepdims=True) for k in range(TOP_K)]
    dest_ref[...] = jnp.concatenate(rows, axis=0).astype(jnp.int32) + rank_ref[...]


def _dest(eidx_t, rank_t, pad_start_col):
    n = eidx_t.shape[1]
    t = ROW_TILE
    col = lambda i: (0, i)
    return pl.pallas_call(
        _dest_kernel,
        grid=(n // t,),
        in_specs=[pl.BlockSpec((TOP_K, t), col), pl.BlockSpec((TOP_K, t), col),
                  pl.BlockSpec((N_EXPERTS, 1), lambda i: (0, 0))],
        out_specs=pl.BlockSpec((TOP_K, t), col),
        out_shape=jax.ShapeDtypeStruct((TOP_K, n), jnp.int32),
        compiler_params=_params(1),
        name="dest",
    )(eidx_t, rank_t, pad_start_col)


def _dispatch_kernel(fill_ref, len_ref, nu_ref, dest_ref, hp_ref, hs_ref, xs_ref, zbuf, sem, fill_sem, *,
                     prompt_tiles, n_blocks):
    i = pl.program_id(0)

    @pl.when(i == 0)
    def _():
        zbuf[...] = jnp.zeros_like(zbuf)

        def fill_copy(row0, size):
            return pltpu.make_async_copy(zbuf.at[pl.ds(0, size)], xs_ref.at[pl.ds(row0, size)], fill_sem)

        def on_padding(fn):
            def body(e, c):
                base, length = fill_ref[e], len_ref[e]
                size = EXPERT_ROWS // 2
                while size >= 1:
                    piece = fill_copy(base + (length & ~(2 * size - 1)), size)
                    pl.when((length & size) != 0)(functools.partial(fn, piece))
                    size //= 2
                return c
            lax.fori_loop(0, N_EXPERTS, body, 0)

        def on_unused_blocks(fn):
            lax.fori_loop(nu_ref[0], n_blocks, lambda b, c: (fn(fill_copy(b * EXPERT_ROWS, EXPERT_ROWS)), c)[1], 0)

        on_padding(lambda cp: cp.start())
        on_unused_blocks(lambda cp: cp.start())
        on_padding(lambda cp: cp.wait())
        on_unused_blocks(lambda cp: cp.wait())

    def scatter_rows(src_ref):
        t = src_ref.shape[0]

        def row_copy(j, k):
            return pltpu.make_async_copy(src_ref.at[pl.ds(j, 1)], xs_ref.at[pl.ds(dest_ref[k, j], 1)], sem)

        def issue(j, c):
            for k in range(TOP_K):
                row_copy(j, k).start(priority=k % DMA_PRIORITIES)
            return c

        def drain(j, c):
            for k in range(TOP_K):
                row_copy(j, k).wait()
            return c

        lax.fori_loop(0, t, issue, 0, unroll=DMA_UNROLL)
        lax.fori_loop(0, t, drain, 0, unroll=DMA_UNROLL)

    @pl.when(i < prompt_tiles)
    def _():
        scatter_rows(hp_ref)

    @pl.when(i >= prompt_tiles)
    def _():
        scatter_rows(hs_ref)


def _dispatch(fill_start, fill_len, n_used, dest_t, hp_p, hp_s, *, n_blocks):
    t = DISPATCH_ROWS
    prompt_tiles = hp_p.shape[0] // t
    sample_tiles = hp_s.shape[0] // t
    tile = (t, SLABS, LANES)
    kern = functools.partial(_dispatch_kernel, prompt_tiles=prompt_tiles, n_blocks=n_blocks)
    return pl.pallas_call(
        kern,
        grid_spec=pltpu.PrefetchScalarGridSpec(
            num_scalar_prefetch=3,
            grid=(prompt_tiles + sample_tiles,),
            in_specs=[
                pl.BlockSpec((TOP_K, t), lambda i, *_: (0, i), memory_space=pltpu.SMEM),
                pl.BlockSpec(tile, lambda i, *_: (jnp.minimum(i, prompt_tiles - 1), 0, 0)),
                pl.BlockSpec(tile, lambda i, *_: (jnp.maximum(i - prompt_tiles, 0), 0, 0)),
            ],
            out_specs=pl.BlockSpec(memory_space=pl.ANY),
            scratch_shapes=[pltpu.VMEM((EXPERT_ROWS, SLABS, LANES), F32), pltpu.SemaphoreType.DMA,
                            pltpu.SemaphoreType.DMA],
        ),
        out_shape=jax.ShapeDtypeStruct((n_blocks * EXPERT_ROWS, SLABS, LANES), F32),
        compiler_params=_params(1),
        name="dispatch",
    )(fill_start, fill_len, n_used, dest_t, hp_p, hp_s)


def _block_plan(counts):
    padded = (counts + EXPERT_ROWS - 1) // EXPERT_ROWS * EXPERT_ROWS
    pad_end = jnp.cumsum(padded).astype(jnp.int32)
    pad_start = pad_end - padded
    n_used = pad_end[-1] // EXPERT_ROWS
    fill_start = pad_start + counts
    fill_len = pad_end - fill_start
    first_block = pad_start // EXPERT_ROWS
    n_expert_blocks = (padded // EXPERT_ROWS).astype(jnp.int32)
    return pad_start, fill_start, fill_len, first_block, n_expert_blocks, n_used.reshape(1).astype(jnp.int32)


def kernel(x_prompt, x_sample, cache_k, cache_v, w_in, sink, gm_ln_g, gm_ln_b, gm_w_s, gm_b_s, w_out, ln1_g, ln1_b,
           router_w, router_bias, w_gate_e, w_up_e, w_down_e, w_gate_s, w_up_s, w_down_s, ln2_g, ln2_b):
    bp, sp = x_prompt.shape[:2]
    bs, ts = x_sample.shape[:2]
    r = cache_k.shape[2]
    assert r == WINDOW and sp % ROW_TILE == 0 and (bs * ts) % ROW_TILE == 0
    n_p, n_s = bp * sp, bs * ts
    n_total = n_p + n_s
    l = 0

    w_in_b = w_in[l].astype(BF16)
    w_out_b = w_out[l].astype(BF16)
    router_t = router_w[l].T
    r_hi = router_t.astype(BF16)
    r_lo = (router_t - r_hi.astype(F32)).astype(BF16)
    row_vec = lambda v: v.reshape(1, -1)
    gm_g, gm_b = row_vec(gm_ln_g[l]), row_vec(gm_ln_b[l])
    sink_l = sink[l].astype(F32)

    xp2 = x_prompt.reshape(n_p, D_MODEL)
    tabs_p = _rope_tables(jnp.arange(sp, dtype=jnp.int32))
    q, k, v, gu, gv = _in_proj(xp2, w_in_b, tabs_p, gm_g, gm_b, tm=ROW_TILE, gv_dtype=BF16, name="in_proj_prompt")
    nb = sp // WINDOW
    att = _attention(sink_l, q, k, v, k, v, batch=bp, nb=nb, tq=WINDOW, seqs=1, prev_blocks=nb,
                     first_block_has_no_prev=True, name="attn_prompt")
    b_tab_p = jnp.repeat(gm_b_s[l].T, HEAD_DIM, axis=1)
    gm = _gate(gu, gv, gm_w_s[l], b_tab_p, chunk=CHUNK, n_chunks=ROW_TILE // CHUNK, name="gate_prompt")
    last_rows = lambda t: t.reshape(bp, sp, KV_WIDTH)[:, sp - r:, :].reshape(1, bp, r, KV_HEADS, HEAD_DIM)
    new_kp, new_vp = last_rows(k), last_rows(v)
    h_p, lt_p = _out_proj(att, gm, xp2, w_out_b, row_vec(ln1_g[l]), row_vec(ln1_b[l]), r_hi, r_lo,
                          name="out_proj_prompt")

    xs2 = x_sample.reshape(n_s, D_MODEL)
    pos_s = PAST_LEN + jnp.arange(ts, dtype=jnp.int32)
    tabs_s = tuple(jnp.tile(t, (bs, 1)) for t in _rope_tables(pos_s))
    q, k, v, gu, gv = _in_proj(xs2, w_in_b, tabs_s, gm_g, gm_b, tm=n_s, gv_dtype=F32, name="in_proj_sample")
    tq = 8
    pad_rows = lambda t: jnp.pad(t.reshape(bs, ts, -1), ((0, 0), (0, tq - ts), (0, 0))).reshape(bs * tq, -1)
    ck2 = cache_k[l].reshape(bs * r, KV_WIDTH)
    cv2 = cache_v[l].reshape(bs * r, KV_WIDTH)
    att = _attention(sink_l, pad_rows(q), pad_rows(k), pad_rows(v), ck2, cv2, batch=bs, nb=1, tq=tq,
                     seqs=SAMPLE_SEQS_PER_STEP, prev_blocks=1, first_block_has_no_prev=False, name="attn_sample")
    att = att.reshape(bs, tq, ATT_WIDTH)[:, :ts].reshape(n_s, ATT_WIDTH)
    w_small = gm_w_s[l][:, :ts, :ts]
    w_big = jnp.einsum("ab,hij->haibj", jnp.eye(bs, dtype=F32), w_small).reshape(GM_HEADS, n_s, n_s)
    b_tab_s = jnp.tile(jnp.repeat(gm_b_s[l][:, :ts].T, HEAD_DIM, axis=1), (bs, 1))
    gm = _gate(gu, gv, w_big, b_tab_s, chunk=n_s, n_chunks=1, name="gate_sample")
    new_ks = jnp.concatenate([cache_k[l], k.reshape(bs, ts, KV_HEADS, HEAD_DIM)], axis=1)[:, ts:][None]
    new_vs = jnp.concatenate([cache_v[l], v.reshape(bs, ts, KV_HEADS, HEAD_DIM)], axis=1)[:, ts:][None]
    new_gs = gv.reshape(bs, ts, GM_WIDTH)[None]
    h_s, lt_s = _out_proj(att, gm, xs2, w_out_b, row_vec(ln1_g[l]), row_vec(ln1_b[l]), r_hi, r_lo,
                          name="out_proj_sample")

    eidx_t, w_t, rank_t, counts = _route(lt_p, lt_s, router_bias[l].astype(F32).reshape(N_EXPERTS, 1))
    a = n_total * TOP_K
    n_blocks = -(-(a + N_EXPERTS * (EXPERT_ROWS - 1)) // EXPERT_ROWS)
    pad_start, fill_start, fill_len, first_block, n_expert_blocks, n_used = _block_plan(counts.reshape(N_EXPERTS))
    dest_t = _dest(eidx_t, rank_t, pad_start.astype(F32).reshape(N_EXPERTS, 1))
    tiles = lambda a: a.reshape(-1, SLABS, LANES)
    x_sorted = _dispatch(fill_start, fill_len, n_used, dest_t, tiles(h_p), tiles(h_s), n_blocks=n_blocks)
    out_sorted = _experts(first_block, n_expert_blocks, n_used, x_sorted, w_gate_e[l], w_up_e[l], w_down_e[l])
    shared = (w_gate_s[l].astype(BF16), w_up_s[l].astype(BF16), w_down_s[l].astype(BF16))
    ln2 = (row_vec(ln2_g[l]), row_vec(ln2_b[l]))
    wts = w_t.T
    y_p = _combine(dest_t, wts, h_p, *shared, *ln2, out_sorted, row_offset=0, name="combine_prompt")
    y_s = _combine(dest_t, wts, h_s, *shared, *ln2, out_sorted, row_offset=n_p, name="combine_sample")
    return (y_p.reshape(bp, sp, D_MODEL), y_s.reshape(bs, ts, D_MODEL), new_kp, new_vp, new_ks, new_vs, new_gs)
```

```python
import functools

import jax
import jax.numpy as jnp
import numpy as np
from jax import lax
from jax.experimental import pallas as pl
from jax.experimental.pallas import tpu as pltpu

D_MODEL = 1024
HEAD_DIM = 64
ATT_HEADS = 8
KV_HEADS = 2
Q_PER_KV = ATT_HEADS // KV_HEADS
GM_HEADS = 8
ATT_WIDTH = ATT_HEADS * HEAD_DIM
KV_WIDTH = KV_HEADS * HEAD_DIM
GM_WIDTH = GM_HEADS * HEAD_DIM
ROPE_WIDTH = ATT_WIDTH + KV_WIDTH
IN_WIDTH = ATT_WIDTH + 2 * KV_WIDTH + 2 * GM_WIDTH
WINDOW = 128
CHUNK = 128
PAST_LEN = 16384
ROPE_THETA = 10000.0
ATT_SCALE = HEAD_DIM ** -0.5
N_EXPERTS = 256
TOP_K = 8
N_GROUPS = 8
TOPK_GROUPS = 4
F_EXPERT = 256
ROUTED_SCALE = 2.5
LN_EPS = 1e-5
DEPTH = 1
ALPHA = (2.0 * DEPTH) ** 0.25

LANES = 128
SLABS = D_MODEL // LANES
ROW_TILE = 512
EXPERT_ROWS = 256
EXPERT_RING = 8
COMBINE_ROWS = 256
DISPATCH_ROWS = 256
ROUTE_TILE = 256
SAMPLE_SEQS_PER_STEP = 16
DMA_UNROLL = 4
DMA_PRIORITIES = 2
VMEM_LIMIT = 56 * 1024 * 1024

F32 = jnp.float32
BF16 = jnp.bfloat16


def _params(n_axes):
    return pltpu.CompilerParams(dimension_semantics=("arbitrary",) * n_axes, vmem_limit_bytes=VMEM_LIMIT)


def _layer_norm(x, g, b):
    mu = jnp.mean(x, axis=-1, keepdims=True)
    xc = x - mu
    var = jnp.mean(xc * xc, axis=-1, keepdims=True)
    return xc * lax.rsqrt(var + LN_EPS) * g + b


def _store_row_tiles(ref2d, x):
    m = x.shape[0]
    for s in range(SLABS):
        ref2d[pl.ds(s, m, stride=SLABS), :] = x[:, s * LANES:(s + 1) * LANES]


def _load_row_tiles(ref2d, m, first_row=0):
    return jnp.concatenate([ref2d[pl.ds(first_row * SLABS + s, m, stride=SLABS), :] for s in range(SLABS)], axis=1)


def _gelu(x):
    return 0.5 * x * (1.0 + lax.erf(x * np.float32(np.sqrt(0.5))))


def _in_proj_kernel(x_ref, w_ref, cos_ref, sa_ref, sb_ref, g_ref, b_ref, q_ref, k_ref, v_ref, gu_ref, gv_ref):
    x = x_ref[...].astype(BF16)
    zr = jnp.dot(x, w_ref[:, :ROPE_WIDTH], preferred_element_type=F32)
    pieces = []
    for c in range(ROPE_WIDTH // LANES):
        zc = zr[:, c * LANES:(c + 1) * LANES]
        tl = slice(0, LANES) if c < ATT_WIDTH // LANES else slice(LANES, 2 * LANES)
        pieces.append(zc * cos_ref[:, tl]
                      + pltpu.roll(zc, LANES - HEAD_DIM // 2, 1) * sa_ref[:, tl]
                      + pltpu.roll(zc, HEAD_DIM // 2, 1) * sb_ref[:, tl])
    for c in range(ATT_WIDTH // LANES):
        q_ref[:, c * LANES:(c + 1) * LANES] = pieces[c].astype(q_ref.dtype)
    k_ref[...] = pieces[ATT_WIDTH // LANES]
    v_ref[...] = jnp.dot(x, w_ref[:, ROPE_WIDTH:ROPE_WIDTH + KV_WIDTH], preferred_element_type=F32)
    g0 = ROPE_WIDTH + KV_WIDTH
    zu = jnp.dot(x, w_ref[:, g0:g0 + GM_WIDTH], preferred_element_type=F32)
    gu_ref[...] = _gelu(zu).astype(gu_ref.dtype)
    zv = jnp.dot(x, w_ref[:, g0 + GM_WIDTH:g0 + 2 * GM_WIDTH], preferred_element_type=F32)
    gv = _layer_norm(_gelu(zv), g_ref[...], b_ref[...])
    gv_ref[...] = gv.astype(gv_ref.dtype)


def _in_proj(x2, w_in_b, tabs, ln_g, ln_b, *, tm, gv_dtype, name):
    n = x2.shape[0]
    cos_t, sa_t, sb_t = tabs
    period = cos_t.shape[0] // tm
    row = lambda i: (i, 0)
    tab = lambda i: (i % period, 0)
    fixed = lambda i: (0, 0)
    return pl.pallas_call(
        _in_proj_kernel,
        grid=(n // tm,),
        in_specs=[
            pl.BlockSpec((tm, D_MODEL), row),
            pl.BlockSpec((D_MODEL, IN_WIDTH), fixed),
            pl.BlockSpec((tm, 2 * LANES), tab),
            pl.BlockSpec((tm, 2 * LANES), tab),
            pl.BlockSpec((tm, 2 * LANES), tab),
            pl.BlockSpec((1, GM_WIDTH), fixed),
            pl.BlockSpec((1, GM_WIDTH), fixed),
        ],
        out_specs=[
            pl.BlockSpec((tm, ATT_WIDTH), row),
            pl.BlockSpec((tm, KV_WIDTH), row),
            pl.BlockSpec((tm, KV_WIDTH), row),
            pl.BlockSpec((tm, GM_WIDTH), row),
            pl.BlockSpec((tm, GM_WIDTH), row),
        ],
        out_shape=[
            jax.ShapeDtypeStruct((n, ATT_WIDTH), BF16),
            jax.ShapeDtypeStruct((n, KV_WIDTH), F32),
            jax.ShapeDtypeStruct((n, KV_WIDTH), F32),
            jax.ShapeDtypeStruct((n, GM_WIDTH), BF16),
            jax.ShapeDtypeStruct((n, GM_WIDTH), gv_dtype),
        ],
        compiler_params=_params(1),
        name=name,
    )(x2, w_in_b, cos_t, sa_t, sb_t, ln_g, ln_b)


def _rope_tables(pos):
    half = HEAD_DIM // 2
    lane = jnp.arange(2 * LANES, dtype=jnp.int32)
    inv = ROPE_THETA ** (-(lane % half).astype(F32) * 2.0 / HEAD_DIM)
    ang = pos.astype(F32)[:, None] * inv[None, :]
    scale = jnp.where(lane < LANES, ATT_SCALE, 1.0).astype(F32)[None, :]
    first_half = ((lane % HEAD_DIM) < half)[None, :]
    cos, sin = jnp.cos(ang) * scale, jnp.sin(ang) * scale
    return cos, jnp.where(first_half, -sin, 0.0), jnp.where(first_half, 0.0, sin)


def _attn_kernel(sink_ref, q_ref, kc_ref, vc_ref, kp_ref, vp_ref, o_ref, *, tq, seqs, stack, first_block_has_no_prev):
    nk = WINDOW + tq
    rows = stack * tq
    qi = lax.broadcasted_iota(jnp.int32, (rows, nk), 0) & (tq - 1)
    ks = lax.broadcasted_iota(jnp.int32, (rows, nk), 1)
    mask = (ks >= qi) & (ks <= qi + WINDOW)
    if first_block_has_no_prev:
        mask = mask & ((pl.program_id(1) > 0) | (ks >= WINDOW))
    sinks = [jnp.concatenate([jnp.full((tq, 1), sink_ref[h0 + j], F32) for j in range(stack)], axis=0)
             for h0 in range(0, ATT_HEADS, stack)]
    for b in range(seqs):
        qrows = slice(b * tq, (b + 1) * tq)
        prows = slice(b * WINDOW, (b + 1) * WINDOW)
        q = q_ref[qrows, :]
        kk = jnp.concatenate([kp_ref[prows, :], kc_ref[qrows, :]], axis=0).astype(BF16)
        vv = jnp.concatenate([vp_ref[prows, :], vc_ref[qrows, :]], axis=0).astype(BF16)
        outs = []
        for i, h0 in enumerate(range(0, ATT_HEADS, stack)):
            g = h0 // Q_PER_KV
            kg = kk[:, g * HEAD_DIM:(g + 1) * HEAD_DIM]
            vg = vv[:, g * HEAD_DIM:(g + 1) * HEAD_DIM]
            qg = jnp.concatenate([q[:, (h0 + j) * HEAD_DIM:(h0 + j + 1) * HEAD_DIM] for j in range(stack)], axis=0)
            s = lax.dot_general(qg, kg, (((1,), (1,)), ((), ())), preferred_element_type=F32)
            s = jnp.where(mask, s, -jnp.inf)
            m = jnp.maximum(jnp.max(s, axis=-1, keepdims=True), sinks[i])
            p = jnp.exp(s - m)
            denom = jnp.sum(p, axis=-1, keepdims=True) + jnp.exp(sinks[i] - m)
            o = jnp.dot((p / denom).astype(BF16), vg, preferred_element_type=F32)
            outs.extend(o[j * tq:(j + 1) * tq, :] for j in range(stack))
        o_ref[qrows, :] = jnp.concatenate(outs, axis=1).astype(o_ref.dtype)


def _attention(sink, q2, k2, v2, kprev2, vprev2, *, batch, nb, tq, seqs, prev_blocks, first_block_has_no_prev, name):
    assert tq & (tq - 1) == 0 and batch % seqs == 0 and (seqs == 1 or nb == prev_blocks == 1)
    cur = lambda b, n, s: (b * nb + n, 0)
    prev = lambda b, n, s: (b * prev_blocks + jnp.maximum(n - 1, 0), 0)
    stack = Q_PER_KV if Q_PER_KV * tq <= WINDOW else 1
    kern = functools.partial(_attn_kernel, tq=tq, seqs=seqs, stack=stack,
                             first_block_has_no_prev=first_block_has_no_prev)
    return pl.pallas_call(
        kern,
        grid_spec=pltpu.PrefetchScalarGridSpec(
            num_scalar_prefetch=1,
            grid=(batch // seqs, nb),
            in_specs=[
                pl.BlockSpec((seqs * tq, ATT_WIDTH), cur),
                pl.BlockSpec((seqs * tq, KV_WIDTH), cur),
                pl.BlockSpec((seqs * tq, KV_WIDTH), cur),
                pl.BlockSpec((seqs * WINDOW, KV_WIDTH), prev),
                pl.BlockSpec((seqs * WINDOW, KV_WIDTH), prev),
            ],
            out_specs=pl.BlockSpec((seqs * tq, ATT_WIDTH), cur),
        ),
        out_shape=jax.ShapeDtypeStruct(q2.shape, BF16),
        compiler_params=_params(2),
        name=name,
    )(sink, q2, k2, v2, kprev2, vprev2)


def _gate_kernel(gu_ref, gv_ref, w_ref, b_ref, o_ref, *, chunk, n_chunks):
    ri = lax.broadcasted_iota(jnp.int32, (chunk, chunk), 0)
    ci = lax.broadcasted_iota(jnp.int32, (chunk, chunk), 1)
    ws = [jnp.where(ci <= ri, w_ref[h], 0.0).astype(BF16) for h in range(GM_HEADS)]
    for c in range(n_chunks):
        rows = slice(c * chunk, (c + 1) * chunk)
        gv = gv_ref[rows, :].astype(BF16)
        sv = jnp.concatenate(
            [jnp.dot(ws[h], gv[:, h * HEAD_DIM:(h + 1) * HEAD_DIM], preferred_element_type=F32)
             for h in range(GM_HEADS)], axis=1)
        o_ref[rows, :] = (gu_ref[rows, :].astype(F32) * (sv + b_ref[...])).astype(o_ref.dtype)


def _gate(gu2, gv2, w_s, b_tab, *, chunk, n_chunks, name):
    n = gu2.shape[0]
    tm = chunk * n_chunks
    row = lambda i: (i, 0)
    kern = functools.partial(_gate_kernel, chunk=chunk, n_chunks=n_chunks)
    return pl.pallas_call(
        kern,
        grid=(n // tm,),
        in_specs=[
            pl.BlockSpec((tm, GM_WIDTH), row),
            pl.BlockSpec((tm, GM_WIDTH), row),
            pl.BlockSpec((GM_HEADS, chunk, chunk), lambda i: (0, 0, 0)),
            pl.BlockSpec((chunk, GM_WIDTH), lambda i: (0, 0)),
        ],
        out_specs=pl.BlockSpec((tm, GM_WIDTH), row),
        out_shape=jax.ShapeDtypeStruct((n, GM_WIDTH), BF16),
        compiler_params=_params(1),
        name=name,
    )(gu2, gv2, w_s, b_tab)


def _gate_short_kernel(gu_ref, gv_ref, coef_ref, b_ref, o_ref, *, length):
    n = gv_ref.shape[0]
    gv = gv_ref[...].astype(F32)
    tiles = lambda x: x.reshape(n // SLABS, SLABS, GM_WIDTH)
    sv = tiles(gv) * coef_ref[0][None] + b_ref[...][None]
    for d in range(1, length):
        sv = sv + tiles(pltpu.roll(gv, d, 0)) * coef_ref[d][None]
    o_ref[...] = (gu_ref[...].astype(F32) * sv.reshape(n, GM_WIDTH)).astype(o_ref.dtype)


def _gate_short(gu2, gv2, w_s, b_s, *, length, name):
    n = gu2.shape[0]
    assert SLABS % length == 0 and n % SLABS == 0
    pos = jnp.arange(length)
    lag = jnp.arange(length)
    src = pos[None, :] - lag[:, None]
    coef = jnp.where(src >= 0, w_s[:, pos[None, :], jnp.maximum(src, 0)], 0.0)
    coef = jnp.repeat(coef.transpose(1, 2, 0), HEAD_DIM, axis=2)
    coef = jnp.tile(coef, (1, SLABS // length, 1))
    bias = jnp.tile(jnp.repeat(b_s[:, :length].T, HEAD_DIM, axis=1), (SLABS // length, 1))
    whole = lambda shape: pl.BlockSpec(shape, lambda i: (0,) * len(shape))
    return pl.pallas_call(
        functools.partial(_gate_short_kernel, length=length),
        grid=(1,),
        in_specs=[whole((n, GM_WIDTH)), whole((n, GM_WIDTH)), whole((length, SLABS, GM_WIDTH)),
                  whole((SLABS, GM_WIDTH))],
        out_specs=whole((n, GM_WIDTH)),
        out_shape=jax.ShapeDtypeStruct((n, GM_WIDTH), BF16),
        compiler_params=_params(1),
        name=name,
    )(gu2, gv2, coef, bias)


def _out_proj_kernel(att_ref, gm_ref, x_ref, wo_ref, g_ref, b_ref, rh_ref, rl_ref, h_ref, lg_ref):
    mix = jnp.dot(att_ref[...], wo_ref[:ATT_WIDTH, :], preferred_element_type=F32)
    mix = mix + jnp.dot(gm_ref[...], wo_ref[ATT_WIDTH:, :], preferred_element_type=F32)
    h = _layer_norm(ALPHA * x_ref[...] + mix, g_ref[...], b_ref[...])
    _store_row_tiles(h_ref, h)
    h_hi = h.astype(BF16)
    h_lo = (h - h_hi.astype(F32)).astype(BF16)
    nt = (((1,), (1,)), ((), ()))
    lg = lax.dot_general(rh_ref[...], h_hi, nt, preferred_element_type=F32)
    lg = lg + lax.dot_general(rh_ref[...], h_lo, nt, preferred_element_type=F32)
    lg = lg + lax.dot_general(rl_ref[...], h_hi, nt, preferred_element_type=F32)
    lg_ref[...] = lg


def _out_proj(att2, gm2, x2, w_out_b, ln_g, ln_b, r_hi, r_lo, *, name):
    n = x2.shape[0]
    tm = ROW_TILE
    row = lambda i: (i, 0)
    fixed = lambda i: (0, 0)
    return pl.pallas_call(
        _out_proj_kernel,
        grid=(n // tm,),
        in_specs=[
            pl.BlockSpec((tm, ATT_WIDTH), row),
            pl.BlockSpec((tm, GM_WIDTH), row),
            pl.BlockSpec((tm, D_MODEL), row),
            pl.BlockSpec((D_MODEL, D_MODEL), fixed),
            pl.BlockSpec((1, D_MODEL), fixed),
            pl.BlockSpec((1, D_MODEL), fixed),
            pl.BlockSpec((N_EXPERTS, D_MODEL), fixed),
            pl.BlockSpec((N_EXPERTS, D_MODEL), fixed),
        ],
        out_specs=[
            pl.BlockSpec((tm * SLABS, LANES), row),
            pl.BlockSpec((N_EXPERTS, tm), lambda i: (0, i)),
        ],
        out_shape=[
            jax.ShapeDtypeStruct((n * SLABS, LANES), F32),
            jax.ShapeDtypeStruct((N_EXPERTS, n), F32),
        ],
        compiler_params=_params(1),
        name=name,
    )(att2, gm2, x2, w_out_b, ln_g, ln_b, r_hi, r_lo)


def _experts_kernel(first_ref, nblk_ref, nu_ref, x_hbm, wg_ref, wu_ref, wd_ref, o_hbm, xbuf, obuf, wg_s, wu_s, wd_s,
                    in_sem, out_sem, fill_sem, *, n_blocks):
    e = pl.program_id(0)
    nb = nblk_ref[e]
    b0 = first_ref[e]
    n_used = nu_ref[0]
    ahead = EXPERT_RING // 2

    def rows_of(block):
        return pl.ds(pl.multiple_of(block * EXPERT_ROWS, EXPERT_ROWS), EXPERT_ROWS)

    def slot_of(block):
        return block & (EXPERT_RING - 1)

    def slab(s):
        return pl.ds(s * LANES, LANES)

    def in_copies(block):
        slot = slot_of(block)
        return [pltpu.make_async_copy(x_hbm.at[rows_of(block), s, :], xbuf.at[slot, :, slab(s)], in_sem.at[slot])
                for s in range(SLABS)]

    def out_copies(block, slot=None, sem=None):
        slot = slot_of(block) if slot is None else slot
        sem = out_sem.at[slot] if sem is None else sem
        return [pltpu.make_async_copy(obuf.at[slot, :, slab(s)], o_hbm.at[rows_of(block), s, :], sem)
                for s in range(SLABS)]

    def start(copies):
        for cp in copies:
            cp.start()

    def wait(copies):
        for cp in copies:
            cp.wait()

    @pl.when(e == 0)
    def _():
        for j in range(ahead):
            pl.when(j < n_used)(lambda j=j: start(in_copies(j)))

    def process(block, count):
        blocks = [block + j for j in range(count)]
        for blk in blocks:
            wait(in_copies(blk))
        x = [xbuf[slot_of(blk)].astype(BF16) for blk in blocks]
        x = x[0] if count == 1 else jnp.concatenate(x, axis=0)
        for blk in blocks:
            pl.when(blk + ahead < n_used)(lambda blk=blk: start(in_copies(blk + ahead)))
            pl.when(blk >= ahead)(lambda blk=blk: wait(out_copies(blk - ahead)))
        a = jnp.dot(x, wg_s[...], preferred_element_type=F32)
        u = jnp.dot(x, wu_s[...], preferred_element_type=F32)
        hb = (a * jax.nn.sigmoid(a) * u).astype(BF16)
        o = jnp.dot(hb, wd_s[...], preferred_element_type=F32)
        for j, blk in enumerate(blocks):
            obuf[slot_of(blk)] = o[j * EXPERT_ROWS:(j + 1) * EXPERT_ROWS, :]
            start(out_copies(blk))

    @pl.when(nb > 0)
    def _():
        wg_s[...] = wg_ref[0].astype(BF16)
        wu_s[...] = wu_ref[0].astype(BF16)
        wd_s[...] = wd_ref[0].astype(BF16)

        def pair(i, carry):
            process(b0 + 2 * i, 2)
            return carry

        lax.fori_loop(0, nb // 2, pair, 0)
        pl.when(nb % 2 == 1)(lambda: process(b0 + nb - 1, 1))

    @pl.when(e == N_EXPERTS - 1)
    def _():
        for j in range(ahead):
            pl.when(n_used - 1 - j >= 0)(lambda j=j: wait(out_copies(n_used - 1 - j)))
        obuf[0] = jnp.zeros((EXPERT_ROWS, D_MODEL), F32)

        def on_unused_blocks(fn):
            def body(b, c):
                fn(out_copies(b, slot=0, sem=fill_sem))
                return c
            lax.fori_loop(n_used, n_blocks, body, 0)

        on_unused_blocks(start)
        on_unused_blocks(wait)


def _experts(first_block, n_expert_blocks, n_used, x_sorted, w_gate_e, w_up_e, w_down_e):
    rows = x_sorted.shape[0]
    wmap = lambda e, *_: (e, 0, 0)
    kern = functools.partial(_experts_kernel, n_blocks=rows // EXPERT_ROWS)
    return pl.pallas_call(
        kern,
        grid_spec=pltpu.PrefetchScalarGridSpec(
            num_scalar_prefetch=3,
            grid=(N_EXPERTS,),
            in_specs=[
                pl.BlockSpec(memory_space=pl.ANY),
                pl.BlockSpec((1, D_MODEL, F_EXPERT), wmap),
                pl.BlockSpec((1, D_MODEL, F_EXPERT), wmap),
                pl.BlockSpec((1, F_EXPERT, D_MODEL), wmap),
            ],
            out_specs=pl.BlockSpec(memory_space=pl.ANY),
            scratch_shapes=[
                pltpu.VMEM((EXPERT_RING, EXPERT_ROWS, D_MODEL), F32),
                pltpu.VMEM((EXPERT_RING, EXPERT_ROWS, D_MODEL), F32),
                pltpu.VMEM((D_MODEL, F_EXPERT), BF16),
                pltpu.VMEM((D_MODEL, F_EXPERT), BF16),
                pltpu.VMEM((F_EXPERT, D_MODEL), BF16),
                pltpu.SemaphoreType.DMA((EXPERT_RING,)),
                pltpu.SemaphoreType.DMA((EXPERT_RING,)),
                pltpu.SemaphoreType.DMA,
            ],
        ),
        out_shape=jax.ShapeDtypeStruct(x_sorted.shape, F32),
        compiler_params=_params(1),
        name="experts",
    )(first_block, n_expert_blocks, n_used, x_sorted, w_gate_e, w_up_e, w_down_e)


def _combine_kernel(dcur_ref, dnext_ref, w_ref, yp_ref, ln_g_ref, ln_b_ref, os_ref, y_ref, gbuf, rbuf, sem):
    i = pl.program_id(0)
    t = y_ref.shape[0]
    gtiles = gbuf.reshape(2 * t * TOP_K, SLABS, LANES)
    rtiles = rbuf.reshape(t, SLABS, LANES)

    def row_copy(dref, half, j, k):
        return pltpu.make_async_copy(os_ref.at[pl.ds(dref[k, j], 1)],
                                     gtiles.at[pl.ds((half * t + j) * TOP_K + k, 1)], sem.at[half])

    def issue(dref, half):
        def body(j, c):
            for k in range(TOP_K):
                row_copy(dref, half, j, k).start(priority=k % DMA_PRIORITIES)
            return c
        lax.fori_loop(0, t, body, 0, unroll=DMA_UNROLL)

    def drain(dref, half):
        def body(j, c):
            for k in range(TOP_K):
                row_copy(dref, half, j, k).wait()
            return c
        lax.fori_loop(0, t, body, 0, unroll=DMA_UNROLL)

    def step(half):
        pl.when(i == 0)(lambda: issue(dcur_ref, half))
        pl.when(i + 1 < pl.num_programs(0))(lambda: issue(dnext_ref, 1 - half))
        drain(dcur_ref, half)

        def weighted_sum(j, c):
            rows = gtiles[pl.ds((half * t + j) * TOP_K, TOP_K)]
            acc = w_ref[j, 0] * rows[0:1]
            for k in range(1, TOP_K):
                acc = acc + w_ref[j, k] * rows[k:k + 1]
            rtiles[pl.ds(j, 1)] = acc
            return c

        lax.fori_loop(0, t, weighted_sum, 0, unroll=SLABS)
        y_ref[...] = _layer_norm(yp_ref[...] + _load_row_tiles(rbuf, t), ln_g_ref[...], ln_b_ref[...])

    pl.when(i % 2 == 0)(lambda: step(0))
    pl.when(i % 2 == 1)(lambda: step(1))


def _combine(dest_t, w2, y_part, ln_g, ln_b, out_sorted, *, n, row_offset, name):
    tm = COMBINE_ROWS
    off = row_offset // tm
    last = n // tm - 1
    fixed = lambda i: (0, 0)
    cols = lambda i: (0, i + off)
    return pl.pallas_call(
        _combine_kernel,
        grid=(n // tm,),
        in_specs=[
            pl.BlockSpec((TOP_K, tm), cols, memory_space=pltpu.SMEM),
            pl.BlockSpec((TOP_K, tm), lambda i: (0, jnp.minimum(i + 1, last) + off), memory_space=pltpu.SMEM),
            pl.BlockSpec((tm, TOP_K), lambda i: (i + off, 0), memory_space=pltpu.SMEM),
            pl.BlockSpec((tm, D_MODEL), lambda i: (i + off, 0)),
            pl.BlockSpec((1, D_MODEL), fixed),
            pl.BlockSpec((1, D_MODEL), fixed),
            pl.BlockSpec(memory_space=pl.ANY),
        ],
        out_specs=pl.BlockSpec((tm, D_MODEL), lambda i: (i, 0)),
        out_shape=jax.ShapeDtypeStruct((n, D_MODEL), F32),
        scratch_shapes=[pltpu.VMEM((2 * TOP_K * tm * SLABS, LANES), F32), pltpu.VMEM((tm * SLABS, LANES), F32),
                        pltpu.SemaphoreType.DMA((2,))],
        compiler_params=_params(1),
        name=name,
    )(dest_t, dest_t, w2, y_part, ln_g, ln_b, out_sorted)


def _route_kernel(lgp_ref, lgs_ref, bias_ref, eidx_ref, w_ref, rank_ref, cnt_ref, carry_ref, *, prompt_tiles):
    @pl.when(pl.program_id(0) == 0)
    def _():
        carry_ref[...] = jnp.zeros_like(carry_ref)

    t = lgp_ref.shape[1]
    gsz = N_EXPERTS // N_GROUPS
    neg = -jnp.inf
    s = jax.nn.sigmoid(jnp.where(pl.program_id(0) < prompt_tiles, lgp_ref[...], lgs_ref[...]))
    biased = s + bias_ref[...]
    io_g = lax.broadcasted_iota(jnp.int32, (gsz, t), 0)
    grp_rows = []
    for g in range(N_GROUPS):
        blk = biased[g * gsz:(g + 1) * gsz, :]
        m1 = jnp.max(blk, axis=0, keepdims=True)
        i1 = jnp.min(jnp.where(blk == m1, io_g, gsz), axis=0, keepdims=True)
        m2 = jnp.max(jnp.where(io_g == i1, neg, blk), axis=0, keepdims=True)
        grp_rows.append(m1 + m2)
    gs = jnp.concatenate(grp_rows, axis=0)
    io8 = lax.broadcasted_iota(jnp.int32, (N_GROUPS, t), 0)
    gsel = jnp.zeros((N_GROUPS, t), jnp.int32)
    for _ in range(TOPK_GROUPS):
        m = jnp.max(gs, axis=0, keepdims=True)
        gi = jnp.min(jnp.where(gs == m, io8, N_GROUPS), axis=0, keepdims=True)
        hit = io8 == gi
        gsel = jnp.where(hit, 1, gsel)
        gs = jnp.where(hit, neg, gs)
    masked = jnp.concatenate(
        [jnp.where(gsel[g:g + 1, :] > 0, biased[g * gsz:(g + 1) * gsz, :], neg) for g in range(N_GROUPS)], axis=0)

    eio = lax.broadcasted_iota(jnp.int32, (N_EXPERTS, t), 0)
    cur = masked
    idx_rows, w_rows = [], []
    for _ in range(TOP_K):
        m = jnp.max(cur, axis=0, keepdims=True)
        idx = jnp.min(jnp.where(cur == m, eio, N_EXPERTS), axis=0, keepdims=True)
        hit = eio == idx
        w_rows.append(jnp.sum(jnp.where(hit, s, 0.0), axis=0, keepdims=True))
        cur = jnp.where(hit, neg, cur)
        idx_rows.append(idx)
    sel = jnp.where(cur != masked, 1.0, 0.0)

    tri = jnp.where(lax.broadcasted_iota(jnp.int32, (t, t), 0) < lax.broadcasted_iota(jnp.int32, (t, t), 1), 1.0, 0.0)
    pref = jnp.dot(sel.astype(BF16), tri.astype(BF16), preferred_element_type=F32) + carry_ref[...]
    rank_rows = [jnp.sum(jnp.where(eio == idx_rows[k], pref, 0.0), axis=0, keepdims=True) for k in range(TOP_K)]
    carry_ref[...] += jnp.sum(sel, axis=1, keepdims=True)

    wk = jnp.concatenate(w_rows, axis=0)
    eidx_ref[...] = jnp.concatenate(idx_rows, axis=0)
    w_ref[...] = wk / jnp.sum(wk, axis=0, keepdims=True) * ROUTED_SCALE
    rank_ref[...] = jnp.concatenate(rank_rows, axis=0).astype(jnp.int32)
    cnt_ref[...] = carry_ref[...].astype(jnp.int32)


def _route(logits_p, logits_s, bias_col):
    t = ROUTE_TILE
    prompt_tiles = logits_p.shape[1] // t
    n = logits_p.shape[1] + logits_s.shape[1]
    col = lambda i: (0, i)
    fixed = lambda i: (0, 0)
    kern = functools.partial(_route_kernel, prompt_tiles=prompt_tiles)
    return pl.pallas_call(
        kern,
        grid=(n // t,),
        in_specs=[pl.BlockSpec((N_EXPERTS, t), lambda i: (0, jnp.minimum(i, prompt_tiles - 1))),
                  pl.BlockSpec((N_EXPERTS, t), lambda i: (0, jnp.maximum(i - prompt_tiles, 0))),
                  pl.BlockSpec((N_EXPERTS, 1), fixed)],
        out_specs=[
            pl.BlockSpec((TOP_K, t), col),
            pl.BlockSpec((TOP_K, t), col),
            pl.BlockSpec((TOP_K, t), col),
            pl.BlockSpec((N_EXPERTS, 1), fixed),
        ],
        out_shape=[
            jax.ShapeDtypeStruct((TOP_K, n), jnp.int32),
            jax.ShapeDtypeStruct((TOP_K, n), F32),
            jax.ShapeDtypeStruct((TOP_K, n), jnp.int32),
            jax.ShapeDtypeStruct((N_EXPERTS, 1), jnp.int32),
        ],
        scratch_shapes=[pltpu.VMEM((N_EXPERTS, 1), F32)],
        compiler_params=_params(1),
        name="route",
    )(logits_p, logits_s, bias_col)


def _dest_kernel(eidx_ref, rank_ref, start_ref, dest_ref):
    t = eidx_ref.shape[1]
    eio = lax.broadcasted_iota(jnp.int32, (N_EXPERTS, t), 0)
    start = start_ref[...]
    rows = [jnp.sum(jnp.where(eio == eidx_ref[k:k + 1, :], start, 0.0), axis=0, keepdims=True) for k in range(TOP_K)]
    dest_ref[...] = jnp.concatenate(rows, axis=0).astype(jnp.int32) + rank_ref[...]


def _dest(eidx_t, rank_t, pad_start_col):
    n = eidx_t.shape[1]
    t = ROW_TILE
    col = lambda i: (0, i)
    return pl.pallas_call(
        _dest_kernel,
        grid=(n // t,),
        in_specs=[pl.BlockSpec((TOP_K, t), col), pl.BlockSpec((TOP_K, t), col),
                  pl.BlockSpec((N_EXPERTS, 1), lambda i: (0, 0))],
        out_specs=pl.BlockSpec((TOP_K, t), col),
        out_shape=jax.ShapeDtypeStruct((TOP_K, n), jnp.int32),
        compiler_params=_params(1),
        name="dest",
    )(eidx_t, rank_t, pad_start_col)


def _dispatch_kernel(fill_ref, len_ref, nu_ref, dest_ref, hp_ref, hs_ref, wg_ref, wu_ref, wd_ref, xs_ref, yp_ref,
                     zbuf, sem, fill_sem, *, prompt_tiles, n_blocks):
    i = pl.program_id(0)

    @pl.when(i == 0)
    def _():
        zbuf[...] = jnp.zeros_like(zbuf)

        def fill_copy(row0, size):
            return pltpu.make_async_copy(zbuf.at[pl.ds(0, size)], xs_ref.at[pl.ds(row0, size)], fill_sem)

        def on_padding(fn):
            def body(e, c):
                base, length = fill_ref[e], len_ref[e]
                size = EXPERT_ROWS // 2
                while size >= 1:
                    piece = fill_copy(base + (length & ~(2 * size - 1)), size)
                    pl.when((length & size) != 0)(functools.partial(fn, piece))
                    size //= 2
                return c
            lax.fori_loop(0, N_EXPERTS, body, 0)

        def on_unused_blocks(fn):
            lax.fori_loop(nu_ref[0], n_blocks, lambda b, c: (fn(fill_copy(b * EXPERT_ROWS, EXPERT_ROWS)), c)[1], 0)

        on_padding(lambda cp: cp.start())
        on_unused_blocks(lambda cp: cp.start())
        on_padding(lambda cp: cp.wait())
        on_unused_blocks(lambda cp: cp.wait())

    def tile_step(src_ref):
        t = src_ref.shape[0]

        def row_copy(j, k):
            return pltpu.make_async_copy(src_ref.at[pl.ds(j, 1)], xs_ref.at[pl.ds(dest_ref[k, j], 1)], sem)

        def issue(j, c):
            for k in range(TOP_K):
                row_copy(j, k).start(priority=k % DMA_PRIORITIES)
            return c

        def drain(j, c):
            for k in range(TOP_K):
                row_copy(j, k).wait()
            return c

        lax.fori_loop(0, t, issue, 0, unroll=DMA_UNROLL)
        h = _load_row_tiles(src_ref.reshape(t * SLABS, LANES), t)
        hb = h.astype(BF16)
        a = jnp.dot(hb, wg_ref[...], preferred_element_type=F32)
        u = jnp.dot(hb, wu_ref[...], preferred_element_type=F32)
        shared = jnp.dot((a * jax.nn.sigmoid(a) * u).astype(BF16), wd_ref[...], preferred_element_type=F32)
        yp_ref[...] = ALPHA * h + shared
        lax.fori_loop(0, t, drain, 0, unroll=DMA_UNROLL)

    @pl.when(i < prompt_tiles)
    def _():
        tile_step(hp_ref)

    @pl.when(i >= prompt_tiles)
    def _():
        tile_step(hs_ref)


def _dispatch(fill_start, fill_len, n_used, dest_t, hp_p, hp_s, wg_b, wu_b, wd_b, *, n_blocks):
    t = DISPATCH_ROWS
    prompt_tiles = hp_p.shape[0] // t
    sample_tiles = hp_s.shape[0] // t
    tile = (t, SLABS, LANES)
    fixed = lambda i, *_: (0, 0)
    kern = functools.partial(_dispatch_kernel, prompt_tiles=prompt_tiles, n_blocks=n_blocks)
    return pl.pallas_call(
        kern,
        grid_spec=pltpu.PrefetchScalarGridSpec(
            num_scalar_prefetch=3,
            grid=(prompt_tiles + sample_tiles,),
            in_specs=[
                pl.BlockSpec((TOP_K, t), lambda i, *_: (0, i), memory_space=pltpu.SMEM),
                pl.BlockSpec(tile, lambda i, *_: (jnp.minimum(i, prompt_tiles - 1), 0, 0)),
                pl.BlockSpec(tile, lambda i, *_: (jnp.maximum(i - prompt_tiles, 0), 0, 0)),
                pl.BlockSpec((D_MODEL, F_EXPERT), fixed),
                pl.BlockSpec((D_MODEL, F_EXPERT), fixed),
                pl.BlockSpec((F_EXPERT, D_MODEL), fixed),
            ],
            out_specs=[pl.BlockSpec(memory_space=pl.ANY), pl.BlockSpec((t, D_MODEL), lambda i, *_: (i, 0))],
            scratch_shapes=[pltpu.VMEM((EXPERT_ROWS, SLABS, LANES), F32), pltpu.SemaphoreType.DMA,
                            pltpu.SemaphoreType.DMA],
        ),
        out_shape=[jax.ShapeDtypeStruct((n_blocks * EXPERT_ROWS, SLABS, LANES), F32),
                   jax.ShapeDtypeStruct(((prompt_tiles + sample_tiles) * t, D_MODEL), F32)],
        compiler_params=_params(1),
        name="dispatch",
    )(fill_start, fill_len, n_used, dest_t, hp_p, hp_s, wg_b, wu_b, wd_b)


def _block_plan(counts):
    padded = (counts + EXPERT_ROWS - 1) // EXPERT_ROWS * EXPERT_ROWS
    pad_end = jnp.cumsum(padded).astype(jnp.int32)
    pad_start = pad_end - padded
    n_used = pad_end[-1] // EXPERT_ROWS
    fill_start = pad_start + counts
    fill_len = pad_end - fill_start
    first_block = pad_start // EXPERT_ROWS
    n_expert_blocks = (padded // EXPERT_ROWS).astype(jnp.int32)
    return pad_start, fill_start, fill_len, first_block, n_expert_blocks, n_used.reshape(1).astype(jnp.int32)


def kernel(x_prompt, x_sample, cache_k, cache_v, w_in, sink, gm_ln_g, gm_ln_b, gm_w_s, gm_b_s, w_out, ln1_g, ln1_b,
           router_w, router_bias, w_gate_e, w_up_e, w_down_e, w_gate_s, w_up_s, w_down_s, ln2_g, ln2_b):
    bp, sp = x_prompt.shape[:2]
    bs, ts = x_sample.shape[:2]
    r = cache_k.shape[2]
    assert r == WINDOW and sp % ROW_TILE == 0 and (bs * ts) % ROW_TILE == 0
    n_p, n_s = bp * sp, bs * ts
    n_total = n_p + n_s
    l = 0

    w_in_b = w_in[l].astype(BF16)
    w_out_b = w_out[l].astype(BF16)
    router_t = router_w[l].T
    r_hi = router_t.astype(BF16)
    r_lo = (router_t - r_hi.astype(F32)).astype(BF16)
    row_vec = lambda v: v.reshape(1, -1)
    gm_g, gm_b = row_vec(gm_ln_g[l]), row_vec(gm_ln_b[l])
    sink_l = sink[l].astype(F32)

    xp2 = x_prompt.reshape(n_p, D_MODEL)
    tabs_p = _rope_tables(jnp.arange(sp, dtype=jnp.int32))
    q, k, v, gu, gv = _in_proj(xp2, w_in_b, tabs_p, gm_g, gm_b, tm=ROW_TILE, gv_dtype=BF16, name="in_proj_prompt")
    nb = sp // WINDOW
    att = _attention(sink_l, q, k, v, k, v, batch=bp, nb=nb, tq=WINDOW, seqs=1, prev_blocks=nb,
                     first_block_has_no_prev=True, name="attn_prompt")
    b_tab_p = jnp.repeat(gm_b_s[l].T, HEAD_DIM, axis=1)
    gm = _gate(gu, gv, gm_w_s[l], b_tab_p, chunk=CHUNK, n_chunks=ROW_TILE // CHUNK, name="gate_prompt")
    last_rows = lambda t: t.reshape(bp, sp, KV_WIDTH)[:, sp - r:, :].reshape(1, bp, r, KV_HEADS, HEAD_DIM)
    new_kp, new_vp = last_rows(k), last_rows(v)
    h_p, lt_p = _out_proj(att, gm, xp2, w_out_b, row_vec(ln1_g[l]), row_vec(ln1_b[l]), r_hi, r_lo,
                          name="out_proj_prompt")

    xs2 = x_sample.reshape(n_s, D_MODEL)
    pos_s = PAST_LEN + jnp.arange(ts, dtype=jnp.int32)
    tabs_s = tuple(jnp.tile(t, (bs, 1)) for t in _rope_tables(pos_s))
    q, k, v, gu, gv = _in_proj(xs2, w_in_b, tabs_s, gm_g, gm_b, tm=n_s, gv_dtype=F32, name="in_proj_sample")
    tq = 8
    pad_rows = lambda t: jnp.pad(t.reshape(bs, ts, -1), ((0, 0), (0, tq - ts), (0, 0))).reshape(bs * tq, -1)
    ck2 = cache_k[l].reshape(bs * r, KV_WIDTH)
    cv2 = cache_v[l].reshape(bs * r, KV_WIDTH)
    att = _attention(sink_l, pad_rows(q), pad_rows(k), pad_rows(v), ck2, cv2, batch=bs, nb=1, tq=tq,
                     seqs=SAMPLE_SEQS_PER_STEP, prev_blocks=1, first_block_has_no_prev=False, name="attn_sample")
    att = att.reshape(bs, tq, ATT_WIDTH)[:, :ts].reshape(n_s, ATT_WIDTH)
    gm = _gate_short(gu, gv, gm_w_s[l], gm_b_s[l], length=ts, name="gate_sample")
    new_ks = jnp.concatenate([cache_k[l], k.reshape(bs, ts, KV_HEADS, HEAD_DIM)], axis=1)[:, ts:][None]
    new_vs = jnp.concatenate([cache_v[l], v.reshape(bs, ts, KV_HEADS, HEAD_DIM)], axis=1)[:, ts:][None]
    new_gs = gv.reshape(bs, ts, GM_WIDTH)[None]
    h_s, lt_s = _out_proj(att, gm, xs2, w_out_b, row_vec(ln1_g[l]), row_vec(ln1_b[l]), r_hi, r_lo,
                          name="out_proj_sample")

    eidx_t, w_t, rank_t, counts = _route(lt_p, lt_s, router_bias[l].astype(F32).reshape(N_EXPERTS, 1))
    a = n_total * TOP_K
    n_blocks = -(-(a + N_EXPERTS * (EXPERT_ROWS - 1)) // EXPERT_ROWS)
    pad_start, fill_start, fill_len, first_block, n_expert_blocks, n_used = _block_plan(counts.reshape(N_EXPERTS))
    dest_t = _dest(eidx_t, rank_t, pad_start.astype(F32).reshape(N_EXPERTS, 1))
    tiles = lambda a: a.reshape(-1, SLABS, LANES)
    shared = (w_gate_s[l].astype(BF16), w_up_s[l].astype(BF16), w_down_s[l].astype(BF16))
    x_sorted, y_part = _dispatch(fill_start, fill_len, n_used, dest_t, tiles(h_p), tiles(h_s), *shared,
                                 n_blocks=n_blocks)
    out_sorted = _experts(first_block, n_expert_blocks, n_used, x_sorted, w_gate_e[l], w_up_e[l], w_down_e[l])
    ln2 = (row_vec(ln2_g[l]), row_vec(ln2_b[l]))
    w2 = w_t.T
    y_p = _combine(dest_t, w2, y_part, *ln2, out_sorted, n=n_p, row_offset=0, name="combine_prompt")
    y_s = _combine(dest_t, w2, y_part, *ln2, out_sorted, n=n_s, row_offset=n_p, name="combine_sample")
    return (y_p.reshape(bp, sp, D_MODEL), y_s.reshape(bs, ts, D_MODEL), new_kp, new_vp, new_ks, new_vs, new_gs)
```

```python
import functools

import jax
import jax.numpy as jnp
import numpy as np
from jax import lax
from jax.experimental import pallas as pl
from jax.experimental.pallas import tpu as pltpu

D_MODEL = 1024
HEAD_DIM = 64
ATT_HEADS = 8
KV_HEADS = 2
Q_PER_KV = ATT_HEADS // KV_HEADS
GM_HEADS = 8
ATT_WIDTH = ATT_HEADS * HEAD_DIM
KV_WIDTH = KV_HEADS * HEAD_DIM
GM_WIDTH = GM_HEADS * HEAD_DIM
ROPE_WIDTH = ATT_WIDTH + KV_WIDTH
IN_WIDTH = ATT_WIDTH + 2 * KV_WIDTH + 2 * GM_WIDTH
WINDOW = 128
CHUNK = 128
PAST_LEN = 16384
ROPE_THETA = 10000.0
ATT_SCALE = HEAD_DIM ** -0.5
N_EXPERTS = 256
TOP_K = 8
N_GROUPS = 8
TOPK_GROUPS = 4
F_EXPERT = 256
ROUTED_SCALE = 2.5
LN_EPS = 1e-5
DEPTH = 1
ALPHA = (2.0 * DEPTH) ** 0.25

LANES = 128
SLABS = 8
PACKED = D_MODEL // 2
WORD_SLABS = PACKED // LANES
ROW_TILE = 512
EXPERT_ROWS = 256
EXPERT_RING = 8
COMBINE_ROWS = 256
DISPATCH_ROWS = 256
ROUTE_TILE = 256
SAMPLE_SEQS_PER_STEP = 16
DMA_UNROLL = 4
DMA_PRIORITIES = 2
VMEM_LIMIT = 56 * 1024 * 1024

F32 = jnp.float32
BF16 = jnp.bfloat16


def _params(n_axes):
    return pltpu.CompilerParams(dimension_semantics=("arbitrary",) * n_axes, vmem_limit_bytes=VMEM_LIMIT)


def _layer_norm(x, g, b):
    mu = jnp.mean(x, axis=-1, keepdims=True)
    xc = x - mu
    var = jnp.mean(xc * xc, axis=-1, keepdims=True)
    return xc * lax.rsqrt(var + LN_EPS) * g + b


def _pack_halves(x):
    xb = x.astype(BF16)
    lo = lax.bitcast_convert_type(xb[:, :PACKED].astype(F32), jnp.uint32)
    hi = lax.bitcast_convert_type(xb[:, PACKED:].astype(F32), jnp.uint32)
    return (lo >> 16) | hi


def _unpack_halves(w):
    lo = lax.bitcast_convert_type(w << 16, F32)
    hi = lax.bitcast_convert_type(w & jnp.uint32(0xFFFF0000), F32)
    return lo, hi


def _store_row_tiles(ref2d, x):
    m, slabs = x.shape[0], x.shape[1] // LANES
    for s in range(slabs):
        ref2d[pl.ds(s, m, stride=slabs), :] = x[:, s * LANES:(s + 1) * LANES]


def _load_row_tiles(ref2d, m, slabs):
    return jnp.concatenate([ref2d[pl.ds(s, m, stride=slabs), :] for s in range(slabs)], axis=1)


def _gelu(x):
    return 0.5 * x * (1.0 + lax.erf(x * np.float32(np.sqrt(0.5))))


def _in_proj_kernel(x_ref, w_ref, cos_ref, sa_ref, sb_ref, g_ref, b_ref, q_ref, k_ref, v_ref, gu_ref, gv_ref):
    x = x_ref[...].astype(BF16)
    zr = jnp.dot(x, w_ref[:, :ROPE_WIDTH], preferred_element_type=F32)
    pieces = []
    for c in range(ROPE_WIDTH // LANES):
        zc = zr[:, c * LANES:(c + 1) * LANES]
        tl = slice(0, LANES) if c < ATT_WIDTH // LANES else slice(LANES, 2 * LANES)
        pieces.append(zc * cos_ref[:, tl]
                      + pltpu.roll(zc, LANES - HEAD_DIM // 2, 1) * sa_ref[:, tl]
                      + pltpu.roll(zc, HEAD_DIM // 2, 1) * sb_ref[:, tl])
    for c in range(ATT_WIDTH // LANES):
        q_ref[:, c * LANES:(c + 1) * LANES] = pieces[c].astype(q_ref.dtype)
    k_ref[...] = pieces[ATT_WIDTH // LANES]
    v_ref[...] = jnp.dot(x, w_ref[:, ROPE_WIDTH:ROPE_WIDTH + KV_WIDTH], preferred_element_type=F32)
    g0 = ROPE_WIDTH + KV_WIDTH
    zu = jnp.dot(x, w_ref[:, g0:g0 + GM_WIDTH], preferred_element_type=F32)
    gu_ref[...] = _gelu(zu).astype(gu_ref.dtype)
    zv = jnp.dot(x, w_ref[:, g0 + GM_WIDTH:g0 + 2 * GM_WIDTH], preferred_element_type=F32)
    gv = _layer_norm(_gelu(zv), g_ref[...], b_ref[...])
    gv_ref[...] = gv.astype(gv_ref.dtype)


def _in_proj(x2, w_in_b, tabs, ln_g, ln_b, *, tm, gv_dtype, name):
    n = x2.shape[0]
    cos_t, sa_t, sb_t = tabs
    period = cos_t.shape[0] // tm
    row = lambda i: (i, 0)
    tab = lambda i: (i % period, 0)
    fixed = lambda i: (0, 0)
    return pl.pallas_call(
        _in_proj_kernel,
        grid=(n // tm,),
        in_specs=[
            pl.BlockSpec((tm, D_MODEL), row),
            pl.BlockSpec((D_MODEL, IN_WIDTH), fixed),
            pl.BlockSpec((tm, 2 * LANES), tab),
            pl.BlockSpec((tm, 2 * LANES), tab),
            pl.BlockSpec((tm, 2 * LANES), tab),
            pl.BlockSpec((1, GM_WIDTH), fixed),
            pl.BlockSpec((1, GM_WIDTH), fixed),
        ],
        out_specs=[
            pl.BlockSpec((tm, ATT_WIDTH), row),
            pl.BlockSpec((tm, KV_WIDTH), row),
            pl.BlockSpec((tm, KV_WIDTH), row),
            pl.BlockSpec((tm, GM_WIDTH), row),
            pl.BlockSpec((tm, GM_WIDTH), row),
        ],
        out_shape=[
            jax.ShapeDtypeStruct((n, ATT_WIDTH), BF16),
            jax.ShapeDtypeStruct((n, KV_WIDTH), F32),
            jax.ShapeDtypeStruct((n, KV_WIDTH), F32),
            jax.ShapeDtypeStruct((n, GM_WIDTH), BF16),
            jax.ShapeDtypeStruct((n, GM_WIDTH), gv_dtype),
        ],
        compiler_params=_params(1),
        name=name,
    )(x2, w_in_b, cos_t, sa_t, sb_t, ln_g, ln_b)


def _rope_tables(pos):
    half = HEAD_DIM // 2
    lane = jnp.arange(2 * LANES, dtype=jnp.int32)
    inv = ROPE_THETA ** (-(lane % half).astype(F32) * 2.0 / HEAD_DIM)
    ang = pos.astype(F32)[:, None] * inv[None, :]
    scale = jnp.where(lane < LANES, ATT_SCALE, 1.0).astype(F32)[None, :]
    first_half = ((lane % HEAD_DIM) < half)[None, :]
    cos, sin = jnp.cos(ang) * scale, jnp.sin(ang) * scale
    return cos, jnp.where(first_half, -sin, 0.0), jnp.where(first_half, 0.0, sin)


def _attn_kernel(sink_ref, q_ref, kc_ref, vc_ref, kp_ref, vp_ref, o_ref, *, tq, seqs, stack, first_block_has_no_prev):
    nk = WINDOW + tq
    rows = stack * tq
    qi = lax.broadcasted_iota(jnp.int32, (rows, nk), 0) & (tq - 1)
    ks = lax.broadcasted_iota(jnp.int32, (rows, nk), 1)
    mask = (ks >= qi) & (ks <= qi + WINDOW)
    if first_block_has_no_prev:
        mask = mask & ((pl.program_id(1) > 0) | (ks >= WINDOW))
    sinks = [jnp.concatenate([jnp.full((tq, 1), sink_ref[h0 + j], F32) for j in range(stack)], axis=0)
             for h0 in range(0, ATT_HEADS, stack)]
    for b in range(seqs):
        qrows = slice(b * tq, (b + 1) * tq)
        prows = slice(b * WINDOW, (b + 1) * WINDOW)
        q = q_ref[qrows, :]
        kk = jnp.concatenate([kp_ref[prows, :], kc_ref[qrows, :]], axis=0).astype(BF16)
        vv = jnp.concatenate([vp_ref[prows, :], vc_ref[qrows, :]], axis=0).astype(BF16)
        outs = []
        for i, h0 in enumerate(range(0, ATT_HEADS, stack)):
            g = h0 // Q_PER_KV
            kg = kk[:, g * HEAD_DIM:(g + 1) * HEAD_DIM]
            vg = vv[:, g * HEAD_DIM:(g + 1) * HEAD_DIM]
            qg = jnp.concatenate([q[:, (h0 + j) * HEAD_DIM:(h0 + j + 1) * HEAD_DIM] for j in range(stack)], axis=0)
            s = lax.dot_general(qg, kg, (((1,), (1,)), ((), ())), preferred_element_type=F32)
            s = jnp.where(mask, s, -jnp.inf)
            m = jnp.maximum(jnp.max(s, axis=-1, keepdims=True), sinks[i])
            p = jnp.exp(s - m)
            denom = jnp.sum(p, axis=-1, keepdims=True) + jnp.exp(sinks[i] - m)
            o = jnp.dot((p / denom).astype(BF16), vg, preferred_element_type=F32)
            outs.extend(o[j * tq:(j + 1) * tq, :] for j in range(stack))
        o_ref[qrows, :] = jnp.concatenate(outs, axis=1).astype(o_ref.dtype)


def _attention(sink, q2, k2, v2, kprev2, vprev2, *, batch, nb, tq, seqs, prev_blocks, first_block_has_no_prev, name):
    assert tq & (tq - 1) == 0 and batch % seqs == 0 and (seqs == 1 or nb == prev_blocks == 1)
    cur = lambda b, n, s: (b * nb + n, 0)
    prev = lambda b, n, s: (b * prev_blocks + jnp.maximum(n - 1, 0), 0)
    stack = Q_PER_KV if Q_PER_KV * tq <= WINDOW else 1
    kern = functools.partial(_attn_kernel, tq=tq, seqs=seqs, stack=stack,
                             first_block_has_no_prev=first_block_has_no_prev)
    return pl.pallas_call(
        kern,
        grid_spec=pltpu.PrefetchScalarGridSpec(
            num_scalar_prefetch=1,
            grid=(batch // seqs, nb),
            in_specs=[
                pl.BlockSpec((seqs * tq, ATT_WIDTH), cur),
                pl.BlockSpec((seqs * tq, KV_WIDTH), cur),
                pl.BlockSpec((seqs * tq, KV_WIDTH), cur),
                pl.BlockSpec((seqs * WINDOW, KV_WIDTH), prev),
                pl.BlockSpec((seqs * WINDOW, KV_WIDTH), prev),
            ],
            out_specs=pl.BlockSpec((seqs * tq, ATT_WIDTH), cur),
        ),
        out_shape=jax.ShapeDtypeStruct(q2.shape, BF16),
        compiler_params=_params(2),
        name=name,
    )(sink, q2, k2, v2, kprev2, vprev2)


def _gate_kernel(gu_ref, gv_ref, w_ref, b_ref, o_ref, *, chunk, n_chunks):
    ri = lax.broadcasted_iota(jnp.int32, (chunk, chunk), 0)
    ci = lax.broadcasted_iota(jnp.int32, (chunk, chunk), 1)
    ws = [jnp.where(ci <= ri, w_ref[h], 0.0).astype(BF16) for h in range(GM_HEADS)]
    for c in range(n_chunks):
        rows = slice(c * chunk, (c + 1) * chunk)
        gv = gv_ref[rows, :].astype(BF16)
        sv = jnp.concatenate(
            [jnp.dot(ws[h], gv[:, h * HEAD_DIM:(h + 1) * HEAD_DIM], preferred_element_type=F32)
             for h in range(GM_HEADS)], axis=1)
        o_ref[rows, :] = (gu_ref[rows, :].astype(F32) * (sv + b_ref[...])).astype(o_ref.dtype)


def _gate(gu2, gv2, w_s, b_tab, *, chunk, n_chunks, name):
    n = gu2.shape[0]
    tm = chunk * n_chunks
    row = lambda i: (i, 0)
    kern = functools.partial(_gate_kernel, chunk=chunk, n_chunks=n_chunks)
    return pl.pallas_call(
        kern,
        grid=(n // tm,),
        in_specs=[
            pl.BlockSpec((tm, GM_WIDTH), row),
            pl.BlockSpec((tm, GM_WIDTH), row),
            pl.BlockSpec((GM_HEADS, chunk, chunk), lambda i: (0, 0, 0)),
            pl.BlockSpec((chunk, GM_WIDTH), lambda i: (0, 0)),
        ],
        out_specs=pl.BlockSpec((tm, GM_WIDTH), row),
        out_shape=jax.ShapeDtypeStruct((n, GM_WIDTH), BF16),
        compiler_params=_params(1),
        name=name,
    )(gu2, gv2, w_s, b_tab)


def _gate_short_kernel(gu_ref, gv_ref, coef_ref, b_ref, o_ref, *, length):
    n = gv_ref.shape[0]
    gv = gv_ref[...].astype(F32)
    tiles = lambda x: x.reshape(n // SLABS, SLABS, GM_WIDTH)
    sv = tiles(gv) * coef_ref[0][None] + b_ref[...][None]
    for d in range(1, length):
        sv = sv + tiles(pltpu.roll(gv, d, 0)) * coef_ref[d][None]
    o_ref[...] = (gu_ref[...].astype(F32) * sv.reshape(n, GM_WIDTH)).astype(o_ref.dtype)


def _gate_short(gu2, gv2, w_s, b_s, *, length, name):
    n = gu2.shape[0]
    assert SLABS % length == 0 and n % SLABS == 0
    pos = jnp.arange(length)
    lag = jnp.arange(length)
    src = pos[None, :] - lag[:, None]
    coef = jnp.where(src >= 0, w_s[:, pos[None, :], jnp.maximum(src, 0)], 0.0)
    coef = jnp.repeat(coef.transpose(1, 2, 0), HEAD_DIM, axis=2)
    coef = jnp.tile(coef, (1, SLABS // length, 1))
    bias = jnp.tile(jnp.repeat(b_s[:, :length].T, HEAD_DIM, axis=1), (SLABS // length, 1))
    whole = lambda shape: pl.BlockSpec(shape, lambda i: (0,) * len(shape))
    return pl.pallas_call(
        functools.partial(_gate_short_kernel, length=length),
        grid=(1,),
        in_specs=[whole((n, GM_WIDTH)), whole((n, GM_WIDTH)), whole((length, SLABS, GM_WIDTH)),
                  whole((SLABS, GM_WIDTH))],
        out_specs=whole((n, GM_WIDTH)),
        out_shape=jax.ShapeDtypeStruct((n, GM_WIDTH), BF16),
        compiler_params=_params(1),
        name=name,
    )(gu2, gv2, coef, bias)


def _out_proj_kernel(att_ref, gm_ref, x_ref, wo_ref, g_ref, b_ref, rh_ref, rl_ref, h_ref, hp_ref, lg_ref):
    mix = jnp.dot(att_ref[...], wo_ref[:ATT_WIDTH, :], preferred_element_type=F32)
    mix = mix + jnp.dot(gm_ref[...], wo_ref[ATT_WIDTH:, :], preferred_element_type=F32)
    h = _layer_norm(ALPHA * x_ref[...] + mix, g_ref[...], b_ref[...])
    h_ref[...] = h
    _store_row_tiles(hp_ref, _pack_halves(h))
    h_hi = h.astype(BF16)
    h_lo = (h - h_hi.astype(F32)).astype(BF16)
    nt = (((1,), (1,)), ((), ()))
    lg = lax.dot_general(rh_ref[...], h_hi, nt, preferred_element_type=F32)
    lg = lg + lax.dot_general(rh_ref[...], h_lo, nt, preferred_element_type=F32)
    lg = lg + lax.dot_general(rl_ref[...], h_hi, nt, preferred_element_type=F32)
    lg_ref[...] = lg


def _out_proj(att2, gm2, x2, w_out_b, ln_g, ln_b, r_hi, r_lo, *, name):
    n = x2.shape[0]
    tm = ROW_TILE
    row = lambda i: (i, 0)
    fixed = lambda i: (0, 0)
    return pl.pallas_call(
        _out_proj_kernel,
        grid=(n // tm,),
        in_specs=[
            pl.BlockSpec((tm, ATT_WIDTH), row),
            pl.BlockSpec((tm, GM_WIDTH), row),
            pl.BlockSpec((tm, D_MODEL), row),
            pl.BlockSpec((D_MODEL, D_MODEL), fixed),
            pl.BlockSpec((1, D_MODEL), fixed),
            pl.BlockSpec((1, D_MODEL), fixed),
            pl.BlockSpec((N_EXPERTS, D_MODEL), fixed),
            pl.BlockSpec((N_EXPERTS, D_MODEL), fixed),
        ],
        out_specs=[
            pl.BlockSpec((tm, D_MODEL), row),
            pl.BlockSpec((tm * WORD_SLABS, LANES), row),
            pl.BlockSpec((N_EXPERTS, tm), lambda i: (0, i)),
        ],
        out_shape=[
            jax.ShapeDtypeStruct((n, D_MODEL), F32),
            jax.ShapeDtypeStruct((n * WORD_SLABS, LANES), jnp.uint32),
            jax.ShapeDtypeStruct((N_EXPERTS, n), F32),
        ],
        compiler_params=_params(1),
        name=name,
    )(att2, gm2, x2, w_out_b, ln_g, ln_b, r_hi, r_lo)


def _experts_kernel(first_ref, nblk_ref, nu_ref, x_hbm, wg_ref, wu_ref, wd_ref, o_hbm, xbuf, obuf, wg_s, wu_s, wd_s,
                    in_sem, out_sem, fill_sem, *, n_blocks):
    e = pl.program_id(0)
    nb = nblk_ref[e]
    b0 = first_ref[e]
    n_used = nu_ref[0]
    ahead = EXPERT_RING // 2

    def rows_of(block):
        return pl.ds(pl.multiple_of(block * EXPERT_ROWS, EXPERT_ROWS), EXPERT_ROWS)

    def slot_of(block):
        return block & (EXPERT_RING - 1)

    def slab(s):
        return pl.ds(s * LANES, LANES)

    def in_copies(block):
        slot = slot_of(block)
        return [pltpu.make_async_copy(x_hbm.at[rows_of(block), s, :], xbuf.at[slot, :, slab(s)], in_sem.at[slot])
                for s in range(WORD_SLABS)]

    def out_copies(block, slot=None, sem=None):
        slot = slot_of(block) if slot is None else slot
        sem = out_sem.at[slot] if sem is None else sem
        return [pltpu.make_async_copy(obuf.at[slot, :, slab(s)], o_hbm.at[rows_of(block), s, :], sem)
                for s in range(WORD_SLABS)]

    def start(copies):
        for cp in copies:
            cp.start()

    def wait(copies):
        for cp in copies:
            cp.wait()

    @pl.when(e == 0)
    def _():
        for j in range(ahead):
            pl.when(j < n_used)(lambda j=j: start(in_copies(j)))

    def process(block, count):
        blocks = [block + j for j in range(count)]
        for blk in blocks:
            wait(in_copies(blk))
        x = [xbuf[slot_of(blk)] for blk in blocks]
        xl, xh = _unpack_halves(x[0] if count == 1 else jnp.concatenate(x, axis=0))
        xl, xh = xl.astype(BF16), xh.astype(BF16)
        for blk in blocks:
            pl.when(blk + ahead < n_used)(lambda blk=blk: start(in_copies(blk + ahead)))
            pl.when(blk >= ahead)(lambda blk=blk: wait(out_copies(blk - ahead)))
        a = (jnp.dot(xl, wg_s[:PACKED, :], preferred_element_type=F32)
             + jnp.dot(xh, wg_s[PACKED:, :], preferred_element_type=F32))
        u = (jnp.dot(xl, wu_s[:PACKED, :], preferred_element_type=F32)
             + jnp.dot(xh, wu_s[PACKED:, :], preferred_element_type=F32))
        hb = (a * jax.nn.sigmoid(a) * u).astype(BF16)
        o = _pack_halves(jnp.dot(hb, wd_s[...], preferred_element_type=F32))
        for j, blk in enumerate(blocks):
            obuf[slot_of(blk)] = o[j * EXPERT_ROWS:(j + 1) * EXPERT_ROWS, :]
            start(out_copies(blk))

    @pl.when(nb > 0)
    def _():
        wg_s[...] = wg_ref[0].astype(BF16)
        wu_s[...] = wu_ref[0].astype(BF16)
        wd_s[...] = wd_ref[0].astype(BF16)

        def pair(i, carry):
            process(b0 + 2 * i, 2)
            return carry

        lax.fori_loop(0, nb // 2, pair, 0)
        pl.when(nb % 2 == 1)(lambda: process(b0 + nb - 1, 1))

    @pl.when(e == N_EXPERTS - 1)
    def _():
        for j in range(ahead):
            pl.when(n_used - 1 - j >= 0)(lambda j=j: wait(out_copies(n_used - 1 - j)))
        obuf[0] = jnp.zeros((EXPERT_ROWS, PACKED), jnp.uint32)

        def on_unused_blocks(fn):
            def body(b, c):
                fn(out_copies(b, slot=0, sem=fill_sem))
                return c
            lax.fori_loop(n_used, n_blocks, body, 0)

        on_unused_blocks(start)
        on_unused_blocks(wait)


def _experts(first_block, n_expert_blocks, n_used, x_sorted, w_gate_e, w_up_e, w_down_e):
    rows = x_sorted.shape[0]
    wmap = lambda e, *_: (e, 0, 0)
    kern = functools.partial(_experts_kernel, n_blocks=rows // EXPERT_ROWS)
    return pl.pallas_call(
        kern,
        grid_spec=pltpu.PrefetchScalarGridSpec(
            num_scalar_prefetch=3,
            grid=(N_EXPERTS,),
            in_specs=[
                pl.BlockSpec(memory_space=pl.ANY),
                pl.BlockSpec((1, D_MODEL, F_EXPERT), wmap),
                pl.BlockSpec((1, D_MODEL, F_EXPERT), wmap),
                pl.BlockSpec((1, F_EXPERT, D_MODEL), wmap),
            ],
            out_specs=pl.BlockSpec(memory_space=pl.ANY),
            scratch_shapes=[
                pltpu.VMEM((EXPERT_RING, EXPERT_ROWS, PACKED), jnp.uint32),
                pltpu.VMEM((EXPERT_RING, EXPERT_ROWS, PACKED), jnp.uint32),
                pltpu.VMEM((D_MODEL, F_EXPERT), BF16),
                pltpu.VMEM((D_MODEL, F_EXPERT), BF16),
                pltpu.VMEM((F_EXPERT, D_MODEL), BF16),
                pltpu.SemaphoreType.DMA((EXPERT_RING,)),
                pltpu.SemaphoreType.DMA((EXPERT_RING,)),
                pltpu.SemaphoreType.DMA,
            ],
        ),
        out_shape=jax.ShapeDtypeStruct(x_sorted.shape, x_sorted.dtype),
        compiler_params=_params(1),
        name="experts",
    )(first_block, n_expert_blocks, n_used, x_sorted, w_gate_e, w_up_e, w_down_e)


def _combine_kernel(dcur_ref, dnext_ref, w_ref, yp_ref, ln_g_ref, ln_b_ref, os_ref, y_ref, gbuf, rlo, rhi, sem):
    i = pl.program_id(0)
    t = y_ref.shape[0]
    gtiles = gbuf.reshape(2 * t * TOP_K, WORD_SLABS, LANES)
    lo_tiles = rlo.reshape(t, WORD_SLABS, LANES)
    hi_tiles = rhi.reshape(t, WORD_SLABS, LANES)

    def row_copy(dref, half, j, k):
        return pltpu.make_async_copy(os_ref.at[pl.ds(dref[k, j], 1)],
                                     gtiles.at[pl.ds((half * t + j) * TOP_K + k, 1)], sem.at[half])

    def issue(dref, half):
        def body(j, c):
            for k in range(TOP_K):
                row_copy(dref, half, j, k).start(priority=k % DMA_PRIORITIES)
            return c
        lax.fori_loop(0, t, body, 0, unroll=DMA_UNROLL)

    def drain(dref, half):
        def body(j, c):
            for k in range(TOP_K):
                row_copy(dref, half, j, k).wait()
            return c
        lax.fori_loop(0, t, body, 0, unroll=DMA_UNROLL)

    def step(half):
        pl.when(i == 0)(lambda: issue(dcur_ref, half))
        pl.when(i + 1 < pl.num_programs(0))(lambda: issue(dnext_ref, 1 - half))
        drain(dcur_ref, half)

        def weighted_sum(j, c):
            lo, hi = _unpack_halves(gtiles[pl.ds((half * t + j) * TOP_K, TOP_K)])
            acc_lo = w_ref[j, 0] * lo[0:1]
            acc_hi = w_ref[j, 0] * hi[0:1]
            for k in range(1, TOP_K):
                acc_lo = acc_lo + w_ref[j, k] * lo[k:k + 1]
                acc_hi = acc_hi + w_ref[j, k] * hi[k:k + 1]
            lo_tiles[pl.ds(j, 1)] = acc_lo
            hi_tiles[pl.ds(j, 1)] = acc_hi
            return c

        lax.fori_loop(0, t, weighted_sum, 0, unroll=SLABS)
        routed = jnp.concatenate([_load_row_tiles(rlo, t, WORD_SLABS), _load_row_tiles(rhi, t, WORD_SLABS)], axis=1)
        y_ref[...] = _layer_norm(yp_ref[...] + routed, ln_g_ref[...], ln_b_ref[...])

    pl.when(i % 2 == 0)(lambda: step(0))
    pl.when(i % 2 == 1)(lambda: step(1))


def _combine(dest_t, w2, y_part, ln_g, ln_b, out_sorted, *, n, row_offset, name):
    tm = COMBINE_ROWS
    off = row_offset // tm
    last = n // tm - 1
    fixed = lambda i: (0, 0)
    cols = lambda i: (0, i + off)
    return pl.pallas_call(
        _combine_kernel,
        grid=(n // tm,),
        in_specs=[
            pl.BlockSpec((TOP_K, tm), cols, memory_space=pltpu.SMEM),
            pl.BlockSpec((TOP_K, tm), lambda i: (0, jnp.minimum(i + 1, last) + off), memory_space=pltpu.SMEM),
            pl.BlockSpec((tm, TOP_K), lambda i: (i + off, 0), memory_space=pltpu.SMEM),
            pl.BlockSpec((tm, D_MODEL), lambda i: (i + off, 0)),
            pl.BlockSpec((1, D_MODEL), fixed),
            pl.BlockSpec((1, D_MODEL), fixed),
            pl.BlockSpec(memory_space=pl.ANY),
        ],
        out_specs=pl.BlockSpec((tm, D_MODEL), lambda i: (i, 0)),
        out_shape=jax.ShapeDtypeStruct((n, D_MODEL), F32),
        scratch_shapes=[pltpu.VMEM((2 * TOP_K * tm * WORD_SLABS, LANES), jnp.uint32),
                        pltpu.VMEM((tm * WORD_SLABS, LANES), F32), pltpu.VMEM((tm * WORD_SLABS, LANES), F32),
                        pltpu.SemaphoreType.DMA((2,))],
        compiler_params=_params(1),
        name=name,
    )(dest_t, dest_t, w2, y_part, ln_g, ln_b, out_sorted)


def _route_kernel(lgp_ref, lgs_ref, bias_ref, eidx_ref, w_ref, rank_ref, cnt_ref, carry_ref, *, prompt_tiles):
    @pl.when(pl.program_id(0) == 0)
    def _():
        carry_ref[...] = jnp.zeros_like(carry_ref)

    t = lgp_ref.shape[1]
    gsz = N_EXPERTS // N_GROUPS
    neg = -jnp.inf
    s = jax.nn.sigmoid(jnp.where(pl.program_id(0) < prompt_tiles, lgp_ref[...], lgs_ref[...]))
    biased = s + bias_ref[...]
    io_g = lax.broadcasted_iota(jnp.int32, (gsz, t), 0)
    grp_rows = []
    for g in range(N_GROUPS):
        blk = biased[g * gsz:(g + 1) * gsz, :]
        m1 = jnp.max(blk, axis=0, keepdims=True)
        i1 = jnp.min(jnp.where(blk == m1, io_g, gsz), axis=0, keepdims=True)
        m2 = jnp.max(jnp.where(io_g == i1, neg, blk), axis=0, keepdims=True)
        grp_rows.append(m1 + m2)
    gs = jnp.concatenate(grp_rows, axis=0)
    io8 = lax.broadcasted_iota(jnp.int32, (N_GROUPS, t), 0)
    gsel = jnp.zeros((N_GROUPS, t), jnp.int32)
    for _ in range(TOPK_GROUPS):
        m = jnp.max(gs, axis=0, keepdims=True)
        gi = jnp.min(jnp.where(gs == m, io8, N_GROUPS), axis=0, keepdims=True)
        hit = io8 == gi
        gsel = jnp.where(hit, 1, gsel)
        gs = jnp.where(hit, neg, gs)
    masked = jnp.concatenate(
        [jnp.where(gsel[g:g + 1, :] > 0, biased[g * gsz:(g + 1) * gsz, :], neg) for g in range(N_GROUPS)], axis=0)

    eio = lax.broadcasted_iota(jnp.int32, (N_EXPERTS, t), 0)
    cur = masked
    idx_rows, w_rows = [], []
    for _ in range(TOP_K):
        m = jnp.max(cur, axis=0, keepdims=True)
        idx = jnp.min(jnp.where(cur == m, eio, N_EXPERTS), axis=0, keepdims=True)
        hit = eio == idx
        w_rows.append(jnp.sum(jnp.where(hit, s, 0.0), axis=0, keepdims=True))
        cur = jnp.where(hit, neg, cur)
        idx_rows.append(idx)
    sel = jnp.where(cur != masked, 1.0, 0.0)

    tri = jnp.where(lax.broadcasted_iota(jnp.int32, (t, t), 0) < lax.broadcasted_iota(jnp.int32, (t, t), 1), 1.0, 0.0)
    pref = jnp.dot(sel.astype(BF16), tri.astype(BF16), preferred_element_type=F32) + carry_ref[...]
    rank_rows = [jnp.sum(jnp.where(eio == idx_rows[k], pref, 0.0), axis=0, keepdims=True) for k in range(TOP_K)]
    carry_ref[...] += jnp.sum(sel, axis=1, keepdims=True)

    wk = jnp.concatenate(w_rows, axis=0)
    eidx_ref[...] = jnp.concatenate(idx_rows, axis=0)
    w_ref[...] = wk / jnp.sum(wk, axis=0, keepdims=True) * ROUTED_SCALE
    rank_ref[...] = jnp.concatenate(rank_rows, axis=0).astype(jnp.int32)
    cnt_ref[...] = carry_ref[...].astype(jnp.int32)


def _route(logits_p, logits_s, bias_col):
    t = ROUTE_TILE
    prompt_tiles = logits_p.shape[1] // t
    n = logits_p.shape[1] + logits_s.shape[1]
    col = lambda i: (0, i)
    fixed = lambda i: (0, 0)
    kern = functools.partial(_route_kernel, prompt_tiles=prompt_tiles)
    return pl.pallas_call(
        kern,
        grid=(n // t,),
        in_specs=[pl.BlockSpec((N_EXPERTS, t), lambda i: (0, jnp.minimum(i, prompt_tiles - 1))),
                  pl.BlockSpec((N_EXPERTS, t), lambda i: (0, jnp.maximum(i - prompt_tiles, 0))),
                  pl.BlockSpec((N_EXPERTS, 1), fixed)],
        out_specs=[
            pl.BlockSpec((TOP_K, t), col),
            pl.BlockSpec((TOP_K, t), col),
            pl.BlockSpec((TOP_K, t), col),
            pl.BlockSpec((N_EXPERTS, 1), fixed),
        ],
        out_shape=[
            jax.ShapeDtypeStruct((TOP_K, n), jnp.int32),
            jax.ShapeDtypeStruct((TOP_K, n), F32),
            jax.ShapeDtypeStruct((TOP_K, n), jnp.int32),
            jax.ShapeDtypeStruct((N_EXPERTS, 1), jnp.int32),
        ],
        scratch_shapes=[pltpu.VMEM((N_EXPERTS, 1), F32)],
        compiler_params=_params(1),
        name="route",
    )(logits_p, logits_s, bias_col)


def _dest_kernel(eidx_ref, rank_ref, start_ref, dest_ref):
    t = eidx_ref.shape[1]
    eio = lax.broadcasted_iota(jnp.int32, (N_EXPERTS, t), 0)
    start = start_ref[...]
    rows = [jnp.sum(jnp.where(eio == eidx_ref[k:k + 1, :], start, 0.0), axis=0, keepdims=True) for k in range(TOP_K)]
    dest_ref[...] = jnp.concatenate(rows, axis=0).astype(jnp.int32) + rank_ref[...]


def _dest(eidx_t, rank_t, pad_start_col):
    n = eidx_t.shape[1]
    t = ROW_TILE
    col = lambda i: (0, i)
    return pl.pallas_call(
        _dest_kernel,
        grid=(n // t,),
        in_specs=[pl.BlockSpec((TOP_K, t), col), pl.BlockSpec((TOP_K, t), col),
                  pl.BlockSpec((N_EXPERTS, 1), lambda i: (0, 0))],
        out_specs=pl.BlockSpec((TOP_K, t), col),
        out_shape=jax.ShapeDtypeStruct((TOP_K, n), jnp.int32),
        compiler_params=_params(1),
        name="dest",
    )(eidx_t, rank_t, pad_start_col)


def _dispatch_kernel(fill_ref, len_ref, nu_ref, dest_ref, pp_ref, ps_ref, hp_ref, hs_ref, wg_ref, wu_ref, wd_ref,
                     xs_ref, yp_ref, zbuf, sem, fill_sem, *, prompt_tiles, n_blocks):
    i = pl.program_id(0)

    @pl.when(i == 0)
    def _():
        zbuf[...] = jnp.zeros_like(zbuf)

        def fill_copy(row0, size):
            return pltpu.make_async_copy(zbuf.at[pl.ds(0, size)], xs_ref.at[pl.ds(row0, size)], fill_sem)

        def on_padding(fn):
            def body(e, c):
                base, length = fill_ref[e], len_ref[e]
                size = EXPERT_ROWS // 2
                while size >= 1:
                    piece = fill_copy(base + (length & ~(2 * size - 1)), size)
                    pl.when((length & size) != 0)(functools.partial(fn, piece))
                    size //= 2
                return c
            lax.fori_loop(0, N_EXPERTS, body, 0)

        def on_unused_blocks(fn):
            lax.fori_loop(nu_ref[0], n_blocks, lambda b, c: (fn(fill_copy(b * EXPERT_ROWS, EXPERT_ROWS)), c)[1], 0)

        on_padding(lambda cp: cp.start())
        on_unused_blocks(lambda cp: cp.start())
        on_padding(lambda cp: cp.wait())
        on_unused_blocks(lambda cp: cp.wait())

    def tile_step(src_ref, h_ref):
        t = src_ref.shape[0]

        def row_copy(j, k):
            return pltpu.make_async_copy(src_ref.at[pl.ds(j, 1)], xs_ref.at[pl.ds(dest_ref[k, j], 1)], sem)

        def issue(j, c):
            for k in range(TOP_K):
                row_copy(j, k).start(priority=k % DMA_PRIORITIES)
            return c

        def drain(j, c):
            for k in range(TOP_K):
                row_copy(j, k).wait()
            return c

        lax.fori_loop(0, t, issue, 0, unroll=DMA_UNROLL)
        h = h_ref[...]
        hb = h.astype(BF16)
        a = jnp.dot(hb, wg_ref[...], preferred_element_type=F32)
        u = jnp.dot(hb, wu_ref[...], preferred_element_type=F32)
        shared = jnp.dot((a * jax.nn.sigmoid(a) * u).astype(BF16), wd_ref[...], preferred_element_type=F32)
        yp_ref[...] = ALPHA * h + shared
        lax.fori_loop(0, t, drain, 0, unroll=DMA_UNROLL)

    @pl.when(i < prompt_tiles)
    def _():
        tile_step(pp_ref, hp_ref)

    @pl.when(i >= prompt_tiles)
    def _():
        tile_step(ps_ref, hs_ref)


def _dispatch(fill_start, fill_len, n_used, dest_t, pk_p, pk_s, h_p, h_s, wg_b, wu_b, wd_b, *, n_blocks):
    t = DISPATCH_ROWS
    prompt_tiles = h_p.shape[0] // t
    sample_tiles = h_s.shape[0] // t
    tile = (t, WORD_SLABS, LANES)
    fixed = lambda i, *_: (0, 0)
    p_idx = lambda i: jnp.minimum(i, prompt_tiles - 1)
    s_idx = lambda i: jnp.maximum(i - prompt_tiles, 0)
    kern = functools.partial(_dispatch_kernel, prompt_tiles=prompt_tiles, n_blocks=n_blocks)
    return pl.pallas_call(
        kern,
        grid_spec=pltpu.PrefetchScalarGridSpec(
            num_scalar_prefetch=3,
            grid=(prompt_tiles + sample_tiles,),
            in_specs=[
                pl.BlockSpec((TOP_K, t), lambda i, *_: (0, i), memory_space=pltpu.SMEM),
                pl.BlockSpec(tile, lambda i, *_: (p_idx(i), 0, 0)),
                pl.BlockSpec(tile, lambda i, *_: (s_idx(i), 0, 0)),
                pl.BlockSpec((t, D_MODEL), lambda i, *_: (p_idx(i), 0)),
                pl.BlockSpec((t, D_MODEL), lambda i, *_: (s_idx(i), 0)),
                pl.BlockSpec((D_MODEL, F_EXPERT), fixed),
                pl.BlockSpec((D_MODEL, F_EXPERT), fixed),
                pl.BlockSpec((F_EXPERT, D_MODEL), fixed),
            ],
            out_specs=[pl.BlockSpec(memory_space=pl.ANY), pl.BlockSpec((t, D_MODEL), lambda i, *_: (i, 0))],
            scratch_shapes=[pltpu.VMEM((EXPERT_ROWS, WORD_SLABS, LANES), jnp.uint32), pltpu.SemaphoreType.DMA,
                            pltpu.SemaphoreType.DMA],
        ),
        out_shape=[jax.ShapeDtypeStruct((n_blocks * EXPERT_ROWS, WORD_SLABS, LANES), jnp.uint32),
                   jax.ShapeDtypeStruct(((prompt_tiles + sample_tiles) * t, D_MODEL), F32)],
        compiler_params=_params(1),
        name="dispatch",
    )(fill_start, fill_len, n_used, dest_t, pk_p, pk_s, h_p, h_s, wg_b, wu_b, wd_b)


def _block_plan(counts):
    padded = (counts + EXPERT_ROWS - 1) // EXPERT_ROWS * EXPERT_ROWS
    pad_end = jnp.cumsum(padded).astype(jnp.int32)
    pad_start = pad_end - padded
    n_used = pad_end[-1] // EXPERT_ROWS
    fill_start = pad_start + counts
    fill_len = pad_end - fill_start
    first_block = pad_start // EXPERT_ROWS
    n_expert_blocks = (padded // EXPERT_ROWS).astype(jnp.int32)
    return pad_start, fill_start, fill_len, first_block, n_expert_blocks, n_used.reshape(1).astype(jnp.int32)


def kernel(x_prompt, x_sample, cache_k, cache_v, w_in, sink, gm_ln_g, gm_ln_b, gm_w_s, gm_b_s, w_out, ln1_g, ln1_b,
           router_w, router_bias, w_gate_e, w_up_e, w_down_e, w_gate_s, w_up_s, w_down_s, ln2_g, ln2_b):
    bp, sp = x_prompt.shape[:2]
    bs, ts = x_sample.shape[:2]
    r = cache_k.shape[2]
    assert r == WINDOW and sp % ROW_TILE == 0 and (bs * ts) % ROW_TILE == 0
    n_p, n_s = bp * sp, bs * ts
    n_total = n_p + n_s
    l = 0

    w_in_b = w_in[l].astype(BF16)
    w_out_b = w_out[l].astype(BF16)
    router_t = router_w[l].T
    r_hi = router_t.astype(BF16)
    r_lo = (router_t - r_hi.astype(F32)).astype(BF16)
    row_vec = lambda v: v.reshape(1, -1)
    gm_g, gm_b = row_vec(gm_ln_g[l]), row_vec(gm_ln_b[l])
    sink_l = sink[l].astype(F32)

    xp2 = x_prompt.reshape(n_p, D_MODEL)
    tabs_p = _rope_tables(jnp.arange(sp, dtype=jnp.int32))
    q, k, v, gu, gv = _in_proj(xp2, w_in_b, tabs_p, gm_g, gm_b, tm=ROW_TILE, gv_dtype=BF16, name="in_proj_prompt")
    nb = sp // WINDOW
    att = _attention(sink_l, q, k, v, k, v, batch=bp, nb=nb, tq=WINDOW, seqs=1, prev_blocks=nb,
                     first_block_has_no_prev=True, name="attn_prompt")
    b_tab_p = jnp.repeat(gm_b_s[l].T, HEAD_DIM, axis=1)
    gm = _gate(gu, gv, gm_w_s[l], b_tab_p, chunk=CHUNK, n_chunks=ROW_TILE // CHUNK, name="gate_prompt")
    last_rows = lambda t: t.reshape(bp, sp, KV_WIDTH)[:, sp - r:, :].reshape(1, bp, r, KV_HEADS, HEAD_DIM)
    new_kp, new_vp = last_rows(k), last_rows(v)
    h_p, pk_p, lt_p = _out_proj(att, gm, xp2, w_out_b, row_vec(ln1_g[l]), row_vec(ln1_b[l]), r_hi, r_lo,
                          name="out_proj_prompt")

    xs2 = x_sample.reshape(n_s, D_MODEL)
    pos_s = PAST_LEN + jnp.arange(ts, dtype=jnp.int32)
    tabs_s = tuple(jnp.tile(t, (bs, 1)) for t in _rope_tables(pos_s))
    q, k, v, gu, gv = _in_proj(xs2, w_in_b, tabs_s, gm_g, gm_b, tm=n_s, gv_dtype=F32, name="in_proj_sample")
    tq = 8
    pad_rows = lambda t: jnp.pad(t.reshape(bs, ts, -1), ((0, 0), (0, tq - ts), (0, 0))).reshape(bs * tq, -1)
    ck2 = cache_k[l].reshape(bs * r, KV_WIDTH)
    cv2 = cache_v[l].reshape(bs * r, KV_WIDTH)
    att = _attention(sink_l, pad_rows(q), pad_rows(k), pad_rows(v), ck2, cv2, batch=bs, nb=1, tq=tq,
                     seqs=SAMPLE_SEQS_PER_STEP, prev_blocks=1, first_block_has_no_prev=False, name="attn_sample")
    att = att.reshape(bs, tq, ATT_WIDTH)[:, :ts].reshape(n_s, ATT_WIDTH)
    gm = _gate_short(gu, gv, gm_w_s[l], gm_b_s[l], length=ts, name="gate_sample")
    new_ks = jnp.concatenate([cache_k[l], k.reshape(bs, ts, KV_HEADS, HEAD_DIM)], axis=1)[:, ts:][None]
    new_vs = jnp.concatenate([cache_v[l], v.reshape(bs, ts, KV_HEADS, HEAD_DIM)], axis=1)[:, ts:][None]
    new_gs = gv.reshape(bs, ts, GM_WIDTH)[None]
    h_s, pk_s, lt_s = _out_proj(att, gm, xs2, w_out_b, row_vec(ln1_g[l]), row_vec(ln1_b[l]), r_hi, r_lo,
                          name="out_proj_sample")

    eidx_t, w_t, rank_t, counts = _route(lt_p, lt_s, router_bias[l].astype(F32).reshape(N_EXPERTS, 1))
    a = n_total * TOP_K
    n_blocks = -(-(a + N_EXPERTS * (EXPERT_ROWS - 1)) // EXPERT_ROWS)
    pad_start, fill_start, fill_len, first_block, n_expert_blocks, n_used = _block_plan(counts.reshape(N_EXPERTS))
    dest_t = _dest(eidx_t, rank_t, pad_start.astype(F32).reshape(N_EXPERTS, 1))
    tiles = lambda a: a.reshape(-1, WORD_SLABS, LANES)
    shared = (w_gate_s[l].astype(BF16), w_up_s[l].astype(BF16), w_down_s[l].astype(BF16))
    x_sorted, y_part = _dispatch(fill_start, fill_len, n_used, dest_t, tiles(pk_p), tiles(pk_s), h_p, h_s, *shared,
                                 n_blocks=n_blocks)
    out_sorted = _experts(first_block, n_expert_blocks, n_used, x_sorted, w_gate_e[l], w_up_e[l], w_down_e[l])
    ln2 = (row_vec(ln2_g[l]), row_vec(ln2_b[l]))
    w2 = w_t.T
    y_p = _combine(dest_t, w2, y_part, *ln2, out_sorted, n=n_p, row_offset=0, name="combine_prompt")
    y_s = _combine(dest_t, w2, y_part, *ln2, out_sorted, n=n_s, row_offset=n_p, name="combine_sample")
    return (y_p.reshape(bp, sp, D_MODEL), y_s.reshape(bs, ts, D_MODEL), new_kp, new_vp, new_ks, new_vs, new_gs)
```

```python
import functools

import jax
import jax.numpy as jnp
import numpy as np
from jax import lax
from jax.experimental import pallas as pl
from jax.experimental.pallas import tpu as pltpu

D_MODEL = 1024
HEAD_DIM = 64
ATT_HEADS = 8
KV_HEADS = 2
Q_PER_KV = ATT_HEADS // KV_HEADS
GM_HEADS = 8
ATT_WIDTH = ATT_HEADS * HEAD_DIM
KV_WIDTH = KV_HEADS * HEAD_DIM
GM_WIDTH = GM_HEADS * HEAD_DIM
ROPE_WIDTH = ATT_WIDTH + KV_WIDTH
IN_WIDTH = ATT_WIDTH + 2 * KV_WIDTH + 2 * GM_WIDTH
WINDOW = 128
CHUNK = 128
PAST_LEN = 16384
ROPE_THETA = 10000.0
ATT_SCALE = HEAD_DIM ** -0.5
N_EXPERTS = 256
TOP_K = 8
N_GROUPS = 8
TOPK_GROUPS = 4
F_EXPERT = 256
ROUTED_SCALE = 2.5
LN_EPS = 1e-5
DEPTH = 1
ALPHA = (2.0 * DEPTH) ** 0.25

LANES = 128
SLABS = 8
PACKED = D_MODEL // 2
WORD_SLABS = PACKED // LANES
ROW_TILE = 512
EXPERT_ROWS = 256
EXPERT_RING = 8
COMBINE_ROWS = 256
COMBINE_GROUP = 4
DISPATCH_ROWS = 256
ROUTE_TILE = 256
SAMPLE_SEQS_PER_STEP = 16
DMA_UNROLL = 4
DMA_PRIORITIES = 2
VMEM_LIMIT = 56 * 1024 * 1024

F32 = jnp.float32
BF16 = jnp.bfloat16


def _params(n_axes):
    return pltpu.CompilerParams(dimension_semantics=("arbitrary",) * n_axes, vmem_limit_bytes=VMEM_LIMIT)


def _layer_norm(x, g, b):
    mu = jnp.mean(x, axis=-1, keepdims=True)
    xc = x - mu
    var = jnp.mean(xc * xc, axis=-1, keepdims=True)
    return xc * lax.rsqrt(var + LN_EPS) * g + b


def _pack_halves(x):
    xb = x.astype(BF16)
    lo = lax.bitcast_convert_type(xb[:, :PACKED].astype(F32), jnp.uint32)
    hi = lax.bitcast_convert_type(xb[:, PACKED:].astype(F32), jnp.uint32)
    return (lo >> 16) | hi


def _unpack_halves(w):
    lo = lax.bitcast_convert_type(w << 16, F32)
    hi = lax.bitcast_convert_type(w & jnp.uint32(0xFFFF0000), F32)
    return lo, hi


def _store_row_tiles(ref2d, x):
    m, slabs = x.shape[0], x.shape[1] // LANES
    for s in range(slabs):
        ref2d[pl.ds(s, m, stride=slabs), :] = x[:, s * LANES:(s + 1) * LANES]


def _load_row_tiles(ref2d, m, slabs):
    return jnp.concatenate([ref2d[pl.ds(s, m, stride=slabs), :] for s in range(slabs)], axis=1)


def _gelu(x):
    return 0.5 * x * (1.0 + lax.erf(x * np.float32(np.sqrt(0.5))))


def _in_proj_kernel(x_ref, w_ref, cos_ref, sa_ref, sb_ref, g_ref, b_ref, q_ref, k_ref, v_ref, gu_ref, gv_ref):
    x = x_ref[...].astype(BF16)
    zr = jnp.dot(x, w_ref[:, :ROPE_WIDTH], preferred_element_type=F32)
    pieces = []
    for c in range(ROPE_WIDTH // LANES):
        zc = zr[:, c * LANES:(c + 1) * LANES]
        tl = slice(0, LANES) if c < ATT_WIDTH // LANES else slice(LANES, 2 * LANES)
        pieces.append(zc * cos_ref[:, tl]
                      + pltpu.roll(zc, LANES - HEAD_DIM // 2, 1) * sa_ref[:, tl]
                      + pltpu.roll(zc, HEAD_DIM // 2, 1) * sb_ref[:, tl])
    for c in range(ATT_WIDTH // LANES):
        q_ref[:, c * LANES:(c + 1) * LANES] = pieces[c].astype(q_ref.dtype)
    k_ref[...] = pieces[ATT_WIDTH // LANES]
    v_ref[...] = jnp.dot(x, w_ref[:, ROPE_WIDTH:ROPE_WIDTH + KV_WIDTH], preferred_element_type=F32)
    g0 = ROPE_WIDTH + KV_WIDTH
    zu = jnp.dot(x, w_ref[:, g0:g0 + GM_WIDTH], preferred_element_type=F32)
    gu_ref[...] = _gelu(zu).astype(gu_ref.dtype)
    zv = jnp.dot(x, w_ref[:, g0 + GM_WIDTH:g0 + 2 * GM_WIDTH], preferred_element_type=F32)
    gv = _layer_norm(_gelu(zv), g_ref[...], b_ref[...])
    gv_ref[...] = gv.astype(gv_ref.dtype)


def _in_proj(x2, w_in_b, tabs, ln_g, ln_b, *, tm, gv_dtype, name):
    n = x2.shape[0]
    cos_t, sa_t, sb_t = tabs
    period = cos_t.shape[0] // tm
    row = lambda i: (i, 0)
    tab = lambda i: (i % period, 0)
    fixed = lambda i: (0, 0)
    return pl.pallas_call(
        _in_proj_kernel,
        grid=(n // tm,),
        in_specs=[
            pl.BlockSpec((tm, D_MODEL), row),
            pl.BlockSpec((D_MODEL, IN_WIDTH), fixed),
            pl.BlockSpec((tm, 2 * LANES), tab),
            pl.BlockSpec((tm, 2 * LANES), tab),
            pl.BlockSpec((tm, 2 * LANES), tab),
            pl.BlockSpec((1, GM_WIDTH), fixed),
            pl.BlockSpec((1, GM_WIDTH), fixed),
        ],
        out_specs=[
            pl.BlockSpec((tm, ATT_WIDTH), row),
            pl.BlockSpec((tm, KV_WIDTH), row),
            pl.BlockSpec((tm, KV_WIDTH), row),
            pl.BlockSpec((tm, GM_WIDTH), row),
            pl.BlockSpec((tm, GM_WIDTH), row),
        ],
        out_shape=[
            jax.ShapeDtypeStruct((n, ATT_WIDTH), BF16),
            jax.ShapeDtypeStruct((n, KV_WIDTH), F32),
            jax.ShapeDtypeStruct((n, KV_WIDTH), F32),
            jax.ShapeDtypeStruct((n, GM_WIDTH), BF16),
            jax.ShapeDtypeStruct((n, GM_WIDTH), gv_dtype),
        ],
        compiler_params=_params(1),
        name=name,
    )(x2, w_in_b, cos_t, sa_t, sb_t, ln_g, ln_b)


def _rope_tables(pos):
    half = HEAD_DIM // 2
    lane = jnp.arange(2 * LANES, dtype=jnp.int32)
    inv = ROPE_THETA ** (-(lane % half).astype(F32) * 2.0 / HEAD_DIM)
    ang = pos.astype(F32)[:, None] * inv[None, :]
    scale = jnp.where(lane < LANES, ATT_SCALE, 1.0).astype(F32)[None, :]
    first_half = ((lane % HEAD_DIM) < half)[None, :]
    cos, sin = jnp.cos(ang) * scale, jnp.sin(ang) * scale
    return cos, jnp.where(first_half, -sin, 0.0), jnp.where(first_half, 0.0, sin)


def _attn_kernel(sink_ref, q_ref, kc_ref, vc_ref, kp_ref, vp_ref, o_ref, *, tq, seqs, stack, first_block_has_no_prev):
    nk = WINDOW + tq
    rows = stack * tq
    qi = lax.broadcasted_iota(jnp.int32, (rows, nk), 0) & (tq - 1)
    ks = lax.broadcasted_iota(jnp.int32, (rows, nk), 1)
    mask = (ks >= qi) & (ks <= qi + WINDOW)
    if first_block_has_no_prev:
        mask = mask & ((pl.program_id(1) > 0) | (ks >= WINDOW))
    sinks = [jnp.concatenate([jnp.full((tq, 1), sink_ref[h0 + j], F32) for j in range(stack)], axis=0)
             for h0 in range(0, ATT_HEADS, stack)]
    for b in range(seqs):
        qrows = slice(b * tq, (b + 1) * tq)
        prows = slice(b * WINDOW, (b + 1) * WINDOW)
        q = q_ref[qrows, :]
        kk = jnp.concatenate([kp_ref[prows, :], kc_ref[qrows, :]], axis=0).astype(BF16)
        vv = jnp.concatenate([vp_ref[prows, :], vc_ref[qrows, :]], axis=0).astype(BF16)
        outs = []
        for i, h0 in enumerate(range(0, ATT_HEADS, stack)):
            g = h0 // Q_PER_KV
            kg = kk[:, g * HEAD_DIM:(g + 1) * HEAD_DIM]
            vg = vv[:, g * HEAD_DIM:(g + 1) * HEAD_DIM]
            qg = jnp.concatenate([q[:, (h0 + j) * HEAD_DIM:(h0 + j + 1) * HEAD_DIM] for j in range(stack)], axis=0)
            s = lax.dot_general(qg, kg, (((1,), (1,)), ((), ())), preferred_element_type=F32)
            s = jnp.where(mask, s, -jnp.inf)
            m = jnp.maximum(jnp.max(s, axis=-1, keepdims=True), sinks[i])
            p = jnp.exp(s - m)
            denom = jnp.sum(p, axis=-1, keepdims=True) + jnp.exp(sinks[i] - m)
            o = jnp.dot((p / denom).astype(BF16), vg, preferred_element_type=F32)
            outs.extend(o[j * tq:(j + 1) * tq, :] for j in range(stack))
        o_ref[qrows, :] = jnp.concatenate(outs, axis=1).astype(o_ref.dtype)


def _attention(sink, q2, k2, v2, kprev2, vprev2, *, batch, nb, tq, seqs, prev_blocks, first_block_has_no_prev, name):
    assert tq & (tq - 1) == 0 and batch % seqs == 0 and (seqs == 1 or nb == prev_blocks == 1)
    cur = lambda b, n, s: (b * nb + n, 0)
    prev = lambda b, n, s: (b * prev_blocks + jnp.maximum(n - 1, 0), 0)
    stack = Q_PER_KV if Q_PER_KV * tq <= WINDOW else 1
    kern = functools.partial(_attn_kernel, tq=tq, seqs=seqs, stack=stack,
                             first_block_has_no_prev=first_block_has_no_prev)
    return pl.pallas_call(
        kern,
        grid_spec=pltpu.PrefetchScalarGridSpec(
            num_scalar_prefetch=1,
            grid=(batch // seqs, nb),
            in_specs=[
                pl.BlockSpec((seqs * tq, ATT_WIDTH), cur),
                pl.BlockSpec((seqs * tq, KV_WIDTH), cur),
                pl.BlockSpec((seqs * tq, KV_WIDTH), cur),
                pl.BlockSpec((seqs * WINDOW, KV_WIDTH), prev),
                pl.BlockSpec((seqs * WINDOW, KV_WIDTH), prev),
            ],
            out_specs=pl.BlockSpec((seqs * tq, ATT_WIDTH), cur),
        ),
        out_shape=jax.ShapeDtypeStruct(q2.shape, BF16),
        compiler_params=_params(2),
        name=name,
    )(sink, q2, k2, v2, kprev2, vprev2)


def _gate_kernel(gu_ref, gv_ref, w_ref, b_ref, o_ref, *, chunk, n_chunks):
    ri = lax.broadcasted_iota(jnp.int32, (chunk, chunk), 0)
    ci = lax.broadcasted_iota(jnp.int32, (chunk, chunk), 1)
    ws = [jnp.where(ci <= ri, w_ref[h], 0.0).astype(BF16) for h in range(GM_HEADS)]
    for c in range(n_chunks):
        rows = slice(c * chunk, (c + 1) * chunk)
        gv = gv_ref[rows, :].astype(BF16)
        sv = jnp.concatenate(
            [jnp.dot(ws[h], gv[:, h * HEAD_DIM:(h + 1) * HEAD_DIM], preferred_element_type=F32)
             for h in range(GM_HEADS)], axis=1)
        o_ref[rows, :] = (gu_ref[rows, :].astype(F32) * (sv + b_ref[...])).astype(o_ref.dtype)


def _gate(gu2, gv2, w_s, b_tab, *, chunk, n_chunks, name):
    n = gu2.shape[0]
    tm = chunk * n_chunks
    row = lambda i: (i, 0)
    kern = functools.partial(_gate_kernel, chunk=chunk, n_chunks=n_chunks)
    return pl.pallas_call(
        kern,
        grid=(n // tm,),
        in_specs=[
            pl.BlockSpec((tm, GM_WIDTH), row),
            pl.BlockSpec((tm, GM_WIDTH), row),
            pl.BlockSpec((GM_HEADS, chunk, chunk), lambda i: (0, 0, 0)),
            pl.BlockSpec((chunk, GM_WIDTH), lambda i: (0, 0)),
        ],
        out_specs=pl.BlockSpec((tm, GM_WIDTH), row),
        out_shape=jax.ShapeDtypeStruct((n, GM_WIDTH), BF16),
        compiler_params=_params(1),
        name=name,
    )(gu2, gv2, w_s, b_tab)


def _gate_short_kernel(gu_ref, gv_ref, coef_ref, b_ref, o_ref, *, length):
    n = gv_ref.shape[0]
    gv = gv_ref[...].astype(F32)
    tiles = lambda x: x.reshape(n // SLABS, SLABS, GM_WIDTH)
    sv = tiles(gv) * coef_ref[0][None] + b_ref[...][None]
    for d in range(1, length):
        sv = sv + tiles(pltpu.roll(gv, d, 0)) * coef_ref[d][None]
    o_ref[...] = (gu_ref[...].astype(F32) * sv.reshape(n, GM_WIDTH)).astype(o_ref.dtype)


def _gate_short(gu2, gv2, w_s, b_s, *, length, name):
    n = gu2.shape[0]
    assert SLABS % length == 0 and n % SLABS == 0
    pos = jnp.arange(length)
    lag = jnp.arange(length)
    src = pos[None, :] - lag[:, None]
    coef = jnp.where(src >= 0, w_s[:, pos[None, :], jnp.maximum(src, 0)], 0.0)
    coef = jnp.repeat(coef.transpose(1, 2, 0), HEAD_DIM, axis=2)
    coef = jnp.tile(coef, (1, SLABS // length, 1))
    bias = jnp.tile(jnp.repeat(b_s[:, :length].T, HEAD_DIM, axis=1), (SLABS // length, 1))
    whole = lambda shape: pl.BlockSpec(shape, lambda i: (0,) * len(shape))
    return pl.pallas_call(
        functools.partial(_gate_short_kernel, length=length),
        grid=(1,),
        in_specs=[whole((n, GM_WIDTH)), whole((n, GM_WIDTH)), whole((length, SLABS, GM_WIDTH)),
                  whole((SLABS, GM_WIDTH))],
        out_specs=whole((n, GM_WIDTH)),
        out_shape=jax.ShapeDtypeStruct((n, GM_WIDTH), BF16),
        compiler_params=_params(1),
        name=name,
    )(gu2, gv2, coef, bias)


def _out_proj_kernel(att_ref, gm_ref, x_ref, wo_ref, g_ref, b_ref, rh_ref, rl_ref, h_ref, hp_ref, lg_ref):
    mix = jnp.dot(att_ref[...], wo_ref[:ATT_WIDTH, :], preferred_element_type=F32)
    mix = mix + jnp.dot(gm_ref[...], wo_ref[ATT_WIDTH:, :], preferred_element_type=F32)
    h = _layer_norm(ALPHA * x_ref[...] + mix, g_ref[...], b_ref[...])
    h_ref[...] = h
    _store_row_tiles(hp_ref, _pack_halves(h))
    h_hi = h.astype(BF16)
    h_lo = (h - h_hi.astype(F32)).astype(BF16)
    nt = (((1,), (1,)), ((), ()))
    lg = lax.dot_general(rh_ref[...], h_hi, nt, preferred_element_type=F32)
    lg = lg + lax.dot_general(rh_ref[...], h_lo, nt, preferred_element_type=F32)
    lg = lg + lax.dot_general(rl_ref[...], h_hi, nt, preferred_element_type=F32)
    lg_ref[...] = lg


def _out_proj(att2, gm2, x2, w_out_b, ln_g, ln_b, r_hi, r_lo, *, name):
    n = x2.shape[0]
    tm = ROW_TILE
    row = lambda i: (i, 0)
    fixed = lambda i: (0, 0)
    return pl.pallas_call(
        _out_proj_kernel,
        grid=(n // tm,),
        in_specs=[
            pl.BlockSpec((tm, ATT_WIDTH), row),
            pl.BlockSpec((tm, GM_WIDTH), row),
            pl.BlockSpec((tm, D_MODEL), row),
            pl.BlockSpec((D_MODEL, D_MODEL), fixed),
            pl.BlockSpec((1, D_MODEL), fixed),
            pl.BlockSpec((1, D_MODEL), fixed),
            pl.BlockSpec((N_EXPERTS, D_MODEL), fixed),
            pl.BlockSpec((N_EXPERTS, D_MODEL), fixed),
        ],
        out_specs=[
            pl.BlockSpec((tm, D_MODEL), row),
            pl.BlockSpec((tm * WORD_SLABS, LANES), row),
            pl.BlockSpec((N_EXPERTS, tm), lambda i: (0, i)),
        ],
        out_shape=[
            jax.ShapeDtypeStruct((n, D_MODEL), F32),
            jax.ShapeDtypeStruct((n * WORD_SLABS, LANES), jnp.uint32),
            jax.ShapeDtypeStruct((N_EXPERTS, n), F32),
        ],
        compiler_params=_params(1),
        name=name,
    )(att2, gm2, x2, w_out_b, ln_g, ln_b, r_hi, r_lo)


def _experts_kernel(first_ref, nblk_ref, nu_ref, x_hbm, wg_ref, wu_ref, wd_ref, o_hbm, xbuf, obuf, wg_s, wu_s, wd_s,
                    in_sem, out_sem, fill_sem, *, n_blocks):
    e = pl.program_id(0)
    nb = nblk_ref[e]
    b0 = first_ref[e]
    n_used = nu_ref[0]
    ahead = EXPERT_RING // 2

    def rows_of(block):
        return pl.ds(pl.multiple_of(block * EXPERT_ROWS, EXPERT_ROWS), EXPERT_ROWS)

    def slot_of(block):
        return block & (EXPERT_RING - 1)

    def slab(s):
        return pl.ds(s * LANES, LANES)

    def in_copies(block):
        slot = slot_of(block)
        return [pltpu.make_async_copy(x_hbm.at[rows_of(block), s, :], xbuf.at[slot, :, slab(s)], in_sem.at[slot])
                for s in range(WORD_SLABS)]

    def out_copies(block, slot=None, sem=None):
        slot = slot_of(block) if slot is None else slot
        sem = out_sem.at[slot] if sem is None else sem
        return [pltpu.make_async_copy(obuf.at[slot, :, slab(s)], o_hbm.at[rows_of(block), s, :], sem)
                for s in range(WORD_SLABS)]

    def start(copies):
        for cp in copies:
            cp.start()

    def wait(copies):
        for cp in copies:
            cp.wait()

    @pl.when(e == 0)
    def _():
        for j in range(ahead):
            pl.when(j < n_used)(lambda j=j: start(in_copies(j)))

    def process(block, count):
        blocks = [block + j for j in range(count)]
        for blk in blocks:
            wait(in_copies(blk))
        x = [xbuf[slot_of(blk)] for blk in blocks]
        xl, xh = _unpack_halves(x[0] if count == 1 else jnp.concatenate(x, axis=0))
        xl, xh = xl.astype(BF16), xh.astype(BF16)
        for blk in blocks:
            pl.when(blk + ahead < n_used)(lambda blk=blk: start(in_copies(blk + ahead)))
            pl.when(blk >= ahead)(lambda blk=blk: wait(out_copies(blk - ahead)))
        a = (jnp.dot(xl, wg_s[:PACKED, :], preferred_element_type=F32)
             + jnp.dot(xh, wg_s[PACKED:, :], preferred_element_type=F32))
        u = (jnp.dot(xl, wu_s[:PACKED, :], preferred_element_type=F32)
             + jnp.dot(xh, wu_s[PACKED:, :], preferred_element_type=F32))
        hb = (a * jax.nn.sigmoid(a) * u).astype(BF16)
        o = _pack_halves(jnp.dot(hb, wd_s[...], preferred_element_type=F32))
        for j, blk in enumerate(blocks):
            obuf[slot_of(blk)] = o[j * EXPERT_ROWS:(j + 1) * EXPERT_ROWS, :]
            start(out_copies(blk))

    @pl.when(nb > 0)
    def _():
        wg_s[...] = wg_ref[0].astype(BF16)
        wu_s[...] = wu_ref[0].astype(BF16)
        wd_s[...] = wd_ref[0].astype(BF16)

        def pair(i, carry):
            process(b0 + 2 * i, 2)
            return carry

        lax.fori_loop(0, nb // 2, pair, 0)
        pl.when(nb % 2 == 1)(lambda: process(b0 + nb - 1, 1))

    @pl.when(e == N_EXPERTS - 1)
    def _():
        for j in range(ahead):
            pl.when(n_used - 1 - j >= 0)(lambda j=j: wait(out_copies(n_used - 1 - j)))
        obuf[0] = jnp.zeros((EXPERT_ROWS, PACKED), jnp.uint32)

        def on_unused_blocks(fn):
            def body(b, c):
                fn(out_copies(b, slot=0, sem=fill_sem))
                return c
            lax.fori_loop(n_used, n_blocks, body, 0)

        on_unused_blocks(start)
        on_unused_blocks(wait)


def _experts(first_block, n_expert_blocks, n_used, x_sorted, w_gate_e, w_up_e, w_down_e):
    rows = x_sorted.shape[0]
    wmap = lambda e, *_: (e, 0, 0)
    kern = functools.partial(_experts_kernel, n_blocks=rows // EXPERT_ROWS)
    return pl.pallas_call(
        kern,
        grid_spec=pltpu.PrefetchScalarGridSpec(
            num_scalar_prefetch=3,
            grid=(N_EXPERTS,),
            in_specs=[
                pl.BlockSpec(memory_space=pl.ANY),
                pl.BlockSpec((1, D_MODEL, F_EXPERT), wmap),
                pl.BlockSpec((1, D_MODEL, F_EXPERT), wmap),
                pl.BlockSpec((1, F_EXPERT, D_MODEL), wmap),
            ],
            out_specs=pl.BlockSpec(memory_space=pl.ANY),
            scratch_shapes=[
                pltpu.VMEM((EXPERT_RING, EXPERT_ROWS, PACKED), jnp.uint32),
                pltpu.VMEM((EXPERT_RING, EXPERT_ROWS, PACKED), jnp.uint32),
                pltpu.VMEM((D_MODEL, F_EXPERT), BF16),
                pltpu.VMEM((D_MODEL, F_EXPERT), BF16),
                pltpu.VMEM((F_EXPERT, D_MODEL), BF16),
                pltpu.SemaphoreType.DMA((EXPERT_RING,)),
                pltpu.SemaphoreType.DMA((EXPERT_RING,)),
                pltpu.SemaphoreType.DMA,
            ],
        ),
        out_shape=jax.ShapeDtypeStruct(x_sorted.shape, x_sorted.dtype),
        compiler_params=_params(1),
        name="experts",
    )(first_block, n_expert_blocks, n_used, x_sorted, w_gate_e, w_up_e, w_down_e)


def _combine_kernel(d_ref, w_ref, yp_ref, ln_g_ref, ln_b_ref, os_ref, y_ref, gbuf0, gbuf1, rlo, rhi, sem):
    s = pl.program_id(0)
    n_tiles = pl.num_programs(0) - 1
    t = y_ref.shape[0]
    gtiles = [g.reshape(t * TOP_K, WORD_SLABS, LANES) for g in (gbuf0, gbuf1)]
    lo_tiles = rlo.reshape(t, WORD_SLABS, LANES)
    hi_tiles = rhi.reshape(t, WORD_SLABS, LANES)

    def row_copy(row, buf, j, k):
        return pltpu.make_async_copy(os_ref.at[pl.ds(row, 1)], gtiles[buf].at[pl.ds(j * TOP_K + k, 1)], sem.at[buf])

    def drain(buf):
        def body(j, c):
            for k in range(TOP_K):
                row_copy(0, buf, j, k).wait()
            return c
        lax.fori_loop(0, t, body, 0, unroll=DMA_UNROLL)

    def weighted_sum(rows, j):
        lo, hi = _unpack_halves(rows)
        acc_lo = w_ref[j, 0] * lo[0:1]
        acc_hi = w_ref[j, 0] * hi[0:1]
        for k in range(1, TOP_K):
            acc_lo = acc_lo + w_ref[j, k] * lo[k:k + 1]
            acc_hi = acc_hi + w_ref[j, k] * hi[k:k + 1]
        lo_tiles[pl.ds(j, 1)] = acc_lo
        hi_tiles[pl.ds(j, 1)] = acc_hi

    def token_loop(new, old, start_new, sum_old):
        def body(g, c):
            tokens = [g * COMBINE_GROUP + u for u in range(COMBINE_GROUP)]
            rows = [gtiles[old][pl.ds(j * TOP_K, TOP_K)] for j in tokens] if sum_old else []
            if start_new:
                for j in tokens:
                    for k in range(TOP_K):
                        row_copy(d_ref[j, k], new, j, k).start(priority=k % DMA_PRIORITIES)
            for r, j in zip(rows, tokens):
                weighted_sum(r, j)
            return c
        lax.fori_loop(0, t // COMBINE_GROUP, body, 0)

    def step(new):
        old = 1 - new
        pl.when(s == 0)(lambda: token_loop(new, old, True, False))

        @pl.when(s >= 1)
        def _():
            drain(old)
            pl.when(s < n_tiles)(lambda: token_loop(new, old, True, True))
            pl.when(s == n_tiles)(lambda: token_loop(new, old, False, True))
            routed = jnp.concatenate([_load_row_tiles(rlo, t, WORD_SLABS), _load_row_tiles(rhi, t, WORD_SLABS)],
                                     axis=1)
            y_ref[...] = _layer_norm(yp_ref[...] + routed, ln_g_ref[...], ln_b_ref[...])

    pl.when(s % 2 == 0)(lambda: step(0))
    pl.when(s % 2 == 1)(lambda: step(1))


def _combine(dest2, w2, y_part, ln_g, ln_b, out_sorted, *, n, row_offset, name):
    tm = COMBINE_ROWS
    off = row_offset // tm
    n_tiles = n // tm
    fixed = lambda s: (0, 0)
    new_tile = lambda s: (jnp.minimum(s, n_tiles - 1) + off, 0)
    old_tile = lambda s: (jnp.maximum(s - 1, 0) + off, 0)
    return pl.pallas_call(
        _combine_kernel,
        grid=(n_tiles + 1,),
        in_specs=[
            pl.BlockSpec((tm, TOP_K), new_tile, memory_space=pltpu.SMEM),
            pl.BlockSpec((tm, TOP_K), old_tile, memory_space=pltpu.SMEM),
            pl.BlockSpec((tm, D_MODEL), old_tile),
            pl.BlockSpec((1, D_MODEL), fixed),
            pl.BlockSpec((1, D_MODEL), fixed),
            pl.BlockSpec(memory_space=pl.ANY),
        ],
        out_specs=pl.BlockSpec((tm, D_MODEL), lambda s: (jnp.maximum(s - 1, 0), 0)),
        out_shape=jax.ShapeDtypeStruct((n, D_MODEL), F32),
        scratch_shapes=[pltpu.VMEM((TOP_K * tm * WORD_SLABS, LANES), jnp.uint32),
                        pltpu.VMEM((TOP_K * tm * WORD_SLABS, LANES), jnp.uint32),
                        pltpu.VMEM((tm * WORD_SLABS, LANES), F32), pltpu.VMEM((tm * WORD_SLABS, LANES), F32),
                        pltpu.SemaphoreType.DMA((2,))],
        compiler_params=_params(1),
        name=name,
    )(dest2, w2, y_part, ln_g, ln_b, out_sorted)


def _route_kernel(lgp_ref, lgs_ref, bias_ref, eidx_ref, w_ref, rank_ref, cnt_ref, carry_ref, *, prompt_tiles):
    @pl.when(pl.program_id(0) == 0)
    def _():
        carry_ref[...] = jnp.zeros_like(carry_ref)

    t = lgp_ref.shape[1]
    gsz = N_EXPERTS // N_GROUPS
    neg = -jnp.inf
    s = jax.nn.sigmoid(jnp.where(pl.program_id(0) < prompt_tiles, lgp_ref[...], lgs_ref[...]))
    biased = s + bias_ref[...]
    io_g = lax.broadcasted_iota(jnp.int32, (gsz, t), 0)
    grp_rows = []
    for g in range(N_GROUPS):
        blk = biased[g * gsz:(g + 1) * gsz, :]
        m1 = jnp.max(blk, axis=0, keepdims=True)
        i1 = jnp.min(jnp.where(blk == m1, io_g, gsz), axis=0, keepdims=True)
        m2 = jnp.max(jnp.where(io_g == i1, neg, blk), axis=0, keepdims=True)
        grp_rows.append(m1 + m2)
    gs = jnp.concatenate(grp_rows, axis=0)
    io8 = lax.broadcasted_iota(jnp.int32, (N_GROUPS, t), 0)
    gsel = jnp.zeros((N_GROUPS, t), jnp.int32)
    for _ in range(TOPK_GROUPS):
        m = jnp.max(gs, axis=0, keepdims=True)
        gi = jnp.min(jnp.where(gs == m, io8, N_GROUPS), axis=0, keepdims=True)
        hit = io8 == gi
        gsel = jnp.where(hit, 1, gsel)
        gs = jnp.where(hit, neg, gs)
    masked = jnp.concatenate(
        [jnp.where(gsel[g:g + 1, :] > 0, biased[g * gsz:(g + 1) * gsz, :], neg) for g in range(N_GROUPS)], axis=0)

    eio = lax.broadcasted_iota(jnp.int32, (N_EXPERTS, t), 0)
    cur = masked
    idx_rows, w_rows = [], []
    for _ in range(TOP_K):
        m = jnp.max(cur, axis=0, keepdims=True)
        idx = jnp.min(jnp.where(cur == m, eio, N_EXPERTS), axis=0, keepdims=True)
        hit = eio == idx
        w_rows.append(jnp.sum(jnp.where(hit, s, 0.0), axis=0, keepdims=True))
        cur = jnp.where(hit, neg, cur)
        idx_rows.append(idx)
    sel = jnp.where(cur != masked, 1.0, 0.0)

    tri = jnp.where(lax.broadcasted_iota(jnp.int32, (t, t), 0) < lax.broadcasted_iota(jnp.int32, (t, t), 1), 1.0, 0.0)
    pref = jnp.dot(sel.astype(BF16), tri.astype(BF16), preferred_element_type=F32) + carry_ref[...]
    rank_rows = [jnp.sum(jnp.where(eio == idx_rows[k], pref, 0.0), axis=0, keepdims=True) for k in range(TOP_K)]
    carry_ref[...] += jnp.sum(sel, axis=1, keepdims=True)

    wk = jnp.concatenate(w_rows, axis=0)
    eidx_ref[...] = jnp.concatenate(idx_rows, axis=0)
    w_ref[...] = wk / jnp.sum(wk, axis=0, keepdims=True) * ROUTED_SCALE
    rank_ref[...] = jnp.concatenate(rank_rows, axis=0).astype(jnp.int32)
    cnt_ref[...] = carry_ref[...].astype(jnp.int32)


def _route(logits_p, logits_s, bias_col):
    t = ROUTE_TILE
    prompt_tiles = logits_p.shape[1] // t
    n = logits_p.shape[1] + logits_s.shape[1]
    col = lambda i: (0, i)
    fixed = lambda i: (0, 0)
    kern = functools.partial(_route_kernel, prompt_tiles=prompt_tiles)
    return pl.pallas_call(
        kern,
        grid=(n // t,),
        in_specs=[pl.BlockSpec((N_EXPERTS, t), lambda i: (0, jnp.minimum(i, prompt_tiles - 1))),
                  pl.BlockSpec((N_EXPERTS, t), lambda i: (0, jnp.maximum(i - prompt_tiles, 0))),
                  pl.BlockSpec((N_EXPERTS, 1), fixed)],
        out_specs=[
            pl.BlockSpec((TOP_K, t), col),
            pl.BlockSpec((TOP_K, t), col),
            pl.BlockSpec((TOP_K, t), col),
            pl.BlockSpec((N_EXPERTS, 1), fixed),
        ],
        out_shape=[
            jax.ShapeDtypeStruct((TOP_K, n), jnp.int32),
            jax.ShapeDtypeStruct((TOP_K, n), F32),
            jax.ShapeDtypeStruct((TOP_K, n), jnp.int32),
            jax.ShapeDtypeStruct((N_EXPERTS, 1), jnp.int32),
        ],
        scratch_shapes=[pltpu.VMEM((N_EXPERTS, 1), F32)],
        compiler_params=_params(1),
        name="route",
    )(logits_p, logits_s, bias_col)


def _dest_kernel(eidx_ref, rank_ref, start_ref, dest_ref):
    t = eidx_ref.shape[1]
    eio = lax.broadcasted_iota(jnp.int32, (N_EXPERTS, t), 0)
    start = start_ref[...]
    rows = [jnp.sum(jnp.where(eio == eidx_ref[k:k + 1, :], start, 0.0), axis=0, keepdims=True) for k in range(TOP_K)]
    dest_ref[...] = jnp.concatenate(rows, axis=0).astype(jnp.int32) + rank_ref[...]


def _dest(eidx_t, rank_t, pad_start_col):
    n = eidx_t.shape[1]
    t = ROW_TILE
    col = lambda i: (0, i)
    return pl.pallas_call(
        _dest_kernel,
        grid=(n // t,),
        in_specs=[pl.BlockSpec((TOP_K, t), col), pl.BlockSpec((TOP_K, t), col),
                  pl.BlockSpec((N_EXPERTS, 1), lambda i: (0, 0))],
        out_specs=pl.BlockSpec((TOP_K, t), col),
        out_shape=jax.ShapeDtypeStruct((TOP_K, n), jnp.int32),
        compiler_params=_params(1),
        name="dest",
    )(eidx_t, rank_t, pad_start_col)


def _dispatch_kernel(fill_ref, len_ref, nu_ref, dest_ref, pp_ref, ps_ref, hp_ref, hs_ref, wg_ref, wu_ref, wd_ref,
                     xs_ref, yp_ref, zbuf, sem, fill_sem, *, prompt_tiles, n_blocks):
    i = pl.program_id(0)

    @pl.when(i == 0)
    def _():
        zbuf[...] = jnp.zeros_like(zbuf)

        def fill_copy(row0, size):
            return pltpu.make_async_copy(zbuf.at[pl.ds(0, size)], xs_ref.at[pl.ds(row0, size)], fill_sem)

        def on_padding(fn):
            def body(e, c):
                base, length = fill_ref[e], len_ref[e]
                size = EXPERT_ROWS // 2
                while size >= 1:
                    piece = fill_copy(base + (length & ~(2 * size - 1)), size)
                    pl.when((length & size) != 0)(functools.partial(fn, piece))
                    size //= 2
                return c
            lax.fori_loop(0, N_EXPERTS, body, 0)

        def on_unused_blocks(fn):
            lax.fori_loop(nu_ref[0], n_blocks, lambda b, c: (fn(fill_copy(b * EXPERT_ROWS, EXPERT_ROWS)), c)[1], 0)

        on_padding(lambda cp: cp.start())
        on_unused_blocks(lambda cp: cp.start())
        on_padding(lambda cp: cp.wait())
        on_unused_blocks(lambda cp: cp.wait())

    def tile_step(src_ref, h_ref):
        t = src_ref.shape[0]

        def row_copy(j, k):
            return pltpu.make_async_copy(src_ref.at[pl.ds(j, 1)], xs_ref.at[pl.ds(dest_ref[k, j], 1)], sem)

        def issue(j, c):
            for k in range(TOP_K):
                row_copy(j, k).start(priority=k % DMA_PRIORITIES)
            return c

        def drain(j, c):
            for k in range(TOP_K):
                row_copy(j, k).wait()
            return c

        lax.fori_loop(0, t, issue, 0, unroll=DMA_UNROLL)
        h = h_ref[...]
        hb = h.astype(BF16)
        a = jnp.dot(hb, wg_ref[...], preferred_element_type=F32)
        u = jnp.dot(hb, wu_ref[...], preferred_element_type=F32)
        shared = jnp.dot((a * jax.nn.sigmoid(a) * u).astype(BF16), wd_ref[...], preferred_element_type=F32)
        yp_ref[...] = ALPHA * h + shared
        lax.fori_loop(0, t, drain, 0, unroll=DMA_UNROLL)

    @pl.when(i < prompt_tiles)
    def _():
        tile_step(pp_ref, hp_ref)

    @pl.when(i >= prompt_tiles)
    def _():
        tile_step(ps_ref, hs_ref)


def _dispatch(fill_start, fill_len, n_used, dest_t, pk_p, pk_s, h_p, h_s, wg_b, wu_b, wd_b, *, n_blocks):
    t = DISPATCH_ROWS
    prompt_tiles = h_p.shape[0] // t
    sample_tiles = h_s.shape[0] // t
    tile = (t, WORD_SLABS, LANES)
    fixed = lambda i, *_: (0, 0)
    p_idx = lambda i: jnp.minimum(i, prompt_tiles - 1)
    s_idx = lambda i: jnp.maximum(i - prompt_tiles, 0)
    kern = functools.partial(_dispatch_kernel, prompt_tiles=prompt_tiles, n_blocks=n_blocks)
    return pl.pallas_call(
        kern,
        grid_spec=pltpu.PrefetchScalarGridSpec(
            num_scalar_prefetch=3,
            grid=(prompt_tiles + sample_tiles,),
            in_specs=[
                pl.BlockSpec((TOP_K, t), lambda i, *_: (0, i), memory_space=pltpu.SMEM),
                pl.BlockSpec(tile, lambda i, *_: (p_idx(i), 0, 0)),
                pl.BlockSpec(tile, lambda i, *_: (s_idx(i), 0, 0)),
                pl.BlockSpec((t, D_MODEL), lambda i, *_: (p_idx(i), 0)),
                pl.BlockSpec((t, D_MODEL), lambda i, *_: (s_idx(i), 0)),
                pl.BlockSpec((D_MODEL, F_EXPERT), fixed),
                pl.BlockSpec((D_MODEL, F_EXPERT), fixed),
                pl.BlockSpec((F_EXPERT, D_MODEL), fixed),
            ],
            out_specs=[pl.BlockSpec(memory_space=pl.ANY), pl.BlockSpec((t, D_MODEL), lambda i, *_: (i, 0))],
            scratch_shapes=[pltpu.VMEM((EXPERT_ROWS, WORD_SLABS, LANES), jnp.uint32), pltpu.SemaphoreType.DMA,
                            pltpu.SemaphoreType.DMA],
        ),
        out_shape=[jax.ShapeDtypeStruct((n_blocks * EXPERT_ROWS, WORD_SLABS, LANES), jnp.uint32),
                   jax.ShapeDtypeStruct(((prompt_tiles + sample_tiles) * t, D_MODEL), F32)],
        compiler_params=_params(1),
        name="dispatch",
    )(fill_start, fill_len, n_used, dest_t, pk_p, pk_s, h_p, h_s, wg_b, wu_b, wd_b)


def _block_plan(counts):
    padded = (counts + EXPERT_ROWS - 1) // EXPERT_ROWS * EXPERT_ROWS
    pad_end = jnp.cumsum(padded).astype(jnp.int32)
    pad_start = pad_end - padded
    n_used = pad_end[-1] // EXPERT_ROWS
    fill_start = pad_start + counts
    fill_len = pad_end - fill_start
    first_block = pad_start // EXPERT_ROWS
    n_expert_blocks = (padded // EXPERT_ROWS).astype(jnp.int32)
    return pad_start, fill_start, fill_len, first_block, n_expert_blocks, n_used.reshape(1).astype(jnp.int32)


def kernel(x_prompt, x_sample, cache_k, cache_v, w_in, sink, gm_ln_g, gm_ln_b, gm_w_s, gm_b_s, w_out, ln1_g, ln1_b,
           router_w, router_bias, w_gate_e, w_up_e, w_down_e, w_gate_s, w_up_s, w_down_s, ln2_g, ln2_b):
    bp, sp = x_prompt.shape[:2]
    bs, ts = x_sample.shape[:2]
    r = cache_k.shape[2]
    assert r == WINDOW and sp % ROW_TILE == 0 and (bs * ts) % ROW_TILE == 0
    n_p, n_s = bp * sp, bs * ts
    n_total = n_p + n_s
    l = 0

    w_in_b = w_in[l].astype(BF16)
    w_out_b = w_out[l].astype(BF16)
    router_t = router_w[l].T
    r_hi = router_t.astype(BF16)
    r_lo = (router_t - r_hi.astype(F32)).astype(BF16)
    row_vec = lambda v: v.reshape(1, -1)
    gm_g, gm_b = row_vec(gm_ln_g[l]), row_vec(gm_ln_b[l])
    sink_l = sink[l].astype(F32)

    xp2 = x_prompt.reshape(n_p, D_MODEL)
    tabs_p = _rope_tables(jnp.arange(sp, dtype=jnp.int32))
    q, k, v, gu, gv = _in_proj(xp2, w_in_b, tabs_p, gm_g, gm_b, tm=ROW_TILE, gv_dtype=BF16, name="in_proj_prompt")
    nb = sp // WINDOW
    att = _attention(sink_l, q, k, v, k, v, batch=bp, nb=nb, tq=WINDOW, seqs=1, prev_blocks=nb,
                     first_block_has_no_prev=True, name="attn_prompt")
    b_tab_p = jnp.repeat(gm_b_s[l].T, HEAD_DIM, axis=1)
    gm = _gate(gu, gv, gm_w_s[l], b_tab_p, chunk=CHUNK, n_chunks=ROW_TILE // CHUNK, name="gate_prompt")
    last_rows = lambda t: t.reshape(bp, sp, KV_WIDTH)[:, sp - r:, :].reshape(1, bp, r, KV_HEADS, HEAD_DIM)
    new_kp, new_vp = last_rows(k), last_rows(v)
    h_p, pk_p, lt_p = _out_proj(att, gm, xp2, w_out_b, row_vec(ln1_g[l]), row_vec(ln1_b[l]), r_hi, r_lo,
                          name="out_proj_prompt")

    xs2 = x_sample.reshape(n_s, D_MODEL)
    pos_s = PAST_LEN + jnp.arange(ts, dtype=jnp.int32)
    tabs_s = tuple(jnp.tile(t, (bs, 1)) for t in _rope_tables(pos_s))
    q, k, v, gu, gv = _in_proj(xs2, w_in_b, tabs_s, gm_g, gm_b, tm=n_s, gv_dtype=F32, name="in_proj_sample")
    tq = 8
    pad_rows = lambda t: jnp.pad(t.reshape(bs, ts, -1), ((0, 0), (0, tq - ts), (0, 0))).reshape(bs * tq, -1)
    ck2 = cache_k[l].reshape(bs * r, KV_WIDTH)
    cv2 = cache_v[l].reshape(bs * r, KV_WIDTH)
    att = _attention(sink_l, pad_rows(q), pad_rows(k), pad_rows(v), ck2, cv2, batch=bs, nb=1, tq=tq,
                     seqs=SAMPLE_SEQS_PER_STEP, prev_blocks=1, first_block_has_no_prev=False, name="attn_sample")
    att = att.reshape(bs, tq, ATT_WIDTH)[:, :ts].reshape(n_s, ATT_WIDTH)
    gm = _gate_short(gu, gv, gm_w_s[l], gm_b_s[l], length=ts, name="gate_sample")
    new_ks = jnp.concatenate([cache_k[l], k.reshape(bs, ts, KV_HEADS, HEAD_DIM)], axis=1)[:, ts:][None]
    new_vs = jnp.concatenate([cache_v[l], v.reshape(bs, ts, KV_HEADS, HEAD_DIM)], axis=1)[:, ts:][None]
    new_gs = gv.reshape(bs, ts, GM_WIDTH)[None]
    h_s, pk_s, lt_s = _out_proj(att, gm, xs2, w_out_b, row_vec(ln1_g[l]), row_vec(ln1_b[l]), r_hi, r_lo,
                          name="out_proj_sample")

    eidx_t, w_t, rank_t, counts = _route(lt_p, lt_s, router_bias[l].astype(F32).reshape(N_EXPERTS, 1))
    a = n_total * TOP_K
    n_blocks = -(-(a + N_EXPERTS * (EXPERT_ROWS - 1)) // EXPERT_ROWS)
    pad_start, fill_start, fill_len, first_block, n_expert_blocks, n_used = _block_plan(counts.reshape(N_EXPERTS))
    dest_t = _dest(eidx_t, rank_t, pad_start.astype(F32).reshape(N_EXPERTS, 1))
    tiles = lambda a: a.reshape(-1, WORD_SLABS, LANES)
    shared = (w_gate_s[l].astype(BF16), w_up_s[l].astype(BF16), w_down_s[l].astype(BF16))
    x_sorted, y_part = _dispatch(fill_start, fill_len, n_used, dest_t, tiles(pk_p), tiles(pk_s), h_p, h_s, *shared,
                                 n_blocks=n_blocks)
    out_sorted = _experts(first_block, n_expert_blocks, n_used, x_sorted, w_gate_e[l], w_up_e[l], w_down_e[l])
    ln2 = (row_vec(ln2_g[l]), row_vec(ln2_b[l]))
    dest2, w2 = dest_t.T, w_t.T
    y_p = _combine(dest2, w2, y_part, *ln2, out_sorted, n=n_p, row_offset=0, name="combine_prompt")
    y_s = _combine(dest2, w2, y_part, *ln2, out_sorted, n=n_s, row_offset=n_p, name="combine_sample")
    return (y_p.reshape(bp, sp, D_MODEL), y_s.reshape(bs, ts, D_MODEL), new_kp, new_vp, new_ks, new_vs, new_gs)
```

```python
import functools

import jax
import jax.numpy as jnp
import numpy as np
from jax import lax
from jax.experimental import pallas as pl
from jax.experimental.pallas import tpu as pltpu

D_MODEL = 1024
HEAD_DIM = 64
ATT_HEADS = 8
KV_HEADS = 2
Q_PER_KV = ATT_HEADS // KV_HEADS
GM_HEADS = 8
ATT_WIDTH = ATT_HEADS * HEAD_DIM
KV_WIDTH = KV_HEADS * HEAD_DIM
GM_WIDTH = GM_HEADS * HEAD_DIM
ROPE_WIDTH = ATT_WIDTH + KV_WIDTH
IN_WIDTH = ATT_WIDTH + 2 * KV_WIDTH + 2 * GM_WIDTH
WINDOW = 128
CHUNK = 128
PAST_LEN = 16384
ROPE_THETA = 10000.0
ATT_SCALE = HEAD_DIM ** -0.5
N_EXPERTS = 256
TOP_K = 8
N_GROUPS = 8
TOPK_GROUPS = 4
F_EXPERT = 256
ROUTED_SCALE = 2.5
LN_EPS = 1e-5
DEPTH = 1
ALPHA = (2.0 * DEPTH) ** 0.25

LANES = 128
SLABS = 8
PACKED = D_MODEL // 2
WORD_SLABS = PACKED // LANES
ROW_TILE = 512
EXPERT_ROWS = 256
EXPERT_RING = 8
COMBINE_ROWS = 256
COMBINE_GROUP = 4
DISPATCH_ROWS = 256
ROUTE_TILE = 256
SAMPLE_SEQS_PER_STEP = 16
PROMPT_BLOCKS_PER_STEP = 2
DMA_UNROLL = 4
DMA_PRIORITIES = 2
VMEM_LIMIT = 56 * 1024 * 1024

F32 = jnp.float32
BF16 = jnp.bfloat16


def _params(n_axes):
    return pltpu.CompilerParams(dimension_semantics=("arbitrary",) * n_axes, vmem_limit_bytes=VMEM_LIMIT)


def _layer_norm(x, g, b):
    mu = jnp.mean(x, axis=-1, keepdims=True)
    xc = x - mu
    var = jnp.mean(xc * xc, axis=-1, keepdims=True)
    return xc * lax.rsqrt(var + LN_EPS) * g + b


def _pack_halves(x):
    xb = x.astype(BF16)
    lo = lax.bitcast_convert_type(xb[:, :PACKED].astype(F32), jnp.uint32)
    hi = lax.bitcast_convert_type(xb[:, PACKED:].astype(F32), jnp.uint32)
    return (lo >> 16) | hi


def _unpack_halves(w):
    lo = lax.bitcast_convert_type(w << 16, F32)
    hi = lax.bitcast_convert_type(w & jnp.uint32(0xFFFF0000), F32)
    return lo, hi


def _store_row_tiles(ref2d, x):
    m, slabs = x.shape[0], x.shape[1] // LANES
    for s in range(slabs):
        ref2d[pl.ds(s, m, stride=slabs), :] = x[:, s * LANES:(s + 1) * LANES]


def _load_row_tiles(ref2d, m, slabs):
    return jnp.concatenate([ref2d[pl.ds(s, m, stride=slabs), :] for s in range(slabs)], axis=1)


def _gelu(x):
    return 0.5 * x * (1.0 + lax.erf(x * np.float32(np.sqrt(0.5))))


def _in_proj_kernel(x_ref, w_ref, cos_ref, sa_ref, sb_ref, g_ref, b_ref, q_ref, k_ref, v_ref, gu_ref, gv_ref):
    x = x_ref[...].astype(BF16)
    zr = jnp.dot(x, w_ref[:, :ROPE_WIDTH], preferred_element_type=F32)
    pieces = []
    for c in range(ROPE_WIDTH // LANES):
        zc = zr[:, c * LANES:(c + 1) * LANES]
        tl = slice(0, LANES) if c < ATT_WIDTH // LANES else slice(LANES, 2 * LANES)
        pieces.append(zc * cos_ref[:, tl]
                      + pltpu.roll(zc, LANES - HEAD_DIM // 2, 1) * sa_ref[:, tl]
                      + pltpu.roll(zc, HEAD_DIM // 2, 1) * sb_ref[:, tl])
    for c in range(ATT_WIDTH // LANES):
        q_ref[:, c * LANES:(c + 1) * LANES] = pieces[c].astype(q_ref.dtype)
    k_ref[...] = pieces[ATT_WIDTH // LANES]
    v_ref[...] = jnp.dot(x, w_ref[:, ROPE_WIDTH:ROPE_WIDTH + KV_WIDTH], preferred_element_type=F32)
    g0 = ROPE_WIDTH + KV_WIDTH
    zu = jnp.dot(x, w_ref[:, g0:g0 + GM_WIDTH], preferred_element_type=F32)
    gu_ref[...] = _gelu(zu).astype(gu_ref.dtype)
    zv = jnp.dot(x, w_ref[:, g0 + GM_WIDTH:g0 + 2 * GM_WIDTH], preferred_element_type=F32)
    gv = _layer_norm(_gelu(zv), g_ref[...], b_ref[...])
    gv_ref[...] = gv.astype(gv_ref.dtype)


def _in_proj(x2, w_in_b, tabs, ln_g, ln_b, *, tm, gv_dtype, name):
    n = x2.shape[0]
    cos_t, sa_t, sb_t = tabs
    period = cos_t.shape[0] // tm
    row = lambda i: (i, 0)
    tab = lambda i: (i % period, 0)
    fixed = lambda i: (0, 0)
    return pl.pallas_call(
        _in_proj_kernel,
        grid=(n // tm,),
        in_specs=[
            pl.BlockSpec((tm, D_MODEL), row),
            pl.BlockSpec((D_MODEL, IN_WIDTH), fixed),
            pl.BlockSpec((tm, 2 * LANES), tab),
            pl.BlockSpec((tm, 2 * LANES), tab),
            pl.BlockSpec((tm, 2 * LANES), tab),
            pl.BlockSpec((1, GM_WIDTH), fixed),
            pl.BlockSpec((1, GM_WIDTH), fixed),
        ],
        out_specs=[
            pl.BlockSpec((tm, ATT_WIDTH), row),
            pl.BlockSpec((tm, KV_WIDTH), row),
            pl.BlockSpec((tm, KV_WIDTH), row),
            pl.BlockSpec((tm, GM_WIDTH), row),
            pl.BlockSpec((tm, GM_WIDTH), row),
        ],
        out_shape=[
            jax.ShapeDtypeStruct((n, ATT_WIDTH), BF16),
            jax.ShapeDtypeStruct((n, KV_WIDTH), F32),
            jax.ShapeDtypeStruct((n, KV_WIDTH), F32),
            jax.ShapeDtypeStruct((n, GM_WIDTH), BF16),
            jax.ShapeDtypeStruct((n, GM_WIDTH), gv_dtype),
        ],
        compiler_params=_params(1),
        name=name,
    )(x2, w_in_b, cos_t, sa_t, sb_t, ln_g, ln_b)


def _rope_tables(pos):
    half = HEAD_DIM // 2
    lane = jnp.arange(2 * LANES, dtype=jnp.int32)
    inv = ROPE_THETA ** (-(lane % half).astype(F32) * 2.0 / HEAD_DIM)
    ang = pos.astype(F32)[:, None] * inv[None, :]
    scale = jnp.where(lane < LANES, ATT_SCALE, 1.0).astype(F32)[None, :]
    first_half = ((lane % HEAD_DIM) < half)[None, :]
    cos, sin = jnp.cos(ang) * scale, jnp.sin(ang) * scale
    return cos, jnp.where(first_half, -sin, 0.0), jnp.where(first_half, 0.0, sin)


def _attn_kernel(sink_ref, q_ref, kc_ref, vc_ref, kp_ref, vp_ref, o_ref, *, tq, seqs, chain, stack,
                 first_block_has_no_prev):
    nk = WINDOW + tq
    rows = stack * tq
    qi = lax.broadcasted_iota(jnp.int32, (rows, nk), 0) & (tq - 1)
    ks = lax.broadcasted_iota(jnp.int32, (rows, nk), 1)
    band = (ks >= qi) & (ks <= qi + WINDOW)
    first_mask = band & ((pl.program_id(1) > 0) | (ks >= WINDOW)) if first_block_has_no_prev else band
    sinks = [jnp.concatenate([jnp.full((tq, 1), sink_ref[h0 + j], F32) for j in range(stack)], axis=0)
             for h0 in range(0, ATT_HEADS, stack)]
    for b in range(seqs * chain):
        qrows = slice(b * tq, (b + 1) * tq)
        if chain > 1 and b > 0:
            kprev, vprev = kc_ref[(b - 1) * tq:b * tq, :], vc_ref[(b - 1) * tq:b * tq, :]
        else:
            kprev, vprev = kp_ref[b * WINDOW:(b + 1) * WINDOW, :], vp_ref[b * WINDOW:(b + 1) * WINDOW, :]
        mask = band if (chain > 1 and b > 0) else first_mask
        q = q_ref[qrows, :]
        kk = jnp.concatenate([kprev, kc_ref[qrows, :]], axis=0).astype(BF16)
        vv = jnp.concatenate([vprev, vc_ref[qrows, :]], axis=0).astype(BF16)
        outs = []
        for i, h0 in enumerate(range(0, ATT_HEADS, stack)):
            g = h0 // Q_PER_KV
            kg = kk[:, g * HEAD_DIM:(g + 1) * HEAD_DIM]
            vg = vv[:, g * HEAD_DIM:(g + 1) * HEAD_DIM]
            qg = jnp.concatenate([q[:, (h0 + j) * HEAD_DIM:(h0 + j + 1) * HEAD_DIM] for j in range(stack)], axis=0)
            s = lax.dot_general(qg, kg, (((1,), (1,)), ((), ())), preferred_element_type=F32)
            s = jnp.where(mask, s, -jnp.inf)
            m = jnp.maximum(jnp.max(s, axis=-1, keepdims=True), sinks[i])
            p = jnp.exp(s - m)
            denom = jnp.sum(p, axis=-1, keepdims=True) + jnp.exp(sinks[i] - m)
            o = jnp.dot((p / denom).astype(BF16), vg, preferred_element_type=F32)
            outs.extend(o[j * tq:(j + 1) * tq, :] for j in range(stack))
        o_ref[qrows, :] = jnp.concatenate(outs, axis=1).astype(o_ref.dtype)


def _attention(sink, q2, k2, v2, kprev2, vprev2, *, batch, nb, tq, seqs, chain, prev_blocks, first_block_has_no_prev,
               name):
    assert tq & (tq - 1) == 0 and batch % seqs == 0 and (seqs == 1 or nb == prev_blocks == 1)
    assert (seqs == 1 or chain == 1) and nb % chain == 0 and (chain == 1 or tq == WINDOW)
    steps = nb // chain
    cur = lambda b, n, s: (b * steps + n, 0)
    prev = lambda b, n, s: (b * prev_blocks + jnp.maximum(n * chain - 1, 0), 0)
    stack = Q_PER_KV if Q_PER_KV * tq <= WINDOW else 1
    kern = functools.partial(_attn_kernel, tq=tq, seqs=seqs, chain=chain, stack=stack,
                             first_block_has_no_prev=first_block_has_no_prev)
    return pl.pallas_call(
        kern,
        grid_spec=pltpu.PrefetchScalarGridSpec(
            num_scalar_prefetch=1,
            grid=(batch // seqs, steps),
            in_specs=[
                pl.BlockSpec((seqs * chain * tq, ATT_WIDTH), cur),
                pl.BlockSpec((seqs * chain * tq, KV_WIDTH), cur),
                pl.BlockSpec((seqs * chain * tq, KV_WIDTH), cur),
                pl.BlockSpec((seqs * WINDOW, KV_WIDTH), prev),
                pl.BlockSpec((seqs * WINDOW, KV_WIDTH), prev),
            ],
            out_specs=pl.BlockSpec((seqs * chain * tq, ATT_WIDTH), cur),
        ),
        out_shape=jax.ShapeDtypeStruct(q2.shape, BF16),
        compiler_params=_params(2),
        name=name,
    )(sink, q2, k2, v2, kprev2, vprev2)


def _gate_kernel(gu_ref, gv_ref, w_ref, b_ref, o_ref, *, chunk, n_chunks):
    ri = lax.broadcasted_iota(jnp.int32, (chunk, chunk), 0)
    ci = lax.broadcasted_iota(jnp.int32, (chunk, chunk), 1)
    ws = [jnp.where(ci <= ri, w_ref[h], 0.0).astype(BF16) for h in range(GM_HEADS)]
    for c in range(n_chunks):
        rows = slice(c * chunk, (c + 1) * chunk)
        gv = gv_ref[rows, :].astype(BF16)
        sv = jnp.concatenate(
            [jnp.dot(ws[h], gv[:, h * HEAD_DIM:(h + 1) * HEAD_DIM], preferred_element_type=F32)
             for h in range(GM_HEADS)], axis=1)
        o_ref[rows, :] = (gu_ref[rows, :].astype(F32) * (sv + b_ref[...])).astype(o_ref.dtype)


def _gate(gu2, gv2, w_s, b_tab, *, chunk, n_chunks, name):
    n = gu2.shape[0]
    tm = chunk * n_chunks
    row = lambda i: (i, 0)
    kern = functools.partial(_gate_kernel, chunk=chunk, n_chunks=n_chunks)
    return pl.pallas_call(
        kern,
        grid=(n // tm,),
        in_specs=[
            pl.BlockSpec((tm, GM_WIDTH), row),
            pl.BlockSpec((tm, GM_WIDTH), row),
            pl.BlockSpec((GM_HEADS, chunk, chunk), lambda i: (0, 0, 0)),
            pl.BlockSpec((chunk, GM_WIDTH), lambda i: (0, 0)),
        ],
        out_specs=pl.BlockSpec((tm, GM_WIDTH), row),
        out_shape=jax.ShapeDtypeStruct((n, GM_WIDTH), BF16),
        compiler_params=_params(1),
        name=name,
    )(gu2, gv2, w_s, b_tab)


def _gate_short_kernel(gu_ref, gv_ref, coef_ref, b_ref, o_ref, *, length):
    n = gv_ref.shape[0]
    gv = gv_ref[...].astype(F32)
    tiles = lambda x: x.reshape(n // SLABS, SLABS, GM_WIDTH)
    sv = tiles(gv) * coef_ref[0][None] + b_ref[...][None]
    for d in range(1, length):
        sv = sv + tiles(pltpu.roll(gv, d, 0)) * coef_ref[d][None]
    o_ref[...] = (gu_ref[...].astype(F32) * sv.reshape(n, GM_WIDTH)).astype(o_ref.dtype)


def _gate_short(gu2, gv2, w_s, b_s, *, length, name):
    n = gu2.shape[0]
    assert SLABS % length == 0 and n % SLABS == 0
    pos = jnp.arange(length)
    lag = jnp.arange(length)
    src = pos[None, :] - lag[:, None]
    coef = jnp.where(src >= 0, w_s[:, pos[None, :], jnp.maximum(src, 0)], 0.0)
    coef = jnp.repeat(coef.transpose(1, 2, 0), HEAD_DIM, axis=2)
    coef = jnp.tile(coef, (1, SLABS // length, 1))
    bias = jnp.tile(jnp.repeat(b_s[:, :length].T, HEAD_DIM, axis=1), (SLABS // length, 1))
    whole = lambda shape: pl.BlockSpec(shape, lambda i: (0,) * len(shape))
    return pl.pallas_call(
        functools.partial(_gate_short_kernel, length=length),
        grid=(1,),
        in_specs=[whole((n, GM_WIDTH)), whole((n, GM_WIDTH)), whole((length, SLABS, GM_WIDTH)),
                  whole((SLABS, GM_WIDTH))],
        out_specs=whole((n, GM_WIDTH)),
        out_shape=jax.ShapeDtypeStruct((n, GM_WIDTH), BF16),
        compiler_params=_params(1),
        name=name,
    )(gu2, gv2, coef, bias)


def _out_proj_kernel(att_ref, gm_ref, x_ref, wo_ref, g_ref, b_ref, rh_ref, rl_ref, h_ref, hp_ref, lg_ref):
    mix = jnp.dot(att_ref[...], wo_ref[:ATT_WIDTH, :], preferred_element_type=F32)
    mix = mix + jnp.dot(gm_ref[...], wo_ref[ATT_WIDTH:, :], preferred_element_type=F32)
    h = _layer_norm(ALPHA * x_ref[...] + mix, g_ref[...], b_ref[...])
    h_ref[...] = h
    _store_row_tiles(hp_ref, _pack_halves(h))
    h_hi = h.astype(BF16)
    h_lo = (h - h_hi.astype(F32)).astype(BF16)
    nt = (((1,), (1,)), ((), ()))
    lg = lax.dot_general(rh_ref[...], h_hi, nt, preferred_element_type=F32)
    lg = lg + lax.dot_general(rh_ref[...], h_lo, nt, preferred_element_type=F32)
    lg = lg + lax.dot_general(rl_ref[...], h_hi, nt, preferred_element_type=F32)
    lg_ref[...] = lg


def _out_proj(att2, gm2, x2, w_out_b, ln_g, ln_b, r_hi, r_lo, *, name):
    n = x2.shape[0]
    tm = ROW_TILE
    row = lambda i: (i, 0)
    fixed = lambda i: (0, 0)
    return pl.pallas_call(
        _out_proj_kernel,
        grid=(n // tm,),
        in_specs=[
            pl.BlockSpec((tm, ATT_WIDTH), row),
            pl.BlockSpec((tm, GM_WIDTH), row),
            pl.BlockSpec((tm, D_MODEL), row),
            pl.BlockSpec((D_MODEL, D_MODEL), fixed),
            pl.BlockSpec((1, D_MODEL), fixed),
            pl.BlockSpec((1, D_MODEL), fixed),
            pl.BlockSpec((N_EXPERTS, D_MODEL), fixed),
            pl.BlockSpec((N_EXPERTS, D_MODEL), fixed),
        ],
        out_specs=[
            pl.BlockSpec((tm, D_MODEL), row),
            pl.BlockSpec((tm * WORD_SLABS, LANES), row),
            pl.BlockSpec((N_EXPERTS, tm), lambda i: (0, i)),
        ],
        out_shape=[
            jax.ShapeDtypeStruct((n, D_MODEL), F32),
            jax.ShapeDtypeStruct((n * WORD_SLABS, LANES), jnp.uint32),
            jax.ShapeDtypeStruct((N_EXPERTS, n), F32),
        ],
        compiler_params=_params(1),
        name=name,
    )(att2, gm2, x2, w_out_b, ln_g, ln_b, r_hi, r_lo)


def _experts_kernel(first_ref, nblk_ref, nu_ref, x_hbm, wg_ref, wu_ref, wd_ref, o_hbm, xbuf, obuf, wg_s, wu_s, wd_s,
                    in_sem, out_sem, fill_sem, *, n_blocks):
    e = pl.program_id(0)
    nb = nblk_ref[e]
    b0 = first_ref[e]
    n_used = nu_ref[0]
    ahead = EXPERT_RING // 2

    def rows_of(block):
        return pl.ds(pl.multiple_of(block * EXPERT_ROWS, EXPERT_ROWS), EXPERT_ROWS)

    def slot_of(block):
        return block & (EXPERT_RING - 1)

    def slab(s):
        return pl.ds(s * LANES, LANES)

    def in_copies(block):
        slot = slot_of(block)
        return [pltpu.make_async_copy(x_hbm.at[rows_of(block), s, :], xbuf.at[slot, :, slab(s)], in_sem.at[slot])
                for s in range(WORD_SLABS)]

    def out_copies(block, slot=None, sem=None):
        slot = slot_of(block) if slot is None else slot
        sem = out_sem.at[slot] if sem is None else sem
        return [pltpu.make_async_copy(obuf.at[slot, :, slab(s)], o_hbm.at[rows_of(block), s, :], sem)
                for s in range(WORD_SLABS)]

    def start(copies):
        for cp in copies:
            cp.start()

    def wait(copies):
        for cp in copies:
            cp.wait()

    @pl.when(e == 0)
    def _():
        for j in range(ahead):
            pl.when(j < n_used)(lambda j=j: start(in_copies(j)))

    def process(block, count):
        blocks = [block + j for j in range(count)]
        for blk in blocks:
            wait(in_copies(blk))
        x = [xbuf[slot_of(blk)] for blk in blocks]
        xl, xh = _unpack_halves(x[0] if count == 1 else jnp.concatenate(x, axis=0))
        xl, xh = xl.astype(BF16), xh.astype(BF16)
        for blk in blocks:
            pl.when(blk + ahead < n_used)(lambda blk=blk: start(in_copies(blk + ahead)))
            pl.when(blk >= ahead)(lambda blk=blk: wait(out_copies(blk - ahead)))
        a = (jnp.dot(xl, wg_s[:PACKED, :], preferred_element_type=F32)
             + jnp.dot(xh, wg_s[PACKED:, :], preferred_element_type=F32))
        u = (jnp.dot(xl, wu_s[:PACKED, :], preferred_element_type=F32)
             + jnp.dot(xh, wu_s[PACKED:, :], preferred_element_type=F32))
        hb = (a * jax.nn.sigmoid(a) * u).astype(BF16)
        o = _pack_halves(jnp.dot(hb, wd_s[...], preferred_element_type=F32))
        for j, blk in enumerate(blocks):
            obuf[slot_of(blk)] = o[j * EXPERT_ROWS:(j + 1) * EXPERT_ROWS, :]
            start(out_copies(blk))

    @pl.when(nb > 0)
    def _():
        wg_s[...] = wg_ref[0].astype(BF16)
        wu_s[...] = wu_ref[0].astype(BF16)
        wd_s[...] = wd_ref[0].astype(BF16)

        def pair(i, carry):
            process(b0 + 2 * i, 2)
            return carry

        lax.fori_loop(0, nb // 2, pair, 0)
        pl.when(nb % 2 == 1)(lambda: process(b0 + nb - 1, 1))

    @pl.when(e == N_EXPERTS - 1)
    def _():
        for j in range(ahead):
            pl.when(n_used - 1 - j >= 0)(lambda j=j: wait(out_copies(n_used - 1 - j)))
        obuf[0] = jnp.zeros((EXPERT_ROWS, PACKED), jnp.uint32)

        def on_unused_blocks(fn):
            def body(b, c):
                fn(out_copies(b, slot=0, sem=fill_sem))
                return c
            lax.fori_loop(n_used, n_blocks, body, 0)

        on_unused_blocks(start)
        on_unused_blocks(wait)


def _experts(first_block, n_expert_blocks, n_used, x_sorted, w_gate_e, w_up_e, w_down_e):
    rows = x_sorted.shape[0]
    wmap = lambda e, *_: (e, 0, 0)
    kern = functools.partial(_experts_kernel, n_blocks=rows // EXPERT_ROWS)
    return pl.pallas_call(
        kern,
        grid_spec=pltpu.PrefetchScalarGridSpec(
            num_scalar_prefetch=3,
            grid=(N_EXPERTS,),
            in_specs=[
                pl.BlockSpec(memory_space=pl.ANY),
                pl.BlockSpec((1, D_MODEL, F_EXPERT), wmap),
                pl.BlockSpec((1, D_MODEL, F_EXPERT), wmap),
                pl.BlockSpec((1, F_EXPERT, D_MODEL), wmap),
            ],
            out_specs=pl.BlockSpec(memory_space=pl.ANY),
            scratch_shapes=[
                pltpu.VMEM((EXPERT_RING, EXPERT_ROWS, PACKED), jnp.uint32),
                pltpu.VMEM((EXPERT_RING, EXPERT_ROWS, PACKED), jnp.uint32),
                pltpu.VMEM((D_MODEL, F_EXPERT), BF16),
                pltpu.VMEM((D_MODEL, F_EXPERT), BF16),
                pltpu.VMEM((F_EXPERT, D_MODEL), BF16),
                pltpu.SemaphoreType.DMA((EXPERT_RING,)),
                pltpu.SemaphoreType.DMA((EXPERT_RING,)),
                pltpu.SemaphoreType.DMA,
            ],
        ),
        out_shape=jax.ShapeDtypeStruct(x_sorted.shape, x_sorted.dtype),
        compiler_params=_params(1),
        name="experts",
    )(first_block, n_expert_blocks, n_used, x_sorted, w_gate_e, w_up_e, w_down_e)


def _combine_kernel(d_ref, w_ref, yp_ref, ln_g_ref, ln_b_ref, os_ref, y_ref, gbuf0, gbuf1, rlo, rhi, sem):
    s = pl.program_id(0)
    n_tiles = pl.num_programs(0) - 1
    t = y_ref.shape[0]
    gtiles = [g.reshape(t * TOP_K, WORD_SLABS, LANES) for g in (gbuf0, gbuf1)]
    lo_tiles = rlo.reshape(t, WORD_SLABS, LANES)
    hi_tiles = rhi.reshape(t, WORD_SLABS, LANES)

    def row_copy(row, buf, j, k):
        return pltpu.make_async_copy(os_ref.at[pl.ds(row, 1)], gtiles[buf].at[pl.ds(j * TOP_K + k, 1)], sem.at[buf])

    def drain(buf):
        def body(j, c):
            for k in range(TOP_K):
                row_copy(0, buf, j, k).wait()
            return c
        lax.fori_loop(0, t, body, 0, unroll=DMA_UNROLL)

    def weighted_sum(rows, j):
        lo, hi = _unpack_halves(rows)
        acc_lo = w_ref[j, 0] * lo[0:1]
        acc_hi = w_ref[j, 0] * hi[0:1]
        for k in range(1, TOP_K):
            acc_lo = acc_lo + w_ref[j, k] * lo[k:k + 1]
            acc_hi = acc_hi + w_ref[j, k] * hi[k:k + 1]
        lo_tiles[pl.ds(j, 1)] = acc_lo
        hi_tiles[pl.ds(j, 1)] = acc_hi

    def token_loop(new, old, start_new, sum_old):
        def body(g, c):
            tokens = [g * COMBINE_GROUP + u for u in range(COMBINE_GROUP)]
            rows = [gtiles[old][pl.ds(j * TOP_K, TOP_K)] for j in tokens] if sum_old else []
            if start_new:
                for j in tokens:
                    for k in range(TOP_K):
                        row_copy(d_ref[j, k], new, j, k).start(priority=k % DMA_PRIORITIES)
            for r, j in zip(rows, tokens):
                weighted_sum(r, j)
            return c
        lax.fori_loop(0, t // COMBINE_GROUP, body, 0)

    def step(new):
        old = 1 - new
        pl.when(s == 0)(lambda: token_loop(new, old, True, False))

        @pl.when(s >= 1)
        def _():
            drain(old)
            pl.when(s < n_tiles)(lambda: token_loop(new, old, True, True))
            pl.when(s == n_tiles)(lambda: token_loop(new, old, False, True))
            routed = jnp.concatenate([_load_row_tiles(rlo, t, WORD_SLABS), _load_row_tiles(rhi, t, WORD_SLABS)],
                                     axis=1)
            y_ref[...] = _layer_norm(yp_ref[...] + routed, ln_g_ref[...], ln_b_ref[...])

    pl.when(s % 2 == 0)(lambda: step(0))
    pl.when(s % 2 == 1)(lambda: step(1))


def _combine(dest2, w2, y_part, ln_g, ln_b, out_sorted, *, n, row_offset, name):
    tm = COMBINE_ROWS
    off = row_offset // tm
    n_tiles = n // tm
    fixed = lambda s: (0, 0)
    new_tile = lambda s: (jnp.minimum(s, n_tiles - 1) + off, 0)
    old_tile = lambda s: (jnp.maximum(s - 1, 0) + off, 0)
    return pl.pallas_call(
        _combine_kernel,
        grid=(n_tiles + 1,),
        in_specs=[
            pl.BlockSpec((tm, TOP_K), new_tile, memory_space=pltpu.SMEM),
            pl.BlockSpec((tm, TOP_K), old_tile, memory_space=pltpu.SMEM),
            pl.BlockSpec((tm, D_MODEL), old_tile),
            pl.BlockSpec((1, D_MODEL), fixed),
            pl.BlockSpec((1, D_MODEL), fixed),
            pl.BlockSpec(memory_space=pl.ANY),
        ],
        out_specs=pl.BlockSpec((tm, D_MODEL), lambda s: (jnp.maximum(s - 1, 0), 0)),
        out_shape=jax.ShapeDtypeStruct((n, D_MODEL), F32),
        scratch_shapes=[pltpu.VMEM((TOP_K * tm * WORD_SLABS, LANES), jnp.uint32),
                        pltpu.VMEM((TOP_K * tm * WORD_SLABS, LANES), jnp.uint32),
                        pltpu.VMEM((tm * WORD_SLABS, LANES), F32), pltpu.VMEM((tm * WORD_SLABS, LANES), F32),
                        pltpu.SemaphoreType.DMA((2,))],
        compiler_params=_params(1),
        name=name,
    )(dest2, w2, y_part, ln_g, ln_b, out_sorted)


def _route_kernel(lgp_ref, lgs_ref, bias_ref, eidx_ref, w_ref, rank_ref, cnt_ref, carry_ref, *, prompt_tiles):
    @pl.when(pl.program_id(0) == 0)
    def _():
        carry_ref[...] = jnp.zeros_like(carry_ref)

    t = lgp_ref.shape[1]
    gsz = N_EXPERTS // N_GROUPS
    neg = -jnp.inf
    s = jax.nn.sigmoid(jnp.where(pl.program_id(0) < prompt_tiles, lgp_ref[...], lgs_ref[...]))
    biased = s + bias_ref[...]
    io_g = lax.broadcasted_iota(jnp.int32, (gsz, t), 0)
    grp_rows = []
    for g in range(N_GROUPS):
        blk = biased[g * gsz:(g + 1) * gsz, :]
        m1 = jnp.max(blk, axis=0, keepdims=True)
        i1 = jnp.min(jnp.where(blk == m1, io_g, gsz), axis=0, keepdims=True)
        m2 = jnp.max(jnp.where(io_g == i1, neg, blk), axis=0, keepdims=True)
        grp_rows.append(m1 + m2)
    gs = jnp.concatenate(grp_rows, axis=0)
    io8 = lax.broadcasted_iota(jnp.int32, (N_GROUPS, t), 0)
    gsel = jnp.zeros((N_GROUPS, t), jnp.int32)
    for _ in range(TOPK_GROUPS):
        m = jnp.max(gs, axis=0, keepdims=True)
        gi = jnp.min(jnp.where(gs == m, io8, N_GROUPS), axis=0, keepdims=True)
        hit = io8 == gi
        gsel = jnp.where(hit, 1, gsel)
        gs = jnp.where(hit, neg, gs)
    masked = jnp.concatenate(
        [jnp.where(gsel[g:g + 1, :] > 0, biased[g * gsz:(g + 1) * gsz, :], neg) for g in range(N_GROUPS)], axis=0)

    eio = lax.broadcasted_iota(jnp.int32, (N_EXPERTS, t), 0)
    cur = masked
    idx_rows, w_rows = [], []
    for _ in range(TOP_K):
        m = jnp.max(cur, axis=0, keepdims=True)
        idx = jnp.min(jnp.where(cur == m, eio, N_EXPERTS), axis=0, keepdims=True)
        hit = eio == idx
        w_rows.append(jnp.sum(jnp.where(hit, s, 0.0), axis=0, keepdims=True))
        cur = jnp.where(hit, neg, cur)
        idx_rows.append(idx)
    sel = jnp.where(cur != masked, 1.0, 0.0)

    tri = jnp.where(lax.broadcasted_iota(jnp.int32, (t, t), 0) < lax.broadcasted_iota(jnp.int32, (t, t), 1), 1.0, 0.0)
    pref = jnp.dot(sel.astype(BF16), tri.astype(BF16), preferred_element_type=F32) + carry_ref[...]
    rank_rows = [jnp.sum(jnp.where(eio == idx_rows[k], pref, 0.0), axis=0, keepdims=True) for k in range(TOP_K)]
    carry_ref[...] += jnp.sum(sel, axis=1, keepdims=True)

    wk = jnp.concatenate(w_rows, axis=0)
    eidx_ref[...] = jnp.concatenate(idx_rows, axis=0)
    w_ref[...] = wk / jnp.sum(wk, axis=0, keepdims=True) * ROUTED_SCALE
    rank_ref[...] = jnp.concatenate(rank_rows, axis=0).astype(jnp.int32)
    cnt_ref[...] = carry_ref[...].astype(jnp.int32)


def _route(logits_p, logits_s, bias_col):
    t = ROUTE_TILE
    prompt_tiles = logits_p.shape[1] // t
    n = logits_p.shape[1] + logits_s.shape[1]
    col = lambda i: (0, i)
    fixed = lambda i: (0, 0)
    kern = functools.partial(_route_kernel, prompt_tiles=prompt_tiles)
    return pl.pallas_call(
        kern,
        grid=(n // t,),
        in_specs=[pl.BlockSpec((N_EXPERTS, t), lambda i: (0, jnp.minimum(i, prompt_tiles - 1))),
                  pl.BlockSpec((N_EXPERTS, t), lambda i: (0, jnp.maximum(i - prompt_tiles, 0))),
                  pl.BlockSpec((N_EXPERTS, 1), fixed)],
        out_specs=[
            pl.BlockSpec((TOP_K, t), col),
            pl.BlockSpec((TOP_K, t), col),
            pl.BlockSpec((TOP_K, t), col),
            pl.BlockSpec((N_EXPERTS, 1), fixed),
        ],
        out_shape=[
            jax.ShapeDtypeStruct((TOP_K, n), jnp.int32),
            jax.ShapeDtypeStruct((TOP_K, n), F32),
            jax.ShapeDtypeStruct((TOP_K, n), jnp.int32),
            jax.ShapeDtypeStruct((N_EXPERTS, 1), jnp.int32),
        ],
        scratch_shapes=[pltpu.VMEM((N_EXPERTS, 1), F32)],
        compiler_params=_params(1),
        name="route",
    )(logits_p, logits_s, bias_col)


def _dest_kernel(eidx_ref, rank_ref, start_ref, dest_ref):
    t = eidx_ref.shape[1]
    eio = lax.broadcasted_iota(jnp.int32, (N_EXPERTS, t), 0)
    start = start_ref[...]
    rows = [jnp.sum(jnp.where(eio == eidx_ref[k:k + 1, :], start, 0.0), axis=0, keepdims=True) for k in range(TOP_K)]
    dest_ref[...] = jnp.concatenate(rows, axis=0).astype(jnp.int32) + rank_ref[...]


def _dest(eidx_t, rank_t, pad_start_col):
    n = eidx_t.shape[1]
    t = ROW_TILE
    col = lambda i: (0, i)
    return pl.pallas_call(
        _dest_kernel,
        grid=(n // t,),
        in_specs=[pl.BlockSpec((TOP_K, t), col), pl.BlockSpec((TOP_K, t), col),
                  pl.BlockSpec((N_EXPERTS, 1), lambda i: (0, 0))],
        out_specs=pl.BlockSpec((TOP_K, t), col),
        out_shape=jax.ShapeDtypeStruct((TOP_K, n), jnp.int32),
        compiler_params=_params(1),
        name="dest",
    )(eidx_t, rank_t, pad_start_col)


def _dispatch_kernel(fill_ref, len_ref, nu_ref, dest_ref, pp_ref, ps_ref, hp_ref, hs_ref, wg_ref, wu_ref, wd_ref,
                     xs_ref, yp_ref, zbuf, sem, fill_sem, *, prompt_tiles, n_blocks):
    i = pl.program_id(0)

    @pl.when(i == 0)
    def _():
        zbuf[...] = jnp.zeros_like(zbuf)

        def fill_copy(row0, size):
            return pltpu.make_async_copy(zbuf.at[pl.ds(0, size)], xs_ref.at[pl.ds(row0, size)], fill_sem)

        def on_padding(fn):
            def body(e, c):
                base, length = fill_ref[e], len_ref[e]
                size = EXPERT_ROWS // 2
                while size >= 1:
                    piece = fill_copy(base + (length & ~(2 * size - 1)), size)
                    pl.when((length & size) != 0)(functools.partial(fn, piece))
                    size //= 2
                return c
            lax.fori_loop(0, N_EXPERTS, body, 0)

        def on_unused_blocks(fn):
            lax.fori_loop(nu_ref[0], n_blocks, lambda b, c: (fn(fill_copy(b * EXPERT_ROWS, EXPERT_ROWS)), c)[1], 0)

        on_padding(lambda cp: cp.start())
        on_unused_blocks(lambda cp: cp.start())
        on_padding(lambda cp: cp.wait())
        on_unused_blocks(lambda cp: cp.wait())

    def tile_step(src_ref, h_ref):
        t = src_ref.shape[0]

        def row_copy(j, k):
            return pltpu.make_async_copy(src_ref.at[pl.ds(j, 1)], xs_ref.at[pl.ds(dest_ref[k, j], 1)], sem)

        def issue(j, c):
            for k in range(TOP_K):
                row_copy(j, k).start(priority=k % DMA_PRIORITIES)
            return c

        def drain(j, c):
            for k in range(TOP_K):
                row_copy(j, k).wait()
            return c

        lax.fori_loop(0, t, issue, 0, unroll=DMA_UNROLL)
        h = h_ref[...]
        hb = h.astype(BF16)
        a = jnp.dot(hb, wg_ref[...], preferred_element_type=F32)
        u = jnp.dot(hb, wu_ref[...], preferred_element_type=F32)
        shared = jnp.dot((a * jax.nn.sigmoid(a) * u).astype(BF16), wd_ref[...], preferred_element_type=F32)
        yp_ref[...] = ALPHA * h + shared
        lax.fori_loop(0, t, drain, 0, unroll=DMA_UNROLL)

    @pl.when(i < prompt_tiles)
    def _():
        tile_step(pp_ref, hp_ref)

    @pl.when(i >= prompt_tiles)
    def _():
        tile_step(ps_ref, hs_ref)


def _dispatch(fill_start, fill_len, n_used, dest_t, pk_p, pk_s, h_p, h_s, wg_b, wu_b, wd_b, *, n_blocks):
    t = DISPATCH_ROWS
    prompt_tiles = h_p.shape[0] // t
    sample_tiles = h_s.shape[0] // t
    tile = (t, WORD_SLABS, LANES)
    fixed = lambda i, *_: (0, 0)
    p_idx = lambda i: jnp.minimum(i, prompt_tiles - 1)
    s_idx = lambda i: jnp.maximum(i - prompt_tiles, 0)
    kern = functools.partial(_dispatch_kernel, prompt_tiles=prompt_tiles, n_blocks=n_blocks)
    return pl.pallas_call(
        kern,
        grid_spec=pltpu.PrefetchScalarGridSpec(
            num_scalar_prefetch=3,
            grid=(prompt_tiles + sample_tiles,),
            in_specs=[
                pl.BlockSpec((TOP_K, t), lambda i, *_: (0, i), memory_space=pltpu.SMEM),
                pl.BlockSpec(tile, lambda i, *_: (p_idx(i), 0, 0)),
                pl.BlockSpec(tile, lambda i, *_: (s_idx(i), 0, 0)),
                pl.BlockSpec((t, D_MODEL), lambda i, *_: (p_idx(i), 0)),
                pl.BlockSpec((t, D_MODEL), lambda i, *_: (s_idx(i), 0)),
                pl.BlockSpec((D_MODEL, F_EXPERT), fixed),
                pl.BlockSpec((D_MODEL, F_EXPERT), fixed),
                pl.BlockSpec((F_EXPERT, D_MODEL), fixed),
            ],
            out_specs=[pl.BlockSpec(memory_space=pl.ANY), pl.BlockSpec((t, D_MODEL), lambda i, *_: (i, 0))],
            scratch_shapes=[pltpu.VMEM((EXPERT_ROWS, WORD_SLABS, LANES), jnp.uint32), pltpu.SemaphoreType.DMA,
                            pltpu.SemaphoreType.DMA],
        ),
        out_shape=[jax.ShapeDtypeStruct((n_blocks * EXPERT_ROWS, WORD_SLABS, LANES), jnp.uint32),
                   jax.ShapeDtypeStruct(((prompt_tiles + sample_tiles) * t, D_MODEL), F32)],
        compiler_params=_params(1),
        name="dispatch",
    )(fill_start, fill_len, n_used, dest_t, pk_p, pk_s, h_p, h_s, wg_b, wu_b, wd_b)


def _block_plan(counts):
    padded = (counts + EXPERT_ROWS - 1) // EXPERT_ROWS * EXPERT_ROWS
    pad_end = jnp.cumsum(padded).astype(jnp.int32)
    pad_start = pad_end - padded
    n_used = pad_end[-1] // EXPERT_ROWS
    fill_start = pad_start + counts
    fill_len = pad_end - fill_start
    first_block = pad_start // EXPERT_ROWS
    n_expert_blocks = (padded // EXPERT_ROWS).astype(jnp.int32)
    return pad_start, fill_start, fill_len, first_block, n_expert_blocks, n_used.reshape(1).astype(jnp.int32)


def kernel(x_prompt, x_sample, cache_k, cache_v, w_in, sink, gm_ln_g, gm_ln_b, gm_w_s, gm_b_s, w_out, ln1_g, ln1_b,
           router_w, router_bias, w_gate_e, w_up_e, w_down_e, w_gate_s, w_up_s, w_down_s, ln2_g, ln2_b):
    bp, sp = x_prompt.shape[:2]
    bs, ts = x_sample.shape[:2]
    r = cache_k.shape[2]
    assert r == WINDOW and sp % ROW_TILE == 0 and (bs * ts) % ROW_TILE == 0
    n_p, n_s = bp * sp, bs * ts
    n_total = n_p + n_s
    l = 0

    w_in_b = w_in[l].astype(BF16)
    w_out_b = w_out[l].astype(BF16)
    router_t = router_w[l].T
    r_hi = router_t.astype(BF16)
    r_lo = (router_t - r_hi.astype(F32)).astype(BF16)
    row_vec = lambda v: v.reshape(1, -1)
    gm_g, gm_b = row_vec(gm_ln_g[l]), row_vec(gm_ln_b[l])
    sink_l = sink[l].astype(F32)

    xp2 = x_prompt.reshape(n_p, D_MODEL)
    tabs_p = _rope_tables(jnp.arange(sp, dtype=jnp.int32))
    q, k, v, gu, gv = _in_proj(xp2, w_in_b, tabs_p, gm_g, gm_b, tm=ROW_TILE, gv_dtype=BF16, name="in_proj_prompt")
    nb = sp // WINDOW
    att = _attention(sink_l, q, k, v, k, v, batch=bp, nb=nb, tq=WINDOW, seqs=1, chain=PROMPT_BLOCKS_PER_STEP,
                     prev_blocks=nb,
                     first_block_has_no_prev=True, name="attn_prompt")
    b_tab_p = jnp.repeat(gm_b_s[l].T, HEAD_DIM, axis=1)
    gm = _gate(gu, gv, gm_w_s[l], b_tab_p, chunk=CHUNK, n_chunks=ROW_TILE // CHUNK, name="gate_prompt")
    last_rows = lambda t: t.reshape(bp, sp, KV_WIDTH)[:, sp - r:, :].reshape(1, bp, r, KV_HEADS, HEAD_DIM)
    new_kp, new_vp = last_rows(k), last_rows(v)
    h_p, pk_p, lt_p = _out_proj(att, gm, xp2, w_out_b, row_vec(ln1_g[l]), row_vec(ln1_b[l]), r_hi, r_lo,
                          name="out_proj_prompt")

    xs2 = x_sample.reshape(n_s, D_MODEL)
    pos_s = PAST_LEN + jnp.arange(ts, dtype=jnp.int32)
    tabs_s = tuple(jnp.tile(t, (bs, 1)) for t in _rope_tables(pos_s))
    q, k, v, gu, gv = _in_proj(xs2, w_in_b, tabs_s, gm_g, gm_b, tm=n_s, gv_dtype=F32, name="in_proj_sample")
    tq = 8
    pad_rows = lambda t: jnp.pad(t.reshape(bs, ts, -1), ((0, 0), (0, tq - ts), (0, 0))).reshape(bs * tq, -1)
    ck2 = cache_k[l].reshape(bs * r, KV_WIDTH)
    cv2 = cache_v[l].reshape(bs * r, KV_WIDTH)
    att = _attention(sink_l, pad_rows(q), pad_rows(k), pad_rows(v), ck2, cv2, batch=bs, nb=1, tq=tq,
                     seqs=SAMPLE_SEQS_PER_STEP, chain=1, prev_blocks=1, first_block_has_no_prev=False, name="attn_sample")
    att = att.reshape(bs, tq, ATT_WIDTH)[:, :ts].reshape(n_s, ATT_WIDTH)
    gm = _gate_short(gu, gv, gm_w_s[l], gm_b_s[l], length=ts, name="gate_sample")
    new_ks = jnp.concatenate([cache_k[l], k.reshape(bs, ts, KV_HEADS, HEAD_DIM)], axis=1)[:, ts:][None]
    new_vs = jnp.concatenate([cache_v[l], v.reshape(bs, ts, KV_HEADS, HEAD_DIM)], axis=1)[:, ts:][None]
    new_gs = gv.reshape(bs, ts, GM_WIDTH)[None]
    h_s, pk_s, lt_s = _out_proj(att, gm, xs2, w_out_b, row_vec(ln1_g[l]), row_vec(ln1_b[l]), r_hi, r_lo,
                          name="out_proj_sample")

    eidx_t, w_t, rank_t, counts = _route(lt_p, lt_s, router_bias[l].astype(F32).reshape(N_EXPERTS, 1))
    a = n_total * TOP_K
    n_blocks = -(-(a + N_EXPERTS * (EXPERT_ROWS - 1)) // EXPERT_ROWS)
    pad_start, fill_start, fill_len, first_block, n_expert_blocks, n_used = _block_plan(counts.reshape(N_EXPERTS))
    dest_t = _dest(eidx_t, rank_t, pad_start.astype(F32).reshape(N_EXPERTS, 1))
    tiles = lambda a: a.reshape(-1, WORD_SLABS, LANES)
    shared = (w_gate_s[l].astype(BF16), w_up_s[l].astype(BF16), w_down_s[l].astype(BF16))
    x_sorted, y_part = _dispatch(fill_start, fill_len, n_used, dest_t, tiles(pk_p), tiles(pk_s), h_p, h_s, *shared,
                                 n_blocks=n_blocks)
    out_sorted = _experts(first_block, n_expert_blocks, n_used, x_sorted, w_gate_e[l], w_up_e[l], w_down_e[l])
    ln2 = (row_vec(ln2_g[l]), row_vec(ln2_b[l]))
    dest2, w2 = dest_t.T, w_t.T
    y_p = _combine(dest2, w2, y_part, *ln2, out_sorted, n=n_p, row_offset=0, name="combine_prompt")
    y_s = _combine(dest2, w2, y_part, *ln2, out_sorted, n=n_s, row_offset=n_p, name="combine_sample")
    return (y_p.reshape(bp, sp, D_MODEL), y_s.reshape(bs, ts, D_MODEL), new_kp, new_vp, new_ks, new_vs, new_gs)
```

```python
import functools

import jax
import jax.numpy as jnp
import numpy as np
from jax import lax
from jax.experimental import pallas as pl
from jax.experimental.pallas import tpu as pltpu

D_MODEL = 1024
HEAD_DIM = 64
ATT_HEADS = 8
KV_HEADS = 2
Q_PER_KV = ATT_HEADS // KV_HEADS
GM_HEADS = 8
ATT_WIDTH = ATT_HEADS * HEAD_DIM
KV_WIDTH = KV_HEADS * HEAD_DIM
GM_WIDTH = GM_HEADS * HEAD_DIM
ROPE_WIDTH = ATT_WIDTH + KV_WIDTH
IN_WIDTH = ATT_WIDTH + 2 * KV_WIDTH + 2 * GM_WIDTH
WINDOW = 128
CHUNK = 128
PAST_LEN = 16384
ROPE_THETA = 10000.0
ATT_SCALE = HEAD_DIM ** -0.5
N_EXPERTS = 256
TOP_K = 8
N_GROUPS = 8
TOPK_GROUPS = 4
F_EXPERT = 256
ROUTED_SCALE = 2.5
LN_EPS = 1e-5
DEPTH = 1
ALPHA = (2.0 * DEPTH) ** 0.25

LANES = 128
SLABS = 8
PACKED = D_MODEL // 2
WORD_SLABS = PACKED // LANES
ROW_TILE = 512
EXPERT_ROWS = 256
EXPERT_RING = 8
COMBINE_ROWS = 256
COMBINE_GROUP = 4
DISPATCH_ROWS = 256
ROUTE_TILE = 512
SAMPLE_SEQS_PER_STEP = 16
PROMPT_BLOCKS_PER_STEP = 2
DMA_UNROLL = 4
DMA_PRIORITIES = 2
VMEM_LIMIT = 56 * 1024 * 1024

F32 = jnp.float32
BF16 = jnp.bfloat16


def _params(n_axes):
    return pltpu.CompilerParams(dimension_semantics=("arbitrary",) * n_axes, vmem_limit_bytes=VMEM_LIMIT)


def _layer_norm(x, g, b):
    mu = jnp.mean(x, axis=-1, keepdims=True)
    xc = x - mu
    var = jnp.mean(xc * xc, axis=-1, keepdims=True)
    return xc * lax.rsqrt(var + LN_EPS) * g + b


def _pack_halves(x):
    xb = x.astype(BF16)
    lo = lax.bitcast_convert_type(xb[:, :PACKED].astype(F32), jnp.uint32)
    hi = lax.bitcast_convert_type(xb[:, PACKED:].astype(F32), jnp.uint32)
    return (lo >> 16) | hi


def _unpack_halves(w):
    lo = lax.bitcast_convert_type(w << 16, F32)
    hi = lax.bitcast_convert_type(w & jnp.uint32(0xFFFF0000), F32)
    return lo, hi


def _store_row_tiles(ref2d, x):
    m, slabs = x.shape[0], x.shape[1] // LANES
    for s in range(slabs):
        ref2d[pl.ds(s, m, stride=slabs), :] = x[:, s * LANES:(s + 1) * LANES]


def _load_row_tiles(ref2d, m, slabs):
    return jnp.concatenate([ref2d[pl.ds(s, m, stride=slabs), :] for s in range(slabs)], axis=1)


def _gelu(x):
    return 0.5 * x * (1.0 + lax.erf(x * np.float32(np.sqrt(0.5))))


def _in_proj_kernel(x_ref, w_ref, cos_ref, sa_ref, sb_ref, g_ref, b_ref, q_ref, k_ref, v_ref, gu_ref, gv_ref):
    x = x_ref[...].astype(BF16)
    zr = jnp.dot(x, w_ref[:, :ROPE_WIDTH], preferred_element_type=F32)
    pieces = []
    for c in range(ROPE_WIDTH // LANES):
        zc = zr[:, c * LANES:(c + 1) * LANES]
        tl = slice(0, LANES) if c < ATT_WIDTH // LANES else slice(LANES, 2 * LANES)
        pieces.append(zc * cos_ref[:, tl]
                      + pltpu.roll(zc, LANES - HEAD_DIM // 2, 1) * sa_ref[:, tl]
                      + pltpu.roll(zc, HEAD_DIM // 2, 1) * sb_ref[:, tl])
    for c in range(ATT_WIDTH // LANES):
        q_ref[:, c * LANES:(c + 1) * LANES] = pieces[c].astype(q_ref.dtype)
    k_ref[...] = pieces[ATT_WIDTH // LANES]
    v_ref[...] = jnp.dot(x, w_ref[:, ROPE_WIDTH:ROPE_WIDTH + KV_WIDTH], preferred_element_type=F32)
    g0 = ROPE_WIDTH + KV_WIDTH
    zu = jnp.dot(x, w_ref[:, g0:g0 + GM_WIDTH], preferred_element_type=F32)
    gu_ref[...] = _gelu(zu).astype(gu_ref.dtype)
    zv = jnp.dot(x, w_ref[:, g0 + GM_WIDTH:g0 + 2 * GM_WIDTH], preferred_element_type=F32)
    gv = _layer_norm(_gelu(zv), g_ref[...], b_ref[...])
    gv_ref[...] = gv.astype(gv_ref.dtype)


def _in_proj(x2, w_in_b, tabs, ln_g, ln_b, *, tm, gv_dtype, name):
    n = x2.shape[0]
    cos_t, sa_t, sb_t = tabs
    period = cos_t.shape[0] // tm
    row = lambda i: (i, 0)
    tab = lambda i: (i % period, 0)
    fixed = lambda i: (0, 0)
    return pl.pallas_call(
        _in_proj_kernel,
        grid=(n // tm,),
        in_specs=[
            pl.BlockSpec((tm, D_MODEL), row),
            pl.BlockSpec((D_MODEL, IN_WIDTH), fixed),
            pl.BlockSpec((tm, 2 * LANES), tab),
            pl.BlockSpec((tm, 2 * LANES), tab),
            pl.BlockSpec((tm, 2 * LANES), tab),
            pl.BlockSpec((1, GM_WIDTH), fixed),
            pl.BlockSpec((1, GM_WIDTH), fixed),
        ],
        out_specs=[
            pl.BlockSpec((tm, ATT_WIDTH), row),
            pl.BlockSpec((tm, KV_WIDTH), row),
            pl.BlockSpec((tm, KV_WIDTH), row),
            pl.BlockSpec((tm, GM_WIDTH), row),
            pl.BlockSpec((tm, GM_WIDTH), row),
        ],
        out_shape=[
            jax.ShapeDtypeStruct((n, ATT_WIDTH), BF16),
            jax.ShapeDtypeStruct((n, KV_WIDTH), F32),
            jax.ShapeDtypeStruct((n, KV_WIDTH), F32),
            jax.ShapeDtypeStruct((n, GM_WIDTH), BF16),
            jax.ShapeDtypeStruct((n, GM_WIDTH), gv_dtype),
        ],
        compiler_params=_params(1),
        name=name,
    )(x2, w_in_b, cos_t, sa_t, sb_t, ln_g, ln_b)


def _rope_tables(pos):
    half = HEAD_DIM // 2
    lane = jnp.arange(2 * LANES, dtype=jnp.int32)
    inv = ROPE_THETA ** (-(lane % half).astype(F32) * 2.0 / HEAD_DIM)
    ang = pos.astype(F32)[:, None] * inv[None, :]
    scale = jnp.where(lane < LANES, ATT_SCALE, 1.0).astype(F32)[None, :]
    first_half = ((lane % HEAD_DIM) < half)[None, :]
    cos, sin = jnp.cos(ang) * scale, jnp.sin(ang) * scale
    return cos, jnp.where(first_half, -sin, 0.0), jnp.where(first_half, 0.0, sin)


def _attn_kernel(sink_ref, q_ref, kc_ref, vc_ref, kp_ref, vp_ref, o_ref, *, tq, seqs, chain, stack,
                 first_block_has_no_prev):
    nk = WINDOW + tq
    rows = stack * tq
    qi = lax.broadcasted_iota(jnp.int32, (rows, nk), 0) & (tq - 1)
    ks = lax.broadcasted_iota(jnp.int32, (rows, nk), 1)
    band = (ks >= qi) & (ks <= qi + WINDOW)
    first_mask = band & ((pl.program_id(1) > 0) | (ks >= WINDOW)) if first_block_has_no_prev else band
    sinks = [jnp.concatenate([jnp.full((tq, 1), sink_ref[h0 + j], F32) for j in range(stack)], axis=0)
             for h0 in range(0, ATT_HEADS, stack)]
    for b in range(seqs * chain):
        qrows = slice(b * tq, (b + 1) * tq)
        if chain > 1 and b > 0:
            kprev, vprev = kc_ref[(b - 1) * tq:b * tq, :], vc_ref[(b - 1) * tq:b * tq, :]
        else:
            kprev, vprev = kp_ref[b * WINDOW:(b + 1) * WINDOW, :], vp_ref[b * WINDOW:(b + 1) * WINDOW, :]
        mask = band if (chain > 1 and b > 0) else first_mask
        q = q_ref[qrows, :]
        kk = jnp.concatenate([kprev, kc_ref[qrows, :]], axis=0).astype(BF16)
        vv = jnp.concatenate([vprev, vc_ref[qrows, :]], axis=0).astype(BF16)
        outs = []
        for i, h0 in enumerate(range(0, ATT_HEADS, stack)):
            g = h0 // Q_PER_KV
            kg = kk[:, g * HEAD_DIM:(g + 1) * HEAD_DIM]
            vg = vv[:, g * HEAD_DIM:(g + 1) * HEAD_DIM]
            qg = jnp.concatenate([q[:, (h0 + j) * HEAD_DIM:(h0 + j + 1) * HEAD_DIM] for j in range(stack)], axis=0)
            s = lax.dot_general(qg, kg, (((1,), (1,)), ((), ())), preferred_element_type=F32)
            s = jnp.where(mask, s, -jnp.inf)
            m = jnp.maximum(jnp.max(s, axis=-1, keepdims=True), sinks[i])
            p = jnp.exp(s - m)
            denom = jnp.sum(p, axis=-1, keepdims=True) + jnp.exp(sinks[i] - m)
            o = jnp.dot((p / denom).astype(BF16), vg, preferred_element_type=F32)
            outs.extend(o[j * tq:(j + 1) * tq, :] for j in range(stack))
        o_ref[qrows, :] = jnp.concatenate(outs, axis=1).astype(o_ref.dtype)


def _attention(sink, q2, k2, v2, kprev2, vprev2, *, batch, nb, tq, seqs, chain, prev_blocks, first_block_has_no_prev,
               name):
    assert tq & (tq - 1) == 0 and batch % seqs == 0 and (seqs == 1 or nb == prev_blocks == 1)
    assert (seqs == 1 or chain == 1) and nb % chain == 0 and (chain == 1 or tq == WINDOW)
    steps = nb // chain
    cur = lambda b, n, s: (b * steps + n, 0)
    prev = lambda b, n, s: (b * prev_blocks + jnp.maximum(n * chain - 1, 0), 0)
    stack = Q_PER_KV if Q_PER_KV * tq <= WINDOW else 1
    kern = functools.partial(_attn_kernel, tq=tq, seqs=seqs, chain=chain, stack=stack,
                             first_block_has_no_prev=first_block_has_no_prev)
    return pl.pallas_call(
        kern,
        grid_spec=pltpu.PrefetchScalarGridSpec(
            num_scalar_prefetch=1,
            grid=(batch // seqs, steps),
            in_specs=[
                pl.BlockSpec((seqs * chain * tq, ATT_WIDTH), cur),
                pl.BlockSpec((seqs * chain * tq, KV_WIDTH), cur),
                pl.BlockSpec((seqs * chain * tq, KV_WIDTH), cur),
                pl.BlockSpec((seqs * WINDOW, KV_WIDTH), prev),
                pl.BlockSpec((seqs * WINDOW, KV_WIDTH), prev),
            ],
            out_specs=pl.BlockSpec((seqs * chain * tq, ATT_WIDTH), cur),
        ),
        out_shape=jax.ShapeDtypeStruct(q2.shape, BF16),
        compiler_params=_params(2),
        name=name,
    )(sink, q2, k2, v2, kprev2, vprev2)


def _gate_kernel(gu_ref, gv_ref, w_ref, b_ref, o_ref, *, chunk, n_chunks):
    ri = lax.broadcasted_iota(jnp.int32, (chunk, chunk), 0)
    ci = lax.broadcasted_iota(jnp.int32, (chunk, chunk), 1)
    ws = [jnp.where(ci <= ri, w_ref[h], 0.0).astype(BF16) for h in range(GM_HEADS)]
    for c in range(n_chunks):
        rows = slice(c * chunk, (c + 1) * chunk)
        gv = gv_ref[rows, :].astype(BF16)
        sv = jnp.concatenate(
            [jnp.dot(ws[h], gv[:, h * HEAD_DIM:(h + 1) * HEAD_DIM], preferred_element_type=F32)
             for h in range(GM_HEADS)], axis=1)
        o_ref[rows, :] = (gu_ref[rows, :].astype(F32) * (sv + b_ref[...])).astype(o_ref.dtype)


def _gate(gu2, gv2, w_s, b_tab, *, chunk, n_chunks, name):
    n = gu2.shape[0]
    tm = chunk * n_chunks
    row = lambda i: (i, 0)
    kern = functools.partial(_gate_kernel, chunk=chunk, n_chunks=n_chunks)
    return pl.pallas_call(
        kern,
        grid=(n // tm,),
        in_specs=[
            pl.BlockSpec((tm, GM_WIDTH), row),
            pl.BlockSpec((tm, GM_WIDTH), row),
            pl.BlockSpec((GM_HEADS, chunk, chunk), lambda i: (0, 0, 0)),
            pl.BlockSpec((chunk, GM_WIDTH), lambda i: (0, 0)),
        ],
        out_specs=pl.BlockSpec((tm, GM_WIDTH), row),
        out_shape=jax.ShapeDtypeStruct((n, GM_WIDTH), BF16),
        compiler_params=_params(1),
        name=name,
    )(gu2, gv2, w_s, b_tab)


def _gate_short_kernel(gu_ref, gv_ref, coef_ref, b_ref, o_ref, *, length):
    n = gv_ref.shape[0]
    gv = gv_ref[...].astype(F32)
    tiles = lambda x: x.reshape(n // SLABS, SLABS, GM_WIDTH)
    sv = tiles(gv) * coef_ref[0][None] + b_ref[...][None]
    for d in range(1, length):
        sv = sv + tiles(pltpu.roll(gv, d, 0)) * coef_ref[d][None]
    o_ref[...] = (gu_ref[...].astype(F32) * sv.reshape(n, GM_WIDTH)).astype(o_ref.dtype)


def _gate_short(gu2, gv2, w_s, b_s, *, length, name):
    n = gu2.shape[0]
    assert SLABS % length == 0 and n % SLABS == 0
    pos = jnp.arange(length)
    lag = jnp.arange(length)
    src = pos[None, :] - lag[:, None]
    coef = jnp.where(src >= 0, w_s[:, pos[None, :], jnp.maximum(src, 0)], 0.0)
    coef = jnp.repeat(coef.transpose(1, 2, 0), HEAD_DIM, axis=2)
    coef = jnp.tile(coef, (1, SLABS // length, 1))
    bias = jnp.tile(jnp.repeat(b_s[:, :length].T, HEAD_DIM, axis=1), (SLABS // length, 1))
    whole = lambda shape: pl.BlockSpec(shape, lambda i: (0,) * len(shape))
    return pl.pallas_call(
        functools.partial(_gate_short_kernel, length=length),
        grid=(1,),
        in_specs=[whole((n, GM_WIDTH)), whole((n, GM_WIDTH)), whole((length, SLABS, GM_WIDTH)),
                  whole((SLABS, GM_WIDTH))],
        out_specs=whole((n, GM_WIDTH)),
        out_shape=jax.ShapeDtypeStruct((n, GM_WIDTH), BF16),
        compiler_params=_params(1),
        name=name,
    )(gu2, gv2, coef, bias)


def _out_proj_kernel(att_ref, gm_ref, x_ref, wo_ref, g_ref, b_ref, rh_ref, rl_ref, h_ref, hp_ref, lg_ref):
    mix = jnp.dot(att_ref[...], wo_ref[:ATT_WIDTH, :], preferred_element_type=F32)
    mix = mix + jnp.dot(gm_ref[...], wo_ref[ATT_WIDTH:, :], preferred_element_type=F32)
    h = _layer_norm(ALPHA * x_ref[...] + mix, g_ref[...], b_ref[...])
    h_ref[...] = h
    _store_row_tiles(hp_ref, _pack_halves(h))
    h_hi = h.astype(BF16)
    h_lo = (h - h_hi.astype(F32)).astype(BF16)
    nt = (((1,), (1,)), ((), ()))
    lg = lax.dot_general(rh_ref[...], h_hi, nt, preferred_element_type=F32)
    lg = lg + lax.dot_general(rh_ref[...], h_lo, nt, preferred_element_type=F32)
    lg = lg + lax.dot_general(rl_ref[...], h_hi, nt, preferred_element_type=F32)
    lg_ref[...] = lg


def _out_proj(att2, gm2, x2, w_out_b, ln_g, ln_b, r_hi, r_lo, *, name):
    n = x2.shape[0]
    tm = ROW_TILE
    row = lambda i: (i, 0)
    fixed = lambda i: (0, 0)
    return pl.pallas_call(
        _out_proj_kernel,
        grid=(n // tm,),
        in_specs=[
            pl.BlockSpec((tm, ATT_WIDTH), row),
            pl.BlockSpec((tm, GM_WIDTH), row),
            pl.BlockSpec((tm, D_MODEL), row),
            pl.BlockSpec((D_MODEL, D_MODEL), fixed),
            pl.BlockSpec((1, D_MODEL), fixed),
            pl.BlockSpec((1, D_MODEL), fixed),
            pl.BlockSpec((N_EXPERTS, D_MODEL), fixed),
            pl.BlockSpec((N_EXPERTS, D_MODEL), fixed),
        ],
        out_specs=[
            pl.BlockSpec((tm, D_MODEL), row),
            pl.BlockSpec((tm * WORD_SLABS, LANES), row),
            pl.BlockSpec((N_EXPERTS, tm), lambda i: (0, i)),
        ],
        out_shape=[
            jax.ShapeDtypeStruct((n, D_MODEL), F32),
            jax.ShapeDtypeStruct((n * WORD_SLABS, LANES), jnp.uint32),
            jax.ShapeDtypeStruct((N_EXPERTS, n), F32),
        ],
        compiler_params=_params(1),
        name=name,
    )(att2, gm2, x2, w_out_b, ln_g, ln_b, r_hi, r_lo)


def _experts_kernel(first_ref, nblk_ref, nu_ref, x_hbm, wg_ref, wu_ref, wd_ref, o_hbm, xbuf, obuf, wg_s, wu_s, wd_s,
                    in_sem, out_sem, fill_sem, *, n_blocks):
    e = pl.program_id(0)
    nb = nblk_ref[e]
    b0 = first_ref[e]
    n_used = nu_ref[0]
    ahead = EXPERT_RING // 2

    def rows_of(block):
        return pl.ds(pl.multiple_of(block * EXPERT_ROWS, EXPERT_ROWS), EXPERT_ROWS)

    def slot_of(block):
        return block & (EXPERT_RING - 1)

    def slab(s):
        return pl.ds(s * LANES, LANES)

    def in_copies(block):
        slot = slot_of(block)
        return [pltpu.make_async_copy(x_hbm.at[rows_of(block), s, :], xbuf.at[slot, :, slab(s)], in_sem.at[slot])
                for s in range(WORD_SLABS)]

    def out_copies(block, slot=None, sem=None):
        slot = slot_of(block) if slot is None else slot
        sem = out_sem.at[slot] if sem is None else sem
        return [pltpu.make_async_copy(obuf.at[slot, :, slab(s)], o_hbm.at[rows_of(block), s, :], sem)
                for s in range(WORD_SLABS)]

    def start(copies):
        for cp in copies:
            cp.start()

    def wait(copies):
        for cp in copies:
            cp.wait()

    @pl.when(e == 0)
    def _():
        for j in range(ahead):
            pl.when(j < n_used)(lambda j=j: start(in_copies(j)))

    def process(block, count):
        blocks = [block + j for j in range(count)]
        for blk in blocks:
            wait(in_copies(blk))
        x = [xbuf[slot_of(blk)] for blk in blocks]
        xl, xh = _unpack_halves(x[0] if count == 1 else jnp.concatenate(x, axis=0))
        xl, xh = xl.astype(BF16), xh.astype(BF16)
        for blk in blocks:
            pl.when(blk + ahead < n_used)(lambda blk=blk: start(in_copies(blk + ahead)))
            pl.when(blk >= ahead)(lambda blk=blk: wait(out_copies(blk - ahead)))
        a = (jnp.dot(xl, wg_s[:PACKED, :], preferred_element_type=F32)
             + jnp.dot(xh, wg_s[PACKED:, :], preferred_element_type=F32))
        u = (jnp.dot(xl, wu_s[:PACKED, :], preferred_element_type=F32)
             + jnp.dot(xh, wu_s[PACKED:, :], preferred_element_type=F32))
        hb = (a * jax.nn.sigmoid(a) * u).astype(BF16)
        o = _pack_halves(jnp.dot(hb, wd_s[...], preferred_element_type=F32))
        for j, blk in enumerate(blocks):
            obuf[slot_of(blk)] = o[j * EXPERT_ROWS:(j + 1) * EXPERT_ROWS, :]
            start(out_copies(blk))

    @pl.when(nb > 0)
    def _():
        wg_s[...] = wg_ref[0].astype(BF16)
        wu_s[...] = wu_ref[0].astype(BF16)
        wd_s[...] = wd_ref[0].astype(BF16)

        def pair(i, carry):
            process(b0 + 2 * i, 2)
            return carry

        lax.fori_loop(0, nb // 2, pair, 0)
        pl.when(nb % 2 == 1)(lambda: process(b0 + nb - 1, 1))

    @pl.when(e == N_EXPERTS - 1)
    def _():
        for j in range(ahead):
            pl.when(n_used - 1 - j >= 0)(lambda j=j: wait(out_copies(n_used - 1 - j)))
        obuf[0] = jnp.zeros((EXPERT_ROWS, PACKED), jnp.uint32)

        def on_unused_blocks(fn):
            def body(b, c):
                fn(out_copies(b, slot=0, sem=fill_sem))
                return c
            lax.fori_loop(n_used, n_blocks, body, 0)

        on_unused_blocks(start)
        on_unused_blocks(wait)


def _experts(first_block, n_expert_blocks, n_used, x_sorted, w_gate_e, w_up_e, w_down_e):
    rows = x_sorted.shape[0]
    wmap = lambda e, *_: (e, 0, 0)
    kern = functools.partial(_experts_kernel, n_blocks=rows // EXPERT_ROWS)
    return pl.pallas_call(
        kern,
        grid_spec=pltpu.PrefetchScalarGridSpec(
            num_scalar_prefetch=3,
            grid=(N_EXPERTS,),
            in_specs=[
                pl.BlockSpec(memory_space=pl.ANY),
                pl.BlockSpec((1, D_MODEL, F_EXPERT), wmap),
                pl.BlockSpec((1, D_MODEL, F_EXPERT), wmap),
                pl.BlockSpec((1, F_EXPERT, D_MODEL), wmap),
            ],
            out_specs=pl.BlockSpec(memory_space=pl.ANY),
            scratch_shapes=[
                pltpu.VMEM((EXPERT_RING, EXPERT_ROWS, PACKED), jnp.uint32),
                pltpu.VMEM((EXPERT_RING, EXPERT_ROWS, PACKED), jnp.uint32),
                pltpu.VMEM((D_MODEL, F_EXPERT), BF16),
                pltpu.VMEM((D_MODEL, F_EXPERT), BF16),
                pltpu.VMEM((F_EXPERT, D_MODEL), BF16),
                pltpu.SemaphoreType.DMA((EXPERT_RING,)),
                pltpu.SemaphoreType.DMA((EXPERT_RING,)),
                pltpu.SemaphoreType.DMA,
            ],
        ),
        out_shape=jax.ShapeDtypeStruct(x_sorted.shape, x_sorted.dtype),
        compiler_params=_params(1),
        name="experts",
    )(first_block, n_expert_blocks, n_used, x_sorted, w_gate_e, w_up_e, w_down_e)


def _combine_kernel(d_ref, w_ref, yp_ref, ln_g_ref, ln_b_ref, os_ref, y_ref, gbuf0, gbuf1, rlo, rhi, sem):
    s = pl.program_id(0)
    n_tiles = pl.num_programs(0) - 1
    t = y_ref.shape[0]
    gtiles = [g.reshape(t * TOP_K, WORD_SLABS, LANES) for g in (gbuf0, gbuf1)]
    lo_tiles = rlo.reshape(t, WORD_SLABS, LANES)
    hi_tiles = rhi.reshape(t, WORD_SLABS, LANES)

    def row_copy(row, buf, j, k):
        return pltpu.make_async_copy(os_ref.at[pl.ds(row, 1)], gtiles[buf].at[pl.ds(j * TOP_K + k, 1)], sem.at[buf])

    def drain(buf):
        def body(j, c):
            for k in range(TOP_K):
                row_copy(0, buf, j, k).wait()
            return c
        lax.fori_loop(0, t, body, 0, unroll=DMA_UNROLL)

    def weighted_sum(rows, j):
        lo, hi = _unpack_halves(rows)
        acc_lo = w_ref[j, 0] * lo[0:1]
        acc_hi = w_ref[j, 0] * hi[0:1]
        for k in range(1, TOP_K):
            acc_lo = acc_lo + w_ref[j, k] * lo[k:k + 1]
            acc_hi = acc_hi + w_ref[j, k] * hi[k:k + 1]
        lo_tiles[pl.ds(j, 1)] = acc_lo
        hi_tiles[pl.ds(j, 1)] = acc_hi

    def token_loop(new, old, start_new, sum_old):
        def body(g, c):
            tokens = [g * COMBINE_GROUP + u for u in range(COMBINE_GROUP)]
            rows = [gtiles[old][pl.ds(j * TOP_K, TOP_K)] for j in tokens] if sum_old else []
            if start_new:
                for j in tokens:
                    for k in range(TOP_K):
                        row_copy(d_ref[j, k], new, j, k).start(priority=k % DMA_PRIORITIES)
            for r, j in zip(rows, tokens):
                weighted_sum(r, j)
            return c
        lax.fori_loop(0, t // COMBINE_GROUP, body, 0)

    def step(new):
        old = 1 - new
        pl.when(s == 0)(lambda: token_loop(new, old, True, False))

        @pl.when(s >= 1)
        def _():
            drain(old)
            pl.when(s < n_tiles)(lambda: token_loop(new, old, True, True))
            pl.when(s == n_tiles)(lambda: token_loop(new, old, False, True))
            routed = jnp.concatenate([_load_row_tiles(rlo, t, WORD_SLABS), _load_row_tiles(rhi, t, WORD_SLABS)],
                                     axis=1)
            y_ref[...] = _layer_norm(yp_ref[...] + routed, ln_g_ref[...], ln_b_ref[...])

    pl.when(s % 2 == 0)(lambda: step(0))
    pl.when(s % 2 == 1)(lambda: step(1))


def _combine(dest2, w2, y_part, ln_g, ln_b, out_sorted, *, n, row_offset, name):
    tm = COMBINE_ROWS
    off = row_offset // tm
    n_tiles = n // tm
    fixed = lambda s: (0, 0)
    new_tile = lambda s: (jnp.minimum(s, n_tiles - 1) + off, 0)
    old_tile = lambda s: (jnp.maximum(s - 1, 0) + off, 0)
    return pl.pallas_call(
        _combine_kernel,
        grid=(n_tiles + 1,),
        in_specs=[
            pl.BlockSpec((tm, TOP_K), new_tile, memory_space=pltpu.SMEM),
            pl.BlockSpec((tm, TOP_K), old_tile, memory_space=pltpu.SMEM),
            pl.BlockSpec((tm, D_MODEL), old_tile),
            pl.BlockSpec((1, D_MODEL), fixed),
            pl.BlockSpec((1, D_MODEL), fixed),
            pl.BlockSpec(memory_space=pl.ANY),
        ],
        out_specs=pl.BlockSpec((tm, D_MODEL), lambda s: (jnp.maximum(s - 1, 0), 0)),
        out_shape=jax.ShapeDtypeStruct((n, D_MODEL), F32),
        scratch_shapes=[pltpu.VMEM((TOP_K * tm * WORD_SLABS, LANES), jnp.uint32),
                        pltpu.VMEM((TOP_K * tm * WORD_SLABS, LANES), jnp.uint32),
                        pltpu.VMEM((tm * WORD_SLABS, LANES), F32), pltpu.VMEM((tm * WORD_SLABS, LANES), F32),
                        pltpu.SemaphoreType.DMA((2,))],
        compiler_params=_params(1),
        name=name,
    )(dest2, w2, y_part, ln_g, ln_b, out_sorted)


def _route_kernel(lgp_ref, lgs_ref, bias_ref, eidx_ref, w_ref, rank_ref, cnt_ref, carry_ref, *, prompt_tiles):
    @pl.when(pl.program_id(0) == 0)
    def _():
        carry_ref[...] = jnp.zeros_like(carry_ref)

    t = lgp_ref.shape[1]
    gsz = N_EXPERTS // N_GROUPS
    neg = -jnp.inf
    s = jax.nn.sigmoid(jnp.where(pl.program_id(0) < prompt_tiles, lgp_ref[...], lgs_ref[...]))
    biased = s + bias_ref[...]
    io_g = lax.broadcasted_iota(jnp.int32, (gsz, t), 0)
    grp_rows = []
    for g in range(N_GROUPS):
        blk = biased[g * gsz:(g + 1) * gsz, :]
        m1 = jnp.max(blk, axis=0, keepdims=True)
        i1 = jnp.min(jnp.where(blk == m1, io_g, gsz), axis=0, keepdims=True)
        m2 = jnp.max(jnp.where(io_g == i1, neg, blk), axis=0, keepdims=True)
        grp_rows.append(m1 + m2)
    gs = jnp.concatenate(grp_rows, axis=0)
    io8 = lax.broadcasted_iota(jnp.int32, (N_GROUPS, t), 0)
    gsel = jnp.zeros((N_GROUPS, t), jnp.int32)
    for _ in range(TOPK_GROUPS):
        m = jnp.max(gs, axis=0, keepdims=True)
        gi = jnp.min(jnp.where(gs == m, io8, N_GROUPS), axis=0, keepdims=True)
        hit = io8 == gi
        gsel = jnp.where(hit, 1, gsel)
        gs = jnp.where(hit, neg, gs)
    masked = jnp.concatenate(
        [jnp.where(gsel[g:g + 1, :] > 0, biased[g * gsz:(g + 1) * gsz, :], neg) for g in range(N_GROUPS)], axis=0)

    eio = lax.broadcasted_iota(jnp.int32, (N_EXPERTS, t), 0)
    cur = masked
    idx_rows, w_rows = [], []
    for _ in range(TOP_K):
        m = jnp.max(cur, axis=0, keepdims=True)
        idx = jnp.min(jnp.where(cur == m, eio, N_EXPERTS), axis=0, keepdims=True)
        hit = eio == idx
        w_rows.append(jnp.sum(jnp.where(hit, s, 0.0), axis=0, keepdims=True))
        cur = jnp.where(hit, neg, cur)
        idx_rows.append(idx)
    sel = jnp.where(cur != masked, 1.0, 0.0)

    tri = jnp.where(lax.broadcasted_iota(jnp.int32, (t, t), 0) < lax.broadcasted_iota(jnp.int32, (t, t), 1), 1.0, 0.0)
    pref = jnp.dot(sel.astype(BF16), tri.astype(BF16), preferred_element_type=F32) + carry_ref[...]
    rank_rows = [jnp.sum(jnp.where(eio == idx_rows[k], pref, 0.0), axis=0, keepdims=True) for k in range(TOP_K)]
    carry_ref[...] += jnp.sum(sel, axis=1, keepdims=True)

    wk = jnp.concatenate(w_rows, axis=0)
    eidx_ref[...] = jnp.concatenate(idx_rows, axis=0)
    w_ref[...] = wk / jnp.sum(wk, axis=0, keepdims=True) * ROUTED_SCALE
    rank_ref[...] = jnp.concatenate(rank_rows, axis=0).astype(jnp.int32)
    cnt_ref[...] = carry_ref[...].astype(jnp.int32)


def _route(logits_p, logits_s, bias_col):
    t = ROUTE_TILE
    prompt_tiles = logits_p.shape[1] // t
    n = logits_p.shape[1] + logits_s.shape[1]
    col = lambda i: (0, i)
    fixed = lambda i: (0, 0)
    kern = functools.partial(_route_kernel, prompt_tiles=prompt_tiles)
    return pl.pallas_call(
        kern,
        grid=(n // t,),
        in_specs=[pl.BlockSpec((N_EXPERTS, t), lambda i: (0, jnp.minimum(i, prompt_tiles - 1))),
                  pl.BlockSpec((N_EXPERTS, t), lambda i: (0, jnp.maximum(i - prompt_tiles, 0))),
                  pl.BlockSpec((N_EXPERTS, 1), fixed)],
        out_specs=[
            pl.BlockSpec((TOP_K, t), col),
            pl.BlockSpec((TOP_K, t), col),
            pl.BlockSpec((TOP_K, t), col),
            pl.BlockSpec((N_EXPERTS, 1), fixed),
        ],
        out_shape=[
            jax.ShapeDtypeStruct((TOP_K, n), jnp.int32),
            jax.ShapeDtypeStruct((TOP_K, n), F32),
            jax.ShapeDtypeStruct((TOP_K, n), jnp.int32),
            jax.ShapeDtypeStruct((N_EXPERTS, 1), jnp.int32),
        ],
        scratch_shapes=[pltpu.VMEM((N_EXPERTS, 1), F32)],
        compiler_params=_params(1),
        name="route",
    )(logits_p, logits_s, bias_col)


def _dest_kernel(eidx_ref, rank_ref, start_ref, dest_ref):
    t = eidx_ref.shape[1]
    eio = lax.broadcasted_iota(jnp.int32, (N_EXPERTS, t), 0)
    start = start_ref[...]
    rows = [jnp.sum(jnp.where(eio == eidx_ref[k:k + 1, :], start, 0.0), axis=0, keepdims=True) for k in range(TOP_K)]
    dest_ref[...] = jnp.concatenate(rows, axis=0).astype(jnp.int32) + rank_ref[...]


def _dest(eidx_t, rank_t, pad_start_col):
    n = eidx_t.shape[1]
    t = ROW_TILE
    col = lambda i: (0, i)
    return pl.pallas_call(
        _dest_kernel,
        grid=(n // t,),
        in_specs=[pl.BlockSpec((TOP_K, t), col), pl.BlockSpec((TOP_K, t), col),
                  pl.BlockSpec((N_EXPERTS, 1), lambda i: (0, 0))],
        out_specs=pl.BlockSpec((TOP_K, t), col),
        out_shape=jax.ShapeDtypeStruct((TOP_K, n), jnp.int32),
        compiler_params=_params(1),
        name="dest",
    )(eidx_t, rank_t, pad_start_col)


def _dispatch_kernel(fill_ref, len_ref, nu_ref, dest_ref, pp_ref, ps_ref, hp_ref, hs_ref, wg_ref, wu_ref, wd_ref,
                     xs_ref, yp_ref, zbuf, sem, fill_sem, *, prompt_tiles, n_blocks):
    i = pl.program_id(0)

    @pl.when(i == 0)
    def _():
        zbuf[...] = jnp.zeros_like(zbuf)

        def fill_copy(row0, size):
            return pltpu.make_async_copy(zbuf.at[pl.ds(0, size)], xs_ref.at[pl.ds(row0, size)], fill_sem)

        def on_padding(fn):
            def body(e, c):
                base, length = fill_ref[e], len_ref[e]
                size = EXPERT_ROWS // 2
                while size >= 1:
                    piece = fill_copy(base + (length & ~(2 * size - 1)), size)
                    pl.when((length & size) != 0)(functools.partial(fn, piece))
                    size //= 2
                return c
            lax.fori_loop(0, N_EXPERTS, body, 0)

        def on_unused_blocks(fn):
            lax.fori_loop(nu_ref[0], n_blocks, lambda b, c: (fn(fill_copy(b * EXPERT_ROWS, EXPERT_ROWS)), c)[1], 0)

        on_padding(lambda cp: cp.start())
        on_unused_blocks(lambda cp: cp.start())
        on_padding(lambda cp: cp.wait())
        on_unused_blocks(lambda cp: cp.wait())

    def tile_step(src_ref, h_ref):
        t = src_ref.shape[0]

        def row_copy(j, k):
            return pltpu.make_async_copy(src_ref.at[pl.ds(j, 1)], xs_ref.at[pl.ds(dest_ref[k, j], 1)], sem)

        def issue(j, c):
            for k in range(TOP_K):
                row_copy(j, k).start(priority=k % DMA_PRIORITIES)
            return c

        def drain(j, c):
            for k in range(TOP_K):
                row_copy(j, k).wait()
            return c

        lax.fori_loop(0, t, issue, 0, unroll=DMA_UNROLL)
        h = h_ref[...]
        hb = h.astype(BF16)
        a = jnp.dot(hb, wg_ref[...], preferred_element_type=F32)
        u = jnp.dot(hb, wu_ref[...], preferred_element_type=F32)
        shared = jnp.dot((a * jax.nn.sigmoid(a) * u).astype(BF16), wd_ref[...], preferred_element_type=F32)
        yp_ref[...] = ALPHA * h + shared
        lax.fori_loop(0, t, drain, 0, unroll=DMA_UNROLL)

    @pl.when(i < prompt_tiles)
    def _():
        tile_step(pp_ref, hp_ref)

    @pl.when(i >= prompt_tiles)
    def _():
        tile_step(ps_ref, hs_ref)


def _dispatch(fill_start, fill_len, n_used, dest_t, pk_p, pk_s, h_p, h_s, wg_b, wu_b, wd_b, *, n_blocks):
    t = DISPATCH_ROWS
    prompt_tiles = h_p.shape[0] // t
    sample_tiles = h_s.shape[0] // t
    tile = (t, WORD_SLABS, LANES)
    fixed = lambda i, *_: (0, 0)
    p_idx = lambda i: jnp.minimum(i, prompt_tiles - 1)
    s_idx = lambda i: jnp.maximum(i - prompt_tiles, 0)
    kern = functools.partial(_dispatch_kernel, prompt_tiles=prompt_tiles, n_blocks=n_blocks)
    return pl.pallas_call(
        kern,
        grid_spec=pltpu.PrefetchScalarGridSpec(
            num_scalar_prefetch=3,
            grid=(prompt_tiles + sample_tiles,),
            in_specs=[
                pl.BlockSpec((TOP_K, t), lambda i, *_: (0, i), memory_space=pltpu.SMEM),
                pl.BlockSpec(tile, lambda i, *_: (p_idx(i), 0, 0)),
                pl.BlockSpec(tile, lambda i, *_: (s_idx(i), 0, 0)),
                pl.BlockSpec((t, D_MODEL), lambda i, *_: (p_idx(i), 0)),
                pl.BlockSpec((t, D_MODEL), lambda i, *_: (s_idx(i), 0)),
                pl.BlockSpec((D_MODEL, F_EXPERT), fixed),
                pl.BlockSpec((D_MODEL, F_EXPERT), fixed),
                pl.BlockSpec((F_EXPERT, D_MODEL), fixed),
            ],
            out_specs=[pl.BlockSpec(memory_space=pl.ANY), pl.BlockSpec((t, D_MODEL), lambda i, *_: (i, 0))],
            scratch_shapes=[pltpu.VMEM((EXPERT_ROWS, WORD_SLABS, LANES), jnp.uint32), pltpu.SemaphoreType.DMA,
                            pltpu.SemaphoreType.DMA],
        ),
        out_shape=[jax.ShapeDtypeStruct((n_blocks * EXPERT_ROWS, WORD_SLABS, LANES), jnp.uint32),
                   jax.ShapeDtypeStruct(((prompt_tiles + sample_tiles) * t, D_MODEL), F32)],
        compiler_params=_params(1),
        name="dispatch",
    )(fill_start, fill_len, n_used, dest_t, pk_p, pk_s, h_p, h_s, wg_b, wu_b, wd_b)


def _block_plan(counts):
    padded = (counts + EXPERT_ROWS - 1) // EXPERT_ROWS * EXPERT_ROWS
    pad_end = jnp.cumsum(padded).astype(jnp.int32)
    pad_start = pad_end - padded
    n_used = pad_end[-1] // EXPERT_ROWS
    fill_start = pad_start + counts
    fill_len = pad_end - fill_start
    first_block = pad_start // EXPERT_ROWS
    n_expert_blocks = (padded // EXPERT_ROWS).astype(jnp.int32)
    return pad_start, fill_start, fill_len, first_block, n_expert_blocks, n_used.reshape(1).astype(jnp.int32)


def kernel(x_prompt, x_sample, cache_k, cache_v, w_in, sink, gm_ln_g, gm_ln_b, gm_w_s, gm_b_s, w_out, ln1_g, ln1_b,
           router_w, router_bias, w_gate_e, w_up_e, w_down_e, w_gate_s, w_up_s, w_down_s, ln2_g, ln2_b):
    bp, sp = x_prompt.shape[:2]
    bs, ts = x_sample.shape[:2]
    r = cache_k.shape[2]
    assert r == WINDOW and sp % ROW_TILE == 0 and (bs * ts) % ROW_TILE == 0
    n_p, n_s = bp * sp, bs * ts
    n_total = n_p + n_s
    l = 0

    w_in_b = w_in[l].astype(BF16)
    w_out_b = w_out[l].astype(BF16)
    router_t = router_w[l].T
    r_hi = router_t.astype(BF16)
    r_lo = (router_t - r_hi.astype(F32)).astype(BF16)
    row_vec = lambda v: v.reshape(1, -1)
    gm_g, gm_b = row_vec(gm_ln_g[l]), row_vec(gm_ln_b[l])
    sink_l = sink[l].astype(F32)

    xp2 = x_prompt.reshape(n_p, D_MODEL)
    tabs_p = _rope_tables(jnp.arange(sp, dtype=jnp.int32))
    q, k, v, gu, gv = _in_proj(xp2, w_in_b, tabs_p, gm_g, gm_b, tm=ROW_TILE, gv_dtype=BF16, name="in_proj_prompt")
    nb = sp // WINDOW
    att = _attention(sink_l, q, k, v, k, v, batch=bp, nb=nb, tq=WINDOW, seqs=1, chain=PROMPT_BLOCKS_PER_STEP,
                     prev_blocks=nb,
                     first_block_has_no_prev=True, name="attn_prompt")
    b_tab_p = jnp.repeat(gm_b_s[l].T, HEAD_DIM, axis=1)
    gm = _gate(gu, gv, gm_w_s[l], b_tab_p, chunk=CHUNK, n_chunks=ROW_TILE // CHUNK, name="gate_prompt")
    last_rows = lambda t: t.reshape(bp, sp, KV_WIDTH)[:, sp - r:, :].reshape(1, bp, r, KV_HEADS, HEAD_DIM)
    new_kp, new_vp = last_rows(k), last_rows(v)
    h_p, pk_p, lt_p = _out_proj(att, gm, xp2, w_out_b, row_vec(ln1_g[l]), row_vec(ln1_b[l]), r_hi, r_lo,
                          name="out_proj_prompt")

    xs2 = x_sample.reshape(n_s, D_MODEL)
    pos_s = PAST_LEN + jnp.arange(ts, dtype=jnp.int32)
    tabs_s = tuple(jnp.tile(t, (bs, 1)) for t in _rope_tables(pos_s))
    q, k, v, gu, gv = _in_proj(xs2, w_in_b, tabs_s, gm_g, gm_b, tm=n_s, gv_dtype=F32, name="in_proj_sample")
    tq = SLABS
    pad_rows = lambda t: jnp.pad(t.reshape(bs, ts, -1), ((0, 0), (0, tq - ts), (0, 0))).reshape(bs * tq, -1)
    ck2 = cache_k[l].reshape(bs * r, KV_WIDTH)
    cv2 = cache_v[l].reshape(bs * r, KV_WIDTH)
    att = _attention(sink_l, pad_rows(q), pad_rows(k), pad_rows(v), ck2, cv2, batch=bs, nb=1, tq=tq,
                     seqs=SAMPLE_SEQS_PER_STEP, chain=1, prev_blocks=1, first_block_has_no_prev=False, name="attn_sample")
    att = att.reshape(bs, tq, ATT_WIDTH)[:, :ts].reshape(n_s, ATT_WIDTH)
    gm = _gate_short(gu, gv, gm_w_s[l], gm_b_s[l], length=ts, name="gate_sample")
    new_ks = jnp.concatenate([cache_k[l], k.reshape(bs, ts, KV_HEADS, HEAD_DIM)], axis=1)[:, ts:][None]
    new_vs = jnp.concatenate([cache_v[l], v.reshape(bs, ts, KV_HEADS, HEAD_DIM)], axis=1)[:, ts:][None]
    new_gs = gv.reshape(bs, ts, GM_WIDTH)[None]
    h_s, pk_s, lt_s = _out_proj(att, gm, xs2, w_out_b, row_vec(ln1_g[l]), row_vec(ln1_b[l]), r_hi, r_lo,
                          name="out_proj_sample")

    eidx_t, w_t, rank_t, counts = _route(lt_p, lt_s, router_bias[l].astype(F32).reshape(N_EXPERTS, 1))
    a = n_total * TOP_K
    n_blocks = -(-(a + N_EXPERTS * (EXPERT_ROWS - 1)) // EXPERT_ROWS)
    pad_start, fill_start, fill_len, first_block, n_expert_blocks, n_used = _block_plan(counts.reshape(N_EXPERTS))
    dest_t = _dest(eidx_t, rank_t, pad_start.astype(F32).reshape(N_EXPERTS, 1))
    tiles = lambda a: a.reshape(-1, WORD_SLABS, LANES)
    shared = (w_gate_s[l].astype(BF16), w_up_s[l].astype(BF16), w_down_s[l].astype(BF16))
    x_sorted, y_part = _dispatch(fill_start, fill_len, n_used, dest_t, tiles(pk_p), tiles(pk_s), h_p, h_s, *shared,
                                 n_blocks=n_blocks)
    out_sorted = _experts(first_block, n_expert_blocks, n_used, x_sorted, w_gate_e[l], w_up_e[l], w_down_e[l])
    ln2 = (row_vec(ln2_g[l]), row_vec(ln2_b[l]))
    dest2, w2 = dest_t.T, w_t.T
    y_p = _combine(dest2, w2, y_part, *ln2, out_sorted, n=n_p, row_offset=0, name="combine_prompt")
    y_s = _combine(dest2, w2, y_part, *ln2, out_sorted, n=n_s, row_offset=n_p, name="combine_sample")
    return (y_p.reshape(bp, sp, D_MODEL), y_s.reshape(bs, ts, D_MODEL), new_kp, new_vp, new_ks, new_vs, new_gs)
```

```python
import functools

import jax
import jax.numpy as jnp
import numpy as np
from jax import lax
from jax.experimental import pallas as pl
from jax.experimental.pallas import tpu as pltpu

D_MODEL = 1024
HEAD_DIM = 64
ATT_HEADS = 8
KV_HEADS = 2
Q_PER_KV = ATT_HEADS // KV_HEADS
GM_HEADS = 8
ATT_WIDTH = ATT_HEADS * HEAD_DIM
KV_WIDTH = KV_HEADS * HEAD_DIM
GM_WIDTH = GM_HEADS * HEAD_DIM
ROPE_WIDTH = ATT_WIDTH + KV_WIDTH
IN_WIDTH = ATT_WIDTH + 2 * KV_WIDTH + 2 * GM_WIDTH
WINDOW = 128
CHUNK = 128
PAST_LEN = 16384
ROPE_THETA = 10000.0
ATT_SCALE = HEAD_DIM ** -0.5
N_EXPERTS = 256
TOP_K = 8
N_GROUPS = 8
TOPK_GROUPS = 4
F_EXPERT = 256
ROUTED_SCALE = 2.5
LN_EPS = 1e-5
DEPTH = 1
ALPHA = (2.0 * DEPTH) ** 0.25

LANES = 128
SLABS = 8
PACKED = D_MODEL // 2
WORD_SLABS = PACKED // LANES
ROW_TILE = 512
EXPERT_ROWS = 256
EXPERT_RING = 8
COMBINE_ROWS = 256
COMBINE_GROUP = 4
DISPATCH_ROWS = 256
ROUTE_TILE = 512
SAMPLE_SEQS_PER_STEP = 16
PROMPT_BLOCKS_PER_STEP = 2
DMA_UNROLL = 4
DMA_PRIORITIES = 2
VMEM_LIMIT = 56 * 1024 * 1024

F32 = jnp.float32
BF16 = jnp.bfloat16


def _params(n_axes):
    return pltpu.CompilerParams(dimension_semantics=("arbitrary",) * n_axes, vmem_limit_bytes=VMEM_LIMIT)


def _layer_norm(x, g, b):
    mu = jnp.mean(x, axis=-1, keepdims=True)
    xc = x - mu
    var = jnp.mean(xc * xc, axis=-1, keepdims=True)
    return xc * lax.rsqrt(var + LN_EPS) * g + b


def _pack_halves(x):
    xb = x.astype(BF16)
    lo = lax.bitcast_convert_type(xb[:, :PACKED].astype(F32), jnp.uint32)
    hi = lax.bitcast_convert_type(xb[:, PACKED:].astype(F32), jnp.uint32)
    return (lo >> 16) | hi


def _unpack_halves(w):
    lo = lax.bitcast_convert_type(w << 16, F32)
    hi = lax.bitcast_convert_type(w & jnp.uint32(0xFFFF0000), F32)
    return lo, hi


def _store_row_tiles(ref2d, x):
    m, slabs = x.shape[0], x.shape[1] // LANES
    for s in range(slabs):
        ref2d[pl.ds(s, m, stride=slabs), :] = x[:, s * LANES:(s + 1) * LANES]


def _load_row_tiles(ref2d, m, slabs):
    return jnp.concatenate([ref2d[pl.ds(s, m, stride=slabs), :] for s in range(slabs)], axis=1)


def _gelu(x):
    return 0.5 * x * (1.0 + lax.erf(x * np.float32(np.sqrt(0.5))))


def _in_proj_kernel(x_ref, w_ref, cos_ref, sa_ref, sb_ref, g_ref, b_ref, q_ref, k_ref, v_ref, gu_ref, gv_ref):
    x = x_ref[...].astype(BF16)
    zr = jnp.dot(x, w_ref[:, :ROPE_WIDTH], preferred_element_type=F32)
    pieces = []
    for c in range(ROPE_WIDTH // LANES):
        zc = zr[:, c * LANES:(c + 1) * LANES]
        tl = slice(0, LANES) if c < ATT_WIDTH // LANES else slice(LANES, 2 * LANES)
        pieces.append(zc * cos_ref[:, tl]
                      + pltpu.roll(zc, LANES - HEAD_DIM // 2, 1) * sa_ref[:, tl]
                      + pltpu.roll(zc, HEAD_DIM // 2, 1) * sb_ref[:, tl])
    for c in range(ATT_WIDTH // LANES):
        q_ref[:, c * LANES:(c + 1) * LANES] = pieces[c].astype(q_ref.dtype)
    k_ref[...] = pieces[ATT_WIDTH // LANES]
    v_ref[...] = jnp.dot(x, w_ref[:, ROPE_WIDTH:ROPE_WIDTH + KV_WIDTH], preferred_element_type=F32)
    g0 = ROPE_WIDTH + KV_WIDTH
    zu = jnp.dot(x, w_ref[:, g0:g0 + GM_WIDTH], preferred_element_type=F32)
    gu_ref[...] = _gelu(zu).astype(gu_ref.dtype)
    zv = jnp.dot(x, w_ref[:, g0 + GM_WIDTH:g0 + 2 * GM_WIDTH], preferred_element_type=F32)
    gv = _layer_norm(_gelu(zv), g_ref[...], b_ref[...])
    gv_ref[...] = gv.astype(gv_ref.dtype)


def _in_proj(x2, w_in_b, tabs, ln_g, ln_b, *, tm, gv_dtype, name):
    n = x2.shape[0]
    cos_t, sa_t, sb_t = tabs
    period = cos_t.shape[0] // tm
    row = lambda i: (i, 0)
    tab = lambda i: (i % period, 0)
    fixed = lambda i: (0, 0)
    return pl.pallas_call(
        _in_proj_kernel,
        grid=(n // tm,),
        in_specs=[
            pl.BlockSpec((tm, D_MODEL), row),
            pl.BlockSpec((D_MODEL, IN_WIDTH), fixed),
            pl.BlockSpec((tm, 2 * LANES), tab),
            pl.BlockSpec((tm, 2 * LANES), tab),
            pl.BlockSpec((tm, 2 * LANES), tab),
            pl.BlockSpec((1, GM_WIDTH), fixed),
            pl.BlockSpec((1, GM_WIDTH), fixed),
        ],
        out_specs=[
            pl.BlockSpec((tm, ATT_WIDTH), row),
            pl.BlockSpec((tm, KV_WIDTH), row),
            pl.BlockSpec((tm, KV_WIDTH), row),
            pl.BlockSpec((tm, GM_WIDTH), row),
            pl.BlockSpec((tm, GM_WIDTH), row),
        ],
        out_shape=[
            jax.ShapeDtypeStruct((n, ATT_WIDTH), BF16),
            jax.ShapeDtypeStruct((n, KV_WIDTH), F32),
            jax.ShapeDtypeStruct((n, KV_WIDTH), F32),
            jax.ShapeDtypeStruct((n, GM_WIDTH), BF16),
            jax.ShapeDtypeStruct((n, GM_WIDTH), gv_dtype),
        ],
        compiler_params=_params(1),
        name=name,
    )(x2, w_in_b, cos_t, sa_t, sb_t, ln_g, ln_b)


def _rope_tables(pos):
    half = HEAD_DIM // 2
    lane = jnp.arange(2 * LANES, dtype=jnp.int32)
    inv = ROPE_THETA ** (-(lane % half).astype(F32) * 2.0 / HEAD_DIM)
    ang = pos.astype(F32)[:, None] * inv[None, :]
    scale = jnp.where(lane < LANES, ATT_SCALE, 1.0).astype(F32)[None, :]
    first_half = ((lane % HEAD_DIM) < half)[None, :]
    cos, sin = jnp.cos(ang) * scale, jnp.sin(ang) * scale
    return cos, jnp.where(first_half, -sin, 0.0), jnp.where(first_half, 0.0, sin)


def _attn_kernel(sink_ref, q_ref, kc_ref, vc_ref, kp_ref, vp_ref, o_ref, *, tq, seqs, chain, stack,
                 first_block_has_no_prev):
    nk = WINDOW + tq
    rows = stack * tq
    qi = lax.broadcasted_iota(jnp.int32, (rows, nk), 0) & (tq - 1)
    ks = lax.broadcasted_iota(jnp.int32, (rows, nk), 1)
    band = (ks >= qi) & (ks <= qi + WINDOW)
    first_mask = band & ((pl.program_id(1) > 0) | (ks >= WINDOW)) if first_block_has_no_prev else band
    sinks = [jnp.concatenate([jnp.full((tq, 1), sink_ref[h0 + j], F32) for j in range(stack)], axis=0)
             for h0 in range(0, ATT_HEADS, stack)]
    for b in range(seqs * chain):
        qrows = slice(b * tq, (b + 1) * tq)
        if chain > 1 and b > 0:
            kprev, vprev = kc_ref[(b - 1) * tq:b * tq, :], vc_ref[(b - 1) * tq:b * tq, :]
        else:
            kprev, vprev = kp_ref[b * WINDOW:(b + 1) * WINDOW, :], vp_ref[b * WINDOW:(b + 1) * WINDOW, :]
        mask = band if (chain > 1 and b > 0) else first_mask
        q = q_ref[qrows, :]
        kk = jnp.concatenate([kprev, kc_ref[qrows, :]], axis=0).astype(BF16)
        vv = jnp.concatenate([vprev, vc_ref[qrows, :]], axis=0).astype(BF16)
        outs = []
        for i, h0 in enumerate(range(0, ATT_HEADS, stack)):
            g = h0 // Q_PER_KV
            kg = kk[:, g * HEAD_DIM:(g + 1) * HEAD_DIM]
            vg = vv[:, g * HEAD_DIM:(g + 1) * HEAD_DIM]
            qg = jnp.concatenate([q[:, (h0 + j) * HEAD_DIM:(h0 + j + 1) * HEAD_DIM] for j in range(stack)], axis=0)
            s = lax.dot_general(qg, kg, (((1,), (1,)), ((), ())), preferred_element_type=F32)
            s = jnp.where(mask, s, -jnp.inf)
            m = jnp.maximum(jnp.max(s, axis=-1, keepdims=True), sinks[i])
            p = jnp.exp(s - m)
            denom = jnp.sum(p, axis=-1, keepdims=True) + jnp.exp(sinks[i] - m)
            o = jnp.dot((p / denom).astype(BF16), vg, preferred_element_type=F32)
            outs.extend(o[j * tq:(j + 1) * tq, :] for j in range(stack))
        o_ref[qrows, :] = jnp.concatenate(outs, axis=1).astype(o_ref.dtype)


def _attention(sink, q2, k2, v2, kprev2, vprev2, *, batch, nb, tq, seqs, chain, prev_blocks, first_block_has_no_prev,
               name):
    assert tq & (tq - 1) == 0 and batch % seqs == 0 and (seqs == 1 or nb == prev_blocks == 1)
    assert (seqs == 1 or chain == 1) and nb % chain == 0 and (chain == 1 or tq == WINDOW)
    steps = nb // chain
    cur = lambda b, n, s: (b * steps + n, 0)
    prev = lambda b, n, s: (b * prev_blocks + jnp.maximum(n * chain - 1, 0), 0)
    stack = Q_PER_KV if Q_PER_KV * tq <= WINDOW else 1
    kern = functools.partial(_attn_kernel, tq=tq, seqs=seqs, chain=chain, stack=stack,
                             first_block_has_no_prev=first_block_has_no_prev)
    return pl.pallas_call(
        kern,
        grid_spec=pltpu.PrefetchScalarGridSpec(
            num_scalar_prefetch=1,
            grid=(batch // seqs, steps),
            in_specs=[
                pl.BlockSpec((seqs * chain * tq, ATT_WIDTH), cur),
                pl.BlockSpec((seqs * chain * tq, KV_WIDTH), cur),
                pl.BlockSpec((seqs * chain * tq, KV_WIDTH), cur),
                pl.BlockSpec((seqs * WINDOW, KV_WIDTH), prev),
                pl.BlockSpec((seqs * WINDOW, KV_WIDTH), prev),
            ],
            out_specs=pl.BlockSpec((seqs * chain * tq, ATT_WIDTH), cur),
        ),
        out_shape=jax.ShapeDtypeStruct(q2.shape, BF16),
        compiler_params=_params(2),
        name=name,
    )(sink, q2, k2, v2, kprev2, vprev2)


def _gate_kernel(gu_ref, gv_ref, w_ref, b_ref, o_ref, *, chunk, n_chunks):
    ri = lax.broadcasted_iota(jnp.int32, (chunk, chunk), 0)
    ci = lax.broadcasted_iota(jnp.int32, (chunk, chunk), 1)
    ws = [jnp.where(ci <= ri, w_ref[h], 0.0).astype(BF16) for h in range(GM_HEADS)]
    for c in range(n_chunks):
        rows = slice(c * chunk, (c + 1) * chunk)
        gv = gv_ref[rows, :].astype(BF16)
        sv = jnp.concatenate(
            [jnp.dot(ws[h], gv[:, h * HEAD_DIM:(h + 1) * HEAD_DIM], preferred_element_type=F32)
             for h in range(GM_HEADS)], axis=1)
        o_ref[rows, :] = (gu_ref[rows, :].astype(F32) * (sv + b_ref[...])).astype(o_ref.dtype)


def _gate(gu2, gv2, w_s, b_tab, *, chunk, n_chunks, name):
    n = gu2.shape[0]
    tm = chunk * n_chunks
    row = lambda i: (i, 0)
    kern = functools.partial(_gate_kernel, chunk=chunk, n_chunks=n_chunks)
    return pl.pallas_call(
        kern,
        grid=(n // tm,),
        in_specs=[
            pl.BlockSpec((tm, GM_WIDTH), row),
            pl.BlockSpec((tm, GM_WIDTH), row),
            pl.BlockSpec((GM_HEADS, chunk, chunk), lambda i: (0, 0, 0)),
            pl.BlockSpec((chunk, GM_WIDTH), lambda i: (0, 0)),
        ],
        out_specs=pl.BlockSpec((tm, GM_WIDTH), row),
        out_shape=jax.ShapeDtypeStruct((n, GM_WIDTH), BF16),
        compiler_params=_params(1),
        name=name,
    )(gu2, gv2, w_s, b_tab)


def _gate_short_kernel(gu_ref, gv_ref, coef_ref, b_ref, o_ref, *, length):
    n = gv_ref.shape[0]
    gv = gv_ref[...].astype(F32)
    tiles = lambda x: x.reshape(n // SLABS, SLABS, GM_WIDTH)
    sv = tiles(gv) * coef_ref[0][None] + b_ref[...][None]
    for d in range(1, length):
        sv = sv + tiles(pltpu.roll(gv, d, 0)) * coef_ref[d][None]
    o_ref[...] = (gu_ref[...].astype(F32) * sv.reshape(n, GM_WIDTH)).astype(o_ref.dtype)


def _gate_short(gu2, gv2, w_s, b_s, *, length, name):
    n = gu2.shape[0]
    assert SLABS % length == 0 and n % SLABS == 0
    pos = jnp.arange(length)
    lag = jnp.arange(length)
    src = pos[None, :] - lag[:, None]
    coef = jnp.where(src >= 0, w_s[:, pos[None, :], jnp.maximum(src, 0)], 0.0)
    coef = jnp.repeat(coef.transpose(1, 2, 0), HEAD_DIM, axis=2)
    coef = jnp.tile(coef, (1, SLABS // length, 1))
    bias = jnp.tile(jnp.repeat(b_s[:, :length].T, HEAD_DIM, axis=1), (SLABS // length, 1))
    whole = lambda shape: pl.BlockSpec(shape, lambda i: (0,) * len(shape))
    return pl.pallas_call(
        functools.partial(_gate_short_kernel, length=length),
        grid=(1,),
        in_specs=[whole((n, GM_WIDTH)), whole((n, GM_WIDTH)), whole((length, SLABS, GM_WIDTH)),
                  whole((SLABS, GM_WIDTH))],
        out_specs=whole((n, GM_WIDTH)),
        out_shape=jax.ShapeDtypeStruct((n, GM_WIDTH), BF16),
        compiler_params=_params(1),
        name=name,
    )(gu2, gv2, coef, bias)


def _out_proj_kernel(att_ref, gm_ref, x_ref, wo_ref, g_ref, b_ref, rh_ref, rl_ref, h_ref, hp_ref, lg_ref):
    mix = jnp.dot(att_ref[...], wo_ref[:ATT_WIDTH, :], preferred_element_type=F32)
    mix = mix + jnp.dot(gm_ref[...], wo_ref[ATT_WIDTH:, :], preferred_element_type=F32)
    h = _layer_norm(ALPHA * x_ref[...] + mix, g_ref[...], b_ref[...])
    h_ref[...] = h
    _store_row_tiles(hp_ref, _pack_halves(h))
    h_hi = h.astype(BF16)
    h_lo = (h - h_hi.astype(F32)).astype(BF16)
    nt = (((1,), (1,)), ((), ()))
    lg = lax.dot_general(rh_ref[...], h_hi, nt, preferred_element_type=F32)
    lg = lg + lax.dot_general(rh_ref[...], h_lo, nt, preferred_element_type=F32)
    lg = lg + lax.dot_general(rl_ref[...], h_hi, nt, preferred_element_type=F32)
    lg_ref[...] = lg


def _out_proj(att2, gm2, x2, w_out_b, ln_g, ln_b, r_hi, r_lo, *, name):
    n = x2.shape[0]
    tm = ROW_TILE
    row = lambda i: (i, 0)
    fixed = lambda i: (0, 0)
    return pl.pallas_call(
        _out_proj_kernel,
        grid=(n // tm,),
        in_specs=[
            pl.BlockSpec((tm, ATT_WIDTH), row),
            pl.BlockSpec((tm, GM_WIDTH), row),
            pl.BlockSpec((tm, D_MODEL), row),
            pl.BlockSpec((D_MODEL, D_MODEL), fixed),
            pl.BlockSpec((1, D_MODEL), fixed),
            pl.BlockSpec((1, D_MODEL), fixed),
            pl.BlockSpec((N_EXPERTS, D_MODEL), fixed),
            pl.BlockSpec((N_EXPERTS, D_MODEL), fixed),
        ],
        out_specs=[
            pl.BlockSpec((tm, D_MODEL), row),
            pl.BlockSpec((tm * WORD_SLABS, LANES), row),
            pl.BlockSpec((N_EXPERTS, tm), lambda i: (0, i)),
        ],
        out_shape=[
            jax.ShapeDtypeStruct((n, D_MODEL), F32),
            jax.ShapeDtypeStruct((n * WORD_SLABS, LANES), jnp.uint32),
            jax.ShapeDtypeStruct((N_EXPERTS, n), F32),
        ],
        compiler_params=_params(1),
        name=name,
    )(att2, gm2, x2, w_out_b, ln_g, ln_b, r_hi, r_lo)


def _experts_kernel(first_ref, nblk_ref, nu_ref, x_hbm, wg_ref, wu_ref, wd_ref, o_hbm, xbuf, obuf, wg_s, wu_s, wd_s,
                    in_sem, out_sem, fill_sem, *, n_blocks):
    e = pl.program_id(0)
    nb = nblk_ref[e]
    b0 = first_ref[e]
    n_used = nu_ref[0]
    ahead = EXPERT_RING // 2

    def rows_of(block):
        return pl.ds(pl.multiple_of(block * EXPERT_ROWS, EXPERT_ROWS), EXPERT_ROWS)

    def slot_of(block):
        return block & (EXPERT_RING - 1)

    def slab(s):
        return pl.ds(s * LANES, LANES)

    def in_copies(block):
        slot = slot_of(block)
        return [pltpu.make_async_copy(x_hbm.at[rows_of(block), s, :], xbuf.at[slot, :, slab(s)], in_sem.at[slot])
                for s in range(WORD_SLABS)]

    def out_copies(block, slot=None, sem=None):
        slot = slot_of(block) if slot is None else slot
        sem = out_sem.at[slot] if sem is None else sem
        return [pltpu.make_async_copy(obuf.at[slot, :, slab(s)], o_hbm.at[rows_of(block), s, :], sem)
                for s in range(WORD_SLABS)]

    def start(copies):
        for cp in copies:
            cp.start()

    def wait(copies):
        for cp in copies:
            cp.wait()

    @pl.when(e == 0)
    def _():
        for j in range(ahead):
            pl.when(j < n_used)(lambda j=j: start(in_copies(j)))

    def process(block, count):
        blocks = [block + j for j in range(count)]
        for blk in blocks:
            wait(in_copies(blk))
        x = [xbuf[slot_of(blk)] for blk in blocks]
        xl, xh = _unpack_halves(x[0] if count == 1 else jnp.concatenate(x, axis=0))
        xl, xh = xl.astype(BF16), xh.astype(BF16)
        for blk in blocks:
            pl.when(blk + ahead < n_used)(lambda blk=blk: start(in_copies(blk + ahead)))
            pl.when(blk >= ahead)(lambda blk=blk: wait(out_copies(blk - ahead)))
        a = (jnp.dot(xl, wg_s[:PACKED, :], preferred_element_type=F32)
             + jnp.dot(xh, wg_s[PACKED:, :], preferred_element_type=F32))
        u = (jnp.dot(xl, wu_s[:PACKED, :], preferred_element_type=F32)
             + jnp.dot(xh, wu_s[PACKED:, :], preferred_element_type=F32))
        hb = (a * jax.nn.sigmoid(a) * u).astype(BF16)
        o = _pack_halves(jnp.dot(hb, wd_s[...], preferred_element_type=F32))
        for j, blk in enumerate(blocks):
            obuf[slot_of(blk)] = o[j * EXPERT_ROWS:(j + 1) * EXPERT_ROWS, :]
            start(out_copies(blk))

    @pl.when(nb > 0)
    def _():
        wg_s[...] = wg_ref[0].astype(BF16)
        wu_s[...] = wu_ref[0].astype(BF16)
        wd_s[...] = wd_ref[0].astype(BF16)

        def pair(i, carry):
            process(b0 + 2 * i, 2)
            return carry

        lax.fori_loop(0, nb // 2, pair, 0)
        pl.when(nb % 2 == 1)(lambda: process(b0 + nb - 1, 1))

    @pl.when(e == N_EXPERTS - 1)
    def _():
        for j in range(ahead):
            pl.when(n_used - 1 - j >= 0)(lambda j=j: wait(out_copies(n_used - 1 - j)))
        obuf[0] = jnp.zeros((EXPERT_ROWS, PACKED), jnp.uint32)

        def on_unused_blocks(fn):
            def body(b, c):
                fn(out_copies(b, slot=0, sem=fill_sem))
                return c
            lax.fori_loop(n_used, n_blocks, body, 0)

        on_unused_blocks(start)
        on_unused_blocks(wait)


def _experts(first_block, n_expert_blocks, n_used, x_sorted, w_gate_e, w_up_e, w_down_e):
    rows = x_sorted.shape[0]
    wmap = lambda e, *_: (e, 0, 0)
    kern = functools.partial(_experts_kernel, n_blocks=rows // EXPERT_ROWS)
    return pl.pallas_call(
        kern,
        grid_spec=pltpu.PrefetchScalarGridSpec(
            num_scalar_prefetch=3,
            grid=(N_EXPERTS,),
            in_specs=[
                pl.BlockSpec(memory_space=pl.ANY),
                pl.BlockSpec((1, D_MODEL, F_EXPERT), wmap),
                pl.BlockSpec((1, D_MODEL, F_EXPERT), wmap),
                pl.BlockSpec((1, F_EXPERT, D_MODEL), wmap),
            ],
            out_specs=pl.BlockSpec(memory_space=pl.ANY),
            scratch_shapes=[
                pltpu.VMEM((EXPERT_RING, EXPERT_ROWS, PACKED), jnp.uint32),
                pltpu.VMEM((EXPERT_RING, EXPERT_ROWS, PACKED), jnp.uint32),
                pltpu.VMEM((D_MODEL, F_EXPERT), BF16),
                pltpu.VMEM((D_MODEL, F_EXPERT), BF16),
                pltpu.VMEM((F_EXPERT, D_MODEL), BF16),
                pltpu.SemaphoreType.DMA((EXPERT_RING,)),
                pltpu.SemaphoreType.DMA((EXPERT_RING,)),
                pltpu.SemaphoreType.DMA,
            ],
        ),
        out_shape=jax.ShapeDtypeStruct(x_sorted.shape, x_sorted.dtype),
        compiler_params=_params(1),
        name="experts",
    )(first_block, n_expert_blocks, n_used, x_sorted, w_gate_e, w_up_e, w_down_e)


def _combine_kernel(d_ref, w_ref, yp_ref, ln_g_ref, ln_b_ref, os_ref, y_ref, gbuf0, gbuf1, rlo, rhi, sem):
    s = pl.program_id(0)
    n_tiles = pl.num_programs(0) - 1
    t = y_ref.shape[0]
    gtiles = [g.reshape(t * TOP_K, WORD_SLABS, LANES) for g in (gbuf0, gbuf1)]
    lo_tiles = rlo.reshape(t, WORD_SLABS, LANES)
    hi_tiles = rhi.reshape(t, WORD_SLABS, LANES)

    def row_copy(row, buf, j, k):
        return pltpu.make_async_copy(os_ref.at[pl.ds(row, 1)], gtiles[buf].at[pl.ds(j * TOP_K + k, 1)], sem.at[buf])

    def drain(buf):
        def body(j, c):
            for k in range(TOP_K):
                row_copy(0, buf, j, k).wait()
            return c
        lax.fori_loop(0, t, body, 0, unroll=DMA_UNROLL)

    def weighted_sum(rows, j):
        lo, hi = _unpack_halves(rows)
        acc_lo = w_ref[j, 0] * lo[0:1]
        acc_hi = w_ref[j, 0] * hi[0:1]
        for k in range(1, TOP_K):
            acc_lo = acc_lo + w_ref[j, k] * lo[k:k + 1]
            acc_hi = acc_hi + w_ref[j, k] * hi[k:k + 1]
        lo_tiles[pl.ds(j, 1)] = acc_lo
        hi_tiles[pl.ds(j, 1)] = acc_hi

    def token_loop(new, old, start_new, sum_old):
        def body(g, c):
            tokens = [g * COMBINE_GROUP + u for u in range(COMBINE_GROUP)]
            rows = [gtiles[old][pl.ds(j * TOP_K, TOP_K)] for j in tokens] if sum_old else []
            if start_new:
                for j in tokens:
                    for k in range(TOP_K):
                        row_copy(d_ref[j, k], new, j, k).start(priority=k % DMA_PRIORITIES)
            for r, j in zip(rows, tokens):
                weighted_sum(r, j)
            return c
        lax.fori_loop(0, t // COMBINE_GROUP, body, 0)

    def step(new):
        old = 1 - new
        pl.when(s == 0)(lambda: token_loop(new, old, True, False))

        @pl.when(s >= 1)
        def _():
            drain(old)
            pl.when(s < n_tiles)(lambda: token_loop(new, old, True, True))
            pl.when(s == n_tiles)(lambda: token_loop(new, old, False, True))
            routed = jnp.concatenate([_load_row_tiles(rlo, t, WORD_SLABS), _load_row_tiles(rhi, t, WORD_SLABS)],
                                     axis=1)
            y_ref[...] = _layer_norm(yp_ref[...] + routed, ln_g_ref[...], ln_b_ref[...])

    pl.when(s % 2 == 0)(lambda: step(0))
    pl.when(s % 2 == 1)(lambda: step(1))


def _combine(dest2, w2, y_part, ln_g, ln_b, out_sorted, *, n, row_offset, name):
    tm = COMBINE_ROWS
    off = row_offset // tm
    n_tiles = n // tm
    fixed = lambda s: (0, 0)
    new_tile = lambda s: (jnp.minimum(s, n_tiles - 1) + off, 0)
    old_tile = lambda s: (jnp.maximum(s - 1, 0) + off, 0)
    return pl.pallas_call(
        _combine_kernel,
        grid=(n_tiles + 1,),
        in_specs=[
            pl.BlockSpec((tm, TOP_K), new_tile, memory_space=pltpu.SMEM),
            pl.BlockSpec((tm, TOP_K), old_tile, memory_space=pltpu.SMEM),
            pl.BlockSpec((tm, D_MODEL), old_tile),
            pl.BlockSpec((1, D_MODEL), fixed),
            pl.BlockSpec((1, D_MODEL), fixed),
            pl.BlockSpec(memory_space=pl.ANY),
        ],
        out_specs=pl.BlockSpec((tm, D_MODEL), lambda s: (jnp.maximum(s - 1, 0), 0)),
        out_shape=jax.ShapeDtypeStruct((n, D_MODEL), F32),
        scratch_shapes=[pltpu.VMEM((TOP_K * tm * WORD_SLABS, LANES), jnp.uint32),
                        pltpu.VMEM((TOP_K * tm * WORD_SLABS, LANES), jnp.uint32),
                        pltpu.VMEM((tm * WORD_SLABS, LANES), F32), pltpu.VMEM((tm * WORD_SLABS, LANES), F32),
                        pltpu.SemaphoreType.DMA((2,))],
        compiler_params=_params(1),
        name=name,
    )(dest2, w2, y_part, ln_g, ln_b, out_sorted)


def _route_kernel(lgp_ref, lgs_ref, bias_ref, eidx_ref, w_ref, rank_ref, cnt_ref, carry_ref, *, prompt_tiles):
    @pl.when(pl.program_id(0) == 0)
    def _():
        carry_ref[...] = jnp.zeros_like(carry_ref)

    t = lgp_ref.shape[1]
    gsz = N_EXPERTS // N_GROUPS
    neg = -jnp.inf
    s = jax.nn.sigmoid(jnp.where(pl.program_id(0) < prompt_tiles, lgp_ref[...], lgs_ref[...]))
    biased = s + bias_ref[...]
    io_g = lax.broadcasted_iota(jnp.int32, (gsz, t), 0)
    grp_rows = []
    for g in range(N_GROUPS):
        blk = biased[g * gsz:(g + 1) * gsz, :]
        m1 = jnp.max(blk, axis=0, keepdims=True)
        i1 = jnp.min(jnp.where(blk == m1, io_g, gsz), axis=0, keepdims=True)
        m2 = jnp.max(jnp.where(io_g == i1, neg, blk), axis=0, keepdims=True)
        grp_rows.append(m1 + m2)
    gs = jnp.concatenate(grp_rows, axis=0)
    io8 = lax.broadcasted_iota(jnp.int32, (N_GROUPS, t), 0)
    gsel = jnp.zeros((N_GROUPS, t), jnp.int32)
    for _ in range(TOPK_GROUPS):
        m = jnp.max(gs, axis=0, keepdims=True)
        gi = jnp.min(jnp.where(gs == m, io8, N_GROUPS), axis=0, keepdims=True)
        hit = io8 == gi
        gsel = jnp.where(hit, 1, gsel)
        gs = jnp.where(hit, neg, gs)
    masked = jnp.concatenate(
        [jnp.where(gsel[g:g + 1, :] > 0, biased[g * gsz:(g + 1) * gsz, :], neg) for g in range(N_GROUPS)], axis=0)

    eio = lax.broadcasted_iota(jnp.int32, (N_EXPERTS, t), 0)
    cur = masked
    idx_rows, w_rows = [], []
    for _ in range(TOP_K):
        m = jnp.max(cur, axis=0, keepdims=True)
        idx = jnp.min(jnp.where(cur == m, eio, N_EXPERTS), axis=0, keepdims=True)
        hit = eio == idx
        w_rows.append(jnp.sum(jnp.where(hit, s, 0.0), axis=0, keepdims=True))
        cur = jnp.where(hit, neg, cur)
        idx_rows.append(idx)
    sel = jnp.where(cur != masked, 1.0, 0.0)

    tri = jnp.where(lax.broadcasted_iota(jnp.int32, (t, t), 0) < lax.broadcasted_iota(jnp.int32, (t, t), 1), 1.0, 0.0)
    pref = jnp.dot(sel.astype(BF16), tri.astype(BF16), preferred_element_type=F32) + carry_ref[...]
    rank_rows = [jnp.sum(jnp.where(eio == idx_rows[k], pref, 0.0), axis=0, keepdims=True) for k in range(TOP_K)]
    carry_ref[...] += jnp.sum(sel, axis=1, keepdims=True)

    wk = jnp.concatenate(w_rows, axis=0)
    eidx_ref[...] = jnp.concatenate(idx_rows, axis=0)
    w_ref[...] = wk / jnp.sum(wk, axis=0, keepdims=True) * ROUTED_SCALE
    rank_ref[...] = jnp.concatenate(rank_rows, axis=0).astype(jnp.int32)
    cnt_ref[...] = carry_ref[...].astype(jnp.int32)


def _route(logits_p, logits_s, bias_col):
    t = ROUTE_TILE
    prompt_tiles = logits_p.shape[1] // t
    n = logits_p.shape[1] + logits_s.shape[1]
    col = lambda i: (0, i)
    fixed = lambda i: (0, 0)
    kern = functools.partial(_route_kernel, prompt_tiles=prompt_tiles)
    return pl.pallas_call(
        kern,
        grid=(n // t,),
        in_specs=[pl.BlockSpec((N_EXPERTS, t), lambda i: (0, jnp.minimum(i, prompt_tiles - 1))),
                  pl.BlockSpec((N_EXPERTS, t), lambda i: (0, jnp.maximum(i - prompt_tiles, 0))),
                  pl.BlockSpec((N_EXPERTS, 1), fixed)],
        out_specs=[
            pl.BlockSpec((TOP_K, t), col),
            pl.BlockSpec((TOP_K, t), col),
            pl.BlockSpec((TOP_K, t), col),
            pl.BlockSpec((N_EXPERTS, 1), fixed),
        ],
        out_shape=[
            jax.ShapeDtypeStruct((TOP_K, n), jnp.int32),
            jax.ShapeDtypeStruct((TOP_K, n), F32),
            jax.ShapeDtypeStruct((TOP_K, n), jnp.int32),
            jax.ShapeDtypeStruct((N_EXPERTS, 1), jnp.int32),
        ],
        scratch_shapes=[pltpu.VMEM((N_EXPERTS, 1), F32)],
        compiler_params=_params(1),
        name="route",
    )(logits_p, logits_s, bias_col)


def _dest_kernel(eidx_ref, rank_ref, start_ref, dest_ref):
    t = eidx_ref.shape[1]
    eio = lax.broadcasted_iota(jnp.int32, (N_EXPERTS, t), 0)
    start = start_ref[...]
    rows = [jnp.sum(jnp.where(eio == eidx_ref[k:k + 1, :], start, 0.0), axis=0, keepdims=True) for k in range(TOP_K)]
    dest_ref[...] = jnp.concatenate(rows, axis=0).astype(jnp.int32) + rank_ref[...]


def _dest(eidx_t, rank_t, pad_start_col):
    n = eidx_t.shape[1]
    t = ROW_TILE
    col = lambda i: (0, i)
    return pl.pallas_call(
        _dest_kernel,
        grid=(n // t,),
        in_specs=[pl.BlockSpec((TOP_K, t), col), pl.BlockSpec((TOP_K, t), col),
                  pl.BlockSpec((N_EXPERTS, 1), lambda i: (0, 0))],
        out_specs=pl.BlockSpec((TOP_K, t), col),
        out_shape=jax.ShapeDtypeStruct((TOP_K, n), jnp.int32),
        compiler_params=_params(1),
        name="dest",
    )(eidx_t, rank_t, pad_start_col)


def _dispatch_kernel(fill_ref, len_ref, nu_ref, dest_ref, pp_ref, ps_ref, hp_ref, hs_ref, wg_ref, wu_ref, wd_ref,
                     pp_hbm, ps_hbm, xs_ref, yp_ref, zbuf, sem, fill_sem, *, prompt_tiles, n_blocks):
    i = pl.program_id(0)

    @pl.when(i == 0)
    def _():
        zbuf[...] = jnp.zeros_like(zbuf)

        def fill_copy(row0, size):
            return pltpu.make_async_copy(zbuf.at[pl.ds(0, size)], xs_ref.at[pl.ds(row0, size)], fill_sem)

        def on_padding(fn):
            def body(e, c):
                base, length = fill_ref[e], len_ref[e]
                size = EXPERT_ROWS // 2
                while size >= 1:
                    piece = fill_copy(base + (length & ~(2 * size - 1)), size)
                    pl.when((length & size) != 0)(functools.partial(fn, piece))
                    size //= 2
                return c
            lax.fori_loop(0, N_EXPERTS, body, 0)

        def on_unused_blocks(fn):
            lax.fori_loop(nu_ref[0], n_blocks, lambda b, c: (fn(fill_copy(b * EXPERT_ROWS, EXPERT_ROWS)), c)[1], 0)

        on_padding(lambda cp: cp.start())
        on_unused_blocks(lambda cp: cp.start())
        on_padding(lambda cp: cp.wait())
        on_unused_blocks(lambda cp: cp.wait())

    def tile_step(src_ref, h_ref, src_hbm, first_row):
        t = src_ref.shape[0]

        def row_copy(j, k):
            src = src_ref.at[pl.ds(j, 1)] if k % 2 == 0 else src_hbm.at[pl.ds(first_row + j, 1)]
            return pltpu.make_async_copy(src, xs_ref.at[pl.ds(dest_ref[k, j], 1)], sem)

        def issue(j, c):
            for k in range(TOP_K):
                row_copy(j, k).start(priority=k % DMA_PRIORITIES)
            return c

        def drain(j, c):
            for k in range(TOP_K):
                row_copy(j, k).wait()
            return c

        lax.fori_loop(0, t, issue, 0, unroll=DMA_UNROLL)
        h = h_ref[...]
        hb = h.astype(BF16)
        a = jnp.dot(hb, wg_ref[...], preferred_element_type=F32)
        u = jnp.dot(hb, wu_ref[...], preferred_element_type=F32)
        shared = jnp.dot((a * jax.nn.sigmoid(a) * u).astype(BF16), wd_ref[...], preferred_element_type=F32)
        yp_ref[...] = ALPHA * h + shared
        lax.fori_loop(0, t, drain, 0, unroll=DMA_UNROLL)

    @pl.when(i < prompt_tiles)
    def _():
        tile_step(pp_ref, hp_ref, pp_hbm, i * pp_ref.shape[0])

    @pl.when(i >= prompt_tiles)
    def _():
        tile_step(ps_ref, hs_ref, ps_hbm, (i - prompt_tiles) * ps_ref.shape[0])


def _dispatch(fill_start, fill_len, n_used, dest_t, pk_p, pk_s, h_p, h_s, wg_b, wu_b, wd_b, *, n_blocks):
    t = DISPATCH_ROWS
    prompt_tiles = h_p.shape[0] // t
    sample_tiles = h_s.shape[0] // t
    tile = (t, WORD_SLABS, LANES)
    fixed = lambda i, *_: (0, 0)
    p_idx = lambda i: jnp.minimum(i, prompt_tiles - 1)
    s_idx = lambda i: jnp.maximum(i - prompt_tiles, 0)
    kern = functools.partial(_dispatch_kernel, prompt_tiles=prompt_tiles, n_blocks=n_blocks)
    return pl.pallas_call(
        kern,
        grid_spec=pltpu.PrefetchScalarGridSpec(
            num_scalar_prefetch=3,
            grid=(prompt_tiles + sample_tiles,),
            in_specs=[
                pl.BlockSpec((TOP_K, t), lambda i, *_: (0, i), memory_space=pltpu.SMEM),
                pl.BlockSpec(tile, lambda i, *_: (p_idx(i), 0, 0)),
                pl.BlockSpec(tile, lambda i, *_: (s_idx(i), 0, 0)),
                pl.BlockSpec((t, D_MODEL), lambda i, *_: (p_idx(i), 0)),
                pl.BlockSpec((t, D_MODEL), lambda i, *_: (s_idx(i), 0)),
                pl.BlockSpec((D_MODEL, F_EXPERT), fixed),
                pl.BlockSpec((D_MODEL, F_EXPERT), fixed),
                pl.BlockSpec((F_EXPERT, D_MODEL), fixed),
                pl.BlockSpec(memory_space=pl.ANY),
                pl.BlockSpec(memory_space=pl.ANY),
            ],
            out_specs=[pl.BlockSpec(memory_space=pl.ANY), pl.BlockSpec((t, D_MODEL), lambda i, *_: (i, 0))],
            scratch_shapes=[pltpu.VMEM((EXPERT_ROWS, WORD_SLABS, LANES), jnp.uint32), pltpu.SemaphoreType.DMA,
                            pltpu.SemaphoreType.DMA],
        ),
        out_shape=[jax.ShapeDtypeStruct((n_blocks * EXPERT_ROWS, WORD_SLABS, LANES), jnp.uint32),
                   jax.ShapeDtypeStruct(((prompt_tiles + sample_tiles) * t, D_MODEL), F32)],
        compiler_params=_params(1),
        name="dispatch",
    )(fill_start, fill_len, n_used, dest_t, pk_p, pk_s, h_p, h_s, wg_b, wu_b, wd_b, pk_p, pk_s)


def _block_plan(counts):
    padded = (counts + EXPERT_ROWS - 1) // EXPERT_ROWS * EXPERT_ROWS
    pad_end = jnp.cumsum(padded).astype(jnp.int32)
    pad_start = pad_end - padded
    n_used = pad_end[-1] // EXPERT_ROWS
    fill_start = pad_start + counts
    fill_len = pad_end - fill_start
    first_block = pad_start // EXPERT_ROWS
    n_expert_blocks = (padded // EXPERT_ROWS).astype(jnp.int32)
    return pad_start, fill_start, fill_len, first_block, n_expert_blocks, n_used.reshape(1).astype(jnp.int32)


def kernel(x_prompt, x_sample, cache_k, cache_v, w_in, sink, gm_ln_g, gm_ln_b, gm_w_s, gm_b_s, w_out, ln1_g, ln1_b,
           router_w, router_bias, w_gate_e, w_up_e, w_down_e, w_gate_s, w_up_s, w_down_s, ln2_g, ln2_b):
    bp, sp = x_prompt.shape[:2]
    bs, ts = x_sample.shape[:2]
    r = cache_k.shape[2]
    assert r == WINDOW and sp % ROW_TILE == 0 and (bs * ts) % ROW_TILE == 0
    n_p, n_s = bp * sp, bs * ts
    n_total = n_p + n_s
    l = 0

    w_in_b = w_in[l].astype(BF16)
    w_out_b = w_out[l].astype(BF16)
    router_t = router_w[l].T
    r_hi = router_t.astype(BF16)
    r_lo = (router_t - r_hi.astype(F32)).astype(BF16)
    row_vec = lambda v: v.reshape(1, -1)
    gm_g, gm_b = row_vec(gm_ln_g[l]), row_vec(gm_ln_b[l])
    sink_l = sink[l].astype(F32)

    xp2 = x_prompt.reshape(n_p, D_MODEL)
    tabs_p = _rope_tables(jnp.arange(sp, dtype=jnp.int32))
    q, k, v, gu, gv = _in_proj(xp2, w_in_b, tabs_p, gm_g, gm_b, tm=ROW_TILE, gv_dtype=BF16, name="in_proj_prompt")
    nb = sp // WINDOW
    att = _attention(sink_l, q, k, v, k, v, batch=bp, nb=nb, tq=WINDOW, seqs=1, chain=PROMPT_BLOCKS_PER_STEP,
                     prev_blocks=nb,
                     first_block_has_no_prev=True, name="attn_prompt")
    b_tab_p = jnp.repeat(gm_b_s[l].T, HEAD_DIM, axis=1)
    gm = _gate(gu, gv, gm_w_s[l], b_tab_p, chunk=CHUNK, n_chunks=ROW_TILE // CHUNK, name="gate_prompt")
    last_rows = lambda t: t.reshape(bp, sp, KV_WIDTH)[:, sp - r:, :].reshape(1, bp, r, KV_HEADS, HEAD_DIM)
    new_kp, new_vp = last_rows(k), last_rows(v)
    h_p, pk_p, lt_p = _out_proj(att, gm, xp2, w_out_b, row_vec(ln1_g[l]), row_vec(ln1_b[l]), r_hi, r_lo,
                          name="out_proj_prompt")

    xs2 = x_sample.reshape(n_s, D_MODEL)
    pos_s = PAST_LEN + jnp.arange(ts, dtype=jnp.int32)
    tabs_s = tuple(jnp.tile(t, (bs, 1)) for t in _rope_tables(pos_s))
    q, k, v, gu, gv = _in_proj(xs2, w_in_b, tabs_s, gm_g, gm_b, tm=n_s, gv_dtype=F32, name="in_proj_sample")
    tq = SLABS
    pad_rows = lambda t: jnp.pad(t.reshape(bs, ts, -1), ((0, 0), (0, tq - ts), (0, 0))).reshape(bs * tq, -1)
    ck2 = cache_k[l].reshape(bs * r, KV_WIDTH)
    cv2 = cache_v[l].reshape(bs * r, KV_WIDTH)
    att = _attention(sink_l, pad_rows(q), pad_rows(k), pad_rows(v), ck2, cv2, batch=bs, nb=1, tq=tq,
                     seqs=SAMPLE_SEQS_PER_STEP, chain=1, prev_blocks=1, first_block_has_no_prev=False, name="attn_sample")
    att = att.reshape(bs, tq, ATT_WIDTH)[:, :ts].reshape(n_s, ATT_WIDTH)
    gm = _gate_short(gu, gv, gm_w_s[l], gm_b_s[l], length=ts, name="gate_sample")
    new_ks = jnp.concatenate([cache_k[l], k.reshape(bs, ts, KV_HEADS, HEAD_DIM)], axis=1)[:, ts:][None]
    new_vs = jnp.concatenate([cache_v[l], v.reshape(bs, ts, KV_HEADS, HEAD_DIM)], axis=1)[:, ts:][None]
    new_gs = gv.reshape(bs, ts, GM_WIDTH)[None]
    h_s, pk_s, lt_s = _out_proj(att, gm, xs2, w_out_b, row_vec(ln1_g[l]), row_vec(ln1_b[l]), r_hi, r_lo,
                          name="out_proj_sample")

    eidx_t, w_t, rank_t, counts = _route(lt_p, lt_s, router_bias[l].astype(F32).reshape(N_EXPERTS, 1))
    a = n_total * TOP_K
    n_blocks = -(-(a + N_EXPERTS * (EXPERT_ROWS - 1)) // EXPERT_ROWS)
    pad_start, fill_start, fill_len, first_block, n_expert_blocks, n_used = _block_plan(counts.reshape(N_EXPERTS))
    dest_t = _dest(eidx_t, rank_t, pad_start.astype(F32).reshape(N_EXPERTS, 1))
    tiles = lambda a: a.reshape(-1, WORD_SLABS, LANES)
    shared = (w_gate_s[l].astype(BF16), w_up_s[l].astype(BF16), w_down_s[l].astype(BF16))
    x_sorted, y_part = _dispatch(fill_start, fill_len, n_used, dest_t, tiles(pk_p), tiles(pk_s), h_p, h_s, *shared,
                                 n_blocks=n_blocks)
    out_sorted = _experts(first_block, n_expert_blocks, n_used, x_sorted, w_gate_e[l], w_up_e[l], w_down_e[l])
    ln2 = (row_vec(ln2_g[l]), row_vec(ln2_b[l]))
    dest2, w2 = dest_t.T, w_t.T
    y_p = _combine(dest2, w2, y_part, *ln2, out_sorted, n=n_p, row_offset=0, name="combine_prompt")
    y_s = _combine(dest2, w2, y_part, *ln2, out_sorted, n=n_s, row_offset=n_p, name="combine_sample")
    return (y_p.reshape(bp, sp, D_MODEL), y_s.reshape(bs, ts, D_MODEL), new_kp, new_vp, new_ks, new_vs, new_gs)
```

```python
import functools

import jax
import jax.numpy as jnp
import numpy as np
from jax import lax
from jax.experimental import pallas as pl
from jax.experimental.pallas import tpu as pltpu

D_MODEL = 1024
HEAD_DIM = 64
ATT_HEADS = 8
KV_HEADS = 2
Q_PER_KV = ATT_HEADS // KV_HEADS
GM_HEADS = 8
ATT_WIDTH = ATT_HEADS * HEAD_DIM
KV_WIDTH = KV_HEADS * HEAD_DIM
GM_WIDTH = GM_HEADS * HEAD_DIM
ROPE_WIDTH = ATT_WIDTH + KV_WIDTH
IN_WIDTH = ATT_WIDTH + 2 * KV_WIDTH + 2 * GM_WIDTH
WINDOW = 128
CHUNK = 128
PAST_LEN = 16384
ROPE_THETA = 10000.0
ATT_SCALE = HEAD_DIM ** -0.5
N_EXPERTS = 256
TOP_K = 8
N_GROUPS = 8
TOPK_GROUPS = 4
F_EXPERT = 256
ROUTED_SCALE = 2.5
LN_EPS = 1e-5
DEPTH = 1
ALPHA = (2.0 * DEPTH) ** 0.25

LANES = 128
SLABS = 8
PACKED = D_MODEL // 2
WORD_SLABS = PACKED // LANES
ROW_TILE = 512
EXPERT_ROWS = 256
EXPERT_RING = 8
COMBINE_ROWS = 256
COMBINE_GROUP = 4
DISPATCH_ROWS = 256
ROUTE_TILE = 512
SAMPLE_SEQS_PER_STEP = 16
PROMPT_BLOCKS_PER_STEP = 2
DMA_UNROLL = 4
DMA_PRIORITIES = 2
VMEM_LIMIT = 56 * 1024 * 1024

F32 = jnp.float32
BF16 = jnp.bfloat16


def _params(n_axes):
    return pltpu.CompilerParams(dimension_semantics=("arbitrary",) * n_axes, vmem_limit_bytes=VMEM_LIMIT)


def _layer_norm(x, g, b):
    mu = jnp.mean(x, axis=-1, keepdims=True)
    xc = x - mu
    var = jnp.mean(xc * xc, axis=-1, keepdims=True)
    return xc * lax.rsqrt(var + LN_EPS) * g + b


def _pack_halves(x):
    xb = x.astype(BF16)
    lo = lax.bitcast_convert_type(xb[:, :PACKED].astype(F32), jnp.uint32)
    hi = lax.bitcast_convert_type(xb[:, PACKED:].astype(F32), jnp.uint32)
    return (lo >> 16) | hi


def _unpack_halves(w):
    lo = lax.bitcast_convert_type(w << 16, F32)
    hi = lax.bitcast_convert_type(w & jnp.uint32(0xFFFF0000), F32)
    return lo, hi


def _store_row_tiles(ref2d, x):
    m, slabs = x.shape[0], x.shape[1] // LANES
    for s in range(slabs):
        ref2d[pl.ds(s, m, stride=slabs), :] = x[:, s * LANES:(s + 1) * LANES]


def _load_row_tiles(ref2d, m, slabs):
    return jnp.concatenate([ref2d[pl.ds(s, m, stride=slabs), :] for s in range(slabs)], axis=1)


def _gelu(x):
    return 0.5 * x * (1.0 + lax.erf(x * np.float32(np.sqrt(0.5))))


def _in_proj_kernel(x_ref, w_ref, cos_ref, sa_ref, sb_ref, g_ref, b_ref, q_ref, k_ref, v_ref, gu_ref, gv_ref):
    x = x_ref[...].astype(BF16)
    zr = jnp.dot(x, w_ref[:, :ROPE_WIDTH], preferred_element_type=F32)
    pieces = []
    for c in range(ROPE_WIDTH // LANES):
        zc = zr[:, c * LANES:(c + 1) * LANES]
        tl = slice(0, LANES) if c < ATT_WIDTH // LANES else slice(LANES, 2 * LANES)
        pieces.append(zc * cos_ref[:, tl]
                      + pltpu.roll(zc, LANES - HEAD_DIM // 2, 1) * sa_ref[:, tl]
                      + pltpu.roll(zc, HEAD_DIM // 2, 1) * sb_ref[:, tl])
    for c in range(ATT_WIDTH // LANES):
        q_ref[:, c * LANES:(c + 1) * LANES] = pieces[c].astype(q_ref.dtype)
    k_ref[...] = pieces[ATT_WIDTH // LANES]
    v_ref[...] = jnp.dot(x, w_ref[:, ROPE_WIDTH:ROPE_WIDTH + KV_WIDTH], preferred_element_type=F32)
    g0 = ROPE_WIDTH + KV_WIDTH
    zu = jnp.dot(x, w_ref[:, g0:g0 + GM_WIDTH], preferred_element_type=F32)
    gu_ref[...] = _gelu(zu).astype(gu_ref.dtype)
    zv = jnp.dot(x, w_ref[:, g0 + GM_WIDTH:g0 + 2 * GM_WIDTH], preferred_element_type=F32)
    gv = _layer_norm(_gelu(zv), g_ref[...], b_ref[...])
    gv_ref[...] = gv.astype(gv_ref.dtype)


def _in_proj(x2, w_in_b, tabs, ln_g, ln_b, *, tm, gv_dtype, name):
    n = x2.shape[0]
    cos_t, sa_t, sb_t = tabs
    period = cos_t.shape[0] // tm
    row = lambda i: (i, 0)
    tab = lambda i: (i % period, 0)
    fixed = lambda i: (0, 0)
    return pl.pallas_call(
        _in_proj_kernel,
        grid=(n // tm,),
        in_specs=[
            pl.BlockSpec((tm, D_MODEL), row),
            pl.BlockSpec((D_MODEL, IN_WIDTH), fixed),
            pl.BlockSpec((tm, 2 * LANES), tab),
            pl.BlockSpec((tm, 2 * LANES), tab),
            pl.BlockSpec((tm, 2 * LANES), tab),
            pl.BlockSpec((1, GM_WIDTH), fixed),
            pl.BlockSpec((1, GM_WIDTH), fixed),
        ],
        out_specs=[
            pl.BlockSpec((tm, ATT_WIDTH), row),
            pl.BlockSpec((tm, KV_WIDTH), row),
            pl.BlockSpec((tm, KV_WIDTH), row),
            pl.BlockSpec((tm, GM_WIDTH), row),
            pl.BlockSpec((tm, GM_WIDTH), row),
        ],
        out_shape=[
            jax.ShapeDtypeStruct((n, ATT_WIDTH), BF16),
            jax.ShapeDtypeStruct((n, KV_WIDTH), F32),
            jax.ShapeDtypeStruct((n, KV_WIDTH), F32),
            jax.ShapeDtypeStruct((n, GM_WIDTH), BF16),
            jax.ShapeDtypeStruct((n, GM_WIDTH), gv_dtype),
        ],
        compiler_params=_params(1),
        name=name,
    )(x2, w_in_b, cos_t, sa_t, sb_t, ln_g, ln_b)


def _rope_tables(pos):
    half = HEAD_DIM // 2
    lane = jnp.arange(2 * LANES, dtype=jnp.int32)
    inv = ROPE_THETA ** (-(lane % half).astype(F32) * 2.0 / HEAD_DIM)
    ang = pos.astype(F32)[:, None] * inv[None, :]
    scale = jnp.where(lane < LANES, ATT_SCALE, 1.0).astype(F32)[None, :]
    first_half = ((lane % HEAD_DIM) < half)[None, :]
    cos, sin = jnp.cos(ang) * scale, jnp.sin(ang) * scale
    return cos, jnp.where(first_half, -sin, 0.0), jnp.where(first_half, 0.0, sin)


def _attn_kernel(sink_ref, q_ref, kc_ref, vc_ref, kp_ref, vp_ref, o_ref, *, tq, seqs, chain, stack,
                 first_block_has_no_prev):
    nk = WINDOW + tq
    rows = stack * tq
    qi = lax.broadcasted_iota(jnp.int32, (rows, nk), 0) & (tq - 1)
    ks = lax.broadcasted_iota(jnp.int32, (rows, nk), 1)
    band = (ks >= qi) & (ks <= qi + WINDOW)
    first_mask = band & ((pl.program_id(1) > 0) | (ks >= WINDOW)) if first_block_has_no_prev else band
    sinks = [jnp.concatenate([jnp.full((tq, 1), sink_ref[h0 + j], F32) for j in range(stack)], axis=0)
             for h0 in range(0, ATT_HEADS, stack)]
    for b in range(seqs * chain):
        qrows = slice(b * tq, (b + 1) * tq)
        if chain > 1 and b > 0:
            kprev, vprev = kc_ref[(b - 1) * tq:b * tq, :], vc_ref[(b - 1) * tq:b * tq, :]
        else:
            kprev, vprev = kp_ref[b * WINDOW:(b + 1) * WINDOW, :], vp_ref[b * WINDOW:(b + 1) * WINDOW, :]
        mask = band if (chain > 1 and b > 0) else first_mask
        q = q_ref[qrows, :]
        kk = jnp.concatenate([kprev, kc_ref[qrows, :]], axis=0).astype(BF16)
        vv = jnp.concatenate([vprev, vc_ref[qrows, :]], axis=0).astype(BF16)
        outs = []
        for i, h0 in enumerate(range(0, ATT_HEADS, stack)):
            g = h0 // Q_PER_KV
            kg = kk[:, g * HEAD_DIM:(g + 1) * HEAD_DIM]
            vg = vv[:, g * HEAD_DIM:(g + 1) * HEAD_DIM]
            qg = jnp.concatenate([q[:, (h0 + j) * HEAD_DIM:(h0 + j + 1) * HEAD_DIM] for j in range(stack)], axis=0)
            s = lax.dot_general(qg, kg, (((1,), (1,)), ((), ())), preferred_element_type=F32)
            s = jnp.where(mask, s, -jnp.inf)
            m = jnp.maximum(jnp.max(s, axis=-1, keepdims=True), sinks[i])
            p = jnp.exp(s - m)
            denom = jnp.sum(p, axis=-1, keepdims=True) + jnp.exp(sinks[i] - m)
            o = jnp.dot((p / denom).astype(BF16), vg, preferred_element_type=F32)
            outs.extend(o[j * tq:(j + 1) * tq, :] for j in range(stack))
        o_ref[qrows, :] = jnp.concatenate(outs, axis=1).astype(o_ref.dtype)


def _attention(sink, q2, k2, v2, kprev2, vprev2, *, batch, nb, tq, seqs, chain, prev_blocks, first_block_has_no_prev,
               name):
    assert tq & (tq - 1) == 0 and batch % seqs == 0 and (seqs == 1 or nb == prev_blocks == 1)
    assert (seqs == 1 or chain == 1) and nb % chain == 0 and (chain == 1 or tq == WINDOW)
    steps = nb // chain
    cur = lambda b, n, s: (b * steps + n, 0)
    prev = lambda b, n, s: (b * prev_blocks + jnp.maximum(n * chain - 1, 0), 0)
    stack = Q_PER_KV if Q_PER_KV * tq <= WINDOW else 1
    kern = functools.partial(_attn_kernel, tq=tq, seqs=seqs, chain=chain, stack=stack,
                             first_block_has_no_prev=first_block_has_no_prev)
    return pl.pallas_call(
        kern,
        grid_spec=pltpu.PrefetchScalarGridSpec(
            num_scalar_prefetch=1,
            grid=(batch // seqs, steps),
            in_specs=[
                pl.BlockSpec((seqs * chain * tq, ATT_WIDTH), cur),
                pl.BlockSpec((seqs * chain * tq, KV_WIDTH), cur),
                pl.BlockSpec((seqs * chain * tq, KV_WIDTH), cur),
                pl.BlockSpec((seqs * WINDOW, KV_WIDTH), prev),
                pl.BlockSpec((seqs * WINDOW, KV_WIDTH), prev),
            ],
            out_specs=pl.BlockSpec((seqs * chain * tq, ATT_WIDTH), cur),
        ),
        out_shape=jax.ShapeDtypeStruct(q2.shape, BF16),
        compiler_params=_params(2),
        name=name,
    )(sink, q2, k2, v2, kprev2, vprev2)


def _gate_kernel(gu_ref, gv_ref, w_ref, b_ref, o_ref, *, chunk, n_chunks):
    ri = lax.broadcasted_iota(jnp.int32, (chunk, chunk), 0)
    ci = lax.broadcasted_iota(jnp.int32, (chunk, chunk), 1)
    ws = [jnp.where(ci <= ri, w_ref[h], 0.0).astype(BF16) for h in range(GM_HEADS)]
    for c in range(n_chunks):
        rows = slice(c * chunk, (c + 1) * chunk)
        gv = gv_ref[rows, :].astype(BF16)
        sv = jnp.concatenate(
            [jnp.dot(ws[h], gv[:, h * HEAD_DIM:(h + 1) * HEAD_DIM], preferred_element_type=F32)
             for h in range(GM_HEADS)], axis=1)
        o_ref[rows, :] = (gu_ref[rows, :].astype(F32) * (sv + b_ref[...])).astype(o_ref.dtype)


def _gate(gu2, gv2, w_s, b_tab, *, chunk, n_chunks, name):
    n = gu2.shape[0]
    tm = chunk * n_chunks
    row = lambda i: (i, 0)
    kern = functools.partial(_gate_kernel, chunk=chunk, n_chunks=n_chunks)
    return pl.pallas_call(
        kern,
        grid=(n // tm,),
        in_specs=[
            pl.BlockSpec((tm, GM_WIDTH), row),
            pl.BlockSpec((tm, GM_WIDTH), row),
            pl.BlockSpec((GM_HEADS, chunk, chunk), lambda i: (0, 0, 0)),
            pl.BlockSpec((chunk, GM_WIDTH), lambda i: (0, 0)),
        ],
        out_specs=pl.BlockSpec((tm, GM_WIDTH), row),
        out_shape=jax.ShapeDtypeStruct((n, GM_WIDTH), BF16),
        compiler_params=_params(1),
        name=name,
    )(gu2, gv2, w_s, b_tab)


def _gate_short_kernel(gu_ref, gv_ref, coef_ref, b_ref, o_ref, *, length):
    n = gv_ref.shape[0]
    gv = gv_ref[...].astype(F32)
    tiles = lambda x: x.reshape(n // SLABS, SLABS, GM_WIDTH)
    sv = tiles(gv) * coef_ref[0][None] + b_ref[...][None]
    for d in range(1, length):
        sv = sv + tiles(pltpu.roll(gv, d, 0)) * coef_ref[d][None]
    o_ref[...] = (gu_ref[...].astype(F32) * sv.reshape(n, GM_WIDTH)).astype(o_ref.dtype)


def _gate_short(gu2, gv2, w_s, b_s, *, length, name):
    n = gu2.shape[0]
    assert SLABS % length == 0 and n % SLABS == 0
    pos = jnp.arange(length)
    lag = jnp.arange(length)
    src = pos[None, :] - lag[:, None]
    coef = jnp.where(src >= 0, w_s[:, pos[None, :], jnp.maximum(src, 0)], 0.0)
    coef = jnp.repeat(coef.transpose(1, 2, 0), HEAD_DIM, axis=2)
    coef = jnp.tile(coef, (1, SLABS // length, 1))
    bias = jnp.tile(jnp.repeat(b_s[:, :length].T, HEAD_DIM, axis=1), (SLABS // length, 1))
    whole = lambda shape: pl.BlockSpec(shape, lambda i: (0,) * len(shape))
    return pl.pallas_call(
        functools.partial(_gate_short_kernel, length=length),
        grid=(1,),
        in_specs=[whole((n, GM_WIDTH)), whole((n, GM_WIDTH)), whole((length, SLABS, GM_WIDTH)),
                  whole((SLABS, GM_WIDTH))],
        out_specs=whole((n, GM_WIDTH)),
        out_shape=jax.ShapeDtypeStruct((n, GM_WIDTH), BF16),
        compiler_params=_params(1),
        name=name,
    )(gu2, gv2, coef, bias)


def _out_proj_kernel(att_ref, gm_ref, x_ref, wo_ref, g_ref, b_ref, rh_ref, rl_ref, h_ref, hp_ref, lg_ref):
    mix = jnp.dot(att_ref[...], wo_ref[:ATT_WIDTH, :], preferred_element_type=F32)
    mix = mix + jnp.dot(gm_ref[...], wo_ref[ATT_WIDTH:, :], preferred_element_type=F32)
    h = _layer_norm(ALPHA * x_ref[...] + mix, g_ref[...], b_ref[...])
    h_ref[...] = h
    _store_row_tiles(hp_ref, _pack_halves(h))
    h_hi = h.astype(BF16)
    h_lo = (h - h_hi.astype(F32)).astype(BF16)
    nt = (((1,), (1,)), ((), ()))
    lg = lax.dot_general(rh_ref[...], h_hi, nt, preferred_element_type=F32)
    lg = lg + lax.dot_general(rh_ref[...], h_lo, nt, preferred_element_type=F32)
    lg = lg + lax.dot_general(rl_ref[...], h_hi, nt, preferred_element_type=F32)
    lg_ref[...] = lg


def _out_proj(att2, gm2, x2, w_out_b, ln_g, ln_b, r_hi, r_lo, *, name):
    n = x2.shape[0]
    tm = ROW_TILE
    row = lambda i: (i, 0)
    fixed = lambda i: (0, 0)
    return pl.pallas_call(
        _out_proj_kernel,
        grid=(n // tm,),
        in_specs=[
            pl.BlockSpec((tm, ATT_WIDTH), row),
            pl.BlockSpec((tm, GM_WIDTH), row),
            pl.BlockSpec((tm, D_MODEL), row),
            pl.BlockSpec((D_MODEL, D_MODEL), fixed),
            pl.BlockSpec((1, D_MODEL), fixed),
            pl.BlockSpec((1, D_MODEL), fixed),
            pl.BlockSpec((N_EXPERTS, D_MODEL), fixed),
            pl.BlockSpec((N_EXPERTS, D_MODEL), fixed),
        ],
        out_specs=[
            pl.BlockSpec((tm, D_MODEL), row),
            pl.BlockSpec((tm * WORD_SLABS, LANES), row),
            pl.BlockSpec((N_EXPERTS, tm), lambda i: (0, i)),
        ],
        out_shape=[
            jax.ShapeDtypeStruct((n, D_MODEL), F32),
            jax.ShapeDtypeStruct((n * WORD_SLABS, LANES), jnp.uint32),
            jax.ShapeDtypeStruct((N_EXPERTS, n), F32),
        ],
        compiler_params=_params(1),
        name=name,
    )(att2, gm2, x2, w_out_b, ln_g, ln_b, r_hi, r_lo)


def _experts_kernel(first_ref, nblk_ref, nu_ref, x_hbm, wg_ref, wu_ref, wd_ref, o_hbm, xbuf, obuf, wg_s, wu_s, wd_s,
                    in_sem, out_sem, fill_sem, *, n_blocks):
    e = pl.program_id(0)
    nb = nblk_ref[e]
    b0 = first_ref[e]
    n_used = nu_ref[0]
    ahead = EXPERT_RING // 2

    def rows_of(block):
        return pl.ds(pl.multiple_of(block * EXPERT_ROWS, EXPERT_ROWS), EXPERT_ROWS)

    def slot_of(block):
        return block & (EXPERT_RING - 1)

    def slab(s):
        return pl.ds(s * LANES, LANES)

    def in_copies(block):
        slot = slot_of(block)
        return [pltpu.make_async_copy(x_hbm.at[rows_of(block), s, :], xbuf.at[slot, :, slab(s)], in_sem.at[slot])
                for s in range(WORD_SLABS)]

    def out_copies(block, slot=None, sem=None):
        slot = slot_of(block) if slot is None else slot
        sem = out_sem.at[slot] if sem is None else sem
        return [pltpu.make_async_copy(obuf.at[slot, :, slab(s)], o_hbm.at[rows_of(block), s, :], sem)
                for s in range(WORD_SLABS)]

    def start(copies):
        for cp in copies:
            cp.start()

    def wait(copies):
        for cp in copies:
            cp.wait()

    @pl.when(e == 0)
    def _():
        for j in range(ahead):
            pl.when(j < n_used)(lambda j=j: start(in_copies(j)))

    def process(block, count):
        blocks = [block + j for j in range(count)]
        for blk in blocks:
            wait(in_copies(blk))
            pl.when(blk + ahead < n_used)(lambda blk=blk: start(in_copies(blk + ahead)))
            pl.when(blk >= ahead)(lambda blk=blk: wait(out_copies(blk - ahead)))
        outs = []
        for blk in blocks:
            xl, xh = _unpack_halves(xbuf[slot_of(blk)])
            xl, xh = xl.astype(BF16), xh.astype(BF16)
            a = (jnp.dot(xl, wg_s[:PACKED, :], preferred_element_type=F32)
                 + jnp.dot(xh, wg_s[PACKED:, :], preferred_element_type=F32))
            u = (jnp.dot(xl, wu_s[:PACKED, :], preferred_element_type=F32)
                 + jnp.dot(xh, wu_s[PACKED:, :], preferred_element_type=F32))
            hb = (a * jax.nn.sigmoid(a) * u).astype(BF16)
            outs.append(_pack_halves(jnp.dot(hb, wd_s[...], preferred_element_type=F32)))
        for blk, o in zip(blocks, outs):
            obuf[slot_of(blk)] = o
            start(out_copies(blk))

    @pl.when(nb > 0)
    def _():
        wg_s[...] = wg_ref[0].astype(BF16)
        wu_s[...] = wu_ref[0].astype(BF16)
        wd_s[...] = wd_ref[0].astype(BF16)

        def pair(i, carry):
            process(b0 + 2 * i, 2)
            return carry

        lax.fori_loop(0, nb // 2, pair, 0)
        pl.when(nb % 2 == 1)(lambda: process(b0 + nb - 1, 1))

    @pl.when(e == N_EXPERTS - 1)
    def _():
        for j in range(ahead):
            pl.when(n_used - 1 - j >= 0)(lambda j=j: wait(out_copies(n_used - 1 - j)))
        obuf[0] = jnp.zeros((EXPERT_ROWS, PACKED), jnp.uint32)

        def on_unused_blocks(fn):
            def body(b, c):
                fn(out_copies(b, slot=0, sem=fill_sem))
                return c
            lax.fori_loop(n_used, n_blocks, body, 0)

        on_unused_blocks(start)
        on_unused_blocks(wait)


def _experts(first_block, n_expert_blocks, n_used, x_sorted, w_gate_e, w_up_e, w_down_e):
    rows = x_sorted.shape[0]
    wmap = lambda e, *_: (e, 0, 0)
    kern = functools.partial(_experts_kernel, n_blocks=rows // EXPERT_ROWS)
    return pl.pallas_call(
        kern,
        grid_spec=pltpu.PrefetchScalarGridSpec(
            num_scalar_prefetch=3,
            grid=(N_EXPERTS,),
            in_specs=[
                pl.BlockSpec(memory_space=pl.ANY),
                pl.BlockSpec((1, D_MODEL, F_EXPERT), wmap),
                pl.BlockSpec((1, D_MODEL, F_EXPERT), wmap),
                pl.BlockSpec((1, F_EXPERT, D_MODEL), wmap),
            ],
            out_specs=pl.BlockSpec(memory_space=pl.ANY),
            scratch_shapes=[
                pltpu.VMEM((EXPERT_RING, EXPERT_ROWS, PACKED), jnp.uint32),
                pltpu.VMEM((EXPERT_RING, EXPERT_ROWS, PACKED), jnp.uint32),
                pltpu.VMEM((D_MODEL, F_EXPERT), BF16),
                pltpu.VMEM((D_MODEL, F_EXPERT), BF16),
                pltpu.VMEM((F_EXPERT, D_MODEL), BF16),
                pltpu.SemaphoreType.DMA((EXPERT_RING,)),
                pltpu.SemaphoreType.DMA((EXPERT_RING,)),
                pltpu.SemaphoreType.DMA,
            ],
        ),
        out_shape=jax.ShapeDtypeStruct(x_sorted.shape, x_sorted.dtype),
        compiler_params=_params(1),
        name="experts",
    )(first_block, n_expert_blocks, n_used, x_sorted, w_gate_e, w_up_e, w_down_e)


def _combine_kernel(d_ref, w_ref, yp_ref, ln_g_ref, ln_b_ref, os_ref, y_ref, gbuf0, gbuf1, rlo, rhi, sem):
    s = pl.program_id(0)
    n_tiles = pl.num_programs(0) - 1
    t = y_ref.shape[0]
    gtiles = [g.reshape(t * TOP_K, WORD_SLABS, LANES) for g in (gbuf0, gbuf1)]
    lo_tiles = rlo.reshape(t, WORD_SLABS, LANES)
    hi_tiles = rhi.reshape(t, WORD_SLABS, LANES)

    def row_copy(row, buf, j, k):
        return pltpu.make_async_copy(os_ref.at[pl.ds(row, 1)], gtiles[buf].at[pl.ds(j * TOP_K + k, 1)], sem.at[buf])

    def drain(buf):
        def body(j, c):
            for k in range(TOP_K):
                row_copy(0, buf, j, k).wait()
            return c
        lax.fori_loop(0, t, body, 0, unroll=DMA_UNROLL)

    def weighted_sum(rows, j):
        lo, hi = _unpack_halves(rows)
        acc_lo = w_ref[j, 0] * lo[0:1]
        acc_hi = w_ref[j, 0] * hi[0:1]
        for k in range(1, TOP_K):
            acc_lo = acc_lo + w_ref[j, k] * lo[k:k + 1]
            acc_hi = acc_hi + w_ref[j, k] * hi[k:k + 1]
        lo_tiles[pl.ds(j, 1)] = acc_lo
        hi_tiles[pl.ds(j, 1)] = acc_hi

    def token_loop(new, old, start_new, sum_old):
        def body(g, c):
            tokens = [g * COMBINE_GROUP + u for u in range(COMBINE_GROUP)]
            rows = [gtiles[old][pl.ds(j * TOP_K, TOP_K)] for j in tokens] if sum_old else []
            if start_new:
                for j in tokens:
                    for k in range(TOP_K):
                        row_copy(d_ref[j, k], new, j, k).start(priority=k % DMA_PRIORITIES)
            for r, j in zip(rows, tokens):
                weighted_sum(r, j)
            return c
        lax.fori_loop(0, t // COMBINE_GROUP, body, 0)

    def step(new):
        old = 1 - new
        pl.when(s == 0)(lambda: token_loop(new, old, True, False))

        @pl.when(s >= 1)
        def _():
            drain(old)
            pl.when(s < n_tiles)(lambda: token_loop(new, old, True, True))
            pl.when(s == n_tiles)(lambda: token_loop(new, old, False, True))
            routed = jnp.concatenate([_load_row_tiles(rlo, t, WORD_SLABS), _load_row_tiles(rhi, t, WORD_SLABS)],
                                     axis=1)
            y_ref[...] = _layer_norm(yp_ref[...] + routed, ln_g_ref[...], ln_b_ref[...])

    pl.when(s % 2 == 0)(lambda: step(0))
    pl.when(s % 2 == 1)(lambda: step(1))


def _combine(dest2, w2, y_part, ln_g, ln_b, out_sorted, *, n, row_offset, name):
    tm = COMBINE_ROWS
    off = row_offset // tm
    n_tiles = n // tm
    fixed = lambda s: (0, 0)
    new_tile = lambda s: (jnp.minimum(s, n_tiles - 1) + off, 0)
    old_tile = lambda s: (jnp.maximum(s - 1, 0) + off, 0)
    return pl.pallas_call(
        _combine_kernel,
        grid=(n_tiles + 1,),
        in_specs=[
            pl.BlockSpec((tm, TOP_K), new_tile, memory_space=pltpu.SMEM),
            pl.BlockSpec((tm, TOP_K), old_tile, memory_space=pltpu.SMEM),
            pl.BlockSpec((tm, D_MODEL), old_tile),
            pl.BlockSpec((1, D_MODEL), fixed),
            pl.BlockSpec((1, D_MODEL), fixed),
            pl.BlockSpec(memory_space=pl.ANY),
        ],
        out_specs=pl.BlockSpec((tm, D_MODEL), lambda s: (jnp.maximum(s - 1, 0), 0)),
        out_shape=jax.ShapeDtypeStruct((n, D_MODEL), F32),
        scratch_shapes=[pltpu.VMEM((TOP_K * tm * WORD_SLABS, LANES), jnp.uint32),
                        pltpu.VMEM((TOP_K * tm * WORD_SLABS, LANES), jnp.uint32),
                        pltpu.VMEM((tm * WORD_SLABS, LANES), F32), pltpu.VMEM((tm * WORD_SLABS, LANES), F32),
                        pltpu.SemaphoreType.DMA((2,))],
        compiler_params=_params(1),
        name=name,
    )(dest2, w2, y_part, ln_g, ln_b, out_sorted)


def _route_kernel(lgp_ref, lgs_ref, bias_ref, eidx_ref, w_ref, rank_ref, cnt_ref, carry_ref, *, prompt_tiles):
    @pl.when(pl.program_id(0) == 0)
    def _():
        carry_ref[...] = jnp.zeros_like(carry_ref)

    t = lgp_ref.shape[1]
    gsz = N_EXPERTS // N_GROUPS
    neg = -jnp.inf
    s = jax.nn.sigmoid(jnp.where(pl.program_id(0) < prompt_tiles, lgp_ref[...], lgs_ref[...]))
    biased = s + bias_ref[...]
    io_g = lax.broadcasted_iota(jnp.int32, (gsz, t), 0)
    grp_rows = []
    for g in range(N_GROUPS):
        blk = biased[g * gsz:(g + 1) * gsz, :]
        m1 = jnp.max(blk, axis=0, keepdims=True)
        i1 = jnp.min(jnp.where(blk == m1, io_g, gsz), axis=0, keepdims=True)
        m2 = jnp.max(jnp.where(io_g == i1, neg, blk), axis=0, keepdims=True)
        grp_rows.append(m1 + m2)
    gs = jnp.concatenate(grp_rows, axis=0)
    io8 = lax.broadcasted_iota(jnp.int32, (N_GROUPS, t), 0)
    gsel = jnp.zeros((N_GROUPS, t), jnp.int32)
    for _ in range(TOPK_GROUPS):
        m = jnp.max(gs, axis=0, keepdims=True)
        gi = jnp.min(jnp.where(gs == m, io8, N_GROUPS), axis=0, keepdims=True)
        hit = io8 == gi
        gsel = jnp.where(hit, 1, gsel)
        gs = jnp.where(hit, neg, gs)
    masked = jnp.concatenate(
        [jnp.where(gsel[g:g + 1, :] > 0, biased[g * gsz:(g + 1) * gsz, :], neg) for g in range(N_GROUPS)], axis=0)

    eio = lax.broadcasted_iota(jnp.int32, (N_EXPERTS, t), 0)
    cur = masked
    idx_rows, w_rows = [], []
    for _ in range(TOP_K):
        m = jnp.max(cur, axis=0, keepdims=True)
        idx = jnp.min(jnp.where(cur == m, eio, N_EXPERTS), axis=0, keepdims=True)
        hit = eio == idx
        w_rows.append(jnp.sum(jnp.where(hit, s, 0.0), axis=0, keepdims=True))
        cur = jnp.where(hit, neg, cur)
        idx_rows.append(idx)
    sel = jnp.where(cur != masked, 1.0, 0.0)

    tri = jnp.where(lax.broadcasted_iota(jnp.int32, (t, t), 0) < lax.broadcasted_iota(jnp.int32, (t, t), 1), 1.0, 0.0)
    pref = jnp.dot(sel.astype(BF16), tri.astype(BF16), preferred_element_type=F32) + carry_ref[...]
    rank_rows = [jnp.sum(jnp.where(eio == idx_rows[k], pref, 0.0), axis=0, keepdims=True) for k in range(TOP_K)]
    carry_ref[...] += jnp.sum(sel, axis=1, keepdims=True)

    wk = jnp.concatenate(w_rows, axis=0)
    eidx_ref[...] = jnp.concatenate(idx_rows, axis=0)
    w_ref[...] = wk / jnp.sum(wk, axis=0, keepdims=True) * ROUTED_SCALE
    rank_ref[...] = jnp.concatenate(rank_rows, axis=0).astype(jnp.int32)
    cnt_ref[...] = carry_ref[...].astype(jnp.int32)


def _route(logits_p, logits_s, bias_col):
    t = ROUTE_TILE
    prompt_tiles = logits_p.shape[1] // t
    n = logits_p.shape[1] + logits_s.shape[1]
    col = lambda i: (0, i)
    fixed = lambda i: (0, 0)
    kern = functools.partial(_route_kernel, prompt_tiles=prompt_tiles)
    return pl.pallas_call(
        kern,
        grid=(n // t,),
        in_specs=[pl.BlockSpec((N_EXPERTS, t), lambda i: (0, jnp.minimum(i, prompt_tiles - 1))),
                  pl.BlockSpec((N_EXPERTS, t), lambda i: (0, jnp.maximum(i - prompt_tiles, 0))),
                  pl.BlockSpec((N_EXPERTS, 1), fixed)],
        out_specs=[
            pl.BlockSpec((TOP_K, t), col),
            pl.BlockSpec((TOP_K, t), col),
            pl.BlockSpec((TOP_K, t), col),
            pl.BlockSpec((N_EXPERTS, 1), fixed),
        ],
        out_shape=[
            jax.ShapeDtypeStruct((TOP_K, n), jnp.int32),
            jax.ShapeDtypeStruct((TOP_K, n), F32),
            jax.ShapeDtypeStruct((TOP_K, n), jnp.int32),
            jax.ShapeDtypeStruct((N_EXPERTS, 1), jnp.int32),
        ],
        scratch_shapes=[pltpu.VMEM((N_EXPERTS, 1), F32)],
        compiler_params=_params(1),
        name="route",
    )(logits_p, logits_s, bias_col)


def _dest_kernel(eidx_ref, rank_ref, start_ref, dest_ref):
    t = eidx_ref.shape[1]
    eio = lax.broadcasted_iota(jnp.int32, (N_EXPERTS, t), 0)
    start = start_ref[...]
    rows = [jnp.sum(jnp.where(eio == eidx_ref[k:k + 1, :], start, 0.0), axis=0, keepdims=True) for k in range(TOP_K)]
    dest_ref[...] = jnp.concatenate(rows, axis=0).astype(jnp.int32) + rank_ref[...]


def _dest(eidx_t, rank_t, pad_start_col):
    n = eidx_t.shape[1]
    t = ROW_TILE
    col = lambda i: (0, i)
    return pl.pallas_call(
        _dest_kernel,
        grid=(n // t,),
        in_specs=[pl.BlockSpec((TOP_K, t), col), pl.BlockSpec((TOP_K, t), col),
                  pl.BlockSpec((N_EXPERTS, 1), lambda i: (0, 0))],
        out_specs=pl.BlockSpec((TOP_K, t), col),
        out_shape=jax.ShapeDtypeStruct((TOP_K, n), jnp.int32),
        compiler_params=_params(1),
        name="dest",
    )(eidx_t, rank_t, pad_start_col)


def _dispatch_kernel(fill_ref, len_ref, nu_ref, dest_ref, pp_ref, ps_ref, hp_ref, hs_ref, wg_ref, wu_ref, wd_ref,
                     xs_ref, yp_ref, zbuf, sem, fill_sem, *, prompt_tiles, n_blocks):
    i = pl.program_id(0)

    @pl.when(i == 0)
    def _():
        zbuf[...] = jnp.zeros_like(zbuf)

        def fill_copy(row0, size):
            return pltpu.make_async_copy(zbuf.at[pl.ds(0, size)], xs_ref.at[pl.ds(row0, size)], fill_sem)

        def on_padding(fn):
            def body(e, c):
                base, length = fill_ref[e], len_ref[e]
                size = EXPERT_ROWS // 2
                while size >= 1:
                    piece = fill_copy(base + (length & ~(2 * size - 1)), size)
                    pl.when((length & size) != 0)(functools.partial(fn, piece))
                    size //= 2
                return c
            lax.fori_loop(0, N_EXPERTS, body, 0)

        def on_unused_blocks(fn):
            lax.fori_loop(nu_ref[0], n_blocks, lambda b, c: (fn(fill_copy(b * EXPERT_ROWS, EXPERT_ROWS)), c)[1], 0)

        on_padding(lambda cp: cp.start())
        on_unused_blocks(lambda cp: cp.start())
        on_padding(lambda cp: cp.wait())
        on_unused_blocks(lambda cp: cp.wait())

    def tile_step(src_ref, h_ref):
        t = src_ref.shape[0]

        def row_copy(j, k):
            return pltpu.make_async_copy(src_ref.at[pl.ds(j, 1)], xs_ref.at[pl.ds(dest_ref[k, j], 1)], sem)

        def issue(j, c):
            for k in range(TOP_K):
                row_copy(j, k).start(priority=k % DMA_PRIORITIES)
            return c

        def drain(j, c):
            for k in range(TOP_K):
                row_copy(j, k).wait()
            return c

        lax.fori_loop(0, t, issue, 0, unroll=DMA_UNROLL)
        h = h_ref[...]
        hb = h.astype(BF16)
        a = jnp.dot(hb, wg_ref[...], preferred_element_type=F32)
        u = jnp.dot(hb, wu_ref[...], preferred_element_type=F32)
        shared = jnp.dot((a * jax.nn.sigmoid(a) * u).astype(BF16), wd_ref[...], preferred_element_type=F32)
        yp_ref[...] = ALPHA * h + shared
        lax.fori_loop(0, t, drain, 0, unroll=DMA_UNROLL)

    @pl.when(i < prompt_tiles)
    def _():
        tile_step(pp_ref, hp_ref)

    @pl.when(i >= prompt_tiles)
    def _():
        tile_step(ps_ref, hs_ref)


def _dispatch(fill_start, fill_len, n_used, dest_t, pk_p, pk_s, h_p, h_s, wg_b, wu_b, wd_b, *, n_blocks):
    t = DISPATCH_ROWS
    prompt_tiles = h_p.shape[0] // t
    sample_tiles = h_s.shape[0] // t
    tile = (t, WORD_SLABS, LANES)
    fixed = lambda i, *_: (0, 0)
    p_idx = lambda i: jnp.minimum(i, prompt_tiles - 1)
    s_idx = lambda i: jnp.maximum(i - prompt_tiles, 0)
    kern = functools.partial(_dispatch_kernel, prompt_tiles=prompt_tiles, n_blocks=n_blocks)
    return pl.pallas_call(
        kern,
        grid_spec=pltpu.PrefetchScalarGridSpec(
            num_scalar_prefetch=3,
            grid=(prompt_tiles + sample_tiles,),
            in_specs=[
                pl.BlockSpec((TOP_K, t), lambda i, *_: (0, i), memory_space=pltpu.SMEM),
                pl.BlockSpec(tile, lambda i, *_: (p_idx(i), 0, 0)),
                pl.BlockSpec(tile, lambda i, *_: (s_idx(i), 0, 0)),
                pl.BlockSpec((t, D_MODEL), lambda i, *_: (p_idx(i), 0)),
                pl.BlockSpec((t, D_MODEL), lambda i, *_: (s_idx(i), 0)),
                pl.BlockSpec((D_MODEL, F_EXPERT), fixed),
                pl.BlockSpec((D_MODEL, F_EXPERT), fixed),
                pl.BlockSpec((F_EXPERT, D_MODEL), fixed),
            ],
            out_specs=[pl.BlockSpec(memory_space=pl.ANY), pl.BlockSpec((t, D_MODEL), lambda i, *_: (i, 0))],
            scratch_shapes=[pltpu.VMEM((EXPERT_ROWS, WORD_SLABS, LANES), jnp.uint32), pltpu.SemaphoreType.DMA,
                            pltpu.SemaphoreType.DMA],
        ),
        out_shape=[jax.ShapeDtypeStruct((n_blocks * EXPERT_ROWS, WORD_SLABS, LANES), jnp.uint32),
                   jax.ShapeDtypeStruct(((prompt_tiles + sample_tiles) * t, D_MODEL), F32)],
        compiler_params=_params(1),
        name="dispatch",
    )(fill_start, fill_len, n_used, dest_t, pk_p, pk_s, h_p, h_s, wg_b, wu_b, wd_b)


def _block_plan(counts):
    padded = (counts + EXPERT_ROWS - 1) // EXPERT_ROWS * EXPERT_ROWS
    pad_end = jnp.cumsum(padded).astype(jnp.int32)
    pad_start = pad_end - padded
    n_used = pad_end[-1] // EXPERT_ROWS
    fill_start = pad_start + counts
    fill_len = pad_end - fill_start
    first_block = pad_start // EXPERT_ROWS
    n_expert_blocks = (padded // EXPERT_ROWS).astype(jnp.int32)
    return pad_start, fill_start, fill_len, first_block, n_expert_blocks, n_used.reshape(1).astype(jnp.int32)


def kernel(x_prompt, x_sample, cache_k, cache_v, w_in, sink, gm_ln_g, gm_ln_b, gm_w_s, gm_b_s, w_out, ln1_g, ln1_b,
           router_w, router_bias, w_gate_e, w_up_e, w_down_e, w_gate_s, w_up_s, w_down_s, ln2_g, ln2_b):
    bp, sp = x_prompt.shape[:2]
    bs, ts = x_sample.shape[:2]
    r = cache_k.shape[2]
    assert r == WINDOW and sp % ROW_TILE == 0 and (bs * ts) % ROW_TILE == 0
    n_p, n_s = bp * sp, bs * ts
    n_total = n_p + n_s
    l = 0

    w_in_b = w_in[l].astype(BF16)
    w_out_b = w_out[l].astype(BF16)
    router_t = router_w[l].T
    r_hi = router_t.astype(BF16)
    r_lo = (router_t - r_hi.astype(F32)).astype(BF16)
    row_vec = lambda v: v.reshape(1, -1)
    gm_g, gm_b = row_vec(gm_ln_g[l]), row_vec(gm_ln_b[l])
    sink_l = sink[l].astype(F32)

    xp2 = x_prompt.reshape(n_p, D_MODEL)
    tabs_p = _rope_tables(jnp.arange(sp, dtype=jnp.int32))
    q, k, v, gu, gv = _in_proj(xp2, w_in_b, tabs_p, gm_g, gm_b, tm=ROW_TILE, gv_dtype=BF16, name="in_proj_prompt")
    nb = sp // WINDOW
    att = _attention(sink_l, q, k, v, k, v, batch=bp, nb=nb, tq=WINDOW, seqs=1, chain=PROMPT_BLOCKS_PER_STEP,
                     prev_blocks=nb,
                     first_block_has_no_prev=True, name="attn_prompt")
    b_tab_p = jnp.repeat(gm_b_s[l].T, HEAD_DIM, axis=1)
    gm = _gate(gu, gv, gm_w_s[l], b_tab_p, chunk=CHUNK, n_chunks=ROW_TILE // CHUNK, name="gate_prompt")
    last_rows = lambda t: t.reshape(bp, sp, KV_WIDTH)[:, sp - r:, :].reshape(1, bp, r, KV_HEADS, HEAD_DIM)
    new_kp, new_vp = last_rows(k), last_rows(v)
    h_p, pk_p, lt_p = _out_proj(att, gm, xp2, w_out_b, row_vec(ln1_g[l]), row_vec(ln1_b[l]), r_hi, r_lo,
                          name="out_proj_prompt")

    xs2 = x_sample.reshape(n_s, D_MODEL)
    pos_s = PAST_LEN + jnp.arange(ts, dtype=jnp.int32)
    tabs_s = tuple(jnp.tile(t, (bs, 1)) for t in _rope_tables(pos_s))
    q, k, v, gu, gv = _in_proj(xs2, w_in_b, tabs_s, gm_g, gm_b, tm=n_s, gv_dtype=F32, name="in_proj_sample")
    tq = SLABS
    pad_rows = lambda t: jnp.pad(t.reshape(bs, ts, -1), ((0, 0), (0, tq - ts), (0, 0))).reshape(bs * tq, -1)
    ck2 = cache_k[l].reshape(bs * r, KV_WIDTH)
    cv2 = cache_v[l].reshape(bs * r, KV_WIDTH)
    att = _attention(sink_l, pad_rows(q), pad_rows(k), pad_rows(v), ck2, cv2, batch=bs, nb=1, tq=tq,
                     seqs=SAMPLE_SEQS_PER_STEP, chain=1, prev_blocks=1, first_block_has_no_prev=False, name="attn_sample")
    att = att.reshape(bs, tq, ATT_WIDTH)[:, :ts].reshape(n_s, ATT_WIDTH)
    gm = _gate_short(gu, gv, gm_w_s[l], gm_b_s[l], length=ts, name="gate_sample")
    new_ks = jnp.concatenate([cache_k[l], k.reshape(bs, ts, KV_HEADS, HEAD_DIM)], axis=1)[:, ts:][None]
    new_vs = jnp.concatenate([cache_v[l], v.reshape(bs, ts, KV_HEADS, HEAD_DIM)], axis=1)[:, ts:][None]
    new_gs = gv.reshape(bs, ts, GM_WIDTH)[None]
    h_s, pk_s, lt_s = _out_proj(att, gm, xs2, w_out_b, row_vec(ln1_g[l]), row_vec(ln1_b[l]), r_hi, r_lo,
                          name="out_proj_sample")

    eidx_t, w_t, rank_t, counts = _route(lt_p, lt_s, router_bias[l].astype(F32).reshape(N_EXPERTS, 1))
    a = n_total * TOP_K
    n_blocks = -(-(a + N_EXPERTS * (EXPERT_ROWS - 1)) // EXPERT_ROWS)
    pad_start, fill_start, fill_len, first_block, n_expert_blocks, n_used = _block_plan(counts.reshape(N_EXPERTS))
    dest_t = _dest(eidx_t, rank_t, pad_start.astype(F32).reshape(N_EXPERTS, 1))
    tiles = lambda a: a.reshape(-1, WORD_SLABS, LANES)
    shared = (w_gate_s[l].astype(BF16), w_up_s[l].astype(BF16), w_down_s[l].astype(BF16))
    x_sorted, y_part = _dispatch(fill_start, fill_len, n_used, dest_t, tiles(pk_p), tiles(pk_s), h_p, h_s, *shared,
                                 n_blocks=n_blocks)
    out_sorted = _experts(first_block, n_expert_blocks, n_used, x_sorted, w_gate_e[l], w_up_e[l], w_down_e[l])
    ln2 = (row_vec(ln2_g[l]), row_vec(ln2_b[l]))
    dest2, w2 = dest_t.T, w_t.T
    y_p = _combine(dest2, w2, y_part, *ln2, out_sorted, n=n_p, row_offset=0, name="combine_prompt")
    y_s = _combine(dest2, w2, y_part, *ln2, out_sorted, n=n_s, row_offset=n_p, name="combine_sample")
    return (y_p.reshape(bp, sp, D_MODEL), y_s.reshape(bs, ts, D_MODEL), new_kp, new_vp, new_ks, new_vs, new_gs)
```

```python
import functools

import jax
import jax.numpy as jnp
import numpy as np
from jax import lax
from jax.experimental import pallas as pl
from jax.experimental.pallas import tpu as pltpu
from jax.experimental.pallas import tpu_sc as plsc

D_MODEL = 1024
HEAD_DIM = 64
ATT_HEADS = 8
KV_HEADS = 2
Q_PER_KV = ATT_HEADS // KV_HEADS
GM_HEADS = 8
ATT_WIDTH = ATT_HEADS * HEAD_DIM
KV_WIDTH = KV_HEADS * HEAD_DIM
GM_WIDTH = GM_HEADS * HEAD_DIM
ROPE_WIDTH = ATT_WIDTH + KV_WIDTH
IN_WIDTH = ATT_WIDTH + 2 * KV_WIDTH + 2 * GM_WIDTH
WINDOW = 128
CHUNK = 128
PAST_LEN = 16384
ROPE_THETA = 10000.0
ATT_SCALE = HEAD_DIM ** -0.5
N_EXPERTS = 256
TOP_K = 8
N_GROUPS = 8
TOPK_GROUPS = 4
F_EXPERT = 256
ROUTED_SCALE = 2.5
LN_EPS = 1e-5
DEPTH = 1
ALPHA = (2.0 * DEPTH) ** 0.25

LANES = 128
SLABS = 8
PACKED = D_MODEL // 2
WORD_SLABS = PACKED // LANES
ROW_TILE = 512
EXPERT_ROWS = 256
EXPERT_RING = 8
COMBINE_ROWS = 256
COMBINE_GROUP = 4
DISPATCH_ROWS = 256
ROUTE_TILE = 512
SAMPLE_SEQS_PER_STEP = 16
PROMPT_BLOCKS_PER_STEP = 2
SC_CORES = 2
SC_SUBCORES = 16
SC_WINDOW = 128
DMA_UNROLL = 4
DMA_PRIORITIES = 2
VMEM_LIMIT = 56 * 1024 * 1024

F32 = jnp.float32
BF16 = jnp.bfloat16


def _params(n_axes):
    return pltpu.CompilerParams(dimension_semantics=("arbitrary",) * n_axes, vmem_limit_bytes=VMEM_LIMIT)


def _layer_norm(x, g, b):
    mu = jnp.mean(x, axis=-1, keepdims=True)
    xc = x - mu
    var = jnp.mean(xc * xc, axis=-1, keepdims=True)
    return xc * lax.rsqrt(var + LN_EPS) * g + b


def _pack_halves(x):
    xb = x.astype(BF16)
    lo = lax.bitcast_convert_type(xb[:, :PACKED].astype(F32), jnp.uint32)
    hi = lax.bitcast_convert_type(xb[:, PACKED:].astype(F32), jnp.uint32)
    return (lo >> 16) | hi


def _unpack_halves(w):
    lo = lax.bitcast_convert_type(w << 16, F32)
    hi = lax.bitcast_convert_type(w & jnp.uint32(0xFFFF0000), F32)
    return lo, hi


def _store_row_tiles(ref2d, x):
    m, slabs = x.shape[0], x.shape[1] // LANES
    for s in range(slabs):
        ref2d[pl.ds(s, m, stride=slabs), :] = x[:, s * LANES:(s + 1) * LANES]


def _load_row_tiles(ref2d, m, slabs):
    return jnp.concatenate([ref2d[pl.ds(s, m, stride=slabs), :] for s in range(slabs)], axis=1)


def _gelu(x):
    return 0.5 * x * (1.0 + lax.erf(x * np.float32(np.sqrt(0.5))))


def _in_proj_kernel(x_ref, w_ref, cos_ref, sa_ref, sb_ref, g_ref, b_ref, q_ref, k_ref, v_ref, gu_ref, gv_ref):
    x = x_ref[...].astype(BF16)
    zr = jnp.dot(x, w_ref[:, :ROPE_WIDTH], preferred_element_type=F32)
    pieces = []
    for c in range(ROPE_WIDTH // LANES):
        zc = zr[:, c * LANES:(c + 1) * LANES]
        tl = slice(0, LANES) if c < ATT_WIDTH // LANES else slice(LANES, 2 * LANES)
        pieces.append(zc * cos_ref[:, tl]
                      + pltpu.roll(zc, LANES - HEAD_DIM // 2, 1) * sa_ref[:, tl]
                      + pltpu.roll(zc, HEAD_DIM // 2, 1) * sb_ref[:, tl])
    for c in range(ATT_WIDTH // LANES):
        q_ref[:, c * LANES:(c + 1) * LANES] = pieces[c].astype(q_ref.dtype)
    k_ref[...] = pieces[ATT_WIDTH // LANES]
    v_ref[...] = jnp.dot(x, w_ref[:, ROPE_WIDTH:ROPE_WIDTH + KV_WIDTH], preferred_element_type=F32)
    g0 = ROPE_WIDTH + KV_WIDTH
    zu = jnp.dot(x, w_ref[:, g0:g0 + GM_WIDTH], preferred_element_type=F32)
    gu_ref[...] = _gelu(zu).astype(gu_ref.dtype)
    zv = jnp.dot(x, w_ref[:, g0 + GM_WIDTH:g0 + 2 * GM_WIDTH], preferred_element_type=F32)
    gv = _layer_norm(_gelu(zv), g_ref[...], b_ref[...])
    gv_ref[...] = gv.astype(gv_ref.dtype)


def _in_proj(x2, w_in_b, tabs, ln_g, ln_b, *, tm, gv_dtype, name):
    n = x2.shape[0]
    cos_t, sa_t, sb_t = tabs
    period = cos_t.shape[0] // tm
    row = lambda i: (i, 0)
    tab = lambda i: (i % period, 0)
    fixed = lambda i: (0, 0)
    return pl.pallas_call(
        _in_proj_kernel,
        grid=(n // tm,),
        in_specs=[
            pl.BlockSpec((tm, D_MODEL), row),
            pl.BlockSpec((D_MODEL, IN_WIDTH), fixed),
            pl.BlockSpec((tm, 2 * LANES), tab),
            pl.BlockSpec((tm, 2 * LANES), tab),
            pl.BlockSpec((tm, 2 * LANES), tab),
            pl.BlockSpec((1, GM_WIDTH), fixed),
            pl.BlockSpec((1, GM_WIDTH), fixed),
        ],
        out_specs=[
            pl.BlockSpec((tm, ATT_WIDTH), row),
            pl.BlockSpec((tm, KV_WIDTH), row),
            pl.BlockSpec((tm, KV_WIDTH), row),
            pl.BlockSpec((tm, GM_WIDTH), row),
            pl.BlockSpec((tm, GM_WIDTH), row),
        ],
        out_shape=[
            jax.ShapeDtypeStruct((n, ATT_WIDTH), BF16),
            jax.ShapeDtypeStruct((n, KV_WIDTH), F32),
            jax.ShapeDtypeStruct((n, KV_WIDTH), F32),
            jax.ShapeDtypeStruct((n, GM_WIDTH), BF16),
            jax.ShapeDtypeStruct((n, GM_WIDTH), gv_dtype),
        ],
        compiler_params=_params(1),
        name=name,
    )(x2, w_in_b, cos_t, sa_t, sb_t, ln_g, ln_b)


def _rope_tables(pos):
    half = HEAD_DIM // 2
    lane = jnp.arange(2 * LANES, dtype=jnp.int32)
    inv = ROPE_THETA ** (-(lane % half).astype(F32) * 2.0 / HEAD_DIM)
    ang = pos.astype(F32)[:, None] * inv[None, :]
    scale = jnp.where(lane < LANES, ATT_SCALE, 1.0).astype(F32)[None, :]
    first_half = ((lane % HEAD_DIM) < half)[None, :]
    cos, sin = jnp.cos(ang) * scale, jnp.sin(ang) * scale
    return cos, jnp.where(first_half, -sin, 0.0), jnp.where(first_half, 0.0, sin)


def _attn_kernel(sink_ref, q_ref, kc_ref, vc_ref, kp_ref, vp_ref, o_ref, *, tq, seqs, chain, stack,
                 first_block_has_no_prev):
    nk = WINDOW + tq
    rows = stack * tq
    qi = lax.broadcasted_iota(jnp.int32, (rows, nk), 0) & (tq - 1)
    ks = lax.broadcasted_iota(jnp.int32, (rows, nk), 1)
    band = (ks >= qi) & (ks <= qi + WINDOW)
    first_mask = band & ((pl.program_id(1) > 0) | (ks >= WINDOW)) if first_block_has_no_prev else band
    sinks = [jnp.concatenate([jnp.full((tq, 1), sink_ref[h0 + j], F32) for j in range(stack)], axis=0)
             for h0 in range(0, ATT_HEADS, stack)]
    for b in range(seqs * chain):
        qrows = slice(b * tq, (b + 1) * tq)
        if chain > 1 and b > 0:
            kprev, vprev = kc_ref[(b - 1) * tq:b * tq, :], vc_ref[(b - 1) * tq:b * tq, :]
        else:
            kprev, vprev = kp_ref[b * WINDOW:(b + 1) * WINDOW, :], vp_ref[b * WINDOW:(b + 1) * WINDOW, :]
        mask = band if (chain > 1 and b > 0) else first_mask
        q = q_ref[qrows, :]
        kk = jnp.concatenate([kprev, kc_ref[qrows, :]], axis=0).astype(BF16)
        vv = jnp.concatenate([vprev, vc_ref[qrows, :]], axis=0).astype(BF16)
        outs = []
        for i, h0 in enumerate(range(0, ATT_HEADS, stack)):
            g = h0 // Q_PER_KV
            kg = kk[:, g * HEAD_DIM:(g + 1) * HEAD_DIM]
            vg = vv[:, g * HEAD_DIM:(g + 1) * HEAD_DIM]
            qg = jnp.concatenate([q[:, (h0 + j) * HEAD_DIM:(h0 + j + 1) * HEAD_DIM] for j in range(stack)], axis=0)
            s = lax.dot_general(qg, kg, (((1,), (1,)), ((), ())), preferred_element_type=F32)
            s = jnp.where(mask, s, -jnp.inf)
            m = jnp.maximum(jnp.max(s, axis=-1, keepdims=True), sinks[i])
            p = jnp.exp(s - m)
            denom = jnp.sum(p, axis=-1, keepdims=True) + jnp.exp(sinks[i] - m)
            o = jnp.dot((p / denom).astype(BF16), vg, preferred_element_type=F32)
            outs.extend(o[j * tq:(j + 1) * tq, :] for j in range(stack))
        o_ref[qrows, :] = jnp.concatenate(outs, axis=1).astype(o_ref.dtype)


def _attention(sink, q2, k2, v2, kprev2, vprev2, *, batch, nb, tq, seqs, chain, prev_blocks, first_block_has_no_prev,
               name):
    assert tq & (tq - 1) == 0 and batch % seqs == 0 and (seqs == 1 or nb == prev_blocks == 1)
    assert (seqs == 1 or chain == 1) and nb % chain == 0 and (chain == 1 or tq == WINDOW)
    steps = nb // chain
    cur = lambda b, n, s: (b * steps + n, 0)
    prev = lambda b, n, s: (b * prev_blocks + jnp.maximum(n * chain - 1, 0), 0)
    stack = Q_PER_KV if Q_PER_KV * tq <= WINDOW else 1
    kern = functools.partial(_attn_kernel, tq=tq, seqs=seqs, chain=chain, stack=stack,
                             first_block_has_no_prev=first_block_has_no_prev)
    return pl.pallas_call(
        kern,
        grid_spec=pltpu.PrefetchScalarGridSpec(
            num_scalar_prefetch=1,
            grid=(batch // seqs, steps),
            in_specs=[
                pl.BlockSpec((seqs * chain * tq, ATT_WIDTH), cur),
                pl.BlockSpec((seqs * chain * tq, KV_WIDTH), cur),
                pl.BlockSpec((seqs * chain * tq, KV_WIDTH), cur),
                pl.BlockSpec((seqs * WINDOW, KV_WIDTH), prev),
                pl.BlockSpec((seqs * WINDOW, KV_WIDTH), prev),
            ],
            out_specs=pl.BlockSpec((seqs * chain * tq, ATT_WIDTH), cur),
        ),
        out_shape=jax.ShapeDtypeStruct(q2.shape, BF16),
        compiler_params=_params(2),
        name=name,
    )(sink, q2, k2, v2, kprev2, vprev2)


def _gate_kernel(gu_ref, gv_ref, w_ref, b_ref, o_ref, *, chunk, n_chunks):
    ri = lax.broadcasted_iota(jnp.int32, (chunk, chunk), 0)
    ci = lax.broadcasted_iota(jnp.int32, (chunk, chunk), 1)
    ws = [jnp.where(ci <= ri, w_ref[h], 0.0).astype(BF16) for h in range(GM_HEADS)]
    for c in range(n_chunks):
        rows = slice(c * chunk, (c + 1) * chunk)
        gv = gv_ref[rows, :].astype(BF16)
        sv = jnp.concatenate(
            [jnp.dot(ws[h], gv[:, h * HEAD_DIM:(h + 1) * HEAD_DIM], preferred_element_type=F32)
             for h in range(GM_HEADS)], axis=1)
        o_ref[rows, :] = (gu_ref[rows, :].astype(F32) * (sv + b_ref[...])).astype(o_ref.dtype)


def _gate(gu2, gv2, w_s, b_tab, *, chunk, n_chunks, name):
    n = gu2.shape[0]
    tm = chunk * n_chunks
    row = lambda i: (i, 0)
    kern = functools.partial(_gate_kernel, chunk=chunk, n_chunks=n_chunks)
    return pl.pallas_call(
        kern,
        grid=(n // tm,),
        in_specs=[
            pl.BlockSpec((tm, GM_WIDTH), row),
            pl.BlockSpec((tm, GM_WIDTH), row),
            pl.BlockSpec((GM_HEADS, chunk, chunk), lambda i: (0, 0, 0)),
            pl.BlockSpec((chunk, GM_WIDTH), lambda i: (0, 0)),
        ],
        out_specs=pl.BlockSpec((tm, GM_WIDTH), row),
        out_shape=jax.ShapeDtypeStruct((n, GM_WIDTH), BF16),
        compiler_params=_params(1),
        name=name,
    )(gu2, gv2, w_s, b_tab)


def _gate_short_kernel(gu_ref, gv_ref, coef_ref, b_ref, o_ref, *, length):
    n = gv_ref.shape[0]
    gv = gv_ref[...].astype(F32)
    tiles = lambda x: x.reshape(n // SLABS, SLABS, GM_WIDTH)
    sv = tiles(gv) * coef_ref[0][None] + b_ref[...][None]
    for d in range(1, length):
        sv = sv + tiles(pltpu.roll(gv, d, 0)) * coef_ref[d][None]
    o_ref[...] = (gu_ref[...].astype(F32) * sv.reshape(n, GM_WIDTH)).astype(o_ref.dtype)


def _gate_short(gu2, gv2, w_s, b_s, *, length, name):
    n = gu2.shape[0]
    assert SLABS % length == 0 and n % SLABS == 0
    pos = jnp.arange(length)
    lag = jnp.arange(length)
    src = pos[None, :] - lag[:, None]
    coef = jnp.where(src >= 0, w_s[:, pos[None, :], jnp.maximum(src, 0)], 0.0)
    coef = jnp.repeat(coef.transpose(1, 2, 0), HEAD_DIM, axis=2)
    coef = jnp.tile(coef, (1, SLABS // length, 1))
    bias = jnp.tile(jnp.repeat(b_s[:, :length].T, HEAD_DIM, axis=1), (SLABS // length, 1))
    whole = lambda shape: pl.BlockSpec(shape, lambda i: (0,) * len(shape))
    return pl.pallas_call(
        functools.partial(_gate_short_kernel, length=length),
        grid=(1,),
        in_specs=[whole((n, GM_WIDTH)), whole((n, GM_WIDTH)), whole((length, SLABS, GM_WIDTH)),
                  whole((SLABS, GM_WIDTH))],
        out_specs=whole((n, GM_WIDTH)),
        out_shape=jax.ShapeDtypeStruct((n, GM_WIDTH), BF16),
        compiler_params=_params(1),
        name=name,
    )(gu2, gv2, coef, bias)


def _out_proj_kernel(att_ref, gm_ref, x_ref, wo_ref, g_ref, b_ref, rh_ref, rl_ref, h_ref, hp_ref, lg_ref):
    mix = jnp.dot(att_ref[...], wo_ref[:ATT_WIDTH, :], preferred_element_type=F32)
    mix = mix + jnp.dot(gm_ref[...], wo_ref[ATT_WIDTH:, :], preferred_element_type=F32)
    h = _layer_norm(ALPHA * x_ref[...] + mix, g_ref[...], b_ref[...])
    h_ref[...] = h
    _store_row_tiles(hp_ref, _pack_halves(h))
    h_hi = h.astype(BF16)
    h_lo = (h - h_hi.astype(F32)).astype(BF16)
    nt = (((1,), (1,)), ((), ()))
    lg = lax.dot_general(rh_ref[...], h_hi, nt, preferred_element_type=F32)
    lg = lg + lax.dot_general(rh_ref[...], h_lo, nt, preferred_element_type=F32)
    lg = lg + lax.dot_general(rl_ref[...], h_hi, nt, preferred_element_type=F32)
    lg_ref[...] = lg


def _out_proj(att2, gm2, x2, w_out_b, ln_g, ln_b, r_hi, r_lo, *, name):
    n = x2.shape[0]
    tm = ROW_TILE
    row = lambda i: (i, 0)
    fixed = lambda i: (0, 0)
    return pl.pallas_call(
        _out_proj_kernel,
        grid=(n // tm,),
        in_specs=[
            pl.BlockSpec((tm, ATT_WIDTH), row),
            pl.BlockSpec((tm, GM_WIDTH), row),
            pl.BlockSpec((tm, D_MODEL), row),
            pl.BlockSpec((D_MODEL, D_MODEL), fixed),
            pl.BlockSpec((1, D_MODEL), fixed),
            pl.BlockSpec((1, D_MODEL), fixed),
            pl.BlockSpec((N_EXPERTS, D_MODEL), fixed),
            pl.BlockSpec((N_EXPERTS, D_MODEL), fixed),
        ],
        out_specs=[
            pl.BlockSpec((tm, D_MODEL), row),
            pl.BlockSpec((tm * WORD_SLABS, LANES), row),
            pl.BlockSpec((N_EXPERTS, tm), lambda i: (0, i)),
        ],
        out_shape=[
            jax.ShapeDtypeStruct((n, D_MODEL), F32),
            jax.ShapeDtypeStruct((n * WORD_SLABS, LANES), jnp.uint32),
            jax.ShapeDtypeStruct((N_EXPERTS, n), F32),
        ],
        compiler_params=_params(1),
        name=name,
    )(att2, gm2, x2, w_out_b, ln_g, ln_b, r_hi, r_lo)


def _experts_kernel(first_ref, nblk_ref, nu_ref, x_hbm, wg_ref, wu_ref, wd_ref, o_hbm, xbuf, obuf, wg_s, wu_s, wd_s,
                    in_sem, out_sem, fill_sem, *, n_blocks):
    e = pl.program_id(0)
    nb = nblk_ref[e]
    b0 = first_ref[e]
    n_used = nu_ref[0]
    ahead = EXPERT_RING // 2

    def rows_of(block):
        return pl.ds(pl.multiple_of(block * EXPERT_ROWS, EXPERT_ROWS), EXPERT_ROWS)

    def slot_of(block):
        return block & (EXPERT_RING - 1)

    def slab(s):
        return pl.ds(s * LANES, LANES)

    def in_copies(block):
        slot = slot_of(block)
        return [pltpu.make_async_copy(x_hbm.at[rows_of(block), s, :], xbuf.at[slot, :, slab(s)], in_sem.at[slot])
                for s in range(WORD_SLABS)]

    def out_copies(block, slot=None, sem=None):
        slot = slot_of(block) if slot is None else slot
        sem = out_sem.at[slot] if sem is None else sem
        return [pltpu.make_async_copy(obuf.at[slot, :, slab(s)], o_hbm.at[rows_of(block), s, :], sem)
                for s in range(WORD_SLABS)]

    def start(copies):
        for cp in copies:
            cp.start()

    def wait(copies):
        for cp in copies:
            cp.wait()

    @pl.when(e == 0)
    def _():
        for j in range(ahead):
            pl.when(j < n_used)(lambda j=j: start(in_copies(j)))

    def process(block, count):
        blocks = [block + j for j in range(count)]
        for blk in blocks:
            wait(in_copies(blk))
        x = [xbuf[slot_of(blk)] for blk in blocks]
        xl, xh = _unpack_halves(x[0] if count == 1 else jnp.concatenate(x, axis=0))
        xl, xh = xl.astype(BF16), xh.astype(BF16)
        for blk in blocks:
            pl.when(blk + ahead < n_used)(lambda blk=blk: start(in_copies(blk + ahead)))
            pl.when(blk >= ahead)(lambda blk=blk: wait(out_copies(blk - ahead)))
        a = (jnp.dot(xl, wg_s[:PACKED, :], preferred_element_type=F32)
             + jnp.dot(xh, wg_s[PACKED:, :], preferred_element_type=F32))
        u = (jnp.dot(xl, wu_s[:PACKED, :], preferred_element_type=F32)
             + jnp.dot(xh, wu_s[PACKED:, :], preferred_element_type=F32))
        hb = (a * jax.nn.sigmoid(a) * u).astype(BF16)
        o = _pack_halves(jnp.dot(hb, wd_s[...], preferred_element_type=F32))
        for j, blk in enumerate(blocks):
            obuf[slot_of(blk)] = o[j * EXPERT_ROWS:(j + 1) * EXPERT_ROWS, :]
            start(out_copies(blk))

    @pl.when(nb > 0)
    def _():
        wg_s[...] = wg_ref[0].astype(BF16)
        wu_s[...] = wu_ref[0].astype(BF16)
        wd_s[...] = wd_ref[0].astype(BF16)

        def pair(i, carry):
            process(b0 + 2 * i, 2)
            return carry

        lax.fori_loop(0, nb // 2, pair, 0)
        pl.when(nb % 2 == 1)(lambda: process(b0 + nb - 1, 1))

    @pl.when(e == N_EXPERTS - 1)
    def _():
        for j in range(ahead):
            pl.when(n_used - 1 - j >= 0)(lambda j=j: wait(out_copies(n_used - 1 - j)))
        obuf[0] = jnp.zeros((EXPERT_ROWS, PACKED), jnp.uint32)

        def on_unused_blocks(fn):
            def body(b, c):
                fn(out_copies(b, slot=0, sem=fill_sem))
                return c
            lax.fori_loop(n_used, n_blocks, body, 0)

        on_unused_blocks(start)
        on_unused_blocks(wait)


def _experts(first_block, n_expert_blocks, n_used, x_sorted, w_gate_e, w_up_e, w_down_e):
    rows = x_sorted.shape[0]
    wmap = lambda e, *_: (e, 0, 0)
    kern = functools.partial(_experts_kernel, n_blocks=rows // EXPERT_ROWS)
    return pl.pallas_call(
        kern,
        grid_spec=pltpu.PrefetchScalarGridSpec(
            num_scalar_prefetch=3,
            grid=(N_EXPERTS,),
            in_specs=[
                pl.BlockSpec(memory_space=pl.ANY),
                pl.BlockSpec((1, D_MODEL, F_EXPERT), wmap),
                pl.BlockSpec((1, D_MODEL, F_EXPERT), wmap),
                pl.BlockSpec((1, F_EXPERT, D_MODEL), wmap),
            ],
            out_specs=pl.BlockSpec(memory_space=pl.ANY),
            scratch_shapes=[
                pltpu.VMEM((EXPERT_RING, EXPERT_ROWS, PACKED), jnp.uint32),
                pltpu.VMEM((EXPERT_RING, EXPERT_ROWS, PACKED), jnp.uint32),
                pltpu.VMEM((D_MODEL, F_EXPERT), BF16),
                pltpu.VMEM((D_MODEL, F_EXPERT), BF16),
                pltpu.VMEM((F_EXPERT, D_MODEL), BF16),
                pltpu.SemaphoreType.DMA((EXPERT_RING,)),
                pltpu.SemaphoreType.DMA((EXPERT_RING,)),
                pltpu.SemaphoreType.DMA,
            ],
        ),
        out_shape=jax.ShapeDtypeStruct(x_sorted.shape, x_sorted.dtype),
        compiler_params=_params(1),
        name="experts",
    )(first_block, n_expert_blocks, n_used, x_sorted, w_gate_e, w_up_e, w_down_e)


def _combine_kernel(d_ref, w_ref, yp_ref, ln_g_ref, ln_b_ref, os_ref, y_ref, gbuf0, gbuf1, rlo, rhi, sem):
    s = pl.program_id(0)
    n_tiles = pl.num_programs(0) - 1
    t = y_ref.shape[0]
    gtiles = [g.reshape(t * TOP_K, WORD_SLABS, LANES) for g in (gbuf0, gbuf1)]
    lo_tiles = rlo.reshape(t, WORD_SLABS, LANES)
    hi_tiles = rhi.reshape(t, WORD_SLABS, LANES)

    def row_copy(row, buf, j, k):
        return pltpu.make_async_copy(os_ref.at[pl.ds(row, 1)], gtiles[buf].at[pl.ds(j * TOP_K + k, 1)], sem.at[buf])

    def drain(buf):
        def body(j, c):
            for k in range(TOP_K):
                row_copy(0, buf, j, k).wait()
            return c
        lax.fori_loop(0, t, body, 0, unroll=DMA_UNROLL)

    def weighted_sum(rows, j):
        lo, hi = _unpack_halves(rows)
        acc_lo = w_ref[j, 0] * lo[0:1]
        acc_hi = w_ref[j, 0] * hi[0:1]
        for k in range(1, TOP_K):
            acc_lo = acc_lo + w_ref[j, k] * lo[k:k + 1]
            acc_hi = acc_hi + w_ref[j, k] * hi[k:k + 1]
        lo_tiles[pl.ds(j, 1)] = acc_lo
        hi_tiles[pl.ds(j, 1)] = acc_hi

    def token_loop(new, old, start_new, sum_old):
        def body(g, c):
            tokens = [g * COMBINE_GROUP + u for u in range(COMBINE_GROUP)]
            rows = [gtiles[old][pl.ds(j * TOP_K, TOP_K)] for j in tokens] if sum_old else []
            if start_new:
                for j in tokens:
                    for k in range(TOP_K):
                        row_copy(d_ref[j, k], new, j, k).start(priority=k % DMA_PRIORITIES)
            for r, j in zip(rows, tokens):
                weighted_sum(r, j)
            return c
        lax.fori_loop(0, t // COMBINE_GROUP, body, 0)

    def step(new):
        old = 1 - new
        pl.when(s == 0)(lambda: token_loop(new, old, True, False))

        @pl.when(s >= 1)
        def _():
            drain(old)
            pl.when(s < n_tiles)(lambda: token_loop(new, old, True, True))
            pl.when(s == n_tiles)(lambda: token_loop(new, old, False, True))
            routed = jnp.concatenate([_load_row_tiles(rlo, t, WORD_SLABS), _load_row_tiles(rhi, t, WORD_SLABS)],
                                     axis=1)
            y_ref[...] = _layer_norm(yp_ref[...] + routed, ln_g_ref[...], ln_b_ref[...])

    pl.when(s % 2 == 0)(lambda: step(0))
    pl.when(s % 2 == 1)(lambda: step(1))


def _combine(dest2, w2, y_part, ln_g, ln_b, out_sorted, *, n, row_offset, name):
    tm = COMBINE_ROWS
    off = row_offset // tm
    n_tiles = n // tm
    fixed = lambda s: (0, 0)
    new_tile = lambda s: (jnp.minimum(s, n_tiles - 1) + off, 0)
    old_tile = lambda s: (jnp.maximum(s - 1, 0) + off, 0)
    return pl.pallas_call(
        _combine_kernel,
        grid=(n_tiles + 1,),
        in_specs=[
            pl.BlockSpec((tm, TOP_K), new_tile, memory_space=pltpu.SMEM),
            pl.BlockSpec((tm, TOP_K), old_tile, memory_space=pltpu.SMEM),
            pl.BlockSpec((tm, D_MODEL), old_tile),
            pl.BlockSpec((1, D_MODEL), fixed),
            pl.BlockSpec((1, D_MODEL), fixed),
            pl.BlockSpec(memory_space=pl.ANY),
        ],
        out_specs=pl.BlockSpec((tm, D_MODEL), lambda s: (jnp.maximum(s - 1, 0), 0)),
        out_shape=jax.ShapeDtypeStruct((n, D_MODEL), F32),
        scratch_shapes=[pltpu.VMEM((TOP_K * tm * WORD_SLABS, LANES), jnp.uint32),
                        pltpu.VMEM((TOP_K * tm * WORD_SLABS, LANES), jnp.uint32),
                        pltpu.VMEM((tm * WORD_SLABS, LANES), F32), pltpu.VMEM((tm * WORD_SLABS, LANES), F32),
                        pltpu.SemaphoreType.DMA((2,))],
        compiler_params=_params(1),
        name=name,
    )(dest2, w2, y_part, ln_g, ln_b, out_sorted)


def _route_kernel(lgp_ref, lgs_ref, bias_ref, eidx_ref, w_ref, rank_ref, cnt_ref, carry_ref, *, prompt_tiles):
    @pl.when(pl.program_id(0) == 0)
    def _():
        carry_ref[...] = jnp.zeros_like(carry_ref)

    t = lgp_ref.shape[1]
    gsz = N_EXPERTS // N_GROUPS
    neg = -jnp.inf
    s = jax.nn.sigmoid(jnp.where(pl.program_id(0) < prompt_tiles, lgp_ref[...], lgs_ref[...]))
    biased = s + bias_ref[...]
    io_g = lax.broadcasted_iota(jnp.int32, (gsz, t), 0)
    grp_rows = []
    for g in range(N_GROUPS):
        blk = biased[g * gsz:(g + 1) * gsz, :]
        m1 = jnp.max(blk, axis=0, keepdims=True)
        i1 = jnp.min(jnp.where(blk == m1, io_g, gsz), axis=0, keepdims=True)
        m2 = jnp.max(jnp.where(io_g == i1, neg, blk), axis=0, keepdims=True)
        grp_rows.append(m1 + m2)
    gs = jnp.concatenate(grp_rows, axis=0)
    io8 = lax.broadcasted_iota(jnp.int32, (N_GROUPS, t), 0)
    gsel = jnp.zeros((N_GROUPS, t), jnp.int32)
    for _ in range(TOPK_GROUPS):
        m = jnp.max(gs, axis=0, keepdims=True)
        gi = jnp.min(jnp.where(gs == m, io8, N_GROUPS), axis=0, keepdims=True)
        hit = io8 == gi
        gsel = jnp.where(hit, 1, gsel)
        gs = jnp.where(hit, neg, gs)
    masked = jnp.concatenate(
        [jnp.where(gsel[g:g + 1, :] > 0, biased[g * gsz:(g + 1) * gsz, :], neg) for g in range(N_GROUPS)], axis=0)

    eio = lax.broadcasted_iota(jnp.int32, (N_EXPERTS, t), 0)
    cur = masked
    idx_rows, w_rows = [], []
    for _ in range(TOP_K):
        m = jnp.max(cur, axis=0, keepdims=True)
        idx = jnp.min(jnp.where(cur == m, eio, N_EXPERTS), axis=0, keepdims=True)
        hit = eio == idx
        w_rows.append(jnp.sum(jnp.where(hit, s, 0.0), axis=0, keepdims=True))
        cur = jnp.where(hit, neg, cur)
        idx_rows.append(idx)
    sel = jnp.where(cur != masked, 1.0, 0.0)

    tri = jnp.where(lax.broadcasted_iota(jnp.int32, (t, t), 0) < lax.broadcasted_iota(jnp.int32, (t, t), 1), 1.0, 0.0)
    pref = jnp.dot(sel.astype(BF16), tri.astype(BF16), preferred_element_type=F32) + carry_ref[...]
    rank_rows = [jnp.sum(jnp.where(eio == idx_rows[k], pref, 0.0), axis=0, keepdims=True) for k in range(TOP_K)]
    carry_ref[...] += jnp.sum(sel, axis=1, keepdims=True)

    wk = jnp.concatenate(w_rows, axis=0)
    eidx_ref[...] = jnp.concatenate(idx_rows, axis=0)
    w_ref[...] = wk / jnp.sum(wk, axis=0, keepdims=True) * ROUTED_SCALE
    rank_ref[...] = jnp.concatenate(rank_rows, axis=0).astype(jnp.int32)
    cnt_ref[...] = carry_ref[...].astype(jnp.int32)


def _route(logits_p, logits_s, bias_col):
    t = ROUTE_TILE
    prompt_tiles = logits_p.shape[1] // t
    n = logits_p.shape[1] + logits_s.shape[1]
    col = lambda i: (0, i)
    fixed = lambda i: (0, 0)
    kern = functools.partial(_route_kernel, prompt_tiles=prompt_tiles)
    return pl.pallas_call(
        kern,
        grid=(n // t,),
        in_specs=[pl.BlockSpec((N_EXPERTS, t), lambda i: (0, jnp.minimum(i, prompt_tiles - 1))),
                  pl.BlockSpec((N_EXPERTS, t), lambda i: (0, jnp.maximum(i - prompt_tiles, 0))),
                  pl.BlockSpec((N_EXPERTS, 1), fixed)],
        out_specs=[
            pl.BlockSpec((TOP_K, t), col),
            pl.BlockSpec((TOP_K, t), col),
            pl.BlockSpec((TOP_K, t), col),
            pl.BlockSpec((N_EXPERTS, 1), fixed),
        ],
        out_shape=[
            jax.ShapeDtypeStruct((TOP_K, n), jnp.int32),
            jax.ShapeDtypeStruct((TOP_K, n), F32),
            jax.ShapeDtypeStruct((TOP_K, n), jnp.int32),
            jax.ShapeDtypeStruct((N_EXPERTS, 1), jnp.int32),
        ],
        scratch_shapes=[pltpu.VMEM((N_EXPERTS, 1), F32)],
        compiler_params=_params(1),
        name="route",
    )(logits_p, logits_s, bias_col)


def _dest_kernel(eidx_ref, rank_ref, start_ref, dest_ref):
    t = eidx_ref.shape[1]
    eio = lax.broadcasted_iota(jnp.int32, (N_EXPERTS, t), 0)
    start = start_ref[...]
    rows = [jnp.sum(jnp.where(eio == eidx_ref[k:k + 1, :], start, 0.0), axis=0, keepdims=True) for k in range(TOP_K)]
    dest_ref[...] = jnp.concatenate(rows, axis=0).astype(jnp.int32) + rank_ref[...]


def _dest(eidx_t, rank_t, pad_start_col):
    n = eidx_t.shape[1]
    t = ROW_TILE
    col = lambda i: (0, i)
    return pl.pallas_call(
        _dest_kernel,
        grid=(n // t,),
        in_specs=[pl.BlockSpec((TOP_K, t), col), pl.BlockSpec((TOP_K, t), col),
                  pl.BlockSpec((N_EXPERTS, 1), lambda i: (0, 0))],
        out_specs=pl.BlockSpec((TOP_K, t), col),
        out_shape=jax.ShapeDtypeStruct((TOP_K, n), jnp.int32),
        compiler_params=_params(1),
        name="dest",
    )(eidx_t, rank_t, pad_start_col)


def _dispatch_sc(dest_t, pk_p, pk_s, *, rows):
    win = SC_WINDOW
    p_win, s_win = pk_p.shape[0] // win, pk_s.shape[0] // win
    n_win = p_win + s_win
    workers = SC_CORES * SC_SUBCORES
    mesh = plsc.VectorSubcoreMesh(core_axis_name="c", subcore_axis_name="s")

    @functools.partial(
        pl.kernel, mesh=mesh, name="dispatch_sc",
        out_type=jax.ShapeDtypeStruct((rows, WORD_SLABS, LANES), jnp.uint32),
        scratch_types=[pltpu.VMEM((TOP_K, win), jnp.int32), pltpu.VMEM((win, WORD_SLABS, LANES), jnp.uint32)])
    def scatter(dest_hbm, pkp_hbm, pks_hbm, xs_hbm, idx_v, rows_v):
        wid = lax.axis_index("s") * SC_CORES + lax.axis_index("c")

        @pl.loop(0, (n_win - wid + workers - 1) // workers)
        def _(i):
            g = wid + i * workers

            @pl.when(g < p_win)
            def _():
                pltpu.sync_copy(pkp_hbm.at[pl.ds(pl.multiple_of(g * win, win), win)], rows_v)

            @pl.when(g >= p_win)
            def _():
                pltpu.sync_copy(pks_hbm.at[pl.ds(pl.multiple_of((g - p_win) * win, win), win)], rows_v)

            pltpu.sync_copy(dest_hbm.at[:, pl.ds(pl.multiple_of(g * win, win), win)], idx_v)
            for k in range(TOP_K):
                pltpu.sync_copy(rows_v, xs_hbm.at[idx_v.at[k]])

    return scatter(dest_t, pk_p, pk_s)


def _dispatch_kernel(fill_ref, len_ref, nu_ref, dest_ref, pp_ref, ps_ref, hp_ref, hs_ref, wg_ref, wu_ref, wd_ref,
                     *rest, prompt_tiles, n_blocks, copy_rows):
    xs_ref, yp_ref, zbuf, sem, fill_sem = rest[-5:]
    i = pl.program_id(0)

    @pl.when(i == 0)
    def _():
        zbuf[...] = jnp.zeros_like(zbuf)

        def fill_copy(row0, size):
            return pltpu.make_async_copy(zbuf.at[pl.ds(0, size)], xs_ref.at[pl.ds(row0, size)], fill_sem)

        def on_padding(fn):
            def body(e, c):
                base, length = fill_ref[e], len_ref[e]
                size = EXPERT_ROWS // 2
                while size >= 1:
                    piece = fill_copy(base + (length & ~(2 * size - 1)), size)
                    pl.when((length & size) != 0)(functools.partial(fn, piece))
                    size //= 2
                return c
            lax.fori_loop(0, N_EXPERTS, body, 0)

        def on_unused_blocks(fn):
            lax.fori_loop(nu_ref[0], n_blocks, lambda b, c: (fn(fill_copy(b * EXPERT_ROWS, EXPERT_ROWS)), c)[1], 0)

        on_padding(lambda cp: cp.start())
        on_unused_blocks(lambda cp: cp.start())
        on_padding(lambda cp: cp.wait())
        on_unused_blocks(lambda cp: cp.wait())

    def tile_step(src_ref, h_ref):
        t = src_ref.shape[0]

        def row_copy(j, k):
            return pltpu.make_async_copy(src_ref.at[pl.ds(j, 1)], xs_ref.at[pl.ds(dest_ref[k, j], 1)], sem)

        def issue(j, c):
            for k in range(TOP_K):
                row_copy(j, k).start(priority=k % DMA_PRIORITIES)
            return c

        def drain(j, c):
            for k in range(TOP_K):
                row_copy(j, k).wait()
            return c

        if copy_rows:
            lax.fori_loop(0, t, issue, 0, unroll=DMA_UNROLL)
        h = h_ref[...]
        hb = h.astype(BF16)
        a = jnp.dot(hb, wg_ref[...], preferred_element_type=F32)
        u = jnp.dot(hb, wu_ref[...], preferred_element_type=F32)
        shared = jnp.dot((a * jax.nn.sigmoid(a) * u).astype(BF16), wd_ref[...], preferred_element_type=F32)
        yp_ref[...] = ALPHA * h + shared
        if copy_rows:
            lax.fori_loop(0, t, drain, 0, unroll=DMA_UNROLL)

    @pl.when(i < prompt_tiles)
    def _():
        tile_step(pp_ref, hp_ref)

    @pl.when(i >= prompt_tiles)
    def _():
        tile_step(ps_ref, hs_ref)


def _dispatch(fill_start, fill_len, n_used, dest_t, pk_p, pk_s, h_p, h_s, wg_b, wu_b, wd_b, *, n_blocks, rows_in=None):
    t = DISPATCH_ROWS
    prompt_tiles = h_p.shape[0] // t
    sample_tiles = h_s.shape[0] // t
    tile = (t, WORD_SLABS, LANES)
    fixed = lambda i, *_: (0, 0)
    p_idx = lambda i: jnp.minimum(i, prompt_tiles - 1)
    s_idx = lambda i: jnp.maximum(i - prompt_tiles, 0)
    kern = functools.partial(_dispatch_kernel, prompt_tiles=prompt_tiles, n_blocks=n_blocks, copy_rows=rows_in is None)
    aliased = [] if rows_in is None else [rows_in]
    return pl.pallas_call(
        kern,
        grid_spec=pltpu.PrefetchScalarGridSpec(
            num_scalar_prefetch=3,
            grid=(prompt_tiles + sample_tiles,),
            in_specs=[
                pl.BlockSpec((TOP_K, t), lambda i, *_: (0, i), memory_space=pltpu.SMEM),
                pl.BlockSpec(tile, lambda i, *_: (p_idx(i), 0, 0)),
                pl.BlockSpec(tile, lambda i, *_: (s_idx(i), 0, 0)),
                pl.BlockSpec((t, D_MODEL), lambda i, *_: (p_idx(i), 0)),
                pl.BlockSpec((t, D_MODEL), lambda i, *_: (s_idx(i), 0)),
                pl.BlockSpec((D_MODEL, F_EXPERT), fixed),
                pl.BlockSpec((D_MODEL, F_EXPERT), fixed),
                pl.BlockSpec((F_EXPERT, D_MODEL), fixed),
            ] + [pl.BlockSpec(memory_space=pl.ANY)] * len(aliased),
            out_specs=[pl.BlockSpec(memory_space=pl.ANY), pl.BlockSpec((t, D_MODEL), lambda i, *_: (i, 0))],
            scratch_shapes=[pltpu.VMEM((EXPERT_ROWS, WORD_SLABS, LANES), jnp.uint32), pltpu.SemaphoreType.DMA,
                            pltpu.SemaphoreType.DMA],
        ),
        out_shape=[jax.ShapeDtypeStruct((n_blocks * EXPERT_ROWS, WORD_SLABS, LANES), jnp.uint32),
                   jax.ShapeDtypeStruct(((prompt_tiles + sample_tiles) * t, D_MODEL), F32)],
        input_output_aliases={} if rows_in is None else {11: 0},
        compiler_params=_params(1),
        name="dispatch",
    )(fill_start, fill_len, n_used, dest_t, pk_p, pk_s, h_p, h_s, wg_b, wu_b, wd_b, *aliased)


def _block_plan(counts):
    padded = (counts + EXPERT_ROWS - 1) // EXPERT_ROWS * EXPERT_ROWS
    pad_end = jnp.cumsum(padded).astype(jnp.int32)
    pad_start = pad_end - padded
    n_used = pad_end[-1] // EXPERT_ROWS
    fill_start = pad_start + counts
    fill_len = pad_end - fill_start
    first_block = pad_start // EXPERT_ROWS
    n_expert_blocks = (padded // EXPERT_ROWS).astype(jnp.int32)
    return pad_start, fill_start, fill_len, first_block, n_expert_blocks, n_used.reshape(1).astype(jnp.int32)


def kernel(x_prompt, x_sample, cache_k, cache_v, w_in, sink, gm_ln_g, gm_ln_b, gm_w_s, gm_b_s, w_out, ln1_g, ln1_b,
           router_w, router_bias, w_gate_e, w_up_e, w_down_e, w_gate_s, w_up_s, w_down_s, ln2_g, ln2_b):
    bp, sp = x_prompt.shape[:2]
    bs, ts = x_sample.shape[:2]
    r = cache_k.shape[2]
    assert r == WINDOW and sp % ROW_TILE == 0 and (bs * ts) % ROW_TILE == 0
    n_p, n_s = bp * sp, bs * ts
    n_total = n_p + n_s
    l = 0

    w_in_b = w_in[l].astype(BF16)
    w_out_b = w_out[l].astype(BF16)
    router_t = router_w[l].T
    r_hi = router_t.astype(BF16)
    r_lo = (router_t - r_hi.astype(F32)).astype(BF16)
    row_vec = lambda v: v.reshape(1, -1)
    gm_g, gm_b = row_vec(gm_ln_g[l]), row_vec(gm_ln_b[l])
    sink_l = sink[l].astype(F32)

    xp2 = x_prompt.reshape(n_p, D_MODEL)
    tabs_p = _rope_tables(jnp.arange(sp, dtype=jnp.int32))
    q, k, v, gu, gv = _in_proj(xp2, w_in_b, tabs_p, gm_g, gm_b, tm=ROW_TILE, gv_dtype=BF16, name="in_proj_prompt")
    nb = sp // WINDOW
    att = _attention(sink_l, q, k, v, k, v, batch=bp, nb=nb, tq=WINDOW, seqs=1, chain=PROMPT_BLOCKS_PER_STEP,
                     prev_blocks=nb,
                     first_block_has_no_prev=True, name="attn_prompt")
    b_tab_p = jnp.repeat(gm_b_s[l].T, HEAD_DIM, axis=1)
    gm = _gate(gu, gv, gm_w_s[l], b_tab_p, chunk=CHUNK, n_chunks=ROW_TILE // CHUNK, name="gate_prompt")
    last_rows = lambda t: t.reshape(bp, sp, KV_WIDTH)[:, sp - r:, :].reshape(1, bp, r, KV_HEADS, HEAD_DIM)
    new_kp, new_vp = last_rows(k), last_rows(v)
    h_p, pk_p, lt_p = _out_proj(att, gm, xp2, w_out_b, row_vec(ln1_g[l]), row_vec(ln1_b[l]), r_hi, r_lo,
                          name="out_proj_prompt")

    xs2 = x_sample.reshape(n_s, D_MODEL)
    pos_s = PAST_LEN + jnp.arange(ts, dtype=jnp.int32)
    tabs_s = tuple(jnp.tile(t, (bs, 1)) for t in _rope_tables(pos_s))
    q, k, v, gu, gv = _in_proj(xs2, w_in_b, tabs_s, gm_g, gm_b, tm=n_s, gv_dtype=F32, name="in_proj_sample")
    tq = SLABS
    pad_rows = lambda t: jnp.pad(t.reshape(bs, ts, -1), ((0, 0), (0, tq - ts), (0, 0))).reshape(bs * tq, -1)
    ck2 = cache_k[l].reshape(bs * r, KV_WIDTH)
    cv2 = cache_v[l].reshape(bs * r, KV_WIDTH)
    att = _attention(sink_l, pad_rows(q), pad_rows(k), pad_rows(v), ck2, cv2, batch=bs, nb=1, tq=tq,
                     seqs=SAMPLE_SEQS_PER_STEP, chain=1, prev_blocks=1, first_block_has_no_prev=False, name="attn_sample")
    att = att.reshape(bs, tq, ATT_WIDTH)[:, :ts].reshape(n_s, ATT_WIDTH)
    gm = _gate_short(gu, gv, gm_w_s[l], gm_b_s[l], length=ts, name="gate_sample")
    new_ks = jnp.concatenate([cache_k[l], k.reshape(bs, ts, KV_HEADS, HEAD_DIM)], axis=1)[:, ts:][None]
    new_vs = jnp.concatenate([cache_v[l], v.reshape(bs, ts, KV_HEADS, HEAD_DIM)], axis=1)[:, ts:][None]
    new_gs = gv.reshape(bs, ts, GM_WIDTH)[None]
    h_s, pk_s, lt_s = _out_proj(att, gm, xs2, w_out_b, row_vec(ln1_g[l]), row_vec(ln1_b[l]), r_hi, r_lo,
                          name="out_proj_sample")

    eidx_t, w_t, rank_t, counts = _route(lt_p, lt_s, router_bias[l].astype(F32).reshape(N_EXPERTS, 1))
    a = n_total * TOP_K
    n_blocks = -(-(a + N_EXPERTS * (EXPERT_ROWS - 1)) // EXPERT_ROWS)
    pad_start, fill_start, fill_len, first_block, n_expert_blocks, n_used = _block_plan(counts.reshape(N_EXPERTS))
    dest_t = _dest(eidx_t, rank_t, pad_start.astype(F32).reshape(N_EXPERTS, 1))
    tiles = lambda a: a.reshape(-1, WORD_SLABS, LANES)
    shared = (w_gate_s[l].astype(BF16), w_up_s[l].astype(BF16), w_down_s[l].astype(BF16))
    x_sorted = _dispatch_sc(dest_t, tiles(pk_p), tiles(pk_s), rows=n_blocks * EXPERT_ROWS)
    x_sorted, y_part = _dispatch(fill_start, fill_len, n_used, dest_t, tiles(pk_p), tiles(pk_s), h_p, h_s, *shared,
                                 n_blocks=n_blocks, rows_in=x_sorted)
    out_sorted = _experts(first_block, n_expert_blocks, n_used, x_sorted, w_gate_e[l], w_up_e[l], w_down_e[l])
    ln2 = (row_vec(ln2_g[l]), row_vec(ln2_b[l]))
    dest2, w2 = dest_t.T, w_t.T
    y_p = _combine(dest2, w2, y_part, *ln2, out_sorted, n=n_p, row_offset=0, name="combine_prompt")
    y_s = _combine(dest2, w2, y_part, *ln2, out_sorted, n=n_s, row_offset=n_p, name="combine_sample")
    return (y_p.reshape(bp, sp, D_MODEL), y_s.reshape(bs, ts, D_MODEL), new_kp, new_vp, new_ks, new_vs, new_gs)
```

```python
import functools

import jax
import jax.numpy as jnp
import numpy as np
from jax import lax
from jax.experimental import pallas as pl
from jax.experimental.pallas import tpu as pltpu
from jax.experimental.pallas import tpu_sc as plsc

D_MODEL = 1024
HEAD_DIM = 64
ATT_HEADS = 8
KV_HEADS = 2
Q_PER_KV = ATT_HEADS // KV_HEADS
GM_HEADS = 8
ATT_WIDTH = ATT_HEADS * HEAD_DIM
KV_WIDTH = KV_HEADS * HEAD_DIM
GM_WIDTH = GM_HEADS * HEAD_DIM
ROPE_WIDTH = ATT_WIDTH + KV_WIDTH
IN_WIDTH = ATT_WIDTH + 2 * KV_WIDTH + 2 * GM_WIDTH
WINDOW = 128
CHUNK = 128
PAST_LEN = 16384
ROPE_THETA = 10000.0
ATT_SCALE = HEAD_DIM ** -0.5
N_EXPERTS = 256
TOP_K = 8
N_GROUPS = 8
TOPK_GROUPS = 4
F_EXPERT = 256
ROUTED_SCALE = 2.5
LN_EPS = 1e-5
DEPTH = 1
ALPHA = (2.0 * DEPTH) ** 0.25

LANES = 128
SLABS = 8
PACKED = D_MODEL // 2
WORD_SLABS = PACKED // LANES
ROW_TILE = 512
EXPERT_ROWS = 256
EXPERT_RING = 8
COMBINE_ROWS = 256
COMBINE_GROUP = 4
DISPATCH_ROWS = 256
ROUTE_TILE = 512
SAMPLE_SEQS_PER_STEP = 16
PROMPT_BLOCKS_PER_STEP = 2
SC_CORES = 2
SC_SUBCORES = 16
SC_WINDOW = 128
DMA_UNROLL = 4
DMA_PRIORITIES = 2
VMEM_LIMIT = 56 * 1024 * 1024

F32 = jnp.float32
BF16 = jnp.bfloat16


def _params(n_axes):
    return pltpu.CompilerParams(dimension_semantics=("arbitrary",) * n_axes, vmem_limit_bytes=VMEM_LIMIT)


def _layer_norm(x, g, b):
    mu = jnp.mean(x, axis=-1, keepdims=True)
    xc = x - mu
    var = jnp.mean(xc * xc, axis=-1, keepdims=True)
    return xc * lax.rsqrt(var + LN_EPS) * g + b


def _pack_halves(x):
    xb = x.astype(BF16)
    lo = lax.bitcast_convert_type(xb[:, :PACKED].astype(F32), jnp.uint32)
    hi = lax.bitcast_convert_type(xb[:, PACKED:].astype(F32), jnp.uint32)
    return (lo >> 16) | hi


def _unpack_halves(w):
    lo = lax.bitcast_convert_type(w << 16, F32)
    hi = lax.bitcast_convert_type(w & jnp.uint32(0xFFFF0000), F32)
    return lo, hi


def _store_row_tiles(ref2d, x):
    m, slabs = x.shape[0], x.shape[1] // LANES
    for s in range(slabs):
        ref2d[pl.ds(s, m, stride=slabs), :] = x[:, s * LANES:(s + 1) * LANES]


def _load_row_tiles(ref2d, m, slabs):
    return jnp.concatenate([ref2d[pl.ds(s, m, stride=slabs), :] for s in range(slabs)], axis=1)


def _gelu(x):
    return 0.5 * x * (1.0 + lax.erf(x * np.float32(np.sqrt(0.5))))


def _in_proj_kernel(x_ref, w_ref, cos_ref, sa_ref, sb_ref, g_ref, b_ref, q_ref, k_ref, v_ref, gu_ref, gv_ref):
    x = x_ref[...].astype(BF16)
    zr = jnp.dot(x, w_ref[:, :ROPE_WIDTH], preferred_element_type=F32)
    pieces = []
    for c in range(ROPE_WIDTH // LANES):
        zc = zr[:, c * LANES:(c + 1) * LANES]
        tl = slice(0, LANES) if c < ATT_WIDTH // LANES else slice(LANES, 2 * LANES)
        pieces.append(zc * cos_ref[:, tl]
                      + pltpu.roll(zc, LANES - HEAD_DIM // 2, 1) * sa_ref[:, tl]
                      + pltpu.roll(zc, HEAD_DIM // 2, 1) * sb_ref[:, tl])
    for c in range(ATT_WIDTH // LANES):
        q_ref[:, c * LANES:(c + 1) * LANES] = pieces[c].astype(q_ref.dtype)
    k_ref[...] = pieces[ATT_WIDTH // LANES]
    v_ref[...] = jnp.dot(x, w_ref[:, ROPE_WIDTH:ROPE_WIDTH + KV_WIDTH], preferred_element_type=F32)
    g0 = ROPE_WIDTH + KV_WIDTH
    zu = jnp.dot(x, w_ref[:, g0:g0 + GM_WIDTH], preferred_element_type=F32)
    gu_ref[...] = _gelu(zu).astype(gu_ref.dtype)
    zv = jnp.dot(x, w_ref[:, g0 + GM_WIDTH:g0 + 2 * GM_WIDTH], preferred_element_type=F32)
    gv = _layer_norm(_gelu(zv), g_ref[...], b_ref[...])
    gv_ref[...] = gv.astype(gv_ref.dtype)


def _in_proj(x2, w_in_b, tabs, ln_g, ln_b, *, tm, gv_dtype, name):
    n = x2.shape[0]
    cos_t, sa_t, sb_t = tabs
    period = cos_t.shape[0] // tm
    row = lambda i: (i, 0)
    tab = lambda i: (i % period, 0)
    fixed = lambda i: (0, 0)
    return pl.pallas_call(
        _in_proj_kernel,
        grid=(n // tm,),
        in_specs=[
            pl.BlockSpec((tm, D_MODEL), row),
            pl.BlockSpec((D_MODEL, IN_WIDTH), fixed),
            pl.BlockSpec((tm, 2 * LANES), tab),
            pl.BlockSpec((tm, 2 * LANES), tab),
            pl.BlockSpec((tm, 2 * LANES), tab),
            pl.BlockSpec((1, GM_WIDTH), fixed),
            pl.BlockSpec((1, GM_WIDTH), fixed),
        ],
        out_specs=[
            pl.BlockSpec((tm, ATT_WIDTH), row),
            pl.BlockSpec((tm, KV_WIDTH), row),
            pl.BlockSpec((tm, KV_WIDTH), row),
            pl.BlockSpec((tm, GM_WIDTH), row),
            pl.BlockSpec((tm, GM_WIDTH), row),
        ],
        out_shape=[
            jax.ShapeDtypeStruct((n, ATT_WIDTH), BF16),
            jax.ShapeDtypeStruct((n, KV_WIDTH), F32),
            jax.ShapeDtypeStruct((n, KV_WIDTH), F32),
            jax.ShapeDtypeStruct((n, GM_WIDTH), BF16),
            jax.ShapeDtypeStruct((n, GM_WIDTH), gv_dtype),
        ],
        compiler_params=_params(1),
        name=name,
    )(x2, w_in_b, cos_t, sa_t, sb_t, ln_g, ln_b)


def _rope_tables(pos):
    half = HEAD_DIM // 2
    lane = jnp.arange(2 * LANES, dtype=jnp.int32)
    inv = ROPE_THETA ** (-(lane % half).astype(F32) * 2.0 / HEAD_DIM)
    ang = pos.astype(F32)[:, None] * inv[None, :]
    scale = jnp.where(lane < LANES, ATT_SCALE, 1.0).astype(F32)[None, :]
    first_half = ((lane % HEAD_DIM) < half)[None, :]
    cos, sin = jnp.cos(ang) * scale, jnp.sin(ang) * scale
    return cos, jnp.where(first_half, -sin, 0.0), jnp.where(first_half, 0.0, sin)


def _attn_kernel(sink_ref, q_ref, kc_ref, vc_ref, kp_ref, vp_ref, o_ref, *, tq, seqs, chain, stack,
                 first_block_has_no_prev):
    nk = WINDOW + tq
    rows = stack * tq
    qi = lax.broadcasted_iota(jnp.int32, (rows, nk), 0) & (tq - 1)
    ks = lax.broadcasted_iota(jnp.int32, (rows, nk), 1)
    band = (ks >= qi) & (ks <= qi + WINDOW)
    first_mask = band & ((pl.program_id(1) > 0) | (ks >= WINDOW)) if first_block_has_no_prev else band
    sinks = [jnp.concatenate([jnp.full((tq, 1), sink_ref[h0 + j], F32) for j in range(stack)], axis=0)
             for h0 in range(0, ATT_HEADS, stack)]
    for b in range(seqs * chain):
        qrows = slice(b * tq, (b + 1) * tq)
        if chain > 1 and b > 0:
            kprev, vprev = kc_ref[(b - 1) * tq:b * tq, :], vc_ref[(b - 1) * tq:b * tq, :]
        else:
            kprev, vprev = kp_ref[b * WINDOW:(b + 1) * WINDOW, :], vp_ref[b * WINDOW:(b + 1) * WINDOW, :]
        mask = band if (chain > 1 and b > 0) else first_mask
        q = q_ref[qrows, :]
        kk = jnp.concatenate([kprev, kc_ref[qrows, :]], axis=0).astype(BF16)
        vv = jnp.concatenate([vprev, vc_ref[qrows, :]], axis=0).astype(BF16)
        outs = []
        for i, h0 in enumerate(range(0, ATT_HEADS, stack)):
            g = h0 // Q_PER_KV
            kg = kk[:, g * HEAD_DIM:(g + 1) * HEAD_DIM]
            vg = vv[:, g * HEAD_DIM:(g + 1) * HEAD_DIM]
            qg = jnp.concatenate([q[:, (h0 + j) * HEAD_DIM:(h0 + j + 1) * HEAD_DIM] for j in range(stack)], axis=0)
            s = lax.dot_general(qg, kg, (((1,), (1,)), ((), ())), preferred_element_type=F32)
            s = jnp.where(mask, s, -jnp.inf)
            m = jnp.maximum(jnp.max(s, axis=-1, keepdims=True), sinks[i])
            p = jnp.exp(s - m)
            denom = jnp.sum(p, axis=-1, keepdims=True) + jnp.exp(sinks[i] - m)
            o = jnp.dot((p / denom).astype(BF16), vg, preferred_element_type=F32)
            outs.extend(o[j * tq:(j + 1) * tq, :] for j in range(stack))
        o_ref[qrows, :] = jnp.concatenate(outs, axis=1).astype(o_ref.dtype)


def _attention(sink, q2, k2, v2, kprev2, vprev2, *, batch, nb, tq, seqs, chain, prev_blocks, first_block_has_no_prev,
               name):
    assert tq & (tq - 1) == 0 and batch % seqs == 0 and (seqs == 1 or nb == prev_blocks == 1)
    assert (seqs == 1 or chain == 1) and nb % chain == 0 and (chain == 1 or tq == WINDOW)
    steps = nb // chain
    cur = lambda b, n, s: (b * steps + n, 0)
    prev = lambda b, n, s: (b * prev_blocks + jnp.maximum(n * chain - 1, 0), 0)
    stack = Q_PER_KV if Q_PER_KV * tq <= WINDOW else 1
    kern = functools.partial(_attn_kernel, tq=tq, seqs=seqs, chain=chain, stack=stack,
                             first_block_has_no_prev=first_block_has_no_prev)
    return pl.pallas_call(
        kern,
        grid_spec=pltpu.PrefetchScalarGridSpec(
            num_scalar_prefetch=1,
            grid=(batch // seqs, steps),
            in_specs=[
                pl.BlockSpec((seqs * chain * tq, ATT_WIDTH), cur),
                pl.BlockSpec((seqs * chain * tq, KV_WIDTH), cur),
                pl.BlockSpec((seqs * chain * tq, KV_WIDTH), cur),
                pl.BlockSpec((seqs * WINDOW, KV_WIDTH), prev),
                pl.BlockSpec((seqs * WINDOW, KV_WIDTH), prev),
            ],
            out_specs=pl.BlockSpec((seqs * chain * tq, ATT_WIDTH), cur),
        ),
        out_shape=jax.ShapeDtypeStruct(q2.shape, BF16),
        compiler_params=_params(2),
        name=name,
    )(sink, q2, k2, v2, kprev2, vprev2)


def _gate_kernel(gu_ref, gv_ref, w_ref, b_ref, o_ref, *, chunk, n_chunks):
    ri = lax.broadcasted_iota(jnp.int32, (chunk, chunk), 0)
    ci = lax.broadcasted_iota(jnp.int32, (chunk, chunk), 1)
    ws = [jnp.where(ci <= ri, w_ref[h], 0.0).astype(BF16) for h in range(GM_HEADS)]
    for c in range(n_chunks):
        rows = slice(c * chunk, (c + 1) * chunk)
        gv = gv_ref[rows, :].astype(BF16)
        sv = jnp.concatenate(
            [jnp.dot(ws[h], gv[:, h * HEAD_DIM:(h + 1) * HEAD_DIM], preferred_element_type=F32)
             for h in range(GM_HEADS)], axis=1)
        o_ref[rows, :] = (gu_ref[rows, :].astype(F32) * (sv + b_ref[...])).astype(o_ref.dtype)


def _gate(gu2, gv2, w_s, b_tab, *, chunk, n_chunks, name):
    n = gu2.shape[0]
    tm = chunk * n_chunks
    row = lambda i: (i, 0)
    kern = functools.partial(_gate_kernel, chunk=chunk, n_chunks=n_chunks)
    return pl.pallas_call(
        kern,
        grid=(n // tm,),
        in_specs=[
            pl.BlockSpec((tm, GM_WIDTH), row),
            pl.BlockSpec((tm, GM_WIDTH), row),
            pl.BlockSpec((GM_HEADS, chunk, chunk), lambda i: (0, 0, 0)),
            pl.BlockSpec((chunk, GM_WIDTH), lambda i: (0, 0)),
        ],
        out_specs=pl.BlockSpec((tm, GM_WIDTH), row),
        out_shape=jax.ShapeDtypeStruct((n, GM_WIDTH), BF16),
        compiler_params=_params(1),
        name=name,
    )(gu2, gv2, w_s, b_tab)


def _gate_short_kernel(gu_ref, gv_ref, coef_ref, b_ref, o_ref, *, length):
    n = gv_ref.shape[0]
    gv = gv_ref[...].astype(F32)
    tiles = lambda x: x.reshape(n // SLABS, SLABS, GM_WIDTH)
    sv = tiles(gv) * coef_ref[0][None] + b_ref[...][None]
    for d in range(1, length):
        sv = sv + tiles(pltpu.roll(gv, d, 0)) * coef_ref[d][None]
    o_ref[...] = (gu_ref[...].astype(F32) * sv.reshape(n, GM_WIDTH)).astype(o_ref.dtype)


def _gate_short(gu2, gv2, w_s, b_s, *, length, name):
    n = gu2.shape[0]
    assert SLABS % length == 0 and n % SLABS == 0
    pos = jnp.arange(length)
    lag = jnp.arange(length)
    src = pos[None, :] - lag[:, None]
    coef = jnp.where(src >= 0, w_s[:, pos[None, :], jnp.maximum(src, 0)], 0.0)
    coef = jnp.repeat(coef.transpose(1, 2, 0), HEAD_DIM, axis=2)
    coef = jnp.tile(coef, (1, SLABS // length, 1))
    bias = jnp.tile(jnp.repeat(b_s[:, :length].T, HEAD_DIM, axis=1), (SLABS // length, 1))
    whole = lambda shape: pl.BlockSpec(shape, lambda i: (0,) * len(shape))
    return pl.pallas_call(
        functools.partial(_gate_short_kernel, length=length),
        grid=(1,),
        in_specs=[whole((n, GM_WIDTH)), whole((n, GM_WIDTH)), whole((length, SLABS, GM_WIDTH)),
                  whole((SLABS, GM_WIDTH))],
        out_specs=whole((n, GM_WIDTH)),
        out_shape=jax.ShapeDtypeStruct((n, GM_WIDTH), BF16),
        compiler_params=_params(1),
        name=name,
    )(gu2, gv2, coef, bias)


def _out_proj_kernel(att_ref, gm_ref, x_ref, wo_ref, g_ref, b_ref, rh_ref, rl_ref, h_ref, hp_ref, lg_ref):
    mix = jnp.dot(att_ref[...], wo_ref[:ATT_WIDTH, :], preferred_element_type=F32)
    mix = mix + jnp.dot(gm_ref[...], wo_ref[ATT_WIDTH:, :], preferred_element_type=F32)
    h = _layer_norm(ALPHA * x_ref[...] + mix, g_ref[...], b_ref[...])
    h_ref[...] = h
    _store_row_tiles(hp_ref, _pack_halves(h))
    h_hi = h.astype(BF16)
    h_lo = (h - h_hi.astype(F32)).astype(BF16)
    nt = (((1,), (1,)), ((), ()))
    lg = lax.dot_general(rh_ref[...], h_hi, nt, preferred_element_type=F32)
    lg = lg + lax.dot_general(rh_ref[...], h_lo, nt, preferred_element_type=F32)
    lg = lg + lax.dot_general(rl_ref[...], h_hi, nt, preferred_element_type=F32)
    lg_ref[...] = lg


def _out_proj(att2, gm2, x2, w_out_b, ln_g, ln_b, r_hi, r_lo, *, name):
    n = x2.shape[0]
    tm = ROW_TILE
    row = lambda i: (i, 0)
    fixed = lambda i: (0, 0)
    return pl.pallas_call(
        _out_proj_kernel,
        grid=(n // tm,),
        in_specs=[
            pl.BlockSpec((tm, ATT_WIDTH), row),
            pl.BlockSpec((tm, GM_WIDTH), row),
            pl.BlockSpec((tm, D_MODEL), row),
            pl.BlockSpec((D_MODEL, D_MODEL), fixed),
            pl.BlockSpec((1, D_MODEL), fixed),
            pl.BlockSpec((1, D_MODEL), fixed),
            pl.BlockSpec((N_EXPERTS, D_MODEL), fixed),
            pl.BlockSpec((N_EXPERTS, D_MODEL), fixed),
        ],
        out_specs=[
            pl.BlockSpec((tm, D_MODEL), row),
            pl.BlockSpec((tm * WORD_SLABS, LANES), row),
            pl.BlockSpec((N_EXPERTS, tm), lambda i: (0, i)),
        ],
        out_shape=[
            jax.ShapeDtypeStruct((n, D_MODEL), F32),
            jax.ShapeDtypeStruct((n * WORD_SLABS, LANES), jnp.uint32),
            jax.ShapeDtypeStruct((N_EXPERTS, n), F32),
        ],
        compiler_params=_params(1),
        name=name,
    )(att2, gm2, x2, w_out_b, ln_g, ln_b, r_hi, r_lo)


def _experts_kernel(first_ref, nblk_ref, nu_ref, x_hbm, wg_ref, wu_ref, wd_ref, o_hbm, xbuf, obuf, wg_s, wu_s, wd_s,
                    in_sem, out_sem, fill_sem, *, n_blocks):
    e = pl.program_id(0)
    nb = nblk_ref[e]
    b0 = first_ref[e]
    n_used = nu_ref[0]
    ahead = EXPERT_RING // 2

    def rows_of(block):
        return pl.ds(pl.multiple_of(block * EXPERT_ROWS, EXPERT_ROWS), EXPERT_ROWS)

    def slot_of(block):
        return block & (EXPERT_RING - 1)

    def slab(s):
        return pl.ds(s * LANES, LANES)

    def in_copies(block):
        slot = slot_of(block)
        return [pltpu.make_async_copy(x_hbm.at[rows_of(block), s, :], xbuf.at[slot, :, slab(s)], in_sem.at[slot])
                for s in range(WORD_SLABS)]

    def out_copies(block, slot=None, sem=None):
        slot = slot_of(block) if slot is None else slot
        sem = out_sem.at[slot] if sem is None else sem
        return [pltpu.make_async_copy(obuf.at[slot, :, slab(s)], o_hbm.at[rows_of(block), s, :], sem)
                for s in range(WORD_SLABS)]

    def start(copies):
        for cp in copies:
            cp.start()

    def wait(copies):
        for cp in copies:
            cp.wait()

    @pl.when(e == 0)
    def _():
        for j in range(ahead):
            pl.when(j < n_used)(lambda j=j: start(in_copies(j)))

    def process(block, count):
        blocks = [block + j for j in range(count)]
        for blk in blocks:
            wait(in_copies(blk))
        x = [xbuf[slot_of(blk)] for blk in blocks]
        xl, xh = _unpack_halves(x[0] if count == 1 else jnp.concatenate(x, axis=0))
        xl, xh = xl.astype(BF16), xh.astype(BF16)
        for blk in blocks:
            pl.when(blk + ahead < n_used)(lambda blk=blk: start(in_copies(blk + ahead)))
            pl.when(blk >= ahead)(lambda blk=blk: wait(out_copies(blk - ahead)))
        a = (jnp.dot(xl, wg_s[:PACKED, :], preferred_element_type=F32)
             + jnp.dot(xh, wg_s[PACKED:, :], preferred_element_type=F32))
        u = (jnp.dot(xl, wu_s[:PACKED, :], preferred_element_type=F32)
             + jnp.dot(xh, wu_s[PACKED:, :], preferred_element_type=F32))
        hb = (a * jax.nn.sigmoid(a) * u).astype(BF16)
        o = _pack_halves(jnp.dot(hb, wd_s[...], preferred_element_type=F32))
        for j, blk in enumerate(blocks):
            obuf[slot_of(blk)] = o[j * EXPERT_ROWS:(j + 1) * EXPERT_ROWS, :]
            start(out_copies(blk))

    @pl.when(nb > 0)
    def _():
        wg_s[...] = wg_ref[0].astype(BF16)
        wu_s[...] = wu_ref[0].astype(BF16)
        wd_s[...] = wd_ref[0].astype(BF16)

        def pair(i, carry):
            process(b0 + 2 * i, 2)
            return carry

        lax.fori_loop(0, nb // 2, pair, 0)
        pl.when(nb % 2 == 1)(lambda: process(b0 + nb - 1, 1))

    @pl.when(e == N_EXPERTS - 1)
    def _():
        for j in range(ahead):
            pl.when(n_used - 1 - j >= 0)(lambda j=j: wait(out_copies(n_used - 1 - j)))
        obuf[0] = jnp.zeros((EXPERT_ROWS, PACKED), jnp.uint32)

        def on_unused_blocks(fn):
            def body(b, c):
                fn(out_copies(b, slot=0, sem=fill_sem))
                return c
            lax.fori_loop(n_used, n_blocks, body, 0)

        on_unused_blocks(start)
        on_unused_blocks(wait)


def _experts(first_block, n_expert_blocks, n_used, x_sorted, w_gate_e, w_up_e, w_down_e):
    rows = x_sorted.shape[0]
    wmap = lambda e, *_: (e, 0, 0)
    kern = functools.partial(_experts_kernel, n_blocks=rows // EXPERT_ROWS)
    return pl.pallas_call(
        kern,
        grid_spec=pltpu.PrefetchScalarGridSpec(
            num_scalar_prefetch=3,
            grid=(N_EXPERTS,),
            in_specs=[
                pl.BlockSpec(memory_space=pl.ANY),
                pl.BlockSpec((1, D_MODEL, F_EXPERT), wmap),
                pl.BlockSpec((1, D_MODEL, F_EXPERT), wmap),
                pl.BlockSpec((1, F_EXPERT, D_MODEL), wmap),
            ],
            out_specs=pl.BlockSpec(memory_space=pl.ANY),
            scratch_shapes=[
                pltpu.VMEM((EXPERT_RING, EXPERT_ROWS, PACKED), jnp.uint32),
                pltpu.VMEM((EXPERT_RING, EXPERT_ROWS, PACKED), jnp.uint32),
                pltpu.VMEM((D_MODEL, F_EXPERT), BF16),
                pltpu.VMEM((D_MODEL, F_EXPERT), BF16),
                pltpu.VMEM((F_EXPERT, D_MODEL), BF16),
                pltpu.SemaphoreType.DMA((EXPERT_RING,)),
                pltpu.SemaphoreType.DMA((EXPERT_RING,)),
                pltpu.SemaphoreType.DMA,
            ],
        ),
        out_shape=jax.ShapeDtypeStruct(x_sorted.shape, x_sorted.dtype),
        compiler_params=_params(1),
        name="experts",
    )(first_block, n_expert_blocks, n_used, x_sorted, w_gate_e, w_up_e, w_down_e)


def _gather_sc(dest_t, out_sorted):
    win = SC_WINDOW
    n = dest_t.shape[1]
    n_win = n // win
    workers = SC_CORES * SC_SUBCORES
    mesh = plsc.VectorSubcoreMesh(core_axis_name="c", subcore_axis_name="s")

    @functools.partial(
        pl.kernel, mesh=mesh, name="gather_sc",
        out_type=jax.ShapeDtypeStruct((TOP_K, n, WORD_SLABS, LANES), jnp.uint32),
        scratch_types=[pltpu.VMEM((TOP_K, win), jnp.int32), pltpu.VMEM((win, WORD_SLABS, LANES), jnp.uint32)])
    def gather(dest_hbm, os_hbm, g_hbm, idx_v, rows_v):
        wid = lax.axis_index("s") * SC_CORES + lax.axis_index("c")

        @pl.loop(0, (n_win - wid + workers - 1) // workers)
        def _(i):
            base = pl.multiple_of((wid + i * workers) * win, win)
            pltpu.sync_copy(dest_hbm.at[:, pl.ds(base, win)], idx_v)
            for k in range(TOP_K):
                pltpu.sync_copy(os_hbm.at[idx_v.at[k]], rows_v)
                pltpu.sync_copy(rows_v, g_hbm.at[k, pl.ds(base, win)])

    return gather(dest_t, out_sorted)


def _combine_kernel(g_ref, w_ref, yp_ref, ln_g_ref, ln_b_ref, y_ref):
    t = y_ref.shape[0]
    w = w_ref[...]
    lo_acc = jnp.zeros((t, PACKED), F32)
    hi_acc = jnp.zeros((t, PACKED), F32)
    for k in range(TOP_K):
        lo, hi = _unpack_halves(_load_row_tiles(g_ref.at[k], t, WORD_SLABS))
        lo_acc = lo_acc + w[:, k:k + 1] * lo
        hi_acc = hi_acc + w[:, k:k + 1] * hi
    routed = jnp.concatenate([lo_acc, hi_acc], axis=1)
    y_ref[...] = _layer_norm(yp_ref[...] + routed, ln_g_ref[...], ln_b_ref[...])


def _combine(g, w2, y_part, ln_g, ln_b, *, n, row_offset, name):
    tm = COMBINE_ROWS
    off = row_offset // tm
    fixed = lambda i: (0, 0)
    return pl.pallas_call(
        _combine_kernel,
        grid=(n // tm,),
        in_specs=[
            pl.BlockSpec((TOP_K, tm * WORD_SLABS, LANES), lambda i: (0, i + off, 0)),
            pl.BlockSpec((tm, TOP_K), lambda i: (i + off, 0)),
            pl.BlockSpec((tm, D_MODEL), lambda i: (i + off, 0)),
            pl.BlockSpec((1, D_MODEL), fixed),
            pl.BlockSpec((1, D_MODEL), fixed),
        ],
        out_specs=pl.BlockSpec((tm, D_MODEL), lambda i: (i, 0)),
        out_shape=jax.ShapeDtypeStruct((n, D_MODEL), F32),
        compiler_params=_params(1),
        name=name,
    )(g, w2, y_part, ln_g, ln_b)


def _route_kernel(lgp_ref, lgs_ref, bias_ref, eidx_ref, w_ref, rank_ref, cnt_ref, carry_ref, *, prompt_tiles):
    @pl.when(pl.program_id(0) == 0)
    def _():
        carry_ref[...] = jnp.zeros_like(carry_ref)

    t = lgp_ref.shape[1]
    gsz = N_EXPERTS // N_GROUPS
    neg = -jnp.inf
    s = jax.nn.sigmoid(jnp.where(pl.program_id(0) < prompt_tiles, lgp_ref[...], lgs_ref[...]))
    biased = s + bias_ref[...]
    io_g = lax.broadcasted_iota(jnp.int32, (gsz, t), 0)
    grp_rows = []
    for g in range(N_GROUPS):
        blk = biased[g * gsz:(g + 1) * gsz, :]
        m1 = jnp.max(blk, axis=0, keepdims=True)
        i1 = jnp.min(jnp.where(blk == m1, io_g, gsz), axis=0, keepdims=True)
        m2 = jnp.max(jnp.where(io_g == i1, neg, blk), axis=0, keepdims=True)
        grp_rows.append(m1 + m2)
    gs = jnp.concatenate(grp_rows, axis=0)
    io8 = lax.broadcasted_iota(jnp.int32, (N_GROUPS, t), 0)
    gsel = jnp.zeros((N_GROUPS, t), jnp.int32)
    for _ in range(TOPK_GROUPS):
        m = jnp.max(gs, axis=0, keepdims=True)
        gi = jnp.min(jnp.where(gs == m, io8, N_GROUPS), axis=0, keepdims=True)
        hit = io8 == gi
        gsel = jnp.where(hit, 1, gsel)
        gs = jnp.where(hit, neg, gs)
    masked = jnp.concatenate(
        [jnp.where(gsel[g:g + 1, :] > 0, biased[g * gsz:(g + 1) * gsz, :], neg) for g in range(N_GROUPS)], axis=0)

    eio = lax.broadcasted_iota(jnp.int32, (N_EXPERTS, t), 0)
    cur = masked
    idx_rows, w_rows = [], []
    for _ in range(TOP_K):
        m = jnp.max(cur, axis=0, keepdims=True)
        idx = jnp.min(jnp.where(cur == m, eio, N_EXPERTS), axis=0, keepdims=True)
        hit = eio == idx
        w_rows.append(jnp.sum(jnp.where(hit, s, 0.0), axis=0, keepdims=True))
        cur = jnp.where(hit, neg, cur)
        idx_rows.append(idx)
    sel = jnp.where(cur != masked, 1.0, 0.0)

    tri = jnp.where(lax.broadcasted_iota(jnp.int32, (t, t), 0) < lax.broadcasted_iota(jnp.int32, (t, t), 1), 1.0, 0.0)
    pref = jnp.dot(sel.astype(BF16), tri.astype(BF16), preferred_element_type=F32) + carry_ref[...]
    rank_rows = [jnp.sum(jnp.where(eio == idx_rows[k], pref, 0.0), axis=0, keepdims=True) for k in range(TOP_K)]
    carry_ref[...] += jnp.sum(sel, axis=1, keepdims=True)

    wk = jnp.concatenate(w_rows, axis=0)
    eidx_ref[...] = jnp.concatenate(idx_rows, axis=0)
    w_ref[...] = wk / jnp.sum(wk, axis=0, keepdims=True) * ROUTED_SCALE
    rank_ref[...] = jnp.concatenate(rank_rows, axis=0).astype(jnp.int32)
    cnt_ref[...] = carry_ref[...].astype(jnp.int32)


def _route(logits_p, logits_s, bias_col):
    t = ROUTE_TILE
    prompt_tiles = logits_p.shape[1] // t
    n = logits_p.shape[1] + logits_s.shape[1]
    col = lambda i: (0, i)
    fixed = lambda i: (0, 0)
    kern = functools.partial(_route_kernel, prompt_tiles=prompt_tiles)
    return pl.pallas_call(
        kern,
        grid=(n // t,),
        in_specs=[pl.BlockSpec((N_EXPERTS, t), lambda i: (0, jnp.minimum(i, prompt_tiles - 1))),
                  pl.BlockSpec((N_EXPERTS, t), lambda i: (0, jnp.maximum(i - prompt_tiles, 0))),
                  pl.BlockSpec((N_EXPERTS, 1), fixed)],
        out_specs=[
            pl.BlockSpec((TOP_K, t), col),
            pl.BlockSpec((TOP_K, t), col),
            pl.BlockSpec((TOP_K, t), col),
            pl.BlockSpec((N_EXPERTS, 1), fixed),
        ],
        out_shape=[
            jax.ShapeDtypeStruct((TOP_K, n), jnp.int32),
            jax.ShapeDtypeStruct((TOP_K, n), F32),
            jax.ShapeDtypeStruct((TOP_K, n), jnp.int32),
            jax.ShapeDtypeStruct((N_EXPERTS, 1), jnp.int32),
        ],
        scratch_shapes=[pltpu.VMEM((N_EXPERTS, 1), F32)],
        compiler_params=_params(1),
        name="route",
    )(logits_p, logits_s, bias_col)


def _dest_kernel(eidx_ref, rank_ref, start_ref, dest_ref):
    t = eidx_ref.shape[1]
    eio = lax.broadcasted_iota(jnp.int32, (N_EXPERTS, t), 0)
    start = start_ref[...]
    rows = [jnp.sum(jnp.where(eio == eidx_ref[k:k + 1, :], start, 0.0), axis=0, keepdims=True) for k in range(TOP_K)]
    dest_ref[...] = jnp.concatenate(rows, axis=0).astype(jnp.int32) + rank_ref[...]


def _dest(eidx_t, rank_t, pad_start_col):
    n = eidx_t.shape[1]
    t = ROW_TILE
    col = lambda i: (0, i)
    return pl.pallas_call(
        _dest_kernel,
        grid=(n // t,),
        in_specs=[pl.BlockSpec((TOP_K, t), col), pl.BlockSpec((TOP_K, t), col),
                  pl.BlockSpec((N_EXPERTS, 1), lambda i: (0, 0))],
        out_specs=pl.BlockSpec((TOP_K, t), col),
        out_shape=jax.ShapeDtypeStruct((TOP_K, n), jnp.int32),
        compiler_params=_params(1),
        name="dest",
    )(eidx_t, rank_t, pad_start_col)


def _dispatch_sc(dest_t, pk_p, pk_s, *, rows):
    win = SC_WINDOW
    p_win, s_win = pk_p.shape[0] // win, pk_s.shape[0] // win
    n_win = p_win + s_win
    workers = SC_CORES * SC_SUBCORES
    mesh = plsc.VectorSubcoreMesh(core_axis_name="c", subcore_axis_name="s")

    @functools.partial(
        pl.kernel, mesh=mesh, name="dispatch_sc",
        out_type=jax.ShapeDtypeStruct((rows, WORD_SLABS, LANES), jnp.uint32),
        scratch_types=[pltpu.VMEM((TOP_K, win), jnp.int32), pltpu.VMEM((win, WORD_SLABS, LANES), jnp.uint32)])
    def scatter(dest_hbm, pkp_hbm, pks_hbm, xs_hbm, idx_v, rows_v):
        wid = lax.axis_index("s") * SC_CORES + lax.axis_index("c")

        @pl.loop(0, (n_win - wid + workers - 1) // workers)
        def _(i):
            g = wid + i * workers

            @pl.when(g < p_win)
            def _():
                pltpu.sync_copy(pkp_hbm.at[pl.ds(pl.multiple_of(g * win, win), win)], rows_v)

            @pl.when(g >= p_win)
            def _():
                pltpu.sync_copy(pks_hbm.at[pl.ds(pl.multiple_of((g - p_win) * win, win), win)], rows_v)

            pltpu.sync_copy(dest_hbm.at[:, pl.ds(pl.multiple_of(g * win, win), win)], idx_v)
            for k in range(TOP_K):
                pltpu.sync_copy(rows_v, xs_hbm.at[idx_v.at[k]])

    return scatter(dest_t, pk_p, pk_s)


def _dispatch_kernel(fill_ref, len_ref, nu_ref, dest_ref, pp_ref, ps_ref, hp_ref, hs_ref, wg_ref, wu_ref, wd_ref,
                     *rest, prompt_tiles, n_blocks, copy_rows):
    xs_ref, yp_ref, zbuf, sem, fill_sem = rest[-5:]
    i = pl.program_id(0)

    @pl.when(i == 0)
    def _():
        zbuf[...] = jnp.zeros_like(zbuf)

        def fill_copy(row0, size):
            return pltpu.make_async_copy(zbuf.at[pl.ds(0, size)], xs_ref.at[pl.ds(row0, size)], fill_sem)

        def on_padding(fn):
            def body(e, c):
                base, length = fill_ref[e], len_ref[e]
                size = EXPERT_ROWS // 2
                while size >= 1:
                    piece = fill_copy(base + (length & ~(2 * size - 1)), size)
                    pl.when((length & size) != 0)(functools.partial(fn, piece))
                    size //= 2
                return c
            lax.fori_loop(0, N_EXPERTS, body, 0)

        def on_unused_blocks(fn):
            lax.fori_loop(nu_ref[0], n_blocks, lambda b, c: (fn(fill_copy(b * EXPERT_ROWS, EXPERT_ROWS)), c)[1], 0)

        on_padding(lambda cp: cp.start())
        on_unused_blocks(lambda cp: cp.start())
        on_padding(lambda cp: cp.wait())
        on_unused_blocks(lambda cp: cp.wait())

    def tile_step(src_ref, h_ref):
        t = src_ref.shape[0]

        def row_copy(j, k):
            return pltpu.make_async_copy(src_ref.at[pl.ds(j, 1)], xs_ref.at[pl.ds(dest_ref[k, j], 1)], sem)

        def issue(j, c):
            for k in range(TOP_K):
                row_copy(j, k).start(priority=k % DMA_PRIORITIES)
            return c

        def drain(j, c):
            for k in range(TOP_K):
                row_copy(j, k).wait()
            return c

        if copy_rows:
            lax.fori_loop(0, t, issue, 0, unroll=DMA_UNROLL)
        h = h_ref[...]
        hb = h.astype(BF16)
        a = jnp.dot(hb, wg_ref[...], preferred_element_type=F32)
        u = jnp.dot(hb, wu_ref[...], preferred_element_type=F32)
        shared = jnp.dot((a * jax.nn.sigmoid(a) * u).astype(BF16), wd_ref[...], preferred_element_type=F32)
        yp_ref[...] = ALPHA * h + shared
        if copy_rows:
            lax.fori_loop(0, t, drain, 0, unroll=DMA_UNROLL)

    @pl.when(i < prompt_tiles)
    def _():
        tile_step(pp_ref, hp_ref)

    @pl.when(i >= prompt_tiles)
    def _():
        tile_step(ps_ref, hs_ref)


def _dispatch(fill_start, fill_len, n_used, dest_t, pk_p, pk_s, h_p, h_s, wg_b, wu_b, wd_b, *, n_blocks, rows_in=None):
    t = DISPATCH_ROWS
    prompt_tiles = h_p.shape[0] // t
    sample_tiles = h_s.shape[0] // t
    tile = (t, WORD_SLABS, LANES)
    fixed = lambda i, *_: (0, 0)
    p_idx = lambda i: jnp.minimum(i, prompt_tiles - 1)
    s_idx = lambda i: jnp.maximum(i - prompt_tiles, 0)
    kern = functools.partial(_dispatch_kernel, prompt_tiles=prompt_tiles, n_blocks=n_blocks, copy_rows=rows_in is None)
    aliased = [] if rows_in is None else [rows_in]
    return pl.pallas_call(
        kern,
        grid_spec=pltpu.PrefetchScalarGridSpec(
            num_scalar_prefetch=3,
            grid=(prompt_tiles + sample_tiles,),
            in_specs=[
                pl.BlockSpec((TOP_K, t), lambda i, *_: (0, i), memory_space=pltpu.SMEM),
                pl.BlockSpec(tile, lambda i, *_: (p_idx(i), 0, 0)),
                pl.BlockSpec(tile, lambda i, *_: (s_idx(i), 0, 0)),
                pl.BlockSpec((t, D_MODEL), lambda i, *_: (p_idx(i), 0)),
                pl.BlockSpec((t, D_MODEL), lambda i, *_: (s_idx(i), 0)),
                pl.BlockSpec((D_MODEL, F_EXPERT), fixed),
                pl.BlockSpec((D_MODEL, F_EXPERT), fixed),
                pl.BlockSpec((F_EXPERT, D_MODEL), fixed),
            ] + [pl.BlockSpec(memory_space=pl.ANY)] * len(aliased),
            out_specs=[pl.BlockSpec(memory_space=pl.ANY), pl.BlockSpec((t, D_MODEL), lambda i, *_: (i, 0))],
            scratch_shapes=[pltpu.VMEM((EXPERT_ROWS, WORD_SLABS, LANES), jnp.uint32), pltpu.SemaphoreType.DMA,
                            pltpu.SemaphoreType.DMA],
        ),
        out_shape=[jax.ShapeDtypeStruct((n_blocks * EXPERT_ROWS, WORD_SLABS, LANES), jnp.uint32),
                   jax.ShapeDtypeStruct(((prompt_tiles + sample_tiles) * t, D_MODEL), F32)],
        input_output_aliases={} if rows_in is None else {11: 0},
        compiler_params=_params(1),
        name="dispatch",
    )(fill_start, fill_len, n_used, dest_t, pk_p, pk_s, h_p, h_s, wg_b, wu_b, wd_b, *aliased)


def _block_plan(counts):
    padded = (counts + EXPERT_ROWS - 1) // EXPERT_ROWS * EXPERT_ROWS
    pad_end = jnp.cumsum(padded).astype(jnp.int32)
    pad_start = pad_end - padded
    n_used = pad_end[-1] // EXPERT_ROWS
    fill_start = pad_start + counts
    fill_len = pad_end - fill_start
    first_block = pad_start // EXPERT_ROWS
    n_expert_blocks = (padded // EXPERT_ROWS).astype(jnp.int32)
    return pad_start, fill_start, fill_len, first_block, n_expert_blocks, n_used.reshape(1).astype(jnp.int32)


def kernel(x_prompt, x_sample, cache_k, cache_v, w_in, sink, gm_ln_g, gm_ln_b, gm_w_s, gm_b_s, w_out, ln1_g, ln1_b,
           router_w, router_bias, w_gate_e, w_up_e, w_down_e, w_gate_s, w_up_s, w_down_s, ln2_g, ln2_b):
    bp, sp = x_prompt.shape[:2]
    bs, ts = x_sample.shape[:2]
    r = cache_k.shape[2]
    assert r == WINDOW and sp % ROW_TILE == 0 and (bs * ts) % ROW_TILE == 0
    n_p, n_s = bp * sp, bs * ts
    n_total = n_p + n_s
    l = 0

    w_in_b = w_in[l].astype(BF16)
    w_out_b = w_out[l].astype(BF16)
    router_t = router_w[l].T
    r_hi = router_t.astype(BF16)
    r_lo = (router_t - r_hi.astype(F32)).astype(BF16)
    row_vec = lambda v: v.reshape(1, -1)
    gm_g, gm_b = row_vec(gm_ln_g[l]), row_vec(gm_ln_b[l])
    sink_l = sink[l].astype(F32)

    xp2 = x_prompt.reshape(n_p, D_MODEL)
    tabs_p = _rope_tables(jnp.arange(sp, dtype=jnp.int32))
    q, k, v, gu, gv = _in_proj(xp2, w_in_b, tabs_p, gm_g, gm_b, tm=ROW_TILE, gv_dtype=BF16, name="in_proj_prompt")
    nb = sp // WINDOW
    att = _attention(sink_l, q, k, v, k, v, batch=bp, nb=nb, tq=WINDOW, seqs=1, chain=PROMPT_BLOCKS_PER_STEP,
                     prev_blocks=nb,
                     first_block_has_no_prev=True, name="attn_prompt")
    b_tab_p = jnp.repeat(gm_b_s[l].T, HEAD_DIM, axis=1)
    gm = _gate(gu, gv, gm_w_s[l], b_tab_p, chunk=CHUNK, n_chunks=ROW_TILE // CHUNK, name="gate_prompt")
    last_rows = lambda t: t.reshape(bp, sp, KV_WIDTH)[:, sp - r:, :].reshape(1, bp, r, KV_HEADS, HEAD_DIM)
    new_kp, new_vp = last_rows(k), last_rows(v)
    h_p, pk_p, lt_p = _out_proj(att, gm, xp2, w_out_b, row_vec(ln1_g[l]), row_vec(ln1_b[l]), r_hi, r_lo,
                          name="out_proj_prompt")

    xs2 = x_sample.reshape(n_s, D_MODEL)
    pos_s = PAST_LEN + jnp.arange(ts, dtype=jnp.int32)
    tabs_s = tuple(jnp.tile(t, (bs, 1)) for t in _rope_tables(pos_s))
    q, k, v, gu, gv = _in_proj(xs2, w_in_b, tabs_s, gm_g, gm_b, tm=n_s, gv_dtype=F32, name="in_proj_sample")
    tq = SLABS
    pad_rows = lambda t: jnp.pad(t.reshape(bs, ts, -1), ((0, 0), (0, tq - ts), (0, 0))).reshape(bs * tq, -1)
    ck2 = cache_k[l].reshape(bs * r, KV_WIDTH)
    cv2 = cache_v[l].reshape(bs * r, KV_WIDTH)
    att = _attention(sink_l, pad_rows(q), pad_rows(k), pad_rows(v), ck2, cv2, batch=bs, nb=1, tq=tq,
                     seqs=SAMPLE_SEQS_PER_STEP, chain=1, prev_blocks=1, first_block_has_no_prev=False, name="attn_sample")
    att = att.reshape(bs, tq, ATT_WIDTH)[:, :ts].reshape(n_s, ATT_WIDTH)
    gm = _gate_short(gu, gv, gm_w_s[l], gm_b_s[l], length=ts, name="gate_sample")
    new_ks = jnp.concatenate([cache_k[l], k.reshape(bs, ts, KV_HEADS, HEAD_DIM)], axis=1)[:, ts:][None]
    new_vs = jnp.concatenate([cache_v[l], v.reshape(bs, ts, KV_HEADS, HEAD_DIM)], axis=1)[:, ts:][None]
    new_gs = gv.reshape(bs, ts, GM_WIDTH)[None]
    h_s, pk_s, lt_s = _out_proj(att, gm, xs2, w_out_b, row_vec(ln1_g[l]), row_vec(ln1_b[l]), r_hi, r_lo,
                          name="out_proj_sample")

    eidx_t, w_t, rank_t, counts = _route(lt_p, lt_s, router_bias[l].astype(F32).reshape(N_EXPERTS, 1))
    a = n_total * TOP_K
    n_blocks = -(-(a + N_EXPERTS * (EXPERT_ROWS - 1)) // EXPERT_ROWS)
    pad_start, fill_start, fill_len, first_block, n_expert_blocks, n_used = _block_plan(counts.reshape(N_EXPERTS))
    dest_t = _dest(eidx_t, rank_t, pad_start.astype(F32).reshape(N_EXPERTS, 1))
    tiles = lambda a: a.reshape(-1, WORD_SLABS, LANES)
    shared = (w_gate_s[l].astype(BF16), w_up_s[l].astype(BF16), w_down_s[l].astype(BF16))
    x_sorted = _dispatch_sc(dest_t, tiles(pk_p), tiles(pk_s), rows=n_blocks * EXPERT_ROWS)
    x_sorted, y_part = _dispatch(fill_start, fill_len, n_used, dest_t, tiles(pk_p), tiles(pk_s), h_p, h_s, *shared,
                                 n_blocks=n_blocks, rows_in=x_sorted)
    out_sorted = _experts(first_block, n_expert_blocks, n_used, x_sorted, w_gate_e[l], w_up_e[l], w_down_e[l])
    ln2 = (row_vec(ln2_g[l]), row_vec(ln2_b[l]))
    g = _gather_sc(dest_t, out_sorted).reshape(TOP_K, n_total * WORD_SLABS, LANES)
    w2 = w_t.T
    y_p = _combine(g, w2, y_part, *ln2, n=n_p, row_offset=0, name="combine_prompt")
    y_s = _combine(g, w2, y_part, *ln2, n=n_s, row_offset=n_p, name="combine_sample")
    return (y_p.reshape(bp, sp, D_MODEL), y_s.reshape(bs, ts, D_MODEL), new_kp, new_vp, new_ks, new_vs, new_gs)
```

```python
import functools

import jax
import jax.numpy as jnp
import numpy as np
from jax import lax
from jax.experimental import pallas as pl
from jax.experimental.pallas import tpu as pltpu
from jax.experimental.pallas import tpu_sc as plsc

D_MODEL = 1024
HEAD_DIM = 64
ATT_HEADS = 8
KV_HEADS = 2
Q_PER_KV = ATT_HEADS // KV_HEADS
GM_HEADS = 8
ATT_WIDTH = ATT_HEADS * HEAD_DIM
KV_WIDTH = KV_HEADS * HEAD_DIM
GM_WIDTH = GM_HEADS * HEAD_DIM
ROPE_WIDTH = ATT_WIDTH + KV_WIDTH
IN_WIDTH = ATT_WIDTH + 2 * KV_WIDTH + 2 * GM_WIDTH
WINDOW = 128
CHUNK = 128
PAST_LEN = 16384
ROPE_THETA = 10000.0
ATT_SCALE = HEAD_DIM ** -0.5
N_EXPERTS = 256
TOP_K = 8
N_GROUPS = 8
TOPK_GROUPS = 4
F_EXPERT = 256
ROUTED_SCALE = 2.5
LN_EPS = 1e-5
DEPTH = 1
ALPHA = (2.0 * DEPTH) ** 0.25

LANES = 128
SLABS = 8
PACKED = D_MODEL // 2
WORD_SLABS = PACKED // LANES
ROW_TILE = 512
EXPERT_ROWS = 256
EXPERT_RING = 8
COMBINE_ROWS = 256
COMBINE_GROUP = 4
DISPATCH_ROWS = 256
ROUTE_TILE = 512
SAMPLE_SEQS_PER_STEP = 16
PROMPT_BLOCKS_PER_STEP = 2
SC_CORES = 2
SC_SUBCORES = 16
SC_WINDOW = 128
DMA_UNROLL = 4
DMA_PRIORITIES = 2
VMEM_LIMIT = 56 * 1024 * 1024

F32 = jnp.float32
BF16 = jnp.bfloat16


def _params(n_axes):
    return pltpu.CompilerParams(dimension_semantics=("arbitrary",) * n_axes, vmem_limit_bytes=VMEM_LIMIT)


def _layer_norm(x, g, b):
    mu = jnp.mean(x, axis=-1, keepdims=True)
    xc = x - mu
    var = jnp.mean(xc * xc, axis=-1, keepdims=True)
    return xc * lax.rsqrt(var + LN_EPS) * g + b


def _pack_halves(x):
    xb = x.astype(BF16)
    lo = lax.bitcast_convert_type(xb[:, :PACKED].astype(F32), jnp.uint32)
    hi = lax.bitcast_convert_type(xb[:, PACKED:].astype(F32), jnp.uint32)
    return (lo >> 16) | hi


def _unpack_halves(w):
    lo = lax.bitcast_convert_type(w << 16, F32)
    hi = lax.bitcast_convert_type(w & jnp.uint32(0xFFFF0000), F32)
    return lo, hi


def _store_row_tiles(ref2d, x):
    m, slabs = x.shape[0], x.shape[1] // LANES
    for s in range(slabs):
        ref2d[pl.ds(s, m, stride=slabs), :] = x[:, s * LANES:(s + 1) * LANES]


def _load_row_tiles(ref2d, m, slabs):
    return jnp.concatenate([ref2d[pl.ds(s, m, stride=slabs), :] for s in range(slabs)], axis=1)


def _gelu(x):
    return 0.5 * x * (1.0 + lax.erf(x * np.float32(np.sqrt(0.5))))


def _in_proj_kernel(x_ref, w_ref, cos_ref, sa_ref, sb_ref, g_ref, b_ref, q_ref, k_ref, v_ref, gu_ref, gv_ref):
    x = x_ref[...].astype(BF16)
    zr = jnp.dot(x, w_ref[:, :ROPE_WIDTH], preferred_element_type=F32)
    pieces = []
    for c in range(ROPE_WIDTH // LANES):
        zc = zr[:, c * LANES:(c + 1) * LANES]
        tl = slice(0, LANES) if c < ATT_WIDTH // LANES else slice(LANES, 2 * LANES)
        pieces.append(zc * cos_ref[:, tl]
                      + pltpu.roll(zc, LANES - HEAD_DIM // 2, 1) * sa_ref[:, tl]
                      + pltpu.roll(zc, HEAD_DIM // 2, 1) * sb_ref[:, tl])
    for c in range(ATT_WIDTH // LANES):
        q_ref[:, c * LANES:(c + 1) * LANES] = pieces[c].astype(q_ref.dtype)
    k_ref[...] = pieces[ATT_WIDTH // LANES]
    v_ref[...] = jnp.dot(x, w_ref[:, ROPE_WIDTH:ROPE_WIDTH + KV_WIDTH], preferred_element_type=F32)
    g0 = ROPE_WIDTH + KV_WIDTH
    zu = jnp.dot(x, w_ref[:, g0:g0 + GM_WIDTH], preferred_element_type=F32)
    gu_ref[...] = _gelu(zu).astype(gu_ref.dtype)
    zv = jnp.dot(x, w_ref[:, g0 + GM_WIDTH:g0 + 2 * GM_WIDTH], preferred_element_type=F32)
    gv = _layer_norm(_gelu(zv), g_ref[...], b_ref[...])
    gv_ref[...] = gv.astype(gv_ref.dtype)


def _in_proj(x2, w_in_b, tabs, ln_g, ln_b, *, tm, gv_dtype, name):
    n = x2.shape[0]
    cos_t, sa_t, sb_t = tabs
    period = cos_t.shape[0] // tm
    row = lambda i: (i, 0)
    tab = lambda i: (i % period, 0)
    fixed = lambda i: (0, 0)
    return pl.pallas_call(
        _in_proj_kernel,
        grid=(n // tm,),
        in_specs=[
            pl.BlockSpec((tm, D_MODEL), row),
            pl.BlockSpec((D_MODEL, IN_WIDTH), fixed),
            pl.BlockSpec((tm, 2 * LANES), tab),
            pl.BlockSpec((tm, 2 * LANES), tab),
            pl.BlockSpec((tm, 2 * LANES), tab),
            pl.BlockSpec((1, GM_WIDTH), fixed),
            pl.BlockSpec((1, GM_WIDTH), fixed),
        ],
        out_specs=[
            pl.BlockSpec((tm, ATT_WIDTH), row),
            pl.BlockSpec((tm, KV_WIDTH), row),
            pl.BlockSpec((tm, KV_WIDTH), row),
            pl.BlockSpec((tm, GM_WIDTH), row),
            pl.BlockSpec((tm, GM_WIDTH), row),
        ],
        out_shape=[
            jax.ShapeDtypeStruct((n, ATT_WIDTH), BF16),
            jax.ShapeDtypeStruct((n, KV_WIDTH), F32),
            jax.ShapeDtypeStruct((n, KV_WIDTH), F32),
            jax.ShapeDtypeStruct((n, GM_WIDTH), BF16),
            jax.ShapeDtypeStruct((n, GM_WIDTH), gv_dtype),
        ],
        compiler_params=_params(1),
        name=name,
    )(x2, w_in_b, cos_t, sa_t, sb_t, ln_g, ln_b)


def _rope_tables(pos):
    half = HEAD_DIM // 2
    lane = jnp.arange(2 * LANES, dtype=jnp.int32)
    inv = ROPE_THETA ** (-(lane % half).astype(F32) * 2.0 / HEAD_DIM)
    ang = pos.astype(F32)[:, None] * inv[None, :]
    scale = jnp.where(lane < LANES, ATT_SCALE, 1.0).astype(F32)[None, :]
    first_half = ((lane % HEAD_DIM) < half)[None, :]
    cos, sin = jnp.cos(ang) * scale, jnp.sin(ang) * scale
    return cos, jnp.where(first_half, -sin, 0.0), jnp.where(first_half, 0.0, sin)


def _attn_kernel(sink_ref, q_ref, kc_ref, vc_ref, kp_ref, vp_ref, o_ref, *, tq, seqs, chain, stack,
                 first_block_has_no_prev):
    nk = WINDOW + tq
    rows = stack * tq
    qi = lax.broadcasted_iota(jnp.int32, (rows, nk), 0) & (tq - 1)
    ks = lax.broadcasted_iota(jnp.int32, (rows, nk), 1)
    band = (ks >= qi) & (ks <= qi + WINDOW)
    first_mask = band & ((pl.program_id(1) > 0) | (ks >= WINDOW)) if first_block_has_no_prev else band
    sinks = [jnp.concatenate([jnp.full((tq, 1), sink_ref[h0 + j], F32) for j in range(stack)], axis=0)
             for h0 in range(0, ATT_HEADS, stack)]
    for b in range(seqs * chain):
        qrows = slice(b * tq, (b + 1) * tq)
        if chain > 1 and b > 0:
            kprev, vprev = kc_ref[(b - 1) * tq:b * tq, :], vc_ref[(b - 1) * tq:b * tq, :]
        else:
            kprev, vprev = kp_ref[b * WINDOW:(b + 1) * WINDOW, :], vp_ref[b * WINDOW:(b + 1) * WINDOW, :]
        mask = band if (chain > 1 and b > 0) else first_mask
        q = q_ref[qrows, :]
        kk = jnp.concatenate([kprev, kc_ref[qrows, :]], axis=0).astype(BF16)
        vv = jnp.concatenate([vprev, vc_ref[qrows, :]], axis=0).astype(BF16)
        outs = []
        for i, h0 in enumerate(range(0, ATT_HEADS, stack)):
            g = h0 // Q_PER_KV
            kg = kk[:, g * HEAD_DIM:(g + 1) * HEAD_DIM]
            vg = vv[:, g * HEAD_DIM:(g + 1) * HEAD_DIM]
            qg = jnp.concatenate([q[:, (h0 + j) * HEAD_DIM:(h0 + j + 1) * HEAD_DIM] for j in range(stack)], axis=0)
            s = lax.dot_general(qg, kg, (((1,), (1,)), ((), ())), preferred_element_type=F32)
            s = jnp.where(mask, s, -jnp.inf)
            m = jnp.maximum(jnp.max(s, axis=-1, keepdims=True), sinks[i])
            p = jnp.exp(s - m)
            denom = jnp.sum(p, axis=-1, keepdims=True) + jnp.exp(sinks[i] - m)
            o = jnp.dot((p / denom).astype(BF16), vg, preferred_element_type=F32)
            outs.extend(o[j * tq:(j + 1) * tq, :] for j in range(stack))
        o_ref[qrows, :] = jnp.concatenate(outs, axis=1).astype(o_ref.dtype)


def _attention(sink, q2, k2, v2, kprev2, vprev2, *, batch, nb, tq, seqs, chain, prev_blocks, first_block_has_no_prev,
               name):
    assert tq & (tq - 1) == 0 and batch % seqs == 0 and (seqs == 1 or nb == prev_blocks == 1)
    assert (seqs == 1 or chain == 1) and nb % chain == 0 and (chain == 1 or tq == WINDOW)
    steps = nb // chain
    cur = lambda b, n, s: (b * steps + n, 0)
    prev = lambda b, n, s: (b * prev_blocks + jnp.maximum(n * chain - 1, 0), 0)
    stack = Q_PER_KV if Q_PER_KV * tq <= WINDOW else 1
    kern = functools.partial(_attn_kernel, tq=tq, seqs=seqs, chain=chain, stack=stack,
                             first_block_has_no_prev=first_block_has_no_prev)
    return pl.pallas_call(
        kern,
        grid_spec=pltpu.PrefetchScalarGridSpec(
            num_scalar_prefetch=1,
            grid=(batch // seqs, steps),
            in_specs=[
                pl.BlockSpec((seqs * chain * tq, ATT_WIDTH), cur),
                pl.BlockSpec((seqs * chain * tq, KV_WIDTH), cur),
                pl.BlockSpec((seqs * chain * tq, KV_WIDTH), cur),
                pl.BlockSpec((seqs * WINDOW, KV_WIDTH), prev),
                pl.BlockSpec((seqs * WINDOW, KV_WIDTH), prev),
            ],
            out_specs=pl.BlockSpec((seqs * chain * tq, ATT_WIDTH), cur),
        ),
        out_shape=jax.ShapeDtypeStruct(q2.shape, BF16),
        compiler_params=_params(2),
        name=name,
    )(sink, q2, k2, v2, kprev2, vprev2)


def _gate_kernel(gu_ref, gv_ref, w_ref, b_ref, o_ref, *, chunk, n_chunks):
    ri = lax.broadcasted_iota(jnp.int32, (chunk, chunk), 0)
    ci = lax.broadcasted_iota(jnp.int32, (chunk, chunk), 1)
    ws = [jnp.where(ci <= ri, w_ref[h], 0.0).astype(BF16) for h in range(GM_HEADS)]
    for c in range(n_chunks):
        rows = slice(c * chunk, (c + 1) * chunk)
        gv = gv_ref[rows, :].astype(BF16)
        sv = jnp.concatenate(
            [jnp.dot(ws[h], gv[:, h * HEAD_DIM:(h + 1) * HEAD_DIM], preferred_element_type=F32)
             for h in range(GM_HEADS)], axis=1)
        o_ref[rows, :] = (gu_ref[rows, :].astype(F32) * (sv + b_ref[...])).astype(o_ref.dtype)


def _gate(gu2, gv2, w_s, b_tab, *, chunk, n_chunks, name):
    n = gu2.shape[0]
    tm = chunk * n_chunks
    row = lambda i: (i, 0)
    kern = functools.partial(_gate_kernel, chunk=chunk, n_chunks=n_chunks)
    return pl.pallas_call(
        kern,
        grid=(n // tm,),
        in_specs=[
            pl.BlockSpec((tm, GM_WIDTH), row),
            pl.BlockSpec((tm, GM_WIDTH), row),
            pl.BlockSpec((GM_HEADS, chunk, chunk), lambda i: (0, 0, 0)),
            pl.BlockSpec((chunk, GM_WIDTH), lambda i: (0, 0)),
        ],
        out_specs=pl.BlockSpec((tm, GM_WIDTH), row),
        out_shape=jax.ShapeDtypeStruct((n, GM_WIDTH), BF16),
        compiler_params=_params(1),
        name=name,
    )(gu2, gv2, w_s, b_tab)


def _gate_short_kernel(gu_ref, gv_ref, coef_ref, b_ref, o_ref, *, length):
    n = gv_ref.shape[0]
    gv = gv_ref[...].astype(F32)
    tiles = lambda x: x.reshape(n // SLABS, SLABS, GM_WIDTH)
    sv = tiles(gv) * coef_ref[0][None] + b_ref[...][None]
    for d in range(1, length):
        sv = sv + tiles(pltpu.roll(gv, d, 0)) * coef_ref[d][None]
    o_ref[...] = (gu_ref[...].astype(F32) * sv.reshape(n, GM_WIDTH)).astype(o_ref.dtype)


def _gate_short(gu2, gv2, w_s, b_s, *, length, name):
    n = gu2.shape[0]
    assert SLABS % length == 0 and n % SLABS == 0
    pos = jnp.arange(length)
    lag = jnp.arange(length)
    src = pos[None, :] - lag[:, None]
    coef = jnp.where(src >= 0, w_s[:, pos[None, :], jnp.maximum(src, 0)], 0.0)
    coef = jnp.repeat(coef.transpose(1, 2, 0), HEAD_DIM, axis=2)
    coef = jnp.tile(coef, (1, SLABS // length, 1))
    bias = jnp.tile(jnp.repeat(b_s[:, :length].T, HEAD_DIM, axis=1), (SLABS // length, 1))
    whole = lambda shape: pl.BlockSpec(shape, lambda i: (0,) * len(shape))
    return pl.pallas_call(
        functools.partial(_gate_short_kernel, length=length),
        grid=(1,),
        in_specs=[whole((n, GM_WIDTH)), whole((n, GM_WIDTH)), whole((length, SLABS, GM_WIDTH)),
                  whole((SLABS, GM_WIDTH))],
        out_specs=whole((n, GM_WIDTH)),
        out_shape=jax.ShapeDtypeStruct((n, GM_WIDTH), BF16),
        compiler_params=_params(1),
        name=name,
    )(gu2, gv2, coef, bias)


def _out_proj_kernel(att_ref, gm_ref, x_ref, wo_ref, g_ref, b_ref, rh_ref, rl_ref, h_ref, hp_ref, lg_ref):
    mix = jnp.dot(att_ref[...], wo_ref[:ATT_WIDTH, :], preferred_element_type=F32)
    mix = mix + jnp.dot(gm_ref[...], wo_ref[ATT_WIDTH:, :], preferred_element_type=F32)
    h = _layer_norm(ALPHA * x_ref[...] + mix, g_ref[...], b_ref[...])
    h_ref[...] = h
    _store_row_tiles(hp_ref, _pack_halves(h))
    h_hi = h.astype(BF16)
    h_lo = (h - h_hi.astype(F32)).astype(BF16)
    nt = (((1,), (1,)), ((), ()))
    lg = lax.dot_general(rh_ref[...], h_hi, nt, preferred_element_type=F32)
    lg = lg + lax.dot_general(rh_ref[...], h_lo, nt, preferred_element_type=F32)
    lg = lg + lax.dot_general(rl_ref[...], h_hi, nt, preferred_element_type=F32)
    lg_ref[...] = lg


def _out_proj(att2, gm2, x2, w_out_b, ln_g, ln_b, r_hi, r_lo, *, name):
    n = x2.shape[0]
    tm = ROW_TILE
    row = lambda i: (i, 0)
    fixed = lambda i: (0, 0)
    return pl.pallas_call(
        _out_proj_kernel,
        grid=(n // tm,),
        in_specs=[
            pl.BlockSpec((tm, ATT_WIDTH), row),
            pl.BlockSpec((tm, GM_WIDTH), row),
            pl.BlockSpec((tm, D_MODEL), row),
            pl.BlockSpec((D_MODEL, D_MODEL), fixed),
            pl.BlockSpec((1, D_MODEL), fixed),
            pl.BlockSpec((1, D_MODEL), fixed),
            pl.BlockSpec((N_EXPERTS, D_MODEL), fixed),
            pl.BlockSpec((N_EXPERTS, D_MODEL), fixed),
        ],
        out_specs=[
            pl.BlockSpec((tm, D_MODEL), row),
            pl.BlockSpec((tm * WORD_SLABS, LANES), row),
            pl.BlockSpec((N_EXPERTS, tm), lambda i: (0, i)),
        ],
        out_shape=[
            jax.ShapeDtypeStruct((n, D_MODEL), F32),
            jax.ShapeDtypeStruct((n * WORD_SLABS, LANES), jnp.uint32),
            jax.ShapeDtypeStruct((N_EXPERTS, n), F32),
        ],
        compiler_params=_params(1),
        name=name,
    )(att2, gm2, x2, w_out_b, ln_g, ln_b, r_hi, r_lo)


def _experts_kernel(first_ref, nblk_ref, nu_ref, x_hbm, wg_ref, wu_ref, wd_ref, o_hbm, xbuf, obuf, wg_s, wu_s, wd_s,
                    in_sem, out_sem, fill_sem, *, n_blocks):
    e = pl.program_id(0)
    nb = nblk_ref[e]
    b0 = first_ref[e]
    n_used = nu_ref[0]
    ahead = EXPERT_RING // 2

    def rows_of(block):
        return pl.ds(pl.multiple_of(block * EXPERT_ROWS, EXPERT_ROWS), EXPERT_ROWS)

    def slot_of(block):
        return block & (EXPERT_RING - 1)

    def slab(s):
        return pl.ds(s * LANES, LANES)

    def in_copies(block):
        slot = slot_of(block)
        return [pltpu.make_async_copy(x_hbm.at[rows_of(block), s, :], xbuf.at[slot, :, slab(s)], in_sem.at[slot])
                for s in range(WORD_SLABS)]

    def out_copies(block, slot=None, sem=None):
        slot = slot_of(block) if slot is None else slot
        sem = out_sem.at[slot] if sem is None else sem
        return [pltpu.make_async_copy(obuf.at[slot, :, slab(s)], o_hbm.at[rows_of(block), s, :], sem)
                for s in range(WORD_SLABS)]

    def start(copies):
        for cp in copies:
            cp.start()

    def wait(copies):
        for cp in copies:
            cp.wait()

    @pl.when(e == 0)
    def _():
        for j in range(ahead):
            pl.when(j < n_used)(lambda j=j: start(in_copies(j)))

    def process(block, count):
        blocks = [block + j for j in range(count)]
        for blk in blocks:
            wait(in_copies(blk))
        x = [xbuf[slot_of(blk)] for blk in blocks]
        xl, xh = _unpack_halves(x[0] if count == 1 else jnp.concatenate(x, axis=0))
        xl, xh = xl.astype(BF16), xh.astype(BF16)
        for blk in blocks:
            pl.when(blk + ahead < n_used)(lambda blk=blk: start(in_copies(blk + ahead)))
            pl.when(blk >= ahead)(lambda blk=blk: wait(out_copies(blk - ahead)))
        a = (jnp.dot(xl, wg_s[:PACKED, :], preferred_element_type=F32)
             + jnp.dot(xh, wg_s[PACKED:, :], preferred_element_type=F32))
        u = (jnp.dot(xl, wu_s[:PACKED, :], preferred_element_type=F32)
             + jnp.dot(xh, wu_s[PACKED:, :], preferred_element_type=F32))
        hb = (a * jax.nn.sigmoid(a) * u).astype(BF16)
        o = _pack_halves(jnp.dot(hb, wd_s[...], preferred_element_type=F32))
        for j, blk in enumerate(blocks):
            obuf[slot_of(blk)] = o[j * EXPERT_ROWS:(j + 1) * EXPERT_ROWS, :]
            start(out_copies(blk))

    @pl.when(nb > 0)
    def _():
        wg_s[...] = wg_ref[0].astype(BF16)
        wu_s[...] = wu_ref[0].astype(BF16)
        wd_s[...] = wd_ref[0].astype(BF16)

        def pair(i, carry):
            process(b0 + 2 * i, 2)
            return carry

        lax.fori_loop(0, nb // 2, pair, 0)
        pl.when(nb % 2 == 1)(lambda: process(b0 + nb - 1, 1))

    @pl.when(e == N_EXPERTS - 1)
    def _():
        for j in range(ahead):
            pl.when(n_used - 1 - j >= 0)(lambda j=j: wait(out_copies(n_used - 1 - j)))
        obuf[0] = jnp.zeros((EXPERT_ROWS, PACKED), jnp.uint32)

        def on_unused_blocks(fn):
            def body(b, c):
                fn(out_copies(b, slot=0, sem=fill_sem))
                return c
            lax.fori_loop(n_used, n_blocks, body, 0)

        on_unused_blocks(start)
        on_unused_blocks(wait)


def _experts(first_block, n_expert_blocks, n_used, x_sorted, w_gate_e, w_up_e, w_down_e):
    rows = x_sorted.shape[0]
    wmap = lambda e, *_: (e, 0, 0)
    kern = functools.partial(_experts_kernel, n_blocks=rows // EXPERT_ROWS)
    return pl.pallas_call(
        kern,
        grid_spec=pltpu.PrefetchScalarGridSpec(
            num_scalar_prefetch=3,
            grid=(N_EXPERTS,),
            in_specs=[
                pl.BlockSpec(memory_space=pl.ANY),
                pl.BlockSpec((1, D_MODEL, F_EXPERT), wmap),
                pl.BlockSpec((1, D_MODEL, F_EXPERT), wmap),
                pl.BlockSpec((1, F_EXPERT, D_MODEL), wmap),
            ],
            out_specs=pl.BlockSpec(memory_space=pl.ANY),
            scratch_shapes=[
                pltpu.VMEM((EXPERT_RING, EXPERT_ROWS, PACKED), jnp.uint32),
                pltpu.VMEM((EXPERT_RING, EXPERT_ROWS, PACKED), jnp.uint32),
                pltpu.VMEM((D_MODEL, F_EXPERT), BF16),
                pltpu.VMEM((D_MODEL, F_EXPERT), BF16),
                pltpu.VMEM((F_EXPERT, D_MODEL), BF16),
                pltpu.SemaphoreType.DMA((EXPERT_RING,)),
                pltpu.SemaphoreType.DMA((EXPERT_RING,)),
                pltpu.SemaphoreType.DMA,
            ],
        ),
        out_shape=jax.ShapeDtypeStruct(x_sorted.shape, x_sorted.dtype),
        compiler_params=_params(1),
        name="experts",
    )(first_block, n_expert_blocks, n_used, x_sorted, w_gate_e, w_up_e, w_down_e)


def _gather_sc(dest_t, out_sorted):
    win = SC_WINDOW
    n = dest_t.shape[1]
    n_win = n // win
    workers = SC_CORES * SC_SUBCORES
    mesh = plsc.VectorSubcoreMesh(core_axis_name="c", subcore_axis_name="s")

    @functools.partial(
        pl.kernel, mesh=mesh, name="gather_sc",
        out_type=jax.ShapeDtypeStruct((TOP_K, n, WORD_SLABS, LANES), jnp.uint32),
        scratch_types=[pltpu.VMEM((TOP_K, win), jnp.int32), pltpu.VMEM((win, WORD_SLABS, LANES), jnp.uint32)])
    def gather(dest_hbm, os_hbm, g_hbm, idx_v, rows_v):
        wid = lax.axis_index("s") * SC_CORES + lax.axis_index("c")

        @pl.loop(0, (n_win - wid + workers - 1) // workers)
        def _(i):
            base = pl.multiple_of((wid + i * workers) * win, win)
            pltpu.sync_copy(dest_hbm.at[:, pl.ds(base, win)], idx_v)
            for k in range(TOP_K):
                pltpu.sync_copy(os_hbm.at[idx_v.at[k]], rows_v)
                pltpu.sync_copy(rows_v, g_hbm.at[k, pl.ds(base, win)])

    return gather(dest_t, out_sorted)


def _combine_kernel(g_ref, w_ref, yp_ref, ln_g_ref, ln_b_ref, y_ref):
    t = y_ref.shape[0]
    w = w_ref[...]
    lo_acc = jnp.zeros((t, PACKED), F32)
    hi_acc = jnp.zeros((t, PACKED), F32)
    for k in range(TOP_K):
        lo, hi = _unpack_halves(_load_row_tiles(g_ref.at[k], t, WORD_SLABS))
        lo_acc = lo_acc + w[:, k:k + 1] * lo
        hi_acc = hi_acc + w[:, k:k + 1] * hi
    routed = jnp.concatenate([lo_acc, hi_acc], axis=1)
    y_ref[...] = _layer_norm(yp_ref[...] + routed, ln_g_ref[...], ln_b_ref[...])


def _combine(g, w2, y_part, ln_g, ln_b, *, n, row_offset, name):
    tm = COMBINE_ROWS
    off = row_offset // tm
    fixed = lambda i: (0, 0)
    return pl.pallas_call(
        _combine_kernel,
        grid=(n // tm,),
        in_specs=[
            pl.BlockSpec((TOP_K, tm * WORD_SLABS, LANES), lambda i: (0, i + off, 0)),
            pl.BlockSpec((tm, TOP_K), lambda i: (i + off, 0)),
            pl.BlockSpec((tm, D_MODEL), lambda i: (i + off, 0)),
            pl.BlockSpec((1, D_MODEL), fixed),
            pl.BlockSpec((1, D_MODEL), fixed),
        ],
        out_specs=pl.BlockSpec((tm, D_MODEL), lambda i: (i, 0)),
        out_shape=jax.ShapeDtypeStruct((n, D_MODEL), F32),
        compiler_params=_params(1),
        name=name,
    )(g, w2, y_part, ln_g, ln_b)


def _route_kernel(lgp_ref, lgs_ref, bias_ref, eidx_ref, w_ref, rank_ref, cnt_ref, carry_ref, *, prompt_tiles):
    @pl.when(pl.program_id(0) == 0)
    def _():
        carry_ref[...] = jnp.zeros_like(carry_ref)

    t = lgp_ref.shape[1]
    gsz = N_EXPERTS // N_GROUPS
    neg = -jnp.inf
    s = jax.nn.sigmoid(jnp.where(pl.program_id(0) < prompt_tiles, lgp_ref[...], lgs_ref[...]))
    biased = s + bias_ref[...]
    io_g = lax.broadcasted_iota(jnp.int32, (gsz, t), 0)
    grp_rows = []
    for g in range(N_GROUPS):
        blk = biased[g * gsz:(g + 1) * gsz, :]
        m1 = jnp.max(blk, axis=0, keepdims=True)
        i1 = jnp.min(jnp.where(blk == m1, io_g, gsz), axis=0, keepdims=True)
        m2 = jnp.max(jnp.where(io_g == i1, neg, blk), axis=0, keepdims=True)
        grp_rows.append(m1 + m2)
    gs = jnp.concatenate(grp_rows, axis=0)
    io8 = lax.broadcasted_iota(jnp.int32, (N_GROUPS, t), 0)
    gsel = jnp.zeros((N_GROUPS, t), jnp.int32)
    for _ in range(TOPK_GROUPS):
        m = jnp.max(gs, axis=0, keepdims=True)
        gi = jnp.min(jnp.where(gs == m, io8, N_GROUPS), axis=0, keepdims=True)
        hit = io8 == gi
        gsel = jnp.where(hit, 1, gsel)
        gs = jnp.where(hit, neg, gs)
    masked = jnp.concatenate(
        [jnp.where(gsel[g:g + 1, :] > 0, biased[g * gsz:(g + 1) * gsz, :], neg) for g in range(N_GROUPS)], axis=0)

    eio = lax.broadcasted_iota(jnp.int32, (N_EXPERTS, t), 0)
    cur = masked
    idx_rows, w_rows = [], []
    for _ in range(TOP_K):
        m = jnp.max(cur, axis=0, keepdims=True)
        idx = jnp.min(jnp.where(cur == m, eio, N_EXPERTS), axis=0, keepdims=True)
        hit = eio == idx
        w_rows.append(jnp.sum(jnp.where(hit, s, 0.0), axis=0, keepdims=True))
        cur = jnp.where(hit, neg, cur)
        idx_rows.append(idx)
    sel = jnp.where(cur != masked, 1.0, 0.0)

    tri = jnp.where(lax.broadcasted_iota(jnp.int32, (t, t), 0) < lax.broadcasted_iota(jnp.int32, (t, t), 1), 1.0, 0.0)
    pref = jnp.dot(sel.astype(BF16), tri.astype(BF16), preferred_element_type=F32) + carry_ref[...]
    rank_rows = [jnp.sum(jnp.where(eio == idx_rows[k], pref, 0.0), axis=0, keepdims=True) for k in range(TOP_K)]
    carry_ref[...] += jnp.sum(sel, axis=1, keepdims=True)

    wk = jnp.concatenate(w_rows, axis=0)
    eidx_ref[...] = jnp.concatenate(idx_rows, axis=0)
    w_ref[...] = wk / jnp.sum(wk, axis=0, keepdims=True) * ROUTED_SCALE
    rank_ref[...] = jnp.concatenate(rank_rows, axis=0).astype(jnp.int32)
    cnt_ref[...] = carry_ref[...].astype(jnp.int32)


def _route(logits_p, logits_s, bias_col):
    t = ROUTE_TILE
    prompt_tiles = logits_p.shape[1] // t
    n = logits_p.shape[1] + logits_s.shape[1]
    col = lambda i: (0, i)
    fixed = lambda i: (0, 0)
    kern = functools.partial(_route_kernel, prompt_tiles=prompt_tiles)
    return pl.pallas_call(
        kern,
        grid=(n // t,),
        in_specs=[pl.BlockSpec((N_EXPERTS, t), lambda i: (0, jnp.minimum(i, prompt_tiles - 1))),
                  pl.BlockSpec((N_EXPERTS, t), lambda i: (0, jnp.maximum(i - prompt_tiles, 0))),
                  pl.BlockSpec((N_EXPERTS, 1), fixed)],
        out_specs=[
            pl.BlockSpec((TOP_K, t), col),
            pl.BlockSpec((TOP_K, t), col),
            pl.BlockSpec((TOP_K, t), col),
            pl.BlockSpec((N_EXPERTS, 1), fixed),
        ],
        out_shape=[
            jax.ShapeDtypeStruct((TOP_K, n), jnp.int32),
            jax.ShapeDtypeStruct((TOP_K, n), F32),
            jax.ShapeDtypeStruct((TOP_K, n), jnp.int32),
            jax.ShapeDtypeStruct((N_EXPERTS, 1), jnp.int32),
        ],
        scratch_shapes=[pltpu.VMEM((N_EXPERTS, 1), F32)],
        compiler_params=_params(1),
        name="route",
    )(logits_p, logits_s, bias_col)


def _dest_kernel(eidx_ref, rank_ref, start_ref, dest_ref):
    t = eidx_ref.shape[1]
    eio = lax.broadcasted_iota(jnp.int32, (N_EXPERTS, t), 0)
    start = start_ref[...]
    rows = [jnp.sum(jnp.where(eio == eidx_ref[k:k + 1, :], start, 0.0), axis=0, keepdims=True) for k in range(TOP_K)]
    dest_ref[...] = jnp.concatenate(rows, axis=0).astype(jnp.int32) + rank_ref[...]


def _dest(eidx_t, rank_t, pad_start_col):
    n = eidx_t.shape[1]
    t = ROW_TILE
    col = lambda i: (0, i)
    return pl.pallas_call(
        _dest_kernel,
        grid=(n // t,),
        in_specs=[pl.BlockSpec((TOP_K, t), col), pl.BlockSpec((TOP_K, t), col),
                  pl.BlockSpec((N_EXPERTS, 1), lambda i: (0, 0))],
        out_specs=pl.BlockSpec((TOP_K, t), col),
        out_shape=jax.ShapeDtypeStruct((TOP_K, n), jnp.int32),
        compiler_params=_params(1),
        name="dest",
    )(eidx_t, rank_t, pad_start_col)


def _dispatch_sc(dest_t, pk_p, pk_s, *, rows):
    win = SC_WINDOW
    p_win, s_win = pk_p.shape[0] // win, pk_s.shape[0] // win
    n_win = p_win + s_win
    workers = SC_CORES * SC_SUBCORES
    mesh = plsc.VectorSubcoreMesh(core_axis_name="c", subcore_axis_name="s")

    @functools.partial(
        pl.kernel, mesh=mesh, name="dispatch_sc",
        out_type=jax.ShapeDtypeStruct((rows, WORD_SLABS, LANES), jnp.uint32),
        scratch_types=[pltpu.VMEM((TOP_K, win), jnp.int32), pltpu.VMEM((win, WORD_SLABS, LANES), jnp.uint32)])
    def scatter(dest_hbm, pkp_hbm, pks_hbm, xs_hbm, idx_v, rows_v):
        wid = lax.axis_index("s") * SC_CORES + lax.axis_index("c")

        @pl.loop(0, (n_win - wid + workers - 1) // workers)
        def _(i):
            g = wid + i * workers

            @pl.when(g < p_win)
            def _():
                pltpu.sync_copy(pkp_hbm.at[pl.ds(pl.multiple_of(g * win, win), win)], rows_v)

            @pl.when(g >= p_win)
            def _():
                pltpu.sync_copy(pks_hbm.at[pl.ds(pl.multiple_of((g - p_win) * win, win), win)], rows_v)

            pltpu.sync_copy(dest_hbm.at[:, pl.ds(pl.multiple_of(g * win, win), win)], idx_v)
            for k in range(TOP_K):
                pltpu.sync_copy(rows_v, xs_hbm.at[idx_v.at[k]])

    return scatter(dest_t, pk_p, pk_s)


def _fill_kernel(fill_ref, len_ref, nu_ref, xs_in_ref, xs_ref, zbuf, fill_sem, *, n_blocks):
    del xs_in_ref
    zbuf[...] = jnp.zeros_like(zbuf)

    def fill_copy(row0, size):
        return pltpu.make_async_copy(zbuf.at[pl.ds(0, size)], xs_ref.at[pl.ds(row0, size)], fill_sem)

    def on_padding(fn):
        def body(e, c):
            base, length = fill_ref[e], len_ref[e]
            size = EXPERT_ROWS // 2
            while size >= 1:
                piece = fill_copy(base + (length & ~(2 * size - 1)), size)
                pl.when((length & size) != 0)(functools.partial(fn, piece))
                size //= 2
            return c
        lax.fori_loop(0, N_EXPERTS, body, 0)

    def on_unused_blocks(fn):
        lax.fori_loop(nu_ref[0], n_blocks, lambda b, c: (fn(fill_copy(b * EXPERT_ROWS, EXPERT_ROWS)), c)[1], 0)

    on_padding(lambda cp: cp.start())
    on_unused_blocks(lambda cp: cp.start())
    on_padding(lambda cp: cp.wait())
    on_unused_blocks(lambda cp: cp.wait())


def _fill_padding(fill_start, fill_len, n_used, x_sorted):
    n_blocks = x_sorted.shape[0] // EXPERT_ROWS
    return pl.pallas_call(
        functools.partial(_fill_kernel, n_blocks=n_blocks),
        grid_spec=pltpu.PrefetchScalarGridSpec(
            num_scalar_prefetch=3,
            grid=(1,),
            in_specs=[pl.BlockSpec(memory_space=pl.ANY)],
            out_specs=pl.BlockSpec(memory_space=pl.ANY),
            scratch_shapes=[pltpu.VMEM((EXPERT_ROWS, WORD_SLABS, LANES), jnp.uint32), pltpu.SemaphoreType.DMA],
        ),
        out_shape=jax.ShapeDtypeStruct(x_sorted.shape, x_sorted.dtype),
        input_output_aliases={3: 0},
        compiler_params=_params(1),
        name="fill_padding",
    )(fill_start, fill_len, n_used, x_sorted)


def _residual_kernel(hp_ref, hs_ref, wg_ref, wu_ref, wd_ref, yp_ref, *, prompt_tiles):
    h = jnp.where(pl.program_id(0) < prompt_tiles, hp_ref[...], hs_ref[...])
    hb = h.astype(BF16)
    a = jnp.dot(hb, wg_ref[...], preferred_element_type=F32)
    u = jnp.dot(hb, wu_ref[...], preferred_element_type=F32)
    shared = jnp.dot((a * jax.nn.sigmoid(a) * u).astype(BF16), wd_ref[...], preferred_element_type=F32)
    yp_ref[...] = ALPHA * h + shared


def _residual(h_p, h_s, wg_b, wu_b, wd_b):
    t = ROW_TILE
    prompt_tiles = h_p.shape[0] // t
    sample_tiles = h_s.shape[0] // t
    fixed = lambda i: (0, 0)
    return pl.pallas_call(
        functools.partial(_residual_kernel, prompt_tiles=prompt_tiles),
        grid=(prompt_tiles + sample_tiles,),
        in_specs=[
            pl.BlockSpec((t, D_MODEL), lambda i: (jnp.minimum(i, prompt_tiles - 1), 0)),
            pl.BlockSpec((t, D_MODEL), lambda i: (jnp.maximum(i - prompt_tiles, 0), 0)),
            pl.BlockSpec((D_MODEL, F_EXPERT), fixed),
            pl.BlockSpec((D_MODEL, F_EXPERT), fixed),
            pl.BlockSpec((F_EXPERT, D_MODEL), fixed),
        ],
        out_specs=pl.BlockSpec((t, D_MODEL), lambda i: (i, 0)),
        out_shape=jax.ShapeDtypeStruct(((prompt_tiles + sample_tiles) * t, D_MODEL), F32),
        compiler_params=_params(1),
        name="residual",
    )(h_p, h_s, wg_b, wu_b, wd_b)


def _block_plan(counts):
    padded = (counts + EXPERT_ROWS - 1) // EXPERT_ROWS * EXPERT_ROWS
    pad_end = jnp.cumsum(padded).astype(jnp.int32)
    pad_start = pad_end - padded
    n_used = pad_end[-1] // EXPERT_ROWS
    fill_start = pad_start + counts
    fill_len = pad_end - fill_start
    first_block = pad_start // EXPERT_ROWS
    n_expert_blocks = (padded // EXPERT_ROWS).astype(jnp.int32)
    return pad_start, fill_start, fill_len, first_block, n_expert_blocks, n_used.reshape(1).astype(jnp.int32)


def kernel(x_prompt, x_sample, cache_k, cache_v, w_in, sink, gm_ln_g, gm_ln_b, gm_w_s, gm_b_s, w_out, ln1_g, ln1_b,
           router_w, router_bias, w_gate_e, w_up_e, w_down_e, w_gate_s, w_up_s, w_down_s, ln2_g, ln2_b):
    bp, sp = x_prompt.shape[:2]
    bs, ts = x_sample.shape[:2]
    r = cache_k.shape[2]
    assert r == WINDOW and sp % ROW_TILE == 0 and (bs * ts) % ROW_TILE == 0
    n_p, n_s = bp * sp, bs * ts
    n_total = n_p + n_s
    l = 0

    w_in_b = w_in[l].astype(BF16)
    w_out_b = w_out[l].astype(BF16)
    router_t = router_w[l].T
    r_hi = router_t.astype(BF16)
    r_lo = (router_t - r_hi.astype(F32)).astype(BF16)
    row_vec = lambda v: v.reshape(1, -1)
    gm_g, gm_b = row_vec(gm_ln_g[l]), row_vec(gm_ln_b[l])
    sink_l = sink[l].astype(F32)

    xp2 = x_prompt.reshape(n_p, D_MODEL)
    tabs_p = _rope_tables(jnp.arange(sp, dtype=jnp.int32))
    q, k, v, gu, gv = _in_proj(xp2, w_in_b, tabs_p, gm_g, gm_b, tm=ROW_TILE, gv_dtype=BF16, name="in_proj_prompt")
    nb = sp // WINDOW
    att = _attention(sink_l, q, k, v, k, v, batch=bp, nb=nb, tq=WINDOW, seqs=1, chain=PROMPT_BLOCKS_PER_STEP,
                     prev_blocks=nb,
                     first_block_has_no_prev=True, name="attn_prompt")
    b_tab_p = jnp.repeat(gm_b_s[l].T, HEAD_DIM, axis=1)
    gm = _gate(gu, gv, gm_w_s[l], b_tab_p, chunk=CHUNK, n_chunks=ROW_TILE // CHUNK, name="gate_prompt")
    last_rows = lambda t: t.reshape(bp, sp, KV_WIDTH)[:, sp - r:, :].reshape(1, bp, r, KV_HEADS, HEAD_DIM)
    new_kp, new_vp = last_rows(k), last_rows(v)
    h_p, pk_p, lt_p = _out_proj(att, gm, xp2, w_out_b, row_vec(ln1_g[l]), row_vec(ln1_b[l]), r_hi, r_lo,
                          name="out_proj_prompt")

    xs2 = x_sample.reshape(n_s, D_MODEL)
    pos_s = PAST_LEN + jnp.arange(ts, dtype=jnp.int32)
    tabs_s = tuple(jnp.tile(t, (bs, 1)) for t in _rope_tables(pos_s))
    q, k, v, gu, gv = _in_proj(xs2, w_in_b, tabs_s, gm_g, gm_b, tm=n_s, gv_dtype=F32, name="in_proj_sample")
    tq = SLABS
    pad_rows = lambda t: jnp.pad(t.reshape(bs, ts, -1), ((0, 0), (0, tq - ts), (0, 0))).reshape(bs * tq, -1)
    ck2 = cache_k[l].reshape(bs * r, KV_WIDTH)
    cv2 = cache_v[l].reshape(bs * r, KV_WIDTH)
    att = _attention(sink_l, pad_rows(q), pad_rows(k), pad_rows(v), ck2, cv2, batch=bs, nb=1, tq=tq,
                     seqs=SAMPLE_SEQS_PER_STEP, chain=1, prev_blocks=1, first_block_has_no_prev=False, name="attn_sample")
    att = att.reshape(bs, tq, ATT_WIDTH)[:, :ts].reshape(n_s, ATT_WIDTH)
    gm = _gate_short(gu, gv, gm_w_s[l], gm_b_s[l], length=ts, name="gate_sample")
    new_ks = jnp.concatenate([cache_k[l], k.reshape(bs, ts, KV_HEADS, HEAD_DIM)], axis=1)[:, ts:][None]
    new_vs = jnp.concatenate([cache_v[l], v.reshape(bs, ts, KV_HEADS, HEAD_DIM)], axis=1)[:, ts:][None]
    new_gs = gv.reshape(bs, ts, GM_WIDTH)[None]
    h_s, pk_s, lt_s = _out_proj(att, gm, xs2, w_out_b, row_vec(ln1_g[l]), row_vec(ln1_b[l]), r_hi, r_lo,
                          name="out_proj_sample")

    eidx_t, w_t, rank_t, counts = _route(lt_p, lt_s, router_bias[l].astype(F32).reshape(N_EXPERTS, 1))
    a = n_total * TOP_K
    n_blocks = -(-(a + N_EXPERTS * (EXPERT_ROWS - 1)) // EXPERT_ROWS)
    pad_start, fill_start, fill_len, first_block, n_expert_blocks, n_used = _block_plan(counts.reshape(N_EXPERTS))
    dest_t = _dest(eidx_t, rank_t, pad_start.astype(F32).reshape(N_EXPERTS, 1))
    tiles = lambda a: a.reshape(-1, WORD_SLABS, LANES)
    shared = (w_gate_s[l].astype(BF16), w_up_s[l].astype(BF16), w_down_s[l].astype(BF16))
    x_sorted = _dispatch_sc(dest_t, tiles(pk_p), tiles(pk_s), rows=n_blocks * EXPERT_ROWS)
    y_part = _residual(h_p, h_s, *shared)
    x_sorted = _fill_padding(fill_start, fill_len, n_used, x_sorted)
    out_sorted = _experts(first_block, n_expert_blocks, n_used, x_sorted, w_gate_e[l], w_up_e[l], w_down_e[l])
    ln2 = (row_vec(ln2_g[l]), row_vec(ln2_b[l]))
    g = _gather_sc(dest_t, out_sorted).reshape(TOP_K, n_total * WORD_SLABS, LANES)
    w2 = w_t.T
    y_p = _combine(g, w2, y_part, *ln2, n=n_p, row_offset=0, name="combine_prompt")
    y_s = _combine(g, w2, y_part, *ln2, n=n_s, row_offset=n_p, name="combine_sample")
    return (y_p.reshape(bp, sp, D_MODEL), y_s.reshape(bs, ts, D_MODEL), new_kp, new_vp, new_ks, new_vs, new_gs)
```

```python
import functools

import jax
import jax.numpy as jnp
import numpy as np
from jax import lax
from jax.experimental import pallas as pl
from jax.experimental.pallas import tpu as pltpu
from jax.experimental.pallas import tpu_sc as plsc

D_MODEL = 1024
HEAD_DIM = 64
ATT_HEADS = 8
KV_HEADS = 2
Q_PER_KV = ATT_HEADS // KV_HEADS
GM_HEADS = 8
ATT_WIDTH = ATT_HEADS * HEAD_DIM
KV_WIDTH = KV_HEADS * HEAD_DIM
GM_WIDTH = GM_HEADS * HEAD_DIM
ROPE_WIDTH = ATT_WIDTH + KV_WIDTH
IN_WIDTH = ATT_WIDTH + 2 * KV_WIDTH + 2 * GM_WIDTH
WINDOW = 128
CHUNK = 128
PAST_LEN = 16384
ROPE_THETA = 10000.0
ATT_SCALE = HEAD_DIM ** -0.5
N_EXPERTS = 256
TOP_K = 8
N_GROUPS = 8
TOPK_GROUPS = 4
F_EXPERT = 256
ROUTED_SCALE = 2.5
LN_EPS = 1e-5
DEPTH = 1
ALPHA = (2.0 * DEPTH) ** 0.25

LANES = 128
SLABS = 8
PACKED = D_MODEL // 2
WORD_SLABS = PACKED // LANES
ROW_TILE = 512
EXPERT_ROWS = 256
EXPERT_RING = 8
COMBINE_ROWS = 512
ROUTE_TILE = 512
SAMPLE_SEQS_PER_STEP = 16
PROMPT_BLOCKS_PER_STEP = 2
SC_CORES = 2
SC_SUBCORES = 16
SC_WINDOW = 128
VMEM_LIMIT = 56 * 1024 * 1024

F32 = jnp.float32
BF16 = jnp.bfloat16


def _params(n_axes):
    return pltpu.CompilerParams(dimension_semantics=("arbitrary",) * n_axes, vmem_limit_bytes=VMEM_LIMIT)


def _layer_norm(x, g, b):
    mu = jnp.mean(x, axis=-1, keepdims=True)
    xc = x - mu
    var = jnp.mean(xc * xc, axis=-1, keepdims=True)
    return xc * lax.rsqrt(var + LN_EPS) * g + b


def _pack_halves(x):
    xb = x.astype(BF16)
    lo = lax.bitcast_convert_type(xb[:, :PACKED].astype(F32), jnp.uint32)
    hi = lax.bitcast_convert_type(xb[:, PACKED:].astype(F32), jnp.uint32)
    return (lo >> 16) | hi


def _unpack_halves(w):
    lo = lax.bitcast_convert_type(w << 16, F32)
    hi = lax.bitcast_convert_type(w & jnp.uint32(0xFFFF0000), F32)
    return lo, hi


def _store_row_tiles(ref2d, x):
    m, slabs = x.shape[0], x.shape[1] // LANES
    for s in range(slabs):
        ref2d[pl.ds(s, m, stride=slabs), :] = x[:, s * LANES:(s + 1) * LANES]


def _load_row_tiles(ref2d, m, slabs):
    return jnp.concatenate([ref2d[pl.ds(s, m, stride=slabs), :] for s in range(slabs)], axis=1)


def _gelu(x):
    return 0.5 * x * (1.0 + lax.erf(x * np.float32(np.sqrt(0.5))))


def _in_proj_kernel(x_ref, w_ref, cos_ref, sa_ref, sb_ref, g_ref, b_ref, q_ref, k_ref, v_ref, gu_ref, gv_ref):
    x = x_ref[...].astype(BF16)
    zr = jnp.dot(x, w_ref[:, :ROPE_WIDTH], preferred_element_type=F32)
    pieces = []
    for c in range(ROPE_WIDTH // LANES):
        zc = zr[:, c * LANES:(c + 1) * LANES]
        tl = slice(0, LANES) if c < ATT_WIDTH // LANES else slice(LANES, 2 * LANES)
        pieces.append(zc * cos_ref[:, tl]
                      + pltpu.roll(zc, LANES - HEAD_DIM // 2, 1) * sa_ref[:, tl]
                      + pltpu.roll(zc, HEAD_DIM // 2, 1) * sb_ref[:, tl])
    for c in range(ATT_WIDTH // LANES):
        q_ref[:, c * LANES:(c + 1) * LANES] = pieces[c].astype(q_ref.dtype)
    k_ref[...] = pieces[ATT_WIDTH // LANES]
    v_ref[...] = jnp.dot(x, w_ref[:, ROPE_WIDTH:ROPE_WIDTH + KV_WIDTH], preferred_element_type=F32)
    g0 = ROPE_WIDTH + KV_WIDTH
    zu = jnp.dot(x, w_ref[:, g0:g0 + GM_WIDTH], preferred_element_type=F32)
    gu_ref[...] = _gelu(zu).astype(gu_ref.dtype)
    zv = jnp.dot(x, w_ref[:, g0 + GM_WIDTH:g0 + 2 * GM_WIDTH], preferred_element_type=F32)
    gv = _layer_norm(_gelu(zv), g_ref[...], b_ref[...])
    gv_ref[...] = gv.astype(gv_ref.dtype)


def _in_proj(x2, w_in_b, tabs, ln_g, ln_b, *, tm, gv_dtype, name):
    n = x2.shape[0]
    cos_t, sa_t, sb_t = tabs
    period = cos_t.shape[0] // tm
    row = lambda i: (i, 0)
    tab = lambda i: (i % period, 0)
    fixed = lambda i: (0, 0)
    return pl.pallas_call(
        _in_proj_kernel,
        grid=(n // tm,),
        in_specs=[
            pl.BlockSpec((tm, D_MODEL), row),
            pl.BlockSpec((D_MODEL, IN_WIDTH), fixed),
            pl.BlockSpec((tm, 2 * LANES), tab),
            pl.BlockSpec((tm, 2 * LANES), tab),
            pl.BlockSpec((tm, 2 * LANES), tab),
            pl.BlockSpec((1, GM_WIDTH), fixed),
            pl.BlockSpec((1, GM_WIDTH), fixed),
        ],
        out_specs=[
            pl.BlockSpec((tm, ATT_WIDTH), row),
            pl.BlockSpec((tm, KV_WIDTH), row),
            pl.BlockSpec((tm, KV_WIDTH), row),
            pl.BlockSpec((tm, GM_WIDTH), row),
            pl.BlockSpec((tm, GM_WIDTH), row),
        ],
        out_shape=[
            jax.ShapeDtypeStruct((n, ATT_WIDTH), BF16),
            jax.ShapeDtypeStruct((n, KV_WIDTH), F32),
            jax.ShapeDtypeStruct((n, KV_WIDTH), F32),
            jax.ShapeDtypeStruct((n, GM_WIDTH), BF16),
            jax.ShapeDtypeStruct((n, GM_WIDTH), gv_dtype),
        ],
        compiler_params=_params(1),
        name=name,
    )(x2, w_in_b, cos_t, sa_t, sb_t, ln_g, ln_b)


def _rope_tables(pos):
    half = HEAD_DIM // 2
    lane = jnp.arange(2 * LANES, dtype=jnp.int32)
    inv = ROPE_THETA ** (-(lane % half).astype(F32) * 2.0 / HEAD_DIM)
    ang = pos.astype(F32)[:, None] * inv[None, :]
    scale = jnp.where(lane < LANES, ATT_SCALE, 1.0).astype(F32)[None, :]
    first_half = ((lane % HEAD_DIM) < half)[None, :]
    cos, sin = jnp.cos(ang) * scale, jnp.sin(ang) * scale
    return cos, jnp.where(first_half, -sin, 0.0), jnp.where(first_half, 0.0, sin)


def _attn_kernel(sink_ref, q_ref, kc_ref, vc_ref, kp_ref, vp_ref, o_ref, *, tq, seqs, chain, stack,
                 first_block_has_no_prev):
    nk = WINDOW + tq
    rows = stack * tq
    qi = lax.broadcasted_iota(jnp.int32, (rows, nk), 0) & (tq - 1)
    ks = lax.broadcasted_iota(jnp.int32, (rows, nk), 1)
    band = (ks >= qi) & (ks <= qi + WINDOW)
    first_mask = band & ((pl.program_id(1) > 0) | (ks >= WINDOW)) if first_block_has_no_prev else band
    sinks = [jnp.concatenate([jnp.full((tq, 1), sink_ref[h0 + j], F32) for j in range(stack)], axis=0)
             for h0 in range(0, ATT_HEADS, stack)]
    for b in range(seqs * chain):
        qrows = slice(b * tq, (b + 1) * tq)
        if chain > 1 and b > 0:
            kprev, vprev = kc_ref[(b - 1) * tq:b * tq, :], vc_ref[(b - 1) * tq:b * tq, :]
        else:
            kprev, vprev = kp_ref[b * WINDOW:(b + 1) * WINDOW, :], vp_ref[b * WINDOW:(b + 1) * WINDOW, :]
        mask = band if (chain > 1 and b > 0) else first_mask
        q = q_ref[qrows, :]
        kk = jnp.concatenate([kprev, kc_ref[qrows, :]], axis=0).astype(BF16)
        vv = jnp.concatenate([vprev, vc_ref[qrows, :]], axis=0).astype(BF16)
        outs = []
        for i, h0 in enumerate(range(0, ATT_HEADS, stack)):
            g = h0 // Q_PER_KV
            kg = kk[:, g * HEAD_DIM:(g + 1) * HEAD_DIM]
            vg = vv[:, g * HEAD_DIM:(g + 1) * HEAD_DIM]
            qg = jnp.concatenate([q[:, (h0 + j) * HEAD_DIM:(h0 + j + 1) * HEAD_DIM] for j in range(stack)], axis=0)
            s = lax.dot_general(qg, kg, (((1,), (1,)), ((), ())), preferred_element_type=F32)
            s = jnp.where(mask, s, -jnp.inf)
            m = jnp.maximum(jnp.max(s, axis=-1, keepdims=True), sinks[i])
            p = jnp.exp(s - m)
            denom = jnp.sum(p, axis=-1, keepdims=True) + jnp.exp(sinks[i] - m)
            o = jnp.dot((p / denom).astype(BF16), vg, preferred_element_type=F32)
            outs.extend(o[j * tq:(j + 1) * tq, :] for j in range(stack))
        o_ref[qrows, :] = jnp.concatenate(outs, axis=1).astype(o_ref.dtype)


def _attention(sink, q2, k2, v2, kprev2, vprev2, *, batch, nb, tq, seqs, chain, prev_blocks, first_block_has_no_prev,
               name):
    assert tq & (tq - 1) == 0 and batch % seqs == 0 and (seqs == 1 or nb == prev_blocks == 1)
    assert (seqs == 1 or chain == 1) and nb % chain == 0 and (chain == 1 or tq == WINDOW)
    steps = nb // chain
    cur = lambda b, n, s: (b * steps + n, 0)
    prev = lambda b, n, s: (b * prev_blocks + jnp.maximum(n * chain - 1, 0), 0)
    stack = Q_PER_KV if Q_PER_KV * tq <= WINDOW else 1
    kern = functools.partial(_attn_kernel, tq=tq, seqs=seqs, chain=chain, stack=stack,
                             first_block_has_no_prev=first_block_has_no_prev)
    return pl.pallas_call(
        kern,
        grid_spec=pltpu.PrefetchScalarGridSpec(
            num_scalar_prefetch=1,
            grid=(batch // seqs, steps),
            in_specs=[
                pl.BlockSpec((seqs * chain * tq, ATT_WIDTH), cur),
                pl.BlockSpec((seqs * chain * tq, KV_WIDTH), cur),
                pl.BlockSpec((seqs * chain * tq, KV_WIDTH), cur),
                pl.BlockSpec((seqs * WINDOW, KV_WIDTH), prev),
                pl.BlockSpec((seqs * WINDOW, KV_WIDTH), prev),
            ],
            out_specs=pl.BlockSpec((seqs * chain * tq, ATT_WIDTH), cur),
        ),
        out_shape=jax.ShapeDtypeStruct(q2.shape, BF16),
        compiler_params=_params(2),
        name=name,
    )(sink, q2, k2, v2, kprev2, vprev2)


def _gate_kernel(gu_ref, gv_ref, w_ref, b_ref, o_ref, *, chunk, n_chunks):
    ri = lax.broadcasted_iota(jnp.int32, (chunk, chunk), 0)
    ci = lax.broadcasted_iota(jnp.int32, (chunk, chunk), 1)
    ws = [jnp.where(ci <= ri, w_ref[h], 0.0).astype(BF16) for h in range(GM_HEADS)]
    for c in range(n_chunks):
        rows = slice(c * chunk, (c + 1) * chunk)
        gv = gv_ref[rows, :].astype(BF16)
        sv = jnp.concatenate(
            [jnp.dot(ws[h], gv[:, h * HEAD_DIM:(h + 1) * HEAD_DIM], preferred_element_type=F32)
             for h in range(GM_HEADS)], axis=1)
        o_ref[rows, :] = (gu_ref[rows, :].astype(F32) * (sv + b_ref[...])).astype(o_ref.dtype)


def _gate(gu2, gv2, w_s, b_tab, *, chunk, n_chunks, name):
    n = gu2.shape[0]
    tm = chunk * n_chunks
    row = lambda i: (i, 0)
    kern = functools.partial(_gate_kernel, chunk=chunk, n_chunks=n_chunks)
    return pl.pallas_call(
        kern,
        grid=(n // tm,),
        in_specs=[
            pl.BlockSpec((tm, GM_WIDTH), row),
            pl.BlockSpec((tm, GM_WIDTH), row),
            pl.BlockSpec((GM_HEADS, chunk, chunk), lambda i: (0, 0, 0)),
            pl.BlockSpec((chunk, GM_WIDTH), lambda i: (0, 0)),
        ],
        out_specs=pl.BlockSpec((tm, GM_WIDTH), row),
        out_shape=jax.ShapeDtypeStruct((n, GM_WIDTH), BF16),
        compiler_params=_params(1),
        name=name,
    )(gu2, gv2, w_s, b_tab)


def _gate_short_kernel(gu_ref, gv_ref, coef_ref, b_ref, o_ref, *, length):
    n = gv_ref.shape[0]
    gv = gv_ref[...].astype(F32)
    tiles = lambda x: x.reshape(n // SLABS, SLABS, GM_WIDTH)
    sv = tiles(gv) * coef_ref[0][None] + b_ref[...][None]
    for d in range(1, length):
        sv = sv + tiles(pltpu.roll(gv, d, 0)) * coef_ref[d][None]
    o_ref[...] = (gu_ref[...].astype(F32) * sv.reshape(n, GM_WIDTH)).astype(o_ref.dtype)


def _gate_short(gu2, gv2, w_s, b_s, *, length, name):
    n = gu2.shape[0]
    assert SLABS % length == 0 and n % SLABS == 0
    pos = jnp.arange(length)
    lag = jnp.arange(length)
    src = pos[None, :] - lag[:, None]
    coef = jnp.where(src >= 0, w_s[:, pos[None, :], jnp.maximum(src, 0)], 0.0)
    coef = jnp.repeat(coef.transpose(1, 2, 0), HEAD_DIM, axis=2)
    coef = jnp.tile(coef, (1, SLABS // length, 1))
    bias = jnp.tile(jnp.repeat(b_s[:, :length].T, HEAD_DIM, axis=1), (SLABS // length, 1))
    whole = lambda shape: pl.BlockSpec(shape, lambda i: (0,) * len(shape))
    return pl.pallas_call(
        functools.partial(_gate_short_kernel, length=length),
        grid=(1,),
        in_specs=[whole((n, GM_WIDTH)), whole((n, GM_WIDTH)), whole((length, SLABS, GM_WIDTH)),
                  whole((SLABS, GM_WIDTH))],
        out_specs=whole((n, GM_WIDTH)),
        out_shape=jax.ShapeDtypeStruct((n, GM_WIDTH), BF16),
        compiler_params=_params(1),
        name=name,
    )(gu2, gv2, coef, bias)


def _out_proj_kernel(att_ref, gm_ref, x_ref, wo_ref, g_ref, b_ref, rh_ref, rl_ref, h_ref, hp_ref, lg_ref):
    mix = jnp.dot(att_ref[...], wo_ref[:ATT_WIDTH, :], preferred_element_type=F32)
    mix = mix + jnp.dot(gm_ref[...], wo_ref[ATT_WIDTH:, :], preferred_element_type=F32)
    h = _layer_norm(ALPHA * x_ref[...] + mix, g_ref[...], b_ref[...])
    h_ref[...] = h
    _store_row_tiles(hp_ref, _pack_halves(h))
    h_hi = h.astype(BF16)
    h_lo = (h - h_hi.astype(F32)).astype(BF16)
    nt = (((1,), (1,)), ((), ()))
    lg = lax.dot_general(rh_ref[...], h_hi, nt, preferred_element_type=F32)
    lg = lg + lax.dot_general(rh_ref[...], h_lo, nt, preferred_element_type=F32)
    lg = lg + lax.dot_general(rl_ref[...], h_hi, nt, preferred_element_type=F32)
    lg_ref[...] = lg


def _out_proj(att2, gm2, x2, w_out_b, ln_g, ln_b, r_hi, r_lo, *, name):
    n = x2.shape[0]
    tm = ROW_TILE
    row = lambda i: (i, 0)
    fixed = lambda i: (0, 0)
    return pl.pallas_call(
        _out_proj_kernel,
        grid=(n // tm,),
        in_specs=[
            pl.BlockSpec((tm, ATT_WIDTH), row),
            pl.BlockSpec((tm, GM_WIDTH), row),
            pl.BlockSpec((tm, D_MODEL), row),
            pl.BlockSpec((D_MODEL, D_MODEL), fixed),
            pl.BlockSpec((1, D_MODEL), fixed),
            pl.BlockSpec((1, D_MODEL), fixed),
            pl.BlockSpec((N_EXPERTS, D_MODEL), fixed),
            pl.BlockSpec((N_EXPERTS, D_MODEL), fixed),
        ],
        out_specs=[
            pl.BlockSpec((tm, D_MODEL), row),
            pl.BlockSpec((tm * WORD_SLABS, LANES), row),
            pl.BlockSpec((N_EXPERTS, tm), lambda i: (0, i)),
        ],
        out_shape=[
            jax.ShapeDtypeStruct((n, D_MODEL), F32),
            jax.ShapeDtypeStruct((n * WORD_SLABS, LANES), jnp.uint32),
            jax.ShapeDtypeStruct((N_EXPERTS, n), F32),
        ],
        compiler_params=_params(1),
        name=name,
    )(att2, gm2, x2, w_out_b, ln_g, ln_b, r_hi, r_lo)


def _experts_kernel(first_ref, nblk_ref, nu_ref, x_hbm, wg_ref, wu_ref, wd_ref, o_hbm, xbuf, obuf, wg_s, wu_s, wd_s,
                    in_sem, out_sem, fill_sem, *, n_blocks):
    e = pl.program_id(0)
    nb = nblk_ref[e]
    b0 = first_ref[e]
    n_used = nu_ref[0]
    ahead = EXPERT_RING // 2

    def rows_of(block):
        return pl.ds(pl.multiple_of(block * EXPERT_ROWS, EXPERT_ROWS), EXPERT_ROWS)

    def slot_of(block):
        return block & (EXPERT_RING - 1)

    def slab(s):
        return pl.ds(s * LANES, LANES)

    def in_copies(block):
        slot = slot_of(block)
        return [pltpu.make_async_copy(x_hbm.at[rows_of(block), s, :], xbuf.at[slot, :, slab(s)], in_sem.at[slot])
                for s in range(WORD_SLABS)]

    def out_copies(block, slot=None, sem=None):
        slot = slot_of(block) if slot is None else slot
        sem = out_sem.at[slot] if sem is None else sem
        return [pltpu.make_async_copy(obuf.at[slot, :, slab(s)], o_hbm.at[rows_of(block), s, :], sem)
                for s in range(WORD_SLABS)]

    def start(copies):
        for cp in copies:
            cp.start()

    def wait(copies):
        for cp in copies:
            cp.wait()

    @pl.when(e == 0)
    def _():
        for j in range(ahead):
            pl.when(j < n_used)(lambda j=j: start(in_copies(j)))

    def process(block, count):
        blocks = [block + j for j in range(count)]
        for blk in blocks:
            wait(in_copies(blk))
        x = [xbuf[slot_of(blk)] for blk in blocks]
        xl, xh = _unpack_halves(x[0] if count == 1 else jnp.concatenate(x, axis=0))
        xl, xh = xl.astype(BF16), xh.astype(BF16)
        for blk in blocks:
            pl.when(blk + ahead < n_used)(lambda blk=blk: start(in_copies(blk + ahead)))
            pl.when(blk >= ahead)(lambda blk=blk: wait(out_copies(blk - ahead)))
        a = (jnp.dot(xl, wg_s[:PACKED, :], preferred_element_type=F32)
             + jnp.dot(xh, wg_s[PACKED:, :], preferred_element_type=F32))
        u = (jnp.dot(xl, wu_s[:PACKED, :], preferred_element_type=F32)
             + jnp.dot(xh, wu_s[PACKED:, :], preferred_element_type=F32))
        hb = (a * jax.nn.sigmoid(a) * u).astype(BF16)
        o = _pack_halves(jnp.dot(hb, wd_s[...], preferred_element_type=F32))
        for j, blk in enumerate(blocks):
            obuf[slot_of(blk)] = o[j * EXPERT_ROWS:(j + 1) * EXPERT_ROWS, :]
            start(out_copies(blk))

    @pl.when(nb > 0)
    def _():
        wg_s[...] = wg_ref[0].astype(BF16)
        wu_s[...] = wu_ref[0].astype(BF16)
        wd_s[...] = wd_ref[0].astype(BF16)

        def pair(i, carry):
            process(b0 + 2 * i, 2)
            return carry

        lax.fori_loop(0, nb // 2, pair, 0)
        pl.when(nb % 2 == 1)(lambda: process(b0 + nb - 1, 1))

    @pl.when(e == N_EXPERTS - 1)
    def _():
        for j in range(ahead):
            pl.when(n_used - 1 - j >= 0)(lambda j=j: wait(out_copies(n_used - 1 - j)))
        obuf[0] = jnp.zeros((EXPERT_ROWS, PACKED), jnp.uint32)

        def on_unused_blocks(fn):
            def body(b, c):
                fn(out_copies(b, slot=0, sem=fill_sem))
                return c
            lax.fori_loop(n_used, n_blocks, body, 0)

        on_unused_blocks(start)
        on_unused_blocks(wait)


def _experts(first_block, n_expert_blocks, n_used, x_sorted, w_gate_e, w_up_e, w_down_e):
    rows = x_sorted.shape[0]
    wmap = lambda e, *_: (e, 0, 0)
    kern = functools.partial(_experts_kernel, n_blocks=rows // EXPERT_ROWS)
    return pl.pallas_call(
        kern,
        grid_spec=pltpu.PrefetchScalarGridSpec(
            num_scalar_prefetch=3,
            grid=(N_EXPERTS,),
            in_specs=[
                pl.BlockSpec(memory_space=pl.ANY),
                pl.BlockSpec((1, D_MODEL, F_EXPERT), wmap),
                pl.BlockSpec((1, D_MODEL, F_EXPERT), wmap),
                pl.BlockSpec((1, F_EXPERT, D_MODEL), wmap),
            ],
            out_specs=pl.BlockSpec(memory_space=pl.ANY),
            scratch_shapes=[
                pltpu.VMEM((EXPERT_RING, EXPERT_ROWS, PACKED), jnp.uint32),
                pltpu.VMEM((EXPERT_RING, EXPERT_ROWS, PACKED), jnp.uint32),
                pltpu.VMEM((D_MODEL, F_EXPERT), BF16),
                pltpu.VMEM((D_MODEL, F_EXPERT), BF16),
                pltpu.VMEM((F_EXPERT, D_MODEL), BF16),
                pltpu.SemaphoreType.DMA((EXPERT_RING,)),
                pltpu.SemaphoreType.DMA((EXPERT_RING,)),
                pltpu.SemaphoreType.DMA,
            ],
        ),
        out_shape=jax.ShapeDtypeStruct(x_sorted.shape, x_sorted.dtype),
        compiler_params=_params(1),
        name="experts",
    )(first_block, n_expert_blocks, n_used, x_sorted, w_gate_e, w_up_e, w_down_e)


def _gather_sc(dest_t, out_sorted):
    win = SC_WINDOW
    n = dest_t.shape[1]
    n_win = n // win
    workers = SC_CORES * SC_SUBCORES
    mesh = plsc.VectorSubcoreMesh(core_axis_name="c", subcore_axis_name="s")

    @functools.partial(
        pl.kernel, mesh=mesh, name="gather_sc",
        out_type=jax.ShapeDtypeStruct((TOP_K, n, WORD_SLABS, LANES), jnp.uint32),
        scratch_types=[pltpu.VMEM((TOP_K, win), jnp.int32), pltpu.VMEM((win, WORD_SLABS, LANES), jnp.uint32)])
    def gather(dest_hbm, os_hbm, g_hbm, idx_v, rows_v):
        wid = lax.axis_index("s") * SC_CORES + lax.axis_index("c")

        @pl.loop(0, (n_win - wid + workers - 1) // workers)
        def _(i):
            base = pl.multiple_of((wid + i * workers) * win, win)
            pltpu.sync_copy(dest_hbm.at[:, pl.ds(base, win)], idx_v)
            for k in range(TOP_K):
                pltpu.sync_copy(os_hbm.at[idx_v.at[k]], rows_v)
                pltpu.sync_copy(rows_v, g_hbm.at[k, pl.ds(base, win)])

    return gather(dest_t, out_sorted)


def _combine_kernel(g_ref, w_ref, yp_ref, ln_g_ref, ln_b_ref, y_ref):
    t = y_ref.shape[0]
    w = w_ref[...]
    lo_acc = jnp.zeros((t, PACKED), F32)
    hi_acc = jnp.zeros((t, PACKED), F32)
    for k in range(TOP_K):
        lo, hi = _unpack_halves(_load_row_tiles(g_ref.at[k], t, WORD_SLABS))
        lo_acc = lo_acc + w[:, k:k + 1] * lo
        hi_acc = hi_acc + w[:, k:k + 1] * hi
    routed = jnp.concatenate([lo_acc, hi_acc], axis=1)
    y_ref[...] = _layer_norm(yp_ref[...] + routed, ln_g_ref[...], ln_b_ref[...])


def _combine(g, w2, y_part, ln_g, ln_b, *, n, row_offset, name):
    tm = COMBINE_ROWS
    off = row_offset // tm
    fixed = lambda i: (0, 0)
    return pl.pallas_call(
        _combine_kernel,
        grid=(n // tm,),
        in_specs=[
            pl.BlockSpec((TOP_K, tm * WORD_SLABS, LANES), lambda i: (0, i + off, 0)),
            pl.BlockSpec((tm, TOP_K), lambda i: (i + off, 0)),
            pl.BlockSpec((tm, D_MODEL), lambda i: (i + off, 0)),
            pl.BlockSpec((1, D_MODEL), fixed),
            pl.BlockSpec((1, D_MODEL), fixed),
        ],
        out_specs=pl.BlockSpec((tm, D_MODEL), lambda i: (i, 0)),
        out_shape=jax.ShapeDtypeStruct((n, D_MODEL), F32),
        compiler_params=_params(1),
        name=name,
    )(g, w2, y_part, ln_g, ln_b)


def _route_kernel(lgp_ref, lgs_ref, bias_ref, eidx_ref, w_ref, rank_ref, cnt_ref, carry_ref, *, prompt_tiles):
    @pl.when(pl.program_id(0) == 0)
    def _():
        carry_ref[...] = jnp.zeros_like(carry_ref)

    t = lgp_ref.shape[1]
    gsz = N_EXPERTS // N_GROUPS
    neg = -jnp.inf
    s = jax.nn.sigmoid(jnp.where(pl.program_id(0) < prompt_tiles, lgp_ref[...], lgs_ref[...]))
    biased = s + bias_ref[...]
    io_g = lax.broadcasted_iota(jnp.int32, (gsz, t), 0)
    grp_rows = []
    for g in range(N_GROUPS):
        blk = biased[g * gsz:(g + 1) * gsz, :]
        m1 = jnp.max(blk, axis=0, keepdims=True)
        i1 = jnp.min(jnp.where(blk == m1, io_g, gsz), axis=0, keepdims=True)
        m2 = jnp.max(jnp.where(io_g == i1, neg, blk), axis=0, keepdims=True)
        grp_rows.append(m1 + m2)
    gs = jnp.concatenate(grp_rows, axis=0)
    io8 = lax.broadcasted_iota(jnp.int32, (N_GROUPS, t), 0)
    gsel = jnp.zeros((N_GROUPS, t), jnp.int32)
    for _ in range(TOPK_GROUPS):
        m = jnp.max(gs, axis=0, keepdims=True)
        gi = jnp.min(jnp.where(gs == m, io8, N_GROUPS), axis=0, keepdims=True)
        hit = io8 == gi
        gsel = jnp.where(hit, 1, gsel)
        gs = jnp.where(hit, neg, gs)
    masked = jnp.concatenate(
        [jnp.where(gsel[g:g + 1, :] > 0, biased[g * gsz:(g + 1) * gsz, :], neg) for g in range(N_GROUPS)], axis=0)

    eio = lax.broadcasted_iota(jnp.int32, (N_EXPERTS, t), 0)
    cur = masked
    idx_rows, w_rows = [], []
    for _ in range(TOP_K):
        m = jnp.max(cur, axis=0, keepdims=True)
        idx = jnp.min(jnp.where(cur == m, eio, N_EXPERTS), axis=0, keepdims=True)
        hit = eio == idx
        w_rows.append(jnp.sum(jnp.where(hit, s, 0.0), axis=0, keepdims=True))
        cur = jnp.where(hit, neg, cur)
        idx_rows.append(idx)
    sel = jnp.where(cur != masked, 1.0, 0.0)

    tri = jnp.where(lax.broadcasted_iota(jnp.int32, (t, t), 0) < lax.broadcasted_iota(jnp.int32, (t, t), 1), 1.0, 0.0)
    pref = jnp.dot(sel.astype(BF16), tri.astype(BF16), preferred_element_type=F32) + carry_ref[...]
    rank_rows = [jnp.sum(jnp.where(eio == idx_rows[k], pref, 0.0), axis=0, keepdims=True) for k in range(TOP_K)]
    carry_ref[...] += jnp.sum(sel, axis=1, keepdims=True)

    wk = jnp.concatenate(w_rows, axis=0)
    eidx_ref[...] = jnp.concatenate(idx_rows, axis=0)
    w_ref[...] = wk / jnp.sum(wk, axis=0, keepdims=True) * ROUTED_SCALE
    rank_ref[...] = jnp.concatenate(rank_rows, axis=0).astype(jnp.int32)
    cnt_ref[...] = carry_ref[...].astype(jnp.int32)


def _route(logits_p, logits_s, bias_col):
    t = ROUTE_TILE
    prompt_tiles = logits_p.shape[1] // t
    n = logits_p.shape[1] + logits_s.shape[1]
    col = lambda i: (0, i)
    fixed = lambda i: (0, 0)
    kern = functools.partial(_route_kernel, prompt_tiles=prompt_tiles)
    return pl.pallas_call(
        kern,
        grid=(n // t,),
        in_specs=[pl.BlockSpec((N_EXPERTS, t), lambda i: (0, jnp.minimum(i, prompt_tiles - 1))),
                  pl.BlockSpec((N_EXPERTS, t), lambda i: (0, jnp.maximum(i - prompt_tiles, 0))),
                  pl.BlockSpec((N_EXPERTS, 1), fixed)],
        out_specs=[
            pl.BlockSpec((TOP_K, t), col),
            pl.BlockSpec((TOP_K, t), col),
            pl.BlockSpec((TOP_K, t), col),
            pl.BlockSpec((N_EXPERTS, 1), fixed),
        ],
        out_shape=[
            jax.ShapeDtypeStruct((TOP_K, n), jnp.int32),
            jax.ShapeDtypeStruct((TOP_K, n), F32),
            jax.ShapeDtypeStruct((TOP_K, n), jnp.int32),
            jax.ShapeDtypeStruct((N_EXPERTS, 1), jnp.int32),
        ],
        scratch_shapes=[pltpu.VMEM((N_EXPERTS, 1), F32)],
        compiler_params=_params(1),
        name="route",
    )(logits_p, logits_s, bias_col)


def _dest_kernel(eidx_ref, rank_ref, start_ref, dest_ref):
    t = eidx_ref.shape[1]
    eio = lax.broadcasted_iota(jnp.int32, (N_EXPERTS, t), 0)
    start = start_ref[...]
    rows = [jnp.sum(jnp.where(eio == eidx_ref[k:k + 1, :], start, 0.0), axis=0, keepdims=True) for k in range(TOP_K)]
    dest_ref[...] = jnp.concatenate(rows, axis=0).astype(jnp.int32) + rank_ref[...]


def _dest(eidx_t, rank_t, pad_start_col):
    n = eidx_t.shape[1]
    t = ROW_TILE
    col = lambda i: (0, i)
    return pl.pallas_call(
        _dest_kernel,
        grid=(n // t,),
        in_specs=[pl.BlockSpec((TOP_K, t), col), pl.BlockSpec((TOP_K, t), col),
                  pl.BlockSpec((N_EXPERTS, 1), lambda i: (0, 0))],
        out_specs=pl.BlockSpec((TOP_K, t), col),
        out_shape=jax.ShapeDtypeStruct((TOP_K, n), jnp.int32),
        compiler_params=_params(1),
        name="dest",
    )(eidx_t, rank_t, pad_start_col)


def _dispatch_sc(dest_t, pk_p, pk_s, *, rows):
    win = SC_WINDOW
    p_win, s_win = pk_p.shape[0] // win, pk_s.shape[0] // win
    n_win = p_win + s_win
    workers = SC_CORES * SC_SUBCORES
    mesh = plsc.VectorSubcoreMesh(core_axis_name="c", subcore_axis_name="s")

    @functools.partial(
        pl.kernel, mesh=mesh, name="dispatch_sc",
        out_type=jax.ShapeDtypeStruct((rows, WORD_SLABS, LANES), jnp.uint32),
        scratch_types=[pltpu.VMEM((TOP_K, win), jnp.int32), pltpu.VMEM((win, WORD_SLABS, LANES), jnp.uint32)])
    def scatter(dest_hbm, pkp_hbm, pks_hbm, xs_hbm, idx_v, rows_v):
        wid = lax.axis_index("s") * SC_CORES + lax.axis_index("c")

        @pl.loop(0, (n_win - wid + workers - 1) // workers)
        def _(i):
            g = wid + i * workers

            @pl.when(g < p_win)
            def _():
                pltpu.sync_copy(pkp_hbm.at[pl.ds(pl.multiple_of(g * win, win), win)], rows_v)

            @pl.when(g >= p_win)
            def _():
                pltpu.sync_copy(pks_hbm.at[pl.ds(pl.multiple_of((g - p_win) * win, win), win)], rows_v)

            pltpu.sync_copy(dest_hbm.at[:, pl.ds(pl.multiple_of(g * win, win), win)], idx_v)
            for k in range(TOP_K):
                pltpu.sync_copy(rows_v, xs_hbm.at[idx_v.at[k]])

    return scatter(dest_t, pk_p, pk_s)


def _fill_kernel(fill_ref, len_ref, nu_ref, xs_in_ref, xs_ref, zbuf, fill_sem, *, n_blocks):
    del xs_in_ref
    zbuf[...] = jnp.zeros_like(zbuf)

    def fill_copy(row0, size):
        return pltpu.make_async_copy(zbuf.at[pl.ds(0, size)], xs_ref.at[pl.ds(row0, size)], fill_sem)

    def on_padding(fn):
        def body(e, c):
            base, length = fill_ref[e], len_ref[e]
            size = EXPERT_ROWS // 2
            while size >= 1:
                piece = fill_copy(base + (length & ~(2 * size - 1)), size)
                pl.when((length & size) != 0)(functools.partial(fn, piece))
                size //= 2
            return c
        lax.fori_loop(0, N_EXPERTS, body, 0)

    def on_unused_blocks(fn):
        lax.fori_loop(nu_ref[0], n_blocks, lambda b, c: (fn(fill_copy(b * EXPERT_ROWS, EXPERT_ROWS)), c)[1], 0)

    on_padding(lambda cp: cp.start())
    on_unused_blocks(lambda cp: cp.start())
    on_padding(lambda cp: cp.wait())
    on_unused_blocks(lambda cp: cp.wait())


def _fill_padding(fill_start, fill_len, n_used, x_sorted):
    n_blocks = x_sorted.shape[0] // EXPERT_ROWS
    return pl.pallas_call(
        functools.partial(_fill_kernel, n_blocks=n_blocks),
        grid_spec=pltpu.PrefetchScalarGridSpec(
            num_scalar_prefetch=3,
            grid=(1,),
            in_specs=[pl.BlockSpec(memory_space=pl.ANY)],
            out_specs=pl.BlockSpec(memory_space=pl.ANY),
            scratch_shapes=[pltpu.VMEM((EXPERT_ROWS, WORD_SLABS, LANES), jnp.uint32), pltpu.SemaphoreType.DMA],
        ),
        out_shape=jax.ShapeDtypeStruct(x_sorted.shape, x_sorted.dtype),
        input_output_aliases={3: 0},
        compiler_params=_params(1),
        name="fill_padding",
    )(fill_start, fill_len, n_used, x_sorted)


def _residual_kernel(hp_ref, hs_ref, wg_ref, wu_ref, wd_ref, yp_ref, *, prompt_tiles):
    h = jnp.where(pl.program_id(0) < prompt_tiles, hp_ref[...], hs_ref[...])
    hb = h.astype(BF16)
    a = jnp.dot(hb, wg_ref[...], preferred_element_type=F32)
    u = jnp.dot(hb, wu_ref[...], preferred_element_type=F32)
    shared = jnp.dot((a * jax.nn.sigmoid(a) * u).astype(BF16), wd_ref[...], preferred_element_type=F32)
    yp_ref[...] = ALPHA * h + shared


def _residual(h_p, h_s, wg_b, wu_b, wd_b):
    t = ROW_TILE
    prompt_tiles = h_p.shape[0] // t
    sample_tiles = h_s.shape[0] // t
    fixed = lambda i: (0, 0)
    return pl.pallas_call(
        functools.partial(_residual_kernel, prompt_tiles=prompt_tiles),
        grid=(prompt_tiles + sample_tiles,),
        in_specs=[
            pl.BlockSpec((t, D_MODEL), lambda i: (jnp.minimum(i, prompt_tiles - 1), 0)),
            pl.BlockSpec((t, D_MODEL), lambda i: (jnp.maximum(i - prompt_tiles, 0), 0)),
            pl.BlockSpec((D_MODEL, F_EXPERT), fixed),
            pl.BlockSpec((D_MODEL, F_EXPERT), fixed),
            pl.BlockSpec((F_EXPERT, D_MODEL), fixed),
        ],
        out_specs=pl.BlockSpec((t, D_MODEL), lambda i: (i, 0)),
        out_shape=jax.ShapeDtypeStruct(((prompt_tiles + sample_tiles) * t, D_MODEL), F32),
        compiler_params=_params(1),
        name="residual",
    )(h_p, h_s, wg_b, wu_b, wd_b)


def _block_plan(counts):
    padded = (counts + EXPERT_ROWS - 1) // EXPERT_ROWS * EXPERT_ROWS
    pad_end = jnp.cumsum(padded).astype(jnp.int32)
    pad_start = pad_end - padded
    n_used = pad_end[-1] // EXPERT_ROWS
    fill_start = pad_start + counts
    fill_len = pad_end - fill_start
    first_block = pad_start // EXPERT_ROWS
    n_expert_blocks = (padded // EXPERT_ROWS).astype(jnp.int32)
    return pad_start, fill_start, fill_len, first_block, n_expert_blocks, n_used.reshape(1).astype(jnp.int32)


def kernel(x_prompt, x_sample, cache_k, cache_v, w_in, sink, gm_ln_g, gm_ln_b, gm_w_s, gm_b_s, w_out, ln1_g, ln1_b,
           router_w, router_bias, w_gate_e, w_up_e, w_down_e, w_gate_s, w_up_s, w_down_s, ln2_g, ln2_b):
    bp, sp = x_prompt.shape[:2]
    bs, ts = x_sample.shape[:2]
    r = cache_k.shape[2]
    assert r == WINDOW and sp % ROW_TILE == 0 and (bs * ts) % ROW_TILE == 0
    n_p, n_s = bp * sp, bs * ts
    n_total = n_p + n_s
    l = 0

    w_in_b = w_in[l].astype(BF16)
    w_out_b = w_out[l].astype(BF16)
    router_t = router_w[l].T
    r_hi = router_t.astype(BF16)
    r_lo = (router_t - r_hi.astype(F32)).astype(BF16)
    row_vec = lambda v: v.reshape(1, -1)
    gm_g, gm_b = row_vec(gm_ln_g[l]), row_vec(gm_ln_b[l])
    sink_l = sink[l].astype(F32)

    xp2 = x_prompt.reshape(n_p, D_MODEL)
    tabs_p = _rope_tables(jnp.arange(sp, dtype=jnp.int32))
    q, k, v, gu, gv = _in_proj(xp2, w_in_b, tabs_p, gm_g, gm_b, tm=ROW_TILE, gv_dtype=BF16, name="in_proj_prompt")
    nb = sp // WINDOW
    att = _attention(sink_l, q, k, v, k, v, batch=bp, nb=nb, tq=WINDOW, seqs=1, chain=PROMPT_BLOCKS_PER_STEP,
                     prev_blocks=nb,
                     first_block_has_no_prev=True, name="attn_prompt")
    b_tab_p = jnp.repeat(gm_b_s[l].T, HEAD_DIM, axis=1)
    gm = _gate(gu, gv, gm_w_s[l], b_tab_p, chunk=CHUNK, n_chunks=ROW_TILE // CHUNK, name="gate_prompt")
    last_rows = lambda t: t.reshape(bp, sp, KV_WIDTH)[:, sp - r:, :].reshape(1, bp, r, KV_HEADS, HEAD_DIM)
    new_kp, new_vp = last_rows(k), last_rows(v)
    h_p, pk_p, lt_p = _out_proj(att, gm, xp2, w_out_b, row_vec(ln1_g[l]), row_vec(ln1_b[l]), r_hi, r_lo,
                          name="out_proj_prompt")

    xs2 = x_sample.reshape(n_s, D_MODEL)
    pos_s = PAST_LEN + jnp.arange(ts, dtype=jnp.int32)
    tabs_s = tuple(jnp.tile(t, (bs, 1)) for t in _rope_tables(pos_s))
    q, k, v, gu, gv = _in_proj(xs2, w_in_b, tabs_s, gm_g, gm_b, tm=n_s, gv_dtype=F32, name="in_proj_sample")
    tq = SLABS
    pad_rows = lambda t: jnp.pad(t.reshape(bs, ts, -1), ((0, 0), (0, tq - ts), (0, 0))).reshape(bs * tq, -1)
    ck2 = cache_k[l].reshape(bs * r, KV_WIDTH)
    cv2 = cache_v[l].reshape(bs * r, KV_WIDTH)
    att = _attention(sink_l, pad_rows(q), pad_rows(k), pad_rows(v), ck2, cv2, batch=bs, nb=1, tq=tq,
                     seqs=SAMPLE_SEQS_PER_STEP, chain=1, prev_blocks=1, first_block_has_no_prev=False, name="attn_sample")
    att = att.reshape(bs, tq, ATT_WIDTH)[:, :ts].reshape(n_s, ATT_WIDTH)
    gm = _gate_short(gu, gv, gm_w_s[l], gm_b_s[l], length=ts, name="gate_sample")
    new_ks = jnp.concatenate([cache_k[l], k.reshape(bs, ts, KV_HEADS, HEAD_DIM)], axis=1)[:, ts:][None]
    new_vs = jnp.concatenate([cache_v[l], v.reshape(bs, ts, KV_HEADS, HEAD_DIM)], axis=1)[:, ts:][None]
    new_gs = gv.reshape(bs, ts, GM_WIDTH)[None]
    h_s, pk_s, lt_s = _out_proj(att, gm, xs2, w_out_b, row_vec(ln1_g[l]), row_vec(ln1_b[l]), r_hi, r_lo,
                          name="out_proj_sample")

    eidx_t, w_t, rank_t, counts = _route(lt_p, lt_s, router_bias[l].astype(F32).reshape(N_EXPERTS, 1))
    a = n_total * TOP_K
    n_blocks = -(-(a + N_EXPERTS * (EXPERT_ROWS - 1)) // EXPERT_ROWS)
    pad_start, fill_start, fill_len, first_block, n_expert_blocks, n_used = _block_plan(counts.reshape(N_EXPERTS))
    dest_t = _dest(eidx_t, rank_t, pad_start.astype(F32).reshape(N_EXPERTS, 1))
    tiles = lambda a: a.reshape(-1, WORD_SLABS, LANES)
    shared = (w_gate_s[l].astype(BF16), w_up_s[l].astype(BF16), w_down_s[l].astype(BF16))
    x_sorted = _dispatch_sc(dest_t, tiles(pk_p), tiles(pk_s), rows=n_blocks * EXPERT_ROWS)
    y_part = _residual(h_p, h_s, *shared)
    x_sorted = _fill_padding(fill_start, fill_len, n_used, x_sorted)
    out_sorted = _experts(first_block, n_expert_blocks, n_used, x_sorted, w_gate_e[l], w_up_e[l], w_down_e[l])
    ln2 = (row_vec(ln2_g[l]), row_vec(ln2_b[l]))
    g = _gather_sc(dest_t, out_sorted).reshape(TOP_K, n_total * WORD_SLABS, LANES)
    w2 = w_t.T
    y_p = _combine(g, w2, y_part, *ln2, n=n_p, row_offset=0, name="combine_prompt")
    y_s = _combine(g, w2, y_part, *ln2, n=n_s, row_offset=n_p, name="combine_sample")
    return (y_p.reshape(bp, sp, D_MODEL), y_s.reshape(bs, ts, D_MODEL), new_kp, new_vp, new_ks, new_vs, new_gs)
```

```python
import functools

import jax
import jax.numpy as jnp
import numpy as np
from jax import lax
from jax.experimental import pallas as pl
from jax.experimental.pallas import tpu as pltpu
from jax.experimental.pallas import tpu_sc as plsc

D_MODEL = 1024
HEAD_DIM = 64
ATT_HEADS = 8
KV_HEADS = 2
Q_PER_KV = ATT_HEADS // KV_HEADS
GM_HEADS = 8
ATT_WIDTH = ATT_HEADS * HEAD_DIM
KV_WIDTH = KV_HEADS * HEAD_DIM
GM_WIDTH = GM_HEADS * HEAD_DIM
ROPE_WIDTH = ATT_WIDTH + KV_WIDTH
IN_WIDTH = ATT_WIDTH + 2 * KV_WIDTH + 2 * GM_WIDTH
WINDOW = 128
CHUNK = 128
PAST_LEN = 16384
ROPE_THETA = 10000.0
ATT_SCALE = HEAD_DIM ** -0.5
N_EXPERTS = 256
TOP_K = 8
N_GROUPS = 8
TOPK_GROUPS = 4
F_EXPERT = 256
ROUTED_SCALE = 2.5
LN_EPS = 1e-5
DEPTH = 1
ALPHA = (2.0 * DEPTH) ** 0.25

LANES = 128
SLABS = 8
PACKED = D_MODEL // 2
WORD_SLABS = PACKED // LANES
ROW_TILE = 512
EXPERT_ROWS = 256
EXPERT_RING = 8
COMBINE_ROWS = 512
ROUTE_TILE = 512
SAMPLE_SEQS_PER_STEP = 16
PROMPT_BLOCKS_PER_STEP = 2
SC_CORES = 2
SC_SUBCORES = 16
SC_WINDOW = 128
VMEM_LIMIT = 56 * 1024 * 1024

F32 = jnp.float32
BF16 = jnp.bfloat16


def _params(n_axes):
    return pltpu.CompilerParams(dimension_semantics=("arbitrary",) * n_axes, vmem_limit_bytes=VMEM_LIMIT)


def _layer_norm(x, g, b):
    mu = jnp.mean(x, axis=-1, keepdims=True)
    xc = x - mu
    var = jnp.mean(xc * xc, axis=-1, keepdims=True)
    return xc * lax.rsqrt(var + LN_EPS) * g + b


def _pack_halves(x):
    xb = x.astype(BF16)
    lo = lax.bitcast_convert_type(xb[:, :PACKED].astype(F32), jnp.uint32)
    hi = lax.bitcast_convert_type(xb[:, PACKED:].astype(F32), jnp.uint32)
    return (lo >> 16) | hi


def _unpack_halves(w):
    lo = lax.bitcast_convert_type(w << 16, F32)
    hi = lax.bitcast_convert_type(w & jnp.uint32(0xFFFF0000), F32)
    return lo, hi


def _store_row_tiles(ref2d, x):
    m, slabs = x.shape[0], x.shape[1] // LANES
    for s in range(slabs):
        ref2d[pl.ds(s, m, stride=slabs), :] = x[:, s * LANES:(s + 1) * LANES]


def _load_row_tiles(ref2d, m, slabs):
    return jnp.concatenate([ref2d[pl.ds(s, m, stride=slabs), :] for s in range(slabs)], axis=1)


def _gelu(x):
    return 0.5 * x * (1.0 + lax.erf(x * np.float32(np.sqrt(0.5))))


def _in_proj_kernel(x_ref, w_ref, cos_ref, sa_ref, sb_ref, g_ref, b_ref, q_ref, k_ref, v_ref, gu_ref, gv_ref):
    x = x_ref[...].astype(BF16)
    zr = jnp.dot(x, w_ref[:, :ROPE_WIDTH], preferred_element_type=F32)
    pieces = []
    for c in range(ROPE_WIDTH // LANES):
        zc = zr[:, c * LANES:(c + 1) * LANES]
        tl = slice(0, LANES) if c < ATT_WIDTH // LANES else slice(LANES, 2 * LANES)
        pieces.append(zc * cos_ref[:, tl]
                      + pltpu.roll(zc, LANES - HEAD_DIM // 2, 1) * sa_ref[:, tl]
                      + pltpu.roll(zc, HEAD_DIM // 2, 1) * sb_ref[:, tl])
    for c in range(ATT_WIDTH // LANES):
        q_ref[:, c * LANES:(c + 1) * LANES] = pieces[c].astype(q_ref.dtype)
    k_ref[...] = pieces[ATT_WIDTH // LANES]
    v_ref[...] = jnp.dot(x, w_ref[:, ROPE_WIDTH:ROPE_WIDTH + KV_WIDTH], preferred_element_type=F32)
    g0 = ROPE_WIDTH + KV_WIDTH
    zu = jnp.dot(x, w_ref[:, g0:g0 + GM_WIDTH], preferred_element_type=F32)
    gu_ref[...] = _gelu(zu).astype(gu_ref.dtype)
    zv = jnp.dot(x, w_ref[:, g0 + GM_WIDTH:g0 + 2 * GM_WIDTH], preferred_element_type=F32)
    gv = _layer_norm(_gelu(zv), g_ref[...], b_ref[...])
    gv_ref[...] = gv.astype(gv_ref.dtype)


def _in_proj(x2, w_in_b, tabs, ln_g, ln_b, *, tm, gv_dtype, name):
    n = x2.shape[0]
    cos_t, sa_t, sb_t = tabs
    period = cos_t.shape[0] // tm
    row = lambda i: (i, 0)
    tab = lambda i: (i % period, 0)
    fixed = lambda i: (0, 0)
    return pl.pallas_call(
        _in_proj_kernel,
        grid=(n // tm,),
        in_specs=[
            pl.BlockSpec((tm, D_MODEL), row),
            pl.BlockSpec((D_MODEL, IN_WIDTH), fixed),
            pl.BlockSpec((tm, 2 * LANES), tab),
            pl.BlockSpec((tm, 2 * LANES), tab),
            pl.BlockSpec((tm, 2 * LANES), tab),
            pl.BlockSpec((1, GM_WIDTH), fixed),
            pl.BlockSpec((1, GM_WIDTH), fixed),
        ],
        out_specs=[
            pl.BlockSpec((tm, ATT_WIDTH), row),
            pl.BlockSpec((tm, KV_WIDTH), row),
            pl.BlockSpec((tm, KV_WIDTH), row),
            pl.BlockSpec((tm, GM_WIDTH), row),
            pl.BlockSpec((tm, GM_WIDTH), row),
        ],
        out_shape=[
            jax.ShapeDtypeStruct((n, ATT_WIDTH), BF16),
            jax.ShapeDtypeStruct((n, KV_WIDTH), F32),
            jax.ShapeDtypeStruct((n, KV_WIDTH), F32),
            jax.ShapeDtypeStruct((n, GM_WIDTH), BF16),
            jax.ShapeDtypeStruct((n, GM_WIDTH), gv_dtype),
        ],
        compiler_params=_params(1),
        name=name,
    )(x2, w_in_b, cos_t, sa_t, sb_t, ln_g, ln_b)


def _rope_tables(pos):
    half = HEAD_DIM // 2
    lane = jnp.arange(2 * LANES, dtype=jnp.int32)
    inv = ROPE_THETA ** (-(lane % half).astype(F32) * 2.0 / HEAD_DIM)
    ang = pos.astype(F32)[:, None] * inv[None, :]
    scale = jnp.where(lane < LANES, ATT_SCALE, 1.0).astype(F32)[None, :]
    first_half = ((lane % HEAD_DIM) < half)[None, :]
    cos, sin = jnp.cos(ang) * scale, jnp.sin(ang) * scale
    return cos, jnp.where(first_half, -sin, 0.0), jnp.where(first_half, 0.0, sin)


def _attn_kernel(sink_ref, q_ref, kc_ref, vc_ref, kp_ref, vp_ref, o_ref, *, tq, seqs, chain, stack,
                 first_block_has_no_prev):
    nk = WINDOW + tq
    rows = stack * tq
    qi = lax.broadcasted_iota(jnp.int32, (rows, nk), 0) & (tq - 1)
    ks = lax.broadcasted_iota(jnp.int32, (rows, nk), 1)
    band = (ks >= qi) & (ks <= qi + WINDOW)
    first_mask = band & ((pl.program_id(1) > 0) | (ks >= WINDOW)) if first_block_has_no_prev else band
    sinks = [jnp.concatenate([jnp.full((tq, 1), sink_ref[h0 + j], F32) for j in range(stack)], axis=0)
             for h0 in range(0, ATT_HEADS, stack)]
    for b in range(seqs * chain):
        qrows = slice(b * tq, (b + 1) * tq)
        if chain > 1 and b > 0:
            kprev, vprev = kc_ref[(b - 1) * tq:b * tq, :], vc_ref[(b - 1) * tq:b * tq, :]
        else:
            kprev, vprev = kp_ref[b * WINDOW:(b + 1) * WINDOW, :], vp_ref[b * WINDOW:(b + 1) * WINDOW, :]
        mask = band if (chain > 1 and b > 0) else first_mask
        q = q_ref[qrows, :]
        kk = jnp.concatenate([kprev, kc_ref[qrows, :]], axis=0).astype(BF16)
        vv = jnp.concatenate([vprev, vc_ref[qrows, :]], axis=0).astype(BF16)
        outs = []
        for i, h0 in enumerate(range(0, ATT_HEADS, stack)):
            g = h0 // Q_PER_KV
            kg = kk[:, g * HEAD_DIM:(g + 1) * HEAD_DIM]
            vg = vv[:, g * HEAD_DIM:(g + 1) * HEAD_DIM]
            qg = jnp.concatenate([q[:, (h0 + j) * HEAD_DIM:(h0 + j + 1) * HEAD_DIM] for j in range(stack)], axis=0)
            s = lax.dot_general(qg, kg, (((1,), (1,)), ((), ())), preferred_element_type=F32)
            s = jnp.where(mask, s, -jnp.inf)
            m = jnp.maximum(jnp.max(s, axis=-1, keepdims=True), sinks[i])
            p = jnp.exp(s - m)
            denom = jnp.sum(p, axis=-1, keepdims=True) + jnp.exp(sinks[i] - m)
            o = jnp.dot((p / denom).astype(BF16), vg, preferred_element_type=F32)
            outs.extend(o[j * tq:(j + 1) * tq, :] for j in range(stack))
        o_ref[qrows, :] = jnp.concatenate(outs, axis=1).astype(o_ref.dtype)


def _attention(sink, q2, k2, v2, kprev2, vprev2, *, batch, nb, tq, seqs, chain, prev_blocks, first_block_has_no_prev,
               name):
    assert tq & (tq - 1) == 0 and batch % seqs == 0 and (seqs == 1 or nb == prev_blocks == 1)
    assert (seqs == 1 or chain == 1) and nb % chain == 0 and (chain == 1 or tq == WINDOW)
    steps = nb // chain
    cur = lambda b, n, s: (b * steps + n, 0)
    prev = lambda b, n, s: (b * prev_blocks + jnp.maximum(n * chain - 1, 0), 0)
    stack = Q_PER_KV if Q_PER_KV * tq <= WINDOW else 1
    kern = functools.partial(_attn_kernel, tq=tq, seqs=seqs, chain=chain, stack=stack,
                             first_block_has_no_prev=first_block_has_no_prev)
    return pl.pallas_call(
        kern,
        grid_spec=pltpu.PrefetchScalarGridSpec(
            num_scalar_prefetch=1,
            grid=(batch // seqs, steps),
            in_specs=[
                pl.BlockSpec((seqs * chain * tq, ATT_WIDTH), cur),
                pl.BlockSpec((seqs * chain * tq, KV_WIDTH), cur),
                pl.BlockSpec((seqs * chain * tq, KV_WIDTH), cur),
                pl.BlockSpec((seqs * WINDOW, KV_WIDTH), prev),
                pl.BlockSpec((seqs * WINDOW, KV_WIDTH), prev),
            ],
            out_specs=pl.BlockSpec((seqs * chain * tq, ATT_WIDTH), cur),
        ),
        out_shape=jax.ShapeDtypeStruct(q2.shape, BF16),
        compiler_params=_params(2),
        name=name,
    )(sink, q2, k2, v2, kprev2, vprev2)


def _gate_kernel(gu_ref, gv_ref, w_ref, b_ref, o_ref, *, chunk, n_chunks):
    ri = lax.broadcasted_iota(jnp.int32, (chunk, chunk), 0)
    ci = lax.broadcasted_iota(jnp.int32, (chunk, chunk), 1)
    ws = [jnp.where(ci <= ri, w_ref[h], 0.0).astype(BF16) for h in range(GM_HEADS)]
    for c in range(n_chunks):
        rows = slice(c * chunk, (c + 1) * chunk)
        gv = gv_ref[rows, :].astype(BF16)
        sv = jnp.concatenate(
            [jnp.dot(ws[h], gv[:, h * HEAD_DIM:(h + 1) * HEAD_DIM], preferred_element_type=F32)
             for h in range(GM_HEADS)], axis=1)
        o_ref[rows, :] = (gu_ref[rows, :].astype(F32) * (sv + b_ref[...])).astype(o_ref.dtype)


def _gate(gu2, gv2, w_s, b_tab, *, chunk, n_chunks, name):
    n = gu2.shape[0]
    tm = chunk * n_chunks
    row = lambda i: (i, 0)
    kern = functools.partial(_gate_kernel, chunk=chunk, n_chunks=n_chunks)
    return pl.pallas_call(
        kern,
        grid=(n // tm,),
        in_specs=[
            pl.BlockSpec((tm, GM_WIDTH), row),
            pl.BlockSpec((tm, GM_WIDTH), row),
            pl.BlockSpec((GM_HEADS, chunk, chunk), lambda i: (0, 0, 0)),
            pl.BlockSpec((chunk, GM_WIDTH), lambda i: (0, 0)),
        ],
        out_specs=pl.BlockSpec((tm, GM_WIDTH), row),
        out_shape=jax.ShapeDtypeStruct((n, GM_WIDTH), BF16),
        compiler_params=_params(1),
        name=name,
    )(gu2, gv2, w_s, b_tab)


def _gate_short_kernel(gu_ref, gv_ref, coef_ref, b_ref, o_ref, *, length):
    n = gv_ref.shape[0]
    gv = gv_ref[...].astype(F32)
    tiles = lambda x: x.reshape(n // SLABS, SLABS, GM_WIDTH)
    sv = tiles(gv) * coef_ref[0][None] + b_ref[...][None]
    for d in range(1, length):
        sv = sv + tiles(pltpu.roll(gv, d, 0)) * coef_ref[d][None]
    o_ref[...] = (gu_ref[...].astype(F32) * sv.reshape(n, GM_WIDTH)).astype(o_ref.dtype)


def _gate_short(gu2, gv2, w_s, b_s, *, length, name):
    n = gu2.shape[0]
    assert SLABS % length == 0 and n % SLABS == 0
    pos = jnp.arange(length)
    lag = jnp.arange(length)
    src = pos[None, :] - lag[:, None]
    coef = jnp.where(src >= 0, w_s[:, pos[None, :], jnp.maximum(src, 0)], 0.0)
    coef = jnp.repeat(coef.transpose(1, 2, 0), HEAD_DIM, axis=2)
    coef = jnp.tile(coef, (1, SLABS // length, 1))
    bias = jnp.tile(jnp.repeat(b_s[:, :length].T, HEAD_DIM, axis=1), (SLABS // length, 1))
    whole = lambda shape: pl.BlockSpec(shape, lambda i: (0,) * len(shape))
    return pl.pallas_call(
        functools.partial(_gate_short_kernel, length=length),
        grid=(1,),
        in_specs=[whole((n, GM_WIDTH)), whole((n, GM_WIDTH)), whole((length, SLABS, GM_WIDTH)),
                  whole((SLABS, GM_WIDTH))],
        out_specs=whole((n, GM_WIDTH)),
        out_shape=jax.ShapeDtypeStruct((n, GM_WIDTH), BF16),
        compiler_params=_params(1),
        name=name,
    )(gu2, gv2, coef, bias)


def _out_proj_kernel(att_ref, gm_ref, x_ref, wo_ref, g_ref, b_ref, rh_ref, rl_ref, wg_ref, wu_ref, wd_ref, yp_ref,
                     hp_ref, lg_ref):
    mix = jnp.dot(att_ref[...], wo_ref[:ATT_WIDTH, :], preferred_element_type=F32)
    mix = mix + jnp.dot(gm_ref[...], wo_ref[ATT_WIDTH:, :], preferred_element_type=F32)
    h = _layer_norm(ALPHA * x_ref[...] + mix, g_ref[...], b_ref[...])
    _store_row_tiles(hp_ref, _pack_halves(h))
    h_hi = h.astype(BF16)
    a = jnp.dot(h_hi, wg_ref[...], preferred_element_type=F32)
    u = jnp.dot(h_hi, wu_ref[...], preferred_element_type=F32)
    shared = jnp.dot((a * jax.nn.sigmoid(a) * u).astype(BF16), wd_ref[...], preferred_element_type=F32)
    yp_ref[...] = ALPHA * h + shared
    h_lo = (h - h_hi.astype(F32)).astype(BF16)
    nt = (((1,), (1,)), ((), ()))
    lg = lax.dot_general(rh_ref[...], h_hi, nt, preferred_element_type=F32)
    lg = lg + lax.dot_general(rh_ref[...], h_lo, nt, preferred_element_type=F32)
    lg = lg + lax.dot_general(rl_ref[...], h_hi, nt, preferred_element_type=F32)
    lg_ref[...] = lg


def _out_proj(att2, gm2, x2, w_out_b, ln_g, ln_b, r_hi, r_lo, wg_b, wu_b, wd_b, *, name):
    n = x2.shape[0]
    tm = ROW_TILE
    row = lambda i: (i, 0)
    fixed = lambda i: (0, 0)
    return pl.pallas_call(
        _out_proj_kernel,
        grid=(n // tm,),
        in_specs=[
            pl.BlockSpec((tm, ATT_WIDTH), row),
            pl.BlockSpec((tm, GM_WIDTH), row),
            pl.BlockSpec((tm, D_MODEL), row),
            pl.BlockSpec((D_MODEL, D_MODEL), fixed),
            pl.BlockSpec((1, D_MODEL), fixed),
            pl.BlockSpec((1, D_MODEL), fixed),
            pl.BlockSpec((N_EXPERTS, D_MODEL), fixed),
            pl.BlockSpec((N_EXPERTS, D_MODEL), fixed),
            pl.BlockSpec((D_MODEL, F_EXPERT), fixed),
            pl.BlockSpec((D_MODEL, F_EXPERT), fixed),
            pl.BlockSpec((F_EXPERT, D_MODEL), fixed),
        ],
        out_specs=[
            pl.BlockSpec((tm, D_MODEL), row),
            pl.BlockSpec((tm * WORD_SLABS, LANES), row),
            pl.BlockSpec((N_EXPERTS, tm), lambda i: (0, i)),
        ],
        out_shape=[
            jax.ShapeDtypeStruct((n, D_MODEL), F32),
            jax.ShapeDtypeStruct((n * WORD_SLABS, LANES), jnp.uint32),
            jax.ShapeDtypeStruct((N_EXPERTS, n), F32),
        ],
        compiler_params=_params(1),
        name=name,
    )(att2, gm2, x2, w_out_b, ln_g, ln_b, r_hi, r_lo, wg_b, wu_b, wd_b)


def _experts_kernel(first_ref, nblk_ref, nu_ref, x_hbm, wg_ref, wu_ref, wd_ref, o_hbm, xbuf, obuf, wg_s, wu_s, wd_s,
                    in_sem, out_sem, fill_sem, *, n_blocks):
    e = pl.program_id(0)
    nb = nblk_ref[e]
    b0 = first_ref[e]
    n_used = nu_ref[0]
    ahead = EXPERT_RING // 2

    def rows_of(block):
        return pl.ds(pl.multiple_of(block * EXPERT_ROWS, EXPERT_ROWS), EXPERT_ROWS)

    def slot_of(block):
        return block & (EXPERT_RING - 1)

    def slab(s):
        return pl.ds(s * LANES, LANES)

    def in_copies(block):
        slot = slot_of(block)
        return [pltpu.make_async_copy(x_hbm.at[rows_of(block), s, :], xbuf.at[slot, :, slab(s)], in_sem.at[slot])
                for s in range(WORD_SLABS)]

    def out_copies(block, slot=None, sem=None):
        slot = slot_of(block) if slot is None else slot
        sem = out_sem.at[slot] if sem is None else sem
        return [pltpu.make_async_copy(obuf.at[slot, :, slab(s)], o_hbm.at[rows_of(block), s, :], sem)
                for s in range(WORD_SLABS)]

    def start(copies):
        for cp in copies:
            cp.start()

    def wait(copies):
        for cp in copies:
            cp.wait()

    @pl.when(e == 0)
    def _():
        for j in range(ahead):
            pl.when(j < n_used)(lambda j=j: start(in_copies(j)))

    def process(block, count):
        blocks = [block + j for j in range(count)]
        for blk in blocks:
            wait(in_copies(blk))
        x = [xbuf[slot_of(blk)] for blk in blocks]
        xl, xh = _unpack_halves(x[0] if count == 1 else jnp.concatenate(x, axis=0))
        xl, xh = xl.astype(BF16), xh.astype(BF16)
        for blk in blocks:
            pl.when(blk + ahead < n_used)(lambda blk=blk: start(in_copies(blk + ahead)))
            pl.when(blk >= ahead)(lambda blk=blk: wait(out_copies(blk - ahead)))
        a = (jnp.dot(xl, wg_s[:PACKED, :], preferred_element_type=F32)
             + jnp.dot(xh, wg_s[PACKED:, :], preferred_element_type=F32))
        u = (jnp.dot(xl, wu_s[:PACKED, :], preferred_element_type=F32)
             + jnp.dot(xh, wu_s[PACKED:, :], preferred_element_type=F32))
        hb = (a * jax.nn.sigmoid(a) * u).astype(BF16)
        o = _pack_halves(jnp.dot(hb, wd_s[...], preferred_element_type=F32))
        for j, blk in enumerate(blocks):
            obuf[slot_of(blk)] = o[j * EXPERT_ROWS:(j + 1) * EXPERT_ROWS, :]
            start(out_copies(blk))

    @pl.when(nb > 0)
    def _():
        wg_s[...] = wg_ref[0].astype(BF16)
        wu_s[...] = wu_ref[0].astype(BF16)
        wd_s[...] = wd_ref[0].astype(BF16)

        def pair(i, carry):
            process(b0 + 2 * i, 2)
            return carry

        lax.fori_loop(0, nb // 2, pair, 0)
        pl.when(nb % 2 == 1)(lambda: process(b0 + nb - 1, 1))

    @pl.when(e == N_EXPERTS - 1)
    def _():
        for j in range(ahead):
            pl.when(n_used - 1 - j >= 0)(lambda j=j: wait(out_copies(n_used - 1 - j)))
        obuf[0] = jnp.zeros((EXPERT_ROWS, PACKED), jnp.uint32)

        def on_unused_blocks(fn):
            def body(b, c):
                fn(out_copies(b, slot=0, sem=fill_sem))
                return c
            lax.fori_loop(n_used, n_blocks, body, 0)

        on_unused_blocks(start)
        on_unused_blocks(wait)


def _experts(first_block, n_expert_blocks, n_used, x_sorted, w_gate_e, w_up_e, w_down_e):
    rows = x_sorted.shape[0]
    wmap = lambda e, *_: (e, 0, 0)
    kern = functools.partial(_experts_kernel, n_blocks=rows // EXPERT_ROWS)
    return pl.pallas_call(
        kern,
        grid_spec=pltpu.PrefetchScalarGridSpec(
            num_scalar_prefetch=3,
            grid=(N_EXPERTS,),
            in_specs=[
                pl.BlockSpec(memory_space=pl.ANY),
                pl.BlockSpec((1, D_MODEL, F_EXPERT), wmap),
                pl.BlockSpec((1, D_MODEL, F_EXPERT), wmap),
                pl.BlockSpec((1, F_EXPERT, D_MODEL), wmap),
            ],
            out_specs=pl.BlockSpec(memory_space=pl.ANY),
            scratch_shapes=[
                pltpu.VMEM((EXPERT_RING, EXPERT_ROWS, PACKED), jnp.uint32),
                pltpu.VMEM((EXPERT_RING, EXPERT_ROWS, PACKED), jnp.uint32),
                pltpu.VMEM((D_MODEL, F_EXPERT), BF16),
                pltpu.VMEM((D_MODEL, F_EXPERT), BF16),
                pltpu.VMEM((F_EXPERT, D_MODEL), BF16),
                pltpu.SemaphoreType.DMA((EXPERT_RING,)),
                pltpu.SemaphoreType.DMA((EXPERT_RING,)),
                pltpu.SemaphoreType.DMA,
            ],
        ),
        out_shape=jax.ShapeDtypeStruct(x_sorted.shape, x_sorted.dtype),
        compiler_params=_params(1),
        name="experts",
    )(first_block, n_expert_blocks, n_used, x_sorted, w_gate_e, w_up_e, w_down_e)


def _gather_sc(dest_t, out_sorted):
    win = SC_WINDOW
    n = dest_t.shape[1]
    n_win = n // win
    workers = SC_CORES * SC_SUBCORES
    mesh = plsc.VectorSubcoreMesh(core_axis_name="c", subcore_axis_name="s")

    @functools.partial(
        pl.kernel, mesh=mesh, name="gather_sc",
        out_type=jax.ShapeDtypeStruct((TOP_K, n, WORD_SLABS, LANES), jnp.uint32),
        scratch_types=[pltpu.VMEM((TOP_K, win), jnp.int32), pltpu.VMEM((win, WORD_SLABS, LANES), jnp.uint32)])
    def gather(dest_hbm, os_hbm, g_hbm, idx_v, rows_v):
        wid = lax.axis_index("s") * SC_CORES + lax.axis_index("c")

        @pl.loop(0, (n_win - wid + workers - 1) // workers)
        def _(i):
            base = pl.multiple_of((wid + i * workers) * win, win)
            pltpu.sync_copy(dest_hbm.at[:, pl.ds(base, win)], idx_v)
            for k in range(TOP_K):
                pltpu.sync_copy(os_hbm.at[idx_v.at[k]], rows_v)
                pltpu.sync_copy(rows_v, g_hbm.at[k, pl.ds(base, win)])

    return gather(dest_t, out_sorted)


def _combine_kernel(g_ref, w_ref, yp_ref, ln_g_ref, ln_b_ref, y_ref):
    t = y_ref.shape[0]
    w = w_ref[...]
    lo_acc = jnp.zeros((t, PACKED), F32)
    hi_acc = jnp.zeros((t, PACKED), F32)
    for k in range(TOP_K):
        lo, hi = _unpack_halves(_load_row_tiles(g_ref.at[k], t, WORD_SLABS))
        lo_acc = lo_acc + w[:, k:k + 1] * lo
        hi_acc = hi_acc + w[:, k:k + 1] * hi
    routed = jnp.concatenate([lo_acc, hi_acc], axis=1)
    y_ref[...] = _layer_norm(yp_ref[...] + routed, ln_g_ref[...], ln_b_ref[...])


def _combine(g, w2, y_part, ln_g, ln_b, *, n, row_offset, name):
    tm = COMBINE_ROWS
    off = row_offset // tm
    fixed = lambda i: (0, 0)
    return pl.pallas_call(
        _combine_kernel,
        grid=(n // tm,),
        in_specs=[
            pl.BlockSpec((TOP_K, tm * WORD_SLABS, LANES), lambda i: (0, i + off, 0)),
            pl.BlockSpec((tm, TOP_K), lambda i: (i + off, 0)),
            pl.BlockSpec((tm, D_MODEL), lambda i: (i, 0)),
            pl.BlockSpec((1, D_MODEL), fixed),
            pl.BlockSpec((1, D_MODEL), fixed),
        ],
        out_specs=pl.BlockSpec((tm, D_MODEL), lambda i: (i, 0)),
        out_shape=jax.ShapeDtypeStruct((n, D_MODEL), F32),
        compiler_params=_params(1),
        name=name,
    )(g, w2, y_part, ln_g, ln_b)


def _route_kernel(lgp_ref, lgs_ref, bias_ref, eidx_ref, w_ref, rank_ref, cnt_ref, carry_ref, *, prompt_tiles):
    @pl.when(pl.program_id(0) == 0)
    def _():
        carry_ref[...] = jnp.zeros_like(carry_ref)

    t = lgp_ref.shape[1]
    gsz = N_EXPERTS // N_GROUPS
    neg = -jnp.inf
    s = jax.nn.sigmoid(jnp.where(pl.program_id(0) < prompt_tiles, lgp_ref[...], lgs_ref[...]))
    biased = s + bias_ref[...]
    io_g = lax.broadcasted_iota(jnp.int32, (gsz, t), 0)
    grp_rows = []
    for g in range(N_GROUPS):
        blk = biased[g * gsz:(g + 1) * gsz, :]
        m1 = jnp.max(blk, axis=0, keepdims=True)
        i1 = jnp.min(jnp.where(blk == m1, io_g, gsz), axis=0, keepdims=True)
        m2 = jnp.max(jnp.where(io_g == i1, neg, blk), axis=0, keepdims=True)
        grp_rows.append(m1 + m2)
    gs = jnp.concatenate(grp_rows, axis=0)
    io8 = lax.broadcasted_iota(jnp.int32, (N_GROUPS, t), 0)
    gsel = jnp.zeros((N_GROUPS, t), jnp.int32)
    for _ in range(TOPK_GROUPS):
        m = jnp.max(gs, axis=0, keepdims=True)
        gi = jnp.min(jnp.where(gs == m, io8, N_GROUPS), axis=0, keepdims=True)
        hit = io8 == gi
        gsel = jnp.where(hit, 1, gsel)
        gs = jnp.where(hit, neg, gs)
    masked = jnp.concatenate(
        [jnp.where(gsel[g:g + 1, :] > 0, biased[g * gsz:(g + 1) * gsz, :], neg) for g in range(N_GROUPS)], axis=0)

    eio = lax.broadcasted_iota(jnp.int32, (N_EXPERTS, t), 0)
    cur = masked
    idx_rows, w_rows = [], []
    for _ in range(TOP_K):
        m = jnp.max(cur, axis=0, keepdims=True)
        idx = jnp.min(jnp.where(cur == m, eio, N_EXPERTS), axis=0, keepdims=True)
        hit = eio == idx
        w_rows.append(jnp.sum(jnp.where(hit, s, 0.0), axis=0, keepdims=True))
        cur = jnp.where(hit, neg, cur)
        idx_rows.append(idx)
    sel = jnp.where(cur != masked, 1.0, 0.0)

    tri = jnp.where(lax.broadcasted_iota(jnp.int32, (t, t), 0) < lax.broadcasted_iota(jnp.int32, (t, t), 1), 1.0, 0.0)
    pref = jnp.dot(sel.astype(BF16), tri.astype(BF16), preferred_element_type=F32) + carry_ref[...]
    rank_rows = [jnp.sum(jnp.where(eio == idx_rows[k], pref, 0.0), axis=0, keepdims=True) for k in range(TOP_K)]
    carry_ref[...] += jnp.sum(sel, axis=1, keepdims=True)

    wk = jnp.concatenate(w_rows, axis=0)
    eidx_ref[...] = jnp.concatenate(idx_rows, axis=0)
    w_ref[...] = wk / jnp.sum(wk, axis=0, keepdims=True) * ROUTED_SCALE
    rank_ref[...] = jnp.concatenate(rank_rows, axis=0).astype(jnp.int32)
    cnt_ref[...] = carry_ref[...].astype(jnp.int32)


def _route(logits_p, logits_s, bias_col):
    t = ROUTE_TILE
    prompt_tiles = logits_p.shape[1] // t
    n = logits_p.shape[1] + logits_s.shape[1]
    col = lambda i: (0, i)
    fixed = lambda i: (0, 0)
    kern = functools.partial(_route_kernel, prompt_tiles=prompt_tiles)
    return pl.pallas_call(
        kern,
        grid=(n // t,),
        in_specs=[pl.BlockSpec((N_EXPERTS, t), lambda i: (0, jnp.minimum(i, prompt_tiles - 1))),
                  pl.BlockSpec((N_EXPERTS, t), lambda i: (0, jnp.maximum(i - prompt_tiles, 0))),
                  pl.BlockSpec((N_EXPERTS, 1), fixed)],
        out_specs=[
            pl.BlockSpec((TOP_K, t), col),
            pl.BlockSpec((TOP_K, t), col),
            pl.BlockSpec((TOP_K, t), col),
            pl.BlockSpec((N_EXPERTS, 1), fixed),
        ],
        out_shape=[
            jax.ShapeDtypeStruct((TOP_K, n), jnp.int32),
            jax.ShapeDtypeStruct((TOP_K, n), F32),
            jax.ShapeDtypeStruct((TOP_K, n), jnp.int32),
            jax.ShapeDtypeStruct((N_EXPERTS, 1), jnp.int32),
        ],
        scratch_shapes=[pltpu.VMEM((N_EXPERTS, 1), F32)],
        compiler_params=_params(1),
        name="route",
    )(logits_p, logits_s, bias_col)


def _dest_kernel(eidx_ref, rank_ref, start_ref, dest_ref):
    t = eidx_ref.shape[1]
    eio = lax.broadcasted_iota(jnp.int32, (N_EXPERTS, t), 0)
    start = start_ref[...]
    rows = [jnp.sum(jnp.where(eio == eidx_ref[k:k + 1, :], start, 0.0), axis=0, keepdims=True) for k in range(TOP_K)]
    dest_ref[...] = jnp.concatenate(rows, axis=0).astype(jnp.int32) + rank_ref[...]


def _dest(eidx_t, rank_t, pad_start_col):
    n = eidx_t.shape[1]
    t = ROW_TILE
    col = lambda i: (0, i)
    return pl.pallas_call(
        _dest_kernel,
        grid=(n // t,),
        in_specs=[pl.BlockSpec((TOP_K, t), col), pl.BlockSpec((TOP_K, t), col),
                  pl.BlockSpec((N_EXPERTS, 1), lambda i: (0, 0))],
        out_specs=pl.BlockSpec((TOP_K, t), col),
        out_shape=jax.ShapeDtypeStruct((TOP_K, n), jnp.int32),
        compiler_params=_params(1),
        name="dest",
    )(eidx_t, rank_t, pad_start_col)


def _dispatch_sc(dest_t, pk_p, pk_s, *, rows):
    win = SC_WINDOW
    p_win, s_win = pk_p.shape[0] // win, pk_s.shape[0] // win
    n_win = p_win + s_win
    workers = SC_CORES * SC_SUBCORES
    mesh = plsc.VectorSubcoreMesh(core_axis_name="c", subcore_axis_name="s")

    @functools.partial(
        pl.kernel, mesh=mesh, name="dispatch_sc",
        out_type=jax.ShapeDtypeStruct((rows, WORD_SLABS, LANES), jnp.uint32),
        scratch_types=[pltpu.VMEM((TOP_K, win), jnp.int32), pltpu.VMEM((win, WORD_SLABS, LANES), jnp.uint32)])
    def scatter(dest_hbm, pkp_hbm, pks_hbm, xs_hbm, idx_v, rows_v):
        wid = lax.axis_index("s") * SC_CORES + lax.axis_index("c")

        @pl.loop(0, (n_win - wid + workers - 1) // workers)
        def _(i):
            g = wid + i * workers

            @pl.when(g < p_win)
            def _():
                pltpu.sync_copy(pkp_hbm.at[pl.ds(pl.multiple_of(g * win, win), win)], rows_v)

            @pl.when(g >= p_win)
            def _():
                pltpu.sync_copy(pks_hbm.at[pl.ds(pl.multiple_of((g - p_win) * win, win), win)], rows_v)

            pltpu.sync_copy(dest_hbm.at[:, pl.ds(pl.multiple_of(g * win, win), win)], idx_v)
            for k in range(TOP_K):
                pltpu.sync_copy(rows_v, xs_hbm.at[idx_v.at[k]])

    return scatter(dest_t, pk_p, pk_s)


def _fill_kernel(fill_ref, len_ref, nu_ref, xs_in_ref, xs_ref, zbuf, fill_sem, *, n_blocks):
    del xs_in_ref
    zbuf[...] = jnp.zeros_like(zbuf)

    def fill_copy(row0, size):
        return pltpu.make_async_copy(zbuf.at[pl.ds(0, size)], xs_ref.at[pl.ds(row0, size)], fill_sem)

    def on_padding(fn):
        def body(e, c):
            base, length = fill_ref[e], len_ref[e]
            size = EXPERT_ROWS // 2
            while size >= 1:
                piece = fill_copy(base + (length & ~(2 * size - 1)), size)
                pl.when((length & size) != 0)(functools.partial(fn, piece))
                size //= 2
            return c
        lax.fori_loop(0, N_EXPERTS, body, 0)

    def on_unused_blocks(fn):
        lax.fori_loop(nu_ref[0], n_blocks, lambda b, c: (fn(fill_copy(b * EXPERT_ROWS, EXPERT_ROWS)), c)[1], 0)

    on_padding(lambda cp: cp.start())
    on_unused_blocks(lambda cp: cp.start())
    on_padding(lambda cp: cp.wait())
    on_unused_blocks(lambda cp: cp.wait())


def _fill_padding(fill_start, fill_len, n_used, x_sorted):
    n_blocks = x_sorted.shape[0] // EXPERT_ROWS
    return pl.pallas_call(
        functools.partial(_fill_kernel, n_blocks=n_blocks),
        grid_spec=pltpu.PrefetchScalarGridSpec(
            num_scalar_prefetch=3,
            grid=(1,),
            in_specs=[pl.BlockSpec(memory_space=pl.ANY)],
            out_specs=pl.BlockSpec(memory_space=pl.ANY),
            scratch_shapes=[pltpu.VMEM((EXPERT_ROWS, WORD_SLABS, LANES), jnp.uint32), pltpu.SemaphoreType.DMA],
        ),
        out_shape=jax.ShapeDtypeStruct(x_sorted.shape, x_sorted.dtype),
        input_output_aliases={3: 0},
        compiler_params=_params(1),
        name="fill_padding",
    )(fill_start, fill_len, n_used, x_sorted)


def _block_plan(counts):
    padded = (counts + EXPERT_ROWS - 1) // EXPERT_ROWS * EXPERT_ROWS
    pad_end = jnp.cumsum(padded).astype(jnp.int32)
    pad_start = pad_end - padded
    n_used = pad_end[-1] // EXPERT_ROWS
    fill_start = pad_start + counts
    fill_len = pad_end - fill_start
    first_block = pad_start // EXPERT_ROWS
    n_expert_blocks = (padded // EXPERT_ROWS).astype(jnp.int32)
    return pad_start, fill_start, fill_len, first_block, n_expert_blocks, n_used.reshape(1).astype(jnp.int32)


def kernel(x_prompt, x_sample, cache_k, cache_v, w_in, sink, gm_ln_g, gm_ln_b, gm_w_s, gm_b_s, w_out, ln1_g, ln1_b,
           router_w, router_bias, w_gate_e, w_up_e, w_down_e, w_gate_s, w_up_s, w_down_s, ln2_g, ln2_b):
    bp, sp = x_prompt.shape[:2]
    bs, ts = x_sample.shape[:2]
    r = cache_k.shape[2]
    assert r == WINDOW and sp % ROW_TILE == 0 and (bs * ts) % ROW_TILE == 0
    n_p, n_s = bp * sp, bs * ts
    n_total = n_p + n_s
    l = 0

    w_in_b = w_in[l].astype(BF16)
    w_out_b = w_out[l].astype(BF16)
    router_t = router_w[l].T
    r_hi = router_t.astype(BF16)
    r_lo = (router_t - r_hi.astype(F32)).astype(BF16)
    row_vec = lambda v: v.reshape(1, -1)
    gm_g, gm_b = row_vec(gm_ln_g[l]), row_vec(gm_ln_b[l])
    sink_l = sink[l].astype(F32)
    shared = (w_gate_s[l].astype(BF16), w_up_s[l].astype(BF16), w_down_s[l].astype(BF16))

    xp2 = x_prompt.reshape(n_p, D_MODEL)
    tabs_p = _rope_tables(jnp.arange(sp, dtype=jnp.int32))
    q, k, v, gu, gv = _in_proj(xp2, w_in_b, tabs_p, gm_g, gm_b, tm=ROW_TILE, gv_dtype=BF16, name="in_proj_prompt")
    nb = sp // WINDOW
    att = _attention(sink_l, q, k, v, k, v, batch=bp, nb=nb, tq=WINDOW, seqs=1, chain=PROMPT_BLOCKS_PER_STEP,
                     prev_blocks=nb,
                     first_block_has_no_prev=True, name="attn_prompt")
    b_tab_p = jnp.repeat(gm_b_s[l].T, HEAD_DIM, axis=1)
    gm = _gate(gu, gv, gm_w_s[l], b_tab_p, chunk=CHUNK, n_chunks=ROW_TILE // CHUNK, name="gate_prompt")
    last_rows = lambda t: t.reshape(bp, sp, KV_WIDTH)[:, sp - r:, :].reshape(1, bp, r, KV_HEADS, HEAD_DIM)
    new_kp, new_vp = last_rows(k), last_rows(v)
    yp_p, pk_p, lt_p = _out_proj(att, gm, xp2, w_out_b, row_vec(ln1_g[l]), row_vec(ln1_b[l]), r_hi, r_lo, *shared,
                          name="out_proj_prompt")

    xs2 = x_sample.reshape(n_s, D_MODEL)
    pos_s = PAST_LEN + jnp.arange(ts, dtype=jnp.int32)
    tabs_s = tuple(jnp.tile(t, (bs, 1)) for t in _rope_tables(pos_s))
    q, k, v, gu, gv = _in_proj(xs2, w_in_b, tabs_s, gm_g, gm_b, tm=n_s, gv_dtype=F32, name="in_proj_sample")
    tq = SLABS
    pad_rows = lambda t: jnp.pad(t.reshape(bs, ts, -1), ((0, 0), (0, tq - ts), (0, 0))).reshape(bs * tq, -1)
    ck2 = cache_k[l].reshape(bs * r, KV_WIDTH)
    cv2 = cache_v[l].reshape(bs * r, KV_WIDTH)
    att = _attention(sink_l, pad_rows(q), pad_rows(k), pad_rows(v), ck2, cv2, batch=bs, nb=1, tq=tq,
                     seqs=SAMPLE_SEQS_PER_STEP, chain=1, prev_blocks=1, first_block_has_no_prev=False, name="attn_sample")
    att = att.reshape(bs, tq, ATT_WIDTH)[:, :ts].reshape(n_s, ATT_WIDTH)
    gm = _gate_short(gu, gv, gm_w_s[l], gm_b_s[l], length=ts, name="gate_sample")
    new_ks = jnp.concatenate([cache_k[l], k.reshape(bs, ts, KV_HEADS, HEAD_DIM)], axis=1)[:, ts:][None]
    new_vs = jnp.concatenate([cache_v[l], v.reshape(bs, ts, KV_HEADS, HEAD_DIM)], axis=1)[:, ts:][None]
    new_gs = gv.reshape(bs, ts, GM_WIDTH)[None]
    yp_s, pk_s, lt_s = _out_proj(att, gm, xs2, w_out_b, row_vec(ln1_g[l]), row_vec(ln1_b[l]), r_hi, r_lo, *shared,
                          name="out_proj_sample")

    eidx_t, w_t, rank_t, counts = _route(lt_p, lt_s, router_bias[l].astype(F32).reshape(N_EXPERTS, 1))
    a = n_total * TOP_K
    n_blocks = -(-(a + N_EXPERTS * (EXPERT_ROWS - 1)) // EXPERT_ROWS)
    pad_start, fill_start, fill_len, first_block, n_expert_blocks, n_used = _block_plan(counts.reshape(N_EXPERTS))
    dest_t = _dest(eidx_t, rank_t, pad_start.astype(F32).reshape(N_EXPERTS, 1))
    tiles = lambda a: a.reshape(-1, WORD_SLABS, LANES)
    x_sorted = _dispatch_sc(dest_t, tiles(pk_p), tiles(pk_s), rows=n_blocks * EXPERT_ROWS)
    x_sorted = _fill_padding(fill_start, fill_len, n_used, x_sorted)
    out_sorted = _experts(first_block, n_expert_blocks, n_used, x_sorted, w_gate_e[l], w_up_e[l], w_down_e[l])
    ln2 = (row_vec(ln2_g[l]), row_vec(ln2_b[l]))
    g = _gather_sc(dest_t, out_sorted).reshape(TOP_K, n_total * WORD_SLABS, LANES)
    w2 = w_t.T
    y_p = _combine(g, w2, yp_p, *ln2, n=n_p, row_offset=0, name="combine_prompt")
    y_s = _combine(g, w2, yp_s, *ln2, n=n_s, row_offset=n_p, name="combine_sample")
    return (y_p.reshape(bp, sp, D_MODEL), y_s.reshape(bs, ts, D_MODEL), new_kp, new_vp, new_ks, new_vs, new_gs)
```

```python
import functools

import jax
import jax.numpy as jnp
import numpy as np
from jax import lax
from jax.experimental import pallas as pl
from jax.experimental.pallas import tpu as pltpu
from jax.experimental.pallas import tpu_sc as plsc

D_MODEL = 1024
HEAD_DIM = 64
ATT_HEADS = 8
KV_HEADS = 2
Q_PER_KV = ATT_HEADS // KV_HEADS
GM_HEADS = 8
ATT_WIDTH = ATT_HEADS * HEAD_DIM
KV_WIDTH = KV_HEADS * HEAD_DIM
GM_WIDTH = GM_HEADS * HEAD_DIM
ROPE_WIDTH = ATT_WIDTH + KV_WIDTH
IN_WIDTH = ATT_WIDTH + 2 * KV_WIDTH + 2 * GM_WIDTH
WINDOW = 128
CHUNK = 128
PAST_LEN = 16384
ROPE_THETA = 10000.0
ATT_SCALE = HEAD_DIM ** -0.5
N_EXPERTS = 256
TOP_K = 8
N_GROUPS = 8
TOPK_GROUPS = 4
F_EXPERT = 256
ROUTED_SCALE = 2.5
LN_EPS = 1e-5
DEPTH = 1
ALPHA = (2.0 * DEPTH) ** 0.25

LANES = 128
SLABS = 8
PACKED = D_MODEL // 2
WORD_SLABS = PACKED // LANES
ROW_TILE = 512
EXPERT_ROWS = 256
EXPERT_RING = 8
COMBINE_ROWS = 512
ROUTE_TILE = 512
SAMPLE_SEQS_PER_STEP = 16
PROMPT_BLOCKS_PER_STEP = 2
SC_CORES = 2
SC_SUBCORES = 16
SC_WINDOW = 128
VMEM_LIMIT = 56 * 1024 * 1024

F32 = jnp.float32
BF16 = jnp.bfloat16


def _params(n_axes):
    return pltpu.CompilerParams(dimension_semantics=("arbitrary",) * n_axes, vmem_limit_bytes=VMEM_LIMIT)


def _layer_norm(x, g, b):
    mu = jnp.mean(x, axis=-1, keepdims=True)
    xc = x - mu
    var = jnp.mean(xc * xc, axis=-1, keepdims=True)
    return xc * lax.rsqrt(var + LN_EPS) * g + b


def _pack_halves(x):
    xb = x.astype(BF16)
    lo = lax.bitcast_convert_type(xb[:, :PACKED].astype(F32), jnp.uint32)
    hi = lax.bitcast_convert_type(xb[:, PACKED:].astype(F32), jnp.uint32)
    return (lo >> 16) | hi


def _unpack_halves(w):
    lo = lax.bitcast_convert_type(w << 16, F32)
    hi = lax.bitcast_convert_type(w & jnp.uint32(0xFFFF0000), F32)
    return lo, hi


def _store_row_tiles(ref2d, x):
    m, slabs = x.shape[0], x.shape[1] // LANES
    for s in range(slabs):
        ref2d[pl.ds(s, m, stride=slabs), :] = x[:, s * LANES:(s + 1) * LANES]


def _load_row_tiles(ref2d, m, slabs):
    return jnp.concatenate([ref2d[pl.ds(s, m, stride=slabs), :] for s in range(slabs)], axis=1)


def _gelu(x):
    return 0.5 * x * (1.0 + lax.erf(x * np.float32(np.sqrt(0.5))))


def _in_proj_kernel(x_ref, w_ref, cos_ref, sa_ref, sb_ref, g_ref, b_ref, q_ref, k_ref, v_ref, gu_ref, gv_ref):
    x = x_ref[...].astype(BF16)
    zr = jnp.dot(x, w_ref[:, :ROPE_WIDTH], preferred_element_type=F32)
    pieces = []
    for c in range(ROPE_WIDTH // LANES):
        zc = zr[:, c * LANES:(c + 1) * LANES]
        tl = slice(0, LANES) if c < ATT_WIDTH // LANES else slice(LANES, 2 * LANES)
        pieces.append(zc * cos_ref[:, tl]
                      + pltpu.roll(zc, LANES - HEAD_DIM // 2, 1) * sa_ref[:, tl]
                      + pltpu.roll(zc, HEAD_DIM // 2, 1) * sb_ref[:, tl])
    for c in range(ATT_WIDTH // LANES):
        q_ref[:, c * LANES:(c + 1) * LANES] = pieces[c].astype(q_ref.dtype)
    k_ref[...] = pieces[ATT_WIDTH // LANES]
    v_ref[...] = jnp.dot(x, w_ref[:, ROPE_WIDTH:ROPE_WIDTH + KV_WIDTH], preferred_element_type=F32)
    g0 = ROPE_WIDTH + KV_WIDTH
    zu = jnp.dot(x, w_ref[:, g0:g0 + GM_WIDTH], preferred_element_type=F32)
    gu_ref[...] = _gelu(zu).astype(gu_ref.dtype)
    zv = jnp.dot(x, w_ref[:, g0 + GM_WIDTH:g0 + 2 * GM_WIDTH], preferred_element_type=F32)
    gv = _layer_norm(_gelu(zv), g_ref[...], b_ref[...])
    gv_ref[...] = gv.astype(gv_ref.dtype)


def _in_proj(x2, w_in_b, tabs, ln_g, ln_b, *, tm, gv_dtype, name):
    n = x2.shape[0]
    cos_t, sa_t, sb_t = tabs
    period = cos_t.shape[0] // tm
    row = lambda i: (i, 0)
    tab = lambda i: (i % period, 0)
    fixed = lambda i: (0, 0)
    return pl.pallas_call(
        _in_proj_kernel,
        grid=(n // tm,),
        in_specs=[
            pl.BlockSpec((tm, D_MODEL), row),
            pl.BlockSpec((D_MODEL, IN_WIDTH), fixed),
            pl.BlockSpec((tm, 2 * LANES), tab),
            pl.BlockSpec((tm, 2 * LANES), tab),
            pl.BlockSpec((tm, 2 * LANES), tab),
            pl.BlockSpec((1, GM_WIDTH), fixed),
            pl.BlockSpec((1, GM_WIDTH), fixed),
        ],
        out_specs=[
            pl.BlockSpec((tm, ATT_WIDTH), row),
            pl.BlockSpec((tm, KV_WIDTH), row),
            pl.BlockSpec((tm, KV_WIDTH), row),
            pl.BlockSpec((tm, GM_WIDTH), row),
            pl.BlockSpec((tm, GM_WIDTH), row),
        ],
        out_shape=[
            jax.ShapeDtypeStruct((n, ATT_WIDTH), BF16),
            jax.ShapeDtypeStruct((n, KV_WIDTH), F32),
            jax.ShapeDtypeStruct((n, KV_WIDTH), F32),
            jax.ShapeDtypeStruct((n, GM_WIDTH), BF16),
            jax.ShapeDtypeStruct((n, GM_WIDTH), gv_dtype),
        ],
        compiler_params=_params(1),
        name=name,
    )(x2, w_in_b, cos_t, sa_t, sb_t, ln_g, ln_b)


def _rope_tables(pos):
    half = HEAD_DIM // 2
    lane = jnp.arange(2 * LANES, dtype=jnp.int32)
    inv = ROPE_THETA ** (-(lane % half).astype(F32) * 2.0 / HEAD_DIM)
    ang = pos.astype(F32)[:, None] * inv[None, :]
    scale = jnp.where(lane < LANES, ATT_SCALE, 1.0).astype(F32)[None, :]
    first_half = ((lane % HEAD_DIM) < half)[None, :]
    cos, sin = jnp.cos(ang) * scale, jnp.sin(ang) * scale
    return cos, jnp.where(first_half, -sin, 0.0), jnp.where(first_half, 0.0, sin)


def _attn_kernel(sink_ref, q_ref, kc_ref, vc_ref, kp_ref, vp_ref, o_ref, *, tq, seqs, chain, stack,
                 first_block_has_no_prev):
    nk = WINDOW + tq
    rows = stack * tq
    qi = lax.broadcasted_iota(jnp.int32, (rows, nk), 0) & (tq - 1)
    ks = lax.broadcasted_iota(jnp.int32, (rows, nk), 1)
    band = (ks >= qi) & (ks <= qi + WINDOW)
    first_mask = band & ((pl.program_id(1) > 0) | (ks >= WINDOW)) if first_block_has_no_prev else band
    sinks = [jnp.concatenate([jnp.full((tq, 1), sink_ref[h0 + j], F32) for j in range(stack)], axis=0)
             for h0 in range(0, ATT_HEADS, stack)]
    for b in range(seqs * chain):
        qrows = slice(b * tq, (b + 1) * tq)
        if chain > 1 and b > 0:
            kprev, vprev = kc_ref[(b - 1) * tq:b * tq, :], vc_ref[(b - 1) * tq:b * tq, :]
        else:
            kprev, vprev = kp_ref[b * WINDOW:(b + 1) * WINDOW, :], vp_ref[b * WINDOW:(b + 1) * WINDOW, :]
        mask = band if (chain > 1 and b > 0) else first_mask
        q = q_ref[qrows, :]
        kk = jnp.concatenate([kprev, kc_ref[qrows, :]], axis=0).astype(BF16)
        vv = jnp.concatenate([vprev, vc_ref[qrows, :]], axis=0).astype(BF16)
        outs = []
        for i, h0 in enumerate(range(0, ATT_HEADS, stack)):
            g = h0 // Q_PER_KV
            kg = kk[:, g * HEAD_DIM:(g + 1) * HEAD_DIM]
            vg = vv[:, g * HEAD_DIM:(g + 1) * HEAD_DIM]
            qg = jnp.concatenate([q[:, (h0 + j) * HEAD_DIM:(h0 + j + 1) * HEAD_DIM] for j in range(stack)], axis=0)
            s = lax.dot_general(qg, kg, (((1,), (1,)), ((), ())), preferred_element_type=F32)
            s = jnp.where(mask, s, -jnp.inf)
            m = jnp.maximum(jnp.max(s, axis=-1, keepdims=True), sinks[i])
            p = jnp.exp(s - m)
            denom = jnp.sum(p, axis=-1, keepdims=True) + jnp.exp(sinks[i] - m)
            o = jnp.dot((p / denom).astype(BF16), vg, preferred_element_type=F32)
            outs.extend(o[j * tq:(j + 1) * tq, :] for j in range(stack))
        o_ref[qrows, :] = jnp.concatenate(outs, axis=1).astype(o_ref.dtype)


def _attention(sink, q2, k2, v2, kprev2, vprev2, *, batch, nb, tq, seqs, chain, prev_blocks, first_block_has_no_prev,
               name):
    assert tq & (tq - 1) == 0 and batch % seqs == 0 and (seqs == 1 or nb == prev_blocks == 1)
    assert (seqs == 1 or chain == 1) and nb % chain == 0 and (chain == 1 or tq == WINDOW)
    steps = nb // chain
    cur = lambda b, n, s: (b * steps + n, 0)
    prev = lambda b, n, s: (b * prev_blocks + jnp.maximum(n * chain - 1, 0), 0)
    stack = Q_PER_KV if Q_PER_KV * tq <= WINDOW else 1
    kern = functools.partial(_attn_kernel, tq=tq, seqs=seqs, chain=chain, stack=stack,
                             first_block_has_no_prev=first_block_has_no_prev)
    return pl.pallas_call(
        kern,
        grid_spec=pltpu.PrefetchScalarGridSpec(
            num_scalar_prefetch=1,
            grid=(batch // seqs, steps),
            in_specs=[
                pl.BlockSpec((seqs * chain * tq, ATT_WIDTH), cur),
                pl.BlockSpec((seqs * chain * tq, KV_WIDTH), cur),
                pl.BlockSpec((seqs * chain * tq, KV_WIDTH), cur),
                pl.BlockSpec((seqs * WINDOW, KV_WIDTH), prev),
                pl.BlockSpec((seqs * WINDOW, KV_WIDTH), prev),
            ],
            out_specs=pl.BlockSpec((seqs * chain * tq, ATT_WIDTH), cur),
        ),
        out_shape=jax.ShapeDtypeStruct(q2.shape, BF16),
        compiler_params=_params(2),
        name=name,
    )(sink, q2, k2, v2, kprev2, vprev2)


def _gate_kernel(gu_ref, gv_ref, w_ref, b_ref, o_ref, *, chunk, n_chunks):
    ri = lax.broadcasted_iota(jnp.int32, (chunk, chunk), 0)
    ci = lax.broadcasted_iota(jnp.int32, (chunk, chunk), 1)
    ws = [jnp.where(ci <= ri, w_ref[h], 0.0).astype(BF16) for h in range(GM_HEADS)]
    for c in range(n_chunks):
        rows = slice(c * chunk, (c + 1) * chunk)
        gv = gv_ref[rows, :].astype(BF16)
        sv = jnp.concatenate(
            [jnp.dot(ws[h], gv[:, h * HEAD_DIM:(h + 1) * HEAD_DIM], preferred_element_type=F32)
             for h in range(GM_HEADS)], axis=1)
        o_ref[rows, :] = (gu_ref[rows, :].astype(F32) * (sv + b_ref[...])).astype(o_ref.dtype)


def _gate(gu2, gv2, w_s, b_tab, *, chunk, n_chunks, name):
    n = gu2.shape[0]
    tm = chunk * n_chunks
    row = lambda i: (i, 0)
    kern = functools.partial(_gate_kernel, chunk=chunk, n_chunks=n_chunks)
    return pl.pallas_call(
        kern,
        grid=(n // tm,),
        in_specs=[
            pl.BlockSpec((tm, GM_WIDTH), row),
            pl.BlockSpec((tm, GM_WIDTH), row),
            pl.BlockSpec((GM_HEADS, chunk, chunk), lambda i: (0, 0, 0)),
            pl.BlockSpec((chunk, GM_WIDTH), lambda i: (0, 0)),
        ],
        out_specs=pl.BlockSpec((tm, GM_WIDTH), row),
        out_shape=jax.ShapeDtypeStruct((n, GM_WIDTH), BF16),
        compiler_params=_params(1),
        name=name,
    )(gu2, gv2, w_s, b_tab)


def _gate_short_kernel(gu_ref, gv_ref, coef_ref, b_ref, o_ref, *, length):
    n = gv_ref.shape[0]
    gv = gv_ref[...].astype(F32)
    tiles = lambda x: x.reshape(n // SLABS, SLABS, GM_WIDTH)
    sv = tiles(gv) * coef_ref[0][None] + b_ref[...][None]
    for d in range(1, length):
        sv = sv + tiles(pltpu.roll(gv, d, 0)) * coef_ref[d][None]
    o_ref[...] = (gu_ref[...].astype(F32) * sv.reshape(n, GM_WIDTH)).astype(o_ref.dtype)


def _gate_short(gu2, gv2, w_s, b_s, *, length, name):
    n = gu2.shape[0]
    assert SLABS % length == 0 and n % SLABS == 0
    pos = jnp.arange(length)
    lag = jnp.arange(length)
    src = pos[None, :] - lag[:, None]
    coef = jnp.where(src >= 0, w_s[:, pos[None, :], jnp.maximum(src, 0)], 0.0)
    coef = jnp.repeat(coef.transpose(1, 2, 0), HEAD_DIM, axis=2)
    coef = jnp.tile(coef, (1, SLABS // length, 1))
    bias = jnp.tile(jnp.repeat(b_s[:, :length].T, HEAD_DIM, axis=1), (SLABS // length, 1))
    whole = lambda shape: pl.BlockSpec(shape, lambda i: (0,) * len(shape))
    return pl.pallas_call(
        functools.partial(_gate_short_kernel, length=length),
        grid=(1,),
        in_specs=[whole((n, GM_WIDTH)), whole((n, GM_WIDTH)), whole((length, SLABS, GM_WIDTH)),
                  whole((SLABS, GM_WIDTH))],
        out_specs=whole((n, GM_WIDTH)),
        out_shape=jax.ShapeDtypeStruct((n, GM_WIDTH), BF16),
        compiler_params=_params(1),
        name=name,
    )(gu2, gv2, coef, bias)


def _out_proj_kernel(att_ref, gm_ref, x_ref, wo_ref, g_ref, b_ref, rh_ref, rl_ref, wg_ref, wu_ref, wd_ref, yp_ref,
                     hp_ref, lg_ref):
    mix = jnp.dot(att_ref[...], wo_ref[:ATT_WIDTH, :], preferred_element_type=F32)
    mix = mix + jnp.dot(gm_ref[...], wo_ref[ATT_WIDTH:, :], preferred_element_type=F32)
    h = _layer_norm(ALPHA * x_ref[...] + mix, g_ref[...], b_ref[...])
    _store_row_tiles(hp_ref, _pack_halves(h))
    h_hi = h.astype(BF16)
    a = jnp.dot(h_hi, wg_ref[...], preferred_element_type=F32)
    u = jnp.dot(h_hi, wu_ref[...], preferred_element_type=F32)
    shared = jnp.dot((a * jax.nn.sigmoid(a) * u).astype(BF16), wd_ref[...], preferred_element_type=F32)
    yp_ref[...] = ALPHA * h + shared
    h_lo = (h - h_hi.astype(F32)).astype(BF16)
    nt = (((1,), (1,)), ((), ()))
    lg = lax.dot_general(rh_ref[...], h_hi, nt, preferred_element_type=F32)
    lg = lg + lax.dot_general(rh_ref[...], h_lo, nt, preferred_element_type=F32)
    lg = lg + lax.dot_general(rl_ref[...], h_hi, nt, preferred_element_type=F32)
    lg_ref[...] = lg


def _out_proj(att2, gm2, x2, w_out_b, ln_g, ln_b, r_hi, r_lo, wg_b, wu_b, wd_b, *, name):
    n = x2.shape[0]
    tm = ROW_TILE
    row = lambda i: (i, 0)
    fixed = lambda i: (0, 0)
    return pl.pallas_call(
        _out_proj_kernel,
        grid=(n // tm,),
        in_specs=[
            pl.BlockSpec((tm, ATT_WIDTH), row),
            pl.BlockSpec((tm, GM_WIDTH), row),
            pl.BlockSpec((tm, D_MODEL), row),
            pl.BlockSpec((D_MODEL, D_MODEL), fixed),
            pl.BlockSpec((1, D_MODEL), fixed),
            pl.BlockSpec((1, D_MODEL), fixed),
            pl.BlockSpec((N_EXPERTS, D_MODEL), fixed),
            pl.BlockSpec((N_EXPERTS, D_MODEL), fixed),
            pl.BlockSpec((D_MODEL, F_EXPERT), fixed),
            pl.BlockSpec((D_MODEL, F_EXPERT), fixed),
            pl.BlockSpec((F_EXPERT, D_MODEL), fixed),
        ],
        out_specs=[
            pl.BlockSpec((tm, D_MODEL), row),
            pl.BlockSpec((tm * WORD_SLABS, LANES), row),
            pl.BlockSpec((N_EXPERTS, tm), lambda i: (0, i)),
        ],
        out_shape=[
            jax.ShapeDtypeStruct((n, D_MODEL), F32),
            jax.ShapeDtypeStruct((n * WORD_SLABS, LANES), jnp.uint32),
            jax.ShapeDtypeStruct((N_EXPERTS, n), F32),
        ],
        compiler_params=_params(1),
        name=name,
    )(att2, gm2, x2, w_out_b, ln_g, ln_b, r_hi, r_lo, wg_b, wu_b, wd_b)


def _experts_kernel(first_ref, nblk_ref, nu_ref, x_hbm, wg_ref, wu_ref, wd_ref, o_hbm, xbuf, obuf, wg_s, wu_s, wd_s,
                    in_sem, out_sem, fill_sem, *, n_blocks):
    e = pl.program_id(0)
    nb = nblk_ref[e]
    b0 = first_ref[e]
    n_used = nu_ref[0]
    ahead = EXPERT_RING // 2

    def rows_of(block):
        return pl.ds(pl.multiple_of(block * EXPERT_ROWS, EXPERT_ROWS), EXPERT_ROWS)

    def slot_of(block):
        return block & (EXPERT_RING - 1)

    def slab(s):
        return pl.ds(s * LANES, LANES)

    def in_copies(block):
        slot = slot_of(block)
        return [pltpu.make_async_copy(x_hbm.at[rows_of(block), s, :], xbuf.at[slot, :, slab(s)], in_sem.at[slot])
                for s in range(WORD_SLABS)]

    def out_copies(block, slot=None, sem=None):
        slot = slot_of(block) if slot is None else slot
        sem = out_sem.at[slot] if sem is None else sem
        return [pltpu.make_async_copy(obuf.at[slot, :, slab(s)], o_hbm.at[rows_of(block), s, :], sem)
                for s in range(WORD_SLABS)]

    def start(copies):
        for cp in copies:
            cp.start()

    def wait(copies):
        for cp in copies:
            cp.wait()

    @pl.when(e == 0)
    def _():
        for j in range(ahead):
            pl.when(j < n_used)(lambda j=j: start(in_copies(j)))

    def process(block, count):
        blocks = [block + j for j in range(count)]
        for blk in blocks:
            wait(in_copies(blk))
        x = [xbuf[slot_of(blk)] for blk in blocks]
        xl, xh = _unpack_halves(x[0] if count == 1 else jnp.concatenate(x, axis=0))
        xl, xh = xl.astype(BF16), xh.astype(BF16)
        for blk in blocks:
            pl.when(blk + ahead < n_used)(lambda blk=blk: start(in_copies(blk + ahead)))
            pl.when(blk >= ahead)(lambda blk=blk: wait(out_copies(blk - ahead)))
        a = (jnp.dot(xl, wg_s[:PACKED, :], preferred_element_type=F32)
             + jnp.dot(xh, wg_s[PACKED:, :], preferred_element_type=F32))
        u = (jnp.dot(xl, wu_s[:PACKED, :], preferred_element_type=F32)
             + jnp.dot(xh, wu_s[PACKED:, :], preferred_element_type=F32))
        hb = (a * jax.nn.sigmoid(a) * u).astype(BF16)
        o = _pack_halves(jnp.dot(hb, wd_s[...], preferred_element_type=F32))
        for j, blk in enumerate(blocks):
            obuf[slot_of(blk)] = o[j * EXPERT_ROWS:(j + 1) * EXPERT_ROWS, :]
            start(out_copies(blk))

    @pl.when(nb > 0)
    def _():
        wg_s[...] = wg_ref[0].astype(BF16)
        wu_s[...] = wu_ref[0].astype(BF16)
        wd_s[...] = wd_ref[0].astype(BF16)

        def pair(i, carry):
            process(b0 + 2 * i, 2)
            return carry

        lax.fori_loop(0, nb // 2, pair, 0)
        pl.when(nb % 2 == 1)(lambda: process(b0 + nb - 1, 1))

    @pl.when(e == N_EXPERTS - 1)
    def _():
        for j in range(ahead):
            pl.when(n_used - 1 - j >= 0)(lambda j=j: wait(out_copies(n_used - 1 - j)))
        obuf[0] = jnp.zeros((EXPERT_ROWS, PACKED), jnp.uint32)

        def on_unused_blocks(fn):
            def body(b, c):
                fn(out_copies(b, slot=0, sem=fill_sem))
                return c
            lax.fori_loop(n_used, n_blocks, body, 0)

        on_unused_blocks(start)
        on_unused_blocks(wait)


def _experts(first_block, n_expert_blocks, n_used, x_sorted, w_gate_e, w_up_e, w_down_e):
    rows = x_sorted.shape[0]
    wmap = lambda e, *_: (e, 0, 0)
    kern = functools.partial(_experts_kernel, n_blocks=rows // EXPERT_ROWS)
    return pl.pallas_call(
        kern,
        grid_spec=pltpu.PrefetchScalarGridSpec(
            num_scalar_prefetch=3,
            grid=(N_EXPERTS,),
            in_specs=[
                pl.BlockSpec(memory_space=pl.ANY),
                pl.BlockSpec((1, D_MODEL, F_EXPERT), wmap),
                pl.BlockSpec((1, D_MODEL, F_EXPERT), wmap),
                pl.BlockSpec((1, F_EXPERT, D_MODEL), wmap),
            ],
            out_specs=pl.BlockSpec(memory_space=pl.ANY),
            scratch_shapes=[
                pltpu.VMEM((EXPERT_RING, EXPERT_ROWS, PACKED), jnp.uint32),
                pltpu.VMEM((EXPERT_RING, EXPERT_ROWS, PACKED), jnp.uint32),
                pltpu.VMEM((D_MODEL, F_EXPERT), BF16),
                pltpu.VMEM((D_MODEL, F_EXPERT), BF16),
                pltpu.VMEM((F_EXPERT, D_MODEL), BF16),
                pltpu.SemaphoreType.DMA((EXPERT_RING,)),
                pltpu.SemaphoreType.DMA((EXPERT_RING,)),
                pltpu.SemaphoreType.DMA,
            ],
        ),
        out_shape=jax.ShapeDtypeStruct(x_sorted.shape, x_sorted.dtype),
        compiler_params=_params(1),
        name="experts",
    )(first_block, n_expert_blocks, n_used, x_sorted, w_gate_e, w_up_e, w_down_e)


def _gather_sc(dest_t, out_sorted):
    win = SC_WINDOW
    n = dest_t.shape[1]
    n_win = n // win
    workers = SC_CORES * SC_SUBCORES
    mesh = plsc.VectorSubcoreMesh(core_axis_name="c", subcore_axis_name="s")

    @functools.partial(
        pl.kernel, mesh=mesh, name="gather_sc",
        out_type=jax.ShapeDtypeStruct((TOP_K, n, WORD_SLABS, LANES), jnp.uint32),
        scratch_types=[pltpu.VMEM((TOP_K, win), jnp.int32), pltpu.VMEM((win, WORD_SLABS, LANES), jnp.uint32)])
    def gather(dest_hbm, os_hbm, g_hbm, idx_v, rows_v):
        wid = lax.axis_index("s") * SC_CORES + lax.axis_index("c")

        @pl.loop(0, (n_win - wid + workers - 1) // workers)
        def _(i):
            base = pl.multiple_of((wid + i * workers) * win, win)
            pltpu.sync_copy(dest_hbm.at[:, pl.ds(base, win)], idx_v)
            for k in range(TOP_K):
                pltpu.sync_copy(os_hbm.at[idx_v.at[k]], rows_v)
                pltpu.sync_copy(rows_v, g_hbm.at[k, pl.ds(base, win)])

    return gather(dest_t, out_sorted)


def _combine_kernel(g_ref, w_ref, yp_ref, ln_g_ref, ln_b_ref, *rest, carried_tiles):
    y_ref = rest[-1]
    t = y_ref.shape[0]

    def finish_tile():
        w = w_ref[...]
        lo_acc = jnp.zeros((t, PACKED), F32)
        hi_acc = jnp.zeros((t, PACKED), F32)
        for k in range(TOP_K):
            lo, hi = _unpack_halves(_load_row_tiles(g_ref.at[k], t, WORD_SLABS))
            lo_acc = lo_acc + w[:, k:k + 1] * lo
            hi_acc = hi_acc + w[:, k:k + 1] * hi
        routed = jnp.concatenate([lo_acc, hi_acc], axis=1)
        y_ref[...] = _layer_norm(yp_ref[...] + routed, ln_g_ref[...], ln_b_ref[...])

    if carried_tiles:
        carried_ref = rest[0]

        @pl.when(pl.program_id(0) < carried_tiles)
        def _():
            y_ref[...] = carried_ref[...]

        pl.when(pl.program_id(0) >= carried_tiles)(finish_tile)
    else:
        finish_tile()


def _combine(g, w2, y_part, ln_g, ln_b, *, n, g_offset, w_offset, carried=None, name):
    tm = COMBINE_ROWS
    c = 0 if carried is None else carried.shape[0] // tm
    g_off, w_off = g_offset // tm, w_offset // tm
    fixed = lambda i: (0, 0)
    row = lambda i: (i, 0)
    in_specs = [
        pl.BlockSpec((TOP_K, tm * WORD_SLABS, LANES), lambda i: (0, jnp.maximum(i - c, 0) + g_off, 0)),
        pl.BlockSpec((tm, TOP_K), lambda i: (i + w_off, 0)),
        pl.BlockSpec((tm, D_MODEL), row),
        pl.BlockSpec((1, D_MODEL), fixed),
        pl.BlockSpec((1, D_MODEL), fixed),
    ]
    args = [g, w2, y_part, ln_g, ln_b]
    if carried is not None:
        in_specs.append(pl.BlockSpec((tm, D_MODEL), lambda i: (jnp.minimum(i, c - 1), 0)))
        args.append(carried)
    return pl.pallas_call(
        functools.partial(_combine_kernel, carried_tiles=c),
        grid=(n // tm,),
        in_specs=in_specs,
        out_specs=pl.BlockSpec((tm, D_MODEL), row),
        out_shape=jax.ShapeDtypeStruct((n, D_MODEL), F32),
        compiler_params=_params(1),
        name=name,
    )(*args)


def _route_kernel(lgp_ref, lgs_ref, bias_ref, eidx_ref, w_ref, rank_ref, cnt_ref, carry_ref, *, prompt_tiles):
    @pl.when(pl.program_id(0) == 0)
    def _():
        carry_ref[...] = jnp.zeros_like(carry_ref)

    t = lgp_ref.shape[1]
    gsz = N_EXPERTS // N_GROUPS
    neg = -jnp.inf
    s = jax.nn.sigmoid(jnp.where(pl.program_id(0) < prompt_tiles, lgp_ref[...], lgs_ref[...]))
    biased = s + bias_ref[...]
    io_g = lax.broadcasted_iota(jnp.int32, (gsz, t), 0)
    grp_rows = []
    for g in range(N_GROUPS):
        blk = biased[g * gsz:(g + 1) * gsz, :]
        m1 = jnp.max(blk, axis=0, keepdims=True)
        i1 = jnp.min(jnp.where(blk == m1, io_g, gsz), axis=0, keepdims=True)
        m2 = jnp.max(jnp.where(io_g == i1, neg, blk), axis=0, keepdims=True)
        grp_rows.append(m1 + m2)
    gs = jnp.concatenate(grp_rows, axis=0)
    io8 = lax.broadcasted_iota(jnp.int32, (N_GROUPS, t), 0)
    gsel = jnp.zeros((N_GROUPS, t), jnp.int32)
    for _ in range(TOPK_GROUPS):
        m = jnp.max(gs, axis=0, keepdims=True)
        gi = jnp.min(jnp.where(gs == m, io8, N_GROUPS), axis=0, keepdims=True)
        hit = io8 == gi
        gsel = jnp.where(hit, 1, gsel)
        gs = jnp.where(hit, neg, gs)
    masked = jnp.concatenate(
        [jnp.where(gsel[g:g + 1, :] > 0, biased[g * gsz:(g + 1) * gsz, :], neg) for g in range(N_GROUPS)], axis=0)

    eio = lax.broadcasted_iota(jnp.int32, (N_EXPERTS, t), 0)
    cur = masked
    idx_rows, w_rows = [], []
    for _ in range(TOP_K):
        m = jnp.max(cur, axis=0, keepdims=True)
        idx = jnp.min(jnp.where(cur == m, eio, N_EXPERTS), axis=0, keepdims=True)
        hit = eio == idx
        w_rows.append(jnp.sum(jnp.where(hit, s, 0.0), axis=0, keepdims=True))
        cur = jnp.where(hit, neg, cur)
        idx_rows.append(idx)
    sel = jnp.where(cur != masked, 1.0, 0.0)

    tri = jnp.where(lax.broadcasted_iota(jnp.int32, (t, t), 0) < lax.broadcasted_iota(jnp.int32, (t, t), 1), 1.0, 0.0)
    pref = jnp.dot(sel.astype(BF16), tri.astype(BF16), preferred_element_type=F32) + carry_ref[...]
    rank_rows = [jnp.sum(jnp.where(eio == idx_rows[k], pref, 0.0), axis=0, keepdims=True) for k in range(TOP_K)]
    carry_ref[...] += jnp.sum(sel, axis=1, keepdims=True)

    wk = jnp.concatenate(w_rows, axis=0)
    eidx_ref[...] = jnp.concatenate(idx_rows, axis=0)
    w_ref[...] = wk / jnp.sum(wk, axis=0, keepdims=True) * ROUTED_SCALE
    rank_ref[...] = jnp.concatenate(rank_rows, axis=0).astype(jnp.int32)
    cnt_ref[...] = carry_ref[...].astype(jnp.int32)


def _route(logits_p, logits_s, bias_col):
    t = ROUTE_TILE
    prompt_tiles = logits_p.shape[1] // t
    n = logits_p.shape[1] + logits_s.shape[1]
    col = lambda i: (0, i)
    fixed = lambda i: (0, 0)
    kern = functools.partial(_route_kernel, prompt_tiles=prompt_tiles)
    return pl.pallas_call(
        kern,
        grid=(n // t,),
        in_specs=[pl.BlockSpec((N_EXPERTS, t), lambda i: (0, jnp.minimum(i, prompt_tiles - 1))),
                  pl.BlockSpec((N_EXPERTS, t), lambda i: (0, jnp.maximum(i - prompt_tiles, 0))),
                  pl.BlockSpec((N_EXPERTS, 1), fixed)],
        out_specs=[
            pl.BlockSpec((TOP_K, t), col),
            pl.BlockSpec((TOP_K, t), col),
            pl.BlockSpec((TOP_K, t), col),
            pl.BlockSpec((N_EXPERTS, 1), fixed),
        ],
        out_shape=[
            jax.ShapeDtypeStruct((TOP_K, n), jnp.int32),
            jax.ShapeDtypeStruct((TOP_K, n), F32),
            jax.ShapeDtypeStruct((TOP_K, n), jnp.int32),
            jax.ShapeDtypeStruct((N_EXPERTS, 1), jnp.int32),
        ],
        scratch_shapes=[pltpu.VMEM((N_EXPERTS, 1), F32)],
        compiler_params=_params(1),
        name="route",
    )(logits_p, logits_s, bias_col)


def _dest_kernel(eidx_ref, rank_ref, start_ref, dest_ref):
    t = eidx_ref.shape[1]
    eio = lax.broadcasted_iota(jnp.int32, (N_EXPERTS, t), 0)
    start = start_ref[...]
    rows = [jnp.sum(jnp.where(eio == eidx_ref[k:k + 1, :], start, 0.0), axis=0, keepdims=True) for k in range(TOP_K)]
    dest_ref[...] = jnp.concatenate(rows, axis=0).astype(jnp.int32) + rank_ref[...]


def _dest(eidx_t, rank_t, pad_start_col):
    n = eidx_t.shape[1]
    t = ROW_TILE
    col = lambda i: (0, i)
    return pl.pallas_call(
        _dest_kernel,
        grid=(n // t,),
        in_specs=[pl.BlockSpec((TOP_K, t), col), pl.BlockSpec((TOP_K, t), col),
                  pl.BlockSpec((N_EXPERTS, 1), lambda i: (0, 0))],
        out_specs=pl.BlockSpec((TOP_K, t), col),
        out_shape=jax.ShapeDtypeStruct((TOP_K, n), jnp.int32),
        compiler_params=_params(1),
        name="dest",
    )(eidx_t, rank_t, pad_start_col)


def _dispatch_sc(dest_t, pk_p, pk_s, *, rows):
    win = SC_WINDOW
    p_win, s_win = pk_p.shape[0] // win, pk_s.shape[0] // win
    n_win = p_win + s_win
    workers = SC_CORES * SC_SUBCORES
    mesh = plsc.VectorSubcoreMesh(core_axis_name="c", subcore_axis_name="s")

    @functools.partial(
        pl.kernel, mesh=mesh, name="dispatch_sc",
        out_type=jax.ShapeDtypeStruct((rows, WORD_SLABS, LANES), jnp.uint32),
        scratch_types=[pltpu.VMEM((TOP_K, win), jnp.int32), pltpu.VMEM((win, WORD_SLABS, LANES), jnp.uint32)])
    def scatter(dest_hbm, pkp_hbm, pks_hbm, xs_hbm, idx_v, rows_v):
        wid = lax.axis_index("s") * SC_CORES + lax.axis_index("c")

        @pl.loop(0, (n_win - wid + workers - 1) // workers)
        def _(i):
            g = wid + i * workers

            @pl.when(g < p_win)
            def _():
                pltpu.sync_copy(pkp_hbm.at[pl.ds(pl.multiple_of(g * win, win), win)], rows_v)

            @pl.when(g >= p_win)
            def _():
                pltpu.sync_copy(pks_hbm.at[pl.ds(pl.multiple_of((g - p_win) * win, win), win)], rows_v)

            pltpu.sync_copy(dest_hbm.at[:, pl.ds(pl.multiple_of(g * win, win), win)], idx_v)
            for k in range(TOP_K):
                pltpu.sync_copy(rows_v, xs_hbm.at[idx_v.at[k]])

    return scatter(dest_t, pk_p, pk_s)


def _fill_kernel(fill_ref, len_ref, nu_ref, xs_in_ref, xs_ref, zbuf, fill_sem, *, n_blocks):
    del xs_in_ref
    zbuf[...] = jnp.zeros_like(zbuf)

    def fill_copy(row0, size):
        return pltpu.make_async_copy(zbuf.at[pl.ds(0, size)], xs_ref.at[pl.ds(row0, size)], fill_sem)

    def on_padding(fn):
        def body(e, c):
            base, length = fill_ref[e], len_ref[e]
            size = EXPERT_ROWS // 2
            while size >= 1:
                piece = fill_copy(base + (length & ~(2 * size - 1)), size)
                pl.when((length & size) != 0)(functools.partial(fn, piece))
                size //= 2
            return c
        lax.fori_loop(0, N_EXPERTS, body, 0)

    def on_unused_blocks(fn):
        lax.fori_loop(nu_ref[0], n_blocks, lambda b, c: (fn(fill_copy(b * EXPERT_ROWS, EXPERT_ROWS)), c)[1], 0)

    on_padding(lambda cp: cp.start())
    on_unused_blocks(lambda cp: cp.start())
    on_padding(lambda cp: cp.wait())
    on_unused_blocks(lambda cp: cp.wait())


def _fill_padding(fill_start, fill_len, n_used, x_sorted):
    n_blocks = x_sorted.shape[0] // EXPERT_ROWS
    return pl.pallas_call(
        functools.partial(_fill_kernel, n_blocks=n_blocks),
        grid_spec=pltpu.PrefetchScalarGridSpec(
            num_scalar_prefetch=3,
            grid=(1,),
            in_specs=[pl.BlockSpec(memory_space=pl.ANY)],
            out_specs=pl.BlockSpec(memory_space=pl.ANY),
            scratch_shapes=[pltpu.VMEM((EXPERT_ROWS, WORD_SLABS, LANES), jnp.uint32), pltpu.SemaphoreType.DMA],
        ),
        out_shape=jax.ShapeDtypeStruct(x_sorted.shape, x_sorted.dtype),
        input_output_aliases={3: 0},
        compiler_params=_params(1),
        name="fill_padding",
    )(fill_start, fill_len, n_used, x_sorted)


def _block_plan(counts):
    padded = (counts + EXPERT_ROWS - 1) // EXPERT_ROWS * EXPERT_ROWS
    pad_end = jnp.cumsum(padded).astype(jnp.int32)
    pad_start = pad_end - padded
    n_used = pad_end[-1] // EXPERT_ROWS
    fill_start = pad_start + counts
    fill_len = pad_end - fill_start
    first_block = pad_start // EXPERT_ROWS
    n_expert_blocks = (padded // EXPERT_ROWS).astype(jnp.int32)
    return pad_start, fill_start, fill_len, first_block, n_expert_blocks, n_used.reshape(1).astype(jnp.int32)


def kernel(x_prompt, x_sample, cache_k, cache_v, w_in, sink, gm_ln_g, gm_ln_b, gm_w_s, gm_b_s, w_out, ln1_g, ln1_b,
           router_w, router_bias, w_gate_e, w_up_e, w_down_e, w_gate_s, w_up_s, w_down_s, ln2_g, ln2_b):
    bp, sp = x_prompt.shape[:2]
    bs, ts = x_sample.shape[:2]
    r = cache_k.shape[2]
    assert r == WINDOW and sp % ROW_TILE == 0 and (bs * ts) % ROW_TILE == 0
    n_p, n_s = bp * sp, bs * ts
    n_total = n_p + n_s
    l = 0

    w_in_b = w_in[l].astype(BF16)
    w_out_b = w_out[l].astype(BF16)
    router_t = router_w[l].T
    r_hi = router_t.astype(BF16)
    r_lo = (router_t - r_hi.astype(F32)).astype(BF16)
    row_vec = lambda v: v.reshape(1, -1)
    gm_g, gm_b = row_vec(gm_ln_g[l]), row_vec(gm_ln_b[l])
    sink_l = sink[l].astype(F32)
    shared = (w_gate_s[l].astype(BF16), w_up_s[l].astype(BF16), w_down_s[l].astype(BF16))

    xp2 = x_prompt.reshape(n_p, D_MODEL)
    tabs_p = _rope_tables(jnp.arange(sp, dtype=jnp.int32))
    q, k, v, gu, gv = _in_proj(xp2, w_in_b, tabs_p, gm_g, gm_b, tm=ROW_TILE, gv_dtype=BF16, name="in_proj_prompt")
    nb = sp // WINDOW
    att = _attention(sink_l, q, k, v, k, v, batch=bp, nb=nb, tq=WINDOW, seqs=1, chain=PROMPT_BLOCKS_PER_STEP,
                     prev_blocks=nb,
                     first_block_has_no_prev=True, name="attn_prompt")
    b_tab_p = jnp.repeat(gm_b_s[l].T, HEAD_DIM, axis=1)
    gm = _gate(gu, gv, gm_w_s[l], b_tab_p, chunk=CHUNK, n_chunks=ROW_TILE // CHUNK, name="gate_prompt")
    last_rows = lambda t: t.reshape(bp, sp, KV_WIDTH)[:, sp - r:, :].reshape(1, bp, r, KV_HEADS, HEAD_DIM)
    new_kp, new_vp = last_rows(k), last_rows(v)
    yp_p, pk_p, lt_p = _out_proj(att, gm, xp2, w_out_b, row_vec(ln1_g[l]), row_vec(ln1_b[l]), r_hi, r_lo, *shared,
                          name="out_proj_prompt")

    xs2 = x_sample.reshape(n_s, D_MODEL)
    pos_s = PAST_LEN + jnp.arange(ts, dtype=jnp.int32)
    tabs_s = tuple(jnp.tile(t, (bs, 1)) for t in _rope_tables(pos_s))
    q, k, v, gu, gv = _in_proj(xs2, w_in_b, tabs_s, gm_g, gm_b, tm=n_s, gv_dtype=F32, name="in_proj_sample")
    tq = SLABS
    pad_rows = lambda t: jnp.pad(t.reshape(bs, ts, -1), ((0, 0), (0, tq - ts), (0, 0))).reshape(bs * tq, -1)
    ck2 = cache_k[l].reshape(bs * r, KV_WIDTH)
    cv2 = cache_v[l].reshape(bs * r, KV_WIDTH)
    att = _attention(sink_l, pad_rows(q), pad_rows(k), pad_rows(v), ck2, cv2, batch=bs, nb=1, tq=tq,
                     seqs=SAMPLE_SEQS_PER_STEP, chain=1, prev_blocks=1, first_block_has_no_prev=False, name="attn_sample")
    att = att.reshape(bs, tq, ATT_WIDTH)[:, :ts].reshape(n_s, ATT_WIDTH)
    gm = _gate_short(gu, gv, gm_w_s[l], gm_b_s[l], length=ts, name="gate_sample")
    new_ks = jnp.concatenate([cache_k[l], k.reshape(bs, ts, KV_HEADS, HEAD_DIM)], axis=1)[:, ts:][None]
    new_vs = jnp.concatenate([cache_v[l], v.reshape(bs, ts, KV_HEADS, HEAD_DIM)], axis=1)[:, ts:][None]
    new_gs = gv.reshape(bs, ts, GM_WIDTH)[None]
    yp_s, pk_s, lt_s = _out_proj(att, gm, xs2, w_out_b, row_vec(ln1_g[l]), row_vec(ln1_b[l]), r_hi, r_lo, *shared,
                          name="out_proj_sample")

    eidx_t, w_t, rank_t, counts = _route(lt_p, lt_s, router_bias[l].astype(F32).reshape(N_EXPERTS, 1))
    a = n_total * TOP_K
    n_blocks = -(-(a + N_EXPERTS * (EXPERT_ROWS - 1)) // EXPERT_ROWS)
    pad_start, fill_start, fill_len, first_block, n_expert_blocks, n_used = _block_plan(counts.reshape(N_EXPERTS))
    dest_t = _dest(eidx_t, rank_t, pad_start.astype(F32).reshape(N_EXPERTS, 1))
    tiles = lambda a: a.reshape(-1, WORD_SLABS, LANES)
    x_sorted = _dispatch_sc(dest_t, tiles(pk_p), tiles(pk_s), rows=n_blocks * EXPERT_ROWS)
    x_sorted = _fill_padding(fill_start, fill_len, n_used, x_sorted)
    out_sorted = _experts(first_block, n_expert_blocks, n_used, x_sorted, w_gate_e[l], w_up_e[l], w_down_e[l])
    ln2 = (row_vec(ln2_g[l]), row_vec(ln2_b[l]))
    half = n_p // 2
    gathered = lambda d: _gather_sc(d, out_sorted).reshape(TOP_K, d.shape[1] * WORD_SLABS, LANES)
    g_a, g_b = gathered(dest_t[:, :half]), gathered(dest_t[:, half:])
    w2 = w_t.T
    y_a = _combine(g_a, w2, yp_p, *ln2, n=half, g_offset=0, w_offset=0, name="combine_prompt_a")
    y_p = _combine(g_b, w2, yp_p, *ln2, n=n_p, g_offset=0, w_offset=0, carried=y_a, name="combine_prompt_b")
    y_s = _combine(g_b, w2, yp_s, *ln2, n=n_s, g_offset=n_p - half, w_offset=n_p, name="combine_sample")
    return (y_p.reshape(bp, sp, D_MODEL), y_s.reshape(bs, ts, D_MODEL), new_kp, new_vp, new_ks, new_vs, new_gs)
```

```python
import functools

import jax
import jax.numpy as jnp
import numpy as np
from jax import lax
from jax.experimental import pallas as pl
from jax.experimental.pallas import tpu as pltpu
from jax.experimental.pallas import tpu_sc as plsc

D_MODEL = 1024
HEAD_DIM = 64
ATT_HEADS = 8
KV_HEADS = 2
Q_PER_KV = ATT_HEADS // KV_HEADS
GM_HEADS = 8
ATT_WIDTH = ATT_HEADS * HEAD_DIM
KV_WIDTH = KV_HEADS * HEAD_DIM
GM_WIDTH = GM_HEADS * HEAD_DIM
ROPE_WIDTH = ATT_WIDTH + KV_WIDTH
IN_WIDTH = ATT_WIDTH + 2 * KV_WIDTH + 2 * GM_WIDTH
WINDOW = 128
CHUNK = 128
PAST_LEN = 16384
ROPE_THETA = 10000.0
ATT_SCALE = HEAD_DIM ** -0.5
N_EXPERTS = 256
TOP_K = 8
N_GROUPS = 8
TOPK_GROUPS = 4
F_EXPERT = 256
ROUTED_SCALE = 2.5
LN_EPS = 1e-5
DEPTH = 1
ALPHA = (2.0 * DEPTH) ** 0.25

LANES = 128
SLABS = 8
PACKED = D_MODEL // 2
WORD_SLABS = PACKED // LANES
ROW_TILE = 512
EXPERT_ROWS = 256
EXPERT_RING = 8
COMBINE_ROWS = 512
ROUTE_TILE = 512
SAMPLE_SEQS_PER_STEP = 16
PROMPT_BLOCKS_PER_STEP = 2
SC_CORES = 2
SC_SUBCORES = 16
SC_WINDOW = 128
VMEM_LIMIT = 56 * 1024 * 1024

F32 = jnp.float32
BF16 = jnp.bfloat16


def _params(n_axes):
    return pltpu.CompilerParams(dimension_semantics=("arbitrary",) * n_axes, vmem_limit_bytes=VMEM_LIMIT)


def _layer_norm(x, g, b):
    mu = jnp.mean(x, axis=-1, keepdims=True)
    xc = x - mu
    var = jnp.mean(xc * xc, axis=-1, keepdims=True)
    return xc * lax.rsqrt(var + LN_EPS) * g + b


def _pack_halves(x):
    xb = x.astype(BF16)
    lo = lax.bitcast_convert_type(xb[:, :PACKED].astype(F32), jnp.uint32)
    hi = lax.bitcast_convert_type(xb[:, PACKED:].astype(F32), jnp.uint32)
    return (lo >> 16) | hi


def _unpack_halves(w):
    lo = lax.bitcast_convert_type(w << 16, F32)
    hi = lax.bitcast_convert_type(w & jnp.uint32(0xFFFF0000), F32)
    return lo, hi


def _store_row_tiles(ref2d, x):
    m, slabs = x.shape[0], x.shape[1] // LANES
    for s in range(slabs):
        ref2d[pl.ds(s, m, stride=slabs), :] = x[:, s * LANES:(s + 1) * LANES]


def _load_row_tiles(ref2d, m, slabs):
    return jnp.concatenate([ref2d[pl.ds(s, m, stride=slabs), :] for s in range(slabs)], axis=1)


def _gelu(x):
    return 0.5 * x * (1.0 + lax.erf(x * np.float32(np.sqrt(0.5))))


def _in_proj_kernel(x_ref, w_ref, cos_ref, sa_ref, sb_ref, g_ref, b_ref, q_ref, k_ref, v_ref, gu_ref, gv_ref):
    x = x_ref[...].astype(BF16)
    zr = jnp.dot(x, w_ref[:, :ROPE_WIDTH], preferred_element_type=F32)
    pieces = []
    for c in range(ROPE_WIDTH // LANES):
        zc = zr[:, c * LANES:(c + 1) * LANES]
        tl = slice(0, LANES) if c < ATT_WIDTH // LANES else slice(LANES, 2 * LANES)
        pieces.append(zc * cos_ref[:, tl]
                      + pltpu.roll(zc, LANES - HEAD_DIM // 2, 1) * sa_ref[:, tl]
                      + pltpu.roll(zc, HEAD_DIM // 2, 1) * sb_ref[:, tl])
    for c in range(ATT_WIDTH // LANES):
        q_ref[:, c * LANES:(c + 1) * LANES] = pieces[c].astype(q_ref.dtype)
    k_ref[...] = pieces[ATT_WIDTH // LANES]
    v_ref[...] = jnp.dot(x, w_ref[:, ROPE_WIDTH:ROPE_WIDTH + KV_WIDTH], preferred_element_type=F32)
    g0 = ROPE_WIDTH + KV_WIDTH
    zu = jnp.dot(x, w_ref[:, g0:g0 + GM_WIDTH], preferred_element_type=F32)
    gu_ref[...] = _gelu(zu).astype(gu_ref.dtype)
    zv = jnp.dot(x, w_ref[:, g0 + GM_WIDTH:g0 + 2 * GM_WIDTH], preferred_element_type=F32)
    gv = _layer_norm(_gelu(zv), g_ref[...], b_ref[...])
    gv_ref[...] = gv.astype(gv_ref.dtype)


def _in_proj(x2, w_in_b, tabs, ln_g, ln_b, *, tm, gv_dtype, name):
    n = x2.shape[0]
    cos_t, sa_t, sb_t = tabs
    period = cos_t.shape[0] // tm
    row = lambda i: (i, 0)
    tab = lambda i: (i % period, 0)
    fixed = lambda i: (0, 0)
    return pl.pallas_call(
        _in_proj_kernel,
        grid=(n // tm,),
        in_specs=[
            pl.BlockSpec((tm, D_MODEL), row),
            pl.BlockSpec((D_MODEL, IN_WIDTH), fixed),
            pl.BlockSpec((tm, 2 * LANES), tab),
            pl.BlockSpec((tm, 2 * LANES), tab),
            pl.BlockSpec((tm, 2 * LANES), tab),
            pl.BlockSpec((1, GM_WIDTH), fixed),
            pl.BlockSpec((1, GM_WIDTH), fixed),
        ],
        out_specs=[
            pl.BlockSpec((tm, ATT_WIDTH), row),
            pl.BlockSpec((tm, KV_WIDTH), row),
            pl.BlockSpec((tm, KV_WIDTH), row),
            pl.BlockSpec((tm, GM_WIDTH), row),
            pl.BlockSpec((tm, GM_WIDTH), row),
        ],
        out_shape=[
            jax.ShapeDtypeStruct((n, ATT_WIDTH), BF16),
            jax.ShapeDtypeStruct((n, KV_WIDTH), F32),
            jax.ShapeDtypeStruct((n, KV_WIDTH), F32),
            jax.ShapeDtypeStruct((n, GM_WIDTH), BF16),
            jax.ShapeDtypeStruct((n, GM_WIDTH), gv_dtype),
        ],
        compiler_params=_params(1),
        name=name,
    )(x2, w_in_b, cos_t, sa_t, sb_t, ln_g, ln_b)


def _rope_tables(pos):
    half = HEAD_DIM // 2
    lane = jnp.arange(2 * LANES, dtype=jnp.int32)
    inv = ROPE_THETA ** (-(lane % half).astype(F32) * 2.0 / HEAD_DIM)
    ang = pos.astype(F32)[:, None] * inv[None, :]
    scale = jnp.where(lane < LANES, ATT_SCALE, 1.0).astype(F32)[None, :]
    first_half = ((lane % HEAD_DIM) < half)[None, :]
    cos, sin = jnp.cos(ang) * scale, jnp.sin(ang) * scale
    return cos, jnp.where(first_half, -sin, 0.0), jnp.where(first_half, 0.0, sin)


def _attn_kernel(sink_ref, q_ref, kc_ref, vc_ref, kp_ref, vp_ref, o_ref, *, tq, seqs, chain, stack,
                 first_block_has_no_prev):
    nk = WINDOW + tq
    rows = stack * tq
    qi = lax.broadcasted_iota(jnp.int32, (rows, nk), 0) & (tq - 1)
    ks = lax.broadcasted_iota(jnp.int32, (rows, nk), 1)
    band = (ks >= qi) & (ks <= qi + WINDOW)
    first_mask = band & ((pl.program_id(1) > 0) | (ks >= WINDOW)) if first_block_has_no_prev else band
    sinks = [jnp.concatenate([jnp.full((tq, 1), sink_ref[h0 + j], F32) for j in range(stack)], axis=0)
             for h0 in range(0, ATT_HEADS, stack)]
    for b in range(seqs * chain):
        qrows = slice(b * tq, (b + 1) * tq)
        if chain > 1 and b > 0:
            kprev, vprev = kc_ref[(b - 1) * tq:b * tq, :], vc_ref[(b - 1) * tq:b * tq, :]
        else:
            kprev, vprev = kp_ref[b * WINDOW:(b + 1) * WINDOW, :], vp_ref[b * WINDOW:(b + 1) * WINDOW, :]
        mask = band if (chain > 1 and b > 0) else first_mask
        q = q_ref[qrows, :]
        kk = jnp.concatenate([kprev, kc_ref[qrows, :]], axis=0).astype(BF16)
        vv = jnp.concatenate([vprev, vc_ref[qrows, :]], axis=0).astype(BF16)
        outs = []
        for i, h0 in enumerate(range(0, ATT_HEADS, stack)):
            g = h0 // Q_PER_KV
            kg = kk[:, g * HEAD_DIM:(g + 1) * HEAD_DIM]
            vg = vv[:, g * HEAD_DIM:(g + 1) * HEAD_DIM]
            qg = jnp.concatenate([q[:, (h0 + j) * HEAD_DIM:(h0 + j + 1) * HEAD_DIM] for j in range(stack)], axis=0)
            s = lax.dot_general(qg, kg, (((1,), (1,)), ((), ())), preferred_element_type=F32)
            s = jnp.where(mask, s, -jnp.inf)
            m = jnp.maximum(jnp.max(s, axis=-1, keepdims=True), sinks[i])
            p = jnp.exp(s - m)
            denom = jnp.sum(p, axis=-1, keepdims=True) + jnp.exp(sinks[i] - m)
            o = jnp.dot((p / denom).astype(BF16), vg, preferred_element_type=F32)
            outs.extend(o[j * tq:(j + 1) * tq, :] for j in range(stack))
        o_ref[qrows, :] = jnp.concatenate(outs, axis=1).astype(o_ref.dtype)


def _attention(sink, q2, k2, v2, kprev2, vprev2, *, batch, nb, tq, seqs, chain, prev_blocks, first_block_has_no_prev,
               name):
    assert tq & (tq - 1) == 0 and batch % seqs == 0 and (seqs == 1 or nb == prev_blocks == 1)
    assert (seqs == 1 or chain == 1) and nb % chain == 0 and (chain == 1 or tq == WINDOW)
    steps = nb // chain
    cur = lambda b, n, s: (b * steps + n, 0)
    prev = lambda b, n, s: (b * prev_blocks + jnp.maximum(n * chain - 1, 0), 0)
    stack = Q_PER_KV if Q_PER_KV * tq <= WINDOW else 1
    kern = functools.partial(_attn_kernel, tq=tq, seqs=seqs, chain=chain, stack=stack,
                             first_block_has_no_prev=first_block_has_no_prev)
    return pl.pallas_call(
        kern,
        grid_spec=pltpu.PrefetchScalarGridSpec(
            num_scalar_prefetch=1,
            grid=(batch // seqs, steps),
            in_specs=[
                pl.BlockSpec((seqs * chain * tq, ATT_WIDTH), cur),
                pl.BlockSpec((seqs * chain * tq, KV_WIDTH), cur),
                pl.BlockSpec((seqs * chain * tq, KV_WIDTH), cur),
                pl.BlockSpec((seqs * WINDOW, KV_WIDTH), prev),
                pl.BlockSpec((seqs * WINDOW, KV_WIDTH), prev),
            ],
            out_specs=pl.BlockSpec((seqs * chain * tq, ATT_WIDTH), cur),
        ),
        out_shape=jax.ShapeDtypeStruct(q2.shape, BF16),
        compiler_params=_params(2),
        name=name,
    )(sink, q2, k2, v2, kprev2, vprev2)


def _gated_chunks(gu_ref, gv_ref, w_ref, b_ref, chunk, n_chunks):
    ri = lax.broadcasted_iota(jnp.int32, (chunk, chunk), 0)
    ci = lax.broadcasted_iota(jnp.int32, (chunk, chunk), 1)
    ws = [jnp.where(ci <= ri, w_ref[h], 0.0).astype(BF16) for h in range(GM_HEADS)]
    out = []
    for c in range(n_chunks):
        rows = slice(c * chunk, (c + 1) * chunk)
        gv = gv_ref[rows, :].astype(BF16)
        sv = jnp.concatenate(
            [jnp.dot(ws[h], gv[:, h * HEAD_DIM:(h + 1) * HEAD_DIM], preferred_element_type=F32)
             for h in range(GM_HEADS)], axis=1)
        out.append((gu_ref[rows, :].astype(F32) * (sv + b_ref[...])).astype(BF16))
    return out


def _gate_kernel(gu_ref, gv_ref, w_ref, b_ref, o_ref, *, chunk, n_chunks):
    for c, gm in enumerate(_gated_chunks(gu_ref, gv_ref, w_ref, b_ref, chunk, n_chunks)):
        o_ref[c * chunk:(c + 1) * chunk, :] = gm


def _gate(gu2, gv2, w_s, b_tab, *, chunk, n_chunks, name):
    n = gu2.shape[0]
    tm = chunk * n_chunks
    row = lambda i: (i, 0)
    kern = functools.partial(_gate_kernel, chunk=chunk, n_chunks=n_chunks)
    return pl.pallas_call(
        kern,
        grid=(n // tm,),
        in_specs=[
            pl.BlockSpec((tm, GM_WIDTH), row),
            pl.BlockSpec((tm, GM_WIDTH), row),
            pl.BlockSpec((GM_HEADS, chunk, chunk), lambda i: (0, 0, 0)),
            pl.BlockSpec((chunk, GM_WIDTH), lambda i: (0, 0)),
        ],
        out_specs=pl.BlockSpec((tm, GM_WIDTH), row),
        out_shape=jax.ShapeDtypeStruct((n, GM_WIDTH), BF16),
        compiler_params=_params(1),
        name=name,
    )(gu2, gv2, w_s, b_tab)


def _gate_short_kernel(gu_ref, gv_ref, coef_ref, b_ref, o_ref, *, length):
    n = gv_ref.shape[0]
    gv = gv_ref[...].astype(F32)
    tiles = lambda x: x.reshape(n // SLABS, SLABS, GM_WIDTH)
    sv = tiles(gv) * coef_ref[0][None] + b_ref[...][None]
    for d in range(1, length):
        sv = sv + tiles(pltpu.roll(gv, d, 0)) * coef_ref[d][None]
    o_ref[...] = (gu_ref[...].astype(F32) * sv.reshape(n, GM_WIDTH)).astype(o_ref.dtype)


def _gate_short(gu2, gv2, w_s, b_s, *, length, name):
    n = gu2.shape[0]
    assert SLABS % length == 0 and n % SLABS == 0
    pos = jnp.arange(length)
    lag = jnp.arange(length)
    src = pos[None, :] - lag[:, None]
    coef = jnp.where(src >= 0, w_s[:, pos[None, :], jnp.maximum(src, 0)], 0.0)
    coef = jnp.repeat(coef.transpose(1, 2, 0), HEAD_DIM, axis=2)
    coef = jnp.tile(coef, (1, SLABS // length, 1))
    bias = jnp.tile(jnp.repeat(b_s[:, :length].T, HEAD_DIM, axis=1), (SLABS // length, 1))
    whole = lambda shape: pl.BlockSpec(shape, lambda i: (0,) * len(shape))
    return pl.pallas_call(
        functools.partial(_gate_short_kernel, length=length),
        grid=(1,),
        in_specs=[whole((n, GM_WIDTH)), whole((n, GM_WIDTH)), whole((length, SLABS, GM_WIDTH)),
                  whole((SLABS, GM_WIDTH))],
        out_specs=whole((n, GM_WIDTH)),
        out_shape=jax.ShapeDtypeStruct((n, GM_WIDTH), BF16),
        compiler_params=_params(1),
        name=name,
    )(gu2, gv2, coef, bias)


def _out_proj_kernel(att_ref, gm_ref, x_ref, wo_ref, g_ref, b_ref, rh_ref, rl_ref, wg_ref, wu_ref, wd_ref, *rest,
                     gate_chunks):
    yp_ref, hp_ref, lg_ref = rest[-3:]
    if gate_chunks:
        gm = jnp.concatenate(_gated_chunks(gm_ref, *rest[:3], CHUNK, gate_chunks), axis=0)
    else:
        gm = gm_ref[...]
    mix = jnp.dot(att_ref[...], wo_ref[:ATT_WIDTH, :], preferred_element_type=F32)
    mix = mix + jnp.dot(gm, wo_ref[ATT_WIDTH:, :], preferred_element_type=F32)
    h = _layer_norm(ALPHA * x_ref[...] + mix, g_ref[...], b_ref[...])
    _store_row_tiles(hp_ref, _pack_halves(h))
    h_hi = h.astype(BF16)
    a = jnp.dot(h_hi, wg_ref[...], preferred_element_type=F32)
    u = jnp.dot(h_hi, wu_ref[...], preferred_element_type=F32)
    shared = jnp.dot((a * jax.nn.sigmoid(a) * u).astype(BF16), wd_ref[...], preferred_element_type=F32)
    yp_ref[...] = ALPHA * h + shared
    h_lo = (h - h_hi.astype(F32)).astype(BF16)
    nt = (((1,), (1,)), ((), ()))
    lg = lax.dot_general(rh_ref[...], h_hi, nt, preferred_element_type=F32)
    lg = lg + lax.dot_general(rh_ref[...], h_lo, nt, preferred_element_type=F32)
    lg = lg + lax.dot_general(rl_ref[...], h_hi, nt, preferred_element_type=F32)
    lg_ref[...] = lg


def _out_proj(att2, gm2, x2, w_out_b, ln_g, ln_b, r_hi, r_lo, wg_b, wu_b, wd_b, *, name, gate=None):
    n = x2.shape[0]
    tm = ROW_TILE
    row = lambda i: (i, 0)
    fixed = lambda i: (0, 0)
    gate_specs = [] if gate is None else [
        pl.BlockSpec((tm, GM_WIDTH), row),
        pl.BlockSpec((GM_HEADS, CHUNK, CHUNK), lambda i: (0, 0, 0)),
        pl.BlockSpec((CHUNK, GM_WIDTH), fixed),
    ]
    return pl.pallas_call(
        functools.partial(_out_proj_kernel, gate_chunks=0 if gate is None else tm // CHUNK),
        grid=(n // tm,),
        in_specs=[
            pl.BlockSpec((tm, ATT_WIDTH), row),
            pl.BlockSpec((tm, GM_WIDTH), row),
            pl.BlockSpec((tm, D_MODEL), row),
            pl.BlockSpec((D_MODEL, D_MODEL), fixed),
            pl.BlockSpec((1, D_MODEL), fixed),
            pl.BlockSpec((1, D_MODEL), fixed),
            pl.BlockSpec((N_EXPERTS, D_MODEL), fixed),
            pl.BlockSpec((N_EXPERTS, D_MODEL), fixed),
            pl.BlockSpec((D_MODEL, F_EXPERT), fixed),
            pl.BlockSpec((D_MODEL, F_EXPERT), fixed),
            pl.BlockSpec((F_EXPERT, D_MODEL), fixed),
        ] + gate_specs,
        out_specs=[
            pl.BlockSpec((tm, D_MODEL), row),
            pl.BlockSpec((tm * WORD_SLABS, LANES), row),
            pl.BlockSpec((N_EXPERTS, tm), lambda i: (0, i)),
        ],
        out_shape=[
            jax.ShapeDtypeStruct((n, D_MODEL), F32),
            jax.ShapeDtypeStruct((n * WORD_SLABS, LANES), jnp.uint32),
            jax.ShapeDtypeStruct((N_EXPERTS, n), F32),
        ],
        compiler_params=_params(1),
        name=name,
    )(att2, gm2, x2, w_out_b, ln_g, ln_b, r_hi, r_lo, wg_b, wu_b, wd_b, *(gate or ()))


def _experts_kernel(first_ref, nblk_ref, nu_ref, x_hbm, wg_ref, wu_ref, wd_ref, o_hbm, xbuf, obuf, wg_s, wu_s, wd_s,
                    in_sem, out_sem, fill_sem, *, n_blocks):
    e = pl.program_id(0)
    nb = nblk_ref[e]
    b0 = first_ref[e]
    n_used = nu_ref[0]
    ahead = EXPERT_RING // 2

    def rows_of(block):
        return pl.ds(pl.multiple_of(block * EXPERT_ROWS, EXPERT_ROWS), EXPERT_ROWS)

    def slot_of(block):
        return block & (EXPERT_RING - 1)

    def slab(s):
        return pl.ds(s * LANES, LANES)

    def in_copies(block):
        slot = slot_of(block)
        return [pltpu.make_async_copy(x_hbm.at[rows_of(block), s, :], xbuf.at[slot, :, slab(s)], in_sem.at[slot])
                for s in range(WORD_SLABS)]

    def out_copies(block, slot=None, sem=None):
        slot = slot_of(block) if slot is None else slot
        sem = out_sem.at[slot] if sem is None else sem
        return [pltpu.make_async_copy(obuf.at[slot, :, slab(s)], o_hbm.at[rows_of(block), s, :], sem)
                for s in range(WORD_SLABS)]

    def start(copies):
        for cp in copies:
            cp.start()

    def wait(copies):
        for cp in copies:
            cp.wait()

    @pl.when(e == 0)
    def _():
        for j in range(ahead):
            pl.when(j < n_used)(lambda j=j: start(in_copies(j)))

    def process(block, count):
        blocks = [block + j for j in range(count)]
        for blk in blocks:
            wait(in_copies(blk))
        x = [xbuf[slot_of(blk)] for blk in blocks]
        xl, xh = _unpack_halves(x[0] if count == 1 else jnp.concatenate(x, axis=0))
        xl, xh = xl.astype(BF16), xh.astype(BF16)
        for blk in blocks:
            pl.when(blk + ahead < n_used)(lambda blk=blk: start(in_copies(blk + ahead)))
            pl.when(blk >= ahead)(lambda blk=blk: wait(out_copies(blk - ahead)))
        a = (jnp.dot(xl, wg_s[:PACKED, :], preferred_element_type=F32)
             + jnp.dot(xh, wg_s[PACKED:, :], preferred_element_type=F32))
        u = (jnp.dot(xl, wu_s[:PACKED, :], preferred_element_type=F32)
             + jnp.dot(xh, wu_s[PACKED:, :], preferred_element_type=F32))
        hb = (a * jax.nn.sigmoid(a) * u).astype(BF16)
        o = _pack_halves(jnp.dot(hb, wd_s[...], preferred_element_type=F32))
        for j, blk in enumerate(blocks):
            obuf[slot_of(blk)] = o[j * EXPERT_ROWS:(j + 1) * EXPERT_ROWS, :]
            start(out_copies(blk))

    @pl.when(nb > 0)
    def _():
        wg_s[...] = wg_ref[0].astype(BF16)
        wu_s[...] = wu_ref[0].astype(BF16)
        wd_s[...] = wd_ref[0].astype(BF16)

        def pair(i, carry):
            process(b0 + 2 * i, 2)
            return carry

        lax.fori_loop(0, nb // 2, pair, 0)
        pl.when(nb % 2 == 1)(lambda: process(b0 + nb - 1, 1))

    @pl.when(e == N_EXPERTS - 1)
    def _():
        for j in range(ahead):
            pl.when(n_used - 1 - j >= 0)(lambda j=j: wait(out_copies(n_used - 1 - j)))
        obuf[0] = jnp.zeros((EXPERT_ROWS, PACKED), jnp.uint32)

        def on_unused_blocks(fn):
            def body(b, c):
                fn(out_copies(b, slot=0, sem=fill_sem))
                return c
            lax.fori_loop(n_used, n_blocks, body, 0)

        on_unused_blocks(start)
        on_unused_blocks(wait)


def _experts(first_block, n_expert_blocks, n_used, x_sorted, w_gate_e, w_up_e, w_down_e):
    rows = x_sorted.shape[0]
    wmap = lambda e, *_: (e, 0, 0)
    kern = functools.partial(_experts_kernel, n_blocks=rows // EXPERT_ROWS)
    return pl.pallas_call(
        kern,
        grid_spec=pltpu.PrefetchScalarGridSpec(
            num_scalar_prefetch=3,
            grid=(N_EXPERTS,),
            in_specs=[
                pl.BlockSpec(memory_space=pl.ANY),
                pl.BlockSpec((1, D_MODEL, F_EXPERT), wmap),
                pl.BlockSpec((1, D_MODEL, F_EXPERT), wmap),
                pl.BlockSpec((1, F_EXPERT, D_MODEL), wmap),
            ],
            out_specs=pl.BlockSpec(memory_space=pl.ANY),
            scratch_shapes=[
                pltpu.VMEM((EXPERT_RING, EXPERT_ROWS, PACKED), jnp.uint32),
                pltpu.VMEM((EXPERT_RING, EXPERT_ROWS, PACKED), jnp.uint32),
                pltpu.VMEM((D_MODEL, F_EXPERT), BF16),
                pltpu.VMEM((D_MODEL, F_EXPERT), BF16),
                pltpu.VMEM((F_EXPERT, D_MODEL), BF16),
                pltpu.SemaphoreType.DMA((EXPERT_RING,)),
                pltpu.SemaphoreType.DMA((EXPERT_RING,)),
                pltpu.SemaphoreType.DMA,
            ],
        ),
        out_shape=jax.ShapeDtypeStruct(x_sorted.shape, x_sorted.dtype),
        compiler_params=_params(1),
        name="experts",
    )(first_block, n_expert_blocks, n_used, x_sorted, w_gate_e, w_up_e, w_down_e)


def _gather_sc(dest_t, out_sorted):
    win = SC_WINDOW
    n = dest_t.shape[1]
    n_win = n // win
    workers = SC_CORES * SC_SUBCORES
    mesh = plsc.VectorSubcoreMesh(core_axis_name="c", subcore_axis_name="s")

    @functools.partial(
        pl.kernel, mesh=mesh, name="gather_sc",
        out_type=jax.ShapeDtypeStruct((TOP_K, n, WORD_SLABS, LANES), jnp.uint32),
        scratch_types=[pltpu.VMEM((TOP_K, win), jnp.int32), pltpu.VMEM((win, WORD_SLABS, LANES), jnp.uint32)])
    def gather(dest_hbm, os_hbm, g_hbm, idx_v, rows_v):
        wid = lax.axis_index("s") * SC_CORES + lax.axis_index("c")

        @pl.loop(0, (n_win - wid + workers - 1) // workers)
        def _(i):
            base = pl.multiple_of((wid + i * workers) * win, win)
            pltpu.sync_copy(dest_hbm.at[:, pl.ds(base, win)], idx_v)
            for k in range(TOP_K):
                pltpu.sync_copy(os_hbm.at[idx_v.at[k]], rows_v)
                pltpu.sync_copy(rows_v, g_hbm.at[k, pl.ds(base, win)])

    return gather(dest_t, out_sorted)


def _combine_kernel(g_ref, w_ref, yp_ref, ln_g_ref, ln_b_ref, y_ref):
    t = y_ref.shape[0]
    w = w_ref[...]
    lo_acc = jnp.zeros((t, PACKED), F32)
    hi_acc = jnp.zeros((t, PACKED), F32)
    for k in range(TOP_K):
        lo, hi = _unpack_halves(_load_row_tiles(g_ref.at[k], t, WORD_SLABS))
        lo_acc = lo_acc + w[:, k:k + 1] * lo
        hi_acc = hi_acc + w[:, k:k + 1] * hi
    routed = jnp.concatenate([lo_acc, hi_acc], axis=1)
    y_ref[...] = _layer_norm(yp_ref[...] + routed, ln_g_ref[...], ln_b_ref[...])


def _combine(g, w2, y_part, ln_g, ln_b, *, n, row_offset, name):
    tm = COMBINE_ROWS
    off = row_offset // tm
    fixed = lambda i: (0, 0)
    return pl.pallas_call(
        _combine_kernel,
        grid=(n // tm,),
        in_specs=[
            pl.BlockSpec((TOP_K, tm * WORD_SLABS, LANES), lambda i: (0, i + off, 0)),
            pl.BlockSpec((tm, TOP_K), lambda i: (i + off, 0)),
            pl.BlockSpec((tm, D_MODEL), lambda i: (i, 0)),
            pl.BlockSpec((1, D_MODEL), fixed),
            pl.BlockSpec((1, D_MODEL), fixed),
        ],
        out_specs=pl.BlockSpec((tm, D_MODEL), lambda i: (i, 0)),
        out_shape=jax.ShapeDtypeStruct((n, D_MODEL), F32),
        compiler_params=_params(1),
        name=name,
    )(g, w2, y_part, ln_g, ln_b)


def _route_kernel(lgp_ref, lgs_ref, bias_ref, eidx_ref, w_ref, rank_ref, cnt_ref, carry_ref, *, prompt_tiles):
    @pl.when(pl.program_id(0) == 0)
    def _():
        carry_ref[...] = jnp.zeros_like(carry_ref)

    t = lgp_ref.shape[1]
    gsz = N_EXPERTS // N_GROUPS
    neg = -jnp.inf
    s = jax.nn.sigmoid(jnp.where(pl.program_id(0) < prompt_tiles, lgp_ref[...], lgs_ref[...]))
    biased = s + bias_ref[...]
    io_g = lax.broadcasted_iota(jnp.int32, (gsz, t), 0)
    grp_rows = []
    for g in range(N_GROUPS):
        blk = biased[g * gsz:(g + 1) * gsz, :]
        m1 = jnp.max(blk, axis=0, keepdims=True)
        i1 = jnp.min(jnp.where(blk == m1, io_g, gsz), axis=0, keepdims=True)
        m2 = jnp.max(jnp.where(io_g == i1, neg, blk), axis=0, keepdims=True)
        grp_rows.append(m1 + m2)
    gs = jnp.concatenate(grp_rows, axis=0)
    io8 = lax.broadcasted_iota(jnp.int32, (N_GROUPS, t), 0)
    gsel = jnp.zeros((N_GROUPS, t), jnp.int32)
    for _ in range(TOPK_GROUPS):
        m = jnp.max(gs, axis=0, keepdims=True)
        gi = jnp.min(jnp.where(gs == m, io8, N_GROUPS), axis=0, keepdims=True)
        hit = io8 == gi
        gsel = jnp.where(hit, 1, gsel)
        gs = jnp.where(hit, neg, gs)
    masked = jnp.concatenate(
        [jnp.where(gsel[g:g + 1, :] > 0, biased[g * gsz:(g + 1) * gsz, :], neg) for g in range(N_GROUPS)], axis=0)

    eio = lax.broadcasted_iota(jnp.int32, (N_EXPERTS, t), 0)
    cur = masked
    idx_rows, w_rows = [], []
    for _ in range(TOP_K):
        m = jnp.max(cur, axis=0, keepdims=True)
        idx = jnp.min(jnp.where(cur == m, eio, N_EXPERTS), axis=0, keepdims=True)
        hit = eio == idx
        w_rows.append(jnp.sum(jnp.where(hit, s, 0.0), axis=0, keepdims=True))
        cur = jnp.where(hit, neg, cur)
        idx_rows.append(idx)
    sel = jnp.where(cur != masked, 1.0, 0.0)

    tri = jnp.where(lax.broadcasted_iota(jnp.int32, (t, t), 0) < lax.broadcasted_iota(jnp.int32, (t, t), 1), 1.0, 0.0)
    pref = jnp.dot(sel.astype(BF16), tri.astype(BF16), preferred_element_type=F32) + carry_ref[...]
    rank_rows = [jnp.sum(jnp.where(eio == idx_rows[k], pref, 0.0), axis=0, keepdims=True) for k in range(TOP_K)]
    carry_ref[...] += jnp.sum(sel, axis=1, keepdims=True)

    wk = jnp.concatenate(w_rows, axis=0)
    eidx_ref[...] = jnp.concatenate(idx_rows, axis=0)
    w_ref[...] = wk / jnp.sum(wk, axis=0, keepdims=True) * ROUTED_SCALE
    rank_ref[...] = jnp.concatenate(rank_rows, axis=0).astype(jnp.int32)
    cnt_ref[...] = carry_ref[...].astype(jnp.int32)


def _route(logits_p, logits_s, bias_col):
    t = ROUTE_TILE
    prompt_tiles = logits_p.shape[1] // t
    n = logits_p.shape[1] + logits_s.shape[1]
    col = lambda i: (0, i)
    fixed = lambda i: (0, 0)
    kern = functools.partial(_route_kernel, prompt_tiles=prompt_tiles)
    return pl.pallas_call(
        kern,
        grid=(n // t,),
        in_specs=[pl.BlockSpec((N_EXPERTS, t), lambda i: (0, jnp.minimum(i, prompt_tiles - 1))),
                  pl.BlockSpec((N_EXPERTS, t), lambda i: (0, jnp.maximum(i - prompt_tiles, 0))),
                  pl.BlockSpec((N_EXPERTS, 1), fixed)],
        out_specs=[
            pl.BlockSpec((TOP_K, t), col),
            pl.BlockSpec((TOP_K, t), col),
            pl.BlockSpec((TOP_K, t), col),
            pl.BlockSpec((N_EXPERTS, 1), fixed),
        ],
        out_shape=[
            jax.ShapeDtypeStruct((TOP_K, n), jnp.int32),
            jax.ShapeDtypeStruct((TOP_K, n), F32),
            jax.ShapeDtypeStruct((TOP_K, n), jnp.int32),
            jax.ShapeDtypeStruct((N_EXPERTS, 1), jnp.int32),
        ],
        scratch_shapes=[pltpu.VMEM((N_EXPERTS, 1), F32)],
        compiler_params=_params(1),
        name="route",
    )(logits_p, logits_s, bias_col)


def _dest_kernel(eidx_ref, rank_ref, start_ref, dest_ref):
    t = eidx_ref.shape[1]
    eio = lax.broadcasted_iota(jnp.int32, (N_EXPERTS, t), 0)
    start = start_ref[...]
    rows = [jnp.sum(jnp.where(eio == eidx_ref[k:k + 1, :], start, 0.0), axis=0, keepdims=True) for k in range(TOP_K)]
    dest_ref[...] = jnp.concatenate(rows, axis=0).astype(jnp.int32) + rank_ref[...]


def _dest(eidx_t, rank_t, pad_start_col):
    n = eidx_t.shape[1]
    t = ROW_TILE
    col = lambda i: (0, i)
    return pl.pallas_call(
        _dest_kernel,
        grid=(n // t,),
        in_specs=[pl.BlockSpec((TOP_K, t), col), pl.BlockSpec((TOP_K, t), col),
                  pl.BlockSpec((N_EXPERTS, 1), lambda i: (0, 0))],
        out_specs=pl.BlockSpec((TOP_K, t), col),
        out_shape=jax.ShapeDtypeStruct((TOP_K, n), jnp.int32),
        compiler_params=_params(1),
        name="dest",
    )(eidx_t, rank_t, pad_start_col)


def _dispatch_sc(dest_t, pk_p, pk_s, *, rows):
    win = SC_WINDOW
    p_win, s_win = pk_p.shape[0] // win, pk_s.shape[0] // win
    n_win = p_win + s_win
    workers = SC_CORES * SC_SUBCORES
    mesh = plsc.VectorSubcoreMesh(core_axis_name="c", subcore_axis_name="s")

    @functools.partial(
        pl.kernel, mesh=mesh, name="dispatch_sc",
        out_type=jax.ShapeDtypeStruct((rows, WORD_SLABS, LANES), jnp.uint32),
        scratch_types=[pltpu.VMEM((TOP_K, win), jnp.int32), pltpu.VMEM((win, WORD_SLABS, LANES), jnp.uint32)])
    def scatter(dest_hbm, pkp_hbm, pks_hbm, xs_hbm, idx_v, rows_v):
        wid = lax.axis_index("s") * SC_CORES + lax.axis_index("c")

        @pl.loop(0, (n_win - wid + workers - 1) // workers)
        def _(i):
            g = wid + i * workers

            @pl.when(g < p_win)
            def _():
                pltpu.sync_copy(pkp_hbm.at[pl.ds(pl.multiple_of(g * win, win), win)], rows_v)

            @pl.when(g >= p_win)
            def _():
                pltpu.sync_copy(pks_hbm.at[pl.ds(pl.multiple_of((g - p_win) * win, win), win)], rows_v)

            pltpu.sync_copy(dest_hbm.at[:, pl.ds(pl.multiple_of(g * win, win), win)], idx_v)
            for k in range(TOP_K):
                pltpu.sync_copy(rows_v, xs_hbm.at[idx_v.at[k]])

    return scatter(dest_t, pk_p, pk_s)


def _fill_kernel(fill_ref, len_ref, nu_ref, xs_in_ref, xs_ref, zbuf, fill_sem, *, n_blocks):
    del xs_in_ref
    zbuf[...] = jnp.zeros_like(zbuf)

    def fill_copy(row0, size):
        return pltpu.make_async_copy(zbuf.at[pl.ds(0, size)], xs_ref.at[pl.ds(row0, size)], fill_sem)

    def on_padding(fn):
        def body(e, c):
            base, length = fill_ref[e], len_ref[e]
            size = EXPERT_ROWS // 2
            while size >= 1:
                piece = fill_copy(base + (length & ~(2 * size - 1)), size)
                pl.when((length & size) != 0)(functools.partial(fn, piece))
                size //= 2
            return c
        lax.fori_loop(0, N_EXPERTS, body, 0)

    def on_unused_blocks(fn):
        lax.fori_loop(nu_ref[0], n_blocks, lambda b, c: (fn(fill_copy(b * EXPERT_ROWS, EXPERT_ROWS)), c)[1], 0)

    on_padding(lambda cp: cp.start())
    on_unused_blocks(lambda cp: cp.start())
    on_padding(lambda cp: cp.wait())
    on_unused_blocks(lambda cp: cp.wait())


def _fill_padding(fill_start, fill_len, n_used, x_sorted):
    n_blocks = x_sorted.shape[0] // EXPERT_ROWS
    return pl.pallas_call(
        functools.partial(_fill_kernel, n_blocks=n_blocks),
        grid_spec=pltpu.PrefetchScalarGridSpec(
            num_scalar_prefetch=3,
            grid=(1,),
            in_specs=[pl.BlockSpec(memory_space=pl.ANY)],
            out_specs=pl.BlockSpec(memory_space=pl.ANY),
            scratch_shapes=[pltpu.VMEM((EXPERT_ROWS, WORD_SLABS, LANES), jnp.uint32), pltpu.SemaphoreType.DMA],
        ),
        out_shape=jax.ShapeDtypeStruct(x_sorted.shape, x_sorted.dtype),
        input_output_aliases={3: 0},
        compiler_params=_params(1),
        name="fill_padding",
    )(fill_start, fill_len, n_used, x_sorted)


def _block_plan(counts):
    padded = (counts + EXPERT_ROWS - 1) // EXPERT_ROWS * EXPERT_ROWS
    pad_end = jnp.cumsum(padded).astype(jnp.int32)
    pad_start = pad_end - padded
    n_used = pad_end[-1] // EXPERT_ROWS
    fill_start = pad_start + counts
    fill_len = pad_end - fill_start
    first_block = pad_start // EXPERT_ROWS
    n_expert_blocks = (padded // EXPERT_ROWS).astype(jnp.int32)
    return pad_start, fill_start, fill_len, first_block, n_expert_blocks, n_used.reshape(1).astype(jnp.int32)


def kernel(x_prompt, x_sample, cache_k, cache_v, w_in, sink, gm_ln_g, gm_ln_b, gm_w_s, gm_b_s, w_out, ln1_g, ln1_b,
           router_w, router_bias, w_gate_e, w_up_e, w_down_e, w_gate_s, w_up_s, w_down_s, ln2_g, ln2_b):
    bp, sp = x_prompt.shape[:2]
    bs, ts = x_sample.shape[:2]
    r = cache_k.shape[2]
    assert r == WINDOW and sp % ROW_TILE == 0 and (bs * ts) % ROW_TILE == 0
    n_p, n_s = bp * sp, bs * ts
    n_total = n_p + n_s
    l = 0

    w_in_b = w_in[l].astype(BF16)
    w_out_b = w_out[l].astype(BF16)
    router_t = router_w[l].T
    r_hi = router_t.astype(BF16)
    r_lo = (router_t - r_hi.astype(F32)).astype(BF16)
    row_vec = lambda v: v.reshape(1, -1)
    gm_g, gm_b = row_vec(gm_ln_g[l]), row_vec(gm_ln_b[l])
    sink_l = sink[l].astype(F32)
    shared = (w_gate_s[l].astype(BF16), w_up_s[l].astype(BF16), w_down_s[l].astype(BF16))

    xp2 = x_prompt.reshape(n_p, D_MODEL)
    tabs_p = _rope_tables(jnp.arange(sp, dtype=jnp.int32))
    q, k, v, gu, gv = _in_proj(xp2, w_in_b, tabs_p, gm_g, gm_b, tm=ROW_TILE, gv_dtype=BF16, name="in_proj_prompt")
    nb = sp // WINDOW
    att = _attention(sink_l, q, k, v, k, v, batch=bp, nb=nb, tq=WINDOW, seqs=1, chain=PROMPT_BLOCKS_PER_STEP,
                     prev_blocks=nb,
                     first_block_has_no_prev=True, name="attn_prompt")
    b_tab_p = jnp.repeat(gm_b_s[l].T, HEAD_DIM, axis=1)
    last_rows = lambda t: t.reshape(bp, sp, KV_WIDTH)[:, sp - r:, :].reshape(1, bp, r, KV_HEADS, HEAD_DIM)
    new_kp, new_vp = last_rows(k), last_rows(v)
    yp_p, pk_p, lt_p = _out_proj(att, gu, xp2, w_out_b, row_vec(ln1_g[l]), row_vec(ln1_b[l]), r_hi, r_lo, *shared,
                                 gate=(gv, gm_w_s[l], b_tab_p), name="out_proj_prompt")

    xs2 = x_sample.reshape(n_s, D_MODEL)
    pos_s = PAST_LEN + jnp.arange(ts, dtype=jnp.int32)
    tabs_s = tuple(jnp.tile(t, (bs, 1)) for t in _rope_tables(pos_s))
    q, k, v, gu, gv = _in_proj(xs2, w_in_b, tabs_s, gm_g, gm_b, tm=n_s, gv_dtype=F32, name="in_proj_sample")
    tq = SLABS
    pad_rows = lambda t: jnp.pad(t.reshape(bs, ts, -1), ((0, 0), (0, tq - ts), (0, 0))).reshape(bs * tq, -1)
    ck2 = cache_k[l].reshape(bs * r, KV_WIDTH)
    cv2 = cache_v[l].reshape(bs * r, KV_WIDTH)
    att = _attention(sink_l, pad_rows(q), pad_rows(k), pad_rows(v), ck2, cv2, batch=bs, nb=1, tq=tq,
                     seqs=SAMPLE_SEQS_PER_STEP, chain=1, prev_blocks=1, first_block_has_no_prev=False, name="attn_sample")
    att = att.reshape(bs, tq, ATT_WIDTH)[:, :ts].reshape(n_s, ATT_WIDTH)
    gm = _gate_short(gu, gv, gm_w_s[l], gm_b_s[l], length=ts, name="gate_sample")
    new_ks = jnp.concatenate([cache_k[l], k.reshape(bs, ts, KV_HEADS, HEAD_DIM)], axis=1)[:, ts:][None]
    new_vs = jnp.concatenate([cache_v[l], v.reshape(bs, ts, KV_HEADS, HEAD_DIM)], axis=1)[:, ts:][None]
    new_gs = gv.reshape(bs, ts, GM_WIDTH)[None]
    yp_s, pk_s, lt_s = _out_proj(att, gm, xs2, w_out_b, row_vec(ln1_g[l]), row_vec(ln1_b[l]), r_hi, r_lo, *shared,
                          name="out_proj_sample")

    eidx_t, w_t, rank_t, counts = _route(lt_p, lt_s, router_bias[l].astype(F32).reshape(N_EXPERTS, 1))
    a = n_total * TOP_K
    n_blocks = -(-(a + N_EXPERTS * (EXPERT_ROWS - 1)) // EXPERT_ROWS)
    pad_start, fill_start, fill_len, first_block, n_expert_blocks, n_used = _block_plan(counts.reshape(N_EXPERTS))
    dest_t = _dest(eidx_t, rank_t, pad_start.astype(F32).reshape(N_EXPERTS, 1))
    tiles = lambda a: a.reshape(-1, WORD_SLABS, LANES)
    x_sorted = _dispatch_sc(dest_t, tiles(pk_p), tiles(pk_s), rows=n_blocks * EXPERT_ROWS)
    x_sorted = _fill_padding(fill_start, fill_len, n_used, x_sorted)
    out_sorted = _experts(first_block, n_expert_blocks, n_used, x_sorted, w_gate_e[l], w_up_e[l], w_down_e[l])
    ln2 = (row_vec(ln2_g[l]), row_vec(ln2_b[l]))
    g = _gather_sc(dest_t, out_sorted).reshape(TOP_K, n_total * WORD_SLABS, LANES)
    w2 = w_t.T
    y_p = _combine(g, w2, yp_p, *ln2, n=n_p, row_offset=0, name="combine_prompt")
    y_s = _combine(g, w2, yp_s, *ln2, n=n_s, row_offset=n_p, name="combine_sample")
    return (y_p.reshape(bp, sp, D_MODEL), y_s.reshape(bs, ts, D_MODEL), new_kp, new_vp, new_ks, new_vs, new_gs)
```
